```python
import math
import jax, jax.numpy as jnp
from jax import lax
import numpy as np

D_MODEL = 1024
BATCH = 8
SEQ = 16384
DEPTH = 1

HEAD_DIM = 64
HEADS_PER_GROUP = 8
DILATED_GROUPS = ((128, 1), (512, 4), (2048, 16))
N_GROUPS = len(DILATED_GROUPS)
N_ATTN_HEADS = N_GROUPS * HEADS_PER_GROUP
ATTN_W = N_ATTN_HEADS * HEAD_DIM
ATTN_OUT_W = HEADS_PER_GROUP * HEAD_DIM
BLOCK = 128
REL_BUCKETS = 32
REL_MAX_DISTANCE = 2048
CONV_CHANNELS = D_MODEL
CONV_WIDTH = 31
FFN_HIDDEN = -(-8 * D_MODEL // (3 * 256)) * 256
IN_W = 3 * ATTN_W + 2 * CONV_CHANNELS + 2 * D_MODEL
RMS_EPS = 1e-6
LN_EPS = 1e-5
NEG_INF = -1e30

kernel_name = "hybrid_dilated_attn_conformer_conv_gated_block"


def rms_norm(x, g):
    xf = x.astype(jnp.float32)
    y = xf * lax.rsqrt(jnp.mean(xf * xf, axis=-1, keepdims=True) + RMS_EPS)
    return (y * g.astype(jnp.float32)).astype(x.dtype)


def layer_norm(x, g, b):
    xf = x.astype(jnp.float32)
    mu = jnp.mean(xf, axis=-1, keepdims=True)
    xc = xf - mu
    y = xc * lax.rsqrt(jnp.mean(xc * xc, axis=-1, keepdims=True) + LN_EPS)
    return (y * g.astype(jnp.float32) + b.astype(jnp.float32)).astype(x.dtype)


def rel_bucket(dist):
    max_exact = REL_BUCKETS // 2
    d = jnp.maximum(dist, 0)
    df = jnp.maximum(d, 1).astype(jnp.float32)
    large = max_exact + (jnp.log(df / max_exact) / math.log(REL_MAX_DISTANCE / max_exact)
                         * (REL_BUCKETS - max_exact)).astype(jnp.int32)
    large = jnp.minimum(large, REL_BUCKETS - 1)
    return jnp.where(d < max_exact, d, large)


def dilated_group_attention(q, k, v, bias_tab, window, dilation):
    B, S, H, Dh = q.shape
    span = window // dilation
    L = S // dilation
    nb = -(-L // BLOCK)
    Lp = nb * BLOCK
    n_prev = -(-span // BLOCK)
    kb_len = (n_prev + 1) * BLOCK

    def to_sub(t, front):
        t = t.reshape(B, L, dilation, H, Dh).transpose(0, 2, 1, 3, 4)
        return jnp.pad(t, ((0, 0), (0, 0), (front, Lp - L), (0, 0), (0, 0)))

    qb = to_sub(q, 0).reshape(B, dilation, nb, BLOCK, H, Dh)

    def band(t):
        tb = to_sub(t, n_prev * BLOCK).reshape(B, dilation, nb + n_prev, BLOCK, H, Dh)
        return jnp.concatenate([tb[:, :, j:j + nb] for j in range(n_prev + 1)], axis=3)

    kb, vb = band(k), band(v)
    a = jnp.arange(BLOCK, dtype=jnp.int32)[:, None]
    c = jnp.arange(kb_len, dtype=jnp.int32)[None, :]
    offset = a - c + n_prev * BLOCK
    bias = bias_tab[rel_bucket(offset * dilation)].astype(jnp.float32).transpose(2, 0, 1)
    kj = (jnp.arange(nb, dtype=jnp.int32)[:, None, None] - n_prev) * BLOCK + c[None]
    valid = (offset >= 0) & (offset <= span) & (kj >= 0)

    s = jnp.einsum('brnqhd,brnkhd->brnhqk', qb, kb).astype(jnp.float32) * (Dh ** -0.5) + bias
    s = jnp.where(valid[:, None], s, NEG_INF)
    lse = jax.nn.logsumexp(s, axis=-1)
    p = jnp.exp(s - lse[..., None])
    o = jnp.einsum('brnhqk,brnkhd->brnqhd', p, vb.astype(jnp.float32))
    o = o.reshape(B, dilation, Lp, H, Dh)[:, :, :L].transpose(0, 2, 1, 3, 4).reshape(B, S, H, Dh)
    lse = lse.transpose(0, 1, 2, 4, 3).reshape(B, dilation, Lp, H)[:, :, :L]
    lse = lse.transpose(0, 2, 1, 3).reshape(B, S, H)
    return o, lse


def dilated_attention_mixer(q, k, v, rel_bias_table, w_attn_out):
    B, S = q.shape[0], q.shape[1]
    outs, lses = [], []
    for g, (window, dilation) in enumerate(DILATED_GROUPS):
        tab = rel_bias_table[:, g * HEADS_PER_GROUP:(g + 1) * HEADS_PER_GROUP]
        o_g, lse_g = dilated_group_attention(q[:, :, g], k[:, :, g], v[:, :, g], tab, window, dilation)
        outs.append(o_g)
        lses.append(lse_g)
    alpha = jax.nn.softmax(jnp.stack(lses, axis=0), axis=0)
    o = jnp.einsum('gbsh,gbshd->bshd', alpha, jnp.stack(outs, axis=0))
    o = o.reshape(B, S, ATTN_OUT_W).astype(w_attn_out.dtype)
    return o @ w_attn_out


def conformer_conv_mixer(glu_in, b_glu, w_dw, b_dw, g_ln, b_ln, w_conv_out, b_conv_out):
    h = glu_in + b_glu
    u, gate = jnp.split(h, 2, axis=-1)
    u = u * jax.nn.sigmoid(gate)
    u = lax.conv_general_dilated(
        u, w_dw.reshape(CONV_WIDTH, 1, CONV_CHANNELS).astype(u.dtype),
        window_strides=(1,), padding=[(CONV_WIDTH - 1, 0)],
        dimension_numbers=('NWC', 'WIO', 'NWC'),
        feature_group_count=CONV_CHANNELS) + b_dw
    u = jax.nn.silu(layer_norm(u, g_ln, b_ln))
    return u @ w_conv_out + b_conv_out


def swiglu_ffn(h, w_ffn_in, w_ffn_out):
    gate, up = jnp.split(h @ w_ffn_in, 2, axis=-1)
    return (jax.nn.silu(gate) * up) @ w_ffn_out


def _fwd_setup_inputs(seed: int = 0) -> dict:
    key = jax.random.key(seed)
    ks = jax.random.split(key, 20)
    f32 = jnp.float32

    def nrm(k, shape, scale):
        return jax.random.normal(k, shape, f32) * scale

    def gain(k, shape):
        return 1.0 + 0.05 * jax.random.normal(k, shape, f32)

    return {
        "x": jax.random.normal(ks[0], (BATCH, SEQ, D_MODEL), f32),
        "rel_bias_table": nrm(ks[1], (REL_BUCKETS, N_ATTN_HEADS), 0.2),
        "g_pre_mix": gain(ks[2], (DEPTH, D_MODEL)),
        "w_in": nrm(ks[3], (DEPTH, D_MODEL, IN_W), D_MODEL ** -0.5),
        "b_glu": nrm(ks[4], (DEPTH, 2 * CONV_CHANNELS), 0.02),
        "w_dw": nrm(ks[5], (DEPTH, CONV_WIDTH, CONV_CHANNELS), CONV_WIDTH ** -0.5),
        "b_dw": nrm(ks[6], (DEPTH, CONV_CHANNELS), 0.02),
        "g_conv_ln": gain(ks[7], (DEPTH, CONV_CHANNELS)),
        "b_conv_ln": nrm(ks[8], (DEPTH, CONV_CHANNELS), 0.02),
        "w_conv_out": nrm(ks[9], (DEPTH, CONV_CHANNELS, D_MODEL), CONV_CHANNELS ** -0.5),
        "b_conv_out": nrm(ks[10], (DEPTH, D_MODEL), 0.02),
        "w_attn_out": nrm(ks[11], (DEPTH, ATTN_OUT_W, D_MODEL), ATTN_OUT_W ** -0.5),
        "w_mix_out": nrm(ks[12], (DEPTH, D_MODEL, D_MODEL), D_MODEL ** -0.5),
        "g_post_mix": gain(ks[13], (DEPTH, D_MODEL)),
        "g_pre_ffn": gain(ks[14], (DEPTH, D_MODEL)),
        "w_ffn_in": nrm(ks[15], (DEPTH, D_MODEL, 2 * FFN_HIDDEN), D_MODEL ** -0.5),
        "w_ffn_out": nrm(ks[16], (DEPTH, FFN_HIDDEN, D_MODEL), FFN_HIDDEN ** -0.5),
        "g_post_ffn": gain(ks[17], (DEPTH, D_MODEL)),
    }


def _fwd_reference(x, rel_bias_table, g_pre_mix, w_in, b_glu, w_dw, b_dw, g_conv_ln, b_conv_ln,
              w_conv_out, b_conv_out, w_attn_out, w_mix_out, g_post_mix, g_pre_ffn,
              w_ffn_in, w_ffn_out, g_post_ffn):
    B, S, D = x.shape
    for l in range(DEPTH):
        h = rms_norm(x, g_pre_mix[l])
        z = h @ w_in[l]
        q, k, v, glu_in, z_ga, z_gc = jnp.split(
            z, np.cumsum([ATTN_W, ATTN_W, ATTN_W, 2 * CONV_CHANNELS, D_MODEL]).tolist(), axis=-1)
        shp = (B, S, N_GROUPS, HEADS_PER_GROUP, HEAD_DIM)
        y_attn = dilated_attention_mixer(q.reshape(shp), k.reshape(shp), v.reshape(shp),
                                         rel_bias_table, w_attn_out[l])
        y_conv = conformer_conv_mixer(glu_in, b_glu[l], w_dw[l], b_dw[l], g_conv_ln[l],
                                      b_conv_ln[l], w_conv_out[l], b_conv_out[l])
        merged = jax.nn.sigmoid(z_ga) * y_attn + jax.nn.sigmoid(z_gc) * y_conv
        x = x + rms_norm(merged @ w_mix_out[l], g_post_mix[l])
        h = rms_norm(x, g_pre_ffn[l])
        x = x + rms_norm(swiglu_ffn(h, w_ffn_in[l], w_ffn_out[l]), g_post_ffn[l])
    return x


import jax as _jax
import jax.numpy as _jnp

TWIN_FORMAT = 'train_step'
FWD_PARAMS = ['x', 'rel_bias_table', 'g_pre_mix', 'w_in', 'b_glu', 'w_dw', 'b_dw', 'g_conv_ln', 'b_conv_ln', 'w_conv_out', 'b_conv_out', 'w_attn_out', 'w_mix_out', 'g_post_mix', 'g_pre_ffn', 'w_ffn_in', 'w_ffn_out', 'g_post_ffn']
TWIN_WEIGHTS = ['rel_bias_table', 'g_pre_mix', 'w_in', 'b_glu', 'w_dw', 'b_dw', 'g_conv_ln', 'b_conv_ln', 'w_conv_out', 'b_conv_out', 'w_attn_out', 'w_mix_out', 'g_post_mix', 'g_pre_ffn', 'w_ffn_in', 'w_ffn_out', 'g_post_ffn']
TWIN_DIFF_INPUT = 'x'
TWIN_INPUTS = ['x', 'rel_bias_table', 'g_pre_mix', 'w_in', 'b_glu', 'w_dw', 'b_dw', 'g_conv_ln', 'b_conv_ln', 'w_conv_out', 'b_conv_out', 'w_attn_out', 'w_mix_out', 'g_post_mix', 'g_pre_ffn', 'w_ffn_in', 'w_ffn_out', 'g_post_ffn', 'loss_target', 'm_rel_bias_table', 'm_g_pre_mix', 'm_w_in', 'm_b_glu', 'm_w_dw', 'm_b_dw', 'm_g_conv_ln', 'm_b_conv_ln', 'm_w_conv_out', 'm_b_conv_out', 'm_w_attn_out', 'm_w_mix_out', 'm_g_post_mix', 'm_g_pre_ffn', 'm_w_ffn_in', 'm_w_ffn_out', 'm_g_post_ffn', 'v_rel_bias_table', 'v_g_pre_mix', 'v_w_in', 'v_b_glu', 'v_w_dw', 'v_b_dw', 'v_g_conv_ln', 'v_b_conv_ln', 'v_w_conv_out', 'v_b_conv_out', 'v_w_attn_out', 'v_w_mix_out', 'v_g_post_mix', 'v_g_pre_ffn', 'v_w_ffn_in', 'v_w_ffn_out', 'v_g_post_ffn']
TWIN_OUTPUTS = ['loss', 'grad_x', 'grad_rel_bias_table', 'grad_g_pre_mix', 'grad_w_in', 'grad_b_glu', 'grad_w_dw', 'grad_b_dw', 'grad_g_conv_ln', 'grad_b_conv_ln', 'grad_w_conv_out', 'grad_b_conv_out', 'grad_w_attn_out', 'grad_w_mix_out', 'grad_g_post_mix', 'grad_g_pre_ffn', 'grad_w_ffn_in', 'grad_w_ffn_out', 'grad_g_post_ffn', 'delta_rel_bias_table', 'delta_g_pre_mix', 'delta_w_in', 'delta_b_glu', 'delta_w_dw', 'delta_b_dw', 'delta_g_conv_ln', 'delta_b_conv_ln', 'delta_w_conv_out', 'delta_b_conv_out', 'delta_w_attn_out', 'delta_w_mix_out', 'delta_g_post_mix', 'delta_g_pre_ffn', 'delta_w_ffn_in', 'delta_w_ffn_out', 'delta_g_post_ffn', 'new_m_rel_bias_table', 'new_m_g_pre_mix', 'new_m_w_in', 'new_m_b_glu', 'new_m_w_dw', 'new_m_b_dw', 'new_m_g_conv_ln', 'new_m_b_conv_ln', 'new_m_w_conv_out', 'new_m_b_conv_out', 'new_m_w_attn_out', 'new_m_w_mix_out', 'new_m_g_post_mix', 'new_m_g_pre_ffn', 'new_m_w_ffn_in', 'new_m_w_ffn_out', 'new_m_g_post_ffn', 'new_v_rel_bias_table', 'new_v_g_pre_mix', 'new_v_w_in', 'new_v_b_glu', 'new_v_w_dw', 'new_v_b_dw', 'new_v_g_conv_ln', 'new_v_b_conv_ln', 'new_v_w_conv_out', 'new_v_b_conv_out', 'new_v_w_attn_out', 'new_v_w_mix_out', 'new_v_g_post_mix', 'new_v_g_pre_ffn', 'new_v_w_ffn_in', 'new_v_w_ffn_out', 'new_v_g_post_ffn']
TWIN_LEAF_KINDS = {'loss': 'loss', 'grad_x': 'grad_x', 'grad_rel_bias_table': 'grad_w', 'grad_g_pre_mix': 'grad_w', 'grad_w_in': 'grad_w', 'grad_b_glu': 'grad_w', 'grad_w_dw': 'grad_w', 'grad_b_dw': 'grad_w', 'grad_g_conv_ln': 'grad_w', 'grad_b_conv_ln': 'grad_w', 'grad_w_conv_out': 'grad_w', 'grad_b_conv_out': 'grad_w', 'grad_w_attn_out': 'grad_w', 'grad_w_mix_out': 'grad_w', 'grad_g_post_mix': 'grad_w', 'grad_g_pre_ffn': 'grad_w', 'grad_w_ffn_in': 'grad_w', 'grad_w_ffn_out': 'grad_w', 'grad_g_post_ffn': 'grad_w', 'delta_rel_bias_table': 'delta_w', 'delta_g_pre_mix': 'delta_w', 'delta_w_in': 'delta_w', 'delta_b_glu': 'delta_w', 'delta_w_dw': 'delta_w', 'delta_b_dw': 'delta_w', 'delta_g_conv_ln': 'delta_w', 'delta_b_conv_ln': 'delta_w', 'delta_w_conv_out': 'delta_w', 'delta_b_conv_out': 'delta_w', 'delta_w_attn_out': 'delta_w', 'delta_w_mix_out': 'delta_w', 'delta_g_post_mix': 'delta_w', 'delta_g_pre_ffn': 'delta_w', 'delta_w_ffn_in': 'delta_w', 'delta_w_ffn_out': 'delta_w', 'delta_g_post_ffn': 'delta_w', 'new_m_rel_bias_table': 'new_m', 'new_m_g_pre_mix': 'new_m', 'new_m_w_in': 'new_m', 'new_m_b_glu': 'new_m', 'new_m_w_dw': 'new_m', 'new_m_b_dw': 'new_m', 'new_m_g_conv_ln': 'new_m', 'new_m_b_conv_ln': 'new_m', 'new_m_w_conv_out': 'new_m', 'new_m_b_conv_out': 'new_m', 'new_m_w_attn_out': 'new_m', 'new_m_w_mix_out': 'new_m', 'new_m_g_post_mix': 'new_m', 'new_m_g_pre_ffn': 'new_m', 'new_m_w_ffn_in': 'new_m', 'new_m_w_ffn_out': 'new_m', 'new_m_g_post_ffn': 'new_m', 'new_v_rel_bias_table': 'new_v', 'new_v_g_pre_mix': 'new_v', 'new_v_w_in': 'new_v', 'new_v_b_glu': 'new_v', 'new_v_w_dw': 'new_v', 'new_v_b_dw': 'new_v', 'new_v_g_conv_ln': 'new_v', 'new_v_b_conv_ln': 'new_v', 'new_v_w_conv_out': 'new_v', 'new_v_b_conv_out': 'new_v', 'new_v_w_attn_out': 'new_v', 'new_v_w_mix_out': 'new_v', 'new_v_g_post_mix': 'new_v', 'new_v_g_pre_ffn': 'new_v', 'new_v_w_ffn_in': 'new_v', 'new_v_w_ffn_out': 'new_v', 'new_v_g_post_ffn': 'new_v'}


def _forward(args):
    return _fwd_reference(*[args[k] for k in FWD_PARAMS])


def _output_shape():
    def fwd():
        inp = _fwd_setup_inputs(0)
        return _fwd_reference(*[inp[k] for k in FWD_PARAMS])
    out = _jax.eval_shape(fwd)
    return out.shape, out.dtype

N_MICROBATCH = 1
ADAM_LR = 0.001
ADAM_B1 = 0.9
ADAM_B2 = 0.999
ADAM_EPS = 1e-08
ADAM_WD = 0.01
ADAM_STEP = 10
PER_EXAMPLE_BATCH_AXIS = {'x': 0, 'loss_target': 0}
SHARED_INPUTS = []
_WEIGHT_DTYPES = {'rel_bias_table': _jnp.float32, 'g_pre_mix': _jnp.float32, 'w_in': _jnp.float32, 'b_glu': _jnp.float32, 'w_dw': _jnp.float32, 'b_dw': _jnp.float32, 'g_conv_ln': _jnp.float32, 'b_conv_ln': _jnp.float32, 'w_conv_out': _jnp.float32, 'b_conv_out': _jnp.float32, 'w_attn_out': _jnp.float32, 'w_mix_out': _jnp.float32, 'g_post_mix': _jnp.float32, 'g_pre_ffn': _jnp.float32, 'w_ffn_in': _jnp.float32, 'w_ffn_out': _jnp.float32, 'g_post_ffn': _jnp.float32}
MOMENT_SCALE = {'rel_bias_table': 2.487196e-01, 'g_pre_mix': 1.139507e+00, 'w_in': 3.732549e-01, 'b_glu': 9.486072e+00, 'w_dw': 1.294643e+00, 'b_dw': 2.305761e+01, 'g_conv_ln': 8.665832e+00, 'b_conv_ln': 1.349953e+01, 'w_conv_out': 4.471038e+00, 'b_conv_out': 2.791335e+01, 'w_attn_out': 2.688368e-01, 'w_mix_out': 4.543095e+00, 'g_post_mix': 1.299448e+02, 'g_pre_ffn': 4.082301e+00, 'w_ffn_in': 1.584141e+00, 'w_ffn_out': 3.509324e+00, 'g_post_ffn': 1.281795e+02}


def _to_microbatches(a, axis):
    t = _jnp.moveaxis(a, axis, 0)
    t = t.reshape((N_MICROBATCH, t.shape[0] // N_MICROBATCH) + t.shape[1:])
    return _jnp.moveaxis(t, 1, axis + 1)


def setup_inputs(seed: int = 0) -> dict:
    inp = _fwd_setup_inputs(seed)
    key = _jax.random.fold_in(_jax.random.key(seed), 7919)
    shape, _ = _output_shape()
    out = dict(inp)
    out["loss_target"] = _jax.random.normal(_jax.random.fold_in(key, 0), shape, _jnp.float32)
    for i, name in enumerate(TWIN_WEIGHTS):
        w = inp[name].astype(_jnp.float32)
        if MOMENT_SCALE is None:
            s = _jnp.sqrt(_jnp.mean(_jnp.square(w)) + 1e-30)
        else:
            s = MOMENT_SCALE[name]
        km, kv = _jax.random.split(_jax.random.fold_in(key, i + 1))
        out[name] = w
        out["m_" + name] = s * _jax.random.normal(km, w.shape, _jnp.float32)
        out["v_" + name] = (s * s) * _jax.random.uniform(kv, w.shape, _jnp.float32, 0.5, 1.5)
    if N_MICROBATCH > 1:
        for name, axis in PER_EXAMPLE_BATCH_AXIS.items():
            out[name] = _to_microbatches(out[name], axis)
    return {'x': out['x'], 'rel_bias_table': out['rel_bias_table'], 'g_pre_mix': out['g_pre_mix'], 'w_in': out['w_in'], 'b_glu': out['b_glu'], 'w_dw': out['w_dw'], 'b_dw': out['b_dw'], 'g_conv_ln': out['g_conv_ln'], 'b_conv_ln': out['b_conv_ln'], 'w_conv_out': out['w_conv_out'], 'b_conv_out': out['b_conv_out'], 'w_attn_out': out['w_attn_out'], 'w_mix_out': out['w_mix_out'], 'g_post_mix': out['g_post_mix'], 'g_pre_ffn': out['g_pre_ffn'], 'w_ffn_in': out['w_ffn_in'], 'w_ffn_out': out['w_ffn_out'], 'g_post_ffn': out['g_post_ffn'], 'loss_target': out['loss_target'], 'm_rel_bias_table': out['m_rel_bias_table'], 'm_g_pre_mix': out['m_g_pre_mix'], 'm_w_in': out['m_w_in'], 'm_b_glu': out['m_b_glu'], 'm_w_dw': out['m_w_dw'], 'm_b_dw': out['m_b_dw'], 'm_g_conv_ln': out['m_g_conv_ln'], 'm_b_conv_ln': out['m_b_conv_ln'], 'm_w_conv_out': out['m_w_conv_out'], 'm_b_conv_out': out['m_b_conv_out'], 'm_w_attn_out': out['m_w_attn_out'], 'm_w_mix_out': out['m_w_mix_out'], 'm_g_post_mix': out['m_g_post_mix'], 'm_g_pre_ffn': out['m_g_pre_ffn'], 'm_w_ffn_in': out['m_w_ffn_in'], 'm_w_ffn_out': out['m_w_ffn_out'], 'm_g_post_ffn': out['m_g_post_ffn'], 'v_rel_bias_table': out['v_rel_bias_table'], 'v_g_pre_mix': out['v_g_pre_mix'], 'v_w_in': out['v_w_in'], 'v_b_glu': out['v_b_glu'], 'v_w_dw': out['v_w_dw'], 'v_b_dw': out['v_b_dw'], 'v_g_conv_ln': out['v_g_conv_ln'], 'v_b_conv_ln': out['v_b_conv_ln'], 'v_w_conv_out': out['v_w_conv_out'], 'v_b_conv_out': out['v_b_conv_out'], 'v_w_attn_out': out['v_w_attn_out'], 'v_w_mix_out': out['v_w_mix_out'], 'v_g_post_mix': out['v_g_post_mix'], 'v_g_pre_ffn': out['v_g_pre_ffn'], 'v_w_ffn_in': out['v_w_ffn_in'], 'v_w_ffn_out': out['v_w_ffn_out'], 'v_g_post_ffn': out['v_g_post_ffn']}


def _loss(weights, diff, rest, loss_target):
    with _jax.named_scope("forward"):
        args = {**rest, TWIN_DIFF_INPUT: diff, **{k: w.astype(_WEIGHT_DTYPES[k]) for k, w in weights.items()}}
        y = _forward(args)
    with _jax.named_scope("loss_head"):
        err = _jnp.square(y.astype(_jnp.float32) - loss_target)
        return 0.5 * _jnp.sum(_jnp.mean(err, axis=-1)) if err.ndim else 0.5 * err


def _adamw(w, g, m, v):
    m = ADAM_B1 * m + (1.0 - ADAM_B1) * g
    v = ADAM_B2 * v + (1.0 - ADAM_B2) * _jnp.square(g)
    m_hat = m / (1.0 - ADAM_B1 ** ADAM_STEP)
    v_hat = v / (1.0 - ADAM_B2 ** ADAM_STEP)
    delta = -ADAM_LR * (m_hat / (_jnp.sqrt(v_hat) + ADAM_EPS) + ADAM_WD * w)
    return delta, m, v


def reference(x, rel_bias_table, g_pre_mix, w_in, b_glu, w_dw, b_dw, g_conv_ln, b_conv_ln, w_conv_out, b_conv_out, w_attn_out, w_mix_out, g_post_mix, g_pre_ffn, w_ffn_in, w_ffn_out, g_post_ffn, loss_target, m_rel_bias_table, m_g_pre_mix, m_w_in, m_b_glu, m_w_dw, m_b_dw, m_g_conv_ln, m_b_conv_ln, m_w_conv_out, m_b_conv_out, m_w_attn_out, m_w_mix_out, m_g_post_mix, m_g_pre_ffn, m_w_ffn_in, m_w_ffn_out, m_g_post_ffn, v_rel_bias_table, v_g_pre_mix, v_w_in, v_b_glu, v_w_dw, v_b_dw, v_g_conv_ln, v_b_conv_ln, v_w_conv_out, v_b_conv_out, v_w_attn_out, v_w_mix_out, v_g_post_mix, v_g_pre_ffn, v_w_ffn_in, v_w_ffn_out, v_g_post_ffn):
    given = dict(x=x, rel_bias_table=rel_bias_table, g_pre_mix=g_pre_mix, w_in=w_in, b_glu=b_glu, w_dw=w_dw, b_dw=b_dw, g_conv_ln=g_conv_ln, b_conv_ln=b_conv_ln, w_conv_out=w_conv_out, b_conv_out=b_conv_out, w_attn_out=w_attn_out, w_mix_out=w_mix_out, g_post_mix=g_post_mix, g_pre_ffn=g_pre_ffn, w_ffn_in=w_ffn_in, w_ffn_out=w_ffn_out, g_post_ffn=g_post_ffn, loss_target=loss_target, m_rel_bias_table=m_rel_bias_table, m_g_pre_mix=m_g_pre_mix, m_w_in=m_w_in, m_b_glu=m_b_glu, m_w_dw=m_w_dw, m_b_dw=m_b_dw, m_g_conv_ln=m_g_conv_ln, m_b_conv_ln=m_b_conv_ln, m_w_conv_out=m_w_conv_out, m_b_conv_out=m_b_conv_out, m_w_attn_out=m_w_attn_out, m_w_mix_out=m_w_mix_out, m_g_post_mix=m_g_post_mix, m_g_pre_ffn=m_g_pre_ffn, m_w_ffn_in=m_w_ffn_in, m_w_ffn_out=m_w_ffn_out, m_g_post_ffn=m_g_post_ffn, v_rel_bias_table=v_rel_bias_table, v_g_pre_mix=v_g_pre_mix, v_w_in=v_w_in, v_b_glu=v_b_glu, v_w_dw=v_w_dw, v_b_dw=v_b_dw, v_g_conv_ln=v_g_conv_ln, v_b_conv_ln=v_b_conv_ln, v_w_conv_out=v_w_conv_out, v_b_conv_out=v_b_conv_out, v_w_attn_out=v_w_attn_out, v_w_mix_out=v_w_mix_out, v_g_post_mix=v_g_post_mix, v_g_pre_ffn=v_g_pre_ffn, v_w_ffn_in=v_w_ffn_in, v_w_ffn_out=v_w_ffn_out, v_g_post_ffn=v_g_post_ffn)
    weights = {n: given[n] for n in TWIN_WEIGHTS}
    shared = {n: given[n] for n in SHARED_INPUTS}
    per_example = {n: given[n] for n in ['x']}
    grad_fn = _jax.value_and_grad(_loss, argnums=(0, 1))

    def one_microbatch(ex, loss_target):
        ex = dict(ex)
        diff = ex.pop(TWIN_DIFF_INPUT)
        return grad_fn(weights, diff, {**shared, **ex}, loss_target)

    if N_MICROBATCH == 1:
        loss, (grad_w, grad_x) = one_microbatch(per_example, given["loss_target"])
    else:
        def body(carry, xs):
            loss_sum, grad_sum = carry
            l_k, (gw_k, gx_k) = one_microbatch(xs[0], xs[1])
            with _jax.named_scope("update"):
                return (loss_sum + l_k, _jax.tree.map(_jnp.add, grad_sum, gw_k)), gx_k

        init = (_jnp.zeros((), _jnp.float32), _jax.tree.map(_jnp.zeros_like, weights))
        (loss, grad_w), grad_x = _jax.lax.scan(body, init, (per_example, given["loss_target"]))
    with _jax.named_scope("update"):
        delta_w, new_m, new_v = {}, {}, {}
        for n in TWIN_WEIGHTS:
            delta_w[n], new_m[n], new_v[n] = _adamw(weights[n], grad_w[n], given["m_" + n], given["v_" + n])
    return (loss, grad_x, *[grad_w[n] for n in TWIN_WEIGHTS], *[delta_w[n] for n in TWIN_WEIGHTS],
            *[new_m[n] for n in TWIN_WEIGHTS], *[new_v[n] for n in TWIN_WEIGHTS])
```

```python
import functools
import math

import numpy as np
import jax
import jax.numpy as jnp
from jax import lax
from jax.experimental import pallas as pl
from jax.experimental.pallas import tpu as pltpu

F32 = jnp.float32
BF = jnp.bfloat16

D = 1024
HEAD_DIM = 64
HEADS = 8
GROUPS = ((128, 1), (512, 4), (2048, 16))
QBLK = 128
GW = HEADS * HEAD_DIM
ATTN_W = 3 * GW
REL_BUCKETS = 32
REL_MAX_DISTANCE = 2048
CONV_W = 31
HALO = 32
FFN = 2816
IN_W = 3 * ATTN_W + 2 * D + 2 * D
RMS_EPS = 1e-6
LN_EPS = 1e-5
NEG_INF = -1e30
SCALE = HEAD_DIM ** -0.5
NDEV = 8

ADAM_LR = 0.001
ADAM_B1 = 0.9
ADAM_B2 = 0.999
ADAM_EPS = 1e-08
ADAM_WD = 0.01
ADAM_STEP = 10

Z_Q = 4096 // GW
Z_K = Z_Q + 3
Z_V = Z_Q + 6
Z_CB = IN_W // GW

VMEM_LIMIT = 52 * 1024 * 1024


def _cp(sem=None):
    if sem is None:
        return pltpu.CompilerParams(vmem_limit_bytes=VMEM_LIMIT)
    return pltpu.CompilerParams(vmem_limit_bytes=VMEM_LIMIT, dimension_semantics=sem)


def _sig(v):
    return jax.nn.sigmoid(v)


def _psum8(v):
    return v.reshape(v.shape[0] // 8, 8, v.shape[1]).sum(axis=0)


def _rms_r(v):
    return lax.rsqrt(jnp.mean(v * v, axis=-1, keepdims=True) + RMS_EPS)


def _rms_bwd(v, r, g, dy):
    gy = dy * g
    dv = r * gy - v * (r * r * r) * jnp.mean(v * gy, axis=-1, keepdims=True)
    return dv, dy * v * r


def _clip_k(k, k0, nk):
    return jnp.clip(k - k0, 0, nk - 1)


def _fused_mm(name, M, tm, grid_n, a_ops, b_ops, terms, acc_shapes, rows, consts, outs, parts, epilogue):
    gm = M // tm
    nk_total = max([t[3] + t[4] for t in terms], default=1)
    n_a, n_b, n_r, n_c, n_o, n_p = len(a_ops), len(b_ops), len(rows), len(consts), len(outs), len(parts)
    n_acc = len(acc_shapes)
    use_scratch = nk_total > 1
    if parts:
        assert grid_n == 1

    def jj(j, follow):
        return j if follow else 0

    in_specs, args = [], []
    for (arr, tk, k0, nk) in a_ops:
        in_specs.append(pl.BlockSpec((tm, tk), functools.partial(lambda i, j, k, k0, nk: (i, _clip_k(k, k0, nk)), k0=k0, nk=nk)))
        args.append(arr)
    for (arr, nt, tk, tn, k0, nk, koff, joff, fj) in b_ops:
        if nt:
            in_specs.append(pl.BlockSpec((tn, tk), functools.partial(
                lambda i, j, k, k0, nk, koff, joff, fj: (joff + jj(j, fj), _clip_k(k, k0, nk) + koff),
                k0=k0, nk=nk, koff=koff, joff=joff, fj=fj)))
        else:
            in_specs.append(pl.BlockSpec((tk, tn), functools.partial(
                lambda i, j, k, k0, nk, koff, joff, fj: (_clip_k(k, k0, nk) + koff, joff + jj(j, fj)),
                k0=k0, nk=nk, koff=koff, joff=joff, fj=fj)))
        args.append(arr)
    for (arr, w, off, fj) in rows:
        in_specs.append(pl.BlockSpec((tm, w), functools.partial(lambda i, j, k, off, fj: (i, off + jj(j, fj)), off=off, fj=fj)))
        args.append(arr)
    for arr in consts:
        in_specs.append(pl.BlockSpec(arr.shape, functools.partial(lambda i, j, k, nd: (0,) * nd, nd=arr.ndim)))
        args.append(arr)
    out_specs, out_shape = [], []
    for (ncols, dt, w, off, fj) in outs:
        out_specs.append(pl.BlockSpec((tm, w), functools.partial(lambda i, j, k, off, fj: (i, off + jj(j, fj)), off=off, fj=fj)))
        out_shape.append(jax.ShapeDtypeStruct((M, ncols), dt))
    for (r, c) in parts:
        out_specs.append(pl.BlockSpec((r, c), lambda i, j, k: (0, 0)))
        out_shape.append(jax.ShapeDtypeStruct((r, c), F32))
    scratch = [pltpu.VMEM(s, F32) for s in acc_shapes] if use_scratch else []

    def body(*refs):
        pos = 0
        a_refs = refs[pos:pos + n_a]; pos += n_a
        b_refs = refs[pos:pos + n_b]; pos += n_b
        r_refs = refs[pos:pos + n_r]; pos += n_r
        c_refs = refs[pos:pos + n_c]; pos += n_c
        o_refs = refs[pos:pos + n_o]; pos += n_o
        p_refs = refs[pos:pos + n_p]; pos += n_p
        acc_refs = refs[pos:pos + n_acc] if use_scratch else ()
        i = pl.program_id(0)
        k = pl.program_id(2)

        def dot_of(ai, bi):
            a = a_refs[ai][...].astype(BF)
            b = b_refs[bi][...].astype(BF)
            if b_ops[bi][1]:
                return lax.dot_general(a, b, (((1,), (1,)), ((), ())), preferred_element_type=F32)
            return jnp.dot(a, b, preferred_element_type=F32)

        if parts:
            @pl.when((i == 0) & (k == 0))
            def _():
                for p in p_refs:
                    p[...] = jnp.zeros(p.shape, F32)

        def finish(accs):
            epilogue(accs, r_refs, c_refs, o_refs, p_refs)
            if parts:
                @pl.when(i == gm - 1)
                def _():
                    for p in p_refs:
                        p[0:1, :] = jnp.sum(p[...], axis=0, keepdims=True)

        if not use_scratch:
            accs = [None] * n_acc
            for (ai, bi, ci, k0, nk) in terms:
                d = dot_of(ai, bi)
                accs[ci] = d if accs[ci] is None else accs[ci] + d
            finish(accs)
        else:
            @pl.when(k == 0)
            def _():
                for acc in acc_refs:
                    acc[...] = jnp.zeros(acc.shape, F32)

            for (ai, bi, ci, k0, nk) in terms:
                def do(ai=ai, bi=bi, ci=ci):
                    acc_refs[ci][...] += dot_of(ai, bi)
                if k0 == 0 and nk == nk_total:
                    do()
                else:
                    pl.when((k >= k0) & (k < k0 + nk))(do)

            @pl.when(k == nk_total - 1)
            def _():
                finish([acc[...] for acc in acc_refs])

    res = pl.pallas_call(
        body, grid=(gm, grid_n, nk_total), in_specs=in_specs, out_specs=out_specs, out_shape=out_shape,
        scratch_shapes=scratch, compiler_params=_cp(("arbitrary", "arbitrary", "arbitrary")), name=name,
    )(*args)
    return res


def _mm_tn(name, a, b, tm, tn, tk):
    S, Ka = a.shape
    Nb = b.shape[1]
    nk = S // tk

    def body(a_ref, b_ref, o_ref, acc):
        k = pl.program_id(2)

        @pl.when(k == 0)
        def _():
            acc[...] = jnp.zeros(acc.shape, F32)

        acc[...] += lax.dot_general(a_ref[...], b_ref[...], (((0,), (0,)), ((), ())), preferred_element_type=F32)

        @pl.when(k == nk - 1)
        def _():
            o_ref[...] = acc[...].astype(o_ref.dtype)

    return pl.pallas_call(
        body, grid=(Ka // tm, Nb // tn, nk),
        in_specs=[pl.BlockSpec((tk, tm), lambda i, j, k: (k, i)), pl.BlockSpec((tk, tn), lambda i, j, k: (k, j))],
        out_specs=pl.BlockSpec((tm, tn), lambda i, j, k: (i, j)),
        out_shape=jax.ShapeDtypeStruct((Ka, Nb), BF),
        scratch_shapes=[pltpu.VMEM((tm, tn), F32)],
        compiler_params=_cp(("parallel", "parallel", "arbitrary")), name=name,
    )(a, b)


def _rel_bucket_np(dist):
    max_exact = REL_BUCKETS // 2
    d = np.maximum(dist, 0)
    df = np.maximum(d, 1).astype(np.float32)
    large = max_exact + (np.log(df / np.float32(max_exact)) / np.float32(math.log(REL_MAX_DISTANCE / max_exact))
                         * np.float32(REL_BUCKETS - max_exact)).astype(np.int32)
    large = np.minimum(large, REL_BUCKETS - 1)
    return np.where(d < max_exact, d, large).astype(np.int32)


def _band_index():
    idx = np.zeros((6, 1, QBLK * 2 * QBLK), np.int32)
    for g, (window, dil) in enumerate(GROUPS):
        span = window // dil
        a = np.arange(QBLK)[:, None]; c = np.arange(2 * QBLK)[None, :]
        off = a - c + QBLK
        idx[g, 0] = np.where((off >= 0) & (off <= span), _rel_bucket_np(off * dil), -1).reshape(-1)
        a = np.arange(2 * QBLK)[:, None]; c = np.arange(QBLK)[None, :]
        off = a - c
        idx[3 + g, 0] = np.where((off >= 0) & (off <= span), _rel_bucket_np(off * dil), -1).reshape(-1)
    return idx


_NB = QBLK * 2 * QBLK
_BCH = 4096


def _bias_build(tab_t, idx):
    def body(t_ref, i_ref, o_ref):
        ix = i_ref[0]
        t = t_ref[0]
        acc = jnp.full((HEADS, _BCH), NEG_INF, F32)
        for b in range(REL_BUCKETS):
            acc = jnp.where(ix == b, t[:, b:b + 1], acc)
        o_ref[0] = acc

    return pl.pallas_call(
        body, grid=(6, _NB // _BCH),
        in_specs=[pl.BlockSpec((1, HEADS, REL_BUCKETS), lambda l, n: (l % 3, 0, 0)),
                  pl.BlockSpec((1, 1, _BCH), lambda l, n: (l, 0, n))],
        out_specs=pl.BlockSpec((1, HEADS, _BCH), lambda l, n: (l, 0, n)),
        out_shape=jax.ShapeDtypeStruct((6, HEADS, _NB), F32), compiler_params=_cp(), name="bias_build",
    )(tab_t, idx)


def _bias_grad(ds, idx):
    nch = _NB // _BCH

    def body(d_ref, i_ref, o_ref):
        n = pl.program_id(1)

        @pl.when(n == 0)
        def _():
            o_ref[...] = jnp.zeros(o_ref.shape, F32)

        ix = i_ref[0]
        d = d_ref[0]
        lane = lax.broadcasted_iota(jnp.int32, (HEADS, 128), 1)
        acc = jnp.zeros((HEADS, 128), F32)
        for b in range(REL_BUCKETS):
            s = jnp.sum(jnp.where(ix == b, d, 0.0), axis=1, keepdims=True)
            acc = acc + jnp.where(lane == b, s, 0.0)
        o_ref[0] += acc

    return pl.pallas_call(
        body, grid=(3, nch),
        in_specs=[pl.BlockSpec((1, HEADS, _BCH), lambda l, n: (l, 0, n)),
                  pl.BlockSpec((1, 1, _BCH), lambda l, n: (l, 0, n))],
        out_specs=pl.BlockSpec((1, HEADS, 128), lambda l, n: (l, 0, 0)),
        out_shape=jax.ShapeDtypeStruct((3, HEADS, 128), F32), compiler_params=_cp(), name="bias_grad",
    )(ds, idx)


def _attn_dims(S, dil):
    L = S // dil
    TQ = min(512, L)
    return L, TQ, L // TQ, TQ // QBLK


def _kv_window(cur_ref, prev_ref, j, hs):
    if j == 0:
        return jnp.concatenate([prev_ref[:, hs], cur_ref[0:QBLK, hs]], axis=0)
    return cur_ref[(j - 1) * QBLK:(j + 1) * QBLK, hs]


def _q_window(cur_ref, nxt_ref, j, nsub, hs):
    if j == nsub - 1:
        return jnp.concatenate([cur_ref[j * QBLK:(j + 1) * QBLK, hs], nxt_ref[:, hs]], axis=0)
    return cur_ref[j * QBLK:(j + 2) * QBLK, hs]


def _attn_fwd(z, bias, g, dil):
    S = z.shape[0]
    L, TQ, nq, nsub = _attn_dims(S, dil)
    zv = z.reshape(L, dil * IN_W)

    def body(q_ref, kc_ref, kp_ref, vc_ref, vp_ref, b_ref, o_ref, l_ref):
        i = pl.program_id(1)
        col = lax.broadcasted_iota(jnp.int32, (QBLK, 2 * QBLK), 1)
        first = (col >= QBLK) | (i > 0)
        for h in range(HEADS):
            hs = slice(h * HEAD_DIM, (h + 1) * HEAD_DIM)
            bias_h = b_ref[h]
            for j in range(nsub):
                rs = slice(j * QBLK, (j + 1) * QBLK)
                kk = _kv_window(kc_ref, kp_ref, j, hs)
                vv = _kv_window(vc_ref, vp_ref, j, hs)
                s = lax.dot_general(q_ref[rs, hs], kk, (((1,), (1,)), ((), ())), preferred_element_type=F32) * SCALE + bias_h
                if j == 0:
                    s = jnp.where(first, s, NEG_INF)
                m = jnp.max(s, axis=-1, keepdims=True)
                p = jnp.exp(s - m)
                den = jnp.sum(p, axis=-1, keepdims=True)
                o = jnp.dot(p.astype(BF), vv, preferred_element_type=F32) / den
                o_ref[rs, hs] = o
                l_ref[rs, hs] = jnp.broadcast_to(m + jnp.log(den), (QBLK, HEAD_DIM))

    cur = lambda cb: pl.BlockSpec((TQ, GW), lambda c, i: (i, c * Z_CB + cb))
    prev = lambda cb: pl.BlockSpec((QBLK, GW), lambda c, i: (jnp.maximum(i * nsub - 1, 0), c * Z_CB + cb))
    o, l = pl.pallas_call(
        body, grid=(dil, nq),
        in_specs=[cur(Z_Q + g), cur(Z_K + g), prev(Z_K + g), cur(Z_V + g), prev(Z_V + g),
                  pl.BlockSpec((HEADS, QBLK, 2 * QBLK), lambda c, i: (0, 0, 0))],
        out_specs=[pl.BlockSpec((TQ, GW), lambda c, i: (i, c))] * 2,
        out_shape=[jax.ShapeDtypeStruct((L, dil * GW), F32)] * 2,
        compiler_params=_cp(), name=f"attn_fwd_g{g}",
    )(zv, zv, zv, zv, zv, bias)
    return o.reshape(S, GW), l.reshape(S, GW)


def _attn_merge(os_, ls_, S):
    def epi(accs, r, c, o, p):
        l0, l1, l2 = r[3][...], r[4][...], r[5][...]
        m = jnp.maximum(jnp.maximum(l0, l1), l2)
        e0, e1, e2 = jnp.exp(l0 - m), jnp.exp(l1 - m), jnp.exp(l2 - m)
        den = e0 + e1 + e2
        o[0][...] = ((e0 * r[0][...] + e1 * r[1][...] + e2 * r[2][...]) / den).astype(BF)
        o[1][...] = m + jnp.log(den)

    rows = [(a, GW, 0, False) for a in (*os_, *ls_)]
    return _fused_mm("attn_merge", S, 1024, 1, [], [], [], [], rows, [], [(GW, BF, GW, 0, False), (GW, F32, GW, 0, False)], [], epi)


def _attn_bwd_dq(z, bias, do, lse, delta, dq_all, g, dil):
    S = z.shape[0]
    L, TQ, nq, nsub = _attn_dims(S, dil)
    zv = z.reshape(L, dil * IN_W)
    dov, lv, dv_ = (t.reshape(L, dil * GW) for t in (do, lse, delta))
    dq_v = dq_all.reshape(L, dil * ATTN_W)

    def body(q_ref, kc_ref, kp_ref, vc_ref, vp_ref, b_ref, do_ref, l_ref, dl_ref, _alias, dq_ref, db_ref):
        c = pl.program_id(0)
        i = pl.program_id(1)

        @pl.when((c == 0) & (i == 0))
        def _():
            db_ref[...] = jnp.zeros(db_ref.shape, F32)

        col = lax.broadcasted_iota(jnp.int32, (QBLK, 2 * QBLK), 1)
        first = (col >= QBLK) | (i > 0)
        for h in range(HEADS):
            hs = slice(h * HEAD_DIM, (h + 1) * HEAD_DIM)
            bias_h = b_ref[h]
            db = jnp.zeros((QBLK, 2 * QBLK), F32)
            for j in range(nsub):
                rs = slice(j * QBLK, (j + 1) * QBLK)
                kk = _kv_window(kc_ref, kp_ref, j, hs)
                vv = _kv_window(vc_ref, vp_ref, j, hs)
                s = lax.dot_general(q_ref[rs, hs], kk, (((1,), (1,)), ((), ())), preferred_element_type=F32) * SCALE + bias_h
                if j == 0:
                    s = jnp.where(first, s, NEG_INF)
                p = jnp.exp(s - l_ref[rs, h * HEAD_DIM:h * HEAD_DIM + 1])
                dp = lax.dot_general(do_ref[rs, hs], vv, (((1,), (1,)), ((), ())), preferred_element_type=F32)
                ds = p * (dp - dl_ref[rs, h * HEAD_DIM:h * HEAD_DIM + 1])
                db = db + ds
                dq_ref[rs, hs] = (jnp.dot(ds.astype(BF), kk, preferred_element_type=F32) * SCALE).astype(BF)
            db_ref[h] += db

    cur = lambda cb: pl.BlockSpec((TQ, GW), lambda c, i: (i, c * Z_CB + cb))
    prev = lambda cb: pl.BlockSpec((QBLK, GW), lambda c, i: (jnp.maximum(i * nsub - 1, 0), c * Z_CB + cb))
    row = pl.BlockSpec((TQ, GW), lambda c, i: (i, c))
    dq, db = pl.pallas_call(
        body, grid=(dil, nq),
        in_specs=[cur(Z_Q + g), cur(Z_K + g), prev(Z_K + g), cur(Z_V + g), prev(Z_V + g),
                  pl.BlockSpec((HEADS, QBLK, 2 * QBLK), lambda c, i: (0, 0, 0)), row, row, row,
                  pl.BlockSpec(memory_space=pl.ANY)],
        out_specs=[pl.BlockSpec((TQ, GW), lambda c, i: (i, c * 3 + g)),
                   pl.BlockSpec((HEADS, QBLK, 2 * QBLK), lambda c, i: (0, 0, 0))],
        out_shape=[jax.ShapeDtypeStruct(dq_v.shape, BF), jax.ShapeDtypeStruct((HEADS, QBLK, 2 * QBLK), F32)],
        input_output_aliases={9: 0},
        compiler_params=_cp(("arbitrary", "arbitrary")), name=f"attn_bwd_dq_g{g}",
    )(zv, zv, zv, zv, zv, bias, dov, lv, dv_, dq_v)
    return dq.reshape(S, ATTN_W), db


def _attn_bwd_dkv(z, bias2, do, lse, delta, dk_all, dv_all, g, dil):
    S = z.shape[0]
    L, TQ, nq, nsub = _attn_dims(S, dil)
    zv = z.reshape(L, dil * IN_W)
    dov, lv, dv_ = (t.reshape(L, dil * GW) for t in (do, lse, delta))
    dk_v = dk_all.reshape(L, dil * ATTN_W)
    dvv = dv_all.reshape(L, dil * ATTN_W)

    def body(k_ref, v_ref, qc_ref, qn_ref, b_ref, doc_ref, don_ref, lc_ref, ln_ref, dc_ref, dn_ref, _a0, _a1, dk_ref, dv_ref):
        i = pl.program_id(1)
        row = lax.broadcasted_iota(jnp.int32, (2 * QBLK, QBLK), 0)
        last = (row < QBLK) | (i < nq - 1)
        for h in range(HEADS):
            hs = slice(h * HEAD_DIM, (h + 1) * HEAD_DIM)
            h1 = slice(h * HEAD_DIM, h * HEAD_DIM + 1)
            bias_h = b_ref[h]
            for j in range(nsub):
                rs = slice(j * QBLK, (j + 1) * QBLK)
                qq = _q_window(qc_ref, qn_ref, j, nsub, hs)
                dd = _q_window(doc_ref, don_ref, j, nsub, hs)
                lq = _q_window(lc_ref, ln_ref, j, nsub, h1)
                dl = _q_window(dc_ref, dn_ref, j, nsub, h1)
                kk = k_ref[rs, hs]
                s = lax.dot_general(qq, kk, (((1,), (1,)), ((), ())), preferred_element_type=F32) * SCALE + bias_h
                if j == nsub - 1:
                    s = jnp.where(last, s, NEG_INF)
                p = jnp.exp(s - lq)
                dp = lax.dot_general(dd, v_ref[rs, hs], (((1,), (1,)), ((), ())), preferred_element_type=F32)
                ds = p * (dp - dl)
                dv_ref[rs, hs] = lax.dot_general(p.astype(BF), dd, (((0,), (0,)), ((), ())), preferred_element_type=F32).astype(BF)
                dk_ref[rs, hs] = (lax.dot_general(ds.astype(BF), qq, (((0,), (0,)), ((), ())), preferred_element_type=F32) * SCALE).astype(BF)

    cur = lambda cb: pl.BlockSpec((TQ, GW), lambda c, i: (i, c * Z_CB + cb))
    nxt = lambda cb: pl.BlockSpec((QBLK, GW), lambda c, i: (jnp.minimum((i + 1) * nsub, L // QBLK - 1), c * Z_CB + cb))
    row = pl.BlockSpec((TQ, GW), lambda c, i: (i, c))
    rown = pl.BlockSpec((QBLK, GW), lambda c, i: (jnp.minimum((i + 1) * nsub, L // QBLK - 1), c))
    dk, dv = pl.pallas_call(
        body, grid=(dil, nq),
        in_specs=[cur(Z_K + g), cur(Z_V + g), cur(Z_Q + g), nxt(Z_Q + g),
                  pl.BlockSpec((HEADS, 2 * QBLK, QBLK), lambda c, i: (0, 0, 0)),
                  row, rown, row, rown, row, rown, pl.BlockSpec(memory_space=pl.ANY), pl.BlockSpec(memory_space=pl.ANY)],
        out_specs=[pl.BlockSpec((TQ, GW), lambda c, i: (i, c * 3 + g))] * 2,
        out_shape=[jax.ShapeDtypeStruct(dk_v.shape, BF)] * 2,
        input_output_aliases={11: 0, 12: 1},
        compiler_params=_cp(), name=f"attn_bwd_dkv_g{g}",
    )(zv, zv, zv, zv, bias2, dov, dov, lv, lv, dv_, dv_, dk_v, dvv)
    return dk.reshape(S, ATTN_W), dv.reshape(S, ATTN_W)


CT = 256


def _ln_hat(u1):
    mu = jnp.mean(u1, axis=-1, keepdims=True)
    xc = u1 - mu
    rstd = lax.rsqrt(jnp.mean(xc * xc, axis=-1, keepdims=True) + LN_EPS)
    return xc * rstd, rstd


def _glu_window(hu_ref, hg_ref, huh_ref, hgh_ref, bglu_ref, buf_ref, i):
    bu = bglu_ref[:, 0:D]
    bg = bglu_ref[:, D:2 * D]
    uh = (huh_ref[...].astype(F32) + bu) * _sig(hgh_ref[...].astype(F32) + bg)
    buf_ref[0:HALO, :] = jnp.where(i > 0, uh, 0.0)
    a = hu_ref[...].astype(F32) + bu
    s = _sig(hg_ref[...].astype(F32) + bg)
    buf_ref[HALO:HALO + CT, :] = a * s
    return a, s


def _conv_specs(S):
    cur = lambda cb: pl.BlockSpec((CT, D), lambda i: (i, cb))
    halo = lambda cb: pl.BlockSpec((HALO, D), lambda i: (jnp.maximum(i * (CT // HALO) - 1, 0), cb))
    full = lambda shp: pl.BlockSpec(shp, lambda i: (0, 0))
    return cur, halo, full


def _conv_fwd(z, b_glu, w_dw, b_dw, g_ln, b_ln):
    S = z.shape[0]
    cur, halo, full = _conv_specs(S)

    def body(hu, hg, huh, hgh, bglu, w, bdw, gln, bln, u1_ref, u3_ref, buf):
        i = pl.program_id(0)
        _glu_window(hu, hg, huh, hgh, bglu, buf, i)
        for rc in range(CT // 8):
            acc = jnp.broadcast_to(bdw[...], (8, D))
            for j in range(CONV_W):
                acc = acc + w[j:j + 1, :] * buf[rc * 8 + 2 + j:rc * 8 + 10 + j, :]
            u1_ref[rc * 8:(rc + 1) * 8, :] = acc
        xh, _ = _ln_hat(u1_ref[...])
        u2 = xh * gln[...] + bln[...]
        u3_ref[...] = (u2 * _sig(u2)).astype(BF)

    return pl.pallas_call(
        body, grid=(S // CT,),
        in_specs=[cur(0), cur(1), halo(0), halo(1), full((1, 2 * D)), full((HALO, D)), full((1, D)), full((1, D)), full((1, D))],
        out_specs=[pl.BlockSpec((CT, D), lambda i: (i, 0))] * 2,
        out_shape=[jax.ShapeDtypeStruct((S, D), F32), jax.ShapeDtypeStruct((S, D), BF)],
        scratch_shapes=[pltpu.VMEM((HALO + CT, D), F32)], compiler_params=_cp(), name="conv_fwd",
    )(z, z, z, z, b_glu, w_dw, b_dw, g_ln, b_ln)


def _conv_bwd(du1, z, b_glu, w_dw):
    S = z.shape[0]
    n = S // CT
    cur, halo, full = _conv_specs(S)

    def body(du, dun, hu, hg, huh, hgh, bglu, w, dz_ref, dw_ref, dbg_ref, bufu, bufd, du0_ref, dwacc):
        i = pl.program_id(0)

        @pl.when(i == 0)
        def _():
            dwacc[...] = jnp.zeros(dwacc.shape, F32)
            dbg_ref[...] = jnp.zeros(dbg_ref.shape, F32)

        a, s = _glu_window(hu, hg, huh, hgh, bglu, bufu, i)
        bufd[0:CT, :] = du[...]
        bufd[CT:CT + HALO, :] = jnp.where(i < n - 1, dun[...], 0.0)
        for rc in range(CT // 8):
            acc = jnp.zeros((8, D), F32)
            for j in range(CONV_W):
                acc = acc + w[j:j + 1, :] * bufd[rc * 8 + 30 - j:rc * 8 + 38 - j, :]
            du0_ref[rc * 8:(rc + 1) * 8, :] = acc
        for j in range(CONV_W):
            acc = jnp.zeros((8, D), F32)
            for rc in range(CT // 8):
                acc = acc + bufd[rc * 8:rc * 8 + 8, :] * bufu[rc * 8 + 2 + j:rc * 8 + 10 + j, :]
            dwacc[j * 8:(j + 1) * 8, :] += acc
        du0 = du0_ref[...]
        dhu = du0 * s
        dhg = du0 * a * s * (1.0 - s)
        dz_ref[:, 0:D] = dhu.astype(BF)
        dz_ref[:, D:2 * D] = dhg.astype(BF)
        dbg_ref[:, 0:D] += _psum8(dhu)
        dbg_ref[:, D:2 * D] += _psum8(dhg)

        @pl.when(i == n - 1)
        def _():
            dbg_ref[0:1, :] = jnp.sum(dbg_ref[...], axis=0, keepdims=True)
            for j in range(CONV_W):
                dw_ref[j:j + 1, :] = jnp.sum(dwacc[j * 8:(j + 1) * 8, :], axis=0, keepdims=True)
            dw_ref[CONV_W:HALO, :] = jnp.zeros((HALO - CONV_W, D), F32)

    nxt = pl.BlockSpec((HALO, D), lambda i: (jnp.minimum((i + 1) * (CT // HALO), S // HALO - 1), 0))
    return pl.pallas_call(
        body, grid=(n,),
        in_specs=[pl.BlockSpec((CT, D), lambda i: (i, 0)), nxt, cur(0), cur(1), halo(0), halo(1), full((1, 2 * D)), full((HALO, D))],
        out_specs=[pl.BlockSpec((CT, 2 * D), lambda i: (i, 0)), full((HALO, D)), full((8, 2 * D))],
        out_shape=[jax.ShapeDtypeStruct((S, 2 * D), BF), jax.ShapeDtypeStruct((HALO, D), F32), jax.ShapeDtypeStruct((8, 2 * D), F32)],
        scratch_shapes=[pltpu.VMEM((HALO + CT, D), F32), pltpu.VMEM((CT + HALO, D), F32), pltpu.VMEM((CT, D), F32),
                        pltpu.VMEM((CONV_W * 8, D), F32)],
        compiler_params=_cp(("arbitrary",)), name="conv_bwd",
    )(du1, du1, z, z, z, z, b_glu, w_dw)


MESH = pl.DeviceIdType.MESH


def _all_gather(name, shards):
    n = len(shards)

    def body(*refs):
        ins, outs = refs[:n], refs[n:2 * n]
        send_sems, recv_sems, local_sems = refs[2 * n:]
        x, y, c = lax.axis_index("x"), lax.axis_index("y"), lax.axis_index("c")
        me, sibling = (x, y, c), (x, y, 1 - c)
        chips = [(1 - x, y), (x, 1 - y), (1 - x, 1 - y)]

        def slot(a, px, py, pc):
            return outs[a].at[4 * px + 2 * py + pc]

        def copy(a, k, block, to, src=None):
            return pltpu.make_async_remote_copy(
                src_ref=slot(a, *block) if src is None else src, dst_ref=slot(a, *block),
                send_sem=send_sems.at[a, k], recv_sem=recv_sems.at[a, k], device_id=to, device_id_type=MESH)

        mine = [pltpu.make_async_copy(ins[a], slot(a, *me), local_sems.at[a]) for a in range(n)]
        for cp in mine:
            cp.start()
        first = []
        for a in range(n):
            first.append(copy(a, 0, me, sibling, src=ins[a]))
            first += [copy(a, 1 + j, me, (*chip, c), src=ins[a]) for j, chip in enumerate(chips)]
        for cp in first:
            cp.start()
        passed = []
        for j, chip in enumerate(chips):
            for a in range(n):
                copy(a, 1 + j, (*chip, c), me).wait_recv()
                fwd = copy(a, 4 + j, (*chip, c), sibling)
                fwd.start()
                passed.append(fwd)
        for a in range(n):
            copy(a, 0, sibling, me).wait_recv()
        for j, chip in enumerate(chips):
            for a in range(n):
                copy(a, 4 + j, (*chip, 1 - c), me).wait_recv()
        for cp in first + passed:
            cp.wait_send()
        for cp in mine:
            cp.wait()

    anyspec = pl.BlockSpec(memory_space=pl.ANY)
    return pl.pallas_call(
        body, in_specs=[anyspec] * n, out_specs=[anyspec] * n,
        out_shape=[jax.ShapeDtypeStruct((NDEV,) + s.shape, s.dtype) for s in shards],
        scratch_shapes=[pltpu.SemaphoreType.DMA((n, 7)), pltpu.SemaphoreType.DMA((n, 7)), pltpu.SemaphoreType.DMA((n,))],
        name=name,
    )(*shards)


def _scatter_blocks(name, blocks):
    n = len(blocks)

    def body(*refs):
        ins, outs = refs[:n], refs[n:2 * n]
        send_sems, recv_sems, local_sems = refs[2 * n:]
        x, y, c = lax.axis_index("x"), lax.axis_index("y"), lax.axis_index("c")
        my = 4 * x + 2 * y + c
        mine = [pltpu.make_async_copy(ins[a].at[my], outs[a].at[my], local_sems.at[a]) for a in range(n)]
        for cp in mine:
            cp.start()
        copies = []
        for k in range(1, NDEV):
            px = 1 - x if k & 4 else x
            py = 1 - y if k & 2 else y
            pc = 1 - c if k & 1 else c
            pid = 4 * px + 2 * py + pc
            for a in range(n):
                cp = pltpu.make_async_remote_copy(
                    src_ref=ins[a].at[pid], dst_ref=outs[a].at[my], send_sem=send_sems.at[a, k - 1],
                    recv_sem=recv_sems.at[a, k - 1], device_id=(px, py, pc), device_id_type=MESH)
                cp.start()
                arrival = pltpu.make_async_remote_copy(
                    src_ref=ins[a].at[pid], dst_ref=outs[a].at[pid], send_sem=send_sems.at[a, k - 1],
                    recv_sem=recv_sems.at[a, k - 1], device_id=(px, py, pc), device_id_type=MESH)
                copies.append((cp, arrival))
        for cp, arrival in copies:
            arrival.wait_recv()
        for cp, arrival in copies:
            cp.wait_send()
        for cp in mine:
            cp.wait()

    anyspec = pl.BlockSpec(memory_space=pl.ANY)
    return pl.pallas_call(
        body, in_specs=[anyspec] * n, out_specs=[anyspec] * n,
        out_shape=[jax.ShapeDtypeStruct(b.shape, b.dtype) for b in blocks],
        scratch_shapes=[pltpu.SemaphoreType.DMA((n, 7)), pltpu.SemaphoreType.DMA((n, 7)), pltpu.SemaphoreType.DMA((n,))],
        name=name,
    )(*blocks)


_C1 = 1.0 - ADAM_B1 ** ADAM_STEP
_C2 = 1.0 - ADAM_B2 ** ADAM_STEP


def _adamw(name, w, m, v, recv, tr):
    R, C = w.shape

    def body(w_ref, m_ref, v_ref, r_ref, g_ref, d_ref, nm_ref, nv_ref):
        g = r_ref[0].astype(F32)
        for s in range(1, NDEV):
            g = g + r_ref[s].astype(F32)
        wv = w_ref[...]
        nm = ADAM_B1 * m_ref[...] + (1.0 - ADAM_B1) * g
        nv = ADAM_B2 * v_ref[...] + (1.0 - ADAM_B2) * (g * g)
        m_hat = nm / _C1
        v_hat = nv / _C2
        g_ref[...] = g
        d_ref[...] = -ADAM_LR * (m_hat / (jnp.sqrt(v_hat) + ADAM_EPS) + ADAM_WD * wv)
        nm_ref[...] = nm
        nv_ref[...] = nv

    blk = pl.BlockSpec((tr, C), lambda i: (i, 0))
    return pl.pallas_call(
        body, grid=(R // tr,), in_specs=[blk, blk, blk, pl.BlockSpec((NDEV, tr, C), lambda i: (0, i, 0))],
        out_specs=[blk] * 4, out_shape=[jax.ShapeDtypeStruct((R, C), F32)] * 4,
        compiler_params=_cp(), name=name,
    )(w, m, v, recv)


def _row(v):
    return v.reshape(1, -1)


def _local_step(xs, tgt, Wp, Wfi, Wfo, Wco, Wmo, Wao, wdw, rel_bias_table, g_pre_mix, b_glu, b_dw, g_conv_ln,
                b_conv_ln, b_conv_out, g_post_mix, g_pre_ffn, g_post_ffn):
    S = xs.shape[0]
    g1, g2, g3, g4 = _row(g_pre_mix), _row(g_post_mix), _row(g_pre_ffn), _row(g_post_ffn)
    bglu, bdw, gln, bln, bco = _row(b_glu), _row(b_dw), _row(g_conv_ln), _row(b_conv_ln), _row(b_conv_out)
    full = (D, F32, D, 0, False)
    fullb = (D, BF, D, 0, False)

    def epi_rms(accs, r, c, o, p):
        v = r[0][...]
        o[0][...] = (v * _rms_r(v) * c[0][...]).astype(BF)

    (h1,) = _fused_mm("rms_in", S, 512, 1, [], [], [], [], [(xs, D, 0, False)], [g1], [fullb], [], epi_rms)

    def epi_cast(accs, r, c, o, p):
        o[0][...] = accs[0].astype(BF)

    (z,) = _fused_mm("in_proj", S, 1024, Z_CB, [(h1, D, 0, 1)], [(Wp, False, D, GW, 0, 1, 0, 0, True)], [(0, 0, 0, 0, 1)],
                     [(1024, GW)], [], [], [(IN_W, BF, GW, 0, True)], [], epi_cast)

    idx = jnp.asarray(_band_index())
    tab_t = rel_bias_table.T.reshape(3, HEADS, REL_BUCKETS)
    bias_all = _bias_build(tab_t, idx)
    os_, ls_ = [], []
    for g, (window, dil) in enumerate(GROUPS):
        o_g, l_g = _attn_fwd(z, bias_all[g].reshape(HEADS, QBLK, 2 * QBLK), g, dil)
        os_.append(o_g)
        ls_.append(l_g)
    o_att, lse = _attn_merge(os_, ls_, S)

    u1, u3 = _conv_fwd(z, bglu, wdw, bdw, gln, bln)

    def epi_mix(accs, r, c, o, p):
        ya = accs[0]
        yc = accs[1] + c[0][...]
        mg = _sig(r[0][...].astype(F32)) * ya + _sig(r[1][...].astype(F32)) * yc
        mgb = mg.astype(BF)
        m2 = jnp.dot(mgb, c[1][...], preferred_element_type=F32)
        x1 = r[2][...] + m2 * _rms_r(m2) * c[2][...]
        o[0][...] = ya.astype(BF)
        o[1][...] = yc.astype(BF)
        o[2][...] = mgb
        o[3][...] = m2.astype(BF)
        o[4][...] = x1
        o[5][...] = (x1 * _rms_r(x1) * c[3][...]).astype(BF)

    y_attn, y_conv, merged, m2, x1, h2 = _fused_mm(
        "mix_fwd", S, 512, 1, [(o_att, GW, 0, 1), (u3, D, 0, 1)],
        [(Wao, False, GW, D, 0, 1, 0, 0, False), (Wco, False, D, D, 0, 1, 0, 0, False)], [(0, 0, 0, 0, 1), (1, 1, 1, 0, 1)],
        [(512, D), (512, D)], [(z, D, 2, False), (z, D, 3, False), (xs, D, 0, False)], [bco, Wmo, g2, g3],
        [fullb, fullb, fullb, fullb, full, fullb], [], epi_mix)

    HN = FFN // 2

    def epi_ffn_in(accs, r, c, o, p):
        gt, up = accs
        o[0][...] = gt.astype(BF)
        o[1][...] = up.astype(BF)
        o[2][...] = (gt * _sig(gt) * up).astype(BF)

    gate, up, act = _fused_mm(
        "ffn_in", S, 512, 2, [(h2, D, 0, 1)],
        [(Wfi, False, D, HN, 0, 1, 0, 0, True), (Wfi, False, D, HN, 0, 1, 0, 2, True)], [(0, 0, 0, 0, 1), (0, 1, 1, 0, 1)],
        [(512, HN), (512, HN)], [], [], [(FFN, BF, HN, 0, True)] * 3, [], epi_ffn_in)

    def epi_loss(accs, r, c, o, p):
        f2 = accs[0]
        g = c[0][...]
        rr = _rms_r(f2)
        err = r[0][...] + f2 * rr * g - r[1][...]
        dy = err * (1.0 / D)
        df2, dgr = _rms_bwd(f2, rr, g, dy)
        o[0][...] = dy
        o[1][...] = df2.astype(BF)
        p[0][...] += _psum8(err * err)
        p[1][...] += _psum8(dgr)

    dy, df2, loss_p, dg4 = _fused_mm(
        "ffn_out_loss", S, 512, 1, [(act, FFN, 0, 1)], [(Wfo, False, FFN, D, 0, 1, 0, 0, False)], [(0, 0, 0, 0, 1)],
        [(512, D)], [(x1, D, 0, False), (tgt, D, 0, False)], [g4], [full, fullb], [(8, D), (8, D)], epi_loss)

    def epi_swiglu(accs, r, c, o, p):
        da = accs[0]
        gt = r[0][...].astype(F32)
        sg = _sig(gt)
        o[0][...] = (da * r[1][...].astype(F32) * sg * (1.0 + gt * (1.0 - sg))).astype(BF)
        o[1][...] = (da * gt * sg).astype(BF)

    dgate, dup = _fused_mm(
        "ffn_out_bwd", S, 512, 2, [(df2, D, 0, 1)], [(Wfo, True, D, HN, 0, 1, 0, 0, True)], [(0, 0, 0, 0, 1)],
        [(512, HN)], [(gate, HN, 0, True), (up, HN, 0, True)], [], [(FFN, BF, HN, 0, True)] * 2, [], epi_swiglu)
    dWfo = _mm_tn("dw_ffn_out", act, df2, HN, D, 1024)

    def epi_dh2(accs, r, c, o, p):
        dh2 = accs[0]
        x1v = r[1][...]
        r3 = _rms_r(x1v)
        d1, dg3r = _rms_bwd(x1v, r3, c[0][...], dh2)
        dx1 = r[0][...] + d1
        m2v = r[2][...].astype(F32)
        r2 = _rms_r(m2v)
        dm2, dg2r = _rms_bwd(m2v, r2, c[1][...], dx1)
        o[0][...] = dx1
        o[1][...] = dm2.astype(BF)
        p[0][...] += _psum8(dg3r)
        p[1][...] += _psum8(dg2r)

    dx1, dm2, dg3, dg2 = _fused_mm(
        "ffn_in_bwd", S, 512, 1, [(dgate, HN, 0, 2), (dup, HN, 2, 2)], [(Wfi, True, HN, D, 0, 4, 0, 0, False)],
        [(0, 0, 0, 0, 2), (1, 0, 0, 2, 2)], [(512, D)], [(dy, D, 0, False), (x1, D, 0, False), (m2, D, 0, False)], [g3, g2],
        [full, fullb], [(8, D), (8, D)], epi_dh2)
    dWfi = jnp.concatenate([_mm_tn("dw_ffn_gate", h2, dgate, D, HN, 1024), _mm_tn("dw_ffn_up", h2, dup, D, HN, 1024)], axis=1)

    def epi_dmix(accs, r, c, o, p):
        dm = accs[0]
        sa = _sig(r[0][...].astype(F32))
        sc = _sig(r[1][...].astype(F32))
        o[0][...] = (dm * sa).astype(BF)
        o[1][...] = (dm * sc).astype(BF)
        o[2][:, 0:D] = (dm * r[2][...].astype(F32) * sa * (1.0 - sa)).astype(BF)
        o[2][:, D:2 * D] = (dm * r[3][...].astype(F32) * sc * (1.0 - sc)).astype(BF)

    dy_attn, dy_conv, dz_gate = _fused_mm(
        "mix_bwd", S, 512, 1, [(dm2, D, 0, 1)], [(Wmo, True, D, D, 0, 1, 0, 0, False)], [(0, 0, 0, 0, 1)], [(512, D)],
        [(z, D, 2, False), (z, D, 3, False), (y_attn, D, 0, False), (y_conv, D, 0, False)], [],
        [fullb, fullb, (2 * D, BF, 2 * D, 0, False)], [], epi_dmix)
    dWmo = _mm_tn("dw_mix_out", merged, dm2, D, D, 1024)

    def epi_dconv(accs, r, c, o, p):
        du3 = accs[0]
        xh, rstd = _ln_hat(r[0][...])
        gl = c[0][...]
        u2 = xh * gl + c[1][...]
        sg = _sig(u2)
        du2 = du3 * sg * (1.0 + u2 * (1.0 - sg))
        dxh = du2 * gl
        du1 = rstd * (dxh - jnp.mean(dxh, axis=-1, keepdims=True) - xh * jnp.mean(dxh * xh, axis=-1, keepdims=True))
        o[0][...] = du1
        p[0][...] += _psum8(du2 * xh)
        p[1][...] += _psum8(du2)
        p[2][...] += _psum8(du1)
        p[3][...] += _psum8(r[1][...].astype(F32))

    du1, dgln, dbln, dbdw, dbco = _fused_mm(
        "conv_out_bwd", S, 512, 1, [(dy_conv, D, 0, 1)], [(Wco, True, D, D, 0, 1, 0, 0, False)], [(0, 0, 0, 0, 1)], [(512, D)],
        [(u1, D, 0, False), (dy_conv, D, 0, False)], [gln, bln], [full], [(8, D)] * 4, epi_dconv)
    dWco = _mm_tn("dw_conv_out", u3, dy_conv, D, D, 1024)
    dz_glu, dwdw, dbglu = _conv_bwd(du1, z, bglu, wdw)

    bd = jnp.asarray(np.kron(np.eye(HEADS, dtype=np.float32), np.ones((HEAD_DIM, HEAD_DIM), np.float32)))

    def epi_do(accs, r, c, o, p):
        do = accs[0]
        o[0][...] = do.astype(BF)
        o[1][...] = jnp.dot(do * r[0][...].astype(F32), c[0][...], preferred_element_type=F32, precision=lax.Precision.HIGHEST)

    do, delta = _fused_mm(
        "attn_out_bwd", S, 1024, 1, [(dy_attn, D, 0, 1)], [(Wao, True, D, GW, 0, 1, 0, 0, False)], [(0, 0, 0, 0, 1)], [(1024, GW)],
        [(o_att, GW, 0, False)], [bd], [(GW, BF, GW, 0, False), (GW, F32, GW, 0, False)], [], epi_do)
    dWao = _mm_tn("dw_attn_out", o_att, dy_attn, GW, D, 1024)

    dq = lax.empty((S, ATTN_W), BF)
    dk = lax.empty((S, ATTN_W), BF)
    dv = lax.empty((S, ATTN_W), BF)
    dbs = []
    for g, (window, dil) in enumerate(GROUPS):
        dq, db = _attn_bwd_dq(z, bias_all[g].reshape(HEADS, QBLK, 2 * QBLK), do, lse, delta, dq, g, dil)
        dk, dv = _attn_bwd_dkv(z, bias_all[3 + g].reshape(HEADS, 2 * QBLK, QBLK), do, lse, delta, dk, dv, g, dil)
        dbs.append(db.reshape(HEADS, _NB))
    dtab = _bias_grad(jnp.stack(dbs), idx)[:, :, :REL_BUCKETS].reshape(3 * HEADS, REL_BUCKETS).T

    def epi_dx(accs, r, c, o, p):
        xv = r[1][...]
        d1, dg1r = _rms_bwd(xv, _rms_r(xv), c[0][...], accs[0])
        o[0][...] = r[0][...] + d1
        p[0][...] += _psum8(dg1r)

    segs = [(dz_glu, 0, 4), (dz_gate, 4, 4), (dq, 8, 3), (dk, 11, 3), (dv, 14, 3)]
    grad_x, dg1 = _fused_mm(
        "in_proj_bwd", S, 1024, 1, [(a, GW, k0, nk) for a, k0, nk in segs], [(Wp, True, GW, D, 0, Z_CB, 0, 0, False)],
        [(t, 0, 0, k0, nk) for t, (a, k0, nk) in enumerate(segs)], [(1024, D)], [(dx1, D, 0, False), (xs, D, 0, False)], [g1],
        [full], [(8, D)], epi_dx)
    dW_in = jnp.concatenate(
        [_mm_tn("dw_in_q", h1, dq, D, ATTN_W, 1024), _mm_tn("dw_in_k", h1, dk, D, ATTN_W, 1024),
         _mm_tn("dw_in_v", h1, dv, D, ATTN_W, 1024), _mm_tn("dw_in_glu", h1, dz_glu, D, D, 1024),
         _mm_tn("dw_in_gate", h1, dz_gate, D, D, 1024)], axis=1)

    small = dict(rel_bias_table=dtab, g_pre_mix=dg1[0], b_glu=dbglu[0], b_dw=dbdw[0], g_conv_ln=dgln[0], b_conv_ln=dbln[0],
                 b_conv_out=dbco[0], g_post_mix=dg2[0], g_pre_ffn=dg3[0], g_post_ffn=dg4[0])
    big = dict(w_in=dW_in, w_ffn_in=dWfi, w_ffn_out=dWfo, w_conv_out=dWco, w_mix_out=dWmo, w_attn_out=dWao, w_dw=dwdw)
    return loss_p[0], grad_x, big, small


SMALL = ['rel_bias_table', 'g_pre_mix', 'b_glu', 'b_dw', 'g_conv_ln', 'b_conv_ln', 'b_conv_out', 'g_post_mix', 'g_pre_ffn',
         'g_post_ffn']
BIG = ['w_in', 'w_ffn_in', 'w_ffn_out', 'w_conv_out', 'w_mix_out', 'w_attn_out', 'w_dw']
WEIGHTS = ['rel_bias_table', 'g_pre_mix', 'w_in', 'b_glu', 'w_dw', 'b_dw', 'g_conv_ln', 'b_conv_ln', 'w_conv_out', 'b_conv_out',
           'w_attn_out', 'w_mix_out', 'g_post_mix', 'g_pre_ffn', 'w_ffn_in', 'w_ffn_out', 'g_post_ffn']
SMALL_ROWS = 16


def _pack_small(d):
    flat = jnp.concatenate([d[n].reshape(-1).astype(F32) for n in SMALL])
    flat = jnp.pad(flat, (0, SMALL_ROWS * D - flat.shape[0]))
    return flat.reshape(SMALL_ROWS, D)


def _unpack_small(p, like):
    flat = p.reshape(-1)
    out, pos = {}, 0
    for n in SMALL:
        sz = like[n].size
        out[n] = flat[pos:pos + sz].reshape(like[n].shape)
        pos += sz
    return out


def _cols_to_blocks(a):
    R = a.shape[0]
    return a.reshape(R, NDEV, a.shape[1] // NDEV).transpose(1, 0, 2)


def _blocks_to_cols(a):
    return a.transpose(1, 0, 2).reshape(a.shape[1], NDEV * a.shape[2])


def kernel(x, rel_bias_table, g_pre_mix, w_in, b_glu, w_dw, b_dw, g_conv_ln, b_conv_ln, w_conv_out, b_conv_out, w_attn_out, w_mix_out, g_post_mix, g_pre_ffn, w_ffn_in, w_ffn_out, g_post_ffn, loss_target, m_rel_bias_table, m_g_pre_mix, m_w_in, m_b_glu, m_w_dw, m_b_dw, m_g_conv_ln, m_b_conv_ln, m_w_conv_out, m_b_conv_out, m_w_attn_out, m_w_mix_out, m_g_post_mix, m_g_pre_ffn, m_w_ffn_in, m_w_ffn_out, m_g_post_ffn, v_rel_bias_table, v_g_pre_mix, v_w_in, v_b_glu, v_w_dw, v_b_dw, v_g_conv_ln, v_b_conv_ln, v_w_conv_out, v_b_conv_out, v_w_attn_out, v_w_mix_out, v_g_post_mix, v_g_pre_ffn, v_w_ffn_in, v_w_ffn_out, v_g_post_ffn):
    w = dict(rel_bias_table=rel_bias_table, g_pre_mix=g_pre_mix, w_in=w_in, b_glu=b_glu, w_dw=w_dw, b_dw=b_dw, g_conv_ln=g_conv_ln, b_conv_ln=b_conv_ln, w_conv_out=w_conv_out, b_conv_out=b_conv_out, w_attn_out=w_attn_out, w_mix_out=w_mix_out, g_post_mix=g_post_mix, g_pre_ffn=g_pre_ffn, w_ffn_in=w_ffn_in, w_ffn_out=w_ffn_out, g_post_ffn=g_post_ffn)
    m = dict(rel_bias_table=m_rel_bias_table, g_pre_mix=m_g_pre_mix, w_in=m_w_in, b_glu=m_b_glu, w_dw=m_w_dw, b_dw=m_b_dw, g_conv_ln=m_g_conv_ln, b_conv_ln=m_b_conv_ln, w_conv_out=m_w_conv_out, b_conv_out=m_b_conv_out, w_attn_out=m_w_attn_out, w_mix_out=m_w_mix_out, g_post_mix=m_g_post_mix, g_pre_ffn=m_g_pre_ffn, w_ffn_in=m_w_ffn_in, w_ffn_out=m_w_ffn_out, g_post_ffn=m_g_post_ffn)
    v = dict(rel_bias_table=v_rel_bias_table, g_pre_mix=v_g_pre_mix, w_in=v_w_in, b_glu=v_b_glu, w_dw=v_w_dw, b_dw=v_b_dw, g_conv_ln=v_g_conv_ln, b_conv_ln=v_b_conv_ln, w_conv_out=v_w_conv_out, b_conv_out=v_b_conv_out, w_attn_out=v_w_attn_out, w_mix_out=v_w_mix_out, g_post_mix=v_g_post_mix, g_pre_ffn=v_g_pre_ffn, w_ffn_in=v_w_ffn_in, w_ffn_out=v_w_ffn_out, g_post_ffn=v_g_post_ffn)

    def shard2d(d, n):
        a = d[n][0]
        return jnp.pad(a, ((0, HALO - CONV_W), (0, 0))) if n == 'w_dw' else a

    gath = _all_gather("gather_weights", [shard2d(w, n).astype(F32 if n == 'w_dw' else BF) for n in BIG])
    gw = dict(zip(BIG, gath))
    W_in = _blocks_to_cols(gw['w_in'])
    Wp = jnp.concatenate([W_in[:, 3 * ATTN_W:], W_in[:, :3 * ATTN_W]], axis=1)
    Wfi = _blocks_to_cols(gw['w_ffn_in'])
    Wfo = gw['w_ffn_out'].reshape(FFN, D)
    Wco = gw['w_conv_out'].reshape(D, D)
    Wmo = gw['w_mix_out'].reshape(D, D)
    Wao = _blocks_to_cols(gw['w_attn_out'])
    wdw = _blocks_to_cols(gw['w_dw'])

    loss_row, grad_x, big, small = _local_step(
        x[0], loss_target[0], Wp, Wfi, Wfo, Wco, Wmo, Wao, wdw, rel_bias_table, g_pre_mix[0], b_glu[0], b_dw[0], g_conv_ln[0],
        b_conv_ln[0], b_conv_out[0], g_post_mix[0], g_pre_ffn[0], g_post_ffn[0])

    loss = lax.psum(jnp.sum(loss_row) * (0.5 / D), ("x", "y", "c"))

    blocks = [_cols_to_blocks(big['w_in']), _cols_to_blocks(big['w_ffn_in']), big['w_ffn_out'].reshape(NDEV, FFN // NDEV, D),
              big['w_conv_out'].reshape(NDEV, D // NDEV, D), big['w_mix_out'].reshape(NDEV, D // NDEV, D),
              _cols_to_blocks(big['w_attn_out']), _cols_to_blocks(big['w_dw'])]
    recv = dict(zip(BIG, _scatter_blocks("scatter_grads", blocks)))
    tiles = dict(w_in=128, w_ffn_in=256, w_ffn_out=176, w_conv_out=128, w_mix_out=128, w_attn_out=512, w_dw=HALO)
    res = {}
    for n in BIG:
        g_, d_, nm_, nv_ = _adamw("adamw_" + n, shard2d(w, n), shard2d(m, n), shard2d(v, n), recv[n], tiles[n])
        if n == 'w_dw':
            g_, d_, nm_, nv_ = (t[:CONV_W] for t in (g_, d_, nm_, nv_))
        res[n] = tuple(t[None] for t in (g_, d_, nm_, nv_))

    (srecv,) = _all_gather("gather_small_grads", [_pack_small(small)])
    sg, sd, sm, sv = _adamw("adamw_small", _pack_small(w), _pack_small(m), _pack_small(v), srecv, SMALL_ROWS)
    unpacked = [_unpack_small(p, w) for p in (sg, sd, sm, sv)]
    for n in SMALL:
        res[n] = tuple(u[n] for u in unpacked)
    return (loss, grad_x[None], *[res[n][0] for n in WEIGHTS], *[res[n][1] for n in WEIGHTS],
            *[res[n][2] for n in WEIGHTS], *[res[n][3] for n in WEIGHTS])
```

```python
import functools
import math

import numpy as np
import jax
import jax.numpy as jnp
from jax import lax
from jax.experimental import pallas as pl
from jax.experimental.pallas import tpu as pltpu

F32 = jnp.float32
BF = jnp.bfloat16

D = 1024
HEAD_DIM = 64
HEADS = 8
GROUPS = ((128, 1), (512, 4), (2048, 16))
QBLK = 128
GW = HEADS * HEAD_DIM
ATTN_W = 3 * GW
REL_BUCKETS = 32
REL_MAX_DISTANCE = 2048
CONV_W = 31
HALO = 32
FFN = 2816
IN_W = 3 * ATTN_W + 2 * D + 2 * D
RMS_EPS = 1e-6
LN_EPS = 1e-5
NEG_INF = -1e30
SCALE = HEAD_DIM ** -0.5
NDEV = 8

ADAM_LR = 0.001
ADAM_B1 = 0.9
ADAM_B2 = 0.999
ADAM_EPS = 1e-08
ADAM_WD = 0.01
ADAM_STEP = 10

Z_G0 = 4096 // GW
Z_CB = IN_W // GW


def _kvq_blocks(g):
    return (Z_G0 + 3 * g, Z_G0 + 3 * g + 1, Z_G0 + 3 * g + 2)


VMEM_LIMIT = 52 * 1024 * 1024


def _cp(sem=None):
    if sem is None:
        return pltpu.CompilerParams(vmem_limit_bytes=VMEM_LIMIT)
    return pltpu.CompilerParams(vmem_limit_bytes=VMEM_LIMIT, dimension_semantics=sem)


def _sig(v):
    return jax.nn.sigmoid(v)


def _psum8(v):
    return v.reshape(v.shape[0] // 8, 8, v.shape[1]).sum(axis=0)


def _rms_r(v):
    return lax.rsqrt(jnp.mean(v * v, axis=-1, keepdims=True) + RMS_EPS)


def _rms_bwd(v, r, g, dy):
    gy = dy * g
    dv = r * gy - v * (r * r * r) * jnp.mean(v * gy, axis=-1, keepdims=True)
    return dv, dy * v * r


def _clip_k(k, k0, nk):
    return jnp.clip(k - k0, 0, nk - 1)


def _fused_mm(name, M, tm, grid_n, a_ops, b_ops, terms, acc_shapes, rows, consts, outs, parts, epilogue):
    gm = M // tm
    nk_total = max([t[3] + t[4] for t in terms], default=1)
    n_a, n_b, n_r, n_c, n_o, n_p = len(a_ops), len(b_ops), len(rows), len(consts), len(outs), len(parts)
    n_acc = len(acc_shapes)
    use_scratch = nk_total > 1
    if parts:
        assert grid_n == 1

    def jj(j, follow):
        return j if follow else 0

    in_specs, args = [], []
    for (arr, tk, k0, nk) in a_ops:
        in_specs.append(pl.BlockSpec((tm, tk), functools.partial(lambda i, j, k, k0, nk: (i, _clip_k(k, k0, nk)), k0=k0, nk=nk)))
        args.append(arr)
    for (arr, nt, tk, tn, k0, nk, koff, joff, fj) in b_ops:
        if nt:
            in_specs.append(pl.BlockSpec((tn, tk), functools.partial(
                lambda i, j, k, k0, nk, koff, joff, fj: (joff + jj(j, fj), _clip_k(k, k0, nk) + koff),
                k0=k0, nk=nk, koff=koff, joff=joff, fj=fj)))
        else:
            in_specs.append(pl.BlockSpec((tk, tn), functools.partial(
                lambda i, j, k, k0, nk, koff, joff, fj: (_clip_k(k, k0, nk) + koff, joff + jj(j, fj)),
                k0=k0, nk=nk, koff=koff, joff=joff, fj=fj)))
        args.append(arr)
    for (arr, w, off, fj) in rows:
        in_specs.append(pl.BlockSpec((tm, w), functools.partial(lambda i, j, k, off, fj: (i, off + jj(j, fj)), off=off, fj=fj)))
        args.append(arr)
    for arr in consts:
        in_specs.append(pl.BlockSpec(arr.shape, functools.partial(lambda i, j, k, nd: (0,) * nd, nd=arr.ndim)))
        args.append(arr)
    out_specs, out_shape = [], []
    for (ncols, dt, w, off, fj) in outs:
        out_specs.append(pl.BlockSpec((tm, w), functools.partial(lambda i, j, k, off, fj: (i, off + jj(j, fj)), off=off, fj=fj)))
        out_shape.append(jax.ShapeDtypeStruct((M, ncols), dt))
    for (r, c) in parts:
        out_specs.append(pl.BlockSpec((r, c), lambda i, j, k: (0, 0)))
        out_shape.append(jax.ShapeDtypeStruct((r, c), F32))
    scratch = [pltpu.VMEM(s, F32) for s in acc_shapes] if use_scratch else []

    def body(*refs):
        pos = 0
        a_refs = refs[pos:pos + n_a]; pos += n_a
        b_refs = refs[pos:pos + n_b]; pos += n_b
        r_refs = refs[pos:pos + n_r]; pos += n_r
        c_refs = refs[pos:pos + n_c]; pos += n_c
        o_refs = refs[pos:pos + n_o]; pos += n_o
        p_refs = refs[pos:pos + n_p]; pos += n_p
        acc_refs = refs[pos:pos + n_acc] if use_scratch else ()
        i = pl.program_id(0)
        k = pl.program_id(2)

        def dot_of(ai, bi):
            a = a_refs[ai][...].astype(BF)
            b = b_refs[bi][...].astype(BF)
            if b_ops[bi][1]:
                return lax.dot_general(a, b, (((1,), (1,)), ((), ())), preferred_element_type=F32)
            return jnp.dot(a, b, preferred_element_type=F32)

        if parts:
            @pl.when((i == 0) & (k == 0))
            def _():
                for p in p_refs:
                    p[...] = jnp.zeros(p.shape, F32)

        def finish(accs):
            epilogue(accs, r_refs, c_refs, o_refs, p_refs)
            if parts:
                @pl.when(i == gm - 1)
                def _():
                    for p in p_refs:
                        p[0:1, :] = jnp.sum(p[...], axis=0, keepdims=True)

        if not use_scratch:
            accs = [None] * n_acc
            for (ai, bi, ci, k0, nk) in terms:
                d = dot_of(ai, bi)
                accs[ci] = d if accs[ci] is None else accs[ci] + d
            finish(accs)
        else:
            @pl.when(k == 0)
            def _():
                for acc in acc_refs:
                    acc[...] = jnp.zeros(acc.shape, F32)

            for (ai, bi, ci, k0, nk) in terms:
                def do(ai=ai, bi=bi, ci=ci):
                    acc_refs[ci][...] += dot_of(ai, bi)
                if k0 == 0 and nk == nk_total:
                    do()
                else:
                    pl.when((k >= k0) & (k < k0 + nk))(do)

            @pl.when(k == nk_total - 1)
            def _():
                finish([acc[...] for acc in acc_refs])

    res = pl.pallas_call(
        body, grid=(gm, grid_n, nk_total), in_specs=in_specs, out_specs=out_specs, out_shape=out_shape,
        scratch_shapes=scratch, compiler_params=_cp(("arbitrary", "arbitrary", "arbitrary")), name=name,
    )(*args)
    return res


def _mm_tn(name, a, b, tm, tn, tk):
    S, Ka = a.shape
    Nb = b.shape[1]
    nk = S // tk

    def body(a_ref, b_ref, o_ref, acc):
        k = pl.program_id(2)

        @pl.when(k == 0)
        def _():
            acc[...] = jnp.zeros(acc.shape, F32)

        acc[...] += lax.dot_general(a_ref[...], b_ref[...], (((0,), (0,)), ((), ())), preferred_element_type=F32)

        @pl.when(k == nk - 1)
        def _():
            o_ref[...] = acc[...].astype(o_ref.dtype)

    return pl.pallas_call(
        body, grid=(Ka // tm, Nb // tn, nk),
        in_specs=[pl.BlockSpec((tk, tm), lambda i, j, k: (k, i)), pl.BlockSpec((tk, tn), lambda i, j, k: (k, j))],
        out_specs=pl.BlockSpec((tm, tn), lambda i, j, k: (i, j)),
        out_shape=jax.ShapeDtypeStruct((Ka, Nb), BF),
        scratch_shapes=[pltpu.VMEM((tm, tn), F32)],
        compiler_params=_cp(("parallel", "parallel", "arbitrary")), name=name,
    )(a, b)


def _rel_bucket_np(dist):
    max_exact = REL_BUCKETS // 2
    d = np.maximum(dist, 0)
    df = np.maximum(d, 1).astype(np.float32)
    large = max_exact + (np.log(df / np.float32(max_exact)) / np.float32(math.log(REL_MAX_DISTANCE / max_exact))
                         * np.float32(REL_BUCKETS - max_exact)).astype(np.int32)
    large = np.minimum(large, REL_BUCKETS - 1)
    return np.where(d < max_exact, d, large).astype(np.int32)


N_LAYOUTS = 3


def _band_index():
    idx = np.zeros((N_LAYOUTS * 3, 1, QBLK * 2 * QBLK), np.int32)
    for g, (window, dil) in enumerate(GROUPS):
        span = window // dil
        a = np.arange(QBLK)[:, None]; c = np.arange(2 * QBLK)[None, :]
        off = a - c + QBLK
        lay0 = np.where((off >= 0) & (off <= span), _rel_bucket_np(off * dil), -1)
        idx[g, 0] = lay0.reshape(-1)
        idx[3 + g, 0] = lay0.T.reshape(-1)
        k = np.arange(QBLK)[:, None]; q = np.arange(2 * QBLK)[None, :]
        off = q - k
        idx[6 + g, 0] = np.where((off >= 0) & (off <= span), _rel_bucket_np(off * dil), -1).reshape(-1)
    return idx


_NB = QBLK * 2 * QBLK
_BCH = 4096


def _bias_build(tab_t, idx):
    def body(t_ref, i_ref, o_ref):
        ix = i_ref[0]
        t = t_ref[0]
        acc = jnp.full((HEADS, _BCH), NEG_INF, F32)
        for b in range(REL_BUCKETS):
            acc = jnp.where(ix == b, t[:, b:b + 1], acc)
        o_ref[0] = acc

    return pl.pallas_call(
        body, grid=(N_LAYOUTS * 3, _NB // _BCH),
        in_specs=[pl.BlockSpec((1, HEADS, REL_BUCKETS), lambda l, n: (l % 3, 0, 0)),
                  pl.BlockSpec((1, 1, _BCH), lambda l, n: (l, 0, n))],
        out_specs=pl.BlockSpec((1, HEADS, _BCH), lambda l, n: (l, 0, n)),
        out_shape=jax.ShapeDtypeStruct((N_LAYOUTS * 3, HEADS, _NB), F32), compiler_params=_cp(), name="bias_build",
    )(tab_t, idx)


def _bias_grad(ds, idx):
    nch = _NB // _BCH

    def body(d_ref, i_ref, o_ref):
        n = pl.program_id(1)

        @pl.when(n == 0)
        def _():
            o_ref[...] = jnp.zeros(o_ref.shape, F32)

        ix = i_ref[0]
        d = d_ref[0]
        lane = lax.broadcasted_iota(jnp.int32, (HEADS, 128), 1)
        acc = jnp.zeros((HEADS, 128), F32)
        for b in range(REL_BUCKETS):
            s = jnp.sum(jnp.where(ix == b, d, 0.0), axis=1, keepdims=True)
            acc = acc + jnp.where(lane == b, s, 0.0)
        o_ref[0] += acc

    return pl.pallas_call(
        body, grid=(3, nch),
        in_specs=[pl.BlockSpec((1, HEADS, _BCH), lambda l, n: (l, 0, n)),
                  pl.BlockSpec((1, 1, _BCH), lambda l, n: (l, 0, n))],
        out_specs=pl.BlockSpec((1, HEADS, 128), lambda l, n: (l, 0, 0)),
        out_shape=jax.ShapeDtypeStruct((3, HEADS, 128), F32), compiler_params=_cp(), name="bias_grad",
    )(ds, idx)


PT = 256
PSTEP = 1024
STAT_W = 128


def _perm_np(dil):
    p = np.zeros((PT, PT), np.float32)
    m = np.arange(PT // dil)
    for c in range(dil):
        p[c * (PT // dil) + m, m * dil + c] = 1.0
    return p


def _perm_const(dil, dtype, inverse):
    p = _perm_np(dil)
    return jnp.asarray(p.T if inverse else p, dtype)


def _apply_perm(p, x):
    if x.dtype == F32:
        return jnp.dot(p, x, preferred_element_type=F32, precision=lax.Precision.HIGHEST)
    return jnp.dot(p, x, preferred_element_type=F32)


def _to_residue(name, arr, col_blocks, dil):
    S = arr.shape[0]
    nc = len(col_blocks)
    p = _perm_const(dil, arr.dtype, False)
    sub = PT // dil

    def body(*refs):
        p_ref, ins, o_ref = refs[0], refs[1:1 + nc], refs[1 + nc]
        for u in range(PSTEP // PT):
            for t, r in enumerate(ins):
                y = _apply_perm(p_ref[...], r[u * PT:(u + 1) * PT, :]).astype(o_ref.dtype)
                o_ref[:, u * sub:(u + 1) * sub, t * GW:(t + 1) * GW] = y.reshape(dil, sub, GW)

    out = pl.pallas_call(
        body, grid=(S // PSTEP,),
        in_specs=[pl.BlockSpec((PT, PT), lambda i: (0, 0))]
                 + [pl.BlockSpec((PSTEP, GW), functools.partial(lambda i, cb: (i, cb), cb=cb)) for cb in col_blocks],
        out_specs=pl.BlockSpec((dil, PSTEP // dil, nc * GW), lambda i: (0, i, 0)),
        out_shape=jax.ShapeDtypeStruct((dil, S // dil, nc * GW), arr.dtype), compiler_params=_cp(), name=name,
    )(p, *([arr] * nc))
    return out.reshape(S, nc * GW)


def _to_residue_stats(name, arr, dil):
    S = arr.shape[0]
    p = _perm_const(dil, F32, False)
    sub = PT // dil

    def body(p_ref, x_ref, o_ref):
        for u in range(PSTEP // PT):
            y = _apply_perm(p_ref[...], x_ref[u * PT:(u + 1) * PT, :])
            o_ref[:, u * sub:(u + 1) * sub, :] = y.reshape(dil, sub, STAT_W)

    out = pl.pallas_call(
        body, grid=(S // PSTEP,),
        in_specs=[pl.BlockSpec((PT, PT), lambda i: (0, 0)), pl.BlockSpec((PSTEP, STAT_W), lambda i: (i, 0))],
        out_specs=pl.BlockSpec((dil, PSTEP // dil, STAT_W), lambda i: (0, i, 0)),
        out_shape=jax.ShapeDtypeStruct((dil, S // dil, STAT_W), F32), compiler_params=_cp(), name=name,
    )(p, arr)
    return out.reshape(S, STAT_W)


def _from_residue(name, arr, dil):
    S, W = arr.shape
    p = _perm_const(dil, arr.dtype, True)
    sub = PT // dil

    def body(p_ref, x_ref, o_ref):
        for u in range(PSTEP // PT):
            x = x_ref[:, u * sub:(u + 1) * sub, :].reshape(PT, W)
            o_ref[u * PT:(u + 1) * PT, :] = _apply_perm(p_ref[...], x).astype(o_ref.dtype)

    return pl.pallas_call(
        body, grid=(S // PSTEP,),
        in_specs=[pl.BlockSpec((PT, PT), lambda i: (0, 0)), pl.BlockSpec((dil, PSTEP // dil, W), lambda i: (0, i, 0))],
        out_specs=pl.BlockSpec((PSTEP, W), lambda i: (i, 0)),
        out_shape=jax.ShapeDtypeStruct((S, W), arr.dtype), compiler_params=_cp(), name=name,
    )(p, arr.reshape(dil, S // dil, W))


NT_DIMS = (((1,), (1,)), ((), ()))
TN_DIMS = (((0,), (0,)), ((), ()))


def _attn_dims(S, dil):
    L = S // dil
    TQ = min(512, L)
    return L, TQ, L // TQ, TQ // QBLK


def _attn_specs(S, dil):
    L, TQ, nq, nsub = _attn_dims(S, dil)
    nb = L // QBLK
    cur = lambda cb, w=GW: pl.BlockSpec((TQ, w), lambda c, i: (c * nq + i, cb))
    prev = lambda cb, w=GW: pl.BlockSpec((QBLK, w), lambda c, i: (c * nb + jnp.maximum(i * nsub - 1, 0), cb))
    nxt = lambda cb, w=GW: pl.BlockSpec((QBLK, w), lambda c, i: (c * nb + jnp.minimum((i + 1) * nsub, nb - 1), cb))
    band = lambda r, c_: pl.BlockSpec((HEADS, r, c_), lambda c, i: (0, 0, 0))
    return L, TQ, nq, nsub, cur, prev, nxt, band


def _fill(buf, first_ref, second_ref):
    n = first_ref.shape[0]
    buf[0:n, :] = first_ref[...]
    buf[n:n + second_ref.shape[0], :] = second_ref[...]


def _attn_fwd(name, arr, bias_kq, cb, dil):
    S = arr.shape[0]
    kcb, vcb, qcb = cb
    L, TQ, nq, nsub, cur, prev, nxt, band = _attn_specs(S, dil)

    def body(q_ref, kc_ref, kp_ref, vc_ref, vp_ref, b_ref, o_ref, l_ref, kbuf, vbuf):
        i = pl.program_id(1)
        _fill(kbuf, kp_ref, kc_ref)
        _fill(vbuf, vp_ref, vc_ref)
        row = lax.broadcasted_iota(jnp.int32, (2 * QBLK, QBLK), 0)
        first = (row >= QBLK) | (i > 0)
        for j in range(nsub):
            rs = slice(j * QBLK, (j + 1) * QBLK)
            ks = slice(j * QBLK, (j + 2) * QBLK)
            lrows = []
            for h in range(HEADS):
                hs = slice(h * HEAD_DIM, (h + 1) * HEAD_DIM)
                s = lax.dot_general(kbuf[ks, hs], q_ref[rs, hs], NT_DIMS, preferred_element_type=F32) * SCALE + b_ref[h]
                if j == 0:
                    s = jnp.where(first, s, NEG_INF)
                m = jnp.max(s, axis=0, keepdims=True)
                p = jnp.exp(s - m)
                den = jnp.sum(p, axis=0, keepdims=True)
                o_t = lax.dot_general(vbuf[ks, hs], p.astype(BF), TN_DIMS, preferred_element_type=F32) / den
                o_ref[rs, hs] = o_t.T.astype(BF)
                lrows.append(m + jnp.log(den))
            lt = jnp.concatenate(lrows + [jnp.zeros((STAT_W - HEADS, QBLK), F32)], axis=0)
            l_ref[rs, :] = lt.T

    return pl.pallas_call(
        body, grid=(dil, nq),
        in_specs=[cur(qcb), cur(kcb), prev(kcb), cur(vcb), prev(vcb), band(2 * QBLK, QBLK)],
        out_specs=[cur(0), cur(0, STAT_W)],
        out_shape=[jax.ShapeDtypeStruct((S, GW), BF), jax.ShapeDtypeStruct((S, STAT_W), F32)],
        scratch_shapes=[pltpu.VMEM((QBLK + TQ, GW), BF), pltpu.VMEM((QBLK + TQ, GW), BF)],
        compiler_params=_cp(), name=name,
    )(arr, arr, arr, arr, arr, bias_kq)


def _attn_bwd_dkv(name, arr, bias_kq2, do, stats, cb, dil):
    S = arr.shape[0]
    kcb, vcb, qcb = cb
    L, TQ, nq, nsub, cur, prev, nxt, band = _attn_specs(S, dil)

    def body(k_ref, v_ref, qc_ref, qn_ref, b_ref, doc_ref, don_ref, sc_ref, sn_ref, o_ref, qbuf, dobuf, sbuf):
        i = pl.program_id(1)
        _fill(qbuf, qc_ref, qn_ref)
        _fill(dobuf, doc_ref, don_ref)
        for j in range(nsub + 1):
            rs = slice(j * QBLK, (j + 1) * QBLK)
            sbuf[:, rs] = (sc_ref[rs, :] if j < nsub else sn_ref[...]).T
        col = lax.broadcasted_iota(jnp.int32, (QBLK, 2 * QBLK), 1)
        last = (col < QBLK) | (i < nq - 1)
        for j in range(nsub):
            rs = slice(j * QBLK, (j + 1) * QBLK)
            qs = slice(j * QBLK, (j + 2) * QBLK)
            for h in range(HEADS):
                hs = slice(h * HEAD_DIM, (h + 1) * HEAD_DIM)
                qq = qbuf[qs, hs]
                dd = dobuf[qs, hs]
                s = lax.dot_general(k_ref[rs, hs], qq, NT_DIMS, preferred_element_type=F32) * SCALE + b_ref[h]
                if j == nsub - 1:
                    s = jnp.where(last, s, NEG_INF)
                p = jnp.exp(s - sbuf[h:h + 1, qs])
                dp = lax.dot_general(v_ref[rs, hs], dd, NT_DIMS, preferred_element_type=F32)
                ds = p * (dp - sbuf[HEADS + h:HEADS + h + 1, qs])
                o_ref[rs, h * HEAD_DIM:(h + 1) * HEAD_DIM] = (jnp.dot(ds.astype(BF), qq, preferred_element_type=F32) * SCALE).astype(BF)
                o_ref[rs, GW + h * HEAD_DIM:GW + (h + 1) * HEAD_DIM] = jnp.dot(p.astype(BF), dd, preferred_element_type=F32).astype(BF)

    return pl.pallas_call(
        body, grid=(dil, nq),
        in_specs=[cur(kcb), cur(vcb), cur(qcb), nxt(qcb), band(QBLK, 2 * QBLK),
                  cur(0), nxt(0), cur(0, STAT_W), nxt(0, STAT_W)],
        out_specs=cur(0, 2 * GW),
        out_shape=jax.ShapeDtypeStruct((S, ATTN_W), BF),
        scratch_shapes=[pltpu.VMEM((TQ + QBLK, GW), BF), pltpu.VMEM((TQ + QBLK, GW), BF), pltpu.VMEM((STAT_W, TQ + QBLK), F32)],
        compiler_params=_cp(), name=name,
    )(arr, arr, arr, arr, bias_kq2, do, do, stats, stats)


def _attn_bwd_dq(name, arr, bias_qk, do, stats, dkv, cb, dil):
    S = arr.shape[0]
    kcb, vcb, qcb = cb
    L, TQ, nq, nsub, cur, prev, nxt, band = _attn_specs(S, dil)

    def body(q_ref, kc_ref, kp_ref, vc_ref, vp_ref, b_ref, do_ref, st_ref, _alias, dq_ref, db_ref, kbuf, vbuf):
        c = pl.program_id(0)
        i = pl.program_id(1)

        @pl.when((c == 0) & (i == 0))
        def _():
            db_ref[...] = jnp.zeros(db_ref.shape, F32)

        _fill(kbuf, kp_ref, kc_ref)
        _fill(vbuf, vp_ref, vc_ref)
        col = lax.broadcasted_iota(jnp.int32, (QBLK, 2 * QBLK), 1)
        first = (col >= QBLK) | (i > 0)
        for h in range(HEADS):
            hs = slice(h * HEAD_DIM, (h + 1) * HEAD_DIM)
            bias_h = b_ref[h]
            db = jnp.zeros((QBLK, 2 * QBLK), F32)
            for j in range(nsub):
                rs = slice(j * QBLK, (j + 1) * QBLK)
                ks = slice(j * QBLK, (j + 2) * QBLK)
                kk = kbuf[ks, hs]
                s = lax.dot_general(q_ref[rs, hs], kk, NT_DIMS, preferred_element_type=F32) * SCALE + bias_h
                if j == 0:
                    s = jnp.where(first, s, NEG_INF)
                p = jnp.exp(s - st_ref[rs, h:h + 1])
                dp = lax.dot_general(do_ref[rs, hs], vbuf[ks, hs], NT_DIMS, preferred_element_type=F32)
                ds = p * (dp - st_ref[rs, HEADS + h:HEADS + h + 1])
                db = db + ds
                dq_ref[rs, hs] = (jnp.dot(ds.astype(BF), kk, preferred_element_type=F32) * SCALE).astype(BF)
            db_ref[h] += db

    return pl.pallas_call(
        body, grid=(dil, nq),
        in_specs=[cur(qcb), cur(kcb), prev(kcb), cur(vcb), prev(vcb), band(QBLK, 2 * QBLK),
                  cur(0), cur(0, STAT_W), pl.BlockSpec(memory_space=pl.ANY)],
        out_specs=[cur(2), band(QBLK, 2 * QBLK)],
        out_shape=[jax.ShapeDtypeStruct((S, ATTN_W), BF), jax.ShapeDtypeStruct((HEADS, QBLK, 2 * QBLK), F32)],
        scratch_shapes=[pltpu.VMEM((QBLK + TQ, GW), BF), pltpu.VMEM((QBLK + TQ, GW), BF)],
        input_output_aliases={8: 0},
        compiler_params=_cp(("arbitrary", "arbitrary")), name=name,
    )(arr, arr, arr, arr, arr, bias_qk, do, stats, dkv)


def _head_expand():
    e = np.zeros((STAT_W, GW), np.float32)
    for h in range(HEADS):
        e[h, h * HEAD_DIM:(h + 1) * HEAD_DIM] = 1.0
    return e


def _attn_merge(os_, ls_, S):
    dils = [d for _, d in GROUPS]
    pb = [_perm_const(d, BF, True) for d in dils[1:]]
    pf = [_perm_const(d, F32, True) for d in dils[1:]]
    expand = jnp.asarray(_head_expand())

    def body(o0, o1, o2, l0, l1, l2, pb1, pb2, pf1, pf2, e_ref, o_ref, l_ref):
        for u in range(PSTEP // PT):
            rs = slice(u * PT, (u + 1) * PT)
            res = lambda r, d: r[:, u * (PT // d):(u + 1) * (PT // d), :].reshape(PT, r.shape[2])
            ov = [o0[rs, :].astype(F32), _apply_perm(pb1[...], res(o1, dils[1])), _apply_perm(pb2[...], res(o2, dils[2]))]
            lv = [l0[rs, :], _apply_perm(pf1[...], res(l1, dils[1])), _apply_perm(pf2[...], res(l2, dils[2]))]
            m = jnp.maximum(jnp.maximum(lv[0], lv[1]), lv[2])
            ev = [jnp.exp(l - m) for l in lv]
            den = ev[0] + ev[1] + ev[2]
            acc = jnp.zeros((PT, GW), F32)
            for g in range(3):
                wide = jnp.dot(ev[g] / den, e_ref[...], preferred_element_type=F32, precision=lax.Precision.HIGHEST)
                acc = acc + wide * ov[g]
            o_ref[rs, :] = acc.astype(BF)
            l_ref[rs, :] = m + jnp.log(den)

    nat = lambda w: pl.BlockSpec((PSTEP, w), lambda i: (i, 0))
    res = lambda d, w: pl.BlockSpec((d, PSTEP // d, w), lambda i: (0, i, 0))
    cst = lambda a: pl.BlockSpec(a.shape, lambda i: (0, 0))
    args = [os_[0], os_[1].reshape(dils[1], S // dils[1], GW), os_[2].reshape(dils[2], S // dils[2], GW),
            ls_[0], ls_[1].reshape(dils[1], S // dils[1], STAT_W), ls_[2].reshape(dils[2], S // dils[2], STAT_W),
            pb[0], pb[1], pf[0], pf[1], expand]
    return pl.pallas_call(
        body, grid=(S // PSTEP,),
        in_specs=[nat(GW), res(dils[1], GW), res(dils[2], GW), nat(STAT_W), res(dils[1], STAT_W), res(dils[2], STAT_W)]
                 + [cst(a) for a in args[6:]],
        out_specs=[nat(GW), nat(STAT_W)],
        out_shape=[jax.ShapeDtypeStruct((S, GW), BF), jax.ShapeDtypeStruct((S, STAT_W), F32)],
        compiler_params=_cp(), name="attn_merge",
    )(*args)


CT = 256
CBUF = HALO + CT + 8
RG = 4


def _ln_hat(u1):
    mu = jnp.mean(u1, axis=-1, keepdims=True)
    xc = u1 - mu
    rstd = lax.rsqrt(jnp.mean(xc * xc, axis=-1, keepdims=True) + LN_EPS)
    return xc * rstd, rstd


def _glu_window(hu_ref, hg_ref, huh_ref, hgh_ref, bglu_ref, buf_ref, i):
    bu = bglu_ref[:, 0:D]
    bg = bglu_ref[:, D:2 * D]
    uh = (huh_ref[...].astype(F32) + bu) * _sig(hgh_ref[...].astype(F32) + bg)
    buf_ref[0:HALO, :] = jnp.where(i > 0, uh, 0.0)
    a = hu_ref[...].astype(F32) + bu
    s = _sig(hg_ref[...].astype(F32) + bg)
    buf_ref[HALO:HALO + CT, :] = a * s
    buf_ref[HALO + CT:CBUF, :] = jnp.zeros((8, D), F32)
    return a, s


def _shift_copies(buf_ref, sh_ref):
    for r in range(8):
        sh_ref[r] = buf_ref[r:r + HALO + CT, :]


def _tap_rows(wb_ref, w_ref):
    for j in range(CONV_W):
        wb_ref[j * 8:(j + 1) * 8, :] = jnp.broadcast_to(w_ref[j:j + 1, :], (8, D))


def _conv_taps(sh_ref, wb_ref, out_ref, init, offset):
    for rg in range(CT // (8 * RG)):
        accs = [init] * RG
        for j in range(CONV_W):
            off = offset(j)
            wj = wb_ref[j * 8:(j + 1) * 8, :]
            for q in range(RG):
                row = 8 * (rg * RG + q + off // 8)
                accs[q] = accs[q] + wj * sh_ref[off % 8, row:row + 8, :]
        for q in range(RG):
            out_ref[(rg * RG + q) * 8:(rg * RG + q + 1) * 8, :] = accs[q]


def _conv_specs(S):
    cur = lambda cb: pl.BlockSpec((CT, D), lambda i: (i, cb))
    halo = lambda cb: pl.BlockSpec((HALO, D), lambda i: (jnp.maximum(i * (CT // HALO) - 1, 0), cb))
    full = lambda shp: pl.BlockSpec(shp, lambda i: (0, 0))
    return cur, halo, full


def _conv_fwd(z, b_glu, w_dw, b_dw, g_ln, b_ln):
    S = z.shape[0]
    cur, halo, full = _conv_specs(S)

    def body(hu, hg, huh, hgh, bglu, w, bdw, gln, bln, u1_ref, u3_ref, buf, sh, wb):
        i = pl.program_id(0)

        @pl.when(i == 0)
        def _():
            _tap_rows(wb, w)

        _glu_window(hu, hg, huh, hgh, bglu, buf, i)
        _shift_copies(buf, sh)
        _conv_taps(sh, wb, u1_ref, jnp.broadcast_to(bdw[...], (8, D)), lambda j: 2 + j)
        xh, _ = _ln_hat(u1_ref[...])
        u2 = xh * gln[...] + bln[...]
        u3_ref[...] = (u2 * _sig(u2)).astype(BF)

    return pl.pallas_call(
        body, grid=(S // CT,),
        in_specs=[cur(0), cur(1), halo(0), halo(1), full((1, 2 * D)), full((HALO, D)), full((1, D)), full((1, D)), full((1, D))],
        out_specs=[pl.BlockSpec((CT, D), lambda i: (i, 0))] * 2,
        out_shape=[jax.ShapeDtypeStruct((S, D), F32), jax.ShapeDtypeStruct((S, D), BF)],
        scratch_shapes=[pltpu.VMEM((CBUF, D), F32), pltpu.VMEM((8, HALO + CT, D), F32), pltpu.VMEM((CONV_W * 8, D), F32)],
        compiler_params=_cp(("arbitrary",)), name="conv_fwd",
    )(z, z, z, z, b_glu, w_dw, b_dw, g_ln, b_ln)


def _conv_bwd(du1, z, b_glu, w_dw):
    S = z.shape[0]
    n = S // CT
    cur, halo, full = _conv_specs(S)

    def body(du, dun, hu, hg, huh, hgh, bglu, w, dz_ref, dw_ref, dbg_ref, bufu, bufd, shu, shd, wb, du0_ref, dwacc):
        i = pl.program_id(0)

        @pl.when(i == 0)
        def _():
            _tap_rows(wb, w)
            dwacc[...] = jnp.zeros(dwacc.shape, F32)
            dbg_ref[...] = jnp.zeros(dbg_ref.shape, F32)

        a, s = _glu_window(hu, hg, huh, hgh, bglu, bufu, i)
        bufd[0:CT, :] = du[...]
        bufd[CT:CT + HALO, :] = jnp.where(i < n - 1, dun[...], 0.0)
        bufd[CT + HALO:CBUF, :] = jnp.zeros((8, D), F32)
        _shift_copies(bufu, shu)
        _shift_copies(bufd, shd)
        _conv_taps(shd, wb, du0_ref, jnp.zeros((8, D), F32), lambda j: 30 - j)
        for rg in range(CT // (8 * RG)):
            dch = [bufd[(rg * RG + q) * 8:(rg * RG + q + 1) * 8, :] for q in range(RG)]
            for j in range(CONV_W):
                off = 2 + j
                acc = dwacc[j * 8:(j + 1) * 8, :]
                for q in range(RG):
                    row = 8 * (rg * RG + q + off // 8)
                    acc = acc + dch[q] * shu[off % 8, row:row + 8, :]
                dwacc[j * 8:(j + 1) * 8, :] = acc
        du0 = du0_ref[...]
        dhu = du0 * s
        dhg = du0 * a * s * (1.0 - s)
        dz_ref[:, 0:D] = dhu.astype(BF)
        dz_ref[:, D:2 * D] = dhg.astype(BF)
        dbg_ref[:, 0:D] += _psum8(dhu)
        dbg_ref[:, D:2 * D] += _psum8(dhg)

        @pl.when(i == n - 1)
        def _():
            dbg_ref[0:1, :] = jnp.sum(dbg_ref[...], axis=0, keepdims=True)
            for j in range(CONV_W):
                dw_ref[j:j + 1, :] = jnp.sum(dwacc[j * 8:(j + 1) * 8, :], axis=0, keepdims=True)
            dw_ref[CONV_W:HALO, :] = jnp.zeros((HALO - CONV_W, D), F32)

    nxt = pl.BlockSpec((HALO, D), lambda i: (jnp.minimum((i + 1) * (CT // HALO), S // HALO - 1), 0))
    return pl.pallas_call(
        body, grid=(n,),
        in_specs=[pl.BlockSpec((CT, D), lambda i: (i, 0)), nxt, cur(0), cur(1), halo(0), halo(1), full((1, 2 * D)), full((HALO, D))],
        out_specs=[pl.BlockSpec((CT, 2 * D), lambda i: (i, 0)), full((HALO, D)), full((8, 2 * D))],
        out_shape=[jax.ShapeDtypeStruct((S, 2 * D), BF), jax.ShapeDtypeStruct((HALO, D), F32), jax.ShapeDtypeStruct((8, 2 * D), F32)],
        scratch_shapes=[pltpu.VMEM((CBUF, D), F32), pltpu.VMEM((CBUF, D), F32), pltpu.VMEM((8, HALO + CT, D), F32),
                        pltpu.VMEM((8, HALO + CT, D), F32), pltpu.VMEM((CONV_W * 8, D), F32), pltpu.VMEM((CT, D), F32),
                        pltpu.VMEM((CONV_W * 8, D), F32)],
        compiler_params=_cp(("arbitrary",)), name="conv_bwd",
    )(du1, du1, z, z, z, z, b_glu, w_dw)


MESH = pl.DeviceIdType.MESH


def _all_gather(name, shards):
    n = len(shards)

    def body(*refs):
        ins, outs = refs[:n], refs[n:2 * n]
        send_sems, recv_sems, local_sems = refs[2 * n:]
        x, y, c = lax.axis_index("x"), lax.axis_index("y"), lax.axis_index("c")
        me, sibling = (x, y, c), (x, y, 1 - c)
        chips = [(1 - x, y), (x, 1 - y), (1 - x, 1 - y)]

        def slot(a, px, py, pc):
            return outs[a].at[4 * px + 2 * py + pc]

        def copy(a, k, block, to, src=None):
            return pltpu.make_async_remote_copy(
                src_ref=slot(a, *block) if src is None else src, dst_ref=slot(a, *block),
                send_sem=send_sems.at[a, k], recv_sem=recv_sems.at[a, k], device_id=to, device_id_type=MESH)

        mine = [pltpu.make_async_copy(ins[a], slot(a, *me), local_sems.at[a]) for a in range(n)]
        for cp in mine:
            cp.start()
        first = []
        for a in range(n):
            first.append(copy(a, 0, me, sibling, src=ins[a]))
            first += [copy(a, 1 + j, me, (*chip, c), src=ins[a]) for j, chip in enumerate(chips)]
        for cp in first:
            cp.start()
        passed = []
        for j, chip in enumerate(chips):
            for a in range(n):
                copy(a, 1 + j, (*chip, c), me).wait_recv()
                fwd = copy(a, 4 + j, (*chip, c), sibling)
                fwd.start()
                passed.append(fwd)
        for a in range(n):
            copy(a, 0, sibling, me).wait_recv()
        for j, chip in enumerate(chips):
            for a in range(n):
                copy(a, 4 + j, (*chip, 1 - c), me).wait_recv()
        for cp in first + passed:
            cp.wait_send()
        for cp in mine:
            cp.wait()

    anyspec = pl.BlockSpec(memory_space=pl.ANY)
    return pl.pallas_call(
        body, in_specs=[anyspec] * n, out_specs=[anyspec] * n,
        out_shape=[jax.ShapeDtypeStruct((NDEV,) + s.shape, s.dtype) for s in shards],
        scratch_shapes=[pltpu.SemaphoreType.DMA((n, 7)), pltpu.SemaphoreType.DMA((n, 7)), pltpu.SemaphoreType.DMA((n,))],
        name=name,
    )(*shards)


def _scatter_blocks(name, blocks):
    n = len(blocks)

    def body(*refs):
        ins, outs = refs[:n], refs[n:2 * n]
        send_sems, recv_sems, local_sems = refs[2 * n:]
        x, y, c = lax.axis_index("x"), lax.axis_index("y"), lax.axis_index("c")
        my = 4 * x + 2 * y + c
        mine = [pltpu.make_async_copy(ins[a].at[my], outs[a].at[my], local_sems.at[a]) for a in range(n)]
        for cp in mine:
            cp.start()
        copies = []
        for k in range(1, NDEV):
            px = 1 - x if k & 4 else x
            py = 1 - y if k & 2 else y
            pc = 1 - c if k & 1 else c
            pid = 4 * px + 2 * py + pc
            for a in range(n):
                cp = pltpu.make_async_remote_copy(
                    src_ref=ins[a].at[pid], dst_ref=outs[a].at[my], send_sem=send_sems.at[a, k - 1],
                    recv_sem=recv_sems.at[a, k - 1], device_id=(px, py, pc), device_id_type=MESH)
                cp.start()
                arrival = pltpu.make_async_remote_copy(
                    src_ref=ins[a].at[pid], dst_ref=outs[a].at[pid], send_sem=send_sems.at[a, k - 1],
                    recv_sem=recv_sems.at[a, k - 1], device_id=(px, py, pc), device_id_type=MESH)
                copies.append((cp, arrival))
        for cp, arrival in copies:
            arrival.wait_recv()
        for cp, arrival in copies:
            cp.wait_send()
        for cp in mine:
            cp.wait()

    anyspec = pl.BlockSpec(memory_space=pl.ANY)
    return pl.pallas_call(
        body, in_specs=[anyspec] * n, out_specs=[anyspec] * n,
        out_shape=[jax.ShapeDtypeStruct(b.shape, b.dtype) for b in blocks],
        scratch_shapes=[pltpu.SemaphoreType.DMA((n, 7)), pltpu.SemaphoreType.DMA((n, 7)), pltpu.SemaphoreType.DMA((n,))],
        name=name,
    )(*blocks)


_C1 = 1.0 - ADAM_B1 ** ADAM_STEP
_C2 = 1.0 - ADAM_B2 ** ADAM_STEP


def _adamw(name, w, m, v, recv, tr):
    R, C = w.shape

    def body(w_ref, m_ref, v_ref, r_ref, g_ref, d_ref, nm_ref, nv_ref):
        g = r_ref[0].astype(F32)
        for s in range(1, NDEV):
            g = g + r_ref[s].astype(F32)
        wv = w_ref[...]
        nm = ADAM_B1 * m_ref[...] + (1.0 - ADAM_B1) * g
        nv = ADAM_B2 * v_ref[...] + (1.0 - ADAM_B2) * (g * g)
        m_hat = nm / _C1
        v_hat = nv / _C2
        g_ref[...] = g
        d_ref[...] = -ADAM_LR * (m_hat / (jnp.sqrt(v_hat) + ADAM_EPS) + ADAM_WD * wv)
        nm_ref[...] = nm
        nv_ref[...] = nv

    blk = pl.BlockSpec((tr, C), lambda i: (i, 0))
    return pl.pallas_call(
        body, grid=(R // tr,), in_specs=[blk, blk, blk, pl.BlockSpec((NDEV, tr, C), lambda i: (0, i, 0))],
        out_specs=[blk] * 4, out_shape=[jax.ShapeDtypeStruct((R, C), F32)] * 4,
        compiler_params=_cp(), name=name,
    )(w, m, v, recv)


def _row(v):
    return v.reshape(1, -1)


def _local_step(xs, tgt, Wp, Wfi, Wfo, Wco, Wmo, Wao, wdw, rel_bias_table, g_pre_mix, b_glu, b_dw, g_conv_ln,
                b_conv_ln, b_conv_out, g_post_mix, g_pre_ffn, g_post_ffn):
    S = xs.shape[0]
    g1, g2, g3, g4 = _row(g_pre_mix), _row(g_post_mix), _row(g_pre_ffn), _row(g_post_ffn)
    bglu, bdw, gln, bln, bco = _row(b_glu), _row(b_dw), _row(g_conv_ln), _row(b_conv_ln), _row(b_conv_out)
    full = (D, F32, D, 0, False)
    fullb = (D, BF, D, 0, False)

    def epi_rms(accs, r, c, o, p):
        v = r[0][...]
        o[0][...] = (v * _rms_r(v) * c[0][...]).astype(BF)

    (h1,) = _fused_mm("rms_in", S, 512, 1, [], [], [], [], [(xs, D, 0, False)], [g1], [fullb], [], epi_rms)

    def epi_cast(accs, r, c, o, p):
        o[0][...] = accs[0].astype(BF)

    (z,) = _fused_mm("in_proj", S, 1024, Z_CB, [(h1, D, 0, 1)], [(Wp, False, D, GW, 0, 1, 0, 0, True)], [(0, 0, 0, 0, 1)],
                     [(1024, GW)], [], [], [(IN_W, BF, GW, 0, True)], [], epi_cast)

    idx = jnp.asarray(_band_index())
    tab_t = rel_bias_table.T.reshape(3, HEADS, REL_BUCKETS)
    bias_all = _bias_build(tab_t, idx)
    bias_qk = [bias_all[g].reshape(HEADS, QBLK, 2 * QBLK) for g in range(3)]
    bias_kq = [bias_all[3 + g].reshape(HEADS, 2 * QBLK, QBLK) for g in range(3)]
    bias_kq2 = [bias_all[6 + g].reshape(HEADS, QBLK, 2 * QBLK) for g in range(3)]
    dils = [d for _, d in GROUPS]
    qkv = [(z, _kvq_blocks(0))] + [(_to_residue(f"qkv_to_residue_g{g}", z, _kvq_blocks(g), dils[g]), (0, 1, 2)) for g in (1, 2)]
    os_, ls_ = [], []
    for g in range(3):
        o_g, l_g = _attn_fwd(f"attn_fwd_g{g}", qkv[g][0], bias_kq[g], qkv[g][1], dils[g])
        os_.append(o_g)
        ls_.append(l_g)
    o_att, lse = _attn_merge(os_, ls_, S)

    u1, u3 = _conv_fwd(z, bglu, wdw, bdw, gln, bln)

    def epi_mix(accs, r, c, o, p):
        ya = accs[0]
        yc = accs[1] + c[0][...]
        mg = _sig(r[0][...].astype(F32)) * ya + _sig(r[1][...].astype(F32)) * yc
        mgb = mg.astype(BF)
        m2 = jnp.dot(mgb, c[1][...], preferred_element_type=F32)
        x1 = r[2][...] + m2 * _rms_r(m2) * c[2][...]
        o[0][...] = ya.astype(BF)
        o[1][...] = yc.astype(BF)
        o[2][...] = mgb
        o[3][...] = m2.astype(BF)
        o[4][...] = x1
        o[5][...] = (x1 * _rms_r(x1) * c[3][...]).astype(BF)

    y_attn, y_conv, merged, m2, x1, h2 = _fused_mm(
        "mix_fwd", S, 512, 1, [(o_att, GW, 0, 1), (u3, D, 0, 1)],
        [(Wao, False, GW, D, 0, 1, 0, 0, False), (Wco, False, D, D, 0, 1, 0, 0, False)], [(0, 0, 0, 0, 1), (1, 1, 1, 0, 1)],
        [(512, D), (512, D)], [(z, D, 2, False), (z, D, 3, False), (xs, D, 0, False)], [bco, Wmo, g2, g3],
        [fullb, fullb, fullb, fullb, full, fullb], [], epi_mix)

    HN = FFN // 2

    def epi_ffn_in(accs, r, c, o, p):
        gt, up = accs
        o[0][...] = gt.astype(BF)
        o[1][...] = up.astype(BF)
        o[2][...] = (gt * _sig(gt) * up).astype(BF)

    gate, up, act = _fused_mm(
        "ffn_in", S, 512, 2, [(h2, D, 0, 1)],
        [(Wfi, False, D, HN, 0, 1, 0, 0, True), (Wfi, False, D, HN, 0, 1, 0, 2, True)], [(0, 0, 0, 0, 1), (0, 1, 1, 0, 1)],
        [(512, HN), (512, HN)], [], [], [(FFN, BF, HN, 0, True)] * 3, [], epi_ffn_in)

    def epi_loss(accs, r, c, o, p):
        f2 = accs[0]
        g = c[0][...]
        rr = _rms_r(f2)
        err = r[0][...] + f2 * rr * g - r[1][...]
        dy = err * (1.0 / D)
        df2, dgr = _rms_bwd(f2, rr, g, dy)
        o[0][...] = dy
        o[1][...] = df2.astype(BF)
        p[0][...] += _psum8(err * err)
        p[1][...] += _psum8(dgr)

    dy, df2, loss_p, dg4 = _fused_mm(
        "ffn_out_loss", S, 512, 1, [(act, FFN, 0, 1)], [(Wfo, False, FFN, D, 0, 1, 0, 0, False)], [(0, 0, 0, 0, 1)],
        [(512, D)], [(x1, D, 0, False), (tgt, D, 0, False)], [g4], [full, fullb], [(8, D), (8, D)], epi_loss)

    def epi_swiglu(accs, r, c, o, p):
        da = accs[0]
        gt = r[0][...].astype(F32)
        sg = _sig(gt)
        o[0][...] = (da * r[1][...].astype(F32) * sg * (1.0 + gt * (1.0 - sg))).astype(BF)
        o[1][...] = (da * gt * sg).astype(BF)

    dgate, dup = _fused_mm(
        "ffn_out_bwd", S, 512, 2, [(df2, D, 0, 1)], [(Wfo, True, D, HN, 0, 1, 0, 0, True)], [(0, 0, 0, 0, 1)],
        [(512, HN)], [(gate, HN, 0, True), (up, HN, 0, True)], [], [(FFN, BF, HN, 0, True)] * 2, [], epi_swiglu)
    dWfo = _mm_tn("dw_ffn_out", act, df2, HN, D, 1024)

    def epi_dh2(accs, r, c, o, p):
        dh2 = accs[0]
        x1v = r[1][...]
        r3 = _rms_r(x1v)
        d1, dg3r = _rms_bwd(x1v, r3, c[0][...], dh2)
        dx1 = r[0][...] + d1
        m2v = r[2][...].astype(F32)
        r2 = _rms_r(m2v)
        dm2, dg2r = _rms_bwd(m2v, r2, c[1][...], dx1)
        o[0][...] = dx1
        o[1][...] = dm2.astype(BF)
        p[0][...] += _psum8(dg3r)
        p[1][...] += _psum8(dg2r)

    dx1, dm2, dg3, dg2 = _fused_mm(
        "ffn_in_bwd", S, 512, 1, [(dgate, HN, 0, 2), (dup, HN, 2, 2)], [(Wfi, True, HN, D, 0, 4, 0, 0, False)],
        [(0, 0, 0, 0, 2), (1, 0, 0, 2, 2)], [(512, D)], [(dy, D, 0, False), (x1, D, 0, False), (m2, D, 0, False)], [g3, g2],
        [full, fullb], [(8, D), (8, D)], epi_dh2)
    dWfi = jnp.concatenate([_mm_tn("dw_ffn_gate", h2, dgate, D, HN, 1024), _mm_tn("dw_ffn_up", h2, dup, D, HN, 1024)], axis=1)

    def epi_dmix(accs, r, c, o, p):
        dm = accs[0]
        sa = _sig(r[0][...].astype(F32))
        sc = _sig(r[1][...].astype(F32))
        o[0][...] = (dm * sa).astype(BF)
        o[1][...] = (dm * sc).astype(BF)
        o[2][:, 0:D] = (dm * r[2][...].astype(F32) * sa * (1.0 - sa)).astype(BF)
        o[2][:, D:2 * D] = (dm * r[3][...].astype(F32) * sc * (1.0 - sc)).astype(BF)

    dy_attn, dy_conv, dz_gate = _fused_mm(
        "mix_bwd", S, 512, 1, [(dm2, D, 0, 1)], [(Wmo, True, D, D, 0, 1, 0, 0, False)], [(0, 0, 0, 0, 1)], [(512, D)],
        [(z, D, 2, False), (z, D, 3, False), (y_attn, D, 0, False), (y_conv, D, 0, False)], [],
        [fullb, fullb, (2 * D, BF, 2 * D, 0, False)], [], epi_dmix)
    dWmo = _mm_tn("dw_mix_out", merged, dm2, D, D, 1024)

    def epi_dconv(accs, r, c, o, p):
        du3 = accs[0]
        xh, rstd = _ln_hat(r[0][...])
        gl = c[0][...]
        u2 = xh * gl + c[1][...]
        sg = _sig(u2)
        du2 = du3 * sg * (1.0 + u2 * (1.0 - sg))
        dxh = du2 * gl
        du1 = rstd * (dxh - jnp.mean(dxh, axis=-1, keepdims=True) - xh * jnp.mean(dxh * xh, axis=-1, keepdims=True))
        o[0][...] = du1
        p[0][...] += _psum8(du2 * xh)
        p[1][...] += _psum8(du2)
        p[2][...] += _psum8(du1)
        p[3][...] += _psum8(r[1][...].astype(F32))

    du1, dgln, dbln, dbdw, dbco = _fused_mm(
        "conv_out_bwd", S, 512, 1, [(dy_conv, D, 0, 1)], [(Wco, True, D, D, 0, 1, 0, 0, False)], [(0, 0, 0, 0, 1)], [(512, D)],
        [(u1, D, 0, False), (dy_conv, D, 0, False)], [gln, bln], [full], [(8, D)] * 4, epi_dconv)
    dWco = _mm_tn("dw_conv_out", u3, dy_conv, D, D, 1024)
    dz_glu, dwdw, dbglu = _conv_bwd(du1, z, bglu, wdw)

    head_sum = np.zeros((GW, STAT_W), np.float32)
    for h in range(HEADS):
        head_sum[h * HEAD_DIM:(h + 1) * HEAD_DIM, HEADS + h] = 1.0
    head_sum = jnp.asarray(head_sum)

    def epi_do(accs, r, c, o, p):
        do = accs[0]
        o[0][...] = do.astype(BF)
        delta = jnp.dot(do * r[0][...].astype(F32), c[0][...], preferred_element_type=F32, precision=lax.Precision.HIGHEST)
        lane = lax.broadcasted_iota(jnp.int32, delta.shape, 1)
        o[1][...] = jnp.where(lane < HEADS, r[1][...], delta)

    do, stats = _fused_mm(
        "attn_out_bwd", S, 1024, 1, [(dy_attn, D, 0, 1)], [(Wao, True, D, GW, 0, 1, 0, 0, False)], [(0, 0, 0, 0, 1)], [(1024, GW)],
        [(o_att, GW, 0, False), (lse, STAT_W, 0, False)], [head_sum], [(GW, BF, GW, 0, False), (STAT_W, F32, STAT_W, 0, False)], [], epi_do)
    dWao = _mm_tn("dw_attn_out", o_att, dy_attn, GW, D, 1024)

    dos = [do] + [_to_residue(f"do_to_residue_g{g}", do, (0,), dils[g]) for g in (1, 2)]
    sts = [stats] + [_to_residue_stats(f"stats_to_residue_g{g}", stats, dils[g]) for g in (1, 2)]
    dqkv, dbs = [], []
    for g in range(3):
        arr, cb = qkv[g]
        dkv = _attn_bwd_dkv(f"attn_bwd_dkv_g{g}", arr, bias_kq2[g], dos[g], sts[g], cb, dils[g])
        dg, db = _attn_bwd_dq(f"attn_bwd_dq_g{g}", arr, bias_qk[g], dos[g], sts[g], dkv, cb, dils[g])
        dqkv.append(dg if g == 0 else _from_residue(f"dqkv_from_residue_g{g}", dg, dils[g]))
        dbs.append(db.reshape(HEADS, _NB))
    dtab = _bias_grad(jnp.stack(dbs), idx)[:, :, :REL_BUCKETS].reshape(3 * HEADS, REL_BUCKETS).T

    def epi_dx(accs, r, c, o, p):
        xv = r[1][...]
        d1, dg1r = _rms_bwd(xv, _rms_r(xv), c[0][...], accs[0])
        o[0][...] = r[0][...] + d1
        p[0][...] += _psum8(dg1r)

    segs = [(dz_glu, 0, 4), (dz_gate, 4, 4), (dqkv[0], 8, 3), (dqkv[1], 11, 3), (dqkv[2], 14, 3)]
    grad_x, dg1 = _fused_mm(
        "in_proj_bwd", S, 1024, 1, [(a, GW, k0, nk) for a, k0, nk in segs], [(Wp, True, GW, D, 0, Z_CB, 0, 0, False)],
        [(t, 0, 0, k0, nk) for t, (a, k0, nk) in enumerate(segs)], [(1024, D)], [(dx1, D, 0, False), (xs, D, 0, False)], [g1],
        [full], [(8, D)], epi_dx)
    dWg = [_mm_tn(f"dw_in_g{g}", h1, dqkv[g], D, ATTN_W, 1024) for g in range(3)]
    dW_in = jnp.concatenate(
        [t[:, 2 * GW:] for t in dWg] + [t[:, :GW] for t in dWg] + [t[:, GW:2 * GW] for t in dWg]
        + [_mm_tn("dw_in_glu", h1, dz_glu, D, D, 1024), _mm_tn("dw_in_gate", h1, dz_gate, D, D, 1024)], axis=1)

    small = dict(rel_bias_table=dtab, g_pre_mix=dg1[0], b_glu=dbglu[0], b_dw=dbdw[0], g_conv_ln=dgln[0], b_conv_ln=dbln[0],
                 b_conv_out=dbco[0], g_post_mix=dg2[0], g_pre_ffn=dg3[0], g_post_ffn=dg4[0])
    big = dict(w_in=dW_in, w_ffn_in=dWfi, w_ffn_out=dWfo, w_conv_out=dWco, w_mix_out=dWmo, w_attn_out=dWao, w_dw=dwdw)
    return loss_p[0], grad_x, big, small


SMALL = ['rel_bias_table', 'g_pre_mix', 'b_glu', 'b_dw', 'g_conv_ln', 'b_conv_ln', 'b_conv_out', 'g_post_mix', 'g_pre_ffn',
         'g_post_ffn']
BIG = ['w_in', 'w_ffn_in', 'w_ffn_out', 'w_conv_out', 'w_mix_out', 'w_attn_out', 'w_dw']
WEIGHTS = ['rel_bias_table', 'g_pre_mix', 'w_in', 'b_glu', 'w_dw', 'b_dw', 'g_conv_ln', 'b_conv_ln', 'w_conv_out', 'b_conv_out',
           'w_attn_out', 'w_mix_out', 'g_post_mix', 'g_pre_ffn', 'w_ffn_in', 'w_ffn_out', 'g_post_ffn']
SMALL_ROWS = 16


def _pack_small(d):
    flat = jnp.concatenate([d[n].reshape(-1).astype(F32) for n in SMALL])
    flat = jnp.pad(flat, (0, SMALL_ROWS * D - flat.shape[0]))
    return flat.reshape(SMALL_ROWS, D)


def _unpack_small(p, like):
    flat = p.reshape(-1)
    out, pos = {}, 0
    for n in SMALL:
        sz = like[n].size
        out[n] = flat[pos:pos + sz].reshape(like[n].shape)
        pos += sz
    return out


def _cols_to_blocks(a):
    R = a.shape[0]
    return a.reshape(R, NDEV, a.shape[1] // NDEV).transpose(1, 0, 2)


def _blocks_to_cols(a):
    return a.transpose(1, 0, 2).reshape(a.shape[1], NDEV * a.shape[2])


def kernel(x, rel_bias_table, g_pre_mix, w_in, b_glu, w_dw, b_dw, g_conv_ln, b_conv_ln, w_conv_out, b_conv_out, w_attn_out, w_mix_out, g_post_mix, g_pre_ffn, w_ffn_in, w_ffn_out, g_post_ffn, loss_target, m_rel_bias_table, m_g_pre_mix, m_w_in, m_b_glu, m_w_dw, m_b_dw, m_g_conv_ln, m_b_conv_ln, m_w_conv_out, m_b_conv_out, m_w_attn_out, m_w_mix_out, m_g_post_mix, m_g_pre_ffn, m_w_ffn_in, m_w_ffn_out, m_g_post_ffn, v_rel_bias_table, v_g_pre_mix, v_w_in, v_b_glu, v_w_dw, v_b_dw, v_g_conv_ln, v_b_conv_ln, v_w_conv_out, v_b_conv_out, v_w_attn_out, v_w_mix_out, v_g_post_mix, v_g_pre_ffn, v_w_ffn_in, v_w_ffn_out, v_g_post_ffn):
    w = dict(rel_bias_table=rel_bias_table, g_pre_mix=g_pre_mix, w_in=w_in, b_glu=b_glu, w_dw=w_dw, b_dw=b_dw, g_conv_ln=g_conv_ln, b_conv_ln=b_conv_ln, w_conv_out=w_conv_out, b_conv_out=b_conv_out, w_attn_out=w_attn_out, w_mix_out=w_mix_out, g_post_mix=g_post_mix, g_pre_ffn=g_pre_ffn, w_ffn_in=w_ffn_in, w_ffn_out=w_ffn_out, g_post_ffn=g_post_ffn)
    m = dict(rel_bias_table=m_rel_bias_table, g_pre_mix=m_g_pre_mix, w_in=m_w_in, b_glu=m_b_glu, w_dw=m_w_dw, b_dw=m_b_dw, g_conv_ln=m_g_conv_ln, b_conv_ln=m_b_conv_ln, w_conv_out=m_w_conv_out, b_conv_out=m_b_conv_out, w_attn_out=m_w_attn_out, w_mix_out=m_w_mix_out, g_post_mix=m_g_post_mix, g_pre_ffn=m_g_pre_ffn, w_ffn_in=m_w_ffn_in, w_ffn_out=m_w_ffn_out, g_post_ffn=m_g_post_ffn)
    v = dict(rel_bias_table=v_rel_bias_table, g_pre_mix=v_g_pre_mix, w_in=v_w_in, b_glu=v_b_glu, w_dw=v_w_dw, b_dw=v_b_dw, g_conv_ln=v_g_conv_ln, b_conv_ln=v_b_conv_ln, w_conv_out=v_w_conv_out, b_conv_out=v_b_conv_out, w_attn_out=v_w_attn_out, w_mix_out=v_w_mix_out, g_post_mix=v_g_post_mix, g_pre_ffn=v_g_pre_ffn, w_ffn_in=v_w_ffn_in, w_ffn_out=v_w_ffn_out, g_post_ffn=v_g_post_ffn)

    def shard2d(d, n):
        a = d[n][0]
        return jnp.pad(a, ((0, HALO - CONV_W), (0, 0))) if n == 'w_dw' else a

    gath = _all_gather("gather_weights", [shard2d(w, n).astype(F32 if n == 'w_dw' else BF) for n in BIG])
    gw = dict(zip(BIG, gath))
    W_in = _blocks_to_cols(gw['w_in'])
    kvq = [W_in[:, t * ATTN_W + g * GW:t * ATTN_W + (g + 1) * GW] for g in range(3) for t in (1, 2, 0)]
    Wp = jnp.concatenate([W_in[:, 3 * ATTN_W:]] + kvq, axis=1)
    Wfi = _blocks_to_cols(gw['w_ffn_in'])
    Wfo = gw['w_ffn_out'].reshape(FFN, D)
    Wco = gw['w_conv_out'].reshape(D, D)
    Wmo = gw['w_mix_out'].reshape(D, D)
    Wao = _blocks_to_cols(gw['w_attn_out'])
    wdw = _blocks_to_cols(gw['w_dw'])

    loss_row, grad_x, big, small = _local_step(
        x[0], loss_target[0], Wp, Wfi, Wfo, Wco, Wmo, Wao, wdw, rel_bias_table, g_pre_mix[0], b_glu[0], b_dw[0], g_conv_ln[0],
        b_conv_ln[0], b_conv_out[0], g_post_mix[0], g_pre_ffn[0], g_post_ffn[0])

    loss = lax.psum(jnp.sum(loss_row) * (0.5 / D), ("x", "y", "c"))

    blocks = [_cols_to_blocks(big['w_in']), _cols_to_blocks(big['w_ffn_in']), big['w_ffn_out'].reshape(NDEV, FFN // NDEV, D),
              big['w_conv_out'].reshape(NDEV, D // NDEV, D), big['w_mix_out'].reshape(NDEV, D // NDEV, D),
              _cols_to_blocks(big['w_attn_out']), _cols_to_blocks(big['w_dw'])]
    recv = dict(zip(BIG, _scatter_blocks("scatter_grads", blocks)))
    tiles = dict(w_in=128, w_ffn_in=256, w_ffn_out=176, w_conv_out=128, w_mix_out=128, w_attn_out=512, w_dw=HALO)
    res = {}
    for n in BIG:
        g_, d_, nm_, nv_ = _adamw("adamw_" + n, shard2d(w, n), shard2d(m, n), shard2d(v, n), recv[n], tiles[n])
        if n == 'w_dw':
            g_, d_, nm_, nv_ = (t[:CONV_W] for t in (g_, d_, nm_, nv_))
        res[n] = tuple(t[None] for t in (g_, d_, nm_, nv_))

    (srecv,) = _all_gather("gather_small_grads", [_pack_small(small)])
    sg, sd, sm, sv = _adamw("adamw_small", _pack_small(w), _pack_small(m), _pack_small(v), srecv, SMALL_ROWS)
    unpacked = [_unpack_small(p, w) for p in (sg, sd, sm, sv)]
    for n in SMALL:
        res[n] = tuple(u[n] for u in unpacked)
    return (loss, grad_x[None], *[res[n][0] for n in WEIGHTS], *[res[n][1] for n in WEIGHTS],
            *[res[n][2] for n in WEIGHTS], *[res[n][3] for n in WEIGHTS])
```

```python
import functools
import math

import numpy as np
import jax
import jax.numpy as jnp
from jax import lax
from jax.experimental import pallas as pl
from jax.experimental.pallas import tpu as pltpu

F32 = jnp.float32
BF = jnp.bfloat16

D = 1024
HEAD_DIM = 64
HEADS = 8
GROUPS = ((128, 1), (512, 4), (2048, 16))
QBLK = 128
GW = HEADS * HEAD_DIM
ATTN_W = 3 * GW
REL_BUCKETS = 32
REL_MAX_DISTANCE = 2048
CONV_W = 31
HALO = 32
FFN = 2816
IN_W = 3 * ATTN_W + 2 * D + 2 * D
RMS_EPS = 1e-6
LN_EPS = 1e-5
NEG_INF = -1e30
SCALE = HEAD_DIM ** -0.5
NDEV = 8

ADAM_LR = 0.001
ADAM_B1 = 0.9
ADAM_B2 = 0.999
ADAM_EPS = 1e-08
ADAM_WD = 0.01
ADAM_STEP = 10

Z_G0 = 4096 // GW
Z_CB = IN_W // GW


def _kvq_blocks(g):
    return (Z_G0 + 3 * g, Z_G0 + 3 * g + 1, Z_G0 + 3 * g + 2)


VMEM_LIMIT = 52 * 1024 * 1024


def _cp(sem=None):
    if sem is None:
        return pltpu.CompilerParams(vmem_limit_bytes=VMEM_LIMIT)
    return pltpu.CompilerParams(vmem_limit_bytes=VMEM_LIMIT, dimension_semantics=sem)


def _sig(v):
    return jax.nn.sigmoid(v)


def _psum8(v):
    return v.reshape(v.shape[0] // 8, 8, v.shape[1]).sum(axis=0)


def _rms_r(v):
    return lax.rsqrt(jnp.mean(v * v, axis=-1, keepdims=True) + RMS_EPS)


def _rms_bwd(v, r, g, dy):
    gy = dy * g
    dv = r * gy - v * (r * r * r) * jnp.mean(v * gy, axis=-1, keepdims=True)
    return dv, dy * v * r


def _clip_k(k, k0, nk):
    return jnp.clip(k - k0, 0, nk - 1)


def _fused_mm(name, M, tm, grid_n, a_ops, b_ops, terms, acc_shapes, rows, consts, outs, parts, epilogue):
    gm = M // tm
    nk_total = max([t[3] + t[4] for t in terms], default=1)
    n_a, n_b, n_r, n_c, n_o, n_p = len(a_ops), len(b_ops), len(rows), len(consts), len(outs), len(parts)
    n_acc = len(acc_shapes)
    use_scratch = nk_total > 1
    if parts:
        assert grid_n == 1

    def jj(j, follow):
        return j if follow else 0

    in_specs, args = [], []
    for (arr, tk, k0, nk) in a_ops:
        in_specs.append(pl.BlockSpec((tm, tk), functools.partial(lambda i, j, k, k0, nk: (i, _clip_k(k, k0, nk)), k0=k0, nk=nk)))
        args.append(arr)
    for (arr, nt, tk, tn, k0, nk, koff, joff, fj) in b_ops:
        if nt:
            in_specs.append(pl.BlockSpec((tn, tk), functools.partial(
                lambda i, j, k, k0, nk, koff, joff, fj: (joff + jj(j, fj), _clip_k(k, k0, nk) + koff),
                k0=k0, nk=nk, koff=koff, joff=joff, fj=fj)))
        else:
            in_specs.append(pl.BlockSpec((tk, tn), functools.partial(
                lambda i, j, k, k0, nk, koff, joff, fj: (_clip_k(k, k0, nk) + koff, joff + jj(j, fj)),
                k0=k0, nk=nk, koff=koff, joff=joff, fj=fj)))
        args.append(arr)
    for (arr, w, off, fj) in rows:
        in_specs.append(pl.BlockSpec((tm, w), functools.partial(lambda i, j, k, off, fj: (i, off + jj(j, fj)), off=off, fj=fj)))
        args.append(arr)
    for arr in consts:
        in_specs.append(pl.BlockSpec(arr.shape, functools.partial(lambda i, j, k, nd: (0,) * nd, nd=arr.ndim)))
        args.append(arr)
    out_specs, out_shape = [], []
    for (ncols, dt, w, off, fj) in outs:
        out_specs.append(pl.BlockSpec((tm, w), functools.partial(lambda i, j, k, off, fj: (i, off + jj(j, fj)), off=off, fj=fj)))
        out_shape.append(jax.ShapeDtypeStruct((M, ncols), dt))
    for (r, c) in parts:
        out_specs.append(pl.BlockSpec((r, c), lambda i, j, k: (0, 0)))
        out_shape.append(jax.ShapeDtypeStruct((r, c), F32))
    scratch = [pltpu.VMEM(s, F32) for s in acc_shapes] if use_scratch else []

    def body(*refs):
        pos = 0
        a_refs = refs[pos:pos + n_a]; pos += n_a
        b_refs = refs[pos:pos + n_b]; pos += n_b
        r_refs = refs[pos:pos + n_r]; pos += n_r
        c_refs = refs[pos:pos + n_c]; pos += n_c
        o_refs = refs[pos:pos + n_o]; pos += n_o
        p_refs = refs[pos:pos + n_p]; pos += n_p
        acc_refs = refs[pos:pos + n_acc] if use_scratch else ()
        i = pl.program_id(0)
        k = pl.program_id(2)

        def dot_of(ai, bi):
            a = a_refs[ai][...].astype(BF)
            b = b_refs[bi][...].astype(BF)
            if b_ops[bi][1]:
                return lax.dot_general(a, b, (((1,), (1,)), ((), ())), preferred_element_type=F32)
            return jnp.dot(a, b, preferred_element_type=F32)

        if parts:
            @pl.when((i == 0) & (k == 0))
            def _():
                for p in p_refs:
                    p[...] = jnp.zeros(p.shape, F32)

        def finish(accs):
            epilogue(accs, r_refs, c_refs, o_refs, p_refs)
            if parts:
                @pl.when(i == gm - 1)
                def _():
                    for p in p_refs:
                        p[0:1, :] = jnp.sum(p[...], axis=0, keepdims=True)

        if not use_scratch:
            accs = [None] * n_acc
            for (ai, bi, ci, k0, nk) in terms:
                d = dot_of(ai, bi)
                accs[ci] = d if accs[ci] is None else accs[ci] + d
            finish(accs)
        else:
            @pl.when(k == 0)
            def _():
                for acc in acc_refs:
                    acc[...] = jnp.zeros(acc.shape, F32)

            for (ai, bi, ci, k0, nk) in terms:
                def do(ai=ai, bi=bi, ci=ci):
                    acc_refs[ci][...] += dot_of(ai, bi)
                if k0 == 0 and nk == nk_total:
                    do()
                else:
                    pl.when((k >= k0) & (k < k0 + nk))(do)

            @pl.when(k == nk_total - 1)
            def _():
                finish([acc[...] for acc in acc_refs])

    res = pl.pallas_call(
        body, grid=(gm, grid_n, nk_total), in_specs=in_specs, out_specs=out_specs, out_shape=out_shape,
        scratch_shapes=scratch, compiler_params=_cp(("arbitrary", "arbitrary", "arbitrary")), name=name,
    )(*args)
    return res


def _mm_tn(name, a, b, tm, tn, tk):
    S, Ka = a.shape
    Nb = b.shape[1]
    nk = S // tk

    def body(a_ref, b_ref, o_ref, acc):
        k = pl.program_id(2)

        @pl.when(k == 0)
        def _():
            acc[...] = jnp.zeros(acc.shape, F32)

        acc[...] += lax.dot_general(a_ref[...], b_ref[...], (((0,), (0,)), ((), ())), preferred_element_type=F32)

        @pl.when(k == nk - 1)
        def _():
            o_ref[...] = acc[...].astype(o_ref.dtype)

    return pl.pallas_call(
        body, grid=(Ka // tm, Nb // tn, nk),
        in_specs=[pl.BlockSpec((tk, tm), lambda i, j, k: (k, i)), pl.BlockSpec((tk, tn), lambda i, j, k: (k, j))],
        out_specs=pl.BlockSpec((tm, tn), lambda i, j, k: (i, j)),
        out_shape=jax.ShapeDtypeStruct((Ka, Nb), BF),
        scratch_shapes=[pltpu.VMEM((tm, tn), F32)],
        compiler_params=_cp(("parallel", "parallel", "arbitrary")), name=name,
    )(a, b)


def _rel_bucket_np(dist):
    max_exact = REL_BUCKETS // 2
    d = np.maximum(dist, 0)
    df = np.maximum(d, 1).astype(np.float32)
    large = max_exact + (np.log(df / np.float32(max_exact)) / np.float32(math.log(REL_MAX_DISTANCE / max_exact))
                         * np.float32(REL_BUCKETS - max_exact)).astype(np.int32)
    large = np.minimum(large, REL_BUCKETS - 1)
    return np.where(d < max_exact, d, large).astype(np.int32)


N_LAYOUTS = 3


def _band_index():
    idx = np.zeros((N_LAYOUTS * 3, 1, QBLK * 2 * QBLK), np.int32)
    for g, (window, dil) in enumerate(GROUPS):
        span = window // dil
        a = np.arange(QBLK)[:, None]; c = np.arange(2 * QBLK)[None, :]
        off = a - c + QBLK
        lay0 = np.where((off >= 0) & (off <= span), _rel_bucket_np(off * dil), -1)
        idx[g, 0] = lay0.reshape(-1)
        idx[3 + g, 0] = lay0.T.reshape(-1)
        k = np.arange(QBLK)[:, None]; q = np.arange(2 * QBLK)[None, :]
        off = q - k
        idx[6 + g, 0] = np.where((off >= 0) & (off <= span), _rel_bucket_np(off * dil), -1).reshape(-1)
    return idx


_NB = QBLK * 2 * QBLK
_BCH = 4096


def _bias_build(tab_t, idx):
    def body(t_ref, i_ref, o_ref):
        ix = i_ref[0]
        t = t_ref[0]
        acc = jnp.full((HEADS, _BCH), NEG_INF, F32)
        for b in range(REL_BUCKETS):
            acc = jnp.where(ix == b, t[:, b:b + 1], acc)
        o_ref[0] = acc

    return pl.pallas_call(
        body, grid=(N_LAYOUTS * 3, _NB // _BCH),
        in_specs=[pl.BlockSpec((1, HEADS, REL_BUCKETS), lambda l, n: (l % 3, 0, 0)),
                  pl.BlockSpec((1, 1, _BCH), lambda l, n: (l, 0, n))],
        out_specs=pl.BlockSpec((1, HEADS, _BCH), lambda l, n: (l, 0, n)),
        out_shape=jax.ShapeDtypeStruct((N_LAYOUTS * 3, HEADS, _NB), F32), compiler_params=_cp(), name="bias_build",
    )(tab_t, idx)


def _bias_grad(ds, idx):
    nch = _NB // _BCH

    def body(d_ref, i_ref, o_ref):
        n = pl.program_id(1)

        @pl.when(n == 0)
        def _():
            o_ref[...] = jnp.zeros(o_ref.shape, F32)

        ix = i_ref[0]
        d = d_ref[0]
        lane = lax.broadcasted_iota(jnp.int32, (HEADS, 128), 1)
        acc = jnp.zeros((HEADS, 128), F32)
        for b in range(REL_BUCKETS):
            s = jnp.sum(jnp.where(ix == b, d, 0.0), axis=1, keepdims=True)
            acc = acc + jnp.where(lane == b, s, 0.0)
        o_ref[0] += acc

    return pl.pallas_call(
        body, grid=(3, nch),
        in_specs=[pl.BlockSpec((1, HEADS, _BCH), lambda l, n: (l, 0, n)),
                  pl.BlockSpec((1, 1, _BCH), lambda l, n: (l, 0, n))],
        out_specs=pl.BlockSpec((1, HEADS, 128), lambda l, n: (l, 0, 0)),
        out_shape=jax.ShapeDtypeStruct((3, HEADS, 128), F32), compiler_params=_cp(), name="bias_grad",
    )(ds, idx)


PT = 256
PSTEP = 1024
STAT_W = 128


def _perm_np(dil):
    p = np.zeros((PT, PT), np.float32)
    m = np.arange(PT // dil)
    for c in range(dil):
        p[c * (PT // dil) + m, m * dil + c] = 1.0
    return p


def _perm_const(dil, dtype, inverse):
    p = _perm_np(dil)
    return jnp.asarray(p.T if inverse else p, dtype)


def _apply_perm(p, x):
    if x.dtype == F32:
        return jnp.dot(p, x, preferred_element_type=F32, precision=lax.Precision.HIGHEST)
    return jnp.dot(p, x, preferred_element_type=F32)


def _to_residue(name, arr, col_blocks, dil):
    S = arr.shape[0]
    nc = len(col_blocks)
    p = _perm_const(dil, arr.dtype, False)
    sub = PT // dil

    def body(*refs):
        p_ref, ins, o_ref = refs[0], refs[1:1 + nc], refs[1 + nc]
        for u in range(PSTEP // PT):
            for t, r in enumerate(ins):
                y = _apply_perm(p_ref[...], r[u * PT:(u + 1) * PT, :]).astype(o_ref.dtype)
                o_ref[:, u * sub:(u + 1) * sub, t * GW:(t + 1) * GW] = y.reshape(dil, sub, GW)

    out = pl.pallas_call(
        body, grid=(S // PSTEP,),
        in_specs=[pl.BlockSpec((PT, PT), lambda i: (0, 0))]
                 + [pl.BlockSpec((PSTEP, GW), functools.partial(lambda i, cb: (i, cb), cb=cb)) for cb in col_blocks],
        out_specs=pl.BlockSpec((dil, PSTEP // dil, nc * GW), lambda i: (0, i, 0)),
        out_shape=jax.ShapeDtypeStruct((dil, S // dil, nc * GW), arr.dtype), compiler_params=_cp(), name=name,
    )(p, *([arr] * nc))
    return out.reshape(S, nc * GW)


def _to_residue_stats(name, arr, dil):
    S = arr.shape[0]
    p = _perm_const(dil, F32, False)
    sub = PT // dil

    def body(p_ref, x_ref, o_ref):
        for u in range(PSTEP // PT):
            y = _apply_perm(p_ref[...], x_ref[u * PT:(u + 1) * PT, :])
            o_ref[:, u * sub:(u + 1) * sub, :] = y.reshape(dil, sub, STAT_W)

    out = pl.pallas_call(
        body, grid=(S // PSTEP,),
        in_specs=[pl.BlockSpec((PT, PT), lambda i: (0, 0)), pl.BlockSpec((PSTEP, STAT_W), lambda i: (i, 0))],
        out_specs=pl.BlockSpec((dil, PSTEP // dil, STAT_W), lambda i: (0, i, 0)),
        out_shape=jax.ShapeDtypeStruct((dil, S // dil, STAT_W), F32), compiler_params=_cp(), name=name,
    )(p, arr)
    return out.reshape(S, STAT_W)


def _from_residue(name, arr, dil):
    S, W = arr.shape
    p = _perm_const(dil, arr.dtype, True)
    sub = PT // dil

    def body(p_ref, x_ref, o_ref):
        for u in range(PSTEP // PT):
            x = x_ref[:, u * sub:(u + 1) * sub, :].reshape(PT, W)
            o_ref[u * PT:(u + 1) * PT, :] = _apply_perm(p_ref[...], x).astype(o_ref.dtype)

    return pl.pallas_call(
        body, grid=(S // PSTEP,),
        in_specs=[pl.BlockSpec((PT, PT), lambda i: (0, 0)), pl.BlockSpec((dil, PSTEP // dil, W), lambda i: (0, i, 0))],
        out_specs=pl.BlockSpec((PSTEP, W), lambda i: (i, 0)),
        out_shape=jax.ShapeDtypeStruct((S, W), arr.dtype), compiler_params=_cp(), name=name,
    )(p, arr.reshape(dil, S // dil, W))


NT_DIMS = (((1,), (1,)), ((), ()))
TN_DIMS = (((0,), (0,)), ((), ()))


def _attn_dims(S, dil):
    L = S // dil
    TQ = min(512, L)
    return L, TQ, L // TQ, TQ // QBLK


def _attn_specs(S, dil):
    L, TQ, nq, nsub = _attn_dims(S, dil)
    nb = L // QBLK
    cur = lambda cb, w=GW: pl.BlockSpec((TQ, w), lambda c, i: (c * nq + i, cb))
    prev = lambda cb, w=GW: pl.BlockSpec((QBLK, w), lambda c, i: (c * nb + jnp.maximum(i * nsub - 1, 0), cb))
    nxt = lambda cb, w=GW: pl.BlockSpec((QBLK, w), lambda c, i: (c * nb + jnp.minimum((i + 1) * nsub, nb - 1), cb))
    band = lambda r, c_: pl.BlockSpec((HEADS, r, c_), lambda c, i: (0, 0, 0))
    return L, TQ, nq, nsub, cur, prev, nxt, band


def _fill(buf, first_ref, second_ref):
    n = first_ref.shape[0]
    buf[0:n, :] = first_ref[...]
    buf[n:n + second_ref.shape[0], :] = second_ref[...]


def _attn_fwd(name, arr, bias_kq, cb, dil):
    S = arr.shape[0]
    kcb, vcb, qcb = cb
    L, TQ, nq, nsub, cur, prev, nxt, band = _attn_specs(S, dil)

    def body(q_ref, kc_ref, kp_ref, vc_ref, vp_ref, b_ref, o_ref, l_ref, kbuf, vbuf):
        i = pl.program_id(1)
        _fill(kbuf, kp_ref, kc_ref)
        _fill(vbuf, vp_ref, vc_ref)
        row = lax.broadcasted_iota(jnp.int32, (2 * QBLK, QBLK), 0)
        first = (row >= QBLK) | (i > 0)
        for j in range(nsub):
            rs = slice(j * QBLK, (j + 1) * QBLK)
            ks = slice(j * QBLK, (j + 2) * QBLK)
            lrows = []
            for h in range(HEADS):
                hs = slice(h * HEAD_DIM, (h + 1) * HEAD_DIM)
                s = lax.dot_general(kbuf[ks, hs], q_ref[rs, hs], NT_DIMS, preferred_element_type=F32) * SCALE + b_ref[h]
                if j == 0:
                    s = jnp.where(first, s, NEG_INF)
                m = jnp.max(s, axis=0, keepdims=True)
                p = jnp.exp(s - m)
                den = jnp.sum(p, axis=0, keepdims=True)
                o_t = lax.dot_general(vbuf[ks, hs], p.astype(BF), TN_DIMS, preferred_element_type=F32) / den
                o_ref[rs, hs] = o_t.T.astype(BF)
                lrows.append(m + jnp.log(den))
            lt = jnp.concatenate(lrows + [jnp.zeros((STAT_W - HEADS, QBLK), F32)], axis=0)
            l_ref[rs, :] = lt.T

    return pl.pallas_call(
        body, grid=(dil, nq),
        in_specs=[cur(qcb), cur(kcb), prev(kcb), cur(vcb), prev(vcb), band(2 * QBLK, QBLK)],
        out_specs=[cur(0), cur(0, STAT_W)],
        out_shape=[jax.ShapeDtypeStruct((S, GW), BF), jax.ShapeDtypeStruct((S, STAT_W), F32)],
        scratch_shapes=[pltpu.VMEM((QBLK + TQ, GW), BF), pltpu.VMEM((QBLK + TQ, GW), BF)],
        compiler_params=_cp(), name=name,
    )(arr, arr, arr, arr, arr, bias_kq)


def _attn_bwd_dkv(name, arr, bias_kq2, do, stats, cb, dil):
    S = arr.shape[0]
    kcb, vcb, qcb = cb
    L, TQ, nq, nsub, cur, prev, nxt, band = _attn_specs(S, dil)

    def body(k_ref, v_ref, qc_ref, qn_ref, b_ref, doc_ref, don_ref, sc_ref, sn_ref, o_ref, qbuf, dobuf, sbuf):
        i = pl.program_id(1)
        _fill(qbuf, qc_ref, qn_ref)
        _fill(dobuf, doc_ref, don_ref)
        for j in range(nsub + 1):
            rs = slice(j * QBLK, (j + 1) * QBLK)
            sbuf[:, rs] = (sc_ref[rs, :] if j < nsub else sn_ref[...]).T
        col = lax.broadcasted_iota(jnp.int32, (QBLK, 2 * QBLK), 1)
        last = (col < QBLK) | (i < nq - 1)
        for j in range(nsub):
            rs = slice(j * QBLK, (j + 1) * QBLK)
            qs = slice(j * QBLK, (j + 2) * QBLK)
            for h in range(HEADS):
                hs = slice(h * HEAD_DIM, (h + 1) * HEAD_DIM)
                qq = qbuf[qs, hs]
                dd = dobuf[qs, hs]
                s = lax.dot_general(k_ref[rs, hs], qq, NT_DIMS, preferred_element_type=F32) * SCALE + b_ref[h]
                if j == nsub - 1:
                    s = jnp.where(last, s, NEG_INF)
                p = jnp.exp(s - sbuf[h:h + 1, qs])
                dp = lax.dot_general(v_ref[rs, hs], dd, NT_DIMS, preferred_element_type=F32)
                ds = p * (dp - sbuf[HEADS + h:HEADS + h + 1, qs])
                o_ref[rs, h * HEAD_DIM:(h + 1) * HEAD_DIM] = (jnp.dot(ds.astype(BF), qq, preferred_element_type=F32) * SCALE).astype(BF)
                o_ref[rs, GW + h * HEAD_DIM:GW + (h + 1) * HEAD_DIM] = jnp.dot(p.astype(BF), dd, preferred_element_type=F32).astype(BF)

    return pl.pallas_call(
        body, grid=(dil, nq),
        in_specs=[cur(kcb), cur(vcb), cur(qcb), nxt(qcb), band(QBLK, 2 * QBLK),
                  cur(0), nxt(0), cur(0, STAT_W), nxt(0, STAT_W)],
        out_specs=cur(0, 2 * GW),
        out_shape=jax.ShapeDtypeStruct((S, ATTN_W), BF),
        scratch_shapes=[pltpu.VMEM((TQ + QBLK, GW), BF), pltpu.VMEM((TQ + QBLK, GW), BF), pltpu.VMEM((STAT_W, TQ + QBLK), F32)],
        compiler_params=_cp(), name=name,
    )(arr, arr, arr, arr, bias_kq2, do, do, stats, stats)


def _attn_bwd_dq(name, arr, bias_qk, do, stats, dkv, cb, dil):
    S = arr.shape[0]
    kcb, vcb, qcb = cb
    L, TQ, nq, nsub, cur, prev, nxt, band = _attn_specs(S, dil)

    def body(q_ref, kc_ref, kp_ref, vc_ref, vp_ref, b_ref, do_ref, st_ref, _alias, dq_ref, db_ref, kbuf, vbuf):
        c = pl.program_id(0)
        i = pl.program_id(1)

        @pl.when((c == 0) & (i == 0))
        def _():
            db_ref[...] = jnp.zeros(db_ref.shape, F32)

        _fill(kbuf, kp_ref, kc_ref)
        _fill(vbuf, vp_ref, vc_ref)
        col = lax.broadcasted_iota(jnp.int32, (QBLK, 2 * QBLK), 1)
        first = (col >= QBLK) | (i > 0)
        for h in range(HEADS):
            hs = slice(h * HEAD_DIM, (h + 1) * HEAD_DIM)
            bias_h = b_ref[h]
            db = jnp.zeros((QBLK, 2 * QBLK), F32)
            for j in range(nsub):
                rs = slice(j * QBLK, (j + 1) * QBLK)
                ks = slice(j * QBLK, (j + 2) * QBLK)
                kk = kbuf[ks, hs]
                s = lax.dot_general(q_ref[rs, hs], kk, NT_DIMS, preferred_element_type=F32) * SCALE + bias_h
                if j == 0:
                    s = jnp.where(first, s, NEG_INF)
                p = jnp.exp(s - st_ref[rs, h:h + 1])
                dp = lax.dot_general(do_ref[rs, hs], vbuf[ks, hs], NT_DIMS, preferred_element_type=F32)
                ds = p * (dp - st_ref[rs, HEADS + h:HEADS + h + 1])
                db = db + ds
                dq_ref[rs, hs] = (jnp.dot(ds.astype(BF), kk, preferred_element_type=F32) * SCALE).astype(BF)
            db_ref[h] += db

    return pl.pallas_call(
        body, grid=(dil, nq),
        in_specs=[cur(qcb), cur(kcb), prev(kcb), cur(vcb), prev(vcb), band(QBLK, 2 * QBLK),
                  cur(0), cur(0, STAT_W), pl.BlockSpec(memory_space=pl.ANY)],
        out_specs=[cur(2), band(QBLK, 2 * QBLK)],
        out_shape=[jax.ShapeDtypeStruct((S, ATTN_W), BF), jax.ShapeDtypeStruct((HEADS, QBLK, 2 * QBLK), F32)],
        scratch_shapes=[pltpu.VMEM((QBLK + TQ, GW), BF), pltpu.VMEM((QBLK + TQ, GW), BF)],
        input_output_aliases={8: 0},
        compiler_params=_cp(("arbitrary", "arbitrary")), name=name,
    )(arr, arr, arr, arr, arr, bias_qk, do, stats, dkv)


def _head_expand():
    e = np.zeros((STAT_W, GW), np.float32)
    for h in range(HEADS):
        e[h, h * HEAD_DIM:(h + 1) * HEAD_DIM] = 1.0
    return e


def _attn_merge(os_, ls_, S):
    dils = [d for _, d in GROUPS]
    pb = [_perm_const(d, BF, True) for d in dils[1:]]
    pf = [_perm_const(d, F32, True) for d in dils[1:]]
    expand = jnp.asarray(_head_expand())

    def body(o0, o1, o2, l0, l1, l2, pb1, pb2, pf1, pf2, e_ref, o_ref, l_ref):
        for u in range(PSTEP // PT):
            rs = slice(u * PT, (u + 1) * PT)
            res = lambda r, d: r[:, u * (PT // d):(u + 1) * (PT // d), :].reshape(PT, r.shape[2])
            ov = [o0[rs, :].astype(F32), _apply_perm(pb1[...], res(o1, dils[1])), _apply_perm(pb2[...], res(o2, dils[2]))]
            lv = [l0[rs, :], _apply_perm(pf1[...], res(l1, dils[1])), _apply_perm(pf2[...], res(l2, dils[2]))]
            m = jnp.maximum(jnp.maximum(lv[0], lv[1]), lv[2])
            ev = [jnp.exp(l - m) for l in lv]
            den = ev[0] + ev[1] + ev[2]
            acc = jnp.zeros((PT, GW), F32)
            for g in range(3):
                wide = jnp.dot(ev[g] / den, e_ref[...], preferred_element_type=F32, precision=lax.Precision.HIGHEST)
                acc = acc + wide * ov[g]
            o_ref[rs, :] = acc.astype(BF)
            l_ref[rs, :] = m + jnp.log(den)

    nat = lambda w: pl.BlockSpec((PSTEP, w), lambda i: (i, 0))
    res = lambda d, w: pl.BlockSpec((d, PSTEP // d, w), lambda i: (0, i, 0))
    cst = lambda a: pl.BlockSpec(a.shape, lambda i: (0, 0))
    args = [os_[0], os_[1].reshape(dils[1], S // dils[1], GW), os_[2].reshape(dils[2], S // dils[2], GW),
            ls_[0], ls_[1].reshape(dils[1], S // dils[1], STAT_W), ls_[2].reshape(dils[2], S // dils[2], STAT_W),
            pb[0], pb[1], pf[0], pf[1], expand]
    return pl.pallas_call(
        body, grid=(S // PSTEP,),
        in_specs=[nat(GW), res(dils[1], GW), res(dils[2], GW), nat(STAT_W), res(dils[1], STAT_W), res(dils[2], STAT_W)]
                 + [cst(a) for a in args[6:]],
        out_specs=[nat(GW), nat(STAT_W)],
        out_shape=[jax.ShapeDtypeStruct((S, GW), BF), jax.ShapeDtypeStruct((S, STAT_W), F32)],
        compiler_params=_cp(), name="attn_merge",
    )(*args)


CT = 256
CBUF = HALO + CT + 8
RG = 4


def _ln_hat(u1):
    mu = jnp.mean(u1, axis=-1, keepdims=True)
    xc = u1 - mu
    rstd = lax.rsqrt(jnp.mean(xc * xc, axis=-1, keepdims=True) + LN_EPS)
    return xc * rstd, rstd


def _glu_window(hu_ref, hg_ref, huh_ref, hgh_ref, bglu_ref, buf_ref, i):
    bu = bglu_ref[:, 0:D]
    bg = bglu_ref[:, D:2 * D]
    uh = (huh_ref[...].astype(F32) + bu) * _sig(hgh_ref[...].astype(F32) + bg)
    buf_ref[0:HALO, :] = jnp.where(i > 0, uh, 0.0)
    a = hu_ref[...].astype(F32) + bu
    s = _sig(hg_ref[...].astype(F32) + bg)
    buf_ref[HALO:HALO + CT, :] = a * s
    buf_ref[HALO + CT:CBUF, :] = jnp.zeros((8, D), F32)
    return a, s


def _shift_copies(buf_ref, sh_ref):
    for r in range(8):
        sh_ref[r] = buf_ref[r:r + HALO + CT, :]


def _tap_rows(wb_ref, w_ref):
    for j in range(CONV_W):
        wb_ref[j * 8:(j + 1) * 8, :] = jnp.broadcast_to(w_ref[j:j + 1, :], (8, D))


def _conv_taps(sh_ref, wb_ref, out_ref, init, offset):
    for rg in range(CT // (8 * RG)):
        accs = [init] * RG
        for j in range(CONV_W):
            off = offset(j)
            wj = wb_ref[j * 8:(j + 1) * 8, :]
            for q in range(RG):
                row = 8 * (rg * RG + q + off // 8)
                accs[q] = accs[q] + wj * sh_ref[off % 8, row:row + 8, :]
        for q in range(RG):
            out_ref[(rg * RG + q) * 8:(rg * RG + q + 1) * 8, :] = accs[q]


def _conv_specs(S):
    cur = lambda cb: pl.BlockSpec((CT, D), lambda i: (i, cb))
    halo = lambda cb: pl.BlockSpec((HALO, D), lambda i: (jnp.maximum(i * (CT // HALO) - 1, 0), cb))
    full = lambda shp: pl.BlockSpec(shp, lambda i: (0, 0))
    return cur, halo, full


def _conv_fwd(z, b_glu, w_dw, b_dw, g_ln, b_ln):
    S = z.shape[0]
    cur, halo, full = _conv_specs(S)

    def body(hu, hg, huh, hgh, bglu, w, bdw, gln, bln, u1_ref, u3_ref, buf, sh, wb):
        i = pl.program_id(0)

        @pl.when(i == 0)
        def _():
            _tap_rows(wb, w)

        _glu_window(hu, hg, huh, hgh, bglu, buf, i)
        _shift_copies(buf, sh)
        _conv_taps(sh, wb, u1_ref, jnp.broadcast_to(bdw[...], (8, D)), lambda j: 2 + j)
        xh, _ = _ln_hat(u1_ref[...])
        u2 = xh * gln[...] + bln[...]
        u3_ref[...] = (u2 * _sig(u2)).astype(BF)

    return pl.pallas_call(
        body, grid=(S // CT,),
        in_specs=[cur(0), cur(1), halo(0), halo(1), full((1, 2 * D)), full((HALO, D)), full((1, D)), full((1, D)), full((1, D))],
        out_specs=[pl.BlockSpec((CT, D), lambda i: (i, 0))] * 2,
        out_shape=[jax.ShapeDtypeStruct((S, D), F32), jax.ShapeDtypeStruct((S, D), BF)],
        scratch_shapes=[pltpu.VMEM((CBUF, D), F32), pltpu.VMEM((8, HALO + CT, D), F32), pltpu.VMEM((CONV_W * 8, D), F32)],
        compiler_params=_cp(("arbitrary",)), name="conv_fwd",
    )(z, z, z, z, b_glu, w_dw, b_dw, g_ln, b_ln)


def _conv_bwd(du1, z, b_glu, w_dw):
    S = z.shape[0]
    n = S // CT
    cur, halo, full = _conv_specs(S)

    def body(du, dun, hu, hg, huh, hgh, bglu, w, dz_ref, dw_ref, dbg_ref, bufu, bufd, shu, shd, wb, du0_ref, dwacc):
        i = pl.program_id(0)

        @pl.when(i == 0)
        def _():
            _tap_rows(wb, w)
            dwacc[...] = jnp.zeros(dwacc.shape, F32)
            dbg_ref[...] = jnp.zeros(dbg_ref.shape, F32)

        a, s = _glu_window(hu, hg, huh, hgh, bglu, bufu, i)
        bufd[0:CT, :] = du[...]
        bufd[CT:CT + HALO, :] = jnp.where(i < n - 1, dun[...], 0.0)
        bufd[CT + HALO:CBUF, :] = jnp.zeros((8, D), F32)
        _shift_copies(bufu, shu)
        _shift_copies(bufd, shd)
        _conv_taps(shd, wb, du0_ref, jnp.zeros((8, D), F32), lambda j: 30 - j)
        for rg in range(CT // (8 * RG)):
            dch = [bufd[(rg * RG + q) * 8:(rg * RG + q + 1) * 8, :] for q in range(RG)]
            for j in range(CONV_W):
                off = 2 + j
                acc = dwacc[j * 8:(j + 1) * 8, :]
                for q in range(RG):
                    row = 8 * (rg * RG + q + off // 8)
                    acc = acc + dch[q] * shu[off % 8, row:row + 8, :]
                dwacc[j * 8:(j + 1) * 8, :] = acc
        du0 = du0_ref[...]
        dhu = du0 * s
        dhg = du0 * a * s * (1.0 - s)
        dz_ref[:, 0:D] = dhu.astype(BF)
        dz_ref[:, D:2 * D] = dhg.astype(BF)
        dbg_ref[:, 0:D] += _psum8(dhu)
        dbg_ref[:, D:2 * D] += _psum8(dhg)

        @pl.when(i == n - 1)
        def _():
            dbg_ref[0:1, :] = jnp.sum(dbg_ref[...], axis=0, keepdims=True)
            for j in range(CONV_W):
                dw_ref[j:j + 1, :] = jnp.sum(dwacc[j * 8:(j + 1) * 8, :], axis=0, keepdims=True)
            dw_ref[CONV_W:HALO, :] = jnp.zeros((HALO - CONV_W, D), F32)

    nxt = pl.BlockSpec((HALO, D), lambda i: (jnp.minimum((i + 1) * (CT // HALO), S // HALO - 1), 0))
    return pl.pallas_call(
        body, grid=(n,),
        in_specs=[pl.BlockSpec((CT, D), lambda i: (i, 0)), nxt, cur(0), cur(1), halo(0), halo(1), full((1, 2 * D)), full((HALO, D))],
        out_specs=[pl.BlockSpec((CT, 2 * D), lambda i: (i, 0)), full((HALO, D)), full((8, 2 * D))],
        out_shape=[jax.ShapeDtypeStruct((S, 2 * D), BF), jax.ShapeDtypeStruct((HALO, D), F32), jax.ShapeDtypeStruct((8, 2 * D), F32)],
        scratch_shapes=[pltpu.VMEM((CBUF, D), F32), pltpu.VMEM((CBUF, D), F32), pltpu.VMEM((8, HALO + CT, D), F32),
                        pltpu.VMEM((8, HALO + CT, D), F32), pltpu.VMEM((CONV_W * 8, D), F32), pltpu.VMEM((CT, D), F32),
                        pltpu.VMEM((CONV_W * 8, D), F32)],
        compiler_params=_cp(("arbitrary",)), name="conv_bwd",
    )(du1, du1, z, z, z, z, b_glu, w_dw)


MESH = pl.DeviceIdType.MESH


def _all_gather(name, shards):
    n = len(shards)

    def body(*refs):
        ins, outs = refs[:n], refs[n:2 * n]
        send_sems, recv_sems, local_sems = refs[2 * n:]
        x, y, c = lax.axis_index("x"), lax.axis_index("y"), lax.axis_index("c")
        me, sibling = (x, y, c), (x, y, 1 - c)
        chips = [(1 - x, y), (x, 1 - y), (1 - x, 1 - y)]

        def slot(a, px, py, pc):
            return outs[a].at[4 * px + 2 * py + pc]

        def copy(a, k, block, to, src=None):
            return pltpu.make_async_remote_copy(
                src_ref=slot(a, *block) if src is None else src, dst_ref=slot(a, *block),
                send_sem=send_sems.at[a, k], recv_sem=recv_sems.at[a, k], device_id=to, device_id_type=MESH)

        mine = [pltpu.make_async_copy(ins[a], slot(a, *me), local_sems.at[a]) for a in range(n)]
        for cp in mine:
            cp.start()
        first = []
        for a in range(n):
            first.append(copy(a, 0, me, sibling, src=ins[a]))
            first += [copy(a, 1 + j, me, (*chip, c), src=ins[a]) for j, chip in enumerate(chips)]
        for cp in first:
            cp.start()
        passed = []
        for j, chip in enumerate(chips):
            for a in range(n):
                copy(a, 1 + j, (*chip, c), me).wait_recv()
                fwd = copy(a, 4 + j, (*chip, c), sibling)
                fwd.start()
                passed.append(fwd)
        for a in range(n):
            copy(a, 0, sibling, me).wait_recv()
        for j, chip in enumerate(chips):
            for a in range(n):
                copy(a, 4 + j, (*chip, 1 - c), me).wait_recv()
        for cp in first + passed:
            cp.wait_send()
        for cp in mine:
            cp.wait()

    anyspec = pl.BlockSpec(memory_space=pl.ANY)
    return pl.pallas_call(
        body, in_specs=[anyspec] * n, out_specs=[anyspec] * n,
        out_shape=[jax.ShapeDtypeStruct((NDEV,) + s.shape, s.dtype) for s in shards],
        scratch_shapes=[pltpu.SemaphoreType.DMA((n, 7)), pltpu.SemaphoreType.DMA((n, 7)), pltpu.SemaphoreType.DMA((n,))],
        name=name,
    )(*shards)


def _scatter_blocks(name, blocks):
    n = len(blocks)

    def body(*refs):
        ins, outs = refs[:n], refs[n:2 * n]
        send_sems, recv_sems, local_sems = refs[2 * n:]
        x, y, c = lax.axis_index("x"), lax.axis_index("y"), lax.axis_index("c")
        my = 4 * x + 2 * y + c
        mine = [pltpu.make_async_copy(ins[a].at[my], outs[a].at[my], local_sems.at[a]) for a in range(n)]
        for cp in mine:
            cp.start()
        copies = []
        for k in range(1, NDEV):
            px = 1 - x if k & 4 else x
            py = 1 - y if k & 2 else y
            pc = 1 - c if k & 1 else c
            pid = 4 * px + 2 * py + pc
            for a in range(n):
                cp = pltpu.make_async_remote_copy(
                    src_ref=ins[a].at[pid], dst_ref=outs[a].at[my], send_sem=send_sems.at[a, k - 1],
                    recv_sem=recv_sems.at[a, k - 1], device_id=(px, py, pc), device_id_type=MESH)
                cp.start()
                arrival = pltpu.make_async_remote_copy(
                    src_ref=ins[a].at[pid], dst_ref=outs[a].at[pid], send_sem=send_sems.at[a, k - 1],
                    recv_sem=recv_sems.at[a, k - 1], device_id=(px, py, pc), device_id_type=MESH)
                copies.append((cp, arrival))
        for cp, arrival in copies:
            arrival.wait_recv()
        for cp, arrival in copies:
            cp.wait_send()
        for cp in mine:
            cp.wait()

    anyspec = pl.BlockSpec(memory_space=pl.ANY)
    return pl.pallas_call(
        body, in_specs=[anyspec] * n, out_specs=[anyspec] * n,
        out_shape=[jax.ShapeDtypeStruct(b.shape, b.dtype) for b in blocks],
        scratch_shapes=[pltpu.SemaphoreType.DMA((n, 7)), pltpu.SemaphoreType.DMA((n, 7)), pltpu.SemaphoreType.DMA((n,))],
        name=name,
    )(*blocks)


HBM_SPEC = pl.BlockSpec(memory_space=pltpu.HBM)
SEM_SPEC = pl.BlockSpec(memory_space=pltpu.SEMAPHORE)
DATAFLOW = pltpu.SideEffectType.DATAFLOW_SIDE_EFFECTING


def _peers():
    x, y, c = lax.axis_index("x"), lax.axis_index("y"), lax.axis_index("c")
    out = []
    for k in range(1, NDEV):
        px = 1 - x if k & 4 else x
        py = 1 - y if k & 2 else y
        pc = 1 - c if k & 1 else c
        out.append(((px, py, pc), 4 * px + 2 * py + pc))
    return 4 * x + 2 * y + c, out


def _exchange_copies(srcs, lands, send_sems, recv_sems, gather):
    my, peers = _peers()
    pairs = []
    for k, (dev, pid) in enumerate(peers):
        for a in range(len(srcs)):
            src = srcs[a] if gather else srcs[a].at[pid]
            sems = dict(send_sem=send_sems[a * (NDEV - 1) + k], recv_sem=recv_sems[a * (NDEV - 1) + k], device_id=dev,
                        device_id_type=MESH)
            pairs.append((pltpu.make_async_remote_copy(src_ref=src, dst_ref=lands[a].at[my], **sems),
                          pltpu.make_async_remote_copy(src_ref=src, dst_ref=lands[a].at[pid], **sems)))
    return pairs


def _exchange_start(name, srcs, gather):
    n = len(srcs)
    ns = n * (NDEV - 1)
    shapes = [(s.shape if gather else s.shape[1:]) for s in srcs]
    lands = [lax.empty((NDEV,) + shp, s.dtype) for shp, s in zip(shapes, srcs)]

    def body(*refs):
        src_refs, land_refs = refs[:n], refs[n:2 * n]
        send_sems, recv_sems = refs[2 * n:2 * n + ns], refs[2 * n + ns:2 * n + 2 * ns]
        token = refs[-1]
        for mine, _ in _exchange_copies(src_refs, land_refs, send_sems, recv_sems, gather):
            mine.start()
        token[...] = jnp.zeros(token.shape, token.dtype)

    hbm = lambda a: pltpu.HBM(a.shape, a.dtype)
    res = pl.pallas_call(
        body, name=name,
        out_shape=(*([pltpu.SemaphoreType.DMA(())] * (2 * ns)), *[hbm(s) for s in srcs], *[hbm(l) for l in lands],
                   jax.ShapeDtypeStruct((8, 128), F32)),
        in_specs=[HBM_SPEC] * (2 * n),
        out_specs=(*([SEM_SPEC] * (2 * ns)), *([HBM_SPEC] * (2 * n)), pl.BlockSpec(memory_space=pltpu.VMEM)),
        input_output_aliases={i: 2 * ns + i for i in range(2 * n)},
        compiler_params=pltpu.CompilerParams(has_side_effects=DATAFLOW),
    )(*[pltpu.with_memory_space_constraint(s, pltpu.HBM) for s in srcs],
      *[pltpu.with_memory_space_constraint(l, pltpu.HBM) for l in lands])
    return list(res[:ns]), list(res[ns:2 * ns]), list(res[2 * ns:2 * ns + n]), list(res[2 * ns + n:2 * ns + 2 * n]), res[-1]


def _exchange_wait(name, handle, after, gather):
    send_sems, recv_sems, srcs, lands, _ = handle
    n = len(srcs)
    ns = n * (NDEV - 1)

    def body(*refs):
        src_refs, land_refs = refs[:n], refs[n:2 * n]
        s_sems, r_sems = refs[2 * n:2 * n + ns], refs[2 * n + ns:2 * n + 2 * ns]
        for mine, theirs in _exchange_copies(src_refs, land_refs, s_sems, r_sems, gather):
            mine.wait_send()
            theirs.wait_recv()

    hbm = lambda a: pltpu.HBM(a.shape, a.dtype)
    res = pl.pallas_call(
        body, name=name,
        out_shape=(*[hbm(s) for s in srcs], *[hbm(l) for l in lands]),
        in_specs=[HBM_SPEC] * (2 * n) + [SEM_SPEC] * (2 * ns) + [pl.BlockSpec(memory_space=pl.ANY)],
        out_specs=tuple([HBM_SPEC] * (2 * n)),
        input_output_aliases={i: i for i in range(2 * n)},
        compiler_params=pltpu.CompilerParams(has_side_effects=DATAFLOW),
    )(*srcs, *lands, *send_sems, *recv_sems, after)
    return list(res[n:])


def _set_own_slot(land, own):
    my = 4 * lax.axis_index("x") + 2 * lax.axis_index("y") + lax.axis_index("c")
    return lax.dynamic_update_slice(land, own[None], (my, 0, 0))


def _own_block(blocks):
    my = 4 * lax.axis_index("x") + 2 * lax.axis_index("y") + lax.axis_index("c")
    return lax.dynamic_index_in_dim(blocks, my, axis=0, keepdims=False)


_C1 = 1.0 - ADAM_B1 ** ADAM_STEP
_C2 = 1.0 - ADAM_B2 ** ADAM_STEP


def _adamw(name, w, m, v, recv, tr):
    R, C = w.shape

    def body(w_ref, m_ref, v_ref, r_ref, g_ref, d_ref, nm_ref, nv_ref):
        g = r_ref[0].astype(F32)
        for s in range(1, NDEV):
            g = g + r_ref[s].astype(F32)
        wv = w_ref[...]
        nm = ADAM_B1 * m_ref[...] + (1.0 - ADAM_B1) * g
        nv = ADAM_B2 * v_ref[...] + (1.0 - ADAM_B2) * (g * g)
        m_hat = nm / _C1
        v_hat = nv / _C2
        g_ref[...] = g
        d_ref[...] = -ADAM_LR * (m_hat / (jnp.sqrt(v_hat) + ADAM_EPS) + ADAM_WD * wv)
        nm_ref[...] = nm
        nv_ref[...] = nv

    blk = pl.BlockSpec((tr, C), lambda i: (i, 0))
    return pl.pallas_call(
        body, grid=(R // tr,), in_specs=[blk, blk, blk, pl.BlockSpec((NDEV, tr, C), lambda i: (0, i, 0))],
        out_specs=[blk] * 4, out_shape=[jax.ShapeDtypeStruct((R, C), F32)] * 4,
        compiler_params=_cp(), name=name,
    )(w, m, v, recv)


def _row(v):
    return v.reshape(1, -1)


def _local_step(xs, tgt, Wp, rest_fn, early_fn, rel_bias_table, g_pre_mix, b_glu, b_dw, g_conv_ln,
                b_conv_ln, b_conv_out, g_post_mix, g_pre_ffn, g_post_ffn):
    S = xs.shape[0]
    g1, g2, g3, g4 = _row(g_pre_mix), _row(g_post_mix), _row(g_pre_ffn), _row(g_post_ffn)
    bglu, bdw, gln, bln, bco = _row(b_glu), _row(b_dw), _row(g_conv_ln), _row(b_conv_ln), _row(b_conv_out)
    full = (D, F32, D, 0, False)
    fullb = (D, BF, D, 0, False)

    def epi_rms(accs, r, c, o, p):
        v = r[0][...]
        o[0][...] = (v * _rms_r(v) * c[0][...]).astype(BF)

    (h1,) = _fused_mm("rms_in", S, 512, 1, [], [], [], [], [(xs, D, 0, False)], [g1], [fullb], [], epi_rms)

    def epi_cast(accs, r, c, o, p):
        o[0][...] = accs[0].astype(BF)

    ZT = IN_W // 4
    (z,) = _fused_mm("in_proj", S, 1024, 4, [(h1, D, 0, 1)], [(Wp, False, D, ZT, 0, 1, 0, 0, True)], [(0, 0, 0, 0, 1)],
                     [(1024, ZT)], [], [], [(IN_W, BF, ZT, 0, True)], [], epi_cast)

    idx = jnp.asarray(_band_index())
    tab_t = rel_bias_table.T.reshape(3, HEADS, REL_BUCKETS)
    bias_all = _bias_build(tab_t, idx)
    bias_qk = [bias_all[g].reshape(HEADS, QBLK, 2 * QBLK) for g in range(3)]
    bias_kq = [bias_all[3 + g].reshape(HEADS, 2 * QBLK, QBLK) for g in range(3)]
    bias_kq2 = [bias_all[6 + g].reshape(HEADS, QBLK, 2 * QBLK) for g in range(3)]
    dils = [d for _, d in GROUPS]
    qkv = [(z, _kvq_blocks(0))] + [(_to_residue(f"qkv_to_residue_g{g}", z, _kvq_blocks(g), dils[g]), (0, 1, 2)) for g in (1, 2)]
    os_, ls_ = [], []
    for g in range(3):
        o_g, l_g = _attn_fwd(f"attn_fwd_g{g}", qkv[g][0], bias_kq[g], qkv[g][1], dils[g])
        os_.append(o_g)
        ls_.append(l_g)
    o_att, lse = _attn_merge(os_, ls_, S)

    Wfi, Wfo, Wco, Wmo, Wao, wdw = rest_fn(lse)
    u1, u3 = _conv_fwd(z, bglu, wdw, bdw, gln, bln)

    def epi_mix(accs, r, c, o, p):
        ya = accs[0]
        yc = accs[1] + c[0][...]
        mg = _sig(r[0][...].astype(F32)) * ya + _sig(r[1][...].astype(F32)) * yc
        mgb = mg.astype(BF)
        m2 = jnp.dot(mgb, c[1][...], preferred_element_type=F32)
        x1 = r[2][...] + m2 * _rms_r(m2) * c[2][...]
        o[0][...] = ya.astype(BF)
        o[1][...] = yc.astype(BF)
        o[2][...] = mgb
        o[3][...] = m2.astype(BF)
        o[4][...] = x1
        o[5][...] = (x1 * _rms_r(x1) * c[3][...]).astype(BF)

    y_attn, y_conv, merged, m2, x1, h2 = _fused_mm(
        "mix_fwd", S, 512, 1, [(o_att, GW, 0, 1), (u3, D, 0, 1)],
        [(Wao, False, GW, D, 0, 1, 0, 0, False), (Wco, False, D, D, 0, 1, 0, 0, False)], [(0, 0, 0, 0, 1), (1, 1, 1, 0, 1)],
        [(512, D), (512, D)], [(z, D, 2, False), (z, D, 3, False), (xs, D, 0, False)], [bco, Wmo, g2, g3],
        [fullb, fullb, fullb, fullb, full, fullb], [], epi_mix)

    HN = FFN // 2

    def epi_ffn_in(accs, r, c, o, p):
        gt, up = accs
        o[0][...] = gt.astype(BF)
        o[1][...] = up.astype(BF)
        o[2][...] = (gt * _sig(gt) * up).astype(BF)

    gate, up, act = _fused_mm(
        "ffn_in", S, 512, 2, [(h2, D, 0, 1)],
        [(Wfi, False, D, HN, 0, 1, 0, 0, True), (Wfi, False, D, HN, 0, 1, 0, 2, True)], [(0, 0, 0, 0, 1), (0, 1, 1, 0, 1)],
        [(512, HN), (512, HN)], [], [], [(FFN, BF, HN, 0, True)] * 3, [], epi_ffn_in)

    def epi_loss(accs, r, c, o, p):
        f2 = accs[0]
        g = c[0][...]
        rr = _rms_r(f2)
        err = r[0][...] + f2 * rr * g - r[1][...]
        dy = err * (1.0 / D)
        df2, dgr = _rms_bwd(f2, rr, g, dy)
        o[0][...] = dy
        o[1][...] = df2.astype(BF)
        p[0][...] += _psum8(err * err)
        p[1][...] += _psum8(dgr)

    dy, df2, loss_p, dg4 = _fused_mm(
        "ffn_out_loss", S, 512, 1, [(act, FFN, 0, 1)], [(Wfo, False, FFN, D, 0, 1, 0, 0, False)], [(0, 0, 0, 0, 1)],
        [(512, D)], [(x1, D, 0, False), (tgt, D, 0, False)], [g4], [full, fullb], [(8, D), (8, D)], epi_loss)

    def epi_swiglu(accs, r, c, o, p):
        da = accs[0]
        gt = r[0][...].astype(F32)
        sg = _sig(gt)
        o[0][...] = (da * r[1][...].astype(F32) * sg * (1.0 + gt * (1.0 - sg))).astype(BF)
        o[1][...] = (da * gt * sg).astype(BF)

    dgate, dup = _fused_mm(
        "ffn_out_bwd", S, 512, 2, [(df2, D, 0, 1)], [(Wfo, True, D, HN, 0, 1, 0, 0, True)], [(0, 0, 0, 0, 1)],
        [(512, HN)], [(gate, HN, 0, True), (up, HN, 0, True)], [], [(FFN, BF, HN, 0, True)] * 2, [], epi_swiglu)
    dWfo = _mm_tn("dw_ffn_out", act, df2, HN, D, 1024)

    def epi_dh2(accs, r, c, o, p):
        dh2 = accs[0]
        x1v = r[1][...]
        r3 = _rms_r(x1v)
        d1, dg3r = _rms_bwd(x1v, r3, c[0][...], dh2)
        dx1 = r[0][...] + d1
        m2v = r[2][...].astype(F32)
        r2 = _rms_r(m2v)
        dm2, dg2r = _rms_bwd(m2v, r2, c[1][...], dx1)
        o[0][...] = dx1
        o[1][...] = dm2.astype(BF)
        p[0][...] += _psum8(dg3r)
        p[1][...] += _psum8(dg2r)

    dx1, dm2, dg3, dg2 = _fused_mm(
        "ffn_in_bwd", S, 512, 1, [(dgate, HN, 0, 2), (dup, HN, 2, 2)], [(Wfi, True, HN, D, 0, 4, 0, 0, False)],
        [(0, 0, 0, 0, 2), (1, 0, 0, 2, 2)], [(512, D)], [(dy, D, 0, False), (x1, D, 0, False), (m2, D, 0, False)], [g3, g2],
        [full, fullb], [(8, D), (8, D)], epi_dh2)
    dWfi = jnp.concatenate([_mm_tn("dw_ffn_gate", h2, dgate, D, HN, 1024), _mm_tn("dw_ffn_up", h2, dup, D, HN, 1024)], axis=1)

    def epi_dmix(accs, r, c, o, p):
        dm = accs[0]
        sa = _sig(r[0][...].astype(F32))
        sc = _sig(r[1][...].astype(F32))
        o[0][...] = (dm * sa).astype(BF)
        o[1][...] = (dm * sc).astype(BF)
        o[2][:, 0:D] = (dm * r[2][...].astype(F32) * sa * (1.0 - sa)).astype(BF)
        o[2][:, D:2 * D] = (dm * r[3][...].astype(F32) * sc * (1.0 - sc)).astype(BF)

    dy_attn, dy_conv, dz_gate = _fused_mm(
        "mix_bwd", S, 512, 1, [(dm2, D, 0, 1)], [(Wmo, True, D, D, 0, 1, 0, 0, False)], [(0, 0, 0, 0, 1)], [(512, D)],
        [(z, D, 2, False), (z, D, 3, False), (y_attn, D, 0, False), (y_conv, D, 0, False)], [],
        [fullb, fullb, (2 * D, BF, 2 * D, 0, False)], [], epi_dmix)
    dWmo = _mm_tn("dw_mix_out", merged, dm2, D, D, 1024)

    def epi_dconv(accs, r, c, o, p):
        du3 = accs[0]
        xh, rstd = _ln_hat(r[0][...])
        gl = c[0][...]
        u2 = xh * gl + c[1][...]
        sg = _sig(u2)
        du2 = du3 * sg * (1.0 + u2 * (1.0 - sg))
        dxh = du2 * gl
        du1 = rstd * (dxh - jnp.mean(dxh, axis=-1, keepdims=True) - xh * jnp.mean(dxh * xh, axis=-1, keepdims=True))
        o[0][...] = du1
        p[0][...] += _psum8(du2 * xh)
        p[1][...] += _psum8(du2)
        p[2][...] += _psum8(du1)
        p[3][...] += _psum8(r[1][...].astype(F32))

    du1, dgln, dbln, dbdw, dbco = _fused_mm(
        "conv_out_bwd", S, 512, 1, [(dy_conv, D, 0, 1)], [(Wco, True, D, D, 0, 1, 0, 0, False)], [(0, 0, 0, 0, 1)], [(512, D)],
        [(u1, D, 0, False), (dy_conv, D, 0, False)], [gln, bln], [full], [(8, D)] * 4, epi_dconv)
    dWco = _mm_tn("dw_conv_out", u3, dy_conv, D, D, 1024)
    dz_glu, dwdw, dbglu = _conv_bwd(du1, z, bglu, wdw)

    head_sum = np.zeros((GW, STAT_W), np.float32)
    for h in range(HEADS):
        head_sum[h * HEAD_DIM:(h + 1) * HEAD_DIM, HEADS + h] = 1.0
    head_sum = jnp.asarray(head_sum)

    def epi_do(accs, r, c, o, p):
        do = accs[0]
        o[0][...] = do.astype(BF)
        delta = jnp.dot(do * r[0][...].astype(F32), c[0][...], preferred_element_type=F32, precision=lax.Precision.HIGHEST)
        lane = lax.broadcasted_iota(jnp.int32, delta.shape, 1)
        o[1][...] = jnp.where(lane < HEADS, r[1][...], delta)

    do, stats = _fused_mm(
        "attn_out_bwd", S, 1024, 1, [(dy_attn, D, 0, 1)], [(Wao, True, D, GW, 0, 1, 0, 0, False)], [(0, 0, 0, 0, 1)], [(1024, GW)],
        [(o_att, GW, 0, False), (lse, STAT_W, 0, False)], [head_sum], [(GW, BF, GW, 0, False), (STAT_W, F32, STAT_W, 0, False)], [], epi_do)
    dWao = _mm_tn("dw_attn_out", o_att, dy_attn, GW, D, 1024)

    tie = early_fn(dict(w_ffn_in=dWfi, w_ffn_out=dWfo, w_conv_out=dWco, w_mix_out=dWmo, w_attn_out=dWao, w_dw=dwdw))
    stats = stats + tie
    dos = [do] + [_to_residue(f"do_to_residue_g{g}", do, (0,), dils[g]) for g in (1, 2)]
    sts = [stats] + [_to_residue_stats(f"stats_to_residue_g{g}", stats, dils[g]) for g in (1, 2)]
    dqkv, dbs = [], []
    for g in range(3):
        arr, cb = qkv[g]
        dkv = _attn_bwd_dkv(f"attn_bwd_dkv_g{g}", arr, bias_kq2[g], dos[g], sts[g], cb, dils[g])
        dg, db = _attn_bwd_dq(f"attn_bwd_dq_g{g}", arr, bias_qk[g], dos[g], sts[g], dkv, cb, dils[g])
        dqkv.append(dg if g == 0 else _from_residue(f"dqkv_from_residue_g{g}", dg, dils[g]))
        dbs.append(db.reshape(HEADS, _NB))
    dtab = _bias_grad(jnp.stack(dbs), idx)[:, :, :REL_BUCKETS].reshape(3 * HEADS, REL_BUCKETS).T

    def epi_dx(accs, r, c, o, p):
        xv = r[1][...]
        d1, dg1r = _rms_bwd(xv, _rms_r(xv), c[0][...], accs[0])
        o[0][...] = r[0][...] + d1
        p[0][...] += _psum8(dg1r)

    segs = [(dz_glu, 0, 4), (dz_gate, 4, 4), (dqkv[0], 8, 3), (dqkv[1], 11, 3), (dqkv[2], 14, 3)]
    grad_x, dg1 = _fused_mm(
        "in_proj_bwd", S, 1024, 1, [(a, GW, k0, nk) for a, k0, nk in segs], [(Wp, True, GW, D, 0, Z_CB, 0, 0, False)],
        [(t, 0, 0, k0, nk) for t, (a, k0, nk) in enumerate(segs)], [(1024, D)], [(dx1, D, 0, False), (xs, D, 0, False)], [g1],
        [full], [(8, D)], epi_dx)
    dWg = [_mm_tn(f"dw_in_g{g}", h1, dqkv[g], D, ATTN_W, 1024) for g in range(3)]
    dW_in = jnp.concatenate(
        [t[:, 2 * GW:] for t in dWg] + [t[:, :GW] for t in dWg] + [t[:, GW:2 * GW] for t in dWg]
        + [_mm_tn("dw_in_glu", h1, dz_glu, D, D, 1024), _mm_tn("dw_in_gate", h1, dz_gate, D, D, 1024)], axis=1)

    small = dict(rel_bias_table=dtab, g_pre_mix=dg1[0], b_glu=dbglu[0], b_dw=dbdw[0], g_conv_ln=dgln[0], b_conv_ln=dbln[0],
                 b_conv_out=dbco[0], g_post_mix=dg2[0], g_pre_ffn=dg3[0], g_post_ffn=dg4[0])
    return loss_p[0], grad_x, dW_in, small


SMALL = ['rel_bias_table', 'g_pre_mix', 'b_glu', 'b_dw', 'g_conv_ln', 'b_conv_ln', 'b_conv_out', 'g_post_mix', 'g_pre_ffn',
         'g_post_ffn']
BIG = ['w_in', 'w_ffn_in', 'w_ffn_out', 'w_conv_out', 'w_mix_out', 'w_attn_out', 'w_dw']
WEIGHTS = ['rel_bias_table', 'g_pre_mix', 'w_in', 'b_glu', 'w_dw', 'b_dw', 'g_conv_ln', 'b_conv_ln', 'w_conv_out', 'b_conv_out',
           'w_attn_out', 'w_mix_out', 'g_post_mix', 'g_pre_ffn', 'w_ffn_in', 'w_ffn_out', 'g_post_ffn']
SMALL_ROWS = 16


def _pack_small(d):
    flat = jnp.concatenate([d[n].reshape(-1).astype(F32) for n in SMALL])
    flat = jnp.pad(flat, (0, SMALL_ROWS * D - flat.shape[0]))
    return flat.reshape(SMALL_ROWS, D)


def _unpack_small(p, like):
    flat = p.reshape(-1)
    out, pos = {}, 0
    for n in SMALL:
        sz = like[n].size
        out[n] = flat[pos:pos + sz].reshape(like[n].shape)
        pos += sz
    return out


def _cols_to_blocks(a):
    R = a.shape[0]
    return a.reshape(R, NDEV, a.shape[1] // NDEV).transpose(1, 0, 2)


def _blocks_to_cols(a):
    return a.transpose(1, 0, 2).reshape(a.shape[1], NDEV * a.shape[2])


def kernel(x, rel_bias_table, g_pre_mix, w_in, b_glu, w_dw, b_dw, g_conv_ln, b_conv_ln, w_conv_out, b_conv_out, w_attn_out, w_mix_out, g_post_mix, g_pre_ffn, w_ffn_in, w_ffn_out, g_post_ffn, loss_target, m_rel_bias_table, m_g_pre_mix, m_w_in, m_b_glu, m_w_dw, m_b_dw, m_g_conv_ln, m_b_conv_ln, m_w_conv_out, m_b_conv_out, m_w_attn_out, m_w_mix_out, m_g_post_mix, m_g_pre_ffn, m_w_ffn_in, m_w_ffn_out, m_g_post_ffn, v_rel_bias_table, v_g_pre_mix, v_w_in, v_b_glu, v_w_dw, v_b_dw, v_g_conv_ln, v_b_conv_ln, v_w_conv_out, v_b_conv_out, v_w_attn_out, v_w_mix_out, v_g_post_mix, v_g_pre_ffn, v_w_ffn_in, v_w_ffn_out, v_g_post_ffn):
    w = dict(rel_bias_table=rel_bias_table, g_pre_mix=g_pre_mix, w_in=w_in, b_glu=b_glu, w_dw=w_dw, b_dw=b_dw, g_conv_ln=g_conv_ln, b_conv_ln=b_conv_ln, w_conv_out=w_conv_out, b_conv_out=b_conv_out, w_attn_out=w_attn_out, w_mix_out=w_mix_out, g_post_mix=g_post_mix, g_pre_ffn=g_pre_ffn, w_ffn_in=w_ffn_in, w_ffn_out=w_ffn_out, g_post_ffn=g_post_ffn)
    m = dict(rel_bias_table=m_rel_bias_table, g_pre_mix=m_g_pre_mix, w_in=m_w_in, b_glu=m_b_glu, w_dw=m_w_dw, b_dw=m_b_dw, g_conv_ln=m_g_conv_ln, b_conv_ln=m_b_conv_ln, w_conv_out=m_w_conv_out, b_conv_out=m_b_conv_out, w_attn_out=m_w_attn_out, w_mix_out=m_w_mix_out, g_post_mix=m_g_post_mix, g_pre_ffn=m_g_pre_ffn, w_ffn_in=m_w_ffn_in, w_ffn_out=m_w_ffn_out, g_post_ffn=m_g_post_ffn)
    v = dict(rel_bias_table=v_rel_bias_table, g_pre_mix=v_g_pre_mix, w_in=v_w_in, b_glu=v_b_glu, w_dw=v_w_dw, b_dw=v_b_dw, g_conv_ln=v_g_conv_ln, b_conv_ln=v_b_conv_ln, w_conv_out=v_w_conv_out, b_conv_out=v_b_conv_out, w_attn_out=v_w_attn_out, w_mix_out=v_w_mix_out, g_post_mix=v_g_post_mix, g_pre_ffn=v_g_pre_ffn, w_ffn_in=v_w_ffn_in, w_ffn_out=v_w_ffn_out, g_post_ffn=v_g_post_ffn)

    def shard2d(d, n):
        a = d[n][0]
        return jnp.pad(a, ((0, HALO - CONV_W), (0, 0))) if n == 'w_dw' else a

    own = {n: shard2d(w, n).astype(F32 if n == 'w_dw' else BF) for n in BIG}
    packed = [n for n in BIG if n not in ('w_in', 'w_dw')]
    shapes = [own[n].shape for n in packed]

    def pack(arrs, lead):
        return jnp.concatenate([a.reshape(lead + (-1, D)) for a in arrs], axis=len(lead))

    def unpack(p):
        out, pos = {}, 0
        for n, shp in zip(packed, shapes):
            rows = shp[0] * shp[1] // D
            out[n] = p[:, pos:pos + rows].reshape((NDEV,) + shp)
            pos += rows
        return out

    (g_in,) = _all_gather("gather_w_in", [own['w_in']])
    rest_own = [pack([own[n] for n in packed], ()), own['w_dw']]
    g_in, rest_own = lax.optimization_barrier((g_in, rest_own))
    gather_rest = _exchange_start("gather_rest_start", rest_own, True)
    W_in = _blocks_to_cols(g_in)
    kvq = [W_in[:, t * ATTN_W + g * GW:t * ATTN_W + (g + 1) * GW] for g in range(3) for t in (1, 2, 0)]
    Wp = jnp.concatenate([W_in[:, 3 * ATTN_W:]] + kvq, axis=1)

    def rest_fn(after):
        lands = _exchange_wait("gather_rest_wait", gather_rest, after, True)
        gw = unpack(_set_own_slot(lands[0], rest_own[0]))
        gw['w_dw'] = _set_own_slot(lands[1], rest_own[1])
        return (_blocks_to_cols(gw['w_ffn_in']), gw['w_ffn_out'].reshape(FFN, D), gw['w_conv_out'].reshape(D, D),
                gw['w_mix_out'].reshape(D, D), _blocks_to_cols(gw['w_attn_out']), _blocks_to_cols(gw['w_dw']))

    def to_blocks(n, g):
        if n in ('w_in', 'w_ffn_in', 'w_attn_out', 'w_dw'):
            return _cols_to_blocks(g)
        return g.reshape(NDEV, g.shape[0] // NDEV, g.shape[1])

    started = {}

    def early_fn(grads):
        blocks = [pack([to_blocks(n, grads[n]) for n in packed], (NDEV,)), to_blocks('w_dw', grads['w_dw'])]
        started['blocks'] = blocks
        started['handle'] = _exchange_start("scatter_early_start", blocks, False)
        return started['handle'][4][0:1, 0:1]

    g1_tied = g_pre_mix[0] + gather_rest[4][0, 0:1]
    loss_row, grad_x, dW_in, small = _local_step(
        x[0], loss_target[0], Wp, rest_fn, early_fn, rel_bias_table, g1_tied, b_glu[0], b_dw[0], g_conv_ln[0],
        b_conv_ln[0], b_conv_out[0], g_post_mix[0], g_pre_ffn[0], g_post_ffn[0])

    loss = lax.psum(jnp.sum(loss_row) * (0.5 / D), ("x", "y", "c"))

    in_blocks = to_blocks('w_in', dW_in)
    lands = _exchange_wait("scatter_early_wait", started['handle'], in_blocks, False)
    lands = [_set_own_slot(l, _own_block(b)) for l, b in zip(lands, started['blocks'])]
    recv = unpack(lands[0])
    recv['w_dw'] = lands[1]
    (recv['w_in'],) = _scatter_blocks("scatter_w_in", [in_blocks])
    tiles = dict(w_in=128, w_ffn_in=256, w_ffn_out=176, w_conv_out=128, w_mix_out=128, w_attn_out=512, w_dw=HALO)
    res = {}
    for n in BIG:
        g_, d_, nm_, nv_ = _adamw("adamw_" + n, shard2d(w, n), shard2d(m, n), shard2d(v, n), recv[n], tiles[n])
        if n == 'w_dw':
            g_, d_, nm_, nv_ = (t[:CONV_W] for t in (g_, d_, nm_, nv_))
        res[n] = tuple(t[None] for t in (g_, d_, nm_, nv_))

    (srecv,) = _all_gather("gather_small_grads", [_pack_small(small)])
    sg, sd, sm, sv = _adamw("adamw_small", _pack_small(w), _pack_small(m), _pack_small(v), srecv, SMALL_ROWS)
    unpacked = [_unpack_small(p, w) for p in (sg, sd, sm, sv)]
    for n in SMALL:
        res[n] = tuple(u[n] for u in unpacked)
    return (loss, grad_x[None], *[res[n][0] for n in WEIGHTS], *[res[n][1] for n in WEIGHTS],
            *[res[n][2] for n in WEIGHTS], *[res[n][3] for n in WEIGHTS])
```

```python
import functools
import math

import numpy as np
import jax
import jax.numpy as jnp
from jax import lax
from jax.experimental import pallas as pl
from jax.experimental.pallas import tpu as pltpu

F32 = jnp.float32
BF = jnp.bfloat16

D = 1024
HEAD_DIM = 64
HEADS = 8
GROUPS = ((128, 1), (512, 4), (2048, 16))
QBLK = 128
GW = HEADS * HEAD_DIM
ATTN_W = 3 * GW
REL_BUCKETS = 32
REL_MAX_DISTANCE = 2048
CONV_W = 31
HALO = 32
FFN = 2816
IN_W = 3 * ATTN_W + 2 * D + 2 * D
RMS_EPS = 1e-6
LN_EPS = 1e-5
NEG_INF = -1e30
SCALE = HEAD_DIM ** -0.5
NDEV = 8

ADAM_LR = 0.001
ADAM_B1 = 0.9
ADAM_B2 = 0.999
ADAM_EPS = 1e-08
ADAM_WD = 0.01
ADAM_STEP = 10

Z_G0 = 4096 // GW
Z_CB = IN_W // GW


def _kvq_blocks(g):
    return (Z_G0 + 3 * g, Z_G0 + 3 * g + 1, Z_G0 + 3 * g + 2)


VMEM_LIMIT = 52 * 1024 * 1024


def _cp(sem=None):
    if sem is None:
        return pltpu.CompilerParams(vmem_limit_bytes=VMEM_LIMIT)
    return pltpu.CompilerParams(vmem_limit_bytes=VMEM_LIMIT, dimension_semantics=sem)


def _sig(v):
    return jax.nn.sigmoid(v)


def _psum8(v):
    return v.reshape(v.shape[0] // 8, 8, v.shape[1]).sum(axis=0)


def _rms_r(v):
    return lax.rsqrt(jnp.mean(v * v, axis=-1, keepdims=True) + RMS_EPS)


def _rms_bwd(v, r, g, dy):
    gy = dy * g
    dv = r * gy - v * (r * r * r) * jnp.mean(v * gy, axis=-1, keepdims=True)
    return dv, dy * v * r


def _clip_k(k, k0, nk):
    return jnp.clip(k - k0, 0, nk - 1)


def _fused_mm(name, M, tm, grid_n, a_ops, b_ops, terms, acc_shapes, rows, consts, outs, parts, epilogue):
    gm = M // tm
    nk_total = max([t[3] + t[4] for t in terms], default=1)
    n_a, n_b, n_r, n_c, n_o, n_p = len(a_ops), len(b_ops), len(rows), len(consts), len(outs), len(parts)
    n_acc = len(acc_shapes)
    use_scratch = nk_total > 1
    if parts:
        assert grid_n == 1

    def jj(j, follow):
        return j if follow else 0

    in_specs, args = [], []
    for (arr, tk, k0, nk) in a_ops:
        in_specs.append(pl.BlockSpec((tm, tk), functools.partial(lambda i, j, k, k0, nk: (i, _clip_k(k, k0, nk)), k0=k0, nk=nk)))
        args.append(arr)
    for (arr, nt, tk, tn, k0, nk, koff, joff, fj) in b_ops:
        if nt:
            in_specs.append(pl.BlockSpec((tn, tk), functools.partial(
                lambda i, j, k, k0, nk, koff, joff, fj: (joff + jj(j, fj), _clip_k(k, k0, nk) + koff),
                k0=k0, nk=nk, koff=koff, joff=joff, fj=fj)))
        else:
            in_specs.append(pl.BlockSpec((tk, tn), functools.partial(
                lambda i, j, k, k0, nk, koff, joff, fj: (_clip_k(k, k0, nk) + koff, joff + jj(j, fj)),
                k0=k0, nk=nk, koff=koff, joff=joff, fj=fj)))
        args.append(arr)
    for (arr, w, off, fj) in rows:
        in_specs.append(pl.BlockSpec((tm, w), functools.partial(lambda i, j, k, off, fj: (i, off + jj(j, fj)), off=off, fj=fj)))
        args.append(arr)
    for arr in consts:
        in_specs.append(pl.BlockSpec(arr.shape, functools.partial(lambda i, j, k, nd: (0,) * nd, nd=arr.ndim)))
        args.append(arr)
    out_specs, out_shape = [], []
    for (ncols, dt, w, off, fj) in outs:
        out_specs.append(pl.BlockSpec((tm, w), functools.partial(lambda i, j, k, off, fj: (i, off + jj(j, fj)), off=off, fj=fj)))
        out_shape.append(jax.ShapeDtypeStruct((M, ncols), dt))
    for (r, c) in parts:
        out_specs.append(pl.BlockSpec((r, c), lambda i, j, k: (0, 0)))
        out_shape.append(jax.ShapeDtypeStruct((r, c), F32))
    scratch = [pltpu.VMEM(s, F32) for s in acc_shapes] if use_scratch else []

    def body(*refs):
        pos = 0
        a_refs = refs[pos:pos + n_a]; pos += n_a
        b_refs = refs[pos:pos + n_b]; pos += n_b
        r_refs = refs[pos:pos + n_r]; pos += n_r
        c_refs = refs[pos:pos + n_c]; pos += n_c
        o_refs = refs[pos:pos + n_o]; pos += n_o
        p_refs = refs[pos:pos + n_p]; pos += n_p
        acc_refs = refs[pos:pos + n_acc] if use_scratch else ()
        i = pl.program_id(0)
        k = pl.program_id(2)

        def dot_of(ai, bi):
            a = a_refs[ai][...].astype(BF)
            b = b_refs[bi][...].astype(BF)
            if b_ops[bi][1]:
                return lax.dot_general(a, b, (((1,), (1,)), ((), ())), preferred_element_type=F32)
            return jnp.dot(a, b, preferred_element_type=F32)

        if parts:
            @pl.when((i == 0) & (k == 0))
            def _():
                for p in p_refs:
                    p[...] = jnp.zeros(p.shape, F32)

        def finish(accs):
            epilogue(accs, r_refs, c_refs, o_refs, p_refs)
            if parts:
                @pl.when(i == gm - 1)
                def _():
                    for p in p_refs:
                        p[0:1, :] = jnp.sum(p[...], axis=0, keepdims=True)

        if not use_scratch:
            accs = [None] * n_acc
            for (ai, bi, ci, k0, nk) in terms:
                d = dot_of(ai, bi)
                accs[ci] = d if accs[ci] is None else accs[ci] + d
            finish(accs)
        else:
            @pl.when(k == 0)
            def _():
                for acc in acc_refs:
                    acc[...] = jnp.zeros(acc.shape, F32)

            for (ai, bi, ci, k0, nk) in terms:
                def do(ai=ai, bi=bi, ci=ci):
                    acc_refs[ci][...] += dot_of(ai, bi)
                if k0 == 0 and nk == nk_total:
                    do()
                else:
                    pl.when((k >= k0) & (k < k0 + nk))(do)

            @pl.when(k == nk_total - 1)
            def _():
                finish([acc[...] for acc in acc_refs])

    res = pl.pallas_call(
        body, grid=(gm, grid_n, nk_total), in_specs=in_specs, out_specs=out_specs, out_shape=out_shape,
        scratch_shapes=scratch, compiler_params=_cp(("arbitrary", "arbitrary", "arbitrary")), name=name,
    )(*args)
    return res


def _mm_tn(name, a, b, tm, tn, tk):
    S, Ka = a.shape
    Nb = b.shape[1]
    nk = S // tk

    def body(a_ref, b_ref, o_ref, acc):
        k = pl.program_id(2)

        @pl.when(k == 0)
        def _():
            acc[...] = jnp.zeros(acc.shape, F32)

        acc[...] += lax.dot_general(a_ref[...], b_ref[...], (((0,), (0,)), ((), ())), preferred_element_type=F32)

        @pl.when(k == nk - 1)
        def _():
            o_ref[...] = acc[...].astype(o_ref.dtype)

    return pl.pallas_call(
        body, grid=(Ka // tm, Nb // tn, nk),
        in_specs=[pl.BlockSpec((tk, tm), lambda i, j, k: (k, i)), pl.BlockSpec((tk, tn), lambda i, j, k: (k, j))],
        out_specs=pl.BlockSpec((tm, tn), lambda i, j, k: (i, j)),
        out_shape=jax.ShapeDtypeStruct((Ka, Nb), BF),
        scratch_shapes=[pltpu.VMEM((tm, tn), F32)],
        compiler_params=_cp(("parallel", "parallel", "arbitrary")), name=name,
    )(a, b)


def _rel_bucket_np(dist):
    max_exact = REL_BUCKETS // 2
    d = np.maximum(dist, 0)
    df = np.maximum(d, 1).astype(np.float32)
    large = max_exact + (np.log(df / np.float32(max_exact)) / np.float32(math.log(REL_MAX_DISTANCE / max_exact))
                         * np.float32(REL_BUCKETS - max_exact)).astype(np.int32)
    large = np.minimum(large, REL_BUCKETS - 1)
    return np.where(d < max_exact, d, large).astype(np.int32)


N_LAYOUTS = 3


def _band_index():
    idx = np.zeros((N_LAYOUTS * 3, 1, QBLK * 2 * QBLK), np.int32)
    for g, (window, dil) in enumerate(GROUPS):
        span = window // dil
        a = np.arange(QBLK)[:, None]; c = np.arange(2 * QBLK)[None, :]
        off = a - c + QBLK
        lay0 = np.where((off >= 0) & (off <= span), _rel_bucket_np(off * dil), -1)
        idx[g, 0] = lay0.reshape(-1)
        idx[3 + g, 0] = lay0.T.reshape(-1)
        k = np.arange(QBLK)[:, None]; q = np.arange(2 * QBLK)[None, :]
        off = q - k
        idx[6 + g, 0] = np.where((off >= 0) & (off <= span), _rel_bucket_np(off * dil), -1).reshape(-1)
    return idx


_NB = QBLK * 2 * QBLK
_BCH = 4096


def _bias_build(tab_t, idx):
    def body(t_ref, i_ref, o_ref):
        ix = i_ref[0]
        t = t_ref[0]
        acc = jnp.full((HEADS, _BCH), NEG_INF, F32)
        for b in range(REL_BUCKETS):
            acc = jnp.where(ix == b, t[:, b:b + 1], acc)
        o_ref[0] = acc

    return pl.pallas_call(
        body, grid=(N_LAYOUTS * 3, _NB // _BCH),
        in_specs=[pl.BlockSpec((1, HEADS, REL_BUCKETS), lambda l, n: (l % 3, 0, 0)),
                  pl.BlockSpec((1, 1, _BCH), lambda l, n: (l, 0, n))],
        out_specs=pl.BlockSpec((1, HEADS, _BCH), lambda l, n: (l, 0, n)),
        out_shape=jax.ShapeDtypeStruct((N_LAYOUTS * 3, HEADS, _NB), F32), compiler_params=_cp(), name="bias_build",
    )(tab_t, idx)


def _bias_grad(ds, idx):
    nch = _NB // _BCH

    def body(d_ref, i_ref, o_ref):
        n = pl.program_id(1)

        @pl.when(n == 0)
        def _():
            o_ref[...] = jnp.zeros(o_ref.shape, F32)

        ix = i_ref[0]
        d = d_ref[0]
        lane = lax.broadcasted_iota(jnp.int32, (HEADS, 128), 1)
        acc = jnp.zeros((HEADS, 128), F32)
        for b in range(REL_BUCKETS):
            s = jnp.sum(jnp.where(ix == b, d, 0.0), axis=1, keepdims=True)
            acc = acc + jnp.where(lane == b, s, 0.0)
        o_ref[0] += acc

    return pl.pallas_call(
        body, grid=(3, nch),
        in_specs=[pl.BlockSpec((1, HEADS, _BCH), lambda l, n: (l, 0, n)),
                  pl.BlockSpec((1, 1, _BCH), lambda l, n: (l, 0, n))],
        out_specs=pl.BlockSpec((1, HEADS, 128), lambda l, n: (l, 0, 0)),
        out_shape=jax.ShapeDtypeStruct((3, HEADS, 128), F32), compiler_params=_cp(), name="bias_grad",
    )(ds, idx)


PT = 256
PSTEP = 1024
STAT_W = 128


def _perm_np(dil):
    p = np.zeros((PT, PT), np.float32)
    m = np.arange(PT // dil)
    for c in range(dil):
        p[c * (PT // dil) + m, m * dil + c] = 1.0
    return p


def _perm_const(dil, dtype, inverse):
    p = _perm_np(dil)
    return jnp.asarray(p.T if inverse else p, dtype)


def _apply_perm(p, x):
    if x.dtype == F32:
        return jnp.dot(p, x, preferred_element_type=F32, precision=lax.Precision.HIGHEST)
    return jnp.dot(p, x, preferred_element_type=F32)


def _to_residue(name, arr, col_blocks, dil):
    S = arr.shape[0]
    nc = len(col_blocks)
    p = _perm_const(dil, arr.dtype, False)
    sub = PT // dil

    def body(*refs):
        p_ref, ins, o_ref = refs[0], refs[1:1 + nc], refs[1 + nc]
        for u in range(PSTEP // PT):
            for t, r in enumerate(ins):
                y = _apply_perm(p_ref[...], r[u * PT:(u + 1) * PT, :]).astype(o_ref.dtype)
                o_ref[:, u * sub:(u + 1) * sub, t * GW:(t + 1) * GW] = y.reshape(dil, sub, GW)

    out = pl.pallas_call(
        body, grid=(S // PSTEP,),
        in_specs=[pl.BlockSpec((PT, PT), lambda i: (0, 0))]
                 + [pl.BlockSpec((PSTEP, GW), functools.partial(lambda i, cb: (i, cb), cb=cb)) for cb in col_blocks],
        out_specs=pl.BlockSpec((dil, PSTEP // dil, nc * GW), lambda i: (0, i, 0)),
        out_shape=jax.ShapeDtypeStruct((dil, S // dil, nc * GW), arr.dtype), compiler_params=_cp(), name=name,
    )(p, *([arr] * nc))
    return out.reshape(S, nc * GW)


def _to_residue_stats(name, arr, dil):
    S = arr.shape[0]
    p = _perm_const(dil, F32, False)
    sub = PT // dil

    def body(p_ref, x_ref, o_ref):
        for u in range(PSTEP // PT):
            y = _apply_perm(p_ref[...], x_ref[u * PT:(u + 1) * PT, :])
            o_ref[:, u * sub:(u + 1) * sub, :] = y.reshape(dil, sub, STAT_W)

    out = pl.pallas_call(
        body, grid=(S // PSTEP,),
        in_specs=[pl.BlockSpec((PT, PT), lambda i: (0, 0)), pl.BlockSpec((PSTEP, STAT_W), lambda i: (i, 0))],
        out_specs=pl.BlockSpec((dil, PSTEP // dil, STAT_W), lambda i: (0, i, 0)),
        out_shape=jax.ShapeDtypeStruct((dil, S // dil, STAT_W), F32), compiler_params=_cp(), name=name,
    )(p, arr)
    return out.reshape(S, STAT_W)


def _from_residue(name, arr, dil):
    S, W = arr.shape
    p = _perm_const(dil, arr.dtype, True)
    sub = PT // dil

    def body(p_ref, x_ref, o_ref):
        for u in range(PSTEP // PT):
            x = x_ref[:, u * sub:(u + 1) * sub, :].reshape(PT, W)
            o_ref[u * PT:(u + 1) * PT, :] = _apply_perm(p_ref[...], x).astype(o_ref.dtype)

    return pl.pallas_call(
        body, grid=(S // PSTEP,),
        in_specs=[pl.BlockSpec((PT, PT), lambda i: (0, 0)), pl.BlockSpec((dil, PSTEP // dil, W), lambda i: (0, i, 0))],
        out_specs=pl.BlockSpec((PSTEP, W), lambda i: (i, 0)),
        out_shape=jax.ShapeDtypeStruct((S, W), arr.dtype), compiler_params=_cp(), name=name,
    )(p, arr.reshape(dil, S // dil, W))


NT_DIMS = (((1,), (1,)), ((), ()))
TN_DIMS = (((0,), (0,)), ((), ()))


def _attn_dims(S, dil):
    L = S // dil
    TQ = min(512, L)
    return L, TQ, L // TQ, TQ // QBLK


def _attn_specs(S, dil):
    L, TQ, nq, nsub = _attn_dims(S, dil)
    nb = L // QBLK
    cur = lambda cb, w=GW: pl.BlockSpec((TQ, w), lambda c, i: (c * nq + i, cb))
    prev = lambda cb, w=GW: pl.BlockSpec((QBLK, w), lambda c, i: (c * nb + jnp.maximum(i * nsub - 1, 0), cb))
    nxt = lambda cb, w=GW: pl.BlockSpec((QBLK, w), lambda c, i: (c * nb + jnp.minimum((i + 1) * nsub, nb - 1), cb))
    band = lambda r, c_: pl.BlockSpec((HEADS, r, c_), lambda c, i: (0, 0, 0))
    return L, TQ, nq, nsub, cur, prev, nxt, band


def _fill(buf, first_ref, second_ref):
    n = first_ref.shape[0]
    buf[0:n, :] = first_ref[...]
    buf[n:n + second_ref.shape[0], :] = second_ref[...]


def _attn_fwd(name, arr, bias_kq, cb, dil):
    S = arr.shape[0]
    kcb, vcb, qcb = cb
    L, TQ, nq, nsub, cur, prev, nxt, band = _attn_specs(S, dil)

    def body(q_ref, kc_ref, kp_ref, vc_ref, vp_ref, b_ref, o_ref, l_ref, kbuf, vbuf):
        i = pl.program_id(1)
        _fill(kbuf, kp_ref, kc_ref)
        _fill(vbuf, vp_ref, vc_ref)
        row = lax.broadcasted_iota(jnp.int32, (2 * QBLK, QBLK), 0)
        first = (row >= QBLK) | (i > 0)
        for j in range(nsub):
            rs = slice(j * QBLK, (j + 1) * QBLK)
            ks = slice(j * QBLK, (j + 2) * QBLK)
            lrows = []
            for h in range(HEADS):
                hs = slice(h * HEAD_DIM, (h + 1) * HEAD_DIM)
                s = lax.dot_general(kbuf[ks, hs], q_ref[rs, hs], NT_DIMS, preferred_element_type=F32) * SCALE + b_ref[h]
                if j == 0:
                    s = jnp.where(first, s, NEG_INF)
                m = jnp.max(s, axis=0, keepdims=True)
                p = jnp.exp(s - m)
                den = jnp.sum(p, axis=0, keepdims=True)
                o_t = lax.dot_general(vbuf[ks, hs], p.astype(BF), TN_DIMS, preferred_element_type=F32) / den
                o_ref[rs, hs] = o_t.T.astype(BF)
                lrows.append(m + jnp.log(den))
            lt = jnp.concatenate(lrows + [jnp.zeros((STAT_W - HEADS, QBLK), F32)], axis=0)
            l_ref[rs, :] = lt.T

    return pl.pallas_call(
        body, grid=(dil, nq),
        in_specs=[cur(qcb), cur(kcb), prev(kcb), cur(vcb), prev(vcb), band(2 * QBLK, QBLK)],
        out_specs=[cur(0), cur(0, STAT_W)],
        out_shape=[jax.ShapeDtypeStruct((S, GW), BF), jax.ShapeDtypeStruct((S, STAT_W), F32)],
        scratch_shapes=[pltpu.VMEM((QBLK + TQ, GW), BF), pltpu.VMEM((QBLK + TQ, GW), BF)],
        compiler_params=_cp(), name=name,
    )(arr, arr, arr, arr, arr, bias_kq)


def _attn_bwd_dkv(name, arr, bias_kq2, do, stats, cb, dil):
    S = arr.shape[0]
    kcb, vcb, qcb = cb
    L, TQ, nq, nsub, cur, prev, nxt, band = _attn_specs(S, dil)

    def body(k_ref, v_ref, qc_ref, qn_ref, b_ref, doc_ref, don_ref, sc_ref, sn_ref, o_ref, qbuf, dobuf, sbuf):
        i = pl.program_id(1)
        _fill(qbuf, qc_ref, qn_ref)
        _fill(dobuf, doc_ref, don_ref)
        for j in range(nsub + 1):
            rs = slice(j * QBLK, (j + 1) * QBLK)
            sbuf[:, rs] = (sc_ref[rs, :] if j < nsub else sn_ref[...]).T
        col = lax.broadcasted_iota(jnp.int32, (QBLK, 2 * QBLK), 1)
        last = (col < QBLK) | (i < nq - 1)
        for j in range(nsub):
            rs = slice(j * QBLK, (j + 1) * QBLK)
            qs = slice(j * QBLK, (j + 2) * QBLK)
            for h in range(HEADS):
                hs = slice(h * HEAD_DIM, (h + 1) * HEAD_DIM)
                qq = qbuf[qs, hs]
                dd = dobuf[qs, hs]
                s = lax.dot_general(k_ref[rs, hs], qq, NT_DIMS, preferred_element_type=F32) * SCALE + b_ref[h]
                if j == nsub - 1:
                    s = jnp.where(last, s, NEG_INF)
                p = jnp.exp(s - sbuf[h:h + 1, qs])
                dp = lax.dot_general(v_ref[rs, hs], dd, NT_DIMS, preferred_element_type=F32)
                ds = p * (dp - sbuf[HEADS + h:HEADS + h + 1, qs])
                o_ref[rs, h * HEAD_DIM:(h + 1) * HEAD_DIM] = (jnp.dot(ds.astype(BF), qq, preferred_element_type=F32) * SCALE).astype(BF)
                o_ref[rs, GW + h * HEAD_DIM:GW + (h + 1) * HEAD_DIM] = jnp.dot(p.astype(BF), dd, preferred_element_type=F32).astype(BF)

    return pl.pallas_call(
        body, grid=(dil, nq),
        in_specs=[cur(kcb), cur(vcb), cur(qcb), nxt(qcb), band(QBLK, 2 * QBLK),
                  cur(0), nxt(0), cur(0, STAT_W), nxt(0, STAT_W)],
        out_specs=cur(0, 2 * GW),
        out_shape=jax.ShapeDtypeStruct((S, ATTN_W), BF),
        scratch_shapes=[pltpu.VMEM((TQ + QBLK, GW), BF), pltpu.VMEM((TQ + QBLK, GW), BF), pltpu.VMEM((STAT_W, TQ + QBLK), F32)],
        compiler_params=_cp(), name=name,
    )(arr, arr, arr, arr, bias_kq2, do, do, stats, stats)


def _attn_bwd_dq(name, arr, bias_qk, do, stats, dkv, cb, dil):
    S = arr.shape[0]
    kcb, vcb, qcb = cb
    L, TQ, nq, nsub, cur, prev, nxt, band = _attn_specs(S, dil)

    def body(q_ref, kc_ref, kp_ref, vc_ref, vp_ref, b_ref, do_ref, st_ref, _alias, dq_ref, db_ref, kbuf, vbuf):
        c = pl.program_id(0)
        i = pl.program_id(1)

        @pl.when((c == 0) & (i == 0))
        def _():
            db_ref[...] = jnp.zeros(db_ref.shape, F32)

        _fill(kbuf, kp_ref, kc_ref)
        _fill(vbuf, vp_ref, vc_ref)
        col = lax.broadcasted_iota(jnp.int32, (QBLK, 2 * QBLK), 1)
        first = (col >= QBLK) | (i > 0)
        for h in range(HEADS):
            hs = slice(h * HEAD_DIM, (h + 1) * HEAD_DIM)
            bias_h = b_ref[h]
            db = jnp.zeros((QBLK, 2 * QBLK), F32)
            for j in range(nsub):
                rs = slice(j * QBLK, (j + 1) * QBLK)
                ks = slice(j * QBLK, (j + 2) * QBLK)
                kk = kbuf[ks, hs]
                s = lax.dot_general(q_ref[rs, hs], kk, NT_DIMS, preferred_element_type=F32) * SCALE + bias_h
                if j == 0:
                    s = jnp.where(first, s, NEG_INF)
                p = jnp.exp(s - st_ref[rs, h:h + 1])
                dp = lax.dot_general(do_ref[rs, hs], vbuf[ks, hs], NT_DIMS, preferred_element_type=F32)
                ds = p * (dp - st_ref[rs, HEADS + h:HEADS + h + 1])
                db = db + ds
                dq_ref[rs, hs] = (jnp.dot(ds.astype(BF), kk, preferred_element_type=F32) * SCALE).astype(BF)
            db_ref[h] += db

    return pl.pallas_call(
        body, grid=(dil, nq),
        in_specs=[cur(qcb), cur(kcb), prev(kcb), cur(vcb), prev(vcb), band(QBLK, 2 * QBLK),
                  cur(0), cur(0, STAT_W), pl.BlockSpec(memory_space=pl.ANY)],
        out_specs=[cur(2), band(QBLK, 2 * QBLK)],
        out_shape=[jax.ShapeDtypeStruct((S, ATTN_W), BF), jax.ShapeDtypeStruct((HEADS, QBLK, 2 * QBLK), F32)],
        scratch_shapes=[pltpu.VMEM((QBLK + TQ, GW), BF), pltpu.VMEM((QBLK + TQ, GW), BF)],
        input_output_aliases={8: 0},
        compiler_params=_cp(("arbitrary", "arbitrary")), name=name,
    )(arr, arr, arr, arr, arr, bias_qk, do, stats, dkv)


def _head_expand():
    e = np.zeros((STAT_W, GW), np.float32)
    for h in range(HEADS):
        e[h, h * HEAD_DIM:(h + 1) * HEAD_DIM] = 1.0
    return e


def _attn_merge(os_, ls_, S):
    dils = [d for _, d in GROUPS]
    pb = [_perm_const(d, BF, True) for d in dils[1:]]
    pf = [_perm_const(d, F32, True) for d in dils[1:]]
    expand = jnp.asarray(_head_expand())

    def body(o0, o1, o2, l0, l1, l2, pb1, pb2, pf1, pf2, e_ref, o_ref, l_ref):
        for u in range(PSTEP // PT):
            rs = slice(u * PT, (u + 1) * PT)
            res = lambda r, d: r[:, u * (PT // d):(u + 1) * (PT // d), :].reshape(PT, r.shape[2])
            ov = [o0[rs, :].astype(F32), _apply_perm(pb1[...], res(o1, dils[1])), _apply_perm(pb2[...], res(o2, dils[2]))]
            lv = [l0[rs, :], _apply_perm(pf1[...], res(l1, dils[1])), _apply_perm(pf2[...], res(l2, dils[2]))]
            m = jnp.maximum(jnp.maximum(lv[0], lv[1]), lv[2])
            ev = [jnp.exp(l - m) for l in lv]
            den = ev[0] + ev[1] + ev[2]
            acc = jnp.zeros((PT, GW), F32)
            for g in range(3):
                wide = jnp.dot(ev[g] / den, e_ref[...], preferred_element_type=F32, precision=lax.Precision.HIGHEST)
                acc = acc + wide * ov[g]
            o_ref[rs, :] = acc.astype(BF)
            l_ref[rs, :] = m + jnp.log(den)

    nat = lambda w: pl.BlockSpec((PSTEP, w), lambda i: (i, 0))
    res = lambda d, w: pl.BlockSpec((d, PSTEP // d, w), lambda i: (0, i, 0))
    cst = lambda a: pl.BlockSpec(a.shape, lambda i: (0, 0))
    args = [os_[0], os_[1].reshape(dils[1], S // dils[1], GW), os_[2].reshape(dils[2], S // dils[2], GW),
            ls_[0], ls_[1].reshape(dils[1], S // dils[1], STAT_W), ls_[2].reshape(dils[2], S // dils[2], STAT_W),
            pb[0], pb[1], pf[0], pf[1], expand]
    return pl.pallas_call(
        body, grid=(S // PSTEP,),
        in_specs=[nat(GW), res(dils[1], GW), res(dils[2], GW), nat(STAT_W), res(dils[1], STAT_W), res(dils[2], STAT_W)]
                 + [cst(a) for a in args[6:]],
        out_specs=[nat(GW), nat(STAT_W)],
        out_shape=[jax.ShapeDtypeStruct((S, GW), BF), jax.ShapeDtypeStruct((S, STAT_W), F32)],
        compiler_params=_cp(), name="attn_merge",
    )(*args)


CT = 256
CBUF = HALO + CT + 8
RG = 4


def _ln_hat(u1):
    mu = jnp.mean(u1, axis=-1, keepdims=True)
    xc = u1 - mu
    rstd = lax.rsqrt(jnp.mean(xc * xc, axis=-1, keepdims=True) + LN_EPS)
    return xc * rstd, rstd


def _glu_window(hu_ref, hg_ref, huh_ref, hgh_ref, bglu_ref, buf_ref, i):
    bu = bglu_ref[:, 0:D]
    bg = bglu_ref[:, D:2 * D]
    uh = (huh_ref[...].astype(F32) + bu) * _sig(hgh_ref[...].astype(F32) + bg)
    buf_ref[0:HALO, :] = jnp.where(i > 0, uh, 0.0)
    a = hu_ref[...].astype(F32) + bu
    s = _sig(hg_ref[...].astype(F32) + bg)
    buf_ref[HALO:HALO + CT, :] = a * s
    buf_ref[HALO + CT:CBUF, :] = jnp.zeros((8, D), F32)
    return a, s


def _shift_copies(buf_ref, sh_ref):
    for r in range(8):
        sh_ref[r] = buf_ref[r:r + HALO + CT, :]


def _tap_rows(wb_ref, w_ref):
    for j in range(CONV_W):
        wb_ref[j * 8:(j + 1) * 8, :] = jnp.broadcast_to(w_ref[j:j + 1, :], (8, D))


def _conv_taps(sh_ref, wb_ref, out_ref, init, offset):
    for rg in range(CT // (8 * RG)):
        accs = [init] * RG
        for j in range(CONV_W):
            off = offset(j)
            wj = wb_ref[j * 8:(j + 1) * 8, :]
            for q in range(RG):
                row = 8 * (rg * RG + q + off // 8)
                accs[q] = accs[q] + wj * sh_ref[off % 8, row:row + 8, :]
        for q in range(RG):
            out_ref[(rg * RG + q) * 8:(rg * RG + q + 1) * 8, :] = accs[q]


def _conv_specs(S):
    cur = lambda cb: pl.BlockSpec((CT, D), lambda i: (i, cb))
    halo = lambda cb: pl.BlockSpec((HALO, D), lambda i: (jnp.maximum(i * (CT // HALO) - 1, 0), cb))
    full = lambda shp: pl.BlockSpec(shp, lambda i: (0, 0))
    return cur, halo, full


def _conv_fwd(z, b_glu, w_dw, b_dw, g_ln, b_ln):
    S = z.shape[0]
    cur, halo, full = _conv_specs(S)

    def body(hu, hg, huh, hgh, bglu, w, bdw, gln, bln, u1_ref, u3_ref, buf, sh, wb):
        i = pl.program_id(0)

        @pl.when(i == 0)
        def _():
            _tap_rows(wb, w)

        _glu_window(hu, hg, huh, hgh, bglu, buf, i)
        _shift_copies(buf, sh)
        _conv_taps(sh, wb, u1_ref, jnp.broadcast_to(bdw[...], (8, D)), lambda j: 2 + j)
        xh, _ = _ln_hat(u1_ref[...])
        u2 = xh * gln[...] + bln[...]
        u3_ref[...] = (u2 * _sig(u2)).astype(BF)

    return pl.pallas_call(
        body, grid=(S // CT,),
        in_specs=[cur(0), cur(1), halo(0), halo(1), full((1, 2 * D)), full((HALO, D)), full((1, D)), full((1, D)), full((1, D))],
        out_specs=[pl.BlockSpec((CT, D), lambda i: (i, 0))] * 2,
        out_shape=[jax.ShapeDtypeStruct((S, D), F32), jax.ShapeDtypeStruct((S, D), BF)],
        scratch_shapes=[pltpu.VMEM((CBUF, D), F32), pltpu.VMEM((8, HALO + CT, D), F32), pltpu.VMEM((CONV_W * 8, D), F32)],
        compiler_params=_cp(("arbitrary",)), name="conv_fwd",
    )(z, z, z, z, b_glu, w_dw, b_dw, g_ln, b_ln)


def _conv_bwd(du1, z, b_glu, w_dw):
    S = z.shape[0]
    n = S // CT
    cur, halo, full = _conv_specs(S)

    def body(du, dun, hu, hg, huh, hgh, bglu, w, dz_ref, dw_ref, dbg_ref, bufu, bufd, shu, shd, wb, du0_ref, dwacc):
        i = pl.program_id(0)

        @pl.when(i == 0)
        def _():
            _tap_rows(wb, w)
            dwacc[...] = jnp.zeros(dwacc.shape, F32)
            dbg_ref[...] = jnp.zeros(dbg_ref.shape, F32)

        a, s = _glu_window(hu, hg, huh, hgh, bglu, bufu, i)
        bufd[0:CT, :] = du[...]
        bufd[CT:CT + HALO, :] = jnp.where(i < n - 1, dun[...], 0.0)
        bufd[CT + HALO:CBUF, :] = jnp.zeros((8, D), F32)
        _shift_copies(bufu, shu)
        _shift_copies(bufd, shd)
        _conv_taps(shd, wb, du0_ref, jnp.zeros((8, D), F32), lambda j: 30 - j)
        for rg in range(CT // (8 * RG)):
            dch = [bufd[(rg * RG + q) * 8:(rg * RG + q + 1) * 8, :] for q in range(RG)]
            for j in range(CONV_W):
                off = 2 + j
                acc = dwacc[j * 8:(j + 1) * 8, :]
                for q in range(RG):
                    row = 8 * (rg * RG + q + off // 8)
                    acc = acc + dch[q] * shu[off % 8, row:row + 8, :]
                dwacc[j * 8:(j + 1) * 8, :] = acc
        du0 = du0_ref[...]
        dhu = du0 * s
        dhg = du0 * a * s * (1.0 - s)
        dz_ref[:, 0:D] = dhu.astype(BF)
        dz_ref[:, D:2 * D] = dhg.astype(BF)
        dbg_ref[:, 0:D] += _psum8(dhu)
        dbg_ref[:, D:2 * D] += _psum8(dhg)

        @pl.when(i == n - 1)
        def _():
            dbg_ref[0:1, :] = jnp.sum(dbg_ref[...], axis=0, keepdims=True)
            for j in range(CONV_W):
                dw_ref[j:j + 1, :] = jnp.sum(dwacc[j * 8:(j + 1) * 8, :], axis=0, keepdims=True)
            dw_ref[CONV_W:HALO, :] = jnp.zeros((HALO - CONV_W, D), F32)

    nxt = pl.BlockSpec((HALO, D), lambda i: (jnp.minimum((i + 1) * (CT // HALO), S // HALO - 1), 0))
    return pl.pallas_call(
        body, grid=(n,),
        in_specs=[pl.BlockSpec((CT, D), lambda i: (i, 0)), nxt, cur(0), cur(1), halo(0), halo(1), full((1, 2 * D)), full((HALO, D))],
        out_specs=[pl.BlockSpec((CT, 2 * D), lambda i: (i, 0)), full((HALO, D)), full((8, 2 * D))],
        out_shape=[jax.ShapeDtypeStruct((S, 2 * D), BF), jax.ShapeDtypeStruct((HALO, D), F32), jax.ShapeDtypeStruct((8, 2 * D), F32)],
        scratch_shapes=[pltpu.VMEM((CBUF, D), F32), pltpu.VMEM((CBUF, D), F32), pltpu.VMEM((8, HALO + CT, D), F32),
                        pltpu.VMEM((8, HALO + CT, D), F32), pltpu.VMEM((CONV_W * 8, D), F32), pltpu.VMEM((CT, D), F32),
                        pltpu.VMEM((CONV_W * 8, D), F32)],
        compiler_params=_cp(("arbitrary",)), name="conv_bwd",
    )(du1, du1, z, z, z, z, b_glu, w_dw)


MESH = pl.DeviceIdType.MESH


def _all_gather(name, shards):
    n = len(shards)

    def body(*refs):
        ins, outs = refs[:n], refs[n:2 * n]
        send_sems, recv_sems, local_sems = refs[2 * n:]
        x, y, c = lax.axis_index("x"), lax.axis_index("y"), lax.axis_index("c")
        me, sibling = (x, y, c), (x, y, 1 - c)
        chips = [(1 - x, y), (x, 1 - y), (1 - x, 1 - y)]

        def slot(a, px, py, pc):
            return outs[a].at[4 * px + 2 * py + pc]

        def copy(a, k, block, to, src=None):
            return pltpu.make_async_remote_copy(
                src_ref=slot(a, *block) if src is None else src, dst_ref=slot(a, *block),
                send_sem=send_sems.at[a, k], recv_sem=recv_sems.at[a, k], device_id=to, device_id_type=MESH)

        mine = [pltpu.make_async_copy(ins[a], slot(a, *me), local_sems.at[a]) for a in range(n)]
        for cp in mine:
            cp.start()
        first = []
        for a in range(n):
            first.append(copy(a, 0, me, sibling, src=ins[a]))
            first += [copy(a, 1 + j, me, (*chip, c), src=ins[a]) for j, chip in enumerate(chips)]
        for cp in first:
            cp.start()
        passed = []
        for j, chip in enumerate(chips):
            for a in range(n):
                copy(a, 1 + j, (*chip, c), me).wait_recv()
                fwd = copy(a, 4 + j, (*chip, c), sibling)
                fwd.start()
                passed.append(fwd)
        for a in range(n):
            copy(a, 0, sibling, me).wait_recv()
        for j, chip in enumerate(chips):
            for a in range(n):
                copy(a, 4 + j, (*chip, 1 - c), me).wait_recv()
        for cp in first + passed:
            cp.wait_send()
        for cp in mine:
            cp.wait()

    anyspec = pl.BlockSpec(memory_space=pl.ANY)
    return pl.pallas_call(
        body, in_specs=[anyspec] * n, out_specs=[anyspec] * n,
        out_shape=[jax.ShapeDtypeStruct((NDEV,) + s.shape, s.dtype) for s in shards],
        scratch_shapes=[pltpu.SemaphoreType.DMA((n, 7)), pltpu.SemaphoreType.DMA((n, 7)), pltpu.SemaphoreType.DMA((n,))],
        name=name,
    )(*shards)


HBM_SPEC = pl.BlockSpec(memory_space=pltpu.HBM)
SEM_SPEC = pl.BlockSpec(memory_space=pltpu.SEMAPHORE)
DATAFLOW = pltpu.SideEffectType.DATAFLOW_SIDE_EFFECTING


def _peers():
    x, y, c = lax.axis_index("x"), lax.axis_index("y"), lax.axis_index("c")
    out = []
    for k in range(1, NDEV):
        px = 1 - x if k & 4 else x
        py = 1 - y if k & 2 else y
        pc = 1 - c if k & 1 else c
        out.append(((px, py, pc), 4 * px + 2 * py + pc))
    return 4 * x + 2 * y + c, out


def _exchange_copies(srcs, lands, send_sems, recv_sems, gather):
    my, peers = _peers()
    pairs = []
    for k, (dev, pid) in enumerate(peers):
        for a in range(len(srcs)):
            src = srcs[a] if gather else srcs[a].at[pid]
            sems = dict(send_sem=send_sems[a * (NDEV - 1) + k], recv_sem=recv_sems[a * (NDEV - 1) + k], device_id=dev,
                        device_id_type=MESH)
            pairs.append((pltpu.make_async_remote_copy(src_ref=src, dst_ref=lands[a].at[my], **sems),
                          pltpu.make_async_remote_copy(src_ref=src, dst_ref=lands[a].at[pid], **sems)))
    return pairs


def _exchange_start(name, srcs, gather):
    n = len(srcs)
    ns = n * (NDEV - 1)
    shapes = [(s.shape if gather else s.shape[1:]) for s in srcs]
    lands = [lax.empty((NDEV,) + shp, s.dtype) for shp, s in zip(shapes, srcs)]

    def body(*refs):
        src_refs, land_refs = refs[:n], refs[n:2 * n]
        send_sems, recv_sems = refs[2 * n:2 * n + ns], refs[2 * n + ns:2 * n + 2 * ns]
        token = refs[-1]
        for mine, _ in _exchange_copies(src_refs, land_refs, send_sems, recv_sems, gather):
            mine.start()
        token[...] = jnp.zeros(token.shape, token.dtype)

    hbm = lambda a: pltpu.HBM(a.shape, a.dtype)
    res = pl.pallas_call(
        body, name=name,
        out_shape=(*([pltpu.SemaphoreType.DMA(())] * (2 * ns)), *[hbm(s) for s in srcs], *[hbm(l) for l in lands],
                   jax.ShapeDtypeStruct((8, 128), F32)),
        in_specs=[HBM_SPEC] * (2 * n),
        out_specs=(*([SEM_SPEC] * (2 * ns)), *([HBM_SPEC] * (2 * n)), pl.BlockSpec(memory_space=pltpu.VMEM)),
        input_output_aliases={i: 2 * ns + i for i in range(2 * n)},
        compiler_params=pltpu.CompilerParams(has_side_effects=DATAFLOW),
    )(*[pltpu.with_memory_space_constraint(s, pltpu.HBM) for s in srcs],
      *[pltpu.with_memory_space_constraint(l, pltpu.HBM) for l in lands])
    return list(res[:ns]), list(res[ns:2 * ns]), list(res[2 * ns:2 * ns + n]), list(res[2 * ns + n:2 * ns + 2 * n]), res[-1]


def _exchange_wait(name, handle, after, gather):
    send_sems, recv_sems, srcs, lands, _ = handle
    n = len(srcs)
    ns = n * (NDEV - 1)

    def body(*refs):
        src_refs, land_refs = refs[:n], refs[n:2 * n]
        s_sems, r_sems = refs[2 * n:2 * n + ns], refs[2 * n + ns:2 * n + 2 * ns]
        for mine, theirs in _exchange_copies(src_refs, land_refs, s_sems, r_sems, gather):
            mine.wait_send()
            theirs.wait_recv()

    hbm = lambda a: pltpu.HBM(a.shape, a.dtype)
    res = pl.pallas_call(
        body, name=name,
        out_shape=(*[hbm(s) for s in srcs], *[hbm(l) for l in lands]),
        in_specs=[HBM_SPEC] * (2 * n) + [SEM_SPEC] * (2 * ns) + [pl.BlockSpec(memory_space=pl.ANY)],
        out_specs=tuple([HBM_SPEC] * (2 * n)),
        input_output_aliases={i: i for i in range(2 * n)},
        compiler_params=pltpu.CompilerParams(has_side_effects=DATAFLOW),
    )(*srcs, *lands, *send_sems, *recv_sems, after)
    return list(res[n:])


def _set_own_slot(land, own):
    my = 4 * lax.axis_index("x") + 2 * lax.axis_index("y") + lax.axis_index("c")
    return lax.dynamic_update_slice(land, own[None], (my, 0, 0))


def _own_block(blocks):
    my = 4 * lax.axis_index("x") + 2 * lax.axis_index("y") + lax.axis_index("c")
    return lax.dynamic_index_in_dim(blocks, my, axis=0, keepdims=False)


_C1 = 1.0 - ADAM_B1 ** ADAM_STEP
_C2 = 1.0 - ADAM_B2 ** ADAM_STEP


def _adamw(name, w, m, v, recv, tr):
    R, C = w.shape

    def body(w_ref, m_ref, v_ref, r_ref, g_ref, d_ref, nm_ref, nv_ref):
        g = r_ref[0].astype(F32)
        for s in range(1, NDEV):
            g = g + r_ref[s].astype(F32)
        wv = w_ref[...]
        nm = ADAM_B1 * m_ref[...] + (1.0 - ADAM_B1) * g
        nv = ADAM_B2 * v_ref[...] + (1.0 - ADAM_B2) * (g * g)
        m_hat = nm / _C1
        v_hat = nv / _C2
        g_ref[...] = g
        d_ref[...] = -ADAM_LR * (m_hat / (jnp.sqrt(v_hat) + ADAM_EPS) + ADAM_WD * wv)
        nm_ref[...] = nm
        nv_ref[...] = nv

    blk = pl.BlockSpec((tr, C), lambda i: (i, 0))
    return pl.pallas_call(
        body, grid=(R // tr,), in_specs=[blk, blk, blk, pl.BlockSpec((NDEV, tr, C), lambda i: (0, i, 0))],
        out_specs=[blk] * 4, out_shape=[jax.ShapeDtypeStruct((R, C), F32)] * 4,
        compiler_params=_cp(), name=name,
    )(w, m, v, recv)


def _row(v):
    return v.reshape(1, -1)


def _local_step(xs, tgt, Wp, rest_fn, early_fn, late_fn, rel_bias_table, g_pre_mix, b_glu, b_dw, g_conv_ln,
                b_conv_ln, b_conv_out, g_post_mix, g_pre_ffn, g_post_ffn):
    S = xs.shape[0]
    g1, g2, g3, g4 = _row(g_pre_mix), _row(g_post_mix), _row(g_pre_ffn), _row(g_post_ffn)
    bglu, bdw, gln, bln, bco = _row(b_glu), _row(b_dw), _row(g_conv_ln), _row(b_conv_ln), _row(b_conv_out)
    full = (D, F32, D, 0, False)
    fullb = (D, BF, D, 0, False)

    def epi_rms(accs, r, c, o, p):
        v = r[0][...]
        o[0][...] = (v * _rms_r(v) * c[0][...]).astype(BF)

    (h1,) = _fused_mm("rms_in", S, 512, 1, [], [], [], [], [(xs, D, 0, False)], [g1], [fullb], [], epi_rms)

    def epi_cast(accs, r, c, o, p):
        o[0][...] = accs[0].astype(BF)

    ZT = IN_W // 4
    (z,) = _fused_mm("in_proj", S, 1024, 4, [(h1, D, 0, 1)], [(Wp, False, D, ZT, 0, 1, 0, 0, True)], [(0, 0, 0, 0, 1)],
                     [(1024, ZT)], [], [], [(IN_W, BF, ZT, 0, True)], [], epi_cast)

    idx = jnp.asarray(_band_index())
    tab_t = rel_bias_table.T.reshape(3, HEADS, REL_BUCKETS)
    bias_all = _bias_build(tab_t, idx)
    bias_qk = [bias_all[g].reshape(HEADS, QBLK, 2 * QBLK) for g in range(3)]
    bias_kq = [bias_all[3 + g].reshape(HEADS, 2 * QBLK, QBLK) for g in range(3)]
    bias_kq2 = [bias_all[6 + g].reshape(HEADS, QBLK, 2 * QBLK) for g in range(3)]
    dils = [d for _, d in GROUPS]
    qkv = [(z, _kvq_blocks(0))] + [(_to_residue(f"qkv_to_residue_g{g}", z, _kvq_blocks(g), dils[g]), (0, 1, 2)) for g in (1, 2)]
    os_, ls_ = [], []
    for g in range(3):
        o_g, l_g = _attn_fwd(f"attn_fwd_g{g}", qkv[g][0], bias_kq[g], qkv[g][1], dils[g])
        os_.append(o_g)
        ls_.append(l_g)
    o_att, lse = _attn_merge(os_, ls_, S)

    Wfi, Wfo, Wco, Wmo, Wao, wdw = rest_fn(lse)
    u1, u3 = _conv_fwd(z, bglu, wdw, bdw, gln, bln)

    def epi_mix(accs, r, c, o, p):
        ya = accs[0]
        yc = accs[1] + c[0][...]
        mg = _sig(r[0][...].astype(F32)) * ya + _sig(r[1][...].astype(F32)) * yc
        mgb = mg.astype(BF)
        m2 = jnp.dot(mgb, c[1][...], preferred_element_type=F32)
        x1 = r[2][...] + m2 * _rms_r(m2) * c[2][...]
        o[0][...] = ya.astype(BF)
        o[1][...] = yc.astype(BF)
        o[2][...] = mgb
        o[3][...] = m2.astype(BF)
        o[4][...] = x1
        o[5][...] = (x1 * _rms_r(x1) * c[3][...]).astype(BF)

    y_attn, y_conv, merged, m2, x1, h2 = _fused_mm(
        "mix_fwd", S, 512, 1, [(o_att, GW, 0, 1), (u3, D, 0, 1)],
        [(Wao, False, GW, D, 0, 1, 0, 0, False), (Wco, False, D, D, 0, 1, 0, 0, False)], [(0, 0, 0, 0, 1), (1, 1, 1, 0, 1)],
        [(512, D), (512, D)], [(z, D, 2, False), (z, D, 3, False), (xs, D, 0, False)], [bco, Wmo, g2, g3],
        [fullb, fullb, fullb, fullb, full, fullb], [], epi_mix)

    HN = FFN // 2

    def epi_ffn_in(accs, r, c, o, p):
        gt, up = accs
        o[0][...] = gt.astype(BF)
        o[1][...] = up.astype(BF)
        o[2][...] = (gt * _sig(gt) * up).astype(BF)

    gate, up, act = _fused_mm(
        "ffn_in", S, 512, 2, [(h2, D, 0, 1)],
        [(Wfi, False, D, HN, 0, 1, 0, 0, True), (Wfi, False, D, HN, 0, 1, 0, 2, True)], [(0, 0, 0, 0, 1), (0, 1, 1, 0, 1)],
        [(512, HN), (512, HN)], [], [], [(FFN, BF, HN, 0, True)] * 3, [], epi_ffn_in)

    def epi_loss(accs, r, c, o, p):
        f2 = accs[0]
        g = c[0][...]
        rr = _rms_r(f2)
        err = r[0][...] + f2 * rr * g - r[1][...]
        dy = err * (1.0 / D)
        df2, dgr = _rms_bwd(f2, rr, g, dy)
        o[0][...] = dy
        o[1][...] = df2.astype(BF)
        p[0][...] += _psum8(err * err)
        p[1][...] += _psum8(dgr)

    dy, df2, loss_p, dg4 = _fused_mm(
        "ffn_out_loss", S, 512, 1, [(act, FFN, 0, 1)], [(Wfo, False, FFN, D, 0, 1, 0, 0, False)], [(0, 0, 0, 0, 1)],
        [(512, D)], [(x1, D, 0, False), (tgt, D, 0, False)], [g4], [full, fullb], [(8, D), (8, D)], epi_loss)

    def epi_swiglu(accs, r, c, o, p):
        da = accs[0]
        gt = r[0][...].astype(F32)
        sg = _sig(gt)
        o[0][...] = (da * r[1][...].astype(F32) * sg * (1.0 + gt * (1.0 - sg))).astype(BF)
        o[1][...] = (da * gt * sg).astype(BF)

    dgate, dup = _fused_mm(
        "ffn_out_bwd", S, 512, 2, [(df2, D, 0, 1)], [(Wfo, True, D, HN, 0, 1, 0, 0, True)], [(0, 0, 0, 0, 1)],
        [(512, HN)], [(gate, HN, 0, True), (up, HN, 0, True)], [], [(FFN, BF, HN, 0, True)] * 2, [], epi_swiglu)
    dWfo = _mm_tn("dw_ffn_out", act, df2, HN, D, 1024)

    def epi_dh2(accs, r, c, o, p):
        dh2 = accs[0]
        x1v = r[1][...]
        r3 = _rms_r(x1v)
        d1, dg3r = _rms_bwd(x1v, r3, c[0][...], dh2)
        dx1 = r[0][...] + d1
        m2v = r[2][...].astype(F32)
        r2 = _rms_r(m2v)
        dm2, dg2r = _rms_bwd(m2v, r2, c[1][...], dx1)
        o[0][...] = dx1
        o[1][...] = dm2.astype(BF)
        p[0][...] += _psum8(dg3r)
        p[1][...] += _psum8(dg2r)

    dx1, dm2, dg3, dg2 = _fused_mm(
        "ffn_in_bwd", S, 512, 1, [(dgate, HN, 0, 2), (dup, HN, 2, 2)], [(Wfi, True, HN, D, 0, 4, 0, 0, False)],
        [(0, 0, 0, 0, 2), (1, 0, 0, 2, 2)], [(512, D)], [(dy, D, 0, False), (x1, D, 0, False), (m2, D, 0, False)], [g3, g2],
        [full, fullb], [(8, D), (8, D)], epi_dh2)
    dWfi = jnp.concatenate([_mm_tn("dw_ffn_gate", h2, dgate, D, HN, 1024), _mm_tn("dw_ffn_up", h2, dup, D, HN, 1024)], axis=1)

    def epi_dmix(accs, r, c, o, p):
        dm = accs[0]
        sa = _sig(r[0][...].astype(F32))
        sc = _sig(r[1][...].astype(F32))
        o[0][...] = (dm * sa).astype(BF)
        o[1][...] = (dm * sc).astype(BF)
        o[2][:, 0:D] = (dm * r[2][...].astype(F32) * sa * (1.0 - sa)).astype(BF)
        o[2][:, D:2 * D] = (dm * r[3][...].astype(F32) * sc * (1.0 - sc)).astype(BF)

    dy_attn, dy_conv, dz_gate = _fused_mm(
        "mix_bwd", S, 512, 1, [(dm2, D, 0, 1)], [(Wmo, True, D, D, 0, 1, 0, 0, False)], [(0, 0, 0, 0, 1)], [(512, D)],
        [(z, D, 2, False), (z, D, 3, False), (y_attn, D, 0, False), (y_conv, D, 0, False)], [],
        [fullb, fullb, (2 * D, BF, 2 * D, 0, False)], [], epi_dmix)
    dWmo = _mm_tn("dw_mix_out", merged, dm2, D, D, 1024)

    def epi_dconv(accs, r, c, o, p):
        du3 = accs[0]
        xh, rstd = _ln_hat(r[0][...])
        gl = c[0][...]
        u2 = xh * gl + c[1][...]
        sg = _sig(u2)
        du2 = du3 * sg * (1.0 + u2 * (1.0 - sg))
        dxh = du2 * gl
        du1 = rstd * (dxh - jnp.mean(dxh, axis=-1, keepdims=True) - xh * jnp.mean(dxh * xh, axis=-1, keepdims=True))
        o[0][...] = du1
        p[0][...] += _psum8(du2 * xh)
        p[1][...] += _psum8(du2)
        p[2][...] += _psum8(du1)
        p[3][...] += _psum8(r[1][...].astype(F32))

    du1, dgln, dbln, dbdw, dbco = _fused_mm(
        "conv_out_bwd", S, 512, 1, [(dy_conv, D, 0, 1)], [(Wco, True, D, D, 0, 1, 0, 0, False)], [(0, 0, 0, 0, 1)], [(512, D)],
        [(u1, D, 0, False), (dy_conv, D, 0, False)], [gln, bln], [full], [(8, D)] * 4, epi_dconv)
    dWco = _mm_tn("dw_conv_out", u3, dy_conv, D, D, 1024)
    dz_glu, dwdw, dbglu = _conv_bwd(du1, z, bglu, wdw)

    head_sum = np.zeros((GW, STAT_W), np.float32)
    for h in range(HEADS):
        head_sum[h * HEAD_DIM:(h + 1) * HEAD_DIM, HEADS + h] = 1.0
    head_sum = jnp.asarray(head_sum)

    def epi_do(accs, r, c, o, p):
        do = accs[0]
        o[0][...] = do.astype(BF)
        delta = jnp.dot(do * r[0][...].astype(F32), c[0][...], preferred_element_type=F32, precision=lax.Precision.HIGHEST)
        lane = lax.broadcasted_iota(jnp.int32, delta.shape, 1)
        o[1][...] = jnp.where(lane < HEADS, r[1][...], delta)

    do, stats = _fused_mm(
        "attn_out_bwd", S, 1024, 1, [(dy_attn, D, 0, 1)], [(Wao, True, D, GW, 0, 1, 0, 0, False)], [(0, 0, 0, 0, 1)], [(1024, GW)],
        [(o_att, GW, 0, False), (lse, STAT_W, 0, False)], [head_sum], [(GW, BF, GW, 0, False), (STAT_W, F32, STAT_W, 0, False)], [], epi_do)
    dWao = _mm_tn("dw_attn_out", o_att, dy_attn, GW, D, 1024)

    tie = early_fn(dict(w_ffn_in=dWfi, w_ffn_out=dWfo, w_conv_out=dWco, w_mix_out=dWmo, w_attn_out=dWao, w_dw=dwdw))
    stats = stats + tie
    dos = [do] + [_to_residue(f"do_to_residue_g{g}", do, (0,), dils[g]) for g in (1, 2)]
    sts = [stats] + [_to_residue_stats(f"stats_to_residue_g{g}", stats, dils[g]) for g in (1, 2)]
    dqkv, dbs = [], []
    for g in range(3):
        arr, cb = qkv[g]
        dkv = _attn_bwd_dkv(f"attn_bwd_dkv_g{g}", arr, bias_kq2[g], dos[g], sts[g], cb, dils[g])
        dg, db = _attn_bwd_dq(f"attn_bwd_dq_g{g}", arr, bias_qk[g], dos[g], sts[g], dkv, cb, dils[g])
        dqkv.append(dg if g == 0 else _from_residue(f"dqkv_from_residue_g{g}", dg, dils[g]))
        dbs.append(db.reshape(HEADS, _NB))
    dtab = _bias_grad(jnp.stack(dbs), idx)[:, :, :REL_BUCKETS].reshape(3 * HEADS, REL_BUCKETS).T

    def epi_dx(accs, r, c, o, p):
        xv = r[1][...]
        d1, dg1r = _rms_bwd(xv, _rms_r(xv), c[0][...], accs[0])
        o[0][...] = r[0][...] + d1
        p[0][...] += _psum8(dg1r)

    dWg = [_mm_tn(f"dw_in_g{g}", h1, dqkv[g], D, ATTN_W, 1024) for g in range(3)]
    dW_in = jnp.concatenate(
        [t[:, 2 * GW:] for t in dWg] + [t[:, :GW] for t in dWg] + [t[:, GW:2 * GW] for t in dWg]
        + [_mm_tn("dw_in_glu", h1, dz_glu, D, D, 1024), _mm_tn("dw_in_gate", h1, dz_gate, D, D, 1024)], axis=1)
    g1_late = g1 + late_fn(dW_in)
    segs = [(dz_glu, 0, 4), (dz_gate, 4, 4), (dqkv[0], 8, 3), (dqkv[1], 11, 3), (dqkv[2], 14, 3)]
    grad_x, dg1 = _fused_mm(
        "in_proj_bwd", S, 1024, 1, [(a, GW, k0, nk) for a, k0, nk in segs], [(Wp, True, GW, D, 0, Z_CB, 0, 0, False)],
        [(t, 0, 0, k0, nk) for t, (a, k0, nk) in enumerate(segs)], [(1024, D)], [(dx1, D, 0, False), (xs, D, 0, False)], [g1_late],
        [full], [(8, D)], epi_dx)

    small = dict(rel_bias_table=dtab, g_pre_mix=dg1[0], b_glu=dbglu[0], b_dw=dbdw[0], g_conv_ln=dgln[0], b_conv_ln=dbln[0],
                 b_conv_out=dbco[0], g_post_mix=dg2[0], g_pre_ffn=dg3[0], g_post_ffn=dg4[0])
    return loss_p[0], grad_x, small


SMALL = ['rel_bias_table', 'g_pre_mix', 'b_glu', 'b_dw', 'g_conv_ln', 'b_conv_ln', 'b_conv_out', 'g_post_mix', 'g_pre_ffn',
         'g_post_ffn']
BIG = ['w_in', 'w_ffn_in', 'w_ffn_out', 'w_conv_out', 'w_mix_out', 'w_attn_out', 'w_dw']
WEIGHTS = ['rel_bias_table', 'g_pre_mix', 'w_in', 'b_glu', 'w_dw', 'b_dw', 'g_conv_ln', 'b_conv_ln', 'w_conv_out', 'b_conv_out',
           'w_attn_out', 'w_mix_out', 'g_post_mix', 'g_pre_ffn', 'w_ffn_in', 'w_ffn_out', 'g_post_ffn']
SMALL_ROWS = 16


ROW_SMALL = ['g_pre_mix', 'b_glu', 'b_dw', 'g_conv_ln', 'b_conv_ln', 'b_conv_out', 'g_post_mix', 'g_pre_ffn', 'g_post_ffn']
LOSS_ROW = 10
TAB_LANES = 128


def _small_rows(small, loss_row):
    rows = [small[n].reshape(-1, D) for n in ROW_SMALL] + [loss_row.reshape(1, D)]
    n = sum(r.shape[0] for r in rows)
    return jnp.concatenate(rows + [jnp.zeros((SMALL_ROWS - n, D), F32)], axis=0)


def _adamw_small(recv_rows, recv_tab, ws, ms, vs):
    np_ = len(SMALL)

    def body(*refs):
        rr, rt = refs[0], refs[1]
        w_refs, m_refs, v_refs = refs[2:2 + np_], refs[2 + np_:2 + 2 * np_], refs[2 + 2 * np_:2 + 3 * np_]
        loss_ref = refs[2 + 3 * np_]
        outs = refs[3 + 3 * np_:]
        rows = rr[0]
        tab = rt[0]
        for s_ in range(1, NDEV):
            rows = rows + rr[s_]
            tab = tab + rt[s_]
        loss_ref[...] = jnp.sum(rows[LOSS_ROW:LOSS_ROW + 1, :], axis=1, keepdims=True) * (0.5 / D)
        row = 0
        for p, n in enumerate(SMALL):
            if n == 'rel_bias_table':
                g = tab[:, 0:3 * HEADS]
            else:
                k = w_refs[p].shape[1] // D
                g = rows[row:row + 1, :] if k == 1 else jnp.concatenate([rows[row + t:row + t + 1, :] for t in range(k)], axis=1)
                row += k
            nm = ADAM_B1 * m_refs[p][...] + (1.0 - ADAM_B1) * g
            nv = ADAM_B2 * v_refs[p][...] + (1.0 - ADAM_B2) * (g * g)
            outs[4 * p][...] = g
            outs[4 * p + 1][...] = -ADAM_LR * ((nm / _C1) / (jnp.sqrt(nv / _C2) + ADAM_EPS) + ADAM_WD * w_refs[p][...])
            outs[4 * p + 2][...] = nm
            outs[4 * p + 3][...] = nv

    out_shape = [jax.ShapeDtypeStruct((1, 1), F32)]
    for a_ in ws:
        out_shape += [jax.ShapeDtypeStruct(a_.shape, F32)] * 4
    res = pl.pallas_call(body, out_shape=out_shape, compiler_params=_cp(), name="adamw_small")(recv_rows, recv_tab, *ws, *ms, *vs)
    return res[0], [tuple(res[1 + 4 * p:5 + 4 * p]) for p in range(np_)]


def _cols_to_blocks(a):
    R = a.shape[0]
    return a.reshape(R, NDEV, a.shape[1] // NDEV).transpose(1, 0, 2)


def _blocks_to_cols(a):
    return a.transpose(1, 0, 2).reshape(a.shape[1], NDEV * a.shape[2])


def kernel(x, rel_bias_table, g_pre_mix, w_in, b_glu, w_dw, b_dw, g_conv_ln, b_conv_ln, w_conv_out, b_conv_out, w_attn_out, w_mix_out, g_post_mix, g_pre_ffn, w_ffn_in, w_ffn_out, g_post_ffn, loss_target, m_rel_bias_table, m_g_pre_mix, m_w_in, m_b_glu, m_w_dw, m_b_dw, m_g_conv_ln, m_b_conv_ln, m_w_conv_out, m_b_conv_out, m_w_attn_out, m_w_mix_out, m_g_post_mix, m_g_pre_ffn, m_w_ffn_in, m_w_ffn_out, m_g_post_ffn, v_rel_bias_table, v_g_pre_mix, v_w_in, v_b_glu, v_w_dw, v_b_dw, v_g_conv_ln, v_b_conv_ln, v_w_conv_out, v_b_conv_out, v_w_attn_out, v_w_mix_out, v_g_post_mix, v_g_pre_ffn, v_w_ffn_in, v_w_ffn_out, v_g_post_ffn):
    w = dict(rel_bias_table=rel_bias_table, g_pre_mix=g_pre_mix, w_in=w_in, b_glu=b_glu, w_dw=w_dw, b_dw=b_dw, g_conv_ln=g_conv_ln, b_conv_ln=b_conv_ln, w_conv_out=w_conv_out, b_conv_out=b_conv_out, w_attn_out=w_attn_out, w_mix_out=w_mix_out, g_post_mix=g_post_mix, g_pre_ffn=g_pre_ffn, w_ffn_in=w_ffn_in, w_ffn_out=w_ffn_out, g_post_ffn=g_post_ffn)
    m = dict(rel_bias_table=m_rel_bias_table, g_pre_mix=m_g_pre_mix, w_in=m_w_in, b_glu=m_b_glu, w_dw=m_w_dw, b_dw=m_b_dw, g_conv_ln=m_g_conv_ln, b_conv_ln=m_b_conv_ln, w_conv_out=m_w_conv_out, b_conv_out=m_b_conv_out, w_attn_out=m_w_attn_out, w_mix_out=m_w_mix_out, g_post_mix=m_g_post_mix, g_pre_ffn=m_g_pre_ffn, w_ffn_in=m_w_ffn_in, w_ffn_out=m_w_ffn_out, g_post_ffn=m_g_post_ffn)
    v = dict(rel_bias_table=v_rel_bias_table, g_pre_mix=v_g_pre_mix, w_in=v_w_in, b_glu=v_b_glu, w_dw=v_w_dw, b_dw=v_b_dw, g_conv_ln=v_g_conv_ln, b_conv_ln=v_b_conv_ln, w_conv_out=v_w_conv_out, b_conv_out=v_b_conv_out, w_attn_out=v_w_attn_out, w_mix_out=v_w_mix_out, g_post_mix=v_g_post_mix, g_pre_ffn=v_g_pre_ffn, w_ffn_in=v_w_ffn_in, w_ffn_out=v_w_ffn_out, g_post_ffn=v_g_post_ffn)

    def shard2d(d, n):
        a = d[n][0]
        return jnp.pad(a, ((0, HALO - CONV_W), (0, 0))) if n == 'w_dw' else a

    own = {n: shard2d(w, n).astype(F32 if n == 'w_dw' else BF) for n in BIG}
    packed = [n for n in BIG if n not in ('w_in', 'w_dw')]
    shapes = [own[n].shape for n in packed]

    def pack(arrs, lead):
        return jnp.concatenate([a.reshape(lead + (-1, D)) for a in arrs], axis=len(lead))

    def unpack(p):
        out, pos = {}, 0
        for n, shp in zip(packed, shapes):
            rows = shp[0] * shp[1] // D
            out[n] = p[:, pos:pos + rows].reshape((NDEV,) + shp)
            pos += rows
        return out

    (g_in,) = _all_gather("gather_w_in", [own['w_in']])
    rest_own = [pack([own[n] for n in packed], ()), own['w_dw']]
    g_in, rest_own = lax.optimization_barrier((g_in, rest_own))
    gather_rest = _exchange_start("gather_rest_start", rest_own, True)
    W_in = _blocks_to_cols(g_in)
    kvq = [W_in[:, t * ATTN_W + g * GW:t * ATTN_W + (g + 1) * GW] for g in range(3) for t in (1, 2, 0)]
    Wp = jnp.concatenate([W_in[:, 3 * ATTN_W:]] + kvq, axis=1)

    def rest_fn(after):
        lands = _exchange_wait("gather_rest_wait", gather_rest, after, True)
        gw = unpack(_set_own_slot(lands[0], rest_own[0]))
        gw['w_dw'] = _set_own_slot(lands[1], rest_own[1])
        return (_blocks_to_cols(gw['w_ffn_in']), gw['w_ffn_out'].reshape(FFN, D), gw['w_conv_out'].reshape(D, D),
                gw['w_mix_out'].reshape(D, D), _blocks_to_cols(gw['w_attn_out']), _blocks_to_cols(gw['w_dw']))

    def to_blocks(n, g):
        if n in ('w_in', 'w_ffn_in', 'w_attn_out', 'w_dw'):
            return _cols_to_blocks(g)
        return g.reshape(NDEV, g.shape[0] // NDEV, g.shape[1])

    started = {}

    def early_fn(grads):
        blocks = [pack([to_blocks(n, grads[n]) for n in packed], (NDEV,)), to_blocks('w_dw', grads['w_dw'])]
        started['blocks'] = blocks
        started['handle'] = _exchange_start("scatter_early_start", blocks, False)
        return started['handle'][4][0:1, 0:1]

    def late_fn(dW_in):
        started['in_blocks'] = [to_blocks('w_in', dW_in)]
        started['in_handle'] = _exchange_start("scatter_w_in_start", started['in_blocks'], False)
        return started['in_handle'][4][0:1, 0:1]

    g1_tied = g_pre_mix[0] + gather_rest[4][0, 0:1]
    loss_row, grad_x, small = _local_step(
        x[0], loss_target[0], Wp, rest_fn, early_fn, late_fn, rel_bias_table, g1_tied, b_glu[0], b_dw[0], g_conv_ln[0],
        b_conv_ln[0], b_conv_out[0], g_post_mix[0], g_pre_ffn[0], g_post_ffn[0])

    lands = _exchange_wait("scatter_early_wait", started['handle'], grad_x, False)
    lands = [_set_own_slot(l, _own_block(b)) for l, b in zip(lands, started['blocks'])]
    recv = unpack(lands[0])
    recv['w_dw'] = lands[1]
    (land_in,) = _exchange_wait("scatter_w_in_wait", started['in_handle'], grad_x, False)
    recv['w_in'] = _set_own_slot(land_in, _own_block(started['in_blocks'][0]))
    tiles = dict(w_in=128, w_ffn_in=256, w_ffn_out=176, w_conv_out=128, w_mix_out=128, w_attn_out=512, w_dw=HALO)
    res = {}
    for n in BIG:
        g_, d_, nm_, nv_ = _adamw("adamw_" + n, shard2d(w, n), shard2d(m, n), shard2d(v, n), recv[n], tiles[n])
        if n == 'w_dw':
            g_, d_, nm_, nv_ = (t[:CONV_W] for t in (g_, d_, nm_, nv_))
        res[n] = tuple(t[None] for t in (g_, d_, nm_, nv_))

    tab = jnp.pad(small['rel_bias_table'], ((0, 0), (0, TAB_LANES - 3 * HEADS)))
    srows, stab = _all_gather("gather_small_grads", [_small_rows(small, loss_row), tab])
    loss11, small_res = _adamw_small(srows, stab, [w[n] for n in SMALL], [m[n] for n in SMALL], [v[n] for n in SMALL])
    loss = loss11.reshape(())
    for n, r in zip(SMALL, small_res):
        res[n] = r
    return (loss, grad_x[None], *[res[n][0] for n in WEIGHTS], *[res[n][1] for n in WEIGHTS],
            *[res[n][2] for n in WEIGHTS], *[res[n][3] for n in WEIGHTS])
```

```python
import functools
import math

import numpy as np
import jax
import jax.numpy as jnp
from jax import lax
from jax.experimental import pallas as pl
from jax.experimental.pallas import tpu as pltpu

F32 = jnp.float32
BF = jnp.bfloat16

D = 1024
HEAD_DIM = 64
HEADS = 8
GROUPS = ((128, 1), (512, 4), (2048, 16))
QBLK = 128
GW = HEADS * HEAD_DIM
ATTN_W = 3 * GW
REL_BUCKETS = 32
REL_MAX_DISTANCE = 2048
CONV_W = 31
HALO = 32
FFN = 2816
IN_W = 3 * ATTN_W + 2 * D + 2 * D
RMS_EPS = 1e-6
LN_EPS = 1e-5
NEG_INF = -1e30
SCALE = HEAD_DIM ** -0.5
NDEV = 8

ADAM_LR = 0.001
ADAM_B1 = 0.9
ADAM_B2 = 0.999
ADAM_EPS = 1e-08
ADAM_WD = 0.01
ADAM_STEP = 10

Z_G0 = 4096 // GW
Z_CB = IN_W // GW


def _kvq_blocks(g):
    return (Z_G0 + 3 * g, Z_G0 + 3 * g + 1, Z_G0 + 3 * g + 2)


VMEM_LIMIT = 52 * 1024 * 1024


def _cp(sem=None):
    if sem is None:
        return pltpu.CompilerParams(vmem_limit_bytes=VMEM_LIMIT)
    return pltpu.CompilerParams(vmem_limit_bytes=VMEM_LIMIT, dimension_semantics=sem)


def _sig(v):
    return jax.nn.sigmoid(v)


def _psum8(v):
    return v.reshape(v.shape[0] // 8, 8, v.shape[1]).sum(axis=0)


def _rms_r(v):
    return lax.rsqrt(jnp.mean(v * v, axis=-1, keepdims=True) + RMS_EPS)


def _rms_bwd(v, r, g, dy):
    gy = dy * g
    dv = r * gy - v * (r * r * r) * jnp.mean(v * gy, axis=-1, keepdims=True)
    return dv, dy * v * r


def _clip_k(k, k0, nk):
    return jnp.clip(k - k0, 0, nk - 1)


def _fused_mm(name, M, tm, grid_n, a_ops, b_ops, terms, acc_shapes, rows, consts, outs, parts, epilogue):
    gm = M // tm
    nk_total = max([t[3] + t[4] for t in terms], default=1)
    n_a, n_b, n_r, n_c, n_o, n_p = len(a_ops), len(b_ops), len(rows), len(consts), len(outs), len(parts)
    n_acc = len(acc_shapes)
    use_scratch = nk_total > 1
    if parts:
        assert grid_n == 1

    def jj(j, follow):
        return j if follow else 0

    in_specs, args = [], []
    for (arr, tk, k0, nk) in a_ops:
        in_specs.append(pl.BlockSpec((tm, tk), functools.partial(lambda i, j, k, k0, nk: (i, _clip_k(k, k0, nk)), k0=k0, nk=nk)))
        args.append(arr)
    for (arr, nt, tk, tn, k0, nk, koff, joff, fj) in b_ops:
        if nt:
            in_specs.append(pl.BlockSpec((tn, tk), functools.partial(
                lambda i, j, k, k0, nk, koff, joff, fj: (joff + jj(j, fj), _clip_k(k, k0, nk) + koff),
                k0=k0, nk=nk, koff=koff, joff=joff, fj=fj)))
        else:
            in_specs.append(pl.BlockSpec((tk, tn), functools.partial(
                lambda i, j, k, k0, nk, koff, joff, fj: (_clip_k(k, k0, nk) + koff, joff + jj(j, fj)),
                k0=k0, nk=nk, koff=koff, joff=joff, fj=fj)))
        args.append(arr)
    for (arr, w, off, fj) in rows:
        in_specs.append(pl.BlockSpec((tm, w), functools.partial(lambda i, j, k, off, fj: (i, off + jj(j, fj)), off=off, fj=fj)))
        args.append(arr)
    for arr in consts:
        in_specs.append(pl.BlockSpec(arr.shape, functools.partial(lambda i, j, k, nd: (0,) * nd, nd=arr.ndim)))
        args.append(arr)
    out_specs, out_shape = [], []
    for (ncols, dt, w, off, fj) in outs:
        out_specs.append(pl.BlockSpec((tm, w), functools.partial(lambda i, j, k, off, fj: (i, off + jj(j, fj)), off=off, fj=fj)))
        out_shape.append(jax.ShapeDtypeStruct((M, ncols), dt))
    for (r, c) in parts:
        out_specs.append(pl.BlockSpec((r, c), lambda i, j, k: (0, 0)))
        out_shape.append(jax.ShapeDtypeStruct((r, c), F32))
    scratch = [pltpu.VMEM(s, F32) for s in acc_shapes] if use_scratch else []

    def body(*refs):
        pos = 0
        a_refs = refs[pos:pos + n_a]; pos += n_a
        b_refs = refs[pos:pos + n_b]; pos += n_b
        r_refs = refs[pos:pos + n_r]; pos += n_r
        c_refs = refs[pos:pos + n_c]; pos += n_c
        o_refs = refs[pos:pos + n_o]; pos += n_o
        p_refs = refs[pos:pos + n_p]; pos += n_p
        acc_refs = refs[pos:pos + n_acc] if use_scratch else ()
        i = pl.program_id(0)
        k = pl.program_id(2)

        def dot_of(ai, bi):
            a = a_refs[ai][...].astype(BF)
            b = b_refs[bi][...].astype(BF)
            if b_ops[bi][1]:
                return lax.dot_general(a, b, (((1,), (1,)), ((), ())), preferred_element_type=F32)
            return jnp.dot(a, b, preferred_element_type=F32)

        if parts:
            @pl.when((i == 0) & (k == 0))
            def _():
                for p in p_refs:
                    p[...] = jnp.zeros(p.shape, F32)

        def finish(accs):
            epilogue(accs, r_refs, c_refs, o_refs, p_refs)
            if parts:
                @pl.when(i == gm - 1)
                def _():
                    for p in p_refs:
                        p[0:1, :] = jnp.sum(p[...], axis=0, keepdims=True)

        if not use_scratch:
            accs = [None] * n_acc
            for (ai, bi, ci, k0, nk) in terms:
                d = dot_of(ai, bi)
                accs[ci] = d if accs[ci] is None else accs[ci] + d
            finish(accs)
        else:
            @pl.when(k == 0)
            def _():
                for acc in acc_refs:
                    acc[...] = jnp.zeros(acc.shape, F32)

            for (ai, bi, ci, k0, nk) in terms:
                def do(ai=ai, bi=bi, ci=ci):
                    acc_refs[ci][...] += dot_of(ai, bi)
                if k0 == 0 and nk == nk_total:
                    do()
                else:
                    pl.when((k >= k0) & (k < k0 + nk))(do)

            @pl.when(k == nk_total - 1)
            def _():
                finish([acc[...] for acc in acc_refs])

    res = pl.pallas_call(
        body, grid=(gm, grid_n, nk_total), in_specs=in_specs, out_specs=out_specs, out_shape=out_shape,
        scratch_shapes=scratch, compiler_params=_cp(("arbitrary", "arbitrary", "arbitrary")), name=name,
    )(*args)
    return res


def _mm_tn(name, a, b, tm, tn, tk):
    S, Ka = a.shape
    Nb = b.shape[1]
    nk = S // tk

    def body(a_ref, b_ref, o_ref, acc):
        k = pl.program_id(2)

        @pl.when(k == 0)
        def _():
            acc[...] = jnp.zeros(acc.shape, F32)

        acc[...] += lax.dot_general(a_ref[...], b_ref[...], (((0,), (0,)), ((), ())), preferred_element_type=F32)

        @pl.when(k == nk - 1)
        def _():
            o_ref[...] = acc[...].astype(o_ref.dtype)

    return pl.pallas_call(
        body, grid=(Ka // tm, Nb // tn, nk),
        in_specs=[pl.BlockSpec((tk, tm), lambda i, j, k: (k, i)), pl.BlockSpec((tk, tn), lambda i, j, k: (k, j))],
        out_specs=pl.BlockSpec((tm, tn), lambda i, j, k: (i, j)),
        out_shape=jax.ShapeDtypeStruct((Ka, Nb), BF),
        scratch_shapes=[pltpu.VMEM((tm, tn), F32)],
        compiler_params=_cp(("parallel", "parallel", "arbitrary")), name=name,
    )(a, b)


def _rel_bucket_np(dist):
    max_exact = REL_BUCKETS // 2
    d = np.maximum(dist, 0)
    df = np.maximum(d, 1).astype(np.float32)
    large = max_exact + (np.log(df / np.float32(max_exact)) / np.float32(math.log(REL_MAX_DISTANCE / max_exact))
                         * np.float32(REL_BUCKETS - max_exact)).astype(np.int32)
    large = np.minimum(large, REL_BUCKETS - 1)
    return np.where(d < max_exact, d, large).astype(np.int32)


N_LAYOUTS = 2


def _band_index():
    idx = np.zeros((N_LAYOUTS * 3, 1, QBLK * 2 * QBLK), np.int32)
    for g, (window, dil) in enumerate(GROUPS):
        span = window // dil
        k = np.arange(2 * QBLK)[:, None]; q = np.arange(QBLK)[None, :]
        off = q - k + QBLK
        idx[g, 0] = np.where((off >= 0) & (off <= span), _rel_bucket_np(off * dil), -1).reshape(-1)
        k = np.arange(QBLK)[:, None]; q = np.arange(2 * QBLK)[None, :]
        off = q - k
        idx[3 + g, 0] = np.where((off >= 0) & (off <= span), _rel_bucket_np(off * dil), -1).reshape(-1)
    return idx


_NB = QBLK * 2 * QBLK
_BCH = 4096


def _bias_build(tab_t, idx):
    def body(t_ref, i_ref, o_ref):
        ix = i_ref[0]
        t = t_ref[0]
        acc = jnp.full((HEADS, _BCH), NEG_INF, F32)
        for b in range(REL_BUCKETS):
            acc = jnp.where(ix == b, t[:, b:b + 1], acc)
        o_ref[0] = acc

    return pl.pallas_call(
        body, grid=(N_LAYOUTS * 3, _NB // _BCH),
        in_specs=[pl.BlockSpec((1, HEADS, REL_BUCKETS), lambda l, n: (l % 3, 0, 0)),
                  pl.BlockSpec((1, 1, _BCH), lambda l, n: (l, 0, n))],
        out_specs=pl.BlockSpec((1, HEADS, _BCH), lambda l, n: (l, 0, n)),
        out_shape=jax.ShapeDtypeStruct((N_LAYOUTS * 3, HEADS, _NB), F32), compiler_params=_cp(), name="bias_build",
    )(tab_t, idx)


def _bias_grad(ds, idx):
    nch = _NB // _BCH

    def body(d_ref, i_ref, o_ref):
        n = pl.program_id(1)

        @pl.when(n == 0)
        def _():
            o_ref[...] = jnp.zeros(o_ref.shape, F32)

        ix = i_ref[0]
        d = d_ref[0]
        lane = lax.broadcasted_iota(jnp.int32, (HEADS, 128), 1)
        acc = jnp.zeros((HEADS, 128), F32)
        for b in range(REL_BUCKETS):
            s = jnp.sum(jnp.where(ix == b, d, 0.0), axis=1, keepdims=True)
            acc = acc + jnp.where(lane == b, s, 0.0)
        o_ref[0] += acc

    return pl.pallas_call(
        body, grid=(3, nch),
        in_specs=[pl.BlockSpec((1, HEADS, _BCH), lambda l, n: (l, 0, n)),
                  pl.BlockSpec((1, 1, _BCH), lambda l, n: (3 + l, 0, n))],
        out_specs=pl.BlockSpec((1, HEADS, 128), lambda l, n: (l, 0, 0)),
        out_shape=jax.ShapeDtypeStruct((3, HEADS, 128), F32), compiler_params=_cp(), name="bias_grad",
    )(ds, idx)


PT = 256
PSTEP = 1024
STAT_W = 128


def _perm_np(dil):
    p = np.zeros((PT, PT), np.float32)
    m = np.arange(PT // dil)
    for c in range(dil):
        p[c * (PT // dil) + m, m * dil + c] = 1.0
    return p


def _perm_const(dil, dtype, inverse):
    p = _perm_np(dil)
    return jnp.asarray(p.T if inverse else p, dtype)


def _apply_perm(p, x):
    if x.dtype == F32:
        return jnp.dot(p, x, preferred_element_type=F32, precision=lax.Precision.HIGHEST)
    return jnp.dot(p, x, preferred_element_type=F32)


def _to_residue(name, arr, col_blocks, dil):
    S = arr.shape[0]
    nc = len(col_blocks)
    p = _perm_const(dil, arr.dtype, False)
    sub = PT // dil

    def body(*refs):
        p_ref, ins, o_ref = refs[0], refs[1:1 + nc], refs[1 + nc]
        for u in range(PSTEP // PT):
            for t, r in enumerate(ins):
                y = _apply_perm(p_ref[...], r[u * PT:(u + 1) * PT, :]).astype(o_ref.dtype)
                o_ref[:, u * sub:(u + 1) * sub, t * GW:(t + 1) * GW] = y.reshape(dil, sub, GW)

    out = pl.pallas_call(
        body, grid=(S // PSTEP,),
        in_specs=[pl.BlockSpec((PT, PT), lambda i: (0, 0))]
                 + [pl.BlockSpec((PSTEP, GW), functools.partial(lambda i, cb: (i, cb), cb=cb)) for cb in col_blocks],
        out_specs=pl.BlockSpec((dil, PSTEP // dil, nc * GW), lambda i: (0, i, 0)),
        out_shape=jax.ShapeDtypeStruct((dil, S // dil, nc * GW), arr.dtype), compiler_params=_cp(), name=name,
    )(p, *([arr] * nc))
    return out.reshape(S, nc * GW)


def _to_residue_stats(name, arr, dil):
    S = arr.shape[0]
    p = _perm_const(dil, F32, False)
    sub = PT // dil

    def body(p_ref, x_ref, o_ref):
        for u in range(PSTEP // PT):
            y = _apply_perm(p_ref[...], x_ref[u * PT:(u + 1) * PT, :])
            o_ref[:, u * sub:(u + 1) * sub, :] = y.reshape(dil, sub, STAT_W)

    out = pl.pallas_call(
        body, grid=(S // PSTEP,),
        in_specs=[pl.BlockSpec((PT, PT), lambda i: (0, 0)), pl.BlockSpec((PSTEP, STAT_W), lambda i: (i, 0))],
        out_specs=pl.BlockSpec((dil, PSTEP // dil, STAT_W), lambda i: (0, i, 0)),
        out_shape=jax.ShapeDtypeStruct((dil, S // dil, STAT_W), F32), compiler_params=_cp(), name=name,
    )(p, arr)
    return out.reshape(S, STAT_W)


def _from_residue(name, arr, dil):
    S, W = arr.shape
    p = _perm_const(dil, arr.dtype, True)
    sub = PT // dil

    def body(p_ref, x_ref, o_ref):
        for u in range(PSTEP // PT):
            x = x_ref[:, u * sub:(u + 1) * sub, :].reshape(PT, W)
            o_ref[u * PT:(u + 1) * PT, :] = _apply_perm(p_ref[...], x).astype(o_ref.dtype)

    return pl.pallas_call(
        body, grid=(S // PSTEP,),
        in_specs=[pl.BlockSpec((PT, PT), lambda i: (0, 0)), pl.BlockSpec((dil, PSTEP // dil, W), lambda i: (0, i, 0))],
        out_specs=pl.BlockSpec((PSTEP, W), lambda i: (i, 0)),
        out_shape=jax.ShapeDtypeStruct((S, W), arr.dtype), compiler_params=_cp(), name=name,
    )(p, arr.reshape(dil, S // dil, W))


NT_DIMS = (((1,), (1,)), ((), ()))
TN_DIMS = (((0,), (0,)), ((), ()))


def _attn_dims(S, dil):
    L = S // dil
    TQ = min(512, L)
    return L, TQ, L // TQ, TQ // QBLK


def _attn_specs(S, dil):
    L, TQ, nq, nsub = _attn_dims(S, dil)
    nb = L // QBLK
    cur = lambda cb, w=GW: pl.BlockSpec((TQ, w), lambda c, i: (c * nq + i, cb))
    prev = lambda cb, w=GW: pl.BlockSpec((QBLK, w), lambda c, i: (c * nb + jnp.maximum(i * nsub - 1, 0), cb))
    nxt = lambda cb, w=GW: pl.BlockSpec((QBLK, w), lambda c, i: (c * nb + jnp.minimum((i + 1) * nsub, nb - 1), cb))
    band = lambda r, c_: pl.BlockSpec((HEADS, r, c_), lambda c, i: (0, 0, 0))
    return L, TQ, nq, nsub, cur, prev, nxt, band


def _fill(buf, first_ref, second_ref):
    n = first_ref.shape[0]
    buf[0:n, :] = first_ref[...]
    buf[n:n + second_ref.shape[0], :] = second_ref[...]


def _attn_fwd(name, arr, bias_kq, cb, dil):
    S = arr.shape[0]
    kcb, vcb, qcb = cb
    L, TQ, nq, nsub, cur, prev, nxt, band = _attn_specs(S, dil)

    def body(q_ref, kc_ref, kp_ref, vc_ref, vp_ref, b_ref, o_ref, l_ref, kbuf, vbuf):
        i = pl.program_id(1)
        _fill(kbuf, kp_ref, kc_ref)
        _fill(vbuf, vp_ref, vc_ref)
        row = lax.broadcasted_iota(jnp.int32, (2 * QBLK, QBLK), 0)
        first = (row >= QBLK) | (i > 0)
        for j in range(nsub):
            rs = slice(j * QBLK, (j + 1) * QBLK)
            ks = slice(j * QBLK, (j + 2) * QBLK)
            lrows = []
            for h in range(HEADS):
                hs = slice(h * HEAD_DIM, (h + 1) * HEAD_DIM)
                s = lax.dot_general(kbuf[ks, hs], q_ref[rs, hs], NT_DIMS, preferred_element_type=F32) * SCALE + b_ref[h]
                if j == 0:
                    s = jnp.where(first, s, NEG_INF)
                m = jnp.max(s, axis=0, keepdims=True)
                p = jnp.exp(s - m)
                den = jnp.sum(p, axis=0, keepdims=True)
                o_t = lax.dot_general(vbuf[ks, hs], p.astype(BF), TN_DIMS, preferred_element_type=F32) / den
                o_ref[rs, hs] = o_t.T.astype(BF)
                lrows.append(m + jnp.log(den))
            lt = jnp.concatenate(lrows + [jnp.zeros((STAT_W - HEADS, QBLK), F32)], axis=0)
            l_ref[rs, :] = lt.T

    return pl.pallas_call(
        body, grid=(dil, nq),
        in_specs=[cur(qcb), cur(kcb), prev(kcb), cur(vcb), prev(vcb), band(2 * QBLK, QBLK)],
        out_specs=[cur(0), cur(0, STAT_W)],
        out_shape=[jax.ShapeDtypeStruct((S, GW), BF), jax.ShapeDtypeStruct((S, STAT_W), F32)],
        scratch_shapes=[pltpu.VMEM((QBLK + TQ, GW), BF), pltpu.VMEM((QBLK + TQ, GW), BF)],
        compiler_params=_cp(), name=name,
    )(arr, arr, arr, arr, arr, bias_kq)


def _attn_bwd(name, arr, bias_kq2, do, stats, cb, dil):
    S = arr.shape[0]
    kcb, vcb, qcb = cb
    L, TQ, nq, nsub, cur, prev, nxt, band = _attn_specs(S, dil)

    def body(k_ref, v_ref, qc_ref, qn_ref, b_ref, doc_ref, don_ref, sc_ref, sn_ref, o_ref, db_ref, qbuf, dobuf, sbuf, carry):
        c = pl.program_id(0)
        i = pl.program_id(1)

        @pl.when((c == 0) & (i == 0))
        def _():
            db_ref[...] = jnp.zeros(db_ref.shape, F32)
            carry[...] = jnp.zeros(carry.shape, F32)

        _fill(qbuf, qc_ref, qn_ref)
        _fill(dobuf, doc_ref, don_ref)
        for j in range(nsub + 1):
            rs = slice(j * QBLK, (j + 1) * QBLK)
            sbuf[:, rs] = (sc_ref[rs, :] if j < nsub else sn_ref[...]).T
        col = lax.broadcasted_iota(jnp.int32, (QBLK, 2 * QBLK), 1)
        last = (col < QBLK) | (i < nq - 1)
        for h in range(HEADS):
            hs = slice(h * HEAD_DIM, (h + 1) * HEAD_DIM)
            bias_h = b_ref[h]
            db = jnp.zeros((QBLK, 2 * QBLK), F32)
            tail = carry[:, hs]
            for j in range(nsub):
                rs = slice(j * QBLK, (j + 1) * QBLK)
                qs = slice(j * QBLK, (j + 2) * QBLK)
                qq = qbuf[qs, hs]
                dd = dobuf[qs, hs]
                kk = k_ref[rs, hs]
                s = lax.dot_general(kk, qq, NT_DIMS, preferred_element_type=F32) * SCALE + bias_h
                if j == nsub - 1:
                    s = jnp.where(last, s, NEG_INF)
                p = jnp.exp(s - sbuf[h:h + 1, qs])
                dp = lax.dot_general(v_ref[rs, hs], dd, NT_DIMS, preferred_element_type=F32)
                ds = p * (dp - sbuf[HEADS + h:HEADS + h + 1, qs])
                db = db + ds
                dsb = ds.astype(BF)
                o_ref[rs, h * HEAD_DIM:(h + 1) * HEAD_DIM] = (jnp.dot(dsb, qq, preferred_element_type=F32) * SCALE).astype(BF)
                o_ref[rs, GW + h * HEAD_DIM:GW + (h + 1) * HEAD_DIM] = jnp.dot(p.astype(BF), dd, preferred_element_type=F32).astype(BF)
                dqw = lax.dot_general(dsb, kk, TN_DIMS, preferred_element_type=F32) * SCALE
                o_ref[rs, 2 * GW + h * HEAD_DIM:2 * GW + (h + 1) * HEAD_DIM] = (dqw[0:QBLK] + tail).astype(BF)
                tail = dqw[QBLK:2 * QBLK]
            carry[:, hs] = tail
            db_ref[h] += db

    return pl.pallas_call(
        body, grid=(dil, nq),
        in_specs=[cur(kcb), cur(vcb), cur(qcb), nxt(qcb), band(QBLK, 2 * QBLK),
                  cur(0), nxt(0), cur(0, STAT_W), nxt(0, STAT_W)],
        out_specs=[cur(0, ATTN_W), band(QBLK, 2 * QBLK)],
        out_shape=[jax.ShapeDtypeStruct((S, ATTN_W), BF), jax.ShapeDtypeStruct((HEADS, QBLK, 2 * QBLK), F32)],
        scratch_shapes=[pltpu.VMEM((TQ + QBLK, GW), BF), pltpu.VMEM((TQ + QBLK, GW), BF), pltpu.VMEM((STAT_W, TQ + QBLK), F32),
                        pltpu.VMEM((QBLK, GW), F32)],
        compiler_params=_cp(("arbitrary", "arbitrary")), name=name,
    )(arr, arr, arr, arr, bias_kq2, do, do, stats, stats)


def _head_expand():
    e = np.zeros((STAT_W, GW), np.float32)
    for h in range(HEADS):
        e[h, h * HEAD_DIM:(h + 1) * HEAD_DIM] = 1.0
    return e


def _attn_merge(os_, ls_, S):
    dils = [d for _, d in GROUPS]
    pb = [_perm_const(d, BF, True) for d in dils[1:]]
    pf = [_perm_const(d, F32, True) for d in dils[1:]]
    expand = jnp.asarray(_head_expand())

    def body(o0, o1, o2, l0, l1, l2, pb1, pb2, pf1, pf2, e_ref, o_ref, l_ref):
        for u in range(PSTEP // PT):
            rs = slice(u * PT, (u + 1) * PT)
            res = lambda r, d: r[:, u * (PT // d):(u + 1) * (PT // d), :].reshape(PT, r.shape[2])
            ov = [o0[rs, :].astype(F32), _apply_perm(pb1[...], res(o1, dils[1])), _apply_perm(pb2[...], res(o2, dils[2]))]
            lv = [l0[rs, :], _apply_perm(pf1[...], res(l1, dils[1])), _apply_perm(pf2[...], res(l2, dils[2]))]
            m = jnp.maximum(jnp.maximum(lv[0], lv[1]), lv[2])
            ev = [jnp.exp(l - m) for l in lv]
            den = ev[0] + ev[1] + ev[2]
            acc = jnp.zeros((PT, GW), F32)
            for g in range(3):
                wide = jnp.dot(ev[g] / den, e_ref[...], preferred_element_type=F32, precision=lax.Precision.HIGHEST)
                acc = acc + wide * ov[g]
            o_ref[rs, :] = acc.astype(BF)
            l_ref[rs, :] = m + jnp.log(den)

    nat = lambda w: pl.BlockSpec((PSTEP, w), lambda i: (i, 0))
    res = lambda d, w: pl.BlockSpec((d, PSTEP // d, w), lambda i: (0, i, 0))
    cst = lambda a: pl.BlockSpec(a.shape, lambda i: (0, 0))
    args = [os_[0], os_[1].reshape(dils[1], S // dils[1], GW), os_[2].reshape(dils[2], S // dils[2], GW),
            ls_[0], ls_[1].reshape(dils[1], S // dils[1], STAT_W), ls_[2].reshape(dils[2], S // dils[2], STAT_W),
            pb[0], pb[1], pf[0], pf[1], expand]
    return pl.pallas_call(
        body, grid=(S // PSTEP,),
        in_specs=[nat(GW), res(dils[1], GW), res(dils[2], GW), nat(STAT_W), res(dils[1], STAT_W), res(dils[2], STAT_W)]
                 + [cst(a) for a in args[6:]],
        out_specs=[nat(GW), nat(STAT_W)],
        out_shape=[jax.ShapeDtypeStruct((S, GW), BF), jax.ShapeDtypeStruct((S, STAT_W), F32)],
        compiler_params=_cp(), name="attn_merge",
    )(*args)


CT = 256
CBUF = HALO + CT + 8
RG = 4


def _ln_hat(u1):
    mu = jnp.mean(u1, axis=-1, keepdims=True)
    xc = u1 - mu
    rstd = lax.rsqrt(jnp.mean(xc * xc, axis=-1, keepdims=True) + LN_EPS)
    return xc * rstd, rstd


def _glu_window(hu_ref, hg_ref, huh_ref, hgh_ref, bglu_ref, buf_ref, i):
    bu = bglu_ref[:, 0:D]
    bg = bglu_ref[:, D:2 * D]
    uh = (huh_ref[...].astype(F32) + bu) * _sig(hgh_ref[...].astype(F32) + bg)
    buf_ref[0:HALO, :] = jnp.where(i > 0, uh, 0.0)
    a = hu_ref[...].astype(F32) + bu
    s = _sig(hg_ref[...].astype(F32) + bg)
    buf_ref[HALO:HALO + CT, :] = a * s
    buf_ref[HALO + CT:CBUF, :] = jnp.zeros((8, D), F32)
    return a, s


def _shift_copies(buf_ref, sh_ref):
    for r in range(8):
        sh_ref[r] = buf_ref[r:r + HALO + CT, :]


def _tap_rows(wb_ref, w_ref):
    for j in range(CONV_W):
        wb_ref[j * 8:(j + 1) * 8, :] = jnp.broadcast_to(w_ref[j:j + 1, :], (8, D))


def _conv_taps(sh_ref, wb_ref, out_ref, init, offset):
    for rg in range(CT // (8 * RG)):
        accs = [init] * RG
        for j in range(CONV_W):
            off = offset(j)
            wj = wb_ref[j * 8:(j + 1) * 8, :]
            for q in range(RG):
                row = 8 * (rg * RG + q + off // 8)
                accs[q] = accs[q] + wj * sh_ref[off % 8, row:row + 8, :]
        for q in range(RG):
            out_ref[(rg * RG + q) * 8:(rg * RG + q + 1) * 8, :] = accs[q]


def _conv_specs(S):
    cur = lambda cb: pl.BlockSpec((CT, D), lambda i: (i, cb))
    halo = lambda cb: pl.BlockSpec((HALO, D), lambda i: (jnp.maximum(i * (CT // HALO) - 1, 0), cb))
    full = lambda shp: pl.BlockSpec(shp, lambda i: (0, 0))
    return cur, halo, full


def _conv_fwd(z, b_glu, w_dw, b_dw, g_ln, b_ln):
    S = z.shape[0]
    cur, halo, full = _conv_specs(S)

    def body(hu, hg, huh, hgh, bglu, w, bdw, gln, bln, u1_ref, u3_ref, buf, sh, wb):
        i = pl.program_id(0)

        @pl.when(i == 0)
        def _():
            _tap_rows(wb, w)

        _glu_window(hu, hg, huh, hgh, bglu, buf, i)
        _shift_copies(buf, sh)
        _conv_taps(sh, wb, u1_ref, jnp.broadcast_to(bdw[...], (8, D)), lambda j: 2 + j)
        xh, _ = _ln_hat(u1_ref[...])
        u2 = xh * gln[...] + bln[...]
        u3_ref[...] = (u2 * _sig(u2)).astype(BF)

    return pl.pallas_call(
        body, grid=(S // CT,),
        in_specs=[cur(0), cur(1), halo(0), halo(1), full((1, 2 * D)), full((HALO, D)), full((1, D)), full((1, D)), full((1, D))],
        out_specs=[pl.BlockSpec((CT, D), lambda i: (i, 0))] * 2,
        out_shape=[jax.ShapeDtypeStruct((S, D), F32), jax.ShapeDtypeStruct((S, D), BF)],
        scratch_shapes=[pltpu.VMEM((CBUF, D), F32), pltpu.VMEM((8, HALO + CT, D), F32), pltpu.VMEM((CONV_W * 8, D), F32)],
        compiler_params=_cp(("arbitrary",)), name="conv_fwd",
    )(z, z, z, z, b_glu, w_dw, b_dw, g_ln, b_ln)


def _conv_bwd(du1, z, b_glu, w_dw):
    S = z.shape[0]
    n = S // CT
    cur, halo, full = _conv_specs(S)

    def body(du, dun, hu, hg, huh, hgh, bglu, w, dz_ref, dw_ref, dbg_ref, bufu, bufd, shu, shd, wb, du0_ref, dwacc):
        i = pl.program_id(0)

        @pl.when(i == 0)
        def _():
            _tap_rows(wb, w)
            dwacc[...] = jnp.zeros(dwacc.shape, F32)
            dbg_ref[...] = jnp.zeros(dbg_ref.shape, F32)

        a, s = _glu_window(hu, hg, huh, hgh, bglu, bufu, i)
        bufd[0:CT, :] = du[...]
        bufd[CT:CT + HALO, :] = jnp.where(i < n - 1, dun[...], 0.0)
        bufd[CT + HALO:CBUF, :] = jnp.zeros((8, D), F32)
        _shift_copies(bufu, shu)
        _shift_copies(bufd, shd)
        _conv_taps(shd, wb, du0_ref, jnp.zeros((8, D), F32), lambda j: 30 - j)
        for rg in range(CT // (8 * RG)):
            dch = [bufd[(rg * RG + q) * 8:(rg * RG + q + 1) * 8, :] for q in range(RG)]
            for j in range(CONV_W):
                off = 2 + j
                acc = dwacc[j * 8:(j + 1) * 8, :]
                for q in range(RG):
                    row = 8 * (rg * RG + q + off // 8)
                    acc = acc + dch[q] * shu[off % 8, row:row + 8, :]
                dwacc[j * 8:(j + 1) * 8, :] = acc
        du0 = du0_ref[...]
        dhu = du0 * s
        dhg = du0 * a * s * (1.0 - s)
        dz_ref[:, 0:D] = dhu.astype(BF)
        dz_ref[:, D:2 * D] = dhg.astype(BF)
        dbg_ref[:, 0:D] += _psum8(dhu)
        dbg_ref[:, D:2 * D] += _psum8(dhg)

        @pl.when(i == n - 1)
        def _():
            dbg_ref[0:1, :] = jnp.sum(dbg_ref[...], axis=0, keepdims=True)
            for j in range(CONV_W):
                dw_ref[j:j + 1, :] = jnp.sum(dwacc[j * 8:(j + 1) * 8, :], axis=0, keepdims=True)
            dw_ref[CONV_W:HALO, :] = jnp.zeros((HALO - CONV_W, D), F32)

    nxt = pl.BlockSpec((HALO, D), lambda i: (jnp.minimum((i + 1) * (CT // HALO), S // HALO - 1), 0))
    return pl.pallas_call(
        body, grid=(n,),
        in_specs=[pl.BlockSpec((CT, D), lambda i: (i, 0)), nxt, cur(0), cur(1), halo(0), halo(1), full((1, 2 * D)), full((HALO, D))],
        out_specs=[pl.BlockSpec((CT, 2 * D), lambda i: (i, 0)), full((HALO, D)), full((8, 2 * D))],
        out_shape=[jax.ShapeDtypeStruct((S, 2 * D), BF), jax.ShapeDtypeStruct((HALO, D), F32), jax.ShapeDtypeStruct((8, 2 * D), F32)],
        scratch_shapes=[pltpu.VMEM((CBUF, D), F32), pltpu.VMEM((CBUF, D), F32), pltpu.VMEM((8, HALO + CT, D), F32),
                        pltpu.VMEM((8, HALO + CT, D), F32), pltpu.VMEM((CONV_W * 8, D), F32), pltpu.VMEM((CT, D), F32),
                        pltpu.VMEM((CONV_W * 8, D), F32)],
        compiler_params=_cp(("arbitrary",)), name="conv_bwd",
    )(du1, du1, z, z, z, z, b_glu, w_dw)


MESH = pl.DeviceIdType.MESH


def _all_gather(name, shards):
    n = len(shards)

    def body(*refs):
        ins, outs = refs[:n], refs[n:2 * n]
        send_sems, recv_sems, local_sems = refs[2 * n:]
        x, y, c = lax.axis_index("x"), lax.axis_index("y"), lax.axis_index("c")
        me, sibling = (x, y, c), (x, y, 1 - c)
        chips = [(1 - x, y), (x, 1 - y), (1 - x, 1 - y)]

        def slot(a, px, py, pc):
            return outs[a].at[4 * px + 2 * py + pc]

        def copy(a, k, block, to, src=None):
            return pltpu.make_async_remote_copy(
                src_ref=slot(a, *block) if src is None else src, dst_ref=slot(a, *block),
                send_sem=send_sems.at[a, k], recv_sem=recv_sems.at[a, k], device_id=to, device_id_type=MESH)

        mine = [pltpu.make_async_copy(ins[a], slot(a, *me), local_sems.at[a]) for a in range(n)]
        for cp in mine:
            cp.start()
        first = []
        for a in range(n):
            first.append(copy(a, 0, me, sibling, src=ins[a]))
            first += [copy(a, 1 + j, me, (*chip, c), src=ins[a]) for j, chip in enumerate(chips)]
        for cp in first:
            cp.start()
        passed = []
        for j, chip in enumerate(chips):
            for a in range(n):
                copy(a, 1 + j, (*chip, c), me).wait_recv()
                fwd = copy(a, 4 + j, (*chip, c), sibling)
                fwd.start()
                passed.append(fwd)
        for a in range(n):
            copy(a, 0, sibling, me).wait_recv()
        for j, chip in enumerate(chips):
            for a in range(n):
                copy(a, 4 + j, (*chip, 1 - c), me).wait_recv()
        for cp in first + passed:
            cp.wait_send()
        for cp in mine:
            cp.wait()

    anyspec = pl.BlockSpec(memory_space=pl.ANY)
    return pl.pallas_call(
        body, in_specs=[anyspec] * n, out_specs=[anyspec] * n,
        out_shape=[jax.ShapeDtypeStruct((NDEV,) + s.shape, s.dtype) for s in shards],
        scratch_shapes=[pltpu.SemaphoreType.DMA((n, 7)), pltpu.SemaphoreType.DMA((n, 7)), pltpu.SemaphoreType.DMA((n,))],
        name=name,
    )(*shards)


HBM_SPEC = pl.BlockSpec(memory_space=pltpu.HBM)
SEM_SPEC = pl.BlockSpec(memory_space=pltpu.SEMAPHORE)
DATAFLOW = pltpu.SideEffectType.DATAFLOW_SIDE_EFFECTING


def _peers():
    x, y, c = lax.axis_index("x"), lax.axis_index("y"), lax.axis_index("c")
    out = []
    for k in range(1, NDEV):
        px = 1 - x if k & 4 else x
        py = 1 - y if k & 2 else y
        pc = 1 - c if k & 1 else c
        out.append(((px, py, pc), 4 * px + 2 * py + pc))
    return 4 * x + 2 * y + c, out


def _exchange_copies(srcs, lands, send_sems, recv_sems, gather):
    my, peers = _peers()
    pairs = []
    for k, (dev, pid) in enumerate(peers):
        for a in range(len(srcs)):
            src = srcs[a] if gather else srcs[a].at[pid]
            sems = dict(send_sem=send_sems[a * (NDEV - 1) + k], recv_sem=recv_sems[a * (NDEV - 1) + k], device_id=dev,
                        device_id_type=MESH)
            pairs.append((pltpu.make_async_remote_copy(src_ref=src, dst_ref=lands[a].at[my], **sems),
                          pltpu.make_async_remote_copy(src_ref=src, dst_ref=lands[a].at[pid], **sems)))
    return pairs


def _exchange_start(name, srcs, gather):
    n = len(srcs)
    ns = n * (NDEV - 1)
    shapes = [(s.shape if gather else s.shape[1:]) for s in srcs]
    lands = [lax.empty((NDEV,) + shp, s.dtype) for shp, s in zip(shapes, srcs)]

    def body(*refs):
        src_refs, land_refs = refs[:n], refs[n:2 * n]
        send_sems, recv_sems = refs[2 * n:2 * n + ns], refs[2 * n + ns:2 * n + 2 * ns]
        token = refs[-1]
        for mine, _ in _exchange_copies(src_refs, land_refs, send_sems, recv_sems, gather):
            mine.start()
        token[...] = jnp.zeros(token.shape, token.dtype)

    hbm = lambda a: pltpu.HBM(a.shape, a.dtype)
    res = pl.pallas_call(
        body, name=name,
        out_shape=(*([pltpu.SemaphoreType.DMA(())] * (2 * ns)), *[hbm(s) for s in srcs], *[hbm(l) for l in lands],
                   jax.ShapeDtypeStruct((8, 128), F32)),
        in_specs=[HBM_SPEC] * (2 * n),
        out_specs=(*([SEM_SPEC] * (2 * ns)), *([HBM_SPEC] * (2 * n)), pl.BlockSpec(memory_space=pltpu.VMEM)),
        input_output_aliases={i: 2 * ns + i for i in range(2 * n)},
        compiler_params=pltpu.CompilerParams(has_side_effects=DATAFLOW),
    )(*[pltpu.with_memory_space_constraint(s, pltpu.HBM) for s in srcs],
      *[pltpu.with_memory_space_constraint(l, pltpu.HBM) for l in lands])
    return list(res[:ns]), list(res[ns:2 * ns]), list(res[2 * ns:2 * ns + n]), list(res[2 * ns + n:2 * ns + 2 * n]), res[-1]


def _exchange_wait(name, handle, after, gather):
    send_sems, recv_sems, srcs, lands, _ = handle
    n = len(srcs)
    ns = n * (NDEV - 1)

    def body(*refs):
        src_refs, land_refs = refs[:n], refs[n:2 * n]
        s_sems, r_sems = refs[2 * n:2 * n + ns], refs[2 * n + ns:2 * n + 2 * ns]
        for mine, theirs in _exchange_copies(src_refs, land_refs, s_sems, r_sems, gather):
            mine.wait_send()
            theirs.wait_recv()

    hbm = lambda a: pltpu.HBM(a.shape, a.dtype)
    res = pl.pallas_call(
        body, name=name,
        out_shape=(*[hbm(s) for s in srcs], *[hbm(l) for l in lands]),
        in_specs=[HBM_SPEC] * (2 * n) + [SEM_SPEC] * (2 * ns) + [pl.BlockSpec(memory_space=pl.ANY)],
        out_specs=tuple([HBM_SPEC] * (2 * n)),
        input_output_aliases={i: i for i in range(2 * n)},
        compiler_params=pltpu.CompilerParams(has_side_effects=DATAFLOW),
    )(*srcs, *lands, *send_sems, *recv_sems, after)
    return list(res[n:])


def _set_own_slot(land, own):
    my = 4 * lax.axis_index("x") + 2 * lax.axis_index("y") + lax.axis_index("c")
    return lax.dynamic_update_slice(land, own[None], (my, 0, 0))


def _own_block(blocks):
    my = 4 * lax.axis_index("x") + 2 * lax.axis_index("y") + lax.axis_index("c")
    return lax.dynamic_index_in_dim(blocks, my, axis=0, keepdims=False)


_C1 = 1.0 - ADAM_B1 ** ADAM_STEP
_C2 = 1.0 - ADAM_B2 ** ADAM_STEP


def _adamw(name, w, m, v, recv, tr):
    R, C = w.shape

    def body(w_ref, m_ref, v_ref, r_ref, g_ref, d_ref, nm_ref, nv_ref):
        g = r_ref[0].astype(F32)
        for s in range(1, NDEV):
            g = g + r_ref[s].astype(F32)
        wv = w_ref[...]
        nm = ADAM_B1 * m_ref[...] + (1.0 - ADAM_B1) * g
        nv = ADAM_B2 * v_ref[...] + (1.0 - ADAM_B2) * (g * g)
        m_hat = nm / _C1
        v_hat = nv / _C2
        g_ref[...] = g
        d_ref[...] = -ADAM_LR * (m_hat / (jnp.sqrt(v_hat) + ADAM_EPS) + ADAM_WD * wv)
        nm_ref[...] = nm
        nv_ref[...] = nv

    blk = pl.BlockSpec((tr, C), lambda i: (i, 0))
    return pl.pallas_call(
        body, grid=(R // tr,), in_specs=[blk, blk, blk, pl.BlockSpec((NDEV, tr, C), lambda i: (0, i, 0))],
        out_specs=[blk] * 4, out_shape=[jax.ShapeDtypeStruct((R, C), F32)] * 4,
        compiler_params=_cp(), name=name,
    )(w, m, v, recv)


def _row(v):
    return v.reshape(1, -1)


def _local_step(xs, tgt, Wp, rest_fn, early_fn, late_fn, rel_bias_table, g_pre_mix, b_glu, b_dw, g_conv_ln,
                b_conv_ln, b_conv_out, g_post_mix, g_pre_ffn, g_post_ffn):
    S = xs.shape[0]
    g1, g2, g3, g4 = _row(g_pre_mix), _row(g_post_mix), _row(g_pre_ffn), _row(g_post_ffn)
    bglu, bdw, gln, bln, bco = _row(b_glu), _row(b_dw), _row(g_conv_ln), _row(b_conv_ln), _row(b_conv_out)
    full = (D, F32, D, 0, False)
    fullb = (D, BF, D, 0, False)

    def epi_rms(accs, r, c, o, p):
        v = r[0][...]
        o[0][...] = (v * _rms_r(v) * c[0][...]).astype(BF)

    (h1,) = _fused_mm("rms_in", S, 512, 1, [], [], [], [], [(xs, D, 0, False)], [g1], [fullb], [], epi_rms)

    def epi_cast(accs, r, c, o, p):
        o[0][...] = accs[0].astype(BF)

    ZT = IN_W // 4
    (z,) = _fused_mm("in_proj", S, 1024, 4, [(h1, D, 0, 1)], [(Wp, False, D, ZT, 0, 1, 0, 0, True)], [(0, 0, 0, 0, 1)],
                     [(1024, ZT)], [], [], [(IN_W, BF, ZT, 0, True)], [], epi_cast)

    idx = jnp.asarray(_band_index())
    tab_t = rel_bias_table.T.reshape(3, HEADS, REL_BUCKETS)
    bias_all = _bias_build(tab_t, idx)
    bias_kq = [bias_all[g].reshape(HEADS, 2 * QBLK, QBLK) for g in range(3)]
    bias_kq2 = [bias_all[3 + g].reshape(HEADS, QBLK, 2 * QBLK) for g in range(3)]
    dils = [d for _, d in GROUPS]
    qkv = [(z, _kvq_blocks(0))] + [(_to_residue(f"qkv_to_residue_g{g}", z, _kvq_blocks(g), dils[g]), (0, 1, 2)) for g in (1, 2)]
    os_, ls_ = [], []
    for g in range(3):
        o_g, l_g = _attn_fwd(f"attn_fwd_g{g}", qkv[g][0], bias_kq[g], qkv[g][1], dils[g])
        os_.append(o_g)
        ls_.append(l_g)
    o_att, lse = _attn_merge(os_, ls_, S)

    Wfi, Wfo, Wco, Wmo, Wao, wdw = rest_fn(lse)
    u1, u3 = _conv_fwd(z, bglu, wdw, bdw, gln, bln)

    def epi_mix(accs, r, c, o, p):
        ya = accs[0]
        yc = accs[1] + c[0][...]
        mg = _sig(r[0][...].astype(F32)) * ya + _sig(r[1][...].astype(F32)) * yc
        mgb = mg.astype(BF)
        m2 = jnp.dot(mgb, c[1][...], preferred_element_type=F32)
        x1 = r[2][...] + m2 * _rms_r(m2) * c[2][...]
        o[0][...] = ya.astype(BF)
        o[1][...] = yc.astype(BF)
        o[2][...] = mgb
        o[3][...] = m2.astype(BF)
        o[4][...] = x1
        o[5][...] = (x1 * _rms_r(x1) * c[3][...]).astype(BF)

    y_attn, y_conv, merged, m2, x1, h2 = _fused_mm(
        "mix_fwd", S, 512, 1, [(o_att, GW, 0, 1), (u3, D, 0, 1)],
        [(Wao, False, GW, D, 0, 1, 0, 0, False), (Wco, False, D, D, 0, 1, 0, 0, False)], [(0, 0, 0, 0, 1), (1, 1, 1, 0, 1)],
        [(512, D), (512, D)], [(z, D, 2, False), (z, D, 3, False), (xs, D, 0, False)], [bco, Wmo, g2, g3],
        [fullb, fullb, fullb, fullb, full, fullb], [], epi_mix)

    HN = FFN // 2

    def epi_ffn_in(accs, r, c, o, p):
        gt, up = accs
        o[0][...] = gt.astype(BF)
        o[1][...] = up.astype(BF)
        o[2][...] = (gt * _sig(gt) * up).astype(BF)

    gate, up, act = _fused_mm(
        "ffn_in", S, 512, 2, [(h2, D, 0, 1)],
        [(Wfi, False, D, HN, 0, 1, 0, 0, True), (Wfi, False, D, HN, 0, 1, 0, 2, True)], [(0, 0, 0, 0, 1), (0, 1, 1, 0, 1)],
        [(512, HN), (512, HN)], [], [], [(FFN, BF, HN, 0, True)] * 3, [], epi_ffn_in)

    def epi_loss(accs, r, c, o, p):
        f2 = accs[0]
        g = c[0][...]
        rr = _rms_r(f2)
        err = r[0][...] + f2 * rr * g - r[1][...]
        dy = err * (1.0 / D)
        df2, dgr = _rms_bwd(f2, rr, g, dy)
        o[0][...] = dy
        o[1][...] = df2.astype(BF)
        p[0][...] += _psum8(err * err)
        p[1][...] += _psum8(dgr)

    dy, df2, loss_p, dg4 = _fused_mm(
        "ffn_out_loss", S, 512, 1, [(act, FFN, 0, 1)], [(Wfo, False, FFN, D, 0, 1, 0, 0, False)], [(0, 0, 0, 0, 1)],
        [(512, D)], [(x1, D, 0, False), (tgt, D, 0, False)], [g4], [full, fullb], [(8, D), (8, D)], epi_loss)

    def epi_swiglu(accs, r, c, o, p):
        da = accs[0]
        gt = r[0][...].astype(F32)
        sg = _sig(gt)
        o[0][...] = (da * r[1][...].astype(F32) * sg * (1.0 + gt * (1.0 - sg))).astype(BF)
        o[1][...] = (da * gt * sg).astype(BF)

    dgate, dup = _fused_mm(
        "ffn_out_bwd", S, 512, 2, [(df2, D, 0, 1)], [(Wfo, True, D, HN, 0, 1, 0, 0, True)], [(0, 0, 0, 0, 1)],
        [(512, HN)], [(gate, HN, 0, True), (up, HN, 0, True)], [], [(FFN, BF, HN, 0, True)] * 2, [], epi_swiglu)
    dWfo = _mm_tn("dw_ffn_out", act, df2, HN, D, 1024)

    def epi_dh2(accs, r, c, o, p):
        dh2 = accs[0]
        x1v = r[1][...]
        r3 = _rms_r(x1v)
        d1, dg3r = _rms_bwd(x1v, r3, c[0][...], dh2)
        dx1 = r[0][...] + d1
        m2v = r[2][...].astype(F32)
        r2 = _rms_r(m2v)
        dm2, dg2r = _rms_bwd(m2v, r2, c[1][...], dx1)
        o[0][...] = dx1
        o[1][...] = dm2.astype(BF)
        p[0][...] += _psum8(dg3r)
        p[1][...] += _psum8(dg2r)

    dx1, dm2, dg3, dg2 = _fused_mm(
        "ffn_in_bwd", S, 512, 1, [(dgate, HN, 0, 2), (dup, HN, 2, 2)], [(Wfi, True, HN, D, 0, 4, 0, 0, False)],
        [(0, 0, 0, 0, 2), (1, 0, 0, 2, 2)], [(512, D)], [(dy, D, 0, False), (x1, D, 0, False), (m2, D, 0, False)], [g3, g2],
        [full, fullb], [(8, D), (8, D)], epi_dh2)
    dWfi = jnp.concatenate([_mm_tn("dw_ffn_gate", h2, dgate, D, HN, 1024), _mm_tn("dw_ffn_up", h2, dup, D, HN, 1024)], axis=1)

    def epi_dmix(accs, r, c, o, p):
        dm = accs[0]
        sa = _sig(r[0][...].astype(F32))
        sc = _sig(r[1][...].astype(F32))
        o[0][...] = (dm * sa).astype(BF)
        o[1][...] = (dm * sc).astype(BF)
        o[2][:, 0:D] = (dm * r[2][...].astype(F32) * sa * (1.0 - sa)).astype(BF)
        o[2][:, D:2 * D] = (dm * r[3][...].astype(F32) * sc * (1.0 - sc)).astype(BF)

    dy_attn, dy_conv, dz_gate = _fused_mm(
        "mix_bwd", S, 512, 1, [(dm2, D, 0, 1)], [(Wmo, True, D, D, 0, 1, 0, 0, False)], [(0, 0, 0, 0, 1)], [(512, D)],
        [(z, D, 2, False), (z, D, 3, False), (y_attn, D, 0, False), (y_conv, D, 0, False)], [],
        [fullb, fullb, (2 * D, BF, 2 * D, 0, False)], [], epi_dmix)
    dWmo = _mm_tn("dw_mix_out", merged, dm2, D, D, 1024)

    def epi_dconv(accs, r, c, o, p):
        du3 = accs[0]
        xh, rstd = _ln_hat(r[0][...])
        gl = c[0][...]
        u2 = xh * gl + c[1][...]
        sg = _sig(u2)
        du2 = du3 * sg * (1.0 + u2 * (1.0 - sg))
        dxh = du2 * gl
        du1 = rstd * (dxh - jnp.mean(dxh, axis=-1, keepdims=True) - xh * jnp.mean(dxh * xh, axis=-1, keepdims=True))
        o[0][...] = du1
        p[0][...] += _psum8(du2 * xh)
        p[1][...] += _psum8(du2)
        p[2][...] += _psum8(du1)
        p[3][...] += _psum8(r[1][...].astype(F32))

    du1, dgln, dbln, dbdw, dbco = _fused_mm(
        "conv_out_bwd", S, 512, 1, [(dy_conv, D, 0, 1)], [(Wco, True, D, D, 0, 1, 0, 0, False)], [(0, 0, 0, 0, 1)], [(512, D)],
        [(u1, D, 0, False), (dy_conv, D, 0, False)], [gln, bln], [full], [(8, D)] * 4, epi_dconv)
    dWco = _mm_tn("dw_conv_out", u3, dy_conv, D, D, 1024)
    dz_glu, dwdw, dbglu = _conv_bwd(du1, z, bglu, wdw)

    head_sum = np.zeros((GW, STAT_W), np.float32)
    for h in range(HEADS):
        head_sum[h * HEAD_DIM:(h + 1) * HEAD_DIM, HEADS + h] = 1.0
    head_sum = jnp.asarray(head_sum)

    def epi_do(accs, r, c, o, p):
        do = accs[0]
        o[0][...] = do.astype(BF)
        delta = jnp.dot(do * r[0][...].astype(F32), c[0][...], preferred_element_type=F32, precision=lax.Precision.HIGHEST)
        lane = lax.broadcasted_iota(jnp.int32, delta.shape, 1)
        o[1][...] = jnp.where(lane < HEADS, r[1][...], delta)

    do, stats = _fused_mm(
        "attn_out_bwd", S, 1024, 1, [(dy_attn, D, 0, 1)], [(Wao, True, D, GW, 0, 1, 0, 0, False)], [(0, 0, 0, 0, 1)], [(1024, GW)],
        [(o_att, GW, 0, False), (lse, STAT_W, 0, False)], [head_sum], [(GW, BF, GW, 0, False), (STAT_W, F32, STAT_W, 0, False)], [], epi_do)
    dWao = _mm_tn("dw_attn_out", o_att, dy_attn, GW, D, 1024)

    tie = early_fn(dict(w_ffn_in=dWfi, w_ffn_out=dWfo, w_conv_out=dWco, w_mix_out=dWmo, w_attn_out=dWao, w_dw=dwdw))
    stats = stats + tie
    dos = [do] + [_to_residue(f"do_to_residue_g{g}", do, (0,), dils[g]) for g in (1, 2)]
    sts = [stats] + [_to_residue_stats(f"stats_to_residue_g{g}", stats, dils[g]) for g in (1, 2)]
    dqkv, dbs = [], []
    for g in range(3):
        arr, cb = qkv[g]
        dg, db = _attn_bwd(f"attn_bwd_g{g}", arr, bias_kq2[g], dos[g], sts[g], cb, dils[g])
        dqkv.append(dg if g == 0 else _from_residue(f"dqkv_from_residue_g{g}", dg, dils[g]))
        dbs.append(db.reshape(HEADS, _NB))
    dtab = _bias_grad(jnp.stack(dbs), idx)[:, :, :REL_BUCKETS].reshape(3 * HEADS, REL_BUCKETS).T

    def epi_dx(accs, r, c, o, p):
        xv = r[1][...]
        d1, dg1r = _rms_bwd(xv, _rms_r(xv), c[0][...], accs[0])
        o[0][...] = r[0][...] + d1
        p[0][...] += _psum8(dg1r)

    dWg = [_mm_tn(f"dw_in_g{g}", h1, dqkv[g], D, ATTN_W, 1024) for g in range(3)]
    dW_in = jnp.concatenate(
        [t[:, 2 * GW:] for t in dWg] + [t[:, :GW] for t in dWg] + [t[:, GW:2 * GW] for t in dWg]
        + [_mm_tn("dw_in_glu", h1, dz_glu, D, D, 1024), _mm_tn("dw_in_gate", h1, dz_gate, D, D, 1024)], axis=1)
    g1_late = g1 + late_fn(dW_in)
    segs = [(dz_glu, 0, 4), (dz_gate, 4, 4), (dqkv[0], 8, 3), (dqkv[1], 11, 3), (dqkv[2], 14, 3)]
    grad_x, dg1 = _fused_mm(
        "in_proj_bwd", S, 1024, 1, [(a, GW, k0, nk) for a, k0, nk in segs], [(Wp, True, GW, D, 0, Z_CB, 0, 0, False)],
        [(t, 0, 0, k0, nk) for t, (a, k0, nk) in enumerate(segs)], [(1024, D)], [(dx1, D, 0, False), (xs, D, 0, False)], [g1_late],
        [full], [(8, D)], epi_dx)

    small = dict(rel_bias_table=dtab, g_pre_mix=dg1[0], b_glu=dbglu[0], b_dw=dbdw[0], g_conv_ln=dgln[0], b_conv_ln=dbln[0],
                 b_conv_out=dbco[0], g_post_mix=dg2[0], g_pre_ffn=dg3[0], g_post_ffn=dg4[0])
    return loss_p[0], grad_x, small


SMALL = ['rel_bias_table', 'g_pre_mix', 'b_glu', 'b_dw', 'g_conv_ln', 'b_conv_ln', 'b_conv_out', 'g_post_mix', 'g_pre_ffn',
         'g_post_ffn']
BIG = ['w_in', 'w_ffn_in', 'w_ffn_out', 'w_conv_out', 'w_mix_out', 'w_attn_out', 'w_dw']
WEIGHTS = ['rel_bias_table', 'g_pre_mix', 'w_in', 'b_glu', 'w_dw', 'b_dw', 'g_conv_ln', 'b_conv_ln', 'w_conv_out', 'b_conv_out',
           'w_attn_out', 'w_mix_out', 'g_post_mix', 'g_pre_ffn', 'w_ffn_in', 'w_ffn_out', 'g_post_ffn']
SMALL_ROWS = 16


ROW_SMALL = ['g_pre_mix', 'b_glu', 'b_dw', 'g_conv_ln', 'b_conv_ln', 'b_conv_out', 'g_post_mix', 'g_pre_ffn', 'g_post_ffn']
LOSS_ROW = 10
TAB_LANES = 128


def _small_rows(small, loss_row):
    rows = [small[n].reshape(-1, D) for n in ROW_SMALL] + [loss_row.reshape(1, D)]
    n = sum(r.shape[0] for r in rows)
    return jnp.concatenate(rows + [jnp.zeros((SMALL_ROWS - n, D), F32)], axis=0)


def _adamw_small(recv_rows, recv_tab, ws, ms, vs):
    np_ = len(SMALL)

    def body(*refs):
        rr, rt = refs[0], refs[1]
        w_refs, m_refs, v_refs = refs[2:2 + np_], refs[2 + np_:2 + 2 * np_], refs[2 + 2 * np_:2 + 3 * np_]
        loss_ref = refs[2 + 3 * np_]
        outs = refs[3 + 3 * np_:]
        rows = rr[0]
        tab = rt[0]
        for s_ in range(1, NDEV):
            rows = rows + rr[s_]
            tab = tab + rt[s_]
        loss_ref[...] = jnp.sum(rows[LOSS_ROW:LOSS_ROW + 1, :], axis=1, keepdims=True) * (0.5 / D)
        row = 0
        for p, n in enumerate(SMALL):
            if n == 'rel_bias_table':
                g = tab[:, 0:3 * HEADS]
            else:
                k = w_refs[p].shape[1] // D
                g = rows[row:row + 1, :] if k == 1 else jnp.concatenate([rows[row + t:row + t + 1, :] for t in range(k)], axis=1)
                row += k
            nm = ADAM_B1 * m_refs[p][...] + (1.0 - ADAM_B1) * g
            nv = ADAM_B2 * v_refs[p][...] + (1.0 - ADAM_B2) * (g * g)
            outs[4 * p][...] = g
            outs[4 * p + 1][...] = -ADAM_LR * ((nm / _C1) / (jnp.sqrt(nv / _C2) + ADAM_EPS) + ADAM_WD * w_refs[p][...])
            outs[4 * p + 2][...] = nm
            outs[4 * p + 3][...] = nv

    out_shape = [jax.ShapeDtypeStruct((1, 1), F32)]
    for a_ in ws:
        out_shape += [jax.ShapeDtypeStruct(a_.shape, F32)] * 4
    res = pl.pallas_call(body, out_shape=out_shape, compiler_params=_cp(), name="adamw_small")(recv_rows, recv_tab, *ws, *ms, *vs)
    return res[0], [tuple(res[1 + 4 * p:5 + 4 * p]) for p in range(np_)]


def _cols_to_blocks(a):
    R = a.shape[0]
    return a.reshape(R, NDEV, a.shape[1] // NDEV).transpose(1, 0, 2)


def _blocks_to_cols(a):
    return a.transpose(1, 0, 2).reshape(a.shape[1], NDEV * a.shape[2])


def kernel(x, rel_bias_table, g_pre_mix, w_in, b_glu, w_dw, b_dw, g_conv_ln, b_conv_ln, w_conv_out, b_conv_out, w_attn_out, w_mix_out, g_post_mix, g_pre_ffn, w_ffn_in, w_ffn_out, g_post_ffn, loss_target, m_rel_bias_table, m_g_pre_mix, m_w_in, m_b_glu, m_w_dw, m_b_dw, m_g_conv_ln, m_b_conv_ln, m_w_conv_out, m_b_conv_out, m_w_attn_out, m_w_mix_out, m_g_post_mix, m_g_pre_ffn, m_w_ffn_in, m_w_ffn_out, m_g_post_ffn, v_rel_bias_table, v_g_pre_mix, v_w_in, v_b_glu, v_w_dw, v_b_dw, v_g_conv_ln, v_b_conv_ln, v_w_conv_out, v_b_conv_out, v_w_attn_out, v_w_mix_out, v_g_post_mix, v_g_pre_ffn, v_w_ffn_in, v_w_ffn_out, v_g_post_ffn):
    w = dict(rel_bias_table=rel_bias_table, g_pre_mix=g_pre_mix, w_in=w_in, b_glu=b_glu, w_dw=w_dw, b_dw=b_dw, g_conv_ln=g_conv_ln, b_conv_ln=b_conv_ln, w_conv_out=w_conv_out, b_conv_out=b_conv_out, w_attn_out=w_attn_out, w_mix_out=w_mix_out, g_post_mix=g_post_mix, g_pre_ffn=g_pre_ffn, w_ffn_in=w_ffn_in, w_ffn_out=w_ffn_out, g_post_ffn=g_post_ffn)
    m = dict(rel_bias_table=m_rel_bias_table, g_pre_mix=m_g_pre_mix, w_in=m_w_in, b_glu=m_b_glu, w_dw=m_w_dw, b_dw=m_b_dw, g_conv_ln=m_g_conv_ln, b_conv_ln=m_b_conv_ln, w_conv_out=m_w_conv_out, b_conv_out=m_b_conv_out, w_attn_out=m_w_attn_out, w_mix_out=m_w_mix_out, g_post_mix=m_g_post_mix, g_pre_ffn=m_g_pre_ffn, w_ffn_in=m_w_ffn_in, w_ffn_out=m_w_ffn_out, g_post_ffn=m_g_post_ffn)
    v = dict(rel_bias_table=v_rel_bias_table, g_pre_mix=v_g_pre_mix, w_in=v_w_in, b_glu=v_b_glu, w_dw=v_w_dw, b_dw=v_b_dw, g_conv_ln=v_g_conv_ln, b_conv_ln=v_b_conv_ln, w_conv_out=v_w_conv_out, b_conv_out=v_b_conv_out, w_attn_out=v_w_attn_out, w_mix_out=v_w_mix_out, g_post_mix=v_g_post_mix, g_pre_ffn=v_g_pre_ffn, w_ffn_in=v_w_ffn_in, w_ffn_out=v_w_ffn_out, g_post_ffn=v_g_post_ffn)

    def shard2d(d, n):
        a = d[n][0]
        return jnp.pad(a, ((0, HALO - CONV_W), (0, 0))) if n == 'w_dw' else a

    own = {n: shard2d(w, n).astype(F32 if n == 'w_dw' else BF) for n in BIG}
    packed = ['w_ffn_out', 'w_conv_out', 'w_mix_out', 'w_attn_out']
    alone = ['w_ffn_in', 'w_dw']
    shapes = [own[n].shape for n in packed]

    def pack(arrs, lead):
        return jnp.concatenate([a.reshape(lead + (-1, D)) for a in arrs], axis=len(lead))

    def unpack(p):
        out, pos = {}, 0
        for n, shp in zip(packed, shapes):
            rows = shp[0] * shp[1] // D
            out[n] = p[:, pos:pos + rows].reshape((NDEV,) + shp)
            pos += rows
        return out

    (g_in,) = _all_gather("gather_w_in", [own['w_in']])
    rest_own = [pack([own[n] for n in packed], ())] + [own[n] for n in alone]
    g_in, rest_own = lax.optimization_barrier((g_in, rest_own))
    gather_rest = _exchange_start("gather_rest_start", rest_own, True)
    W_in = _blocks_to_cols(g_in)
    kvq = [W_in[:, t * ATTN_W + g * GW:t * ATTN_W + (g + 1) * GW] for g in range(3) for t in (1, 2, 0)]
    Wp = jnp.concatenate([W_in[:, 3 * ATTN_W:]] + kvq, axis=1)

    def rest_fn(after):
        lands = _exchange_wait("gather_rest_wait", gather_rest, after, True)
        gw = unpack(_set_own_slot(lands[0], rest_own[0]))
        for n, l, o in zip(alone, lands[1:], rest_own[1:]):
            gw[n] = _set_own_slot(l, o)
        return (_blocks_to_cols(gw['w_ffn_in']), gw['w_ffn_out'].reshape(FFN, D), gw['w_conv_out'].reshape(D, D),
                gw['w_mix_out'].reshape(D, D), _blocks_to_cols(gw['w_attn_out']), _blocks_to_cols(gw['w_dw']))

    def to_blocks(n, g):
        if n in ('w_in', 'w_ffn_in', 'w_attn_out', 'w_dw'):
            return _cols_to_blocks(g)
        return g.reshape(NDEV, g.shape[0] // NDEV, g.shape[1])

    started = {}

    def early_fn(grads):
        blocks = [pack([to_blocks(n, grads[n]) for n in packed], (NDEV,))] + [to_blocks(n, grads[n]) for n in alone]
        started['blocks'] = blocks
        started['handle'] = _exchange_start("scatter_early_start", blocks, False)
        return started['handle'][4][0:1, 0:1]

    def late_fn(dW_in):
        started['in_blocks'] = [to_blocks('w_in', dW_in)]
        started['in_handle'] = _exchange_start("scatter_w_in_start", started['in_blocks'], False)
        return started['in_handle'][4][0:1, 0:1]

    g1_tied = g_pre_mix[0] + gather_rest[4][0, 0:1]
    loss_row, grad_x, small = _local_step(
        x[0], loss_target[0], Wp, rest_fn, early_fn, late_fn, rel_bias_table, g1_tied, b_glu[0], b_dw[0], g_conv_ln[0],
        b_conv_ln[0], b_conv_out[0], g_post_mix[0], g_pre_ffn[0], g_post_ffn[0])

    lands = _exchange_wait("scatter_early_wait", started['handle'], grad_x, False)
    lands = [_set_own_slot(l, _own_block(b)) for l, b in zip(lands, started['blocks'])]
    recv = unpack(lands[0])
    recv.update(zip(alone, lands[1:]))
    (land_in,) = _exchange_wait("scatter_w_in_wait", started['in_handle'], grad_x, False)
    recv['w_in'] = _set_own_slot(land_in, _own_block(started['in_blocks'][0]))
    tiles = dict(w_in=128, w_ffn_in=256, w_ffn_out=176, w_conv_out=128, w_mix_out=128, w_attn_out=512, w_dw=HALO)
    res = {}
    for n in BIG:
        g_, d_, nm_, nv_ = _adamw("adamw_" + n, shard2d(w, n), shard2d(m, n), shard2d(v, n), recv[n], tiles[n])
        if n == 'w_dw':
            g_, d_, nm_, nv_ = (t[:CONV_W] for t in (g_, d_, nm_, nv_))
        res[n] = tuple(t[None] for t in (g_, d_, nm_, nv_))

    tab = jnp.pad(small['rel_bias_table'], ((0, 0), (0, TAB_LANES - 3 * HEADS)))
    srows, stab = _all_gather("gather_small_grads", [_small_rows(small, loss_row), tab])
    loss11, small_res = _adamw_small(srows, stab, [w[n] for n in SMALL], [m[n] for n in SMALL], [v[n] for n in SMALL])
    loss = loss11.reshape(())
    for n, r in zip(SMALL, small_res):
        res[n] = r
    return (loss, grad_x[None], *[res[n][0] for n in WEIGHTS], *[res[n][1] for n in WEIGHTS],
            *[res[n][2] for n in WEIGHTS], *[res[n][3] for n in WEIGHTS])
```

```python
import functools
import math

import numpy as np
import jax
import jax.numpy as jnp
from jax import lax
from jax.experimental import pallas as pl
from jax.experimental.pallas import tpu as pltpu

F32 = jnp.float32
BF = jnp.bfloat16

D = 1024
HEAD_DIM = 64
HEADS = 8
GROUPS = ((128, 1), (512, 4), (2048, 16))
QBLK = 128
GW = HEADS * HEAD_DIM
ATTN_W = 3 * GW
REL_BUCKETS = 32
REL_MAX_DISTANCE = 2048
CONV_W = 31
HALO = 32
FFN = 2816
IN_W = 3 * ATTN_W + 2 * D + 2 * D
RMS_EPS = 1e-6
LN_EPS = 1e-5
NEG_INF = -1e30
SCALE = HEAD_DIM ** -0.5
NDEV = 8

ADAM_LR = 0.001
ADAM_B1 = 0.9
ADAM_B2 = 0.999
ADAM_EPS = 1e-08
ADAM_WD = 0.01
ADAM_STEP = 10

Z_G0 = 4096 // GW
Z_CB = IN_W // GW


def _kvq_blocks(g):
    return (Z_G0 + 3 * g, Z_G0 + 3 * g + 1, Z_G0 + 3 * g + 2)


VMEM_LIMIT = 52 * 1024 * 1024


def _cp(sem=None):
    if sem is None:
        return pltpu.CompilerParams(vmem_limit_bytes=VMEM_LIMIT)
    return pltpu.CompilerParams(vmem_limit_bytes=VMEM_LIMIT, dimension_semantics=sem)


def _sig(v):
    return jax.nn.sigmoid(v)


def _psum8(v):
    return v.reshape(v.shape[0] // 8, 8, v.shape[1]).sum(axis=0)


def _rms_r(v):
    return lax.rsqrt(jnp.mean(v * v, axis=-1, keepdims=True) + RMS_EPS)


def _rms_bwd(v, r, g, dy):
    gy = dy * g
    dv = r * gy - v * (r * r * r) * jnp.mean(v * gy, axis=-1, keepdims=True)
    return dv, dy * v * r


def _clip_k(k, k0, nk):
    return jnp.clip(k - k0, 0, nk - 1)


def _fused_mm(name, M, tm, grid_n, a_ops, b_ops, terms, acc_shapes, rows, consts, outs, parts, epilogue, n_outer=False):
    gm = M // tm
    nk_total = max([t[3] + t[4] for t in terms], default=1)
    n_a, n_b, n_r, n_c, n_o, n_p = len(a_ops), len(b_ops), len(rows), len(consts), len(outs), len(parts)
    n_acc = len(acc_shapes)
    use_scratch = nk_total > 1
    if parts:
        assert grid_n == 1

    def jj(j, follow):
        return j if follow else 0

    in_specs, args = [], []
    for (arr, tk, k0, nk) in a_ops:
        in_specs.append(pl.BlockSpec((tm, tk), functools.partial(lambda i, j, k, k0, nk: (i, _clip_k(k, k0, nk)), k0=k0, nk=nk)))
        args.append(arr)
    for (arr, nt, tk, tn, k0, nk, koff, joff, fj) in b_ops:
        if nt:
            in_specs.append(pl.BlockSpec((tn, tk), functools.partial(
                lambda i, j, k, k0, nk, koff, joff, fj: (joff + jj(j, fj), _clip_k(k, k0, nk) + koff),
                k0=k0, nk=nk, koff=koff, joff=joff, fj=fj)))
        else:
            in_specs.append(pl.BlockSpec((tk, tn), functools.partial(
                lambda i, j, k, k0, nk, koff, joff, fj: (_clip_k(k, k0, nk) + koff, joff + jj(j, fj)),
                k0=k0, nk=nk, koff=koff, joff=joff, fj=fj)))
        args.append(arr)
    for (arr, w, off, fj) in rows:
        in_specs.append(pl.BlockSpec((tm, w), functools.partial(lambda i, j, k, off, fj: (i, off + jj(j, fj)), off=off, fj=fj)))
        args.append(arr)
    for arr in consts:
        in_specs.append(pl.BlockSpec(arr.shape, functools.partial(lambda i, j, k, nd: (0,) * nd, nd=arr.ndim)))
        args.append(arr)
    out_specs, out_shape = [], []
    for (ncols, dt, w, off, fj) in outs:
        out_specs.append(pl.BlockSpec((tm, w), functools.partial(lambda i, j, k, off, fj: (i, off + jj(j, fj)), off=off, fj=fj)))
        out_shape.append(jax.ShapeDtypeStruct((M, ncols), dt))
    for (r, c) in parts:
        out_specs.append(pl.BlockSpec((r, c), lambda i, j, k: (0, 0)))
        out_shape.append(jax.ShapeDtypeStruct((r, c), F32))
    scratch = [pltpu.VMEM(s, F32) for s in acc_shapes] if use_scratch else []

    def body(*refs):
        pos = 0
        a_refs = refs[pos:pos + n_a]; pos += n_a
        b_refs = refs[pos:pos + n_b]; pos += n_b
        r_refs = refs[pos:pos + n_r]; pos += n_r
        c_refs = refs[pos:pos + n_c]; pos += n_c
        o_refs = refs[pos:pos + n_o]; pos += n_o
        p_refs = refs[pos:pos + n_p]; pos += n_p
        acc_refs = refs[pos:pos + n_acc] if use_scratch else ()
        i = pl.program_id(0)
        k = pl.program_id(2)

        def dot_of(ai, bi):
            a = a_refs[ai][...].astype(BF)
            b = b_refs[bi][...].astype(BF)
            if b_ops[bi][1]:
                return lax.dot_general(a, b, (((1,), (1,)), ((), ())), preferred_element_type=F32)
            return jnp.dot(a, b, preferred_element_type=F32)

        if parts:
            @pl.when((i == 0) & (k == 0))
            def _():
                for p in p_refs:
                    p[...] = jnp.zeros(p.shape, F32)

        def finish(accs):
            epilogue(accs, r_refs, c_refs, o_refs, p_refs)
            if parts:
                @pl.when(i == gm - 1)
                def _():
                    for p in p_refs:
                        p[0:1, :] = jnp.sum(p[...], axis=0, keepdims=True)

        if not use_scratch:
            accs = [None] * n_acc
            for (ai, bi, ci, k0, nk) in terms:
                d = dot_of(ai, bi)
                accs[ci] = d if accs[ci] is None else accs[ci] + d
            finish(accs)
        else:
            @pl.when(k == 0)
            def _():
                for acc in acc_refs:
                    acc[...] = jnp.zeros(acc.shape, F32)

            for (ai, bi, ci, k0, nk) in terms:
                def do(ai=ai, bi=bi, ci=ci):
                    acc_refs[ci][...] += dot_of(ai, bi)
                if k0 == 0 and nk == nk_total:
                    do()
                else:
                    pl.when((k >= k0) & (k < k0 + nk))(do)

            @pl.when(k == nk_total - 1)
            def _():
                finish([acc[...] for acc in acc_refs])

    grid = (gm, grid_n, nk_total)
    if n_outer:
        assert not parts
        swap = lambda spec: pl.BlockSpec(spec.block_shape, functools.partial(lambda j, i, k, f: f(i, j, k), f=spec.index_map))
        in_specs, out_specs, grid = [swap(sp) for sp in in_specs], [swap(sp) for sp in out_specs], (grid_n, gm, nk_total)
    res = pl.pallas_call(
        body, grid=grid, in_specs=in_specs, out_specs=out_specs, out_shape=out_shape,
        scratch_shapes=scratch, compiler_params=_cp(("arbitrary", "arbitrary", "arbitrary")), name=name,
    )(*args)
    return res


def _mm_tn(name, a, b, tm, tn, tk):
    S, Ka = a.shape
    Nb = b.shape[1]
    nk = S // tk

    def body(a_ref, b_ref, o_ref, acc):
        k = pl.program_id(2)

        @pl.when(k == 0)
        def _():
            acc[...] = jnp.zeros(acc.shape, F32)

        acc[...] += lax.dot_general(a_ref[...], b_ref[...], (((0,), (0,)), ((), ())), preferred_element_type=F32)

        @pl.when(k == nk - 1)
        def _():
            o_ref[...] = acc[...].astype(o_ref.dtype)

    return pl.pallas_call(
        body, grid=(Ka // tm, Nb // tn, nk),
        in_specs=[pl.BlockSpec((tk, tm), lambda i, j, k: (k, i)), pl.BlockSpec((tk, tn), lambda i, j, k: (k, j))],
        out_specs=pl.BlockSpec((tm, tn), lambda i, j, k: (i, j)),
        out_shape=jax.ShapeDtypeStruct((Ka, Nb), BF),
        scratch_shapes=[pltpu.VMEM((tm, tn), F32)],
        compiler_params=_cp(("parallel", "parallel", "arbitrary")), name=name,
    )(a, b)


def _resident_mm(name, M, tm, a_segs, w, rows, consts, outs, parts, epilogue):
    gm = M // tm
    n_a, n_r, n_c, n_o, n_p = len(a_segs), len(rows), len(consts), len(outs), len(parts)
    widths = [a.shape[1] for a in a_segs]
    offs = [sum(widths[:t]) for t in range(n_a)]
    once = pl.Buffered(1)

    def body(*refs):
        pos = 0
        a_refs = refs[pos:pos + n_a]; pos += n_a
        w_ref = refs[pos]; pos += 1
        r_refs = refs[pos:pos + n_r]; pos += n_r
        c_refs = refs[pos:pos + n_c]; pos += n_c
        o_refs = refs[pos:pos + n_o]; pos += n_o
        p_refs = refs[pos:pos + n_p]
        i = pl.program_id(0)
        if parts:
            @pl.when(i == 0)
            def _():
                for p in p_refs:
                    p[...] = jnp.zeros(p.shape, F32)
        acc = None
        for t in range(n_a):
            d = lax.dot_general(a_refs[t][...], w_ref[:, offs[t]:offs[t] + widths[t]], (((1,), (1,)), ((), ())),
                                preferred_element_type=F32)
            acc = d if acc is None else acc + d
        epilogue([acc], r_refs, c_refs, o_refs, p_refs)
        if parts:
            @pl.when(i == gm - 1)
            def _():
                for p in p_refs:
                    p[0:1, :] = jnp.sum(p[...], axis=0, keepdims=True)

    in_specs = [pl.BlockSpec((tm, wd), lambda i: (i, 0)) for wd in widths]
    in_specs.append(pl.BlockSpec(w.shape, lambda i: (0, 0), pipeline_mode=once))
    in_specs += [pl.BlockSpec((tm, c), lambda i: (i, 0)) for _, c in rows]
    in_specs += [pl.BlockSpec(c.shape, lambda i: (0, 0), pipeline_mode=once) for c in consts]
    out_specs = [pl.BlockSpec((tm, nc), lambda i: (i, 0)) for nc, _ in outs] + [pl.BlockSpec(pc, lambda i: (0, 0)) for pc in parts]
    out_shape = [jax.ShapeDtypeStruct((M, nc), dt) for nc, dt in outs] + [jax.ShapeDtypeStruct(pc, F32) for pc in parts]
    return pl.pallas_call(
        body, grid=(gm,), in_specs=in_specs, out_specs=out_specs, out_shape=out_shape,
        compiler_params=_cp(("arbitrary",)), name=name,
    )(*a_segs, w, *[r for r, _ in rows], *consts)


def _rel_bucket_np(dist):
    max_exact = REL_BUCKETS // 2
    d = np.maximum(dist, 0)
    df = np.maximum(d, 1).astype(np.float32)
    large = max_exact + (np.log(df / np.float32(max_exact)) / np.float32(math.log(REL_MAX_DISTANCE / max_exact))
                         * np.float32(REL_BUCKETS - max_exact)).astype(np.int32)
    large = np.minimum(large, REL_BUCKETS - 1)
    return np.where(d < max_exact, d, large).astype(np.int32)


N_LAYOUTS = 2


def _band_index():
    idx = np.zeros((N_LAYOUTS * 3, 1, QBLK * 2 * QBLK), np.int32)
    for g, (window, dil) in enumerate(GROUPS):
        span = window // dil
        k = np.arange(2 * QBLK)[:, None]; q = np.arange(QBLK)[None, :]
        off = q - k + QBLK
        idx[g, 0] = np.where((off >= 0) & (off <= span), _rel_bucket_np(off * dil), -1).reshape(-1)
        k = np.arange(QBLK)[:, None]; q = np.arange(2 * QBLK)[None, :]
        off = q - k
        idx[3 + g, 0] = np.where((off >= 0) & (off <= span), _rel_bucket_np(off * dil), -1).reshape(-1)
    return idx


_NB = QBLK * 2 * QBLK
_BCH = 4096


def _bias_build(tab_t, idx):
    def body(t_ref, i_ref, o_ref):
        ix = i_ref[0]
        t = t_ref[0]
        acc = jnp.full((HEADS, _BCH), NEG_INF, F32)
        for b in range(REL_BUCKETS):
            acc = jnp.where(ix == b, t[:, b:b + 1], acc)
        o_ref[0] = acc

    return pl.pallas_call(
        body, grid=(N_LAYOUTS * 3, _NB // _BCH),
        in_specs=[pl.BlockSpec((1, HEADS, REL_BUCKETS), lambda l, n: (l % 3, 0, 0)),
                  pl.BlockSpec((1, 1, _BCH), lambda l, n: (l, 0, n))],
        out_specs=pl.BlockSpec((1, HEADS, _BCH), lambda l, n: (l, 0, n)),
        out_shape=jax.ShapeDtypeStruct((N_LAYOUTS * 3, HEADS, _NB), F32), compiler_params=_cp(), name="bias_build",
    )(tab_t, idx)


def _bias_grad(ds, idx):
    nch = _NB // _BCH

    def body(d_ref, i_ref, o_ref):
        n = pl.program_id(1)

        @pl.when(n == 0)
        def _():
            o_ref[...] = jnp.zeros(o_ref.shape, F32)

        ix = i_ref[0]
        d = d_ref[0]
        lane = lax.broadcasted_iota(jnp.int32, (HEADS, 128), 1)
        acc = jnp.zeros((HEADS, 128), F32)
        for b in range(REL_BUCKETS):
            s = jnp.sum(jnp.where(ix == b, d, 0.0), axis=1, keepdims=True)
            acc = acc + jnp.where(lane == b, s, 0.0)
        o_ref[0] += acc

    return pl.pallas_call(
        body, grid=(3, nch),
        in_specs=[pl.BlockSpec((1, HEADS, _BCH), lambda l, n: (l, 0, n)),
                  pl.BlockSpec((1, 1, _BCH), lambda l, n: (3 + l, 0, n))],
        out_specs=pl.BlockSpec((1, HEADS, 128), lambda l, n: (l, 0, 0)),
        out_shape=jax.ShapeDtypeStruct((3, HEADS, 128), F32), compiler_params=_cp(), name="bias_grad",
    )(ds, idx)


PT = 256
PSTEP = 1024
STAT_W = 128


def _perm_np(dil):
    p = np.zeros((PT, PT), np.float32)
    m = np.arange(PT // dil)
    for c in range(dil):
        p[c * (PT // dil) + m, m * dil + c] = 1.0
    return p


def _perm_const(dil, dtype, inverse):
    p = _perm_np(dil)
    return jnp.asarray(p.T if inverse else p, dtype)


def _apply_perm(p, x):
    if x.dtype == F32:
        return jnp.dot(p, x, preferred_element_type=F32, precision=lax.Precision.HIGHEST)
    return jnp.dot(p, x, preferred_element_type=F32)


def _to_residue(name, arr, col_blocks, dil):
    S = arr.shape[0]
    nc = len(col_blocks)
    p = _perm_const(dil, arr.dtype, False)
    sub = PT // dil

    def body(*refs):
        p_ref, ins, o_ref = refs[0], refs[1:1 + nc], refs[1 + nc]
        for u in range(PSTEP // PT):
            for t, r in enumerate(ins):
                y = _apply_perm(p_ref[...], r[u * PT:(u + 1) * PT, :]).astype(o_ref.dtype)
                o_ref[:, u * sub:(u + 1) * sub, t * GW:(t + 1) * GW] = y.reshape(dil, sub, GW)

    out = pl.pallas_call(
        body, grid=(S // PSTEP,),
        in_specs=[pl.BlockSpec((PT, PT), lambda i: (0, 0))]
                 + [pl.BlockSpec((PSTEP, GW), functools.partial(lambda i, cb: (i, cb), cb=cb)) for cb in col_blocks],
        out_specs=pl.BlockSpec((dil, PSTEP // dil, nc * GW), lambda i: (0, i, 0)),
        out_shape=jax.ShapeDtypeStruct((dil, S // dil, nc * GW), arr.dtype), compiler_params=_cp(), name=name,
    )(p, *([arr] * nc))
    return out.reshape(S, nc * GW)


def _to_residue_stats(name, arr, dil):
    S = arr.shape[0]
    p = _perm_const(dil, F32, False)
    sub = PT // dil

    def body(p_ref, x_ref, o_ref):
        for u in range(PSTEP // PT):
            y = _apply_perm(p_ref[...], x_ref[u * PT:(u + 1) * PT, :])
            o_ref[:, u * sub:(u + 1) * sub, :] = y.reshape(dil, sub, STAT_W)

    out = pl.pallas_call(
        body, grid=(S // PSTEP,),
        in_specs=[pl.BlockSpec((PT, PT), lambda i: (0, 0)), pl.BlockSpec((PSTEP, STAT_W), lambda i: (i, 0))],
        out_specs=pl.BlockSpec((dil, PSTEP // dil, STAT_W), lambda i: (0, i, 0)),
        out_shape=jax.ShapeDtypeStruct((dil, S // dil, STAT_W), F32), compiler_params=_cp(), name=name,
    )(p, arr)
    return out.reshape(S, STAT_W)


def _from_residue(name, arr, dil):
    S, W = arr.shape
    p = _perm_const(dil, arr.dtype, True)
    sub = PT // dil

    def body(p_ref, x_ref, o_ref):
        for u in range(PSTEP // PT):
            x = x_ref[:, u * sub:(u + 1) * sub, :].reshape(PT, W)
            o_ref[u * PT:(u + 1) * PT, :] = _apply_perm(p_ref[...], x).astype(o_ref.dtype)

    return pl.pallas_call(
        body, grid=(S // PSTEP,),
        in_specs=[pl.BlockSpec((PT, PT), lambda i: (0, 0)), pl.BlockSpec((dil, PSTEP // dil, W), lambda i: (0, i, 0))],
        out_specs=pl.BlockSpec((PSTEP, W), lambda i: (i, 0)),
        out_shape=jax.ShapeDtypeStruct((S, W), arr.dtype), compiler_params=_cp(), name=name,
    )(p, arr.reshape(dil, S // dil, W))


NT_DIMS = (((1,), (1,)), ((), ()))
TN_DIMS = (((0,), (0,)), ((), ()))


def _attn_dims(S, dil):
    L = S // dil
    TQ = min(512, L)
    return L, TQ, L // TQ, TQ // QBLK


def _attn_specs(S, dil):
    L, TQ, nq, nsub = _attn_dims(S, dil)
    nb = L // QBLK
    cur = lambda cb, w=GW: pl.BlockSpec((TQ, w), lambda c, i: (c * nq + i, cb))
    prev = lambda cb, w=GW: pl.BlockSpec((QBLK, w), lambda c, i: (c * nb + jnp.maximum(i * nsub - 1, 0), cb))
    nxt = lambda cb, w=GW: pl.BlockSpec((QBLK, w), lambda c, i: (c * nb + jnp.minimum((i + 1) * nsub, nb - 1), cb))
    band = lambda r, c_: pl.BlockSpec((HEADS, r, c_), lambda c, i: (0, 0, 0))
    return L, TQ, nq, nsub, cur, prev, nxt, band


def _fill(buf, first_ref, second_ref):
    n = first_ref.shape[0]
    buf[0:n, :] = first_ref[...]
    buf[n:n + second_ref.shape[0], :] = second_ref[...]


def _attn_fwd(name, arr, bias_kq, cb, dil):
    S = arr.shape[0]
    kcb, vcb, qcb = cb
    L, TQ, nq, nsub, cur, prev, nxt, band = _attn_specs(S, dil)

    def body(q_ref, kc_ref, kp_ref, vc_ref, vp_ref, b_ref, o_ref, l_ref, kbuf, vbuf):
        i = pl.program_id(1)
        _fill(kbuf, kp_ref, kc_ref)
        _fill(vbuf, vp_ref, vc_ref)
        row = lax.broadcasted_iota(jnp.int32, (2 * QBLK, QBLK), 0)
        first = (row >= QBLK) | (i > 0)
        for j in range(nsub):
            rs = slice(j * QBLK, (j + 1) * QBLK)
            ks = slice(j * QBLK, (j + 2) * QBLK)
            lrows = []
            for h in range(HEADS):
                hs = slice(h * HEAD_DIM, (h + 1) * HEAD_DIM)
                s = lax.dot_general(kbuf[ks, hs], q_ref[rs, hs], NT_DIMS, preferred_element_type=F32) * SCALE + b_ref[h]
                if j == 0:
                    s = jnp.where(first, s, NEG_INF)
                m = jnp.max(s, axis=0, keepdims=True)
                p = jnp.exp(s - m)
                den = jnp.sum(p, axis=0, keepdims=True)
                o_t = lax.dot_general(vbuf[ks, hs], p.astype(BF), TN_DIMS, preferred_element_type=F32) / den
                o_ref[rs, hs] = o_t.T.astype(BF)
                lrows.append(m + jnp.log(den))
            lt = jnp.concatenate(lrows + [jnp.zeros((STAT_W - HEADS, QBLK), F32)], axis=0)
            l_ref[rs, :] = lt.T

    return pl.pallas_call(
        body, grid=(dil, nq),
        in_specs=[cur(qcb), cur(kcb), prev(kcb), cur(vcb), prev(vcb), band(2 * QBLK, QBLK)],
        out_specs=[cur(0), cur(0, STAT_W)],
        out_shape=[jax.ShapeDtypeStruct((S, GW), BF), jax.ShapeDtypeStruct((S, STAT_W), F32)],
        scratch_shapes=[pltpu.VMEM((QBLK + TQ, GW), BF), pltpu.VMEM((QBLK + TQ, GW), BF)],
        compiler_params=_cp(), name=name,
    )(arr, arr, arr, arr, arr, bias_kq)


def _attn_bwd(name, arr, bias_kq2, do, stats, cb, dil):
    S = arr.shape[0]
    kcb, vcb, qcb = cb
    L, TQ, nq, nsub, cur, prev, nxt, band = _attn_specs(S, dil)

    def body(k_ref, v_ref, qc_ref, qn_ref, b_ref, doc_ref, don_ref, sc_ref, sn_ref, o_ref, db_ref, qbuf, dobuf, sbuf, carry):
        c = pl.program_id(0)
        i = pl.program_id(1)

        @pl.when((c == 0) & (i == 0))
        def _():
            db_ref[...] = jnp.zeros(db_ref.shape, F32)
            carry[...] = jnp.zeros(carry.shape, F32)

        _fill(qbuf, qc_ref, qn_ref)
        _fill(dobuf, doc_ref, don_ref)
        for j in range(nsub + 1):
            rs = slice(j * QBLK, (j + 1) * QBLK)
            sbuf[:, rs] = (sc_ref[rs, :] if j < nsub else sn_ref[...]).T
        col = lax.broadcasted_iota(jnp.int32, (QBLK, 2 * QBLK), 1)
        last = (col < QBLK) | (i < nq - 1)
        for h in range(HEADS):
            hs = slice(h * HEAD_DIM, (h + 1) * HEAD_DIM)
            bias_h = b_ref[h]
            db = jnp.zeros((QBLK, 2 * QBLK), F32)
            tail = carry[:, hs]
            for j in range(nsub):
                rs = slice(j * QBLK, (j + 1) * QBLK)
                qs = slice(j * QBLK, (j + 2) * QBLK)
                qq = qbuf[qs, hs]
                dd = dobuf[qs, hs]
                kk = k_ref[rs, hs]
                s = lax.dot_general(kk, qq, NT_DIMS, preferred_element_type=F32) * SCALE + bias_h
                if j == nsub - 1:
                    s = jnp.where(last, s, NEG_INF)
                p = jnp.exp(s - sbuf[h:h + 1, qs])
                dp = lax.dot_general(v_ref[rs, hs], dd, NT_DIMS, preferred_element_type=F32)
                ds = p * (dp - sbuf[HEADS + h:HEADS + h + 1, qs])
                db = db + ds
                dsb = ds.astype(BF)
                o_ref[rs, h * HEAD_DIM:(h + 1) * HEAD_DIM] = (jnp.dot(dsb, qq, preferred_element_type=F32) * SCALE).astype(BF)
                o_ref[rs, GW + h * HEAD_DIM:GW + (h + 1) * HEAD_DIM] = jnp.dot(p.astype(BF), dd, preferred_element_type=F32).astype(BF)
                dqw = lax.dot_general(dsb, kk, TN_DIMS, preferred_element_type=F32) * SCALE
                o_ref[rs, 2 * GW + h * HEAD_DIM:2 * GW + (h + 1) * HEAD_DIM] = (dqw[0:QBLK] + tail).astype(BF)
                tail = dqw[QBLK:2 * QBLK]
            carry[:, hs] = tail
            db_ref[h] += db

    return pl.pallas_call(
        body, grid=(dil, nq),
        in_specs=[cur(kcb), cur(vcb), cur(qcb), nxt(qcb), band(QBLK, 2 * QBLK),
                  cur(0), nxt(0), cur(0, STAT_W), nxt(0, STAT_W)],
        out_specs=[cur(0, ATTN_W), band(QBLK, 2 * QBLK)],
        out_shape=[jax.ShapeDtypeStruct((S, ATTN_W), BF), jax.ShapeDtypeStruct((HEADS, QBLK, 2 * QBLK), F32)],
        scratch_shapes=[pltpu.VMEM((TQ + QBLK, GW), BF), pltpu.VMEM((TQ + QBLK, GW), BF), pltpu.VMEM((STAT_W, TQ + QBLK), F32),
                        pltpu.VMEM((QBLK, GW), F32)],
        compiler_params=_cp(("arbitrary", "arbitrary")), name=name,
    )(arr, arr, arr, arr, bias_kq2, do, do, stats, stats)


def _head_expand():
    e = np.zeros((STAT_W, GW), np.float32)
    for h in range(HEADS):
        e[h, h * HEAD_DIM:(h + 1) * HEAD_DIM] = 1.0
    return e


def _attn_merge(os_, ls_, S):
    dils = [d for _, d in GROUPS]
    pb = [_perm_const(d, BF, True) for d in dils[1:]]
    pf = [_perm_const(d, F32, True) for d in dils[1:]]
    expand = jnp.asarray(_head_expand())

    def body(o0, o1, o2, l0, l1, l2, pb1, pb2, pf1, pf2, e_ref, o_ref, l_ref):
        for u in range(PSTEP // PT):
            rs = slice(u * PT, (u + 1) * PT)
            res = lambda r, d: r[:, u * (PT // d):(u + 1) * (PT // d), :].reshape(PT, r.shape[2])
            ov = [o0[rs, :].astype(F32), _apply_perm(pb1[...], res(o1, dils[1])), _apply_perm(pb2[...], res(o2, dils[2]))]
            lv = [l0[rs, :], _apply_perm(pf1[...], res(l1, dils[1])), _apply_perm(pf2[...], res(l2, dils[2]))]
            m = jnp.maximum(jnp.maximum(lv[0], lv[1]), lv[2])
            ev = [jnp.exp(l - m) for l in lv]
            den = ev[0] + ev[1] + ev[2]
            acc = jnp.zeros((PT, GW), F32)
            for g in range(3):
                wide = jnp.dot(ev[g] / den, e_ref[...], preferred_element_type=F32, precision=lax.Precision.HIGHEST)
                acc = acc + wide * ov[g]
            o_ref[rs, :] = acc.astype(BF)
            l_ref[rs, :] = m + jnp.log(den)

    nat = lambda w: pl.BlockSpec((PSTEP, w), lambda i: (i, 0))
    res = lambda d, w: pl.BlockSpec((d, PSTEP // d, w), lambda i: (0, i, 0))
    cst = lambda a: pl.BlockSpec(a.shape, lambda i: (0, 0))
    args = [os_[0], os_[1].reshape(dils[1], S // dils[1], GW), os_[2].reshape(dils[2], S // dils[2], GW),
            ls_[0], ls_[1].reshape(dils[1], S // dils[1], STAT_W), ls_[2].reshape(dils[2], S // dils[2], STAT_W),
            pb[0], pb[1], pf[0], pf[1], expand]
    return pl.pallas_call(
        body, grid=(S // PSTEP,),
        in_specs=[nat(GW), res(dils[1], GW), res(dils[2], GW), nat(STAT_W), res(dils[1], STAT_W), res(dils[2], STAT_W)]
                 + [cst(a) for a in args[6:]],
        out_specs=[nat(GW), nat(STAT_W)],
        out_shape=[jax.ShapeDtypeStruct((S, GW), BF), jax.ShapeDtypeStruct((S, STAT_W), F32)],
        compiler_params=_cp(), name="attn_merge",
    )(*args)


CT = 256
CBUF = HALO + CT + 8
RG = 4


def _ln_hat(u1):
    mu = jnp.mean(u1, axis=-1, keepdims=True)
    xc = u1 - mu
    rstd = lax.rsqrt(jnp.mean(xc * xc, axis=-1, keepdims=True) + LN_EPS)
    return xc * rstd, rstd


def _glu_window(hu_ref, hg_ref, huh_ref, hgh_ref, bglu_ref, buf_ref, i):
    bu = bglu_ref[:, 0:D]
    bg = bglu_ref[:, D:2 * D]
    uh = (huh_ref[...].astype(F32) + bu) * _sig(hgh_ref[...].astype(F32) + bg)
    buf_ref[0:HALO, :] = jnp.where(i > 0, uh, 0.0)
    a = hu_ref[...].astype(F32) + bu
    s = _sig(hg_ref[...].astype(F32) + bg)
    buf_ref[HALO:HALO + CT, :] = a * s
    buf_ref[HALO + CT:CBUF, :] = jnp.zeros((8, D), F32)
    return a, s


def _shift_copies(buf_ref, sh_ref):
    for r in range(8):
        sh_ref[r] = buf_ref[r:r + HALO + CT, :]


def _tap_rows(wb_ref, w_ref):
    for j in range(CONV_W):
        wb_ref[j * 8:(j + 1) * 8, :] = jnp.broadcast_to(w_ref[j:j + 1, :], (8, D))


def _conv_taps(sh_ref, wb_ref, out_ref, init, offset):
    for rg in range(CT // (8 * RG)):
        accs = [init] * RG
        for j in range(CONV_W):
            off = offset(j)
            wj = wb_ref[j * 8:(j + 1) * 8, :]
            for q in range(RG):
                row = 8 * (rg * RG + q + off // 8)
                accs[q] = accs[q] + wj * sh_ref[off % 8, row:row + 8, :]
        for q in range(RG):
            out_ref[(rg * RG + q) * 8:(rg * RG + q + 1) * 8, :] = accs[q]


def _conv_specs(S):
    cur = lambda cb: pl.BlockSpec((CT, D), lambda i: (i, cb))
    halo = lambda cb: pl.BlockSpec((HALO, D), lambda i: (jnp.maximum(i * (CT // HALO) - 1, 0), cb))
    full = lambda shp: pl.BlockSpec(shp, lambda i: (0, 0))
    return cur, halo, full


def _conv_fwd(z, b_glu, w_dw, b_dw, g_ln, b_ln):
    S = z.shape[0]
    cur, halo, full = _conv_specs(S)

    def body(hu, hg, huh, hgh, bglu, w, bdw, gln, bln, u1_ref, u3_ref, buf, sh, wb):
        i = pl.program_id(0)

        @pl.when(i == 0)
        def _():
            _tap_rows(wb, w)

        _glu_window(hu, hg, huh, hgh, bglu, buf, i)
        _shift_copies(buf, sh)
        _conv_taps(sh, wb, u1_ref, jnp.broadcast_to(bdw[...], (8, D)), lambda j: 2 + j)
        xh, _ = _ln_hat(u1_ref[...])
        u2 = xh * gln[...] + bln[...]
        u3_ref[...] = (u2 * _sig(u2)).astype(BF)

    return pl.pallas_call(
        body, grid=(S // CT,),
        in_specs=[cur(0), cur(1), halo(0), halo(1), full((1, 2 * D)), full((HALO, D)), full((1, D)), full((1, D)), full((1, D))],
        out_specs=[pl.BlockSpec((CT, D), lambda i: (i, 0))] * 2,
        out_shape=[jax.ShapeDtypeStruct((S, D), F32), jax.ShapeDtypeStruct((S, D), BF)],
        scratch_shapes=[pltpu.VMEM((CBUF, D), F32), pltpu.VMEM((8, HALO + CT, D), F32), pltpu.VMEM((CONV_W * 8, D), F32)],
        compiler_params=_cp(("arbitrary",)), name="conv_fwd",
    )(z, z, z, z, b_glu, w_dw, b_dw, g_ln, b_ln)


def _conv_bwd(du1, z, b_glu, w_dw):
    S = z.shape[0]
    n = S // CT
    cur, halo, full = _conv_specs(S)

    def body(du, dun, hu, hg, huh, hgh, bglu, w, dz_ref, dw_ref, dbg_ref, bufu, bufd, shu, shd, wb, du0_ref, dwacc):
        i = pl.program_id(0)

        @pl.when(i == 0)
        def _():
            _tap_rows(wb, w)
            dwacc[...] = jnp.zeros(dwacc.shape, F32)
            dbg_ref[...] = jnp.zeros(dbg_ref.shape, F32)

        a, s = _glu_window(hu, hg, huh, hgh, bglu, bufu, i)
        bufd[0:CT, :] = du[...]
        bufd[CT:CT + HALO, :] = jnp.where(i < n - 1, dun[...], 0.0)
        bufd[CT + HALO:CBUF, :] = jnp.zeros((8, D), F32)
        _shift_copies(bufu, shu)
        _shift_copies(bufd, shd)
        _conv_taps(shd, wb, du0_ref, jnp.zeros((8, D), F32), lambda j: 30 - j)
        for rg in range(CT // (8 * RG)):
            dch = [bufd[(rg * RG + q) * 8:(rg * RG + q + 1) * 8, :] for q in range(RG)]
            for j in range(CONV_W):
                off = 2 + j
                acc = dwacc[j * 8:(j + 1) * 8, :]
                for q in range(RG):
                    row = 8 * (rg * RG + q + off // 8)
                    acc = acc + dch[q] * shu[off % 8, row:row + 8, :]
                dwacc[j * 8:(j + 1) * 8, :] = acc
        du0 = du0_ref[...]
        dhu = du0 * s
        dhg = du0 * a * s * (1.0 - s)
        dz_ref[:, 0:D] = dhu.astype(BF)
        dz_ref[:, D:2 * D] = dhg.astype(BF)
        dbg_ref[:, 0:D] += _psum8(dhu)
        dbg_ref[:, D:2 * D] += _psum8(dhg)

        @pl.when(i == n - 1)
        def _():
            dbg_ref[0:1, :] = jnp.sum(dbg_ref[...], axis=0, keepdims=True)
            for j in range(CONV_W):
                dw_ref[j:j + 1, :] = jnp.sum(dwacc[j * 8:(j + 1) * 8, :], axis=0, keepdims=True)
            dw_ref[CONV_W:HALO, :] = jnp.zeros((HALO - CONV_W, D), F32)

    nxt = pl.BlockSpec((HALO, D), lambda i: (jnp.minimum((i + 1) * (CT // HALO), S // HALO - 1), 0))
    return pl.pallas_call(
        body, grid=(n,),
        in_specs=[pl.BlockSpec((CT, D), lambda i: (i, 0)), nxt, cur(0), cur(1), halo(0), halo(1), full((1, 2 * D)), full((HALO, D))],
        out_specs=[pl.BlockSpec((CT, 2 * D), lambda i: (i, 0)), full((HALO, D)), full((8, 2 * D))],
        out_shape=[jax.ShapeDtypeStruct((S, 2 * D), BF), jax.ShapeDtypeStruct((HALO, D), F32), jax.ShapeDtypeStruct((8, 2 * D), F32)],
        scratch_shapes=[pltpu.VMEM((CBUF, D), F32), pltpu.VMEM((CBUF, D), F32), pltpu.VMEM((8, HALO + CT, D), F32),
                        pltpu.VMEM((8, HALO + CT, D), F32), pltpu.VMEM((CONV_W * 8, D), F32), pltpu.VMEM((CT, D), F32),
                        pltpu.VMEM((CONV_W * 8, D), F32)],
        compiler_params=_cp(("arbitrary",)), name="conv_bwd",
    )(du1, du1, z, z, z, z, b_glu, w_dw)


MESH = pl.DeviceIdType.MESH


def _all_gather(name, shards):
    n = len(shards)

    def body(*refs):
        ins, outs = refs[:n], refs[n:2 * n]
        send_sems, recv_sems, local_sems = refs[2 * n:]
        x, y, c = lax.axis_index("x"), lax.axis_index("y"), lax.axis_index("c")
        me, sibling = (x, y, c), (x, y, 1 - c)
        chips = [(1 - x, y), (x, 1 - y), (1 - x, 1 - y)]

        def slot(a, px, py, pc):
            return outs[a].at[4 * px + 2 * py + pc]

        def copy(a, k, block, to, src=None):
            return pltpu.make_async_remote_copy(
                src_ref=slot(a, *block) if src is None else src, dst_ref=slot(a, *block),
                send_sem=send_sems.at[a, k], recv_sem=recv_sems.at[a, k], device_id=to, device_id_type=MESH)

        mine = [pltpu.make_async_copy(ins[a], slot(a, *me), local_sems.at[a]) for a in range(n)]
        for cp in mine:
            cp.start()
        first = []
        for a in range(n):
            first.append(copy(a, 0, me, sibling, src=ins[a]))
            first += [copy(a, 1 + j, me, (*chip, c), src=ins[a]) for j, chip in enumerate(chips)]
        for cp in first:
            cp.start()
        passed = []
        for j, chip in enumerate(chips):
            for a in range(n):
                copy(a, 1 + j, (*chip, c), me).wait_recv()
                fwd = copy(a, 4 + j, (*chip, c), sibling)
                fwd.start()
                passed.append(fwd)
        for a in range(n):
            copy(a, 0, sibling, me).wait_recv()
        for j, chip in enumerate(chips):
            for a in range(n):
                copy(a, 4 + j, (*chip, 1 - c), me).wait_recv()
        for cp in first + passed:
            cp.wait_send()
        for cp in mine:
            cp.wait()

    anyspec = pl.BlockSpec(memory_space=pl.ANY)
    return pl.pallas_call(
        body, in_specs=[anyspec] * n, out_specs=[anyspec] * n,
        out_shape=[jax.ShapeDtypeStruct((NDEV,) + s.shape, s.dtype) for s in shards],
        scratch_shapes=[pltpu.SemaphoreType.DMA((n, 7)), pltpu.SemaphoreType.DMA((n, 7)), pltpu.SemaphoreType.DMA((n,))],
        name=name,
    )(*shards)


HBM_SPEC = pl.BlockSpec(memory_space=pltpu.HBM)
SEM_SPEC = pl.BlockSpec(memory_space=pltpu.SEMAPHORE)
DATAFLOW = pltpu.SideEffectType.DATAFLOW_SIDE_EFFECTING


def _peers():
    x, y, c = lax.axis_index("x"), lax.axis_index("y"), lax.axis_index("c")
    out = []
    for k in range(1, NDEV):
        px = 1 - x if k & 4 else x
        py = 1 - y if k & 2 else y
        pc = 1 - c if k & 1 else c
        out.append(((px, py, pc), 4 * px + 2 * py + pc))
    return 4 * x + 2 * y + c, out


def _exchange_copies(srcs, lands, send_sems, recv_sems, gather):
    my, peers = _peers()
    pairs = []
    for k, (dev, pid) in enumerate(peers):
        for a in range(len(srcs)):
            src = srcs[a] if gather else srcs[a].at[pid]
            sems = dict(send_sem=send_sems[a * (NDEV - 1) + k], recv_sem=recv_sems[a * (NDEV - 1) + k], device_id=dev,
                        device_id_type=MESH)
            pairs.append((pltpu.make_async_remote_copy(src_ref=src, dst_ref=lands[a].at[my], **sems),
                          pltpu.make_async_remote_copy(src_ref=src, dst_ref=lands[a].at[pid], **sems)))
    return pairs


def _exchange_start(name, srcs, gather):
    n = len(srcs)
    ns = n * (NDEV - 1)
    shapes = [(s.shape if gather else s.shape[1:]) for s in srcs]
    lands = [lax.empty((NDEV,) + shp, s.dtype) for shp, s in zip(shapes, srcs)]

    def body(*refs):
        src_refs, land_refs = refs[:n], refs[n:2 * n]
        send_sems, recv_sems = refs[2 * n:2 * n + ns], refs[2 * n + ns:2 * n + 2 * ns]
        token = refs[-1]
        for mine, _ in _exchange_copies(src_refs, land_refs, send_sems, recv_sems, gather):
            mine.start()
        token[...] = jnp.zeros(token.shape, token.dtype)

    hbm = lambda a: pltpu.HBM(a.shape, a.dtype)
    res = pl.pallas_call(
        body, name=name,
        out_shape=(*([pltpu.SemaphoreType.DMA(())] * (2 * ns)), *[hbm(s) for s in srcs], *[hbm(l) for l in lands],
                   jax.ShapeDtypeStruct((8, 128), F32)),
        in_specs=[HBM_SPEC] * (2 * n),
        out_specs=(*([SEM_SPEC] * (2 * ns)), *([HBM_SPEC] * (2 * n)), pl.BlockSpec(memory_space=pltpu.VMEM)),
        input_output_aliases={i: 2 * ns + i for i in range(2 * n)},
        compiler_params=pltpu.CompilerParams(has_side_effects=DATAFLOW),
    )(*[pltpu.with_memory_space_constraint(s, pltpu.HBM) for s in srcs],
      *[pltpu.with_memory_space_constraint(l, pltpu.HBM) for l in lands])
    return list(res[:ns]), list(res[ns:2 * ns]), list(res[2 * ns:2 * ns + n]), list(res[2 * ns + n:2 * ns + 2 * n]), res[-1]


def _exchange_wait(name, handle, after, gather):
    send_sems, recv_sems, srcs, lands, _ = handle
    n = len(srcs)
    ns = n * (NDEV - 1)

    def body(*refs):
        src_refs, land_refs = refs[:n], refs[n:2 * n]
        s_sems, r_sems = refs[2 * n:2 * n + ns], refs[2 * n + ns:2 * n + 2 * ns]
        for mine, theirs in _exchange_copies(src_refs, land_refs, s_sems, r_sems, gather):
            mine.wait_send()
            theirs.wait_recv()

    hbm = lambda a: pltpu.HBM(a.shape, a.dtype)
    res = pl.pallas_call(
        body, name=name,
        out_shape=(*[hbm(s) for s in srcs], *[hbm(l) for l in lands]),
        in_specs=[HBM_SPEC] * (2 * n) + [SEM_SPEC] * (2 * ns) + [pl.BlockSpec(memory_space=pl.ANY)],
        out_specs=tuple([HBM_SPEC] * (2 * n)),
        input_output_aliases={i: i for i in range(2 * n)},
        compiler_params=pltpu.CompilerParams(has_side_effects=DATAFLOW),
    )(*srcs, *lands, *send_sems, *recv_sems, after)
    return list(res[n:])


def _set_own_slot(land, own):
    my = 4 * lax.axis_index("x") + 2 * lax.axis_index("y") + lax.axis_index("c")
    return lax.dynamic_update_slice(land, own[None], (my, 0, 0))


def _own_block(blocks):
    my = 4 * lax.axis_index("x") + 2 * lax.axis_index("y") + lax.axis_index("c")
    return lax.dynamic_index_in_dim(blocks, my, axis=0, keepdims=False)


_C1 = 1.0 - ADAM_B1 ** ADAM_STEP
_C2 = 1.0 - ADAM_B2 ** ADAM_STEP


def _adamw(name, w, m, v, recv, tr):
    R, C = w.shape

    def body(w_ref, m_ref, v_ref, r_ref, g_ref, d_ref, nm_ref, nv_ref):
        g = r_ref[0].astype(F32)
        for s in range(1, NDEV):
            g = g + r_ref[s].astype(F32)
        wv = w_ref[...]
        nm = ADAM_B1 * m_ref[...] + (1.0 - ADAM_B1) * g
        nv = ADAM_B2 * v_ref[...] + (1.0 - ADAM_B2) * (g * g)
        m_hat = nm / _C1
        v_hat = nv / _C2
        g_ref[...] = g
        d_ref[...] = -ADAM_LR * (m_hat / (jnp.sqrt(v_hat) + ADAM_EPS) + ADAM_WD * wv)
        nm_ref[...] = nm
        nv_ref[...] = nv

    blk = pl.BlockSpec((tr, C), lambda i: (i, 0))
    return pl.pallas_call(
        body, grid=(R // tr,), in_specs=[blk, blk, blk, pl.BlockSpec((NDEV, tr, C), lambda i: (0, i, 0))],
        out_specs=[blk] * 4, out_shape=[jax.ShapeDtypeStruct((R, C), F32)] * 4,
        compiler_params=_cp(), name=name,
    )(w, m, v, recv)


def _row(v):
    return v.reshape(1, -1)


def _local_step(xs, tgt, Wp, rest_fn, early_fn, late_fn, rel_bias_table, g_pre_mix, b_glu, b_dw, g_conv_ln,
                b_conv_ln, b_conv_out, g_post_mix, g_pre_ffn, g_post_ffn):
    S = xs.shape[0]
    g1, g2, g3, g4 = _row(g_pre_mix), _row(g_post_mix), _row(g_pre_ffn), _row(g_post_ffn)
    bglu, bdw, gln, bln, bco = _row(b_glu), _row(b_dw), _row(g_conv_ln), _row(b_conv_ln), _row(b_conv_out)
    full = (D, F32, D, 0, False)
    fullb = (D, BF, D, 0, False)

    def epi_rms(accs, r, c, o, p):
        v = r[0][...]
        o[0][...] = (v * _rms_r(v) * c[0][...]).astype(BF)

    (h1,) = _fused_mm("rms_in", S, 512, 1, [], [], [], [], [(xs, D, 0, False)], [g1], [fullb], [], epi_rms)

    def epi_cast(accs, r, c, o, p):
        o[0][...] = accs[0].astype(BF)

    ZT = IN_W // 4
    (z,) = _fused_mm("in_proj", S, 1024, 4, [(h1, D, 0, 1)], [(Wp, False, D, ZT, 0, 1, 0, 0, True)], [(0, 0, 0, 0, 1)],
                     [(1024, ZT)], [], [], [(IN_W, BF, ZT, 0, True)], [], epi_cast, n_outer=True)

    idx = jnp.asarray(_band_index())
    tab_t = rel_bias_table.T.reshape(3, HEADS, REL_BUCKETS)
    bias_all = _bias_build(tab_t, idx)
    bias_kq = [bias_all[g].reshape(HEADS, 2 * QBLK, QBLK) for g in range(3)]
    bias_kq2 = [bias_all[3 + g].reshape(HEADS, QBLK, 2 * QBLK) for g in range(3)]
    dils = [d for _, d in GROUPS]
    qkv = [(z, _kvq_blocks(0))] + [(_to_residue(f"qkv_to_residue_g{g}", z, _kvq_blocks(g), dils[g]), (0, 1, 2)) for g in (1, 2)]
    os_, ls_ = [], []
    for g in range(3):
        o_g, l_g = _attn_fwd(f"attn_fwd_g{g}", qkv[g][0], bias_kq[g], qkv[g][1], dils[g])
        os_.append(o_g)
        ls_.append(l_g)
    o_att, lse = _attn_merge(os_, ls_, S)

    Wfi, Wfo, Wco, Wmo, Wao, wdw = rest_fn(lse)
    u1, u3 = _conv_fwd(z, bglu, wdw, bdw, gln, bln)

    def epi_mix(accs, r, c, o, p):
        ya = accs[0]
        yc = accs[1] + c[0][...]
        mg = _sig(r[0][...].astype(F32)) * ya + _sig(r[1][...].astype(F32)) * yc
        mgb = mg.astype(BF)
        m2 = jnp.dot(mgb, c[1][...], preferred_element_type=F32)
        x1 = r[2][...] + m2 * _rms_r(m2) * c[2][...]
        o[0][...] = ya.astype(BF)
        o[1][...] = yc.astype(BF)
        o[2][...] = mgb
        o[3][...] = m2.astype(BF)
        o[4][...] = x1
        o[5][...] = (x1 * _rms_r(x1) * c[3][...]).astype(BF)

    y_attn, y_conv, merged, m2, x1, h2 = _fused_mm(
        "mix_fwd", S, 512, 1, [(o_att, GW, 0, 1), (u3, D, 0, 1)],
        [(Wao, False, GW, D, 0, 1, 0, 0, False), (Wco, False, D, D, 0, 1, 0, 0, False)], [(0, 0, 0, 0, 1), (1, 1, 1, 0, 1)],
        [(512, D), (512, D)], [(z, D, 2, False), (z, D, 3, False), (xs, D, 0, False)], [bco, Wmo, g2, g3],
        [fullb, fullb, fullb, fullb, full, fullb], [], epi_mix)

    HN = FFN // 2

    def epi_ffn_in(accs, r, c, o, p):
        gt, up = accs
        o[0][...] = gt.astype(BF)
        o[1][...] = up.astype(BF)
        o[2][...] = (gt * _sig(gt) * up).astype(BF)

    gate, up, act = _fused_mm(
        "ffn_in", S, 512, 2, [(h2, D, 0, 1)],
        [(Wfi, False, D, HN, 0, 1, 0, 0, True), (Wfi, False, D, HN, 0, 1, 0, 2, True)], [(0, 0, 0, 0, 1), (0, 1, 1, 0, 1)],
        [(512, HN), (512, HN)], [], [], [(FFN, BF, HN, 0, True)] * 3, [], epi_ffn_in, n_outer=True)

    def epi_loss(accs, r, c, o, p):
        f2 = accs[0]
        g = c[0][...]
        rr = _rms_r(f2)
        err = r[0][...] + f2 * rr * g - r[1][...]
        dy = err * (1.0 / D)
        df2, dgr = _rms_bwd(f2, rr, g, dy)
        o[0][...] = dy
        o[1][...] = df2.astype(BF)
        p[0][...] += _psum8(err * err)
        p[1][...] += _psum8(dgr)

    dy, df2, loss_p, dg4 = _fused_mm(
        "ffn_out_loss", S, 512, 1, [(act, FFN, 0, 1)], [(Wfo, False, FFN, D, 0, 1, 0, 0, False)], [(0, 0, 0, 0, 1)],
        [(512, D)], [(x1, D, 0, False), (tgt, D, 0, False)], [g4], [full, fullb], [(8, D), (8, D)], epi_loss)

    def epi_swiglu(accs, r, c, o, p):
        da = accs[0]
        gt = r[0][...].astype(F32)
        sg = _sig(gt)
        o[0][...] = (da * r[1][...].astype(F32) * sg * (1.0 + gt * (1.0 - sg))).astype(BF)
        o[1][...] = (da * gt * sg).astype(BF)

    dgate, dup = _fused_mm(
        "ffn_out_bwd", S, 512, 2, [(df2, D, 0, 1)], [(Wfo, True, D, HN, 0, 1, 0, 0, True)], [(0, 0, 0, 0, 1)],
        [(512, HN)], [(gate, HN, 0, True), (up, HN, 0, True)], [], [(FFN, BF, HN, 0, True)] * 2, [], epi_swiglu, n_outer=True)
    dWfo = _mm_tn("dw_ffn_out", act, df2, HN, D, 1024)

    def epi_dh2(accs, r, c, o, p):
        dh2 = accs[0]
        x1v = r[1][...]
        r3 = _rms_r(x1v)
        d1, dg3r = _rms_bwd(x1v, r3, c[0][...], dh2)
        dx1 = r[0][...] + d1
        m2v = r[2][...].astype(F32)
        r2 = _rms_r(m2v)
        dm2, dg2r = _rms_bwd(m2v, r2, c[1][...], dx1)
        o[0][...] = dx1
        o[1][...] = dm2.astype(BF)
        p[0][...] += _psum8(dg3r)
        p[1][...] += _psum8(dg2r)

    dx1, dm2, dg3, dg2 = _resident_mm(
        "ffn_in_bwd", S, 512, [dgate, dup], Wfi, [(dy, D), (x1, D), (m2, D)], [g3, g2], [(D, F32), (D, BF)], [(8, D), (8, D)], epi_dh2)
    dWfi = jnp.concatenate([_mm_tn("dw_ffn_gate", h2, dgate, D, HN, 1024), _mm_tn("dw_ffn_up", h2, dup, D, HN, 1024)], axis=1)

    def epi_dmix(accs, r, c, o, p):
        dm = accs[0]
        sa = _sig(r[0][...].astype(F32))
        sc = _sig(r[1][...].astype(F32))
        o[0][...] = (dm * sa).astype(BF)
        o[1][...] = (dm * sc).astype(BF)
        o[2][:, 0:D] = (dm * r[2][...].astype(F32) * sa * (1.0 - sa)).astype(BF)
        o[2][:, D:2 * D] = (dm * r[3][...].astype(F32) * sc * (1.0 - sc)).astype(BF)

    dy_attn, dy_conv, dz_gate = _fused_mm(
        "mix_bwd", S, 512, 1, [(dm2, D, 0, 1)], [(Wmo, True, D, D, 0, 1, 0, 0, False)], [(0, 0, 0, 0, 1)], [(512, D)],
        [(z, D, 2, False), (z, D, 3, False), (y_attn, D, 0, False), (y_conv, D, 0, False)], [],
        [fullb, fullb, (2 * D, BF, 2 * D, 0, False)], [], epi_dmix)
    dWmo = _mm_tn("dw_mix_out", merged, dm2, D, D, 1024)

    def epi_dconv(accs, r, c, o, p):
        du3 = accs[0]
        xh, rstd = _ln_hat(r[0][...])
        gl = c[0][...]
        u2 = xh * gl + c[1][...]
        sg = _sig(u2)
        du2 = du3 * sg * (1.0 + u2 * (1.0 - sg))
        dxh = du2 * gl
        du1 = rstd * (dxh - jnp.mean(dxh, axis=-1, keepdims=True) - xh * jnp.mean(dxh * xh, axis=-1, keepdims=True))
        o[0][...] = du1
        p[0][...] += _psum8(du2 * xh)
        p[1][...] += _psum8(du2)
        p[2][...] += _psum8(du1)
        p[3][...] += _psum8(r[1][...].astype(F32))

    du1, dgln, dbln, dbdw, dbco = _fused_mm(
        "conv_out_bwd", S, 512, 1, [(dy_conv, D, 0, 1)], [(Wco, True, D, D, 0, 1, 0, 0, False)], [(0, 0, 0, 0, 1)], [(512, D)],
        [(u1, D, 0, False), (dy_conv, D, 0, False)], [gln, bln], [full], [(8, D)] * 4, epi_dconv)
    dWco = _mm_tn("dw_conv_out", u3, dy_conv, D, D, 1024)
    dz_glu, dwdw, dbglu = _conv_bwd(du1, z, bglu, wdw)

    head_sum = np.zeros((GW, STAT_W), np.float32)
    for h in range(HEADS):
        head_sum[h * HEAD_DIM:(h + 1) * HEAD_DIM, HEADS + h] = 1.0
    head_sum = jnp.asarray(head_sum)

    def epi_do(accs, r, c, o, p):
        do = accs[0]
        o[0][...] = do.astype(BF)
        delta = jnp.dot(do * r[0][...].astype(F32), c[0][...], preferred_element_type=F32, precision=lax.Precision.HIGHEST)
        lane = lax.broadcasted_iota(jnp.int32, delta.shape, 1)
        o[1][...] = jnp.where(lane < HEADS, r[1][...], delta)

    do, stats = _fused_mm(
        "attn_out_bwd", S, 1024, 1, [(dy_attn, D, 0, 1)], [(Wao, True, D, GW, 0, 1, 0, 0, False)], [(0, 0, 0, 0, 1)], [(1024, GW)],
        [(o_att, GW, 0, False), (lse, STAT_W, 0, False)], [head_sum], [(GW, BF, GW, 0, False), (STAT_W, F32, STAT_W, 0, False)], [], epi_do)
    dWao = _mm_tn("dw_attn_out", o_att, dy_attn, GW, D, 1024)

    tie = early_fn(dict(w_ffn_in=dWfi, w_ffn_out=dWfo, w_conv_out=dWco, w_mix_out=dWmo, w_attn_out=dWao, w_dw=dwdw))
    stats = stats + tie
    dos = [do] + [_to_residue(f"do_to_residue_g{g}", do, (0,), dils[g]) for g in (1, 2)]
    sts = [stats] + [_to_residue_stats(f"stats_to_residue_g{g}", stats, dils[g]) for g in (1, 2)]
    dqkv, dbs = [], []
    for g in range(3):
        arr, cb = qkv[g]
        dg, db = _attn_bwd(f"attn_bwd_g{g}", arr, bias_kq2[g], dos[g], sts[g], cb, dils[g])
        dqkv.append(dg if g == 0 else _from_residue(f"dqkv_from_residue_g{g}", dg, dils[g]))
        dbs.append(db.reshape(HEADS, _NB))
    dtab = _bias_grad(jnp.stack(dbs), idx)[:, :, :REL_BUCKETS].reshape(3 * HEADS, REL_BUCKETS).T

    def epi_dx(accs, r, c, o, p):
        xv = r[1][...]
        d1, dg1r = _rms_bwd(xv, _rms_r(xv), c[0][...], accs[0])
        o[0][...] = r[0][...] + d1
        p[0][...] += _psum8(dg1r)

    dWg = [_mm_tn(f"dw_in_g{g}", h1, dqkv[g], D, ATTN_W, 1024) for g in range(3)]
    dW_in = jnp.concatenate(
        [t[:, 2 * GW:] for t in dWg] + [t[:, :GW] for t in dWg] + [t[:, GW:2 * GW] for t in dWg]
        + [_mm_tn("dw_in_glu", h1, dz_glu, D, D, 1024), _mm_tn("dw_in_gate", h1, dz_gate, D, D, 1024)], axis=1)
    g1_late = g1 + late_fn(dW_in)
    grad_x, dg1 = _resident_mm(
        "in_proj_bwd", S, 512, [dz_glu, dz_gate, dqkv[0], dqkv[1], dqkv[2]], Wp, [(dx1, D), (xs, D)], [g1_late], [(D, F32)], [(8, D)], epi_dx)

    small = dict(rel_bias_table=dtab, g_pre_mix=dg1[0], b_glu=dbglu[0], b_dw=dbdw[0], g_conv_ln=dgln[0], b_conv_ln=dbln[0],
                 b_conv_out=dbco[0], g_post_mix=dg2[0], g_pre_ffn=dg3[0], g_post_ffn=dg4[0])
    return loss_p[0], grad_x, small


SMALL = ['rel_bias_table', 'g_pre_mix', 'b_glu', 'b_dw', 'g_conv_ln', 'b_conv_ln', 'b_conv_out', 'g_post_mix', 'g_pre_ffn',
         'g_post_ffn']
BIG = ['w_in', 'w_ffn_in', 'w_ffn_out', 'w_conv_out', 'w_mix_out', 'w_attn_out', 'w_dw']
WEIGHTS = ['rel_bias_table', 'g_pre_mix', 'w_in', 'b_glu', 'w_dw', 'b_dw', 'g_conv_ln', 'b_conv_ln', 'w_conv_out', 'b_conv_out',
           'w_attn_out', 'w_mix_out', 'g_post_mix', 'g_pre_ffn', 'w_ffn_in', 'w_ffn_out', 'g_post_ffn']
SMALL_ROWS = 16


ROW_SMALL = ['g_pre_mix', 'b_glu', 'b_dw', 'g_conv_ln', 'b_conv_ln', 'b_conv_out', 'g_post_mix', 'g_pre_ffn', 'g_post_ffn']
LOSS_ROW = 10
TAB_LANES = 128


def _small_rows(small, loss_row):
    rows = [small[n].reshape(-1, D) for n in ROW_SMALL] + [loss_row.reshape(1, D)]
    n = sum(r.shape[0] for r in rows)
    return jnp.concatenate(rows + [jnp.zeros((SMALL_ROWS - n, D), F32)], axis=0)


def _adamw_small(recv_rows, recv_tab, ws, ms, vs):
    np_ = len(SMALL)

    def body(*refs):
        rr, rt = refs[0], refs[1]
        w_refs, m_refs, v_refs = refs[2:2 + np_], refs[2 + np_:2 + 2 * np_], refs[2 + 2 * np_:2 + 3 * np_]
        loss_ref = refs[2 + 3 * np_]
        outs = refs[3 + 3 * np_:]
        rows = rr[0]
        tab = rt[0]
        for s_ in range(1, NDEV):
            rows = rows + rr[s_]
            tab = tab + rt[s_]
        loss_ref[...] = jnp.sum(rows[LOSS_ROW:LOSS_ROW + 1, :], axis=1, keepdims=True) * (0.5 / D)
        row = 0
        for p, n in enumerate(SMALL):
            if n == 'rel_bias_table':
                g = tab[:, 0:3 * HEADS]
            else:
                k = w_refs[p].shape[1] // D
                g = rows[row:row + 1, :] if k == 1 else jnp.concatenate([rows[row + t:row + t + 1, :] for t in range(k)], axis=1)
                row += k
            nm = ADAM_B1 * m_refs[p][...] + (1.0 - ADAM_B1) * g
            nv = ADAM_B2 * v_refs[p][...] + (1.0 - ADAM_B2) * (g * g)
            outs[4 * p][...] = g
            outs[4 * p + 1][...] = -ADAM_LR * ((nm / _C1) / (jnp.sqrt(nv / _C2) + ADAM_EPS) + ADAM_WD * w_refs[p][...])
            outs[4 * p + 2][...] = nm
            outs[4 * p + 3][...] = nv

    out_shape = [jax.ShapeDtypeStruct((1, 1), F32)]
    for a_ in ws:
        out_shape += [jax.ShapeDtypeStruct(a_.shape, F32)] * 4
    res = pl.pallas_call(body, out_shape=out_shape, compiler_params=_cp(), name="adamw_small")(recv_rows, recv_tab, *ws, *ms, *vs)
    return res[0], [tuple(res[1 + 4 * p:5 + 4 * p]) for p in range(np_)]


def _cols_to_blocks(a):
    R = a.shape[0]
    return a.reshape(R, NDEV, a.shape[1] // NDEV).transpose(1, 0, 2)


def _blocks_to_cols(a):
    return a.transpose(1, 0, 2).reshape(a.shape[1], NDEV * a.shape[2])


def kernel(x, rel_bias_table, g_pre_mix, w_in, b_glu, w_dw, b_dw, g_conv_ln, b_conv_ln, w_conv_out, b_conv_out, w_attn_out, w_mix_out, g_post_mix, g_pre_ffn, w_ffn_in, w_ffn_out, g_post_ffn, loss_target, m_rel_bias_table, m_g_pre_mix, m_w_in, m_b_glu, m_w_dw, m_b_dw, m_g_conv_ln, m_b_conv_ln, m_w_conv_out, m_b_conv_out, m_w_attn_out, m_w_mix_out, m_g_post_mix, m_g_pre_ffn, m_w_ffn_in, m_w_ffn_out, m_g_post_ffn, v_rel_bias_table, v_g_pre_mix, v_w_in, v_b_glu, v_w_dw, v_b_dw, v_g_conv_ln, v_b_conv_ln, v_w_conv_out, v_b_conv_out, v_w_attn_out, v_w_mix_out, v_g_post_mix, v_g_pre_ffn, v_w_ffn_in, v_w_ffn_out, v_g_post_ffn):
    w = dict(rel_bias_table=rel_bias_table, g_pre_mix=g_pre_mix, w_in=w_in, b_glu=b_glu, w_dw=w_dw, b_dw=b_dw, g_conv_ln=g_conv_ln, b_conv_ln=b_conv_ln, w_conv_out=w_conv_out, b_conv_out=b_conv_out, w_attn_out=w_attn_out, w_mix_out=w_mix_out, g_post_mix=g_post_mix, g_pre_ffn=g_pre_ffn, w_ffn_in=w_ffn_in, w_ffn_out=w_ffn_out, g_post_ffn=g_post_ffn)
    m = dict(rel_bias_table=m_rel_bias_table, g_pre_mix=m_g_pre_mix, w_in=m_w_in, b_glu=m_b_glu, w_dw=m_w_dw, b_dw=m_b_dw, g_conv_ln=m_g_conv_ln, b_conv_ln=m_b_conv_ln, w_conv_out=m_w_conv_out, b_conv_out=m_b_conv_out, w_attn_out=m_w_attn_out, w_mix_out=m_w_mix_out, g_post_mix=m_g_post_mix, g_pre_ffn=m_g_pre_ffn, w_ffn_in=m_w_ffn_in, w_ffn_out=m_w_ffn_out, g_post_ffn=m_g_post_ffn)
    v = dict(rel_bias_table=v_rel_bias_table, g_pre_mix=v_g_pre_mix, w_in=v_w_in, b_glu=v_b_glu, w_dw=v_w_dw, b_dw=v_b_dw, g_conv_ln=v_g_conv_ln, b_conv_ln=v_b_conv_ln, w_conv_out=v_w_conv_out, b_conv_out=v_b_conv_out, w_attn_out=v_w_attn_out, w_mix_out=v_w_mix_out, g_post_mix=v_g_post_mix, g_pre_ffn=v_g_pre_ffn, w_ffn_in=v_w_ffn_in, w_ffn_out=v_w_ffn_out, g_post_ffn=v_g_post_ffn)

    def shard2d(d, n):
        a = d[n][0]
        return jnp.pad(a, ((0, HALO - CONV_W), (0, 0))) if n == 'w_dw' else a

    own = {n: shard2d(w, n).astype(F32 if n == 'w_dw' else BF) for n in BIG}
    packed = ['w_ffn_out', 'w_conv_out', 'w_mix_out', 'w_attn_out']
    alone = ['w_ffn_in', 'w_dw']
    shapes = [own[n].shape for n in packed]

    def pack(arrs, lead):
        return jnp.concatenate([a.reshape(lead + (-1, D)) for a in arrs], axis=len(lead))

    def unpack(p):
        out, pos = {}, 0
        for n, shp in zip(packed, shapes):
            rows = shp[0] * shp[1] // D
            out[n] = p[:, pos:pos + rows].reshape((NDEV,) + shp)
            pos += rows
        return out

    (g_in,) = _all_gather("gather_w_in", [own['w_in']])
    rest_own = [pack([own[n] for n in packed], ())] + [own[n] for n in alone]
    g_in, rest_own = lax.optimization_barrier((g_in, rest_own))
    gather_rest = _exchange_start("gather_rest_start", rest_own, True)
    W_in = _blocks_to_cols(g_in)
    kvq = [W_in[:, t * ATTN_W + g * GW:t * ATTN_W + (g + 1) * GW] for g in range(3) for t in (1, 2, 0)]
    Wp = jnp.concatenate([W_in[:, 3 * ATTN_W:]] + kvq, axis=1)

    def rest_fn(after):
        lands = _exchange_wait("gather_rest_wait", gather_rest, after, True)
        gw = unpack(_set_own_slot(lands[0], rest_own[0]))
        for n, l, o in zip(alone, lands[1:], rest_own[1:]):
            gw[n] = _set_own_slot(l, o)
        return (_blocks_to_cols(gw['w_ffn_in']), gw['w_ffn_out'].reshape(FFN, D), gw['w_conv_out'].reshape(D, D),
                gw['w_mix_out'].reshape(D, D), _blocks_to_cols(gw['w_attn_out']), _blocks_to_cols(gw['w_dw']))

    def to_blocks(n, g):
        if n in ('w_in', 'w_ffn_in', 'w_attn_out', 'w_dw'):
            return _cols_to_blocks(g)
        return g.reshape(NDEV, g.shape[0] // NDEV, g.shape[1])

    started = {}

    def early_fn(grads):
        blocks = [pack([to_blocks(n, grads[n]) for n in packed], (NDEV,))] + [to_blocks(n, grads[n]) for n in alone]
        started['blocks'] = blocks
        started['handle'] = _exchange_start("scatter_early_start", blocks, False)
        return started['handle'][4][0:1, 0:1]

    def late_fn(dW_in):
        started['in_blocks'] = [to_blocks('w_in', dW_in)]
        started['in_handle'] = _exchange_start("scatter_w_in_start", started['in_blocks'], False)
        return started['in_handle'][4][0:1, 0:1]

    g1_tied = g_pre_mix[0] + gather_rest[4][0, 0:1]
    loss_row, grad_x, small = _local_step(
        x[0], loss_target[0], Wp, rest_fn, early_fn, late_fn, rel_bias_table, g1_tied, b_glu[0], b_dw[0], g_conv_ln[0],
        b_conv_ln[0], b_conv_out[0], g_post_mix[0], g_pre_ffn[0], g_post_ffn[0])

    lands = _exchange_wait("scatter_early_wait", started['handle'], grad_x, False)
    lands = [_set_own_slot(l, _own_block(b)) for l, b in zip(lands, started['blocks'])]
    recv = unpack(lands[0])
    recv.update(zip(alone, lands[1:]))
    (land_in,) = _exchange_wait("scatter_w_in_wait", started['in_handle'], grad_x, False)
    recv['w_in'] = _set_own_slot(land_in, _own_block(started['in_blocks'][0]))
    tiles = dict(w_in=128, w_ffn_in=256, w_ffn_out=176, w_conv_out=128, w_mix_out=128, w_attn_out=512, w_dw=HALO)
    res = {}
    for n in BIG:
        g_, d_, nm_, nv_ = _adamw("adamw_" + n, shard2d(w, n), shard2d(m, n), shard2d(v, n), recv[n], tiles[n])
        if n == 'w_dw':
            g_, d_, nm_, nv_ = (t[:CONV_W] for t in (g_, d_, nm_, nv_))
        res[n] = tuple(t[None] for t in (g_, d_, nm_, nv_))

    tab = jnp.pad(small['rel_bias_table'], ((0, 0), (0, TAB_LANES - 3 * HEADS)))
    srows, stab = _all_gather("gather_small_grads", [_small_rows(small, loss_row), tab])
    loss11, small_res = _adamw_small(srows, stab, [w[n] for n in SMALL], [m[n] for n in SMALL], [v[n] for n in SMALL])
    loss = loss11.reshape(())
    for n, r in zip(SMALL, small_res):
        res[n] = r
    return (loss, grad_x[None], *[res[n][0] for n in WEIGHTS], *[res[n][1] for n in WEIGHTS],
            *[res[n][2] for n in WEIGHTS], *[res[n][3] for n in WEIGHTS])
```

```python
import functools
import math

import numpy as np
import jax
import jax.numpy as jnp
from jax import lax
from jax.experimental import pallas as pl
from jax.experimental.pallas import tpu as pltpu

F32 = jnp.float32
BF = jnp.bfloat16

D = 1024
HEAD_DIM = 64
HEADS = 8
GROUPS = ((128, 1), (512, 4), (2048, 16))
QBLK = 128
GW = HEADS * HEAD_DIM
ATTN_W = 3 * GW
REL_BUCKETS = 32
REL_MAX_DISTANCE = 2048
CONV_W = 31
HALO = 32
FFN = 2816
IN_W = 3 * ATTN_W + 2 * D + 2 * D
RMS_EPS = 1e-6
LN_EPS = 1e-5
NEG_INF = -1e30
SCALE = HEAD_DIM ** -0.5
NDEV = 8

ADAM_LR = 0.001
ADAM_B1 = 0.9
ADAM_B2 = 0.999
ADAM_EPS = 1e-08
ADAM_WD = 0.01
ADAM_STEP = 10

Z_G0 = 4096 // GW
Z_CB = IN_W // GW


def _kvq_blocks(g):
    return (Z_G0 + 3 * g, Z_G0 + 3 * g + 1, Z_G0 + 3 * g + 2)


WG_TK = 2048
VMEM_LIMIT = 52 * 1024 * 1024


def _cp(sem=None):
    if sem is None:
        return pltpu.CompilerParams(vmem_limit_bytes=VMEM_LIMIT)
    return pltpu.CompilerParams(vmem_limit_bytes=VMEM_LIMIT, dimension_semantics=sem)


def _sig(v):
    return jax.nn.sigmoid(v)


def _psum8(v):
    return v.reshape(v.shape[0] // 8, 8, v.shape[1]).sum(axis=0)


def _rms_r(v):
    return lax.rsqrt(jnp.mean(v * v, axis=-1, keepdims=True) + RMS_EPS)


def _rms_bwd(v, r, g, dy):
    gy = dy * g
    dv = r * gy - v * (r * r * r) * jnp.mean(v * gy, axis=-1, keepdims=True)
    return dv, dy * v * r


def _clip_k(k, k0, nk):
    return jnp.clip(k - k0, 0, nk - 1)


def _fused_mm(name, M, tm, grid_n, a_ops, b_ops, terms, acc_shapes, rows, consts, outs, parts, epilogue, n_outer=False):
    gm = M // tm
    nk_total = max([t[3] + t[4] for t in terms], default=1)
    n_a, n_b, n_r, n_c, n_o, n_p = len(a_ops), len(b_ops), len(rows), len(consts), len(outs), len(parts)
    n_acc = len(acc_shapes)
    use_scratch = nk_total > 1
    if parts:
        assert grid_n == 1

    def jj(j, follow):
        return j if follow else 0

    in_specs, args = [], []
    for (arr, tk, k0, nk) in a_ops:
        in_specs.append(pl.BlockSpec((tm, tk), functools.partial(lambda i, j, k, k0, nk: (i, _clip_k(k, k0, nk)), k0=k0, nk=nk)))
        args.append(arr)
    for (arr, nt, tk, tn, k0, nk, koff, joff, fj) in b_ops:
        if nt:
            in_specs.append(pl.BlockSpec((tn, tk), functools.partial(
                lambda i, j, k, k0, nk, koff, joff, fj: (joff + jj(j, fj), _clip_k(k, k0, nk) + koff),
                k0=k0, nk=nk, koff=koff, joff=joff, fj=fj)))
        else:
            in_specs.append(pl.BlockSpec((tk, tn), functools.partial(
                lambda i, j, k, k0, nk, koff, joff, fj: (_clip_k(k, k0, nk) + koff, joff + jj(j, fj)),
                k0=k0, nk=nk, koff=koff, joff=joff, fj=fj)))
        args.append(arr)
    for (arr, w, off, fj) in rows:
        in_specs.append(pl.BlockSpec((tm, w), functools.partial(lambda i, j, k, off, fj: (i, off + jj(j, fj)), off=off, fj=fj)))
        args.append(arr)
    for arr in consts:
        in_specs.append(pl.BlockSpec(arr.shape, functools.partial(lambda i, j, k, nd: (0,) * nd, nd=arr.ndim)))
        args.append(arr)
    out_specs, out_shape = [], []
    for (ncols, dt, w, off, fj) in outs:
        out_specs.append(pl.BlockSpec((tm, w), functools.partial(lambda i, j, k, off, fj: (i, off + jj(j, fj)), off=off, fj=fj)))
        out_shape.append(jax.ShapeDtypeStruct((M, ncols), dt))
    for (r, c) in parts:
        out_specs.append(pl.BlockSpec((r, c), lambda i, j, k: (0, 0)))
        out_shape.append(jax.ShapeDtypeStruct((r, c), F32))
    scratch = [pltpu.VMEM(s, F32) for s in acc_shapes] if use_scratch else []

    def body(*refs):
        pos = 0
        a_refs = refs[pos:pos + n_a]; pos += n_a
        b_refs = refs[pos:pos + n_b]; pos += n_b
        r_refs = refs[pos:pos + n_r]; pos += n_r
        c_refs = refs[pos:pos + n_c]; pos += n_c
        o_refs = refs[pos:pos + n_o]; pos += n_o
        p_refs = refs[pos:pos + n_p]; pos += n_p
        acc_refs = refs[pos:pos + n_acc] if use_scratch else ()
        i = pl.program_id(0)
        k = pl.program_id(2)

        def dot_of(ai, bi):
            a = a_refs[ai][...].astype(BF)
            b = b_refs[bi][...].astype(BF)
            if b_ops[bi][1]:
                return lax.dot_general(a, b, (((1,), (1,)), ((), ())), preferred_element_type=F32)
            return jnp.dot(a, b, preferred_element_type=F32)

        if parts:
            @pl.when((i == 0) & (k == 0))
            def _():
                for p in p_refs:
                    p[...] = jnp.zeros(p.shape, F32)

        def finish(accs):
            epilogue(accs, r_refs, c_refs, o_refs, p_refs)
            if parts:
                @pl.when(i == gm - 1)
                def _():
                    for p in p_refs:
                        p[0:1, :] = jnp.sum(p[...], axis=0, keepdims=True)

        if not use_scratch:
            accs = [None] * n_acc
            for (ai, bi, ci, k0, nk) in terms:
                d = dot_of(ai, bi)
                accs[ci] = d if accs[ci] is None else accs[ci] + d
            finish(accs)
        else:
            @pl.when(k == 0)
            def _():
                for acc in acc_refs:
                    acc[...] = jnp.zeros(acc.shape, F32)

            for (ai, bi, ci, k0, nk) in terms:
                def do(ai=ai, bi=bi, ci=ci):
                    acc_refs[ci][...] += dot_of(ai, bi)
                if k0 == 0 and nk == nk_total:
                    do()
                else:
                    pl.when((k >= k0) & (k < k0 + nk))(do)

            @pl.when(k == nk_total - 1)
            def _():
                finish([acc[...] for acc in acc_refs])

    grid = (gm, grid_n, nk_total)
    if n_outer:
        assert not parts
        swap = lambda spec: pl.BlockSpec(spec.block_shape, functools.partial(lambda j, i, k, f: f(i, j, k), f=spec.index_map))
        in_specs, out_specs, grid = [swap(sp) for sp in in_specs], [swap(sp) for sp in out_specs], (grid_n, gm, nk_total)
    res = pl.pallas_call(
        body, grid=grid, in_specs=in_specs, out_specs=out_specs, out_shape=out_shape,
        scratch_shapes=scratch, compiler_params=_cp(("arbitrary", "arbitrary", "arbitrary")), name=name,
    )(*args)
    return res


def _mm_tn(name, a, b, tm, tn, tk):
    S, Ka = a.shape
    Nb = b.shape[1]
    nk = S // tk

    def body(a_ref, b_ref, o_ref, acc):
        k = pl.program_id(2)

        @pl.when(k == 0)
        def _():
            acc[...] = jnp.zeros(acc.shape, F32)

        acc[...] += lax.dot_general(a_ref[...], b_ref[...], (((0,), (0,)), ((), ())), preferred_element_type=F32)

        @pl.when(k == nk - 1)
        def _():
            o_ref[...] = acc[...].astype(o_ref.dtype)

    return pl.pallas_call(
        body, grid=(Ka // tm, Nb // tn, nk),
        in_specs=[pl.BlockSpec((tk, tm), lambda i, j, k: (k, i)), pl.BlockSpec((tk, tn), lambda i, j, k: (k, j))],
        out_specs=pl.BlockSpec((tm, tn), lambda i, j, k: (i, j)),
        out_shape=jax.ShapeDtypeStruct((Ka, Nb), BF),
        scratch_shapes=[pltpu.VMEM((tm, tn), F32)],
        compiler_params=_cp(("parallel", "parallel", "arbitrary")), name=name,
    )(a, b)


def _resident_mm(name, M, tm, a_segs, w, rows, consts, outs, parts, epilogue):
    gm = M // tm
    n_a, n_r, n_c, n_o, n_p = len(a_segs), len(rows), len(consts), len(outs), len(parts)
    widths = [a.shape[1] for a in a_segs]
    offs = [sum(widths[:t]) for t in range(n_a)]
    once = pl.Buffered(1)

    def body(*refs):
        pos = 0
        a_refs = refs[pos:pos + n_a]; pos += n_a
        w_ref = refs[pos]; pos += 1
        r_refs = refs[pos:pos + n_r]; pos += n_r
        c_refs = refs[pos:pos + n_c]; pos += n_c
        o_refs = refs[pos:pos + n_o]; pos += n_o
        p_refs = refs[pos:pos + n_p]
        i = pl.program_id(0)
        if parts:
            @pl.when(i == 0)
            def _():
                for p in p_refs:
                    p[...] = jnp.zeros(p.shape, F32)
        acc = None
        for t in range(n_a):
            d = lax.dot_general(a_refs[t][...], w_ref[:, offs[t]:offs[t] + widths[t]], (((1,), (1,)), ((), ())),
                                preferred_element_type=F32)
            acc = d if acc is None else acc + d
        epilogue([acc], r_refs, c_refs, o_refs, p_refs)
        if parts:
            @pl.when(i == gm - 1)
            def _():
                for p in p_refs:
                    p[0:1, :] = jnp.sum(p[...], axis=0, keepdims=True)

    in_specs = [pl.BlockSpec((tm, wd), lambda i: (i, 0)) for wd in widths]
    in_specs.append(pl.BlockSpec(w.shape, lambda i: (0, 0), pipeline_mode=once))
    in_specs += [pl.BlockSpec((tm, c), lambda i: (i, 0)) for _, c in rows]
    in_specs += [pl.BlockSpec(c.shape, lambda i: (0, 0), pipeline_mode=once) for c in consts]
    out_specs = [pl.BlockSpec((tm, nc), lambda i: (i, 0)) for nc, _ in outs] + [pl.BlockSpec(pc, lambda i: (0, 0)) for pc in parts]
    out_shape = [jax.ShapeDtypeStruct((M, nc), dt) for nc, dt in outs] + [jax.ShapeDtypeStruct(pc, F32) for pc in parts]
    return pl.pallas_call(
        body, grid=(gm,), in_specs=in_specs, out_specs=out_specs, out_shape=out_shape,
        compiler_params=_cp(("arbitrary",)), name=name,
    )(*a_segs, w, *[r for r, _ in rows], *consts)


def _rel_bucket_np(dist):
    max_exact = REL_BUCKETS // 2
    d = np.maximum(dist, 0)
    df = np.maximum(d, 1).astype(np.float32)
    large = max_exact + (np.log(df / np.float32(max_exact)) / np.float32(math.log(REL_MAX_DISTANCE / max_exact))
                         * np.float32(REL_BUCKETS - max_exact)).astype(np.int32)
    large = np.minimum(large, REL_BUCKETS - 1)
    return np.where(d < max_exact, d, large).astype(np.int32)


N_LAYOUTS = 2


def _band_index():
    idx = np.zeros((N_LAYOUTS * 3, 1, QBLK * 2 * QBLK), np.int32)
    for g, (window, dil) in enumerate(GROUPS):
        span = window // dil
        k = np.arange(2 * QBLK)[:, None]; q = np.arange(QBLK)[None, :]
        off = q - k + QBLK
        idx[g, 0] = np.where((off >= 0) & (off <= span), _rel_bucket_np(off * dil), -1).reshape(-1)
        k = np.arange(QBLK)[:, None]; q = np.arange(2 * QBLK)[None, :]
        off = q - k
        idx[3 + g, 0] = np.where((off >= 0) & (off <= span), _rel_bucket_np(off * dil), -1).reshape(-1)
    return idx


_NB = QBLK * 2 * QBLK
_BCH = 4096


def _bias_build(tab_t, idx):
    def body(t_ref, i_ref, o_ref):
        ix = i_ref[0]
        t = t_ref[0]
        acc = jnp.full((HEADS, _BCH), NEG_INF, F32)
        for b in range(REL_BUCKETS):
            acc = jnp.where(ix == b, t[:, b:b + 1], acc)
        o_ref[0] = acc

    return pl.pallas_call(
        body, grid=(N_LAYOUTS * 3, _NB // _BCH),
        in_specs=[pl.BlockSpec((1, HEADS, REL_BUCKETS), lambda l, n: (l % 3, 0, 0)),
                  pl.BlockSpec((1, 1, _BCH), lambda l, n: (l, 0, n))],
        out_specs=pl.BlockSpec((1, HEADS, _BCH), lambda l, n: (l, 0, n)),
        out_shape=jax.ShapeDtypeStruct((N_LAYOUTS * 3, HEADS, _NB), F32), compiler_params=_cp(), name="bias_build",
    )(tab_t, idx)


def _bias_grad(ds, idx):
    nch = _NB // _BCH

    def body(d_ref, i_ref, o_ref):
        n = pl.program_id(1)

        @pl.when(n == 0)
        def _():
            o_ref[...] = jnp.zeros(o_ref.shape, F32)

        ix = i_ref[0]
        d = d_ref[0]
        lane = lax.broadcasted_iota(jnp.int32, (HEADS, 128), 1)
        acc = jnp.zeros((HEADS, 128), F32)
        for b in range(REL_BUCKETS):
            s = jnp.sum(jnp.where(ix == b, d, 0.0), axis=1, keepdims=True)
            acc = acc + jnp.where(lane == b, s, 0.0)
        o_ref[0] += acc

    return pl.pallas_call(
        body, grid=(3, nch),
        in_specs=[pl.BlockSpec((1, HEADS, _BCH), lambda l, n: (l, 0, n)),
                  pl.BlockSpec((1, 1, _BCH), lambda l, n: (3 + l, 0, n))],
        out_specs=pl.BlockSpec((1, HEADS, 128), lambda l, n: (l, 0, 0)),
        out_shape=jax.ShapeDtypeStruct((3, HEADS, 128), F32), compiler_params=_cp(), name="bias_grad",
    )(ds, idx)


PT = 256
PSTEP = 1024
STAT_W = 128


def _perm_np(dil):
    p = np.zeros((PT, PT), np.float32)
    m = np.arange(PT // dil)
    for c in range(dil):
        p[c * (PT // dil) + m, m * dil + c] = 1.0
    return p


def _perm_const(dil, dtype, inverse):
    p = _perm_np(dil)
    return jnp.asarray(p.T if inverse else p, dtype)


def _apply_perm(p, x):
    if x.dtype == F32:
        return jnp.dot(p, x, preferred_element_type=F32, precision=lax.Precision.HIGHEST)
    return jnp.dot(p, x, preferred_element_type=F32)


def _to_residue(name, arr, col_blocks, dil):
    S = arr.shape[0]
    nc = len(col_blocks)
    p = _perm_const(dil, arr.dtype, False)
    sub = PT // dil

    def body(*refs):
        p_ref, ins, o_ref = refs[0], refs[1:1 + nc], refs[1 + nc]
        for u in range(PSTEP // PT):
            for t, r in enumerate(ins):
                y = _apply_perm(p_ref[...], r[u * PT:(u + 1) * PT, :]).astype(o_ref.dtype)
                o_ref[:, u * sub:(u + 1) * sub, t * GW:(t + 1) * GW] = y.reshape(dil, sub, GW)

    out = pl.pallas_call(
        body, grid=(S // PSTEP,),
        in_specs=[pl.BlockSpec((PT, PT), lambda i: (0, 0))]
                 + [pl.BlockSpec((PSTEP, GW), functools.partial(lambda i, cb: (i, cb), cb=cb)) for cb in col_blocks],
        out_specs=pl.BlockSpec((dil, PSTEP // dil, nc * GW), lambda i: (0, i, 0)),
        out_shape=jax.ShapeDtypeStruct((dil, S // dil, nc * GW), arr.dtype), compiler_params=_cp(), name=name,
    )(p, *([arr] * nc))
    return out.reshape(S, nc * GW)


def _to_residue_stats(name, arr, dil):
    S = arr.shape[0]
    p = _perm_const(dil, F32, False)
    sub = PT // dil

    def body(p_ref, x_ref, o_ref):
        for u in range(PSTEP // PT):
            y = _apply_perm(p_ref[...], x_ref[u * PT:(u + 1) * PT, :])
            o_ref[:, u * sub:(u + 1) * sub, :] = y.reshape(dil, sub, STAT_W)

    out = pl.pallas_call(
        body, grid=(S // PSTEP,),
        in_specs=[pl.BlockSpec((PT, PT), lambda i: (0, 0)), pl.BlockSpec((PSTEP, STAT_W), lambda i: (i, 0))],
        out_specs=pl.BlockSpec((dil, PSTEP // dil, STAT_W), lambda i: (0, i, 0)),
        out_shape=jax.ShapeDtypeStruct((dil, S // dil, STAT_W), F32), compiler_params=_cp(), name=name,
    )(p, arr)
    return out.reshape(S, STAT_W)


def _from_residue(name, arr, dil):
    S, W = arr.shape
    p = _perm_const(dil, arr.dtype, True)
    sub = PT // dil

    def body(p_ref, x_ref, o_ref):
        for u in range(PSTEP // PT):
            x = x_ref[:, u * sub:(u + 1) * sub, :].reshape(PT, W)
            o_ref[u * PT:(u + 1) * PT, :] = _apply_perm(p_ref[...], x).astype(o_ref.dtype)

    return pl.pallas_call(
        body, grid=(S // PSTEP,),
        in_specs=[pl.BlockSpec((PT, PT), lambda i: (0, 0)), pl.BlockSpec((dil, PSTEP // dil, W), lambda i: (0, i, 0))],
        out_specs=pl.BlockSpec((PSTEP, W), lambda i: (i, 0)),
        out_shape=jax.ShapeDtypeStruct((S, W), arr.dtype), compiler_params=_cp(), name=name,
    )(p, arr.reshape(dil, S // dil, W))


NT_DIMS = (((1,), (1,)), ((), ()))
TN_DIMS = (((0,), (0,)), ((), ()))


def _attn_dims(S, dil):
    L = S // dil
    TQ = min(512, L)
    return L, TQ, L // TQ, TQ // QBLK


def _attn_specs(S, dil):
    L, TQ, nq, nsub = _attn_dims(S, dil)
    nb = L // QBLK
    cur = lambda cb, w=GW: pl.BlockSpec((TQ, w), lambda c, i: (c * nq + i, cb))
    prev = lambda cb, w=GW: pl.BlockSpec((QBLK, w), lambda c, i: (c * nb + jnp.maximum(i * nsub - 1, 0), cb))
    nxt = lambda cb, w=GW: pl.BlockSpec((QBLK, w), lambda c, i: (c * nb + jnp.minimum((i + 1) * nsub, nb - 1), cb))
    band = lambda r, c_: pl.BlockSpec((HEADS, r, c_), lambda c, i: (0, 0, 0))
    return L, TQ, nq, nsub, cur, prev, nxt, band


def _fill(buf, first_ref, second_ref):
    n = first_ref.shape[0]
    buf[0:n, :] = first_ref[...]
    buf[n:n + second_ref.shape[0], :] = second_ref[...]


def _attn_fwd(name, arr, bias_kq, cb, dil):
    S = arr.shape[0]
    kcb, vcb, qcb = cb
    L, TQ, nq, nsub, cur, prev, nxt, band = _attn_specs(S, dil)

    def body(q_ref, kc_ref, kp_ref, vc_ref, vp_ref, b_ref, o_ref, l_ref, kbuf, vbuf):
        i = pl.program_id(1)
        _fill(kbuf, kp_ref, kc_ref)
        _fill(vbuf, vp_ref, vc_ref)
        row = lax.broadcasted_iota(jnp.int32, (2 * QBLK, QBLK), 0)
        first = (row >= QBLK) | (i > 0)
        for j in range(nsub):
            rs = slice(j * QBLK, (j + 1) * QBLK)
            ks = slice(j * QBLK, (j + 2) * QBLK)
            lrows = []
            for h in range(HEADS):
                hs = slice(h * HEAD_DIM, (h + 1) * HEAD_DIM)
                s = lax.dot_general(kbuf[ks, hs], q_ref[rs, hs], NT_DIMS, preferred_element_type=F32) * SCALE + b_ref[h]
                if j == 0:
                    s = jnp.where(first, s, NEG_INF)
                m = jnp.max(s, axis=0, keepdims=True)
                p = jnp.exp(s - m)
                den = jnp.sum(p, axis=0, keepdims=True)
                o_t = lax.dot_general(vbuf[ks, hs], p.astype(BF), TN_DIMS, preferred_element_type=F32) / den
                o_ref[rs, hs] = o_t.T.astype(BF)
                lrows.append(m + jnp.log(den))
            lt = jnp.concatenate(lrows + [jnp.zeros((STAT_W - HEADS, QBLK), F32)], axis=0)
            l_ref[rs, :] = lt.T

    return pl.pallas_call(
        body, grid=(dil, nq),
        in_specs=[cur(qcb), cur(kcb), prev(kcb), cur(vcb), prev(vcb), band(2 * QBLK, QBLK)],
        out_specs=[cur(0), cur(0, STAT_W)],
        out_shape=[jax.ShapeDtypeStruct((S, GW), BF), jax.ShapeDtypeStruct((S, STAT_W), F32)],
        scratch_shapes=[pltpu.VMEM((QBLK + TQ, GW), BF), pltpu.VMEM((QBLK + TQ, GW), BF)],
        compiler_params=_cp(), name=name,
    )(arr, arr, arr, arr, arr, bias_kq)


def _attn_bwd(name, arr, bias_kq2, do, stats, cb, dil):
    S = arr.shape[0]
    kcb, vcb, qcb = cb
    L, TQ, nq, nsub, cur, prev, nxt, band = _attn_specs(S, dil)

    def body(k_ref, v_ref, qc_ref, qn_ref, b_ref, doc_ref, don_ref, sc_ref, sn_ref, o_ref, db_ref, qbuf, dobuf, sbuf, carry):
        c = pl.program_id(0)
        i = pl.program_id(1)

        @pl.when((c == 0) & (i == 0))
        def _():
            db_ref[...] = jnp.zeros(db_ref.shape, F32)
            carry[...] = jnp.zeros(carry.shape, F32)

        _fill(qbuf, qc_ref, qn_ref)
        _fill(dobuf, doc_ref, don_ref)
        for j in range(nsub + 1):
            rs = slice(j * QBLK, (j + 1) * QBLK)
            sbuf[:, rs] = (sc_ref[rs, :] if j < nsub else sn_ref[...]).T
        col = lax.broadcasted_iota(jnp.int32, (QBLK, 2 * QBLK), 1)
        last = (col < QBLK) | (i < nq - 1)
        for h in range(HEADS):
            hs = slice(h * HEAD_DIM, (h + 1) * HEAD_DIM)
            bias_h = b_ref[h]
            db = jnp.zeros((QBLK, 2 * QBLK), F32)
            tail = carry[:, hs]
            for j in range(nsub):
                rs = slice(j * QBLK, (j + 1) * QBLK)
                qs = slice(j * QBLK, (j + 2) * QBLK)
                qq = qbuf[qs, hs]
                dd = dobuf[qs, hs]
                kk = k_ref[rs, hs]
                s = lax.dot_general(kk, qq, NT_DIMS, preferred_element_type=F32) * SCALE + bias_h
                if j == nsub - 1:
                    s = jnp.where(last, s, NEG_INF)
                p = jnp.exp(s - sbuf[h:h + 1, qs])
                dp = lax.dot_general(v_ref[rs, hs], dd, NT_DIMS, preferred_element_type=F32)
                ds = p * (dp - sbuf[HEADS + h:HEADS + h + 1, qs])
                db = db + ds
                dsb = ds.astype(BF)
                o_ref[rs, h * HEAD_DIM:(h + 1) * HEAD_DIM] = (jnp.dot(dsb, qq, preferred_element_type=F32) * SCALE).astype(BF)
                o_ref[rs, GW + h * HEAD_DIM:GW + (h + 1) * HEAD_DIM] = jnp.dot(p.astype(BF), dd, preferred_element_type=F32).astype(BF)
                dqw = lax.dot_general(dsb, kk, TN_DIMS, preferred_element_type=F32) * SCALE
                o_ref[rs, 2 * GW + h * HEAD_DIM:2 * GW + (h + 1) * HEAD_DIM] = (dqw[0:QBLK] + tail).astype(BF)
                tail = dqw[QBLK:2 * QBLK]
            carry[:, hs] = tail
            db_ref[h] += db

    return pl.pallas_call(
        body, grid=(dil, nq),
        in_specs=[cur(kcb), cur(vcb), cur(qcb), nxt(qcb), band(QBLK, 2 * QBLK),
                  cur(0), nxt(0), cur(0, STAT_W), nxt(0, STAT_W)],
        out_specs=[cur(0, ATTN_W), band(QBLK, 2 * QBLK)],
        out_shape=[jax.ShapeDtypeStruct((S, ATTN_W), BF), jax.ShapeDtypeStruct((HEADS, QBLK, 2 * QBLK), F32)],
        scratch_shapes=[pltpu.VMEM((TQ + QBLK, GW), BF), pltpu.VMEM((TQ + QBLK, GW), BF), pltpu.VMEM((STAT_W, TQ + QBLK), F32),
                        pltpu.VMEM((QBLK, GW), F32)],
        compiler_params=_cp(("arbitrary", "arbitrary")), name=name,
    )(arr, arr, arr, arr, bias_kq2, do, do, stats, stats)


def _head_expand():
    e = np.zeros((STAT_W, GW), np.float32)
    for h in range(HEADS):
        e[h, h * HEAD_DIM:(h + 1) * HEAD_DIM] = 1.0
    return e


def _attn_merge(os_, ls_, S):
    dils = [d for _, d in GROUPS]
    pb = [_perm_const(d, BF, True) for d in dils[1:]]
    pf = [_perm_const(d, F32, True) for d in dils[1:]]
    expand = jnp.asarray(_head_expand(), BF)

    def body(o0, o1, o2, l0, l1, l2, pb1, pb2, pf1, pf2, e_ref, o_ref, l_ref):
        for u in range(PSTEP // PT):
            rs = slice(u * PT, (u + 1) * PT)
            res = lambda r, d: r[:, u * (PT // d):(u + 1) * (PT // d), :].reshape(PT, r.shape[2])
            ov = [o0[rs, :].astype(F32), _apply_perm(pb1[...], res(o1, dils[1])), _apply_perm(pb2[...], res(o2, dils[2]))]
            lv = [l0[rs, :], _apply_perm(pf1[...], res(l1, dils[1])), _apply_perm(pf2[...], res(l2, dils[2]))]
            m = jnp.maximum(jnp.maximum(lv[0], lv[1]), lv[2])
            ev = [jnp.exp(l - m) for l in lv]
            den = ev[0] + ev[1] + ev[2]
            acc = jnp.zeros((PT, GW), F32)
            for g in range(3):
                wide = jnp.dot((ev[g] / den).astype(BF), e_ref[...], preferred_element_type=F32)
                acc = acc + wide * ov[g]
            o_ref[rs, :] = acc.astype(BF)
            l_ref[rs, :] = m + jnp.log(den)

    nat = lambda w: pl.BlockSpec((PSTEP, w), lambda i: (i, 0))
    res = lambda d, w: pl.BlockSpec((d, PSTEP // d, w), lambda i: (0, i, 0))
    cst = lambda a: pl.BlockSpec(a.shape, lambda i: (0, 0))
    args = [os_[0], os_[1].reshape(dils[1], S // dils[1], GW), os_[2].reshape(dils[2], S // dils[2], GW),
            ls_[0], ls_[1].reshape(dils[1], S // dils[1], STAT_W), ls_[2].reshape(dils[2], S // dils[2], STAT_W),
            pb[0], pb[1], pf[0], pf[1], expand]
    return pl.pallas_call(
        body, grid=(S // PSTEP,),
        in_specs=[nat(GW), res(dils[1], GW), res(dils[2], GW), nat(STAT_W), res(dils[1], STAT_W), res(dils[2], STAT_W)]
                 + [cst(a) for a in args[6:]],
        out_specs=[nat(GW), nat(STAT_W)],
        out_shape=[jax.ShapeDtypeStruct((S, GW), BF), jax.ShapeDtypeStruct((S, STAT_W), F32)],
        compiler_params=_cp(), name="attn_merge",
    )(*args)


CT = 256
CBUF = HALO + CT + 8
RG = 4


def _ln_hat(u1):
    mu = jnp.mean(u1, axis=-1, keepdims=True)
    xc = u1 - mu
    rstd = lax.rsqrt(jnp.mean(xc * xc, axis=-1, keepdims=True) + LN_EPS)
    return xc * rstd, rstd


def _glu_window(hu_ref, hg_ref, huh_ref, hgh_ref, bglu_ref, buf_ref, i):
    bu = bglu_ref[:, 0:D]
    bg = bglu_ref[:, D:2 * D]
    uh = (huh_ref[...].astype(F32) + bu) * _sig(hgh_ref[...].astype(F32) + bg)
    buf_ref[0:HALO, :] = jnp.where(i > 0, uh, 0.0)
    a = hu_ref[...].astype(F32) + bu
    s = _sig(hg_ref[...].astype(F32) + bg)
    buf_ref[HALO:HALO + CT, :] = a * s
    buf_ref[HALO + CT:CBUF, :] = jnp.zeros((8, D), F32)
    return a, s


def _shift_copies(buf_ref, sh_ref):
    for r in range(8):
        sh_ref[r] = buf_ref[r:r + HALO + CT, :]


def _tap_rows(wb_ref, w_ref):
    for j in range(CONV_W):
        wb_ref[j * 8:(j + 1) * 8, :] = jnp.broadcast_to(w_ref[j:j + 1, :], (8, D))


def _conv_taps(sh_ref, wb_ref, out_ref, init, offset):
    for rg in range(CT // (8 * RG)):
        accs = [init] * RG
        for j in range(CONV_W):
            off = offset(j)
            wj = wb_ref[j * 8:(j + 1) * 8, :]
            for q in range(RG):
                row = 8 * (rg * RG + q + off // 8)
                accs[q] = accs[q] + wj * sh_ref[off % 8, row:row + 8, :]
        for q in range(RG):
            out_ref[(rg * RG + q) * 8:(rg * RG + q + 1) * 8, :] = accs[q]


def _conv_specs(S):
    cur = lambda cb: pl.BlockSpec((CT, D), lambda i: (i, cb))
    halo = lambda cb: pl.BlockSpec((HALO, D), lambda i: (jnp.maximum(i * (CT // HALO) - 1, 0), cb))
    full = lambda shp: pl.BlockSpec(shp, lambda i: (0, 0))
    return cur, halo, full


def _conv_fwd(z, b_glu, w_dw, b_dw, g_ln, b_ln):
    S = z.shape[0]
    cur, halo, full = _conv_specs(S)

    def body(hu, hg, huh, hgh, bglu, w, bdw, gln, bln, u1_ref, u3_ref, buf, sh, wb):
        i = pl.program_id(0)

        @pl.when(i == 0)
        def _():
            _tap_rows(wb, w)

        _glu_window(hu, hg, huh, hgh, bglu, buf, i)
        _shift_copies(buf, sh)
        _conv_taps(sh, wb, u1_ref, jnp.broadcast_to(bdw[...], (8, D)), lambda j: 2 + j)
        xh, _ = _ln_hat(u1_ref[...])
        u2 = xh * gln[...] + bln[...]
        u3_ref[...] = (u2 * _sig(u2)).astype(BF)

    return pl.pallas_call(
        body, grid=(S // CT,),
        in_specs=[cur(0), cur(1), halo(0), halo(1), full((1, 2 * D)), full((HALO, D)), full((1, D)), full((1, D)), full((1, D))],
        out_specs=[pl.BlockSpec((CT, D), lambda i: (i, 0))] * 2,
        out_shape=[jax.ShapeDtypeStruct((S, D), F32), jax.ShapeDtypeStruct((S, D), BF)],
        scratch_shapes=[pltpu.VMEM((CBUF, D), F32), pltpu.VMEM((8, HALO + CT, D), F32), pltpu.VMEM((CONV_W * 8, D), F32)],
        compiler_params=_cp(("arbitrary",)), name="conv_fwd",
    )(z, z, z, z, b_glu, w_dw, b_dw, g_ln, b_ln)


def _conv_bwd(du1, z, b_glu, w_dw):
    S = z.shape[0]
    n = S // CT
    cur, halo, full = _conv_specs(S)

    def body(du, dun, hu, hg, huh, hgh, bglu, w, dz_ref, dw_ref, dbg_ref, bufu, bufd, shu, shd, wb, du0_ref, dwacc):
        i = pl.program_id(0)

        @pl.when(i == 0)
        def _():
            _tap_rows(wb, w)
            dwacc[...] = jnp.zeros(dwacc.shape, F32)
            dbg_ref[...] = jnp.zeros(dbg_ref.shape, F32)

        a, s = _glu_window(hu, hg, huh, hgh, bglu, bufu, i)
        bufd[0:CT, :] = du[...]
        bufd[CT:CT + HALO, :] = jnp.where(i < n - 1, dun[...], 0.0)
        bufd[CT + HALO:CBUF, :] = jnp.zeros((8, D), F32)
        _shift_copies(bufu, shu)
        _shift_copies(bufd, shd)
        _conv_taps(shd, wb, du0_ref, jnp.zeros((8, D), F32), lambda j: 30 - j)
        for rg in range(CT // (8 * RG)):
            dch = [bufd[(rg * RG + q) * 8:(rg * RG + q + 1) * 8, :] for q in range(RG)]
            for j in range(CONV_W):
                off = 2 + j
                acc = dwacc[j * 8:(j + 1) * 8, :]
                for q in range(RG):
                    row = 8 * (rg * RG + q + off // 8)
                    acc = acc + dch[q] * shu[off % 8, row:row + 8, :]
                dwacc[j * 8:(j + 1) * 8, :] = acc
        du0 = du0_ref[...]
        dhu = du0 * s
        dhg = du0 * a * s * (1.0 - s)
        dz_ref[:, 0:D] = dhu.astype(BF)
        dz_ref[:, D:2 * D] = dhg.astype(BF)
        dbg_ref[:, 0:D] += _psum8(dhu)
        dbg_ref[:, D:2 * D] += _psum8(dhg)

        @pl.when(i == n - 1)
        def _():
            dbg_ref[0:1, :] = jnp.sum(dbg_ref[...], axis=0, keepdims=True)
            for j in range(CONV_W):
                dw_ref[j:j + 1, :] = jnp.sum(dwacc[j * 8:(j + 1) * 8, :], axis=0, keepdims=True)
            dw_ref[CONV_W:HALO, :] = jnp.zeros((HALO - CONV_W, D), F32)

    nxt = pl.BlockSpec((HALO, D), lambda i: (jnp.minimum((i + 1) * (CT // HALO), S // HALO - 1), 0))
    return pl.pallas_call(
        body, grid=(n,),
        in_specs=[pl.BlockSpec((CT, D), lambda i: (i, 0)), nxt, cur(0), cur(1), halo(0), halo(1), full((1, 2 * D)), full((HALO, D))],
        out_specs=[pl.BlockSpec((CT, 2 * D), lambda i: (i, 0)), full((HALO, D)), full((8, 2 * D))],
        out_shape=[jax.ShapeDtypeStruct((S, 2 * D), BF), jax.ShapeDtypeStruct((HALO, D), F32), jax.ShapeDtypeStruct((8, 2 * D), F32)],
        scratch_shapes=[pltpu.VMEM((CBUF, D), F32), pltpu.VMEM((CBUF, D), F32), pltpu.VMEM((8, HALO + CT, D), F32),
                        pltpu.VMEM((8, HALO + CT, D), F32), pltpu.VMEM((CONV_W * 8, D), F32), pltpu.VMEM((CT, D), F32),
                        pltpu.VMEM((CONV_W * 8, D), F32)],
        compiler_params=_cp(("arbitrary",)), name="conv_bwd",
    )(du1, du1, z, z, z, z, b_glu, w_dw)


MESH = pl.DeviceIdType.MESH


def _all_gather(name, shards):
    n = len(shards)

    def body(*refs):
        ins, outs = refs[:n], refs[n:2 * n]
        send_sems, recv_sems, local_sems = refs[2 * n:]
        x, y, c = lax.axis_index("x"), lax.axis_index("y"), lax.axis_index("c")
        me, sibling = (x, y, c), (x, y, 1 - c)
        chips = [(1 - x, y), (x, 1 - y), (1 - x, 1 - y)]

        def slot(a, px, py, pc):
            return outs[a].at[4 * px + 2 * py + pc]

        def copy(a, k, block, to, src=None):
            return pltpu.make_async_remote_copy(
                src_ref=slot(a, *block) if src is None else src, dst_ref=slot(a, *block),
                send_sem=send_sems.at[a, k], recv_sem=recv_sems.at[a, k], device_id=to, device_id_type=MESH)

        mine = [pltpu.make_async_copy(ins[a], slot(a, *me), local_sems.at[a]) for a in range(n)]
        for cp in mine:
            cp.start()
        first = []
        for a in range(n):
            first.append(copy(a, 0, me, sibling, src=ins[a]))
            first += [copy(a, 1 + j, me, (*chip, c), src=ins[a]) for j, chip in enumerate(chips)]
        for cp in first:
            cp.start()
        passed = []
        for j, chip in enumerate(chips):
            for a in range(n):
                copy(a, 1 + j, (*chip, c), me).wait_recv()
                fwd = copy(a, 4 + j, (*chip, c), sibling)
                fwd.start()
                passed.append(fwd)
        for a in range(n):
            copy(a, 0, sibling, me).wait_recv()
        for j, chip in enumerate(chips):
            for a in range(n):
                copy(a, 4 + j, (*chip, 1 - c), me).wait_recv()
        for cp in first + passed:
            cp.wait_send()
        for cp in mine:
            cp.wait()

    anyspec = pl.BlockSpec(memory_space=pl.ANY)
    return pl.pallas_call(
        body, in_specs=[anyspec] * n, out_specs=[anyspec] * n,
        out_shape=[jax.ShapeDtypeStruct((NDEV,) + s.shape, s.dtype) for s in shards],
        scratch_shapes=[pltpu.SemaphoreType.DMA((n, 7)), pltpu.SemaphoreType.DMA((n, 7)), pltpu.SemaphoreType.DMA((n,))],
        name=name,
    )(*shards)


HBM_SPEC = pl.BlockSpec(memory_space=pltpu.HBM)
SEM_SPEC = pl.BlockSpec(memory_space=pltpu.SEMAPHORE)
DATAFLOW = pltpu.SideEffectType.DATAFLOW_SIDE_EFFECTING


def _peers():
    x, y, c = lax.axis_index("x"), lax.axis_index("y"), lax.axis_index("c")
    out = []
    for k in range(1, NDEV):
        px = 1 - x if k & 4 else x
        py = 1 - y if k & 2 else y
        pc = 1 - c if k & 1 else c
        out.append(((px, py, pc), 4 * px + 2 * py + pc))
    return 4 * x + 2 * y + c, out


def _exchange_copies(srcs, lands, send_sems, recv_sems, gather):
    my, peers = _peers()
    pairs = []
    for k, (dev, pid) in enumerate(peers):
        for a in range(len(srcs)):
            src = srcs[a] if gather else srcs[a].at[pid]
            sems = dict(send_sem=send_sems[a * (NDEV - 1) + k], recv_sem=recv_sems[a * (NDEV - 1) + k], device_id=dev,
                        device_id_type=MESH)
            pairs.append((pltpu.make_async_remote_copy(src_ref=src, dst_ref=lands[a].at[my], **sems),
                          pltpu.make_async_remote_copy(src_ref=src, dst_ref=lands[a].at[pid], **sems)))
    return pairs


def _exchange_start(name, srcs, gather):
    n = len(srcs)
    ns = n * (NDEV - 1)
    shapes = [(s.shape if gather else s.shape[1:]) for s in srcs]
    lands = [lax.empty((NDEV,) + shp, s.dtype) for shp, s in zip(shapes, srcs)]

    def body(*refs):
        src_refs, land_refs = refs[:n], refs[n:2 * n]
        send_sems, recv_sems = refs[2 * n:2 * n + ns], refs[2 * n + ns:2 * n + 2 * ns]
        token = refs[-1]
        for mine, _ in _exchange_copies(src_refs, land_refs, send_sems, recv_sems, gather):
            mine.start()
        token[...] = jnp.zeros(token.shape, token.dtype)

    hbm = lambda a: pltpu.HBM(a.shape, a.dtype)
    res = pl.pallas_call(
        body, name=name,
        out_shape=(*([pltpu.SemaphoreType.DMA(())] * (2 * ns)), *[hbm(s) for s in srcs], *[hbm(l) for l in lands],
                   jax.ShapeDtypeStruct((8, 128), F32)),
        in_specs=[HBM_SPEC] * (2 * n),
        out_specs=(*([SEM_SPEC] * (2 * ns)), *([HBM_SPEC] * (2 * n)), pl.BlockSpec(memory_space=pltpu.VMEM)),
        input_output_aliases={i: 2 * ns + i for i in range(2 * n)},
        compiler_params=pltpu.CompilerParams(has_side_effects=DATAFLOW),
    )(*[pltpu.with_memory_space_constraint(s, pltpu.HBM) for s in srcs],
      *[pltpu.with_memory_space_constraint(l, pltpu.HBM) for l in lands])
    return list(res[:ns]), list(res[ns:2 * ns]), list(res[2 * ns:2 * ns + n]), list(res[2 * ns + n:2 * ns + 2 * n]), res[-1]


def _exchange_wait(name, handle, after, gather):
    send_sems, recv_sems, srcs, lands, _ = handle
    n = len(srcs)
    ns = n * (NDEV - 1)

    def body(*refs):
        src_refs, land_refs = refs[:n], refs[n:2 * n]
        s_sems, r_sems = refs[2 * n:2 * n + ns], refs[2 * n + ns:2 * n + 2 * ns]
        for mine, theirs in _exchange_copies(src_refs, land_refs, s_sems, r_sems, gather):
            mine.wait_send()
            theirs.wait_recv()

    hbm = lambda a: pltpu.HBM(a.shape, a.dtype)
    res = pl.pallas_call(
        body, name=name,
        out_shape=(*[hbm(s) for s in srcs], *[hbm(l) for l in lands]),
        in_specs=[HBM_SPEC] * (2 * n) + [SEM_SPEC] * (2 * ns) + [pl.BlockSpec(memory_space=pl.ANY)],
        out_specs=tuple([HBM_SPEC] * (2 * n)),
        input_output_aliases={i: i for i in range(2 * n)},
        compiler_params=pltpu.CompilerParams(has_side_effects=DATAFLOW),
    )(*srcs, *lands, *send_sems, *recv_sems, after)
    return list(res[n:])


def _set_own_slot(land, own):
    my = 4 * lax.axis_index("x") + 2 * lax.axis_index("y") + lax.axis_index("c")
    return lax.dynamic_update_slice(land, own[None], (my, 0, 0))


def _own_block(blocks):
    my = 4 * lax.axis_index("x") + 2 * lax.axis_index("y") + lax.axis_index("c")
    return lax.dynamic_index_in_dim(blocks, my, axis=0, keepdims=False)


_C1 = 1.0 - ADAM_B1 ** ADAM_STEP
_C2 = 1.0 - ADAM_B2 ** ADAM_STEP


def _adamw(name, w, m, v, recv, tr):
    R, C = w.shape

    def body(w_ref, m_ref, v_ref, r_ref, g_ref, d_ref, nm_ref, nv_ref):
        g = r_ref[0].astype(F32)
        for s in range(1, NDEV):
            g = g + r_ref[s].astype(F32)
        wv = w_ref[...]
        nm = ADAM_B1 * m_ref[...] + (1.0 - ADAM_B1) * g
        nv = ADAM_B2 * v_ref[...] + (1.0 - ADAM_B2) * (g * g)
        m_hat = nm / _C1
        v_hat = nv / _C2
        g_ref[...] = g
        d_ref[...] = -ADAM_LR * (m_hat / (jnp.sqrt(v_hat) + ADAM_EPS) + ADAM_WD * wv)
        nm_ref[...] = nm
        nv_ref[...] = nv

    blk = pl.BlockSpec((tr, C), lambda i: (i, 0))
    return pl.pallas_call(
        body, grid=(R // tr,), in_specs=[blk, blk, blk, pl.BlockSpec((NDEV, tr, C), lambda i: (0, i, 0))],
        out_specs=[blk] * 4, out_shape=[jax.ShapeDtypeStruct((R, C), F32)] * 4,
        compiler_params=_cp(), name=name,
    )(w, m, v, recv)


def _row(v):
    return v.reshape(1, -1)


def _local_step(xs, tgt, Wp, rest_fn, early_fn, late_fn, rel_bias_table, g_pre_mix, b_glu, b_dw, g_conv_ln,
                b_conv_ln, b_conv_out, g_post_mix, g_pre_ffn, g_post_ffn):
    S = xs.shape[0]
    g1, g2, g3, g4 = _row(g_pre_mix), _row(g_post_mix), _row(g_pre_ffn), _row(g_post_ffn)
    bglu, bdw, gln, bln, bco = _row(b_glu), _row(b_dw), _row(g_conv_ln), _row(b_conv_ln), _row(b_conv_out)
    full = (D, F32, D, 0, False)
    fullb = (D, BF, D, 0, False)

    def epi_rms(accs, r, c, o, p):
        v = r[0][...]
        o[0][...] = (v * _rms_r(v) * c[0][...]).astype(BF)

    (h1,) = _fused_mm("rms_in", S, 512, 1, [], [], [], [], [(xs, D, 0, False)], [g1], [fullb], [], epi_rms)

    def epi_cast(accs, r, c, o, p):
        o[0][...] = accs[0].astype(BF)

    ZT = IN_W // 4
    (z,) = _fused_mm("in_proj", S, 1024, 4, [(h1, D, 0, 1)], [(Wp, False, D, ZT, 0, 1, 0, 0, True)], [(0, 0, 0, 0, 1)],
                     [(1024, ZT)], [], [], [(IN_W, BF, ZT, 0, True)], [], epi_cast, n_outer=True)

    idx = jnp.asarray(_band_index())
    tab_t = rel_bias_table.T.reshape(3, HEADS, REL_BUCKETS)
    bias_all = _bias_build(tab_t, idx)
    bias_kq = [bias_all[g].reshape(HEADS, 2 * QBLK, QBLK) for g in range(3)]
    bias_kq2 = [bias_all[3 + g].reshape(HEADS, QBLK, 2 * QBLK) for g in range(3)]
    dils = [d for _, d in GROUPS]
    qkv = [(z, _kvq_blocks(0))] + [(_to_residue(f"qkv_to_residue_g{g}", z, _kvq_blocks(g), dils[g]), (0, 1, 2)) for g in (1, 2)]
    os_, ls_ = [], []
    for g in range(3):
        o_g, l_g = _attn_fwd(f"attn_fwd_g{g}", qkv[g][0], bias_kq[g], qkv[g][1], dils[g])
        os_.append(o_g)
        ls_.append(l_g)
    o_att, lse = _attn_merge(os_, ls_, S)

    Wfi, Wfo, Wco, Wmo, Wao, wdw = rest_fn(lse)
    u1, u3 = _conv_fwd(z, bglu, wdw, bdw, gln, bln)

    def epi_mix(accs, r, c, o, p):
        ya = accs[0]
        yc = accs[1] + c[0][...]
        mg = _sig(r[0][...].astype(F32)) * ya + _sig(r[1][...].astype(F32)) * yc
        mgb = mg.astype(BF)
        m2 = jnp.dot(mgb, c[1][...], preferred_element_type=F32)
        x1 = r[2][...] + m2 * _rms_r(m2) * c[2][...]
        o[0][...] = ya.astype(BF)
        o[1][...] = yc.astype(BF)
        o[2][...] = mgb
        o[3][...] = m2.astype(BF)
        o[4][...] = x1
        o[5][...] = (x1 * _rms_r(x1) * c[3][...]).astype(BF)

    y_attn, y_conv, merged, m2, x1, h2 = _fused_mm(
        "mix_fwd", S, 512, 1, [(o_att, GW, 0, 1), (u3, D, 0, 1)],
        [(Wao, False, GW, D, 0, 1, 0, 0, False), (Wco, False, D, D, 0, 1, 0, 0, False)], [(0, 0, 0, 0, 1), (1, 1, 1, 0, 1)],
        [(512, D), (512, D)], [(z, D, 2, False), (z, D, 3, False), (xs, D, 0, False)], [bco, Wmo, g2, g3],
        [fullb, fullb, fullb, fullb, full, fullb], [], epi_mix)

    HN = FFN // 2

    def epi_ffn_in(accs, r, c, o, p):
        gt, up = accs
        o[0][...] = gt.astype(BF)
        o[1][...] = up.astype(BF)
        o[2][...] = (gt * _sig(gt) * up).astype(BF)

    gate, up, act = _fused_mm(
        "ffn_in", S, 512, 2, [(h2, D, 0, 1)],
        [(Wfi, False, D, HN, 0, 1, 0, 0, True), (Wfi, False, D, HN, 0, 1, 0, 2, True)], [(0, 0, 0, 0, 1), (0, 1, 1, 0, 1)],
        [(512, HN), (512, HN)], [], [], [(FFN, BF, HN, 0, True)] * 3, [], epi_ffn_in, n_outer=True)

    def epi_loss(accs, r, c, o, p):
        f2 = accs[0]
        g = c[0][...]
        rr = _rms_r(f2)
        err = r[0][...] + f2 * rr * g - r[1][...]
        dy = err * (1.0 / D)
        df2, dgr = _rms_bwd(f2, rr, g, dy)
        o[0][...] = dy
        o[1][...] = df2.astype(BF)
        p[0][...] += _psum8(err * err)
        p[1][...] += _psum8(dgr)

    dy, df2, loss_p, dg4 = _fused_mm(
        "ffn_out_loss", S, 512, 1, [(act, FFN, 0, 1)], [(Wfo, False, FFN, D, 0, 1, 0, 0, False)], [(0, 0, 0, 0, 1)],
        [(512, D)], [(x1, D, 0, False), (tgt, D, 0, False)], [g4], [full, fullb], [(8, D), (8, D)], epi_loss)

    def epi_swiglu(accs, r, c, o, p):
        da = accs[0]
        gt = r[0][...].astype(F32)
        sg = _sig(gt)
        o[0][...] = (da * r[1][...].astype(F32) * sg * (1.0 + gt * (1.0 - sg))).astype(BF)
        o[1][...] = (da * gt * sg).astype(BF)

    dgate, dup = _fused_mm(
        "ffn_out_bwd", S, 512, 2, [(df2, D, 0, 1)], [(Wfo, True, D, HN, 0, 1, 0, 0, True)], [(0, 0, 0, 0, 1)],
        [(512, HN)], [(gate, HN, 0, True), (up, HN, 0, True)], [], [(FFN, BF, HN, 0, True)] * 2, [], epi_swiglu, n_outer=True)
    dWfo = _mm_tn("dw_ffn_out", act, df2, HN, D, WG_TK)

    def epi_dh2(accs, r, c, o, p):
        dh2 = accs[0]
        x1v = r[1][...]
        r3 = _rms_r(x1v)
        d1, dg3r = _rms_bwd(x1v, r3, c[0][...], dh2)
        dx1 = r[0][...] + d1
        m2v = r[2][...].astype(F32)
        r2 = _rms_r(m2v)
        dm2, dg2r = _rms_bwd(m2v, r2, c[1][...], dx1)
        o[0][...] = dx1
        o[1][...] = dm2.astype(BF)
        p[0][...] += _psum8(dg3r)
        p[1][...] += _psum8(dg2r)

    dx1, dm2, dg3, dg2 = _resident_mm(
        "ffn_in_bwd", S, 512, [dgate, dup], Wfi, [(dy, D), (x1, D), (m2, D)], [g3, g2], [(D, F32), (D, BF)], [(8, D), (8, D)], epi_dh2)
    dWfi = jnp.concatenate([_mm_tn("dw_ffn_gate", h2, dgate, D, HN, WG_TK), _mm_tn("dw_ffn_up", h2, dup, D, HN, WG_TK)], axis=1)

    def epi_dmix(accs, r, c, o, p):
        dm = accs[0]
        sa = _sig(r[0][...].astype(F32))
        sc = _sig(r[1][...].astype(F32))
        o[0][...] = (dm * sa).astype(BF)
        o[1][...] = (dm * sc).astype(BF)
        o[2][:, 0:D] = (dm * r[2][...].astype(F32) * sa * (1.0 - sa)).astype(BF)
        o[2][:, D:2 * D] = (dm * r[3][...].astype(F32) * sc * (1.0 - sc)).astype(BF)

    dy_attn, dy_conv, dz_gate = _fused_mm(
        "mix_bwd", S, 512, 1, [(dm2, D, 0, 1)], [(Wmo, True, D, D, 0, 1, 0, 0, False)], [(0, 0, 0, 0, 1)], [(512, D)],
        [(z, D, 2, False), (z, D, 3, False), (y_attn, D, 0, False), (y_conv, D, 0, False)], [],
        [fullb, fullb, (2 * D, BF, 2 * D, 0, False)], [], epi_dmix)
    dWmo = _mm_tn("dw_mix_out", merged, dm2, D, D, WG_TK)

    def epi_dconv(accs, r, c, o, p):
        du3 = accs[0]
        xh, rstd = _ln_hat(r[0][...])
        gl = c[0][...]
        u2 = xh * gl + c[1][...]
        sg = _sig(u2)
        du2 = du3 * sg * (1.0 + u2 * (1.0 - sg))
        dxh = du2 * gl
        du1 = rstd * (dxh - jnp.mean(dxh, axis=-1, keepdims=True) - xh * jnp.mean(dxh * xh, axis=-1, keepdims=True))
        o[0][...] = du1
        p[0][...] += _psum8(du2 * xh)
        p[1][...] += _psum8(du2)
        p[2][...] += _psum8(du1)
        p[3][...] += _psum8(r[1][...].astype(F32))

    du1, dgln, dbln, dbdw, dbco = _fused_mm(
        "conv_out_bwd", S, 512, 1, [(dy_conv, D, 0, 1)], [(Wco, True, D, D, 0, 1, 0, 0, False)], [(0, 0, 0, 0, 1)], [(512, D)],
        [(u1, D, 0, False), (dy_conv, D, 0, False)], [gln, bln], [full], [(8, D)] * 4, epi_dconv)
    dWco = _mm_tn("dw_conv_out", u3, dy_conv, D, D, WG_TK)
    dz_glu, dwdw, dbglu = _conv_bwd(du1, z, bglu, wdw)

    head_sum = np.zeros((GW, STAT_W), np.float32)
    for h in range(HEADS):
        head_sum[h * HEAD_DIM:(h + 1) * HEAD_DIM, HEADS + h] = 1.0
    head_sum = jnp.asarray(head_sum)

    def epi_do(accs, r, c, o, p):
        do = accs[0]
        o[0][...] = do.astype(BF)
        delta = jnp.dot(do * r[0][...].astype(F32), c[0][...], preferred_element_type=F32, precision=lax.Precision.HIGHEST)
        lane = lax.broadcasted_iota(jnp.int32, delta.shape, 1)
        o[1][...] = jnp.where(lane < HEADS, r[1][...], delta)

    do, stats = _fused_mm(
        "attn_out_bwd", S, 1024, 1, [(dy_attn, D, 0, 1)], [(Wao, True, D, GW, 0, 1, 0, 0, False)], [(0, 0, 0, 0, 1)], [(1024, GW)],
        [(o_att, GW, 0, False), (lse, STAT_W, 0, False)], [head_sum], [(GW, BF, GW, 0, False), (STAT_W, F32, STAT_W, 0, False)], [], epi_do)
    dWao = _mm_tn("dw_attn_out", o_att, dy_attn, GW, D, WG_TK)

    tie = early_fn(dict(w_ffn_in=dWfi, w_ffn_out=dWfo, w_conv_out=dWco, w_mix_out=dWmo, w_attn_out=dWao, w_dw=dwdw))
    stats = stats + tie
    dos = [do] + [_to_residue(f"do_to_residue_g{g}", do, (0,), dils[g]) for g in (1, 2)]
    sts = [stats] + [_to_residue_stats(f"stats_to_residue_g{g}", stats, dils[g]) for g in (1, 2)]
    dqkv, dbs = [], []
    for g in range(3):
        arr, cb = qkv[g]
        dg, db = _attn_bwd(f"attn_bwd_g{g}", arr, bias_kq2[g], dos[g], sts[g], cb, dils[g])
        dqkv.append(dg if g == 0 else _from_residue(f"dqkv_from_residue_g{g}", dg, dils[g]))
        dbs.append(db.reshape(HEADS, _NB))
    dtab = _bias_grad(jnp.stack(dbs), idx)[:, :, :REL_BUCKETS].reshape(3 * HEADS, REL_BUCKETS).T

    def epi_dx(accs, r, c, o, p):
        xv = r[1][...]
        d1, dg1r = _rms_bwd(xv, _rms_r(xv), c[0][...], accs[0])
        o[0][...] = r[0][...] + d1
        p[0][...] += _psum8(dg1r)

    dWg = [_mm_tn(f"dw_in_g{g}", h1, dqkv[g], D, ATTN_W, WG_TK) for g in range(3)]
    dW_in = jnp.concatenate(
        [t[:, 2 * GW:] for t in dWg] + [t[:, :GW] for t in dWg] + [t[:, GW:2 * GW] for t in dWg]
        + [_mm_tn("dw_in_glu", h1, dz_glu, D, D, WG_TK), _mm_tn("dw_in_gate", h1, dz_gate, D, D, WG_TK)], axis=1)
    g1_late = g1 + late_fn(dW_in)
    grad_x, dg1 = _resident_mm(
        "in_proj_bwd", S, 512, [dz_glu, dz_gate, dqkv[0], dqkv[1], dqkv[2]], Wp, [(dx1, D), (xs, D)], [g1_late], [(D, F32)], [(8, D)], epi_dx)

    small = dict(rel_bias_table=dtab, g_pre_mix=dg1[0], b_glu=dbglu[0], b_dw=dbdw[0], g_conv_ln=dgln[0], b_conv_ln=dbln[0],
                 b_conv_out=dbco[0], g_post_mix=dg2[0], g_pre_ffn=dg3[0], g_post_ffn=dg4[0])
    return loss_p[0], grad_x, small


SMALL = ['rel_bias_table', 'g_pre_mix', 'b_glu', 'b_dw', 'g_conv_ln', 'b_conv_ln', 'b_conv_out', 'g_post_mix', 'g_pre_ffn',
         'g_post_ffn']
BIG = ['w_in', 'w_ffn_in', 'w_ffn_out', 'w_conv_out', 'w_mix_out', 'w_attn_out', 'w_dw']
WEIGHTS = ['rel_bias_table', 'g_pre_mix', 'w_in', 'b_glu', 'w_dw', 'b_dw', 'g_conv_ln', 'b_conv_ln', 'w_conv_out', 'b_conv_out',
           'w_attn_out', 'w_mix_out', 'g_post_mix', 'g_pre_ffn', 'w_ffn_in', 'w_ffn_out', 'g_post_ffn']
SMALL_ROWS = 16


ROW_SMALL = ['g_pre_mix', 'b_glu', 'b_dw', 'g_conv_ln', 'b_conv_ln', 'b_conv_out', 'g_post_mix', 'g_pre_ffn', 'g_post_ffn']
LOSS_ROW = 10
TAB_LANES = 128


def _small_rows(small, loss_row):
    rows = [small[n].reshape(-1, D) for n in ROW_SMALL] + [loss_row.reshape(1, D)]
    n = sum(r.shape[0] for r in rows)
    return jnp.concatenate(rows + [jnp.zeros((SMALL_ROWS - n, D), F32)], axis=0)


def _adamw_small(recv_rows, recv_tab, ws, ms, vs):
    np_ = len(SMALL)

    def body(*refs):
        rr, rt = refs[0], refs[1]
        w_refs, m_refs, v_refs = refs[2:2 + np_], refs[2 + np_:2 + 2 * np_], refs[2 + 2 * np_:2 + 3 * np_]
        loss_ref = refs[2 + 3 * np_]
        outs = refs[3 + 3 * np_:]
        rows = rr[0]
        tab = rt[0]
        for s_ in range(1, NDEV):
            rows = rows + rr[s_]
            tab = tab + rt[s_]
        loss_ref[...] = jnp.sum(rows[LOSS_ROW:LOSS_ROW + 1, :], axis=1, keepdims=True) * (0.5 / D)
        row = 0
        for p, n in enumerate(SMALL):
            if n == 'rel_bias_table':
                g = tab[:, 0:3 * HEADS]
            else:
                k = w_refs[p].shape[1] // D
                g = rows[row:row + 1, :] if k == 1 else jnp.concatenate([rows[row + t:row + t + 1, :] for t in range(k)], axis=1)
                row += k
            nm = ADAM_B1 * m_refs[p][...] + (1.0 - ADAM_B1) * g
            nv = ADAM_B2 * v_refs[p][...] + (1.0 - ADAM_B2) * (g * g)
            outs[4 * p][...] = g
            outs[4 * p + 1][...] = -ADAM_LR * ((nm / _C1) / (jnp.sqrt(nv / _C2) + ADAM_EPS) + ADAM_WD * w_refs[p][...])
            outs[4 * p + 2][...] = nm
            outs[4 * p + 3][...] = nv

    out_shape = [jax.ShapeDtypeStruct((1, 1), F32)]
    for a_ in ws:
        out_shape += [jax.ShapeDtypeStruct(a_.shape, F32)] * 4
    res = pl.pallas_call(body, out_shape=out_shape, compiler_params=_cp(), name="adamw_small")(recv_rows, recv_tab, *ws, *ms, *vs)
    return res[0], [tuple(res[1 + 4 * p:5 + 4 * p]) for p in range(np_)]


def _cols_to_blocks(a):
    R = a.shape[0]
    return a.reshape(R, NDEV, a.shape[1] // NDEV).transpose(1, 0, 2)


def _blocks_to_cols(a):
    return a.transpose(1, 0, 2).reshape(a.shape[1], NDEV * a.shape[2])


def kernel(x, rel_bias_table, g_pre_mix, w_in, b_glu, w_dw, b_dw, g_conv_ln, b_conv_ln, w_conv_out, b_conv_out, w_attn_out, w_mix_out, g_post_mix, g_pre_ffn, w_ffn_in, w_ffn_out, g_post_ffn, loss_target, m_rel_bias_table, m_g_pre_mix, m_w_in, m_b_glu, m_w_dw, m_b_dw, m_g_conv_ln, m_b_conv_ln, m_w_conv_out, m_b_conv_out, m_w_attn_out, m_w_mix_out, m_g_post_mix, m_g_pre_ffn, m_w_ffn_in, m_w_ffn_out, m_g_post_ffn, v_rel_bias_table, v_g_pre_mix, v_w_in, v_b_glu, v_w_dw, v_b_dw, v_g_conv_ln, v_b_conv_ln, v_w_conv_out, v_b_conv_out, v_w_attn_out, v_w_mix_out, v_g_post_mix, v_g_pre_ffn, v_w_ffn_in, v_w_ffn_out, v_g_post_ffn):
    w = dict(rel_bias_table=rel_bias_table, g_pre_mix=g_pre_mix, w_in=w_in, b_glu=b_glu, w_dw=w_dw, b_dw=b_dw, g_conv_ln=g_conv_ln, b_conv_ln=b_conv_ln, w_conv_out=w_conv_out, b_conv_out=b_conv_out, w_attn_out=w_attn_out, w_mix_out=w_mix_out, g_post_mix=g_post_mix, g_pre_ffn=g_pre_ffn, w_ffn_in=w_ffn_in, w_ffn_out=w_ffn_out, g_post_ffn=g_post_ffn)
    m = dict(rel_bias_table=m_rel_bias_table, g_pre_mix=m_g_pre_mix, w_in=m_w_in, b_glu=m_b_glu, w_dw=m_w_dw, b_dw=m_b_dw, g_conv_ln=m_g_conv_ln, b_conv_ln=m_b_conv_ln, w_conv_out=m_w_conv_out, b_conv_out=m_b_conv_out, w_attn_out=m_w_attn_out, w_mix_out=m_w_mix_out, g_post_mix=m_g_post_mix, g_pre_ffn=m_g_pre_ffn, w_ffn_in=m_w_ffn_in, w_ffn_out=m_w_ffn_out, g_post_ffn=m_g_post_ffn)
    v = dict(rel_bias_table=v_rel_bias_table, g_pre_mix=v_g_pre_mix, w_in=v_w_in, b_glu=v_b_glu, w_dw=v_w_dw, b_dw=v_b_dw, g_conv_ln=v_g_conv_ln, b_conv_ln=v_b_conv_ln, w_conv_out=v_w_conv_out, b_conv_out=v_b_conv_out, w_attn_out=v_w_attn_out, w_mix_out=v_w_mix_out, g_post_mix=v_g_post_mix, g_pre_ffn=v_g_pre_ffn, w_ffn_in=v_w_ffn_in, w_ffn_out=v_w_ffn_out, g_post_ffn=v_g_post_ffn)

    def shard2d(d, n):
        a = d[n][0]
        return jnp.pad(a, ((0, HALO - CONV_W), (0, 0))) if n == 'w_dw' else a

    own = {n: shard2d(w, n).astype(F32 if n == 'w_dw' else BF) for n in BIG}
    packed = ['w_ffn_out', 'w_conv_out', 'w_mix_out', 'w_attn_out']
    alone = ['w_ffn_in', 'w_dw']
    shapes = [own[n].shape for n in packed]

    def pack(arrs, lead):
        return jnp.concatenate([a.reshape(lead + (-1, D)) for a in arrs], axis=len(lead))

    def unpack(p):
        out, pos = {}, 0
        for n, shp in zip(packed, shapes):
            rows = shp[0] * shp[1] // D
            out[n] = p[:, pos:pos + rows].reshape((NDEV,) + shp)
            pos += rows
        return out

    (g_in,) = _all_gather("gather_w_in", [own['w_in']])
    rest_own = [pack([own[n] for n in packed], ())] + [own[n] for n in alone]
    g_in, rest_own = lax.optimization_barrier((g_in, rest_own))
    gather_rest = _exchange_start("gather_rest_start", rest_own, True)
    W_in = _blocks_to_cols(g_in)
    kvq = [W_in[:, t * ATTN_W + g * GW:t * ATTN_W + (g + 1) * GW] for g in range(3) for t in (1, 2, 0)]
    Wp = jnp.concatenate([W_in[:, 3 * ATTN_W:]] + kvq, axis=1)

    def rest_fn(after):
        lands = _exchange_wait("gather_rest_wait", gather_rest, after, True)
        gw = unpack(_set_own_slot(lands[0], rest_own[0]))
        for n, l, o in zip(alone, lands[1:], rest_own[1:]):
            gw[n] = _set_own_slot(l, o)
        return (_blocks_to_cols(gw['w_ffn_in']), gw['w_ffn_out'].reshape(FFN, D), gw['w_conv_out'].reshape(D, D),
                gw['w_mix_out'].reshape(D, D), _blocks_to_cols(gw['w_attn_out']), _blocks_to_cols(gw['w_dw']))

    def to_blocks(n, g):
        if n in ('w_in', 'w_ffn_in', 'w_attn_out', 'w_dw'):
            return _cols_to_blocks(g)
        return g.reshape(NDEV, g.shape[0] // NDEV, g.shape[1])

    started = {}

    def early_fn(grads):
        blocks = [pack([to_blocks(n, grads[n]) for n in packed], (NDEV,))] + [to_blocks(n, grads[n]) for n in alone]
        started['blocks'] = blocks
        started['handle'] = _exchange_start("scatter_early_start", blocks, False)
        return started['handle'][4][0:1, 0:1]

    def late_fn(dW_in):
        started['in_blocks'] = [to_blocks('w_in', dW_in)]
        started['in_handle'] = _exchange_start("scatter_w_in_start", started['in_blocks'], False)
        return started['in_handle'][4][0:1, 0:1]

    g1_tied = g_pre_mix[0] + gather_rest[4][0, 0:1]
    loss_row, grad_x, small = _local_step(
        x[0], loss_target[0], Wp, rest_fn, early_fn, late_fn, rel_bias_table, g1_tied, b_glu[0], b_dw[0], g_conv_ln[0],
        b_conv_ln[0], b_conv_out[0], g_post_mix[0], g_pre_ffn[0], g_post_ffn[0])

    lands = _exchange_wait("scatter_early_wait", started['handle'], grad_x, False)
    lands = [_set_own_slot(l, _own_block(b)) for l, b in zip(lands, started['blocks'])]
    recv = unpack(lands[0])
    recv.update(zip(alone, lands[1:]))
    (land_in,) = _exchange_wait("scatter_w_in_wait", started['in_handle'], grad_x, False)
    recv['w_in'] = _set_own_slot(land_in, _own_block(started['in_blocks'][0]))
    tiles = dict(w_in=128, w_ffn_in=256, w_ffn_out=176, w_conv_out=128, w_mix_out=128, w_attn_out=512, w_dw=HALO)
    res = {}
    for n in BIG:
        g_, d_, nm_, nv_ = _adamw("adamw_" + n, shard2d(w, n), shard2d(m, n), shard2d(v, n), recv[n], tiles[n])
        if n == 'w_dw':
            g_, d_, nm_, nv_ = (t[:CONV_W] for t in (g_, d_, nm_, nv_))
        res[n] = tuple(t[None] for t in (g_, d_, nm_, nv_))

    tab = jnp.pad(small['rel_bias_table'], ((0, 0), (0, TAB_LANES - 3 * HEADS)))
    srows, stab = _all_gather("gather_small_grads", [_small_rows(small, loss_row), tab])
    loss11, small_res = _adamw_small(srows, stab, [w[n] for n in SMALL], [m[n] for n in SMALL], [v[n] for n in SMALL])
    loss = loss11.reshape(())
    for n, r in zip(SMALL, small_res):
        res[n] = r
    return (loss, grad_x[None], *[res[n][0] for n in WEIGHTS], *[res[n][1] for n in WEIGHTS],
            *[res[n][2] for n in WEIGHTS], *[res[n][3] for n in WEIGHTS])
```

```python
import functools
import math

import numpy as np
import jax
import jax.numpy as jnp
from jax import lax
from jax.experimental import pallas as pl
from jax.experimental.pallas import tpu as pltpu

F32 = jnp.float32
BF = jnp.bfloat16

D = 1024
HEAD_DIM = 64
HEADS = 8
GROUPS = ((128, 1), (512, 4), (2048, 16))
QBLK = 128
GW = HEADS * HEAD_DIM
ATTN_W = 3 * GW
REL_BUCKETS = 32
REL_MAX_DISTANCE = 2048
CONV_W = 31
HALO = 32
FFN = 2816
IN_W = 3 * ATTN_W + 2 * D + 2 * D
RMS_EPS = 1e-6
LN_EPS = 1e-5
NEG_INF = -1e30
SCALE = HEAD_DIM ** -0.5
NDEV = 8

ADAM_LR = 0.001
ADAM_B1 = 0.9
ADAM_B2 = 0.999
ADAM_EPS = 1e-08
ADAM_WD = 0.01
ADAM_STEP = 10

Z_G0 = 4096 // GW


def _kvq_blocks(g):
    return (Z_G0 + 3 * g, Z_G0 + 3 * g + 1, Z_G0 + 3 * g + 2)


WG_TK = 2048
VMEM_LIMIT = 52 * 1024 * 1024


def _cp(sem=None):
    if sem is None:
        return pltpu.CompilerParams(vmem_limit_bytes=VMEM_LIMIT)
    return pltpu.CompilerParams(vmem_limit_bytes=VMEM_LIMIT, dimension_semantics=sem)


def _sig(v):
    return jax.nn.sigmoid(v)


def _psum8(v):
    return v.reshape(v.shape[0] // 8, 8, v.shape[1]).sum(axis=0)


def _rms_r(v):
    return lax.rsqrt(jnp.mean(v * v, axis=-1, keepdims=True) + RMS_EPS)


def _rms_bwd(v, r, g, dy):
    gy = dy * g
    dv = r * gy - v * (r * r * r) * jnp.mean(v * gy, axis=-1, keepdims=True)
    return dv, dy * v * r


def _clip_k(k, k0, nk):
    return jnp.clip(k - k0, 0, nk - 1)


def _fused_mm(name, M, tm, grid_n, a_ops, b_ops, terms, acc_shapes, rows, consts, outs, parts, epilogue, n_outer=False):
    gm = M // tm
    nk_total = max([t[3] + t[4] for t in terms], default=1)
    n_a, n_b, n_r, n_c, n_o, n_p = len(a_ops), len(b_ops), len(rows), len(consts), len(outs), len(parts)
    n_acc = len(acc_shapes)
    use_scratch = nk_total > 1
    if parts:
        assert grid_n == 1

    def jj(j, follow):
        return j if follow else 0

    in_specs, args = [], []
    for (arr, tk, k0, nk) in a_ops:
        in_specs.append(pl.BlockSpec((tm, tk), functools.partial(lambda i, j, k, k0, nk: (i, _clip_k(k, k0, nk)), k0=k0, nk=nk)))
        args.append(arr)
    for (arr, nt, tk, tn, k0, nk, koff, joff, fj) in b_ops:
        if nt:
            in_specs.append(pl.BlockSpec((tn, tk), functools.partial(
                lambda i, j, k, k0, nk, koff, joff, fj: (joff + jj(j, fj), _clip_k(k, k0, nk) + koff),
                k0=k0, nk=nk, koff=koff, joff=joff, fj=fj)))
        else:
            in_specs.append(pl.BlockSpec((tk, tn), functools.partial(
                lambda i, j, k, k0, nk, koff, joff, fj: (_clip_k(k, k0, nk) + koff, joff + jj(j, fj)),
                k0=k0, nk=nk, koff=koff, joff=joff, fj=fj)))
        args.append(arr)
    for (arr, w, off, fj) in rows:
        in_specs.append(pl.BlockSpec((tm, w), functools.partial(lambda i, j, k, off, fj: (i, off + jj(j, fj)), off=off, fj=fj)))
        args.append(arr)
    for arr in consts:
        in_specs.append(pl.BlockSpec(arr.shape, functools.partial(lambda i, j, k, nd: (0,) * nd, nd=arr.ndim)))
        args.append(arr)
    out_specs, out_shape = [], []
    for (ncols, dt, w, off, fj) in outs:
        out_specs.append(pl.BlockSpec((tm, w), functools.partial(lambda i, j, k, off, fj: (i, off + jj(j, fj)), off=off, fj=fj)))
        out_shape.append(jax.ShapeDtypeStruct((M, ncols), dt))
    for (r, c) in parts:
        out_specs.append(pl.BlockSpec((r, c), lambda i, j, k: (0, 0)))
        out_shape.append(jax.ShapeDtypeStruct((r, c), F32))
    scratch = [pltpu.VMEM(s, F32) for s in acc_shapes] if use_scratch else []

    def body(*refs):
        pos = 0
        a_refs = refs[pos:pos + n_a]; pos += n_a
        b_refs = refs[pos:pos + n_b]; pos += n_b
        r_refs = refs[pos:pos + n_r]; pos += n_r
        c_refs = refs[pos:pos + n_c]; pos += n_c
        o_refs = refs[pos:pos + n_o]; pos += n_o
        p_refs = refs[pos:pos + n_p]; pos += n_p
        acc_refs = refs[pos:pos + n_acc] if use_scratch else ()
        i = pl.program_id(0)
        k = pl.program_id(2)

        def dot_of(ai, bi):
            a = a_refs[ai][...].astype(BF)
            b = b_refs[bi][...].astype(BF)
            if b_ops[bi][1]:
                return lax.dot_general(a, b, (((1,), (1,)), ((), ())), preferred_element_type=F32)
            return jnp.dot(a, b, preferred_element_type=F32)

        if parts:
            @pl.when((i == 0) & (k == 0))
            def _():
                for p in p_refs:
                    p[...] = jnp.zeros(p.shape, F32)

        def finish(accs):
            epilogue(accs, r_refs, c_refs, o_refs, p_refs)
            if parts:
                @pl.when(i == gm - 1)
                def _():
                    for p in p_refs:
                        p[0:1, :] = jnp.sum(p[...], axis=0, keepdims=True)

        if not use_scratch:
            accs = [None] * n_acc
            for (ai, bi, ci, k0, nk) in terms:
                d = dot_of(ai, bi)
                accs[ci] = d if accs[ci] is None else accs[ci] + d
            finish(accs)
        else:
            @pl.when(k == 0)
            def _():
                for acc in acc_refs:
                    acc[...] = jnp.zeros(acc.shape, F32)

            for (ai, bi, ci, k0, nk) in terms:
                def do(ai=ai, bi=bi, ci=ci):
                    acc_refs[ci][...] += dot_of(ai, bi)
                if k0 == 0 and nk == nk_total:
                    do()
                else:
                    pl.when((k >= k0) & (k < k0 + nk))(do)

            @pl.when(k == nk_total - 1)
            def _():
                finish([acc[...] for acc in acc_refs])

    grid = (gm, grid_n, nk_total)
    if n_outer:
        assert not parts
        swap = lambda spec: pl.BlockSpec(spec.block_shape, functools.partial(lambda j, i, k, f: f(i, j, k), f=spec.index_map))
        in_specs, out_specs, grid = [swap(sp) for sp in in_specs], [swap(sp) for sp in out_specs], (grid_n, gm, nk_total)
    res = pl.pallas_call(
        body, grid=grid, in_specs=in_specs, out_specs=out_specs, out_shape=out_shape,
        scratch_shapes=scratch, compiler_params=_cp(("arbitrary", "arbitrary", "arbitrary")), name=name,
    )(*args)
    return res


def _mm_tn(name, a, b, tm, tn, tk):
    S, Ka = a.shape
    Nb = b.shape[1]
    nk = S // tk

    def body(a_ref, b_ref, o_ref, acc):
        k = pl.program_id(2)

        @pl.when(k == 0)
        def _():
            acc[...] = jnp.zeros(acc.shape, F32)

        acc[...] += lax.dot_general(a_ref[...], b_ref[...], (((0,), (0,)), ((), ())), preferred_element_type=F32)

        @pl.when(k == nk - 1)
        def _():
            o_ref[...] = acc[...].astype(o_ref.dtype)

    return pl.pallas_call(
        body, grid=(Ka // tm, Nb // tn, nk),
        in_specs=[pl.BlockSpec((tk, tm), lambda i, j, k: (k, i)), pl.BlockSpec((tk, tn), lambda i, j, k: (k, j))],
        out_specs=pl.BlockSpec((tm, tn), lambda i, j, k: (i, j)),
        out_shape=jax.ShapeDtypeStruct((Ka, Nb), BF),
        scratch_shapes=[pltpu.VMEM((tm, tn), F32)],
        compiler_params=_cp(("parallel", "parallel", "arbitrary")), name=name,
    )(a, b)


def _resident_mm(name, M, tm, a_segs, w, rows, consts, outs, parts, epilogue):
    gm = M // tm
    n_a, n_r, n_c, n_o, n_p = len(a_segs), len(rows), len(consts), len(outs), len(parts)
    widths = [a.shape[1] for a in a_segs]
    offs = [sum(widths[:t]) for t in range(n_a)]
    once = pl.Buffered(1)

    def body(*refs):
        pos = 0
        a_refs = refs[pos:pos + n_a]; pos += n_a
        w_ref = refs[pos]; pos += 1
        r_refs = refs[pos:pos + n_r]; pos += n_r
        c_refs = refs[pos:pos + n_c]; pos += n_c
        o_refs = refs[pos:pos + n_o]; pos += n_o
        p_refs = refs[pos:pos + n_p]
        i = pl.program_id(0)
        if parts:
            @pl.when(i == 0)
            def _():
                for p in p_refs:
                    p[...] = jnp.zeros(p.shape, F32)
        acc = None
        for t in range(n_a):
            d = lax.dot_general(a_refs[t][...], w_ref[:, offs[t]:offs[t] + widths[t]], (((1,), (1,)), ((), ())),
                                preferred_element_type=F32)
            acc = d if acc is None else acc + d
        epilogue([acc], r_refs, c_refs, o_refs, p_refs)
        if parts:
            @pl.when(i == gm - 1)
            def _():
                for p in p_refs:
                    p[0:1, :] = jnp.sum(p[...], axis=0, keepdims=True)

    in_specs = [pl.BlockSpec((tm, wd), lambda i: (i, 0)) for wd in widths]
    in_specs.append(pl.BlockSpec(w.shape, lambda i: (0, 0), pipeline_mode=once))
    in_specs += [pl.BlockSpec((tm, c), lambda i: (i, 0)) for _, c in rows]
    in_specs += [pl.BlockSpec(c.shape, lambda i: (0, 0), pipeline_mode=once) for c in consts]
    out_specs = [pl.BlockSpec((tm, nc), lambda i: (i, 0)) for nc, _ in outs] + [pl.BlockSpec(pc, lambda i: (0, 0)) for pc in parts]
    out_shape = [jax.ShapeDtypeStruct((M, nc), dt) for nc, dt in outs] + [jax.ShapeDtypeStruct(pc, F32) for pc in parts]
    return pl.pallas_call(
        body, grid=(gm,), in_specs=in_specs, out_specs=out_specs, out_shape=out_shape,
        compiler_params=_cp(("arbitrary",)), name=name,
    )(*a_segs, w, *[r for r, _ in rows], *consts)


def _rel_bucket_np(dist):
    max_exact = REL_BUCKETS // 2
    d = np.maximum(dist, 0)
    df = np.maximum(d, 1).astype(np.float32)
    large = max_exact + (np.log(df / np.float32(max_exact)) / np.float32(math.log(REL_MAX_DISTANCE / max_exact))
                         * np.float32(REL_BUCKETS - max_exact)).astype(np.int32)
    large = np.minimum(large, REL_BUCKETS - 1)
    return np.where(d < max_exact, d, large).astype(np.int32)


N_LAYOUTS = 2


def _band_index():
    idx = np.zeros((N_LAYOUTS * 3, 1, QBLK * 2 * QBLK), np.int32)
    for g, (window, dil) in enumerate(GROUPS):
        span = window // dil
        k = np.arange(2 * QBLK)[:, None]; q = np.arange(QBLK)[None, :]
        off = q - k + QBLK
        idx[g, 0] = np.where((off >= 0) & (off <= span), _rel_bucket_np(off * dil), -1).reshape(-1)
        k = np.arange(QBLK)[:, None]; q = np.arange(2 * QBLK)[None, :]
        off = q - k
        idx[3 + g, 0] = np.where((off >= 0) & (off <= span), _rel_bucket_np(off * dil), -1).reshape(-1)
    return idx


_NB = QBLK * 2 * QBLK
_BCH = 4096


def _bias_build(tab_t, idx):
    def body(t_ref, i_ref, o_ref):
        ix = i_ref[0]
        t = t_ref[0]
        acc = jnp.full((HEADS, _BCH), NEG_INF, F32)
        for b in range(REL_BUCKETS):
            acc = jnp.where(ix == b, t[:, b:b + 1], acc)
        o_ref[0] = acc

    return pl.pallas_call(
        body, grid=(N_LAYOUTS * 3, _NB // _BCH),
        in_specs=[pl.BlockSpec((1, HEADS, REL_BUCKETS), lambda l, n: (l % 3, 0, 0)),
                  pl.BlockSpec((1, 1, _BCH), lambda l, n: (l, 0, n))],
        out_specs=pl.BlockSpec((1, HEADS, _BCH), lambda l, n: (l, 0, n)),
        out_shape=jax.ShapeDtypeStruct((N_LAYOUTS * 3, HEADS, _NB), F32), compiler_params=_cp(), name="bias_build",
    )(tab_t, idx)


def _bias_grad(ds, idx):
    nch = _NB // _BCH

    def body(d_ref, i_ref, o_ref):
        n = pl.program_id(1)

        @pl.when(n == 0)
        def _():
            o_ref[...] = jnp.zeros(o_ref.shape, F32)

        ix = i_ref[0]
        d = d_ref[0]
        lane = lax.broadcasted_iota(jnp.int32, (HEADS, 128), 1)
        acc = jnp.zeros((HEADS, 128), F32)
        for b in range(REL_BUCKETS):
            s = jnp.sum(jnp.where(ix == b, d, 0.0), axis=1, keepdims=True)
            acc = acc + jnp.where(lane == b, s, 0.0)
        o_ref[0] += acc

    return pl.pallas_call(
        body, grid=(3, nch),
        in_specs=[pl.BlockSpec((1, HEADS, _BCH), lambda l, n: (l, 0, n)),
                  pl.BlockSpec((1, 1, _BCH), lambda l, n: (3 + l, 0, n))],
        out_specs=pl.BlockSpec((1, HEADS, 128), lambda l, n: (l, 0, 0)),
        out_shape=jax.ShapeDtypeStruct((3, HEADS, 128), F32), compiler_params=_cp(), name="bias_grad",
    )(ds, idx)


PT = 256
PSTEP = 1024
STAT_W = 128


def _perm_np(dil):
    p = np.zeros((PT, PT), np.float32)
    m = np.arange(PT // dil)
    for c in range(dil):
        p[c * (PT // dil) + m, m * dil + c] = 1.0
    return p


def _perm_const(dil, dtype, inverse):
    p = _perm_np(dil)
    return jnp.asarray(p.T if inverse else p, dtype)


def _apply_perm(p, x):
    if x.dtype == F32:
        return jnp.dot(p, x, preferred_element_type=F32, precision=lax.Precision.HIGHEST)
    return jnp.dot(p, x, preferred_element_type=F32)


def _to_residue(name, arr, col_blocks, dil):
    S = arr.shape[0]
    nc = len(col_blocks)
    p = _perm_const(dil, arr.dtype, False)
    sub = PT // dil

    def body(*refs):
        p_ref, ins, o_ref = refs[0], refs[1:1 + nc], refs[1 + nc]
        for u in range(PSTEP // PT):
            for t, r in enumerate(ins):
                y = _apply_perm(p_ref[...], r[u * PT:(u + 1) * PT, :]).astype(o_ref.dtype)
                o_ref[:, u * sub:(u + 1) * sub, t * GW:(t + 1) * GW] = y.reshape(dil, sub, GW)

    out = pl.pallas_call(
        body, grid=(S // PSTEP,),
        in_specs=[pl.BlockSpec((PT, PT), lambda i: (0, 0))]
                 + [pl.BlockSpec((PSTEP, GW), functools.partial(lambda i, cb: (i, cb), cb=cb)) for cb in col_blocks],
        out_specs=pl.BlockSpec((dil, PSTEP // dil, nc * GW), lambda i: (0, i, 0)),
        out_shape=jax.ShapeDtypeStruct((dil, S // dil, nc * GW), arr.dtype), compiler_params=_cp(), name=name,
    )(p, *([arr] * nc))
    return out.reshape(S, nc * GW)


def _to_residue_stats(name, arr, dil):
    S = arr.shape[0]
    p = _perm_const(dil, F32, False)
    sub = PT // dil

    def body(p_ref, x_ref, o_ref):
        for u in range(PSTEP // PT):
            y = _apply_perm(p_ref[...], x_ref[u * PT:(u + 1) * PT, :])
            o_ref[:, u * sub:(u + 1) * sub, :] = y.reshape(dil, sub, STAT_W)

    out = pl.pallas_call(
        body, grid=(S // PSTEP,),
        in_specs=[pl.BlockSpec((PT, PT), lambda i: (0, 0)), pl.BlockSpec((PSTEP, STAT_W), lambda i: (i, 0))],
        out_specs=pl.BlockSpec((dil, PSTEP // dil, STAT_W), lambda i: (0, i, 0)),
        out_shape=jax.ShapeDtypeStruct((dil, S // dil, STAT_W), F32), compiler_params=_cp(), name=name,
    )(p, arr)
    return out.reshape(S, STAT_W)


def _from_residue(name, arr, dil):
    S, W = arr.shape
    p = _perm_const(dil, arr.dtype, True)
    sub = PT // dil

    def body(p_ref, x_ref, o_ref):
        for u in range(PSTEP // PT):
            x = x_ref[:, u * sub:(u + 1) * sub, :].reshape(PT, W)
            o_ref[u * PT:(u + 1) * PT, :] = _apply_perm(p_ref[...], x).astype(o_ref.dtype)

    return pl.pallas_call(
        body, grid=(S // PSTEP,),
        in_specs=[pl.BlockSpec((PT, PT), lambda i: (0, 0)), pl.BlockSpec((dil, PSTEP // dil, W), lambda i: (0, i, 0))],
        out_specs=pl.BlockSpec((PSTEP, W), lambda i: (i, 0)),
        out_shape=jax.ShapeDtypeStruct((S, W), arr.dtype), compiler_params=_cp(), name=name,
    )(p, arr.reshape(dil, S // dil, W))


PAIR_W = 2 * HEAD_DIM
NT_DIMS = (((1,), (1,)), ((), ()))
TN_DIMS = (((0,), (0,)), ((), ()))


def _attn_dims(S, dil):
    L = S // dil
    TQ = min(512, L)
    return L, TQ, L // TQ, TQ // QBLK


def _attn_specs(S, dil):
    L, TQ, nq, nsub = _attn_dims(S, dil)
    nb = L // QBLK
    cur = lambda cb, w=GW: pl.BlockSpec((TQ, w), lambda c, i: (c * nq + i, cb))
    prev = lambda cb, w=GW: pl.BlockSpec((QBLK, w), lambda c, i: (c * nb + jnp.maximum(i * nsub - 1, 0), cb))
    nxt = lambda cb, w=GW: pl.BlockSpec((QBLK, w), lambda c, i: (c * nb + jnp.minimum((i + 1) * nsub, nb - 1), cb))
    band = lambda r, c_: pl.BlockSpec((HEADS, r, c_), lambda c, i: (0, 0, 0))
    return L, TQ, nq, nsub, cur, prev, nxt, band


def _fill(buf, first_ref, second_ref):
    n = first_ref.shape[0]
    buf[0:n, :] = first_ref[...]
    buf[n:n + second_ref.shape[0], :] = second_ref[...]


def _attn_fwd(name, arr, bias_kq, cb, dil):
    S = arr.shape[0]
    kcb, vcb, qcb = cb
    L, TQ, nq, nsub, cur, prev, nxt, band = _attn_specs(S, dil)

    def body(q_ref, kc_ref, kp_ref, vc_ref, vp_ref, b_ref, o_ref, l_ref, kbuf, vbuf):
        i = pl.program_id(1)
        _fill(kbuf, kp_ref, kc_ref)
        _fill(vbuf, vp_ref, vc_ref)
        row = lax.broadcasted_iota(jnp.int32, (2 * QBLK, QBLK), 0)
        first = (row >= QBLK) | (i > 0)
        low = lax.broadcasted_iota(jnp.int32, (QBLK, PAIR_W), 1) < HEAD_DIM
        zero = jnp.zeros((QBLK, PAIR_W), BF)
        for j in range(nsub):
            rs = slice(j * QBLK, (j + 1) * QBLK)
            ks = slice(j * QBLK, (j + 2) * QBLK)
            lrows = []
            for hp in range(HEADS // 2):
                ps = slice(hp * PAIR_W, (hp + 1) * PAIR_W)
                qp = q_ref[rs, ps]
                kp = kbuf[ks, ps]
                vp = vbuf[ks, ps]
                halves = []
                for t in range(2):
                    qm = jnp.where(low if t == 0 else ~low, qp, zero)
                    s = lax.dot_general(kp, qm, NT_DIMS, preferred_element_type=F32) * SCALE + b_ref[2 * hp + t]
                    if j == 0:
                        s = jnp.where(first, s, NEG_INF)
                    m = jnp.max(s, axis=0, keepdims=True)
                    p = jnp.exp(s - m)
                    den = jnp.sum(p, axis=0, keepdims=True)
                    o2 = lax.dot_general(vp, p.astype(BF), TN_DIMS, preferred_element_type=F32)
                    halves.append(o2[t * HEAD_DIM:(t + 1) * HEAD_DIM, :] / den)
                    lrows.append(m + jnp.log(den))
                o_ref[rs, ps] = jnp.concatenate(halves, axis=0).T.astype(BF)
            lt = jnp.concatenate(lrows + [jnp.zeros((STAT_W - HEADS, QBLK), F32)], axis=0)
            l_ref[rs, :] = lt.T

    return pl.pallas_call(
        body, grid=(dil, nq),
        in_specs=[cur(qcb), cur(kcb), prev(kcb), cur(vcb), prev(vcb), band(2 * QBLK, QBLK)],
        out_specs=[cur(0), cur(0, STAT_W)],
        out_shape=[jax.ShapeDtypeStruct((S, GW), BF), jax.ShapeDtypeStruct((S, STAT_W), F32)],
        scratch_shapes=[pltpu.VMEM((QBLK + TQ, GW), BF), pltpu.VMEM((QBLK + TQ, GW), BF)],
        compiler_params=_cp(), name=name,
    )(arr, arr, arr, arr, arr, bias_kq)


def _attn_bwd(name, arr, bias_kq2, do, stats, cb, dil):
    S = arr.shape[0]
    kcb, vcb, qcb = cb
    L, TQ, nq, nsub, cur, prev, nxt, band = _attn_specs(S, dil)

    def body(k_ref, v_ref, qc_ref, qn_ref, b_ref, doc_ref, don_ref, sc_ref, sn_ref, o_ref, db_ref, qbuf, dobuf, sbuf, carry):
        c = pl.program_id(0)
        i = pl.program_id(1)

        @pl.when((c == 0) & (i == 0))
        def _():
            db_ref[...] = jnp.zeros(db_ref.shape, F32)
            carry[...] = jnp.zeros(carry.shape, F32)

        _fill(qbuf, qc_ref, qn_ref)
        _fill(dobuf, doc_ref, don_ref)
        for j in range(nsub + 1):
            rs = slice(j * QBLK, (j + 1) * QBLK)
            sbuf[:, rs] = (sc_ref[rs, :] if j < nsub else sn_ref[...]).T
        col = lax.broadcasted_iota(jnp.int32, (QBLK, 2 * QBLK), 1)
        last = (col < QBLK) | (i < nq - 1)
        for h in range(HEADS):
            hs = slice(h * HEAD_DIM, (h + 1) * HEAD_DIM)
            bias_h = b_ref[h]
            db = jnp.zeros((QBLK, 2 * QBLK), F32)
            tail = carry[:, hs]
            for j in range(nsub):
                rs = slice(j * QBLK, (j + 1) * QBLK)
                qs = slice(j * QBLK, (j + 2) * QBLK)
                qq = qbuf[qs, hs]
                dd = dobuf[qs, hs]
                kk = k_ref[rs, hs]
                s = lax.dot_general(kk, qq, NT_DIMS, preferred_element_type=F32) * SCALE + bias_h
                if j == nsub - 1:
                    s = jnp.where(last, s, NEG_INF)
                p = jnp.exp(s - sbuf[h:h + 1, qs])
                dp = lax.dot_general(v_ref[rs, hs], dd, NT_DIMS, preferred_element_type=F32)
                ds = p * (dp - sbuf[HEADS + h:HEADS + h + 1, qs])
                db = db + ds
                dsb = ds.astype(BF)
                o_ref[rs, h * HEAD_DIM:(h + 1) * HEAD_DIM] = (jnp.dot(dsb, qq, preferred_element_type=F32) * SCALE).astype(BF)
                o_ref[rs, GW + h * HEAD_DIM:GW + (h + 1) * HEAD_DIM] = jnp.dot(p.astype(BF), dd, preferred_element_type=F32).astype(BF)
                dqw = lax.dot_general(dsb, kk, TN_DIMS, preferred_element_type=F32) * SCALE
                o_ref[rs, 2 * GW + h * HEAD_DIM:2 * GW + (h + 1) * HEAD_DIM] = (dqw[0:QBLK] + tail).astype(BF)
                tail = dqw[QBLK:2 * QBLK]
            carry[:, hs] = tail
            db_ref[h] += db

    return pl.pallas_call(
        body, grid=(dil, nq),
        in_specs=[cur(kcb), cur(vcb), cur(qcb), nxt(qcb), band(QBLK, 2 * QBLK),
                  cur(0), nxt(0), cur(0, STAT_W), nxt(0, STAT_W)],
        out_specs=[cur(0, ATTN_W), band(QBLK, 2 * QBLK)],
        out_shape=[jax.ShapeDtypeStruct((S, ATTN_W), BF), jax.ShapeDtypeStruct((HEADS, QBLK, 2 * QBLK), F32)],
        scratch_shapes=[pltpu.VMEM((TQ + QBLK, GW), BF), pltpu.VMEM((TQ + QBLK, GW), BF), pltpu.VMEM((STAT_W, TQ + QBLK), F32),
                        pltpu.VMEM((QBLK, GW), F32)],
        compiler_params=_cp(("arbitrary", "arbitrary")), name=name,
    )(arr, arr, arr, arr, bias_kq2, do, do, stats, stats)


def _head_expand():
    e = np.zeros((STAT_W, GW), np.float32)
    for h in range(HEADS):
        e[h, h * HEAD_DIM:(h + 1) * HEAD_DIM] = 1.0
    return e


def _attn_merge(os_, ls_, S):
    dils = [d for _, d in GROUPS]
    pb = [_perm_const(d, BF, True) for d in dils[1:]]
    pf = [_perm_const(d, F32, True) for d in dils[1:]]
    expand = jnp.asarray(_head_expand(), BF)

    def body(o0, o1, o2, l0, l1, l2, pb1, pb2, pf1, pf2, e_ref, o_ref, l_ref):
        for u in range(PSTEP // PT):
            rs = slice(u * PT, (u + 1) * PT)
            res = lambda r, d: r[:, u * (PT // d):(u + 1) * (PT // d), :].reshape(PT, r.shape[2])
            ov = [o0[rs, :].astype(F32), _apply_perm(pb1[...], res(o1, dils[1])), _apply_perm(pb2[...], res(o2, dils[2]))]
            lv = [l0[rs, :], _apply_perm(pf1[...], res(l1, dils[1])), _apply_perm(pf2[...], res(l2, dils[2]))]
            m = jnp.maximum(jnp.maximum(lv[0], lv[1]), lv[2])
            ev = [jnp.exp(l - m) for l in lv]
            den = ev[0] + ev[1] + ev[2]
            acc = jnp.zeros((PT, GW), F32)
            for g in range(3):
                wide = jnp.dot((ev[g] / den).astype(BF), e_ref[...], preferred_element_type=F32)
                acc = acc + wide * ov[g]
            o_ref[rs, :] = acc.astype(BF)
            l_ref[rs, :] = m + jnp.log(den)

    nat = lambda w: pl.BlockSpec((PSTEP, w), lambda i: (i, 0))
    res = lambda d, w: pl.BlockSpec((d, PSTEP // d, w), lambda i: (0, i, 0))
    cst = lambda a: pl.BlockSpec(a.shape, lambda i: (0, 0))
    args = [os_[0], os_[1].reshape(dils[1], S // dils[1], GW), os_[2].reshape(dils[2], S // dils[2], GW),
            ls_[0], ls_[1].reshape(dils[1], S // dils[1], STAT_W), ls_[2].reshape(dils[2], S // dils[2], STAT_W),
            pb[0], pb[1], pf[0], pf[1], expand]
    return pl.pallas_call(
        body, grid=(S // PSTEP,),
        in_specs=[nat(GW), res(dils[1], GW), res(dils[2], GW), nat(STAT_W), res(dils[1], STAT_W), res(dils[2], STAT_W)]
                 + [cst(a) for a in args[6:]],
        out_specs=[nat(GW), nat(STAT_W)],
        out_shape=[jax.ShapeDtypeStruct((S, GW), BF), jax.ShapeDtypeStruct((S, STAT_W), F32)],
        compiler_params=_cp(), name="attn_merge",
    )(*args)


CT = 256
CBUF = HALO + CT + 8
RG = 4


def _ln_hat(u1):
    mu = jnp.mean(u1, axis=-1, keepdims=True)
    xc = u1 - mu
    rstd = lax.rsqrt(jnp.mean(xc * xc, axis=-1, keepdims=True) + LN_EPS)
    return xc * rstd, rstd


def _glu_window(hu_ref, hg_ref, huh_ref, hgh_ref, bglu_ref, buf_ref, i):
    bu = bglu_ref[:, 0:D]
    bg = bglu_ref[:, D:2 * D]
    uh = (huh_ref[...].astype(F32) + bu) * _sig(hgh_ref[...].astype(F32) + bg)
    buf_ref[0:HALO, :] = jnp.where(i > 0, uh, 0.0)
    a = hu_ref[...].astype(F32) + bu
    s = _sig(hg_ref[...].astype(F32) + bg)
    buf_ref[HALO:HALO + CT, :] = a * s
    buf_ref[HALO + CT:CBUF, :] = jnp.zeros((8, D), F32)
    return a, s


def _shift_copies(buf_ref, sh_ref):
    for r in range(8):
        sh_ref[r] = buf_ref[r:r + HALO + CT, :]


def _tap_rows(wb_ref, w_ref):
    for j in range(CONV_W):
        wb_ref[j * 8:(j + 1) * 8, :] = jnp.broadcast_to(w_ref[j:j + 1, :], (8, D))


def _conv_taps(sh_ref, wb_ref, out_ref, init, offset):
    for rg in range(CT // (8 * RG)):
        accs = [init] * RG
        for j in range(CONV_W):
            off = offset(j)
            wj = wb_ref[j * 8:(j + 1) * 8, :]
            for q in range(RG):
                row = 8 * (rg * RG + q + off // 8)
                accs[q] = accs[q] + wj * sh_ref[off % 8, row:row + 8, :]
        for q in range(RG):
            out_ref[(rg * RG + q) * 8:(rg * RG + q + 1) * 8, :] = accs[q]


def _conv_specs(S):
    cur = lambda cb: pl.BlockSpec((CT, D), lambda i: (i, cb))
    halo = lambda cb: pl.BlockSpec((HALO, D), lambda i: (jnp.maximum(i * (CT // HALO) - 1, 0), cb))
    full = lambda shp: pl.BlockSpec(shp, lambda i: (0, 0))
    return cur, halo, full


def _conv_fwd(z, b_glu, w_dw, b_dw, g_ln, b_ln):
    S = z.shape[0]
    cur, halo, full = _conv_specs(S)

    def body(hu, hg, huh, hgh, bglu, w, bdw, gln, bln, u1_ref, u3_ref, buf, sh, wb):
        i = pl.program_id(0)

        @pl.when(i == 0)
        def _():
            _tap_rows(wb, w)

        _glu_window(hu, hg, huh, hgh, bglu, buf, i)
        _shift_copies(buf, sh)
        _conv_taps(sh, wb, u1_ref, jnp.broadcast_to(bdw[...], (8, D)), lambda j: 2 + j)
        xh, _ = _ln_hat(u1_ref[...])
        u2 = xh * gln[...] + bln[...]
        u3_ref[...] = (u2 * _sig(u2)).astype(BF)

    return pl.pallas_call(
        body, grid=(S // CT,),
        in_specs=[cur(0), cur(1), halo(0), halo(1), full((1, 2 * D)), full((HALO, D)), full((1, D)), full((1, D)), full((1, D))],
        out_specs=[pl.BlockSpec((CT, D), lambda i: (i, 0))] * 2,
        out_shape=[jax.ShapeDtypeStruct((S, D), F32), jax.ShapeDtypeStruct((S, D), BF)],
        scratch_shapes=[pltpu.VMEM((CBUF, D), F32), pltpu.VMEM((8, HALO + CT, D), F32), pltpu.VMEM((CONV_W * 8, D), F32)],
        compiler_params=_cp(("arbitrary",)), name="conv_fwd",
    )(z, z, z, z, b_glu, w_dw, b_dw, g_ln, b_ln)


def _conv_bwd(du1, z, b_glu, w_dw):
    S = z.shape[0]
    n = S // CT
    cur, halo, full = _conv_specs(S)

    def body(du, dun, hu, hg, huh, hgh, bglu, w, dz_ref, dw_ref, dbg_ref, bufu, bufd, shu, shd, wb, du0_ref, dwacc):
        i = pl.program_id(0)

        @pl.when(i == 0)
        def _():
            _tap_rows(wb, w)
            dwacc[...] = jnp.zeros(dwacc.shape, F32)
            dbg_ref[...] = jnp.zeros(dbg_ref.shape, F32)

        a, s = _glu_window(hu, hg, huh, hgh, bglu, bufu, i)
        bufd[0:CT, :] = du[...]
        bufd[CT:CT + HALO, :] = jnp.where(i < n - 1, dun[...], 0.0)
        bufd[CT + HALO:CBUF, :] = jnp.zeros((8, D), F32)
        _shift_copies(bufu, shu)
        _shift_copies(bufd, shd)
        _conv_taps(shd, wb, du0_ref, jnp.zeros((8, D), F32), lambda j: 30 - j)
        for rg in range(CT // (8 * RG)):
            dch = [bufd[(rg * RG + q) * 8:(rg * RG + q + 1) * 8, :] for q in range(RG)]
            for j in range(CONV_W):
                off = 2 + j
                acc = dwacc[j * 8:(j + 1) * 8, :]
                for q in range(RG):
                    row = 8 * (rg * RG + q + off // 8)
                    acc = acc + dch[q] * shu[off % 8, row:row + 8, :]
                dwacc[j * 8:(j + 1) * 8, :] = acc
        du0 = du0_ref[...]
        dhu = du0 * s
        dhg = du0 * a * s * (1.0 - s)
        dz_ref[:, 0:D] = dhu.astype(BF)
        dz_ref[:, D:2 * D] = dhg.astype(BF)
        dbg_ref[:, 0:D] += _psum8(dhu)
        dbg_ref[:, D:2 * D] += _psum8(dhg)

        @pl.when(i == n - 1)
        def _():
            dbg_ref[0:1, :] = jnp.sum(dbg_ref[...], axis=0, keepdims=True)
            for j in range(CONV_W):
                dw_ref[j:j + 1, :] = jnp.sum(dwacc[j * 8:(j + 1) * 8, :], axis=0, keepdims=True)
            dw_ref[CONV_W:HALO, :] = jnp.zeros((HALO - CONV_W, D), F32)

    nxt = pl.BlockSpec((HALO, D), lambda i: (jnp.minimum((i + 1) * (CT // HALO), S // HALO - 1), 0))
    return pl.pallas_call(
        body, grid=(n,),
        in_specs=[pl.BlockSpec((CT, D), lambda i: (i, 0)), nxt, cur(0), cur(1), halo(0), halo(1), full((1, 2 * D)), full((HALO, D))],
        out_specs=[pl.BlockSpec((CT, 2 * D), lambda i: (i, 0)), full((HALO, D)), full((8, 2 * D))],
        out_shape=[jax.ShapeDtypeStruct((S, 2 * D), BF), jax.ShapeDtypeStruct((HALO, D), F32), jax.ShapeDtypeStruct((8, 2 * D), F32)],
        scratch_shapes=[pltpu.VMEM((CBUF, D), F32), pltpu.VMEM((CBUF, D), F32), pltpu.VMEM((8, HALO + CT, D), F32),
                        pltpu.VMEM((8, HALO + CT, D), F32), pltpu.VMEM((CONV_W * 8, D), F32), pltpu.VMEM((CT, D), F32),
                        pltpu.VMEM((CONV_W * 8, D), F32)],
        compiler_params=_cp(("arbitrary",)), name="conv_bwd",
    )(du1, du1, z, z, z, z, b_glu, w_dw)


MESH = pl.DeviceIdType.MESH


def _all_gather(name, shards):
    n = len(shards)

    def body(*refs):
        ins, outs = refs[:n], refs[n:2 * n]
        send_sems, recv_sems, local_sems = refs[2 * n:]
        x, y, c = lax.axis_index("x"), lax.axis_index("y"), lax.axis_index("c")
        me, sibling = (x, y, c), (x, y, 1 - c)
        chips = [(1 - x, y), (x, 1 - y), (1 - x, 1 - y)]

        def slot(a, px, py, pc):
            return outs[a].at[4 * px + 2 * py + pc]

        def copy(a, k, block, to, src=None):
            return pltpu.make_async_remote_copy(
                src_ref=slot(a, *block) if src is None else src, dst_ref=slot(a, *block),
                send_sem=send_sems.at[a, k], recv_sem=recv_sems.at[a, k], device_id=to, device_id_type=MESH)

        mine = [pltpu.make_async_copy(ins[a], slot(a, *me), local_sems.at[a]) for a in range(n)]
        for cp in mine:
            cp.start()
        first = []
        for a in range(n):
            first.append(copy(a, 0, me, sibling, src=ins[a]))
            first += [copy(a, 1 + j, me, (*chip, c), src=ins[a]) for j, chip in enumerate(chips)]
        for cp in first:
            cp.start()
        passed = []
        for j, chip in enumerate(chips):
            for a in range(n):
                copy(a, 1 + j, (*chip, c), me).wait_recv()
                fwd = copy(a, 4 + j, (*chip, c), sibling)
                fwd.start()
                passed.append(fwd)
        for a in range(n):
            copy(a, 0, sibling, me).wait_recv()
        for j, chip in enumerate(chips):
            for a in range(n):
                copy(a, 4 + j, (*chip, 1 - c), me).wait_recv()
        for cp in first + passed:
            cp.wait_send()
        for cp in mine:
            cp.wait()

    anyspec = pl.BlockSpec(memory_space=pl.ANY)
    return pl.pallas_call(
        body, in_specs=[anyspec] * n, out_specs=[anyspec] * n,
        out_shape=[jax.ShapeDtypeStruct((NDEV,) + s.shape, s.dtype) for s in shards],
        scratch_shapes=[pltpu.SemaphoreType.DMA((n, 7)), pltpu.SemaphoreType.DMA((n, 7)), pltpu.SemaphoreType.DMA((n,))],
        name=name,
    )(*shards)


HBM_SPEC = pl.BlockSpec(memory_space=pltpu.HBM)
SEM_SPEC = pl.BlockSpec(memory_space=pltpu.SEMAPHORE)
DATAFLOW = pltpu.SideEffectType.DATAFLOW_SIDE_EFFECTING


def _peers():
    x, y, c = lax.axis_index("x"), lax.axis_index("y"), lax.axis_index("c")
    out = []
    for k in range(1, NDEV):
        px = 1 - x if k & 4 else x
        py = 1 - y if k & 2 else y
        pc = 1 - c if k & 1 else c
        out.append(((px, py, pc), 4 * px + 2 * py + pc))
    return 4 * x + 2 * y + c, out


def _exchange_copies(srcs, lands, send_sems, recv_sems, gather):
    my, peers = _peers()
    pairs = []
    for k, (dev, pid) in enumerate(peers):
        for a in range(len(srcs)):
            src = srcs[a] if gather else srcs[a].at[pid]
            sems = dict(send_sem=send_sems[a * (NDEV - 1) + k], recv_sem=recv_sems[a * (NDEV - 1) + k], device_id=dev,
                        device_id_type=MESH)
            pairs.append((pltpu.make_async_remote_copy(src_ref=src, dst_ref=lands[a].at[my], **sems),
                          pltpu.make_async_remote_copy(src_ref=src, dst_ref=lands[a].at[pid], **sems)))
    return pairs


def _exchange_start(name, srcs, gather):
    n = len(srcs)
    ns = n * (NDEV - 1)
    shapes = [(s.shape if gather else s.shape[1:]) for s in srcs]
    lands = [lax.empty((NDEV,) + shp, s.dtype) for shp, s in zip(shapes, srcs)]

    def body(*refs):
        src_refs, land_refs = refs[:n], refs[n:2 * n]
        send_sems, recv_sems = refs[2 * n:2 * n + ns], refs[2 * n + ns:2 * n + 2 * ns]
        token = refs[-1]
        for mine, _ in _exchange_copies(src_refs, land_refs, send_sems, recv_sems, gather):
            mine.start()
        token[...] = jnp.zeros(token.shape, token.dtype)

    hbm = lambda a: pltpu.HBM(a.shape, a.dtype)
    res = pl.pallas_call(
        body, name=name,
        out_shape=(*([pltpu.SemaphoreType.DMA(())] * (2 * ns)), *[hbm(s) for s in srcs], *[hbm(l) for l in lands],
                   jax.ShapeDtypeStruct((8, 128), F32)),
        in_specs=[HBM_SPEC] * (2 * n),
        out_specs=(*([SEM_SPEC] * (2 * ns)), *([HBM_SPEC] * (2 * n)), pl.BlockSpec(memory_space=pltpu.VMEM)),
        input_output_aliases={i: 2 * ns + i for i in range(2 * n)},
        compiler_params=pltpu.CompilerParams(has_side_effects=DATAFLOW),
    )(*[pltpu.with_memory_space_constraint(s, pltpu.HBM) for s in srcs],
      *[pltpu.with_memory_space_constraint(l, pltpu.HBM) for l in lands])
    return list(res[:ns]), list(res[ns:2 * ns]), list(res[2 * ns:2 * ns + n]), list(res[2 * ns + n:2 * ns + 2 * n]), res[-1]


def _exchange_wait(name, handle, after, gather):
    send_sems, recv_sems, srcs, lands, _ = handle
    n = len(srcs)
    ns = n * (NDEV - 1)

    def body(*refs):
        src_refs, land_refs = refs[:n], refs[n:2 * n]
        s_sems, r_sems = refs[2 * n:2 * n + ns], refs[2 * n + ns:2 * n + 2 * ns]
        for mine, theirs in _exchange_copies(src_refs, land_refs, s_sems, r_sems, gather):
            mine.wait_send()
            theirs.wait_recv()

    hbm = lambda a: pltpu.HBM(a.shape, a.dtype)
    res = pl.pallas_call(
        body, name=name,
        out_shape=(*[hbm(s) for s in srcs], *[hbm(l) for l in lands]),
        in_specs=[HBM_SPEC] * (2 * n) + [SEM_SPEC] * (2 * ns) + [pl.BlockSpec(memory_space=pl.ANY)],
        out_specs=tuple([HBM_SPEC] * (2 * n)),
        input_output_aliases={i: i for i in range(2 * n)},
        compiler_params=pltpu.CompilerParams(has_side_effects=DATAFLOW),
    )(*srcs, *lands, *send_sems, *recv_sems, after)
    return list(res[n:])


def _set_own_slot(land, own):
    my = 4 * lax.axis_index("x") + 2 * lax.axis_index("y") + lax.axis_index("c")
    return lax.dynamic_update_slice(land, own[None], (my, 0, 0))


def _own_block(blocks):
    my = 4 * lax.axis_index("x") + 2 * lax.axis_index("y") + lax.axis_index("c")
    return lax.dynamic_index_in_dim(blocks, my, axis=0, keepdims=False)


_C1 = 1.0 - ADAM_B1 ** ADAM_STEP
_C2 = 1.0 - ADAM_B2 ** ADAM_STEP


def _adamw(name, w, m, v, recv, tr):
    R, C = w.shape

    def body(w_ref, m_ref, v_ref, r_ref, g_ref, d_ref, nm_ref, nv_ref):
        g = r_ref[0].astype(F32)
        for s in range(1, NDEV):
            g = g + r_ref[s].astype(F32)
        wv = w_ref[...]
        nm = ADAM_B1 * m_ref[...] + (1.0 - ADAM_B1) * g
        nv = ADAM_B2 * v_ref[...] + (1.0 - ADAM_B2) * (g * g)
        m_hat = nm / _C1
        v_hat = nv / _C2
        g_ref[...] = g
        d_ref[...] = -ADAM_LR * (m_hat / (jnp.sqrt(v_hat) + ADAM_EPS) + ADAM_WD * wv)
        nm_ref[...] = nm
        nv_ref[...] = nv

    blk = pl.BlockSpec((tr, C), lambda i: (i, 0))
    return pl.pallas_call(
        body, grid=(R // tr,), in_specs=[blk, blk, blk, pl.BlockSpec((NDEV, tr, C), lambda i: (0, i, 0))],
        out_specs=[blk] * 4, out_shape=[jax.ShapeDtypeStruct((R, C), F32)] * 4,
        compiler_params=_cp(), name=name,
    )(w, m, v, recv)


def _row(v):
    return v.reshape(1, -1)


def _local_step(xs, tgt, Wp, rest_fn, early_fn, late_fn, rel_bias_table, g_pre_mix, b_glu, b_dw, g_conv_ln,
                b_conv_ln, b_conv_out, g_post_mix, g_pre_ffn, g_post_ffn):
    S = xs.shape[0]
    g1, g2, g3, g4 = _row(g_pre_mix), _row(g_post_mix), _row(g_pre_ffn), _row(g_post_ffn)
    bglu, bdw, gln, bln, bco = _row(b_glu), _row(b_dw), _row(g_conv_ln), _row(b_conv_ln), _row(b_conv_out)
    full = (D, F32, D, 0, False)
    fullb = (D, BF, D, 0, False)

    def epi_rms(accs, r, c, o, p):
        v = r[0][...]
        o[0][...] = (v * _rms_r(v) * c[0][...]).astype(BF)

    (h1,) = _fused_mm("rms_in", S, 512, 1, [], [], [], [], [(xs, D, 0, False)], [g1], [fullb], [], epi_rms)

    def epi_cast(accs, r, c, o, p):
        o[0][...] = accs[0].astype(BF)

    ZT = IN_W // 4
    (z,) = _fused_mm("in_proj", S, 1024, 4, [(h1, D, 0, 1)], [(Wp, False, D, ZT, 0, 1, 0, 0, True)], [(0, 0, 0, 0, 1)],
                     [(1024, ZT)], [], [], [(IN_W, BF, ZT, 0, True)], [], epi_cast, n_outer=True)

    idx = jnp.asarray(_band_index())
    tab_t = rel_bias_table.T.reshape(3, HEADS, REL_BUCKETS)
    bias_all = _bias_build(tab_t, idx)
    bias_kq = [bias_all[g].reshape(HEADS, 2 * QBLK, QBLK) for g in range(3)]
    bias_kq2 = [bias_all[3 + g].reshape(HEADS, QBLK, 2 * QBLK) for g in range(3)]
    dils = [d for _, d in GROUPS]
    qkv = [(z, _kvq_blocks(0))] + [(_to_residue(f"qkv_to_residue_g{g}", z, _kvq_blocks(g), dils[g]), (0, 1, 2)) for g in (1, 2)]
    os_, ls_ = [], []
    for g in range(3):
        o_g, l_g = _attn_fwd(f"attn_fwd_g{g}", qkv[g][0], bias_kq[g], qkv[g][1], dils[g])
        os_.append(o_g)
        ls_.append(l_g)
    o_att, lse = _attn_merge(os_, ls_, S)

    Wfi, Wfo, Wco, Wmo, Wao, wdw = rest_fn(lse)
    u1, u3 = _conv_fwd(z, bglu, wdw, bdw, gln, bln)

    def epi_mix(accs, r, c, o, p):
        ya = accs[0]
        yc = accs[1] + c[0][...]
        mg = _sig(r[0][...].astype(F32)) * ya + _sig(r[1][...].astype(F32)) * yc
        mgb = mg.astype(BF)
        m2 = jnp.dot(mgb, c[1][...], preferred_element_type=F32)
        x1 = r[2][...] + m2 * _rms_r(m2) * c[2][...]
        o[0][...] = ya.astype(BF)
        o[1][...] = yc.astype(BF)
        o[2][...] = mgb
        o[3][...] = m2.astype(BF)
        o[4][...] = x1
        o[5][...] = (x1 * _rms_r(x1) * c[3][...]).astype(BF)

    y_attn, y_conv, merged, m2, x1, h2 = _fused_mm(
        "mix_fwd", S, 512, 1, [(o_att, GW, 0, 1), (u3, D, 0, 1)],
        [(Wao, False, GW, D, 0, 1, 0, 0, False), (Wco, False, D, D, 0, 1, 0, 0, False)], [(0, 0, 0, 0, 1), (1, 1, 1, 0, 1)],
        [(512, D), (512, D)], [(z, D, 2, False), (z, D, 3, False), (xs, D, 0, False)], [bco, Wmo, g2, g3],
        [fullb, fullb, fullb, fullb, full, fullb], [], epi_mix)

    HN = FFN // 2

    def epi_ffn_in(accs, r, c, o, p):
        gt, up = accs
        o[0][...] = gt.astype(BF)
        o[1][...] = up.astype(BF)
        o[2][...] = (gt * _sig(gt) * up).astype(BF)

    gate, up, act = _fused_mm(
        "ffn_in", S, 512, 2, [(h2, D, 0, 1)],
        [(Wfi, False, D, HN, 0, 1, 0, 0, True), (Wfi, False, D, HN, 0, 1, 0, 2, True)], [(0, 0, 0, 0, 1), (0, 1, 1, 0, 1)],
        [(512, HN), (512, HN)], [], [], [(FFN, BF, HN, 0, True)] * 3, [], epi_ffn_in, n_outer=True)

    def epi_loss(accs, r, c, o, p):
        f2 = accs[0]
        g = c[0][...]
        rr = _rms_r(f2)
        err = r[0][...] + f2 * rr * g - r[1][...]
        dy = err * (1.0 / D)
        df2, dgr = _rms_bwd(f2, rr, g, dy)
        o[0][...] = dy
        o[1][...] = df2.astype(BF)
        p[0][...] += _psum8(err * err)
        p[1][...] += _psum8(dgr)

    dy, df2, loss_p, dg4 = _fused_mm(
        "ffn_out_loss", S, 512, 1, [(act, FFN, 0, 1)], [(Wfo, False, FFN, D, 0, 1, 0, 0, False)], [(0, 0, 0, 0, 1)],
        [(512, D)], [(x1, D, 0, False), (tgt, D, 0, False)], [g4], [full, fullb], [(8, D), (8, D)], epi_loss)

    def epi_swiglu(accs, r, c, o, p):
        da = accs[0]
        gt = r[0][...].astype(F32)
        sg = _sig(gt)
        o[0][...] = (da * r[1][...].astype(F32) * sg * (1.0 + gt * (1.0 - sg))).astype(BF)
        o[1][...] = (da * gt * sg).astype(BF)

    dgate, dup = _fused_mm(
        "ffn_out_bwd", S, 512, 2, [(df2, D, 0, 1)], [(Wfo, True, D, HN, 0, 1, 0, 0, True)], [(0, 0, 0, 0, 1)],
        [(512, HN)], [(gate, HN, 0, True), (up, HN, 0, True)], [], [(FFN, BF, HN, 0, True)] * 2, [], epi_swiglu, n_outer=True)
    dWfo = _mm_tn("dw_ffn_out", act, df2, HN, D, WG_TK)

    def epi_dh2(accs, r, c, o, p):
        dh2 = accs[0]
        x1v = r[1][...]
        r3 = _rms_r(x1v)
        d1, dg3r = _rms_bwd(x1v, r3, c[0][...], dh2)
        dx1 = r[0][...] + d1
        m2v = r[2][...].astype(F32)
        r2 = _rms_r(m2v)
        dm2, dg2r = _rms_bwd(m2v, r2, c[1][...], dx1)
        o[0][...] = dx1
        o[1][...] = dm2.astype(BF)
        p[0][...] += _psum8(dg3r)
        p[1][...] += _psum8(dg2r)

    dx1, dm2, dg3, dg2 = _resident_mm(
        "ffn_in_bwd", S, 512, [dgate, dup], Wfi, [(dy, D), (x1, D), (m2, D)], [g3, g2], [(D, F32), (D, BF)], [(8, D), (8, D)], epi_dh2)
    dWfi = jnp.concatenate([_mm_tn("dw_ffn_gate", h2, dgate, D, HN, WG_TK), _mm_tn("dw_ffn_up", h2, dup, D, HN, WG_TK)], axis=1)

    def epi_dmix(accs, r, c, o, p):
        dm = accs[0]
        sa = _sig(r[0][...].astype(F32))
        sc = _sig(r[1][...].astype(F32))
        o[0][...] = (dm * sa).astype(BF)
        o[1][...] = (dm * sc).astype(BF)
        o[2][:, 0:D] = (dm * r[2][...].astype(F32) * sa * (1.0 - sa)).astype(BF)
        o[2][:, D:2 * D] = (dm * r[3][...].astype(F32) * sc * (1.0 - sc)).astype(BF)

    dy_attn, dy_conv, dz_gate = _fused_mm(
        "mix_bwd", S, 512, 1, [(dm2, D, 0, 1)], [(Wmo, True, D, D, 0, 1, 0, 0, False)], [(0, 0, 0, 0, 1)], [(512, D)],
        [(z, D, 2, False), (z, D, 3, False), (y_attn, D, 0, False), (y_conv, D, 0, False)], [],
        [fullb, fullb, (2 * D, BF, 2 * D, 0, False)], [], epi_dmix)
    dWmo = _mm_tn("dw_mix_out", merged, dm2, D, D, WG_TK)

    def epi_dconv(accs, r, c, o, p):
        du3 = accs[0]
        xh, rstd = _ln_hat(r[0][...])
        gl = c[0][...]
        u2 = xh * gl + c[1][...]
        sg = _sig(u2)
        du2 = du3 * sg * (1.0 + u2 * (1.0 - sg))
        dxh = du2 * gl
        du1 = rstd * (dxh - jnp.mean(dxh, axis=-1, keepdims=True) - xh * jnp.mean(dxh * xh, axis=-1, keepdims=True))
        o[0][...] = du1
        p[0][...] += _psum8(du2 * xh)
        p[1][...] += _psum8(du2)
        p[2][...] += _psum8(du1)
        p[3][...] += _psum8(r[1][...].astype(F32))

    du1, dgln, dbln, dbdw, dbco = _fused_mm(
        "conv_out_bwd", S, 512, 1, [(dy_conv, D, 0, 1)], [(Wco, True, D, D, 0, 1, 0, 0, False)], [(0, 0, 0, 0, 1)], [(512, D)],
        [(u1, D, 0, False), (dy_conv, D, 0, False)], [gln, bln], [full], [(8, D)] * 4, epi_dconv)
    dWco = _mm_tn("dw_conv_out", u3, dy_conv, D, D, WG_TK)
    dz_glu, dwdw, dbglu = _conv_bwd(du1, z, bglu, wdw)

    head_sum = np.zeros((GW, STAT_W), np.float32)
    for h in range(HEADS):
        head_sum[h * HEAD_DIM:(h + 1) * HEAD_DIM, HEADS + h] = 1.0
    head_sum = jnp.asarray(head_sum)

    def epi_do(accs, r, c, o, p):
        do = accs[0]
        o[0][...] = do.astype(BF)
        delta = jnp.dot(do * r[0][...].astype(F32), c[0][...], preferred_element_type=F32, precision=lax.Precision.HIGHEST)
        lane = lax.broadcasted_iota(jnp.int32, delta.shape, 1)
        o[1][...] = jnp.where(lane < HEADS, r[1][...], delta)

    do, stats = _fused_mm(
        "attn_out_bwd", S, 1024, 1, [(dy_attn, D, 0, 1)], [(Wao, True, D, GW, 0, 1, 0, 0, False)], [(0, 0, 0, 0, 1)], [(1024, GW)],
        [(o_att, GW, 0, False), (lse, STAT_W, 0, False)], [head_sum], [(GW, BF, GW, 0, False), (STAT_W, F32, STAT_W, 0, False)], [], epi_do)
    dWao = _mm_tn("dw_attn_out", o_att, dy_attn, GW, D, WG_TK)

    tie = early_fn(dict(w_ffn_in=dWfi, w_ffn_out=dWfo, w_conv_out=dWco, w_mix_out=dWmo, w_attn_out=dWao, w_dw=dwdw))
    stats = stats + tie
    dos = [do] + [_to_residue(f"do_to_residue_g{g}", do, (0,), dils[g]) for g in (1, 2)]
    sts = [stats] + [_to_residue_stats(f"stats_to_residue_g{g}", stats, dils[g]) for g in (1, 2)]
    dqkv, dbs = [], []
    for g in range(3):
        arr, cb = qkv[g]
        dg, db = _attn_bwd(f"attn_bwd_g{g}", arr, bias_kq2[g], dos[g], sts[g], cb, dils[g])
        dqkv.append(dg if g == 0 else _from_residue(f"dqkv_from_residue_g{g}", dg, dils[g]))
        dbs.append(db.reshape(HEADS, _NB))
    dtab = _bias_grad(jnp.stack(dbs), idx)[:, :, :REL_BUCKETS].reshape(3 * HEADS, REL_BUCKETS).T

    def epi_dx(accs, r, c, o, p):
        xv = r[1][...]
        d1, dg1r = _rms_bwd(xv, _rms_r(xv), c[0][...], accs[0])
        o[0][...] = r[0][...] + d1
        p[0][...] += _psum8(dg1r)

    dWg = [_mm_tn(f"dw_in_g{g}", h1, dqkv[g], D, ATTN_W, WG_TK) for g in range(3)]
    dW_in = jnp.concatenate(
        [t[:, 2 * GW:] for t in dWg] + [t[:, :GW] for t in dWg] + [t[:, GW:2 * GW] for t in dWg]
        + [_mm_tn("dw_in_glu", h1, dz_glu, D, D, WG_TK), _mm_tn("dw_in_gate", h1, dz_gate, D, D, WG_TK)], axis=1)
    g1_late = g1 + late_fn(dW_in)
    grad_x, dg1 = _resident_mm(
        "in_proj_bwd", S, 512, [dz_glu, dz_gate, dqkv[0], dqkv[1], dqkv[2]], Wp, [(dx1, D), (xs, D)], [g1_late], [(D, F32)], [(8, D)], epi_dx)

    small = dict(rel_bias_table=dtab, g_pre_mix=dg1[0], b_glu=dbglu[0], b_dw=dbdw[0], g_conv_ln=dgln[0], b_conv_ln=dbln[0],
                 b_conv_out=dbco[0], g_post_mix=dg2[0], g_pre_ffn=dg3[0], g_post_ffn=dg4[0])
    return loss_p[0], grad_x, small


SMALL = ['rel_bias_table', 'g_pre_mix', 'b_glu', 'b_dw', 'g_conv_ln', 'b_conv_ln', 'b_conv_out', 'g_post_mix', 'g_pre_ffn',
         'g_post_ffn']
BIG = ['w_in', 'w_ffn_in', 'w_ffn_out', 'w_conv_out', 'w_mix_out', 'w_attn_out', 'w_dw']
WEIGHTS = ['rel_bias_table', 'g_pre_mix', 'w_in', 'b_glu', 'w_dw', 'b_dw', 'g_conv_ln', 'b_conv_ln', 'w_conv_out', 'b_conv_out',
           'w_attn_out', 'w_mix_out', 'g_post_mix', 'g_pre_ffn', 'w_ffn_in', 'w_ffn_out', 'g_post_ffn']
SMALL_ROWS = 16


ROW_SMALL = ['g_pre_mix', 'b_glu', 'b_dw', 'g_conv_ln', 'b_conv_ln', 'b_conv_out', 'g_post_mix', 'g_pre_ffn', 'g_post_ffn']
LOSS_ROW = 10
TAB_LANES = 128


def _small_rows(small, loss_row):
    rows = [small[n].reshape(-1, D) for n in ROW_SMALL] + [loss_row.reshape(1, D)]
    n = sum(r.shape[0] for r in rows)
    return jnp.concatenate(rows + [jnp.zeros((SMALL_ROWS - n, D), F32)], axis=0)


def _adamw_small(recv_rows, recv_tab, ws, ms, vs):
    np_ = len(SMALL)

    def body(*refs):
        rr, rt = refs[0], refs[1]
        w_refs, m_refs, v_refs = refs[2:2 + np_], refs[2 + np_:2 + 2 * np_], refs[2 + 2 * np_:2 + 3 * np_]
        loss_ref = refs[2 + 3 * np_]
        outs = refs[3 + 3 * np_:]
        rows = rr[0]
        tab = rt[0]
        for s_ in range(1, NDEV):
            rows = rows + rr[s_]
            tab = tab + rt[s_]
        loss_ref[...] = jnp.sum(rows[LOSS_ROW:LOSS_ROW + 1, :], axis=1, keepdims=True) * (0.5 / D)
        row = 0
        for p, n in enumerate(SMALL):
            if n == 'rel_bias_table':
                g = tab[:, 0:3 * HEADS]
            else:
                k = w_refs[p].shape[1] // D
                g = rows[row:row + 1, :] if k == 1 else jnp.concatenate([rows[row + t:row + t + 1, :] for t in range(k)], axis=1)
                row += k
            nm = ADAM_B1 * m_refs[p][...] + (1.0 - ADAM_B1) * g
            nv = ADAM_B2 * v_refs[p][...] + (1.0 - ADAM_B2) * (g * g)
            outs[4 * p][...] = g
            outs[4 * p + 1][...] = -ADAM_LR * ((nm / _C1) / (jnp.sqrt(nv / _C2) + ADAM_EPS) + ADAM_WD * w_refs[p][...])
            outs[4 * p + 2][...] = nm
            outs[4 * p + 3][...] = nv

    out_shape = [jax.ShapeDtypeStruct((1, 1), F32)]
    for a_ in ws:
        out_shape += [jax.ShapeDtypeStruct(a_.shape, F32)] * 4
    res = pl.pallas_call(body, out_shape=out_shape, compiler_params=_cp(), name="adamw_small")(recv_rows, recv_tab, *ws, *ms, *vs)
    return res[0], [tuple(res[1 + 4 * p:5 + 4 * p]) for p in range(np_)]


def _cols_to_blocks(a):
    R = a.shape[0]
    return a.reshape(R, NDEV, a.shape[1] // NDEV).transpose(1, 0, 2)


def _blocks_to_cols(a):
    return a.transpose(1, 0, 2).reshape(a.shape[1], NDEV * a.shape[2])


def kernel(x, rel_bias_table, g_pre_mix, w_in, b_glu, w_dw, b_dw, g_conv_ln, b_conv_ln, w_conv_out, b_conv_out, w_attn_out, w_mix_out, g_post_mix, g_pre_ffn, w_ffn_in, w_ffn_out, g_post_ffn, loss_target, m_rel_bias_table, m_g_pre_mix, m_w_in, m_b_glu, m_w_dw, m_b_dw, m_g_conv_ln, m_b_conv_ln, m_w_conv_out, m_b_conv_out, m_w_attn_out, m_w_mix_out, m_g_post_mix, m_g_pre_ffn, m_w_ffn_in, m_w_ffn_out, m_g_post_ffn, v_rel_bias_table, v_g_pre_mix, v_w_in, v_b_glu, v_w_dw, v_b_dw, v_g_conv_ln, v_b_conv_ln, v_w_conv_out, v_b_conv_out, v_w_attn_out, v_w_mix_out, v_g_post_mix, v_g_pre_ffn, v_w_ffn_in, v_w_ffn_out, v_g_post_ffn):
    w = dict(rel_bias_table=rel_bias_table, g_pre_mix=g_pre_mix, w_in=w_in, b_glu=b_glu, w_dw=w_dw, b_dw=b_dw, g_conv_ln=g_conv_ln, b_conv_ln=b_conv_ln, w_conv_out=w_conv_out, b_conv_out=b_conv_out, w_attn_out=w_attn_out, w_mix_out=w_mix_out, g_post_mix=g_post_mix, g_pre_ffn=g_pre_ffn, w_ffn_in=w_ffn_in, w_ffn_out=w_ffn_out, g_post_ffn=g_post_ffn)
    m = dict(rel_bias_table=m_rel_bias_table, g_pre_mix=m_g_pre_mix, w_in=m_w_in, b_glu=m_b_glu, w_dw=m_w_dw, b_dw=m_b_dw, g_conv_ln=m_g_conv_ln, b_conv_ln=m_b_conv_ln, w_conv_out=m_w_conv_out, b_conv_out=m_b_conv_out, w_attn_out=m_w_attn_out, w_mix_out=m_w_mix_out, g_post_mix=m_g_post_mix, g_pre_ffn=m_g_pre_ffn, w_ffn_in=m_w_ffn_in, w_ffn_out=m_w_ffn_out, g_post_ffn=m_g_post_ffn)
    v = dict(rel_bias_table=v_rel_bias_table, g_pre_mix=v_g_pre_mix, w_in=v_w_in, b_glu=v_b_glu, w_dw=v_w_dw, b_dw=v_b_dw, g_conv_ln=v_g_conv_ln, b_conv_ln=v_b_conv_ln, w_conv_out=v_w_conv_out, b_conv_out=v_b_conv_out, w_attn_out=v_w_attn_out, w_mix_out=v_w_mix_out, g_post_mix=v_g_post_mix, g_pre_ffn=v_g_pre_ffn, w_ffn_in=v_w_ffn_in, w_ffn_out=v_w_ffn_out, g_post_ffn=v_g_post_ffn)

    def shard2d(d, n):
        a = d[n][0]
        return jnp.pad(a, ((0, HALO - CONV_W), (0, 0))) if n == 'w_dw' else a

    own = {n: shard2d(w, n).astype(F32 if n == 'w_dw' else BF) for n in BIG}
    packed = ['w_ffn_out', 'w_conv_out', 'w_mix_out', 'w_attn_out']
    alone = ['w_ffn_in', 'w_dw']
    shapes = [own[n].shape for n in packed]

    def pack(arrs, lead):
        return jnp.concatenate([a.reshape(lead + (-1, D)) for a in arrs], axis=len(lead))

    def unpack(p):
        out, pos = {}, 0
        for n, shp in zip(packed, shapes):
            rows = shp[0] * shp[1] // D
            out[n] = p[:, pos:pos + rows].reshape((NDEV,) + shp)
            pos += rows
        return out

    (g_in,) = _all_gather("gather_w_in", [own['w_in']])
    rest_own = [pack([own[n] for n in packed], ())] + [own[n] for n in alone]
    g_in, rest_own = lax.optimization_barrier((g_in, rest_own))
    gather_rest = _exchange_start("gather_rest_start", rest_own, True)
    W_in = _blocks_to_cols(g_in)
    kvq = [W_in[:, t * ATTN_W + g * GW:t * ATTN_W + (g + 1) * GW] for g in range(3) for t in (1, 2, 0)]
    Wp = jnp.concatenate([W_in[:, 3 * ATTN_W:]] + kvq, axis=1)

    def rest_fn(after):
        lands = _exchange_wait("gather_rest_wait", gather_rest, after, True)
        gw = unpack(_set_own_slot(lands[0], rest_own[0]))
        for n, l, o in zip(alone, lands[1:], rest_own[1:]):
            gw[n] = _set_own_slot(l, o)
        return (_blocks_to_cols(gw['w_ffn_in']), gw['w_ffn_out'].reshape(FFN, D), gw['w_conv_out'].reshape(D, D),
                gw['w_mix_out'].reshape(D, D), _blocks_to_cols(gw['w_attn_out']), _blocks_to_cols(gw['w_dw']))

    def to_blocks(n, g):
        if n in ('w_in', 'w_ffn_in', 'w_attn_out', 'w_dw'):
            return _cols_to_blocks(g)
        return g.reshape(NDEV, g.shape[0] // NDEV, g.shape[1])

    started = {}

    def early_fn(grads):
        blocks = [pack([to_blocks(n, grads[n]) for n in packed], (NDEV,))] + [to_blocks(n, grads[n]) for n in alone]
        started['blocks'] = blocks
        started['handle'] = _exchange_start("scatter_early_start", blocks, False)
        return started['handle'][4][0:1, 0:1]

    def late_fn(dW_in):
        started['in_blocks'] = [to_blocks('w_in', dW_in)]
        started['in_handle'] = _exchange_start("scatter_w_in_start", started['in_blocks'], False)
        return started['in_handle'][4][0:1, 0:1]

    g1_tied = g_pre_mix[0] + gather_rest[4][0, 0:1]
    loss_row, grad_x, small = _local_step(
        x[0], loss_target[0], Wp, rest_fn, early_fn, late_fn, rel_bias_table, g1_tied, b_glu[0], b_dw[0], g_conv_ln[0],
        b_conv_ln[0], b_conv_out[0], g_post_mix[0], g_pre_ffn[0], g_post_ffn[0])

    lands = _exchange_wait("scatter_early_wait", started['handle'], grad_x, False)
    lands = [_set_own_slot(l, _own_block(b)) for l, b in zip(lands, started['blocks'])]
    recv = unpack(lands[0])
    recv.update(zip(alone, lands[1:]))
    (land_in,) = _exchange_wait("scatter_w_in_wait", started['in_handle'], grad_x, False)
    recv['w_in'] = _set_own_slot(land_in, _own_block(started['in_blocks'][0]))
    tiles = dict(w_in=128, w_ffn_in=256, w_ffn_out=176, w_conv_out=128, w_mix_out=128, w_attn_out=512, w_dw=HALO)
    res = {}
    for n in BIG:
        g_, d_, nm_, nv_ = _adamw("adamw_" + n, shard2d(w, n), shard2d(m, n), shard2d(v, n), recv[n], tiles[n])
        if n == 'w_dw':
            g_, d_, nm_, nv_ = (t[:CONV_W] for t in (g_, d_, nm_, nv_))
        res[n] = tuple(t[None] for t in (g_, d_, nm_, nv_))

    tab = jnp.pad(small['rel_bias_table'], ((0, 0), (0, TAB_LANES - 3 * HEADS)))
    srows, stab = _all_gather("gather_small_grads", [_small_rows(small, loss_row), tab])
    loss11, small_res = _adamw_small(srows, stab, [w[n] for n in SMALL], [m[n] for n in SMALL], [v[n] for n in SMALL])
    loss = loss11.reshape(())
    for n, r in zip(SMALL, small_res):
        res[n] = r
    return (loss, grad_x[None], *[res[n][0] for n in WEIGHTS], *[res[n][1] for n in WEIGHTS],
            *[res[n][2] for n in WEIGHTS], *[res[n][3] for n in WEIGHTS])
```

```python
import functools
import math

import numpy as np
import jax
import jax.numpy as jnp
from jax import lax
from jax.experimental import pallas as pl
from jax.experimental.pallas import tpu as pltpu

F32 = jnp.float32
BF = jnp.bfloat16

D = 1024
HEAD_DIM = 64
HEADS = 8
GROUPS = ((128, 1), (512, 4), (2048, 16))
QBLK = 128
GW = HEADS * HEAD_DIM
ATTN_W = 3 * GW
REL_BUCKETS = 32
REL_MAX_DISTANCE = 2048
CONV_W = 31
HALO = 32
FFN = 2816
IN_W = 3 * ATTN_W + 2 * D + 2 * D
RMS_EPS = 1e-6
LN_EPS = 1e-5
NEG_INF = -1e30
SCALE = HEAD_DIM ** -0.5
NDEV = 8

ADAM_LR = 0.001
ADAM_B1 = 0.9
ADAM_B2 = 0.999
ADAM_EPS = 1e-08
ADAM_WD = 0.01
ADAM_STEP = 10

Z_G0 = 4096 // GW


def _kvq_blocks(g):
    return (Z_G0 + 3 * g, Z_G0 + 3 * g + 1, Z_G0 + 3 * g + 2)


WG_TK = 2048
VMEM_LIMIT = 52 * 1024 * 1024


def _cp(sem=None):
    if sem is None:
        return pltpu.CompilerParams(vmem_limit_bytes=VMEM_LIMIT)
    return pltpu.CompilerParams(vmem_limit_bytes=VMEM_LIMIT, dimension_semantics=sem)


def _sig(v):
    return jax.nn.sigmoid(v)


def _psum8(v):
    return v.reshape(v.shape[0] // 8, 8, v.shape[1]).sum(axis=0)


def _rms_r(v):
    return lax.rsqrt(jnp.mean(v * v, axis=-1, keepdims=True) + RMS_EPS)


def _rms_bwd(v, r, g, dy):
    gy = dy * g
    dv = r * gy - v * (r * r * r) * jnp.mean(v * gy, axis=-1, keepdims=True)
    return dv, dy * v * r


def _clip_k(k, k0, nk):
    return jnp.clip(k - k0, 0, nk - 1)


def _fused_mm(name, M, tm, grid_n, a_ops, b_ops, terms, acc_shapes, rows, consts, outs, parts, epilogue, n_outer=False):
    gm = M // tm
    nk_total = max([t[3] + t[4] for t in terms], default=1)
    n_a, n_b, n_r, n_c, n_o, n_p = len(a_ops), len(b_ops), len(rows), len(consts), len(outs), len(parts)
    n_acc = len(acc_shapes)
    use_scratch = nk_total > 1
    if parts:
        assert grid_n == 1

    def jj(j, follow):
        return j if follow else 0

    in_specs, args = [], []
    for (arr, tk, k0, nk) in a_ops:
        in_specs.append(pl.BlockSpec((tm, tk), functools.partial(lambda i, j, k, k0, nk: (i, _clip_k(k, k0, nk)), k0=k0, nk=nk)))
        args.append(arr)
    for (arr, nt, tk, tn, k0, nk, koff, joff, fj) in b_ops:
        if nt:
            in_specs.append(pl.BlockSpec((tn, tk), functools.partial(
                lambda i, j, k, k0, nk, koff, joff, fj: (joff + jj(j, fj), _clip_k(k, k0, nk) + koff),
                k0=k0, nk=nk, koff=koff, joff=joff, fj=fj)))
        else:
            in_specs.append(pl.BlockSpec((tk, tn), functools.partial(
                lambda i, j, k, k0, nk, koff, joff, fj: (_clip_k(k, k0, nk) + koff, joff + jj(j, fj)),
                k0=k0, nk=nk, koff=koff, joff=joff, fj=fj)))
        args.append(arr)
    for (arr, w, off, fj) in rows:
        in_specs.append(pl.BlockSpec((tm, w), functools.partial(lambda i, j, k, off, fj: (i, off + jj(j, fj)), off=off, fj=fj)))
        args.append(arr)
    for arr in consts:
        in_specs.append(pl.BlockSpec(arr.shape, functools.partial(lambda i, j, k, nd: (0,) * nd, nd=arr.ndim)))
        args.append(arr)
    out_specs, out_shape = [], []
    for (ncols, dt, w, off, fj) in outs:
        out_specs.append(pl.BlockSpec((tm, w), functools.partial(lambda i, j, k, off, fj: (i, off + jj(j, fj)), off=off, fj=fj)))
        out_shape.append(jax.ShapeDtypeStruct((M, ncols), dt))
    for (r, c) in parts:
        out_specs.append(pl.BlockSpec((r, c), lambda i, j, k: (0, 0)))
        out_shape.append(jax.ShapeDtypeStruct((r, c), F32))
    scratch = [pltpu.VMEM(s, F32) for s in acc_shapes] if use_scratch else []

    def body(*refs):
        pos = 0
        a_refs = refs[pos:pos + n_a]; pos += n_a
        b_refs = refs[pos:pos + n_b]; pos += n_b
        r_refs = refs[pos:pos + n_r]; pos += n_r
        c_refs = refs[pos:pos + n_c]; pos += n_c
        o_refs = refs[pos:pos + n_o]; pos += n_o
        p_refs = refs[pos:pos + n_p]; pos += n_p
        acc_refs = refs[pos:pos + n_acc] if use_scratch else ()
        i = pl.program_id(0)
        k = pl.program_id(2)

        def dot_of(ai, bi):
            a = a_refs[ai][...].astype(BF)
            b = b_refs[bi][...].astype(BF)
            if b_ops[bi][1]:
                return lax.dot_general(a, b, (((1,), (1,)), ((), ())), preferred_element_type=F32)
            return jnp.dot(a, b, preferred_element_type=F32)

        if parts:
            @pl.when((i == 0) & (k == 0))
            def _():
                for p in p_refs:
                    p[...] = jnp.zeros(p.shape, F32)

        def finish(accs):
            epilogue(accs, r_refs, c_refs, o_refs, p_refs)
            if parts:
                @pl.when(i == gm - 1)
                def _():
                    for p in p_refs:
                        p[0:1, :] = jnp.sum(p[...], axis=0, keepdims=True)

        if not use_scratch:
            accs = [None] * n_acc
            for (ai, bi, ci, k0, nk) in terms:
                d = dot_of(ai, bi)
                accs[ci] = d if accs[ci] is None else accs[ci] + d
            finish(accs)
        else:
            @pl.when(k == 0)
            def _():
                for acc in acc_refs:
                    acc[...] = jnp.zeros(acc.shape, F32)

            for (ai, bi, ci, k0, nk) in terms:
                def do(ai=ai, bi=bi, ci=ci):
                    acc_refs[ci][...] += dot_of(ai, bi)
                if k0 == 0 and nk == nk_total:
                    do()
                else:
                    pl.when((k >= k0) & (k < k0 + nk))(do)

            @pl.when(k == nk_total - 1)
            def _():
                finish([acc[...] for acc in acc_refs])

    grid = (gm, grid_n, nk_total)
    if n_outer:
        assert not parts
        swap = lambda spec: pl.BlockSpec(spec.block_shape, functools.partial(lambda j, i, k, f: f(i, j, k), f=spec.index_map))
        in_specs, out_specs, grid = [swap(sp) for sp in in_specs], [swap(sp) for sp in out_specs], (grid_n, gm, nk_total)
    res = pl.pallas_call(
        body, grid=grid, in_specs=in_specs, out_specs=out_specs, out_shape=out_shape,
        scratch_shapes=scratch, compiler_params=_cp(("arbitrary", "arbitrary", "arbitrary")), name=name,
    )(*args)
    return res


def _mm_tn(name, a, b, tm, tn, tk):
    S, Ka = a.shape
    Nb = b.shape[1]
    nk = S // tk

    def body(a_ref, b_ref, o_ref, acc):
        k = pl.program_id(2)

        @pl.when(k == 0)
        def _():
            acc[...] = jnp.zeros(acc.shape, F32)

        acc[...] += lax.dot_general(a_ref[...], b_ref[...], (((0,), (0,)), ((), ())), preferred_element_type=F32)

        @pl.when(k == nk - 1)
        def _():
            o_ref[...] = acc[...].astype(o_ref.dtype)

    return pl.pallas_call(
        body, grid=(Ka // tm, Nb // tn, nk),
        in_specs=[pl.BlockSpec((tk, tm), lambda i, j, k: (k, i)), pl.BlockSpec((tk, tn), lambda i, j, k: (k, j))],
        out_specs=pl.BlockSpec((tm, tn), lambda i, j, k: (i, j)),
        out_shape=jax.ShapeDtypeStruct((Ka, Nb), BF),
        scratch_shapes=[pltpu.VMEM((tm, tn), F32)],
        compiler_params=_cp(("parallel", "parallel", "arbitrary")), name=name,
    )(a, b)


def _resident_mm(name, M, tm, a_segs, w, rows, consts, outs, parts, epilogue):
    gm = M // tm
    n_a, n_r, n_c, n_o, n_p = len(a_segs), len(rows), len(consts), len(outs), len(parts)
    widths = [a.shape[1] for a in a_segs]
    offs = [sum(widths[:t]) for t in range(n_a)]
    once = pl.Buffered(1)

    def body(*refs):
        pos = 0
        a_refs = refs[pos:pos + n_a]; pos += n_a
        w_ref = refs[pos]; pos += 1
        r_refs = refs[pos:pos + n_r]; pos += n_r
        c_refs = refs[pos:pos + n_c]; pos += n_c
        o_refs = refs[pos:pos + n_o]; pos += n_o
        p_refs = refs[pos:pos + n_p]
        i = pl.program_id(0)
        if parts:
            @pl.when(i == 0)
            def _():
                for p in p_refs:
                    p[...] = jnp.zeros(p.shape, F32)
        acc = None
        for t in range(n_a):
            d = lax.dot_general(a_refs[t][...], w_ref[:, offs[t]:offs[t] + widths[t]], (((1,), (1,)), ((), ())),
                                preferred_element_type=F32)
            acc = d if acc is None else acc + d
        epilogue([acc], r_refs, c_refs, o_refs, p_refs)
        if parts:
            @pl.when(i == gm - 1)
            def _():
                for p in p_refs:
                    p[0:1, :] = jnp.sum(p[...], axis=0, keepdims=True)

    in_specs = [pl.BlockSpec((tm, wd), lambda i: (i, 0)) for wd in widths]
    in_specs.append(pl.BlockSpec(w.shape, lambda i: (0, 0), pipeline_mode=once))
    in_specs += [pl.BlockSpec((tm, c), lambda i: (i, 0)) for _, c in rows]
    in_specs += [pl.BlockSpec(c.shape, lambda i: (0, 0), pipeline_mode=once) for c in consts]
    out_specs = [pl.BlockSpec((tm, nc), lambda i: (i, 0)) for nc, _ in outs] + [pl.BlockSpec(pc, lambda i: (0, 0)) for pc in parts]
    out_shape = [jax.ShapeDtypeStruct((M, nc), dt) for nc, dt in outs] + [jax.ShapeDtypeStruct(pc, F32) for pc in parts]
    return pl.pallas_call(
        body, grid=(gm,), in_specs=in_specs, out_specs=out_specs, out_shape=out_shape,
        compiler_params=_cp(("arbitrary",)), name=name,
    )(*a_segs, w, *[r for r, _ in rows], *consts)


def _rel_bucket_np(dist):
    max_exact = REL_BUCKETS // 2
    d = np.maximum(dist, 0)
    df = np.maximum(d, 1).astype(np.float32)
    large = max_exact + (np.log(df / np.float32(max_exact)) / np.float32(math.log(REL_MAX_DISTANCE / max_exact))
                         * np.float32(REL_BUCKETS - max_exact)).astype(np.int32)
    large = np.minimum(large, REL_BUCKETS - 1)
    return np.where(d < max_exact, d, large).astype(np.int32)


N_LAYOUTS = 2


def _band_index():
    idx = np.zeros((N_LAYOUTS * 3, 1, QBLK * 2 * QBLK), np.int32)
    for g, (window, dil) in enumerate(GROUPS):
        span = window // dil
        k = np.arange(2 * QBLK)[:, None]; q = np.arange(QBLK)[None, :]
        off = q - k + QBLK
        idx[g, 0] = np.where((off >= 0) & (off <= span), _rel_bucket_np(off * dil), -1).reshape(-1)
        k = np.arange(QBLK)[:, None]; q = np.arange(2 * QBLK)[None, :]
        off = q - k
        idx[3 + g, 0] = np.where((off >= 0) & (off <= span), _rel_bucket_np(off * dil), -1).reshape(-1)
    return idx


_NB = QBLK * 2 * QBLK
_BCH = 4096


def _bias_build(tab_t, idx):
    def body(t_ref, i_ref, o_ref):
        ix = i_ref[0]
        t = t_ref[0]
        acc = jnp.full((HEADS, _BCH), NEG_INF, F32)
        for b in range(REL_BUCKETS):
            acc = jnp.where(ix == b, t[:, b:b + 1], acc)
        o_ref[0] = acc

    return pl.pallas_call(
        body, grid=(N_LAYOUTS * 3, _NB // _BCH),
        in_specs=[pl.BlockSpec((1, HEADS, REL_BUCKETS), lambda l, n: (l % 3, 0, 0)),
                  pl.BlockSpec((1, 1, _BCH), lambda l, n: (l, 0, n))],
        out_specs=pl.BlockSpec((1, HEADS, _BCH), lambda l, n: (l, 0, n)),
        out_shape=jax.ShapeDtypeStruct((N_LAYOUTS * 3, HEADS, _NB), F32), compiler_params=_cp(), name="bias_build",
    )(tab_t, idx)


def _bias_grad(ds, idx):
    nch = _NB // _BCH

    def body(d_ref, i_ref, o_ref):
        n = pl.program_id(1)

        @pl.when(n == 0)
        def _():
            o_ref[...] = jnp.zeros(o_ref.shape, F32)

        ix = i_ref[0]
        d = d_ref[0]
        lane = lax.broadcasted_iota(jnp.int32, (HEADS, 128), 1)
        acc = jnp.zeros((HEADS, 128), F32)
        for b in range(REL_BUCKETS):
            s = jnp.sum(jnp.where(ix == b, d, 0.0), axis=1, keepdims=True)
            acc = acc + jnp.where(lane == b, s, 0.0)
        o_ref[0] += acc

    return pl.pallas_call(
        body, grid=(3, nch),
        in_specs=[pl.BlockSpec((1, HEADS, _BCH), lambda l, n: (l, 0, n)),
                  pl.BlockSpec((1, 1, _BCH), lambda l, n: (3 + l, 0, n))],
        out_specs=pl.BlockSpec((1, HEADS, 128), lambda l, n: (l, 0, 0)),
        out_shape=jax.ShapeDtypeStruct((3, HEADS, 128), F32), compiler_params=_cp(), name="bias_grad",
    )(ds, idx)


PT = 256
PSTEP = 1024
STAT_W = 128


def _perm_np(dil):
    p = np.zeros((PT, PT), np.float32)
    m = np.arange(PT // dil)
    for c in range(dil):
        p[c * (PT // dil) + m, m * dil + c] = 1.0
    return p


def _perm_const(dil, dtype, inverse):
    p = _perm_np(dil)
    return jnp.asarray(p.T if inverse else p, dtype)


def _apply_perm(p, x):
    if x.dtype == F32:
        return jnp.dot(p, x, preferred_element_type=F32, precision=lax.Precision.HIGHEST)
    return jnp.dot(p, x, preferred_element_type=F32)


def _to_residue(name, arr, col_blocks, dil):
    S = arr.shape[0]
    nc = len(col_blocks)
    p = _perm_const(dil, arr.dtype, False)
    sub = PT // dil

    def body(*refs):
        p_ref, ins, o_ref = refs[0], refs[1:1 + nc], refs[1 + nc]
        for u in range(PSTEP // PT):
            for t, r in enumerate(ins):
                y = _apply_perm(p_ref[...], r[u * PT:(u + 1) * PT, :]).astype(o_ref.dtype)
                o_ref[:, u * sub:(u + 1) * sub, t * GW:(t + 1) * GW] = y.reshape(dil, sub, GW)

    out = pl.pallas_call(
        body, grid=(S // PSTEP,),
        in_specs=[pl.BlockSpec((PT, PT), lambda i: (0, 0))]
                 + [pl.BlockSpec((PSTEP, GW), functools.partial(lambda i, cb: (i, cb), cb=cb)) for cb in col_blocks],
        out_specs=pl.BlockSpec((dil, PSTEP // dil, nc * GW), lambda i: (0, i, 0)),
        out_shape=jax.ShapeDtypeStruct((dil, S // dil, nc * GW), arr.dtype), compiler_params=_cp(), name=name,
    )(p, *([arr] * nc))
    return out.reshape(S, nc * GW)


def _to_residue_stats(name, arr, dil):
    S = arr.shape[0]
    p = _perm_const(dil, F32, False)
    sub = PT // dil

    def body(p_ref, x_ref, o_ref):
        for u in range(PSTEP // PT):
            y = _apply_perm(p_ref[...], x_ref[u * PT:(u + 1) * PT, :])
            o_ref[:, u * sub:(u + 1) * sub, :] = y.reshape(dil, sub, STAT_W)

    out = pl.pallas_call(
        body, grid=(S // PSTEP,),
        in_specs=[pl.BlockSpec((PT, PT), lambda i: (0, 0)), pl.BlockSpec((PSTEP, STAT_W), lambda i: (i, 0))],
        out_specs=pl.BlockSpec((dil, PSTEP // dil, STAT_W), lambda i: (0, i, 0)),
        out_shape=jax.ShapeDtypeStruct((dil, S // dil, STAT_W), F32), compiler_params=_cp(), name=name,
    )(p, arr)
    return out.reshape(S, STAT_W)


def _from_residue(name, arr, dil):
    S, W = arr.shape
    p = _perm_const(dil, arr.dtype, True)
    sub = PT // dil

    def body(p_ref, x_ref, o_ref):
        for u in range(PSTEP // PT):
            x = x_ref[:, u * sub:(u + 1) * sub, :].reshape(PT, W)
            o_ref[u * PT:(u + 1) * PT, :] = _apply_perm(p_ref[...], x).astype(o_ref.dtype)

    return pl.pallas_call(
        body, grid=(S // PSTEP,),
        in_specs=[pl.BlockSpec((PT, PT), lambda i: (0, 0)), pl.BlockSpec((dil, PSTEP // dil, W), lambda i: (0, i, 0))],
        out_specs=pl.BlockSpec((PSTEP, W), lambda i: (i, 0)),
        out_shape=jax.ShapeDtypeStruct((S, W), arr.dtype), compiler_params=_cp(), name=name,
    )(p, arr.reshape(dil, S // dil, W))


PAIR_W = 2 * HEAD_DIM
NT_DIMS = (((1,), (1,)), ((), ()))
TN_DIMS = (((0,), (0,)), ((), ()))


def _attn_dims(S, dil):
    L = S // dil
    TQ = min(512, L)
    return L, TQ, L // TQ, TQ // QBLK


def _attn_specs(S, dil):
    L, TQ, nq, nsub = _attn_dims(S, dil)
    nb = L // QBLK
    cur = lambda cb, w=GW: pl.BlockSpec((TQ, w), lambda c, i: (c * nq + i, cb))
    prev = lambda cb, w=GW: pl.BlockSpec((QBLK, w), lambda c, i: (c * nb + jnp.maximum(i * nsub - 1, 0), cb))
    nxt = lambda cb, w=GW: pl.BlockSpec((QBLK, w), lambda c, i: (c * nb + jnp.minimum((i + 1) * nsub, nb - 1), cb))
    band = lambda r, c_: pl.BlockSpec((HEADS, r, c_), lambda c, i: (0, 0, 0))
    return L, TQ, nq, nsub, cur, prev, nxt, band


def _fill(buf, first_ref, second_ref):
    n = first_ref.shape[0]
    buf[0:n, :] = first_ref[...]
    buf[n:n + second_ref.shape[0], :] = second_ref[...]


def _attn_fwd(name, arr, bias_kq, cb, dil):
    S = arr.shape[0]
    kcb, vcb, qcb = cb
    L, TQ, nq, nsub, cur, prev, nxt, band = _attn_specs(S, dil)

    def body(q_ref, kc_ref, kp_ref, vc_ref, vp_ref, b_ref, o_ref, l_ref, kbuf, vbuf):
        i = pl.program_id(1)
        _fill(kbuf, kp_ref, kc_ref)
        _fill(vbuf, vp_ref, vc_ref)
        row = lax.broadcasted_iota(jnp.int32, (2 * QBLK, QBLK), 0)
        first = (row >= QBLK) | (i > 0)
        low = lax.broadcasted_iota(jnp.int32, (QBLK, PAIR_W), 1) < HEAD_DIM
        zero = jnp.zeros((QBLK, PAIR_W), BF)
        for j in range(nsub):
            rs = slice(j * QBLK, (j + 1) * QBLK)
            ks = slice(j * QBLK, (j + 2) * QBLK)
            lrows = []
            for hp in range(HEADS // 2):
                ps = slice(hp * PAIR_W, (hp + 1) * PAIR_W)
                qp = q_ref[rs, ps]
                kp = kbuf[ks, ps]
                vp = vbuf[ks, ps]
                halves = []
                for t in range(2):
                    qm = jnp.where(low if t == 0 else ~low, qp, zero)
                    s = lax.dot_general(kp, qm, NT_DIMS, preferred_element_type=F32) * SCALE + b_ref[2 * hp + t]
                    if j == 0:
                        s = jnp.where(first, s, NEG_INF)
                    m = jnp.max(s, axis=0, keepdims=True)
                    p = jnp.exp(s - m)
                    den = jnp.sum(p, axis=0, keepdims=True)
                    o2 = lax.dot_general(vp, p.astype(BF), TN_DIMS, preferred_element_type=F32)
                    halves.append(o2[t * HEAD_DIM:(t + 1) * HEAD_DIM, :] / den)
                    lrows.append(m + jnp.log(den))
                o_ref[rs, ps] = jnp.concatenate(halves, axis=0).T.astype(BF)
            lt = jnp.concatenate(lrows + [jnp.zeros((STAT_W - HEADS, QBLK), F32)], axis=0)
            l_ref[rs, :] = lt.T

    return pl.pallas_call(
        body, grid=(dil, nq),
        in_specs=[cur(qcb), cur(kcb), prev(kcb), cur(vcb), prev(vcb), band(2 * QBLK, QBLK)],
        out_specs=[cur(0), cur(0, STAT_W)],
        out_shape=[jax.ShapeDtypeStruct((S, GW), BF), jax.ShapeDtypeStruct((S, STAT_W), F32)],
        scratch_shapes=[pltpu.VMEM((QBLK + TQ, GW), BF), pltpu.VMEM((QBLK + TQ, GW), BF)],
        compiler_params=_cp(), name=name,
    )(arr, arr, arr, arr, arr, bias_kq)


def _attn_bwd(name, arr, bias_kq2, do, stats, cb, dil):
    S = arr.shape[0]
    kcb, vcb, qcb = cb
    L, TQ, nq, nsub, cur, prev, nxt, band = _attn_specs(S, dil)

    def body(k_ref, v_ref, qc_ref, qn_ref, b_ref, doc_ref, don_ref, sc_ref, sn_ref, o_ref, db_ref, qbuf, dobuf, sbuf, carry):
        c = pl.program_id(0)
        i = pl.program_id(1)

        @pl.when((c == 0) & (i == 0))
        def _():
            db_ref[...] = jnp.zeros(db_ref.shape, F32)
            carry[...] = jnp.zeros(carry.shape, F32)

        _fill(qbuf, qc_ref, qn_ref)
        _fill(dobuf, doc_ref, don_ref)
        for j in range(nsub + 1):
            rs = slice(j * QBLK, (j + 1) * QBLK)
            sbuf[:, rs] = (sc_ref[rs, :] if j < nsub else sn_ref[...]).T
        col = lax.broadcasted_iota(jnp.int32, (QBLK, 2 * QBLK), 1)
        last = (col < QBLK) | (i < nq - 1)
        low = lax.broadcasted_iota(jnp.int32, (QBLK, PAIR_W), 1) < HEAD_DIM
        zero = jnp.zeros((QBLK, PAIR_W), BF)
        for hp in range(HEADS // 2):
            ps = slice(hp * PAIR_W, (hp + 1) * PAIR_W)
            dbs = [jnp.zeros((QBLK, 2 * QBLK), F32), jnp.zeros((QBLK, 2 * QBLK), F32)]
            tail = carry[:, ps]
            for j in range(nsub):
                rs = slice(j * QBLK, (j + 1) * QBLK)
                qs = slice(j * QBLK, (j + 2) * QBLK)
                qp = qbuf[qs, ps]
                dd = dobuf[qs, ps]
                kp = k_ref[rs, ps]
                vp = v_ref[rs, ps]
                kt = kp.T
                dk, dv, dqt = [], [], []
                for t in range(2):
                    h = 2 * hp + t
                    sel = low if t == 0 else ~low
                    s = lax.dot_general(jnp.where(sel, kp, zero), qp, NT_DIMS, preferred_element_type=F32) * SCALE + b_ref[h]
                    if j == nsub - 1:
                        s = jnp.where(last, s, NEG_INF)
                    p = jnp.exp(s - sbuf[h:h + 1, qs])
                    dp = lax.dot_general(jnp.where(sel, vp, zero), dd, NT_DIMS, preferred_element_type=F32)
                    ds = p * (dp - sbuf[HEADS + h:HEADS + h + 1, qs])
                    dbs[t] = dbs[t] + ds
                    dsb = ds.astype(BF)
                    dk.append(jnp.dot(dsb, qp, preferred_element_type=F32))
                    dv.append(jnp.dot(p.astype(BF), dd, preferred_element_type=F32))
                    dqt.append(jnp.dot(kt[t * HEAD_DIM:(t + 1) * HEAD_DIM, :], dsb, preferred_element_type=F32))
                o_ref[rs, ps] = (jnp.where(low, dk[0], dk[1]) * SCALE).astype(BF)
                o_ref[rs, GW + hp * PAIR_W:GW + (hp + 1) * PAIR_W] = jnp.where(low, dv[0], dv[1]).astype(BF)
                dq2 = jnp.concatenate(dqt, axis=0).T * SCALE
                o_ref[rs, 2 * GW + hp * PAIR_W:2 * GW + (hp + 1) * PAIR_W] = (dq2[0:QBLK] + tail).astype(BF)
                tail = dq2[QBLK:2 * QBLK]
            carry[:, ps] = tail
            db_ref[2 * hp] += dbs[0]
            db_ref[2 * hp + 1] += dbs[1]

    return pl.pallas_call(
        body, grid=(dil, nq),
        in_specs=[cur(kcb), cur(vcb), cur(qcb), nxt(qcb), band(QBLK, 2 * QBLK),
                  cur(0), nxt(0), cur(0, STAT_W), nxt(0, STAT_W)],
        out_specs=[cur(0, ATTN_W), band(QBLK, 2 * QBLK)],
        out_shape=[jax.ShapeDtypeStruct((S, ATTN_W), BF), jax.ShapeDtypeStruct((HEADS, QBLK, 2 * QBLK), F32)],
        scratch_shapes=[pltpu.VMEM((TQ + QBLK, GW), BF), pltpu.VMEM((TQ + QBLK, GW), BF), pltpu.VMEM((STAT_W, TQ + QBLK), F32),
                        pltpu.VMEM((QBLK, GW), F32)],
        compiler_params=_cp(("arbitrary", "arbitrary")), name=name,
    )(arr, arr, arr, arr, bias_kq2, do, do, stats, stats)


def _head_expand():
    e = np.zeros((STAT_W, GW), np.float32)
    for h in range(HEADS):
        e[h, h * HEAD_DIM:(h + 1) * HEAD_DIM] = 1.0
    return e


def _attn_merge(os_, ls_, S):
    dils = [d for _, d in GROUPS]
    pb = [_perm_const(d, BF, True) for d in dils[1:]]
    pf = [_perm_const(d, F32, True) for d in dils[1:]]
    expand = jnp.asarray(_head_expand(), BF)

    def body(o0, o1, o2, l0, l1, l2, pb1, pb2, pf1, pf2, e_ref, o_ref, l_ref):
        for u in range(PSTEP // PT):
            rs = slice(u * PT, (u + 1) * PT)
            res = lambda r, d: r[:, u * (PT // d):(u + 1) * (PT // d), :].reshape(PT, r.shape[2])
            ov = [o0[rs, :].astype(F32), _apply_perm(pb1[...], res(o1, dils[1])), _apply_perm(pb2[...], res(o2, dils[2]))]
            lv = [l0[rs, :], _apply_perm(pf1[...], res(l1, dils[1])), _apply_perm(pf2[...], res(l2, dils[2]))]
            m = jnp.maximum(jnp.maximum(lv[0], lv[1]), lv[2])
            ev = [jnp.exp(l - m) for l in lv]
            den = ev[0] + ev[1] + ev[2]
            acc = jnp.zeros((PT, GW), F32)
            for g in range(3):
                wide = jnp.dot((ev[g] / den).astype(BF), e_ref[...], preferred_element_type=F32)
                acc = acc + wide * ov[g]
            o_ref[rs, :] = acc.astype(BF)
            l_ref[rs, :] = m + jnp.log(den)

    nat = lambda w: pl.BlockSpec((PSTEP, w), lambda i: (i, 0))
    res = lambda d, w: pl.BlockSpec((d, PSTEP // d, w), lambda i: (0, i, 0))
    cst = lambda a: pl.BlockSpec(a.shape, lambda i: (0, 0))
    args = [os_[0], os_[1].reshape(dils[1], S // dils[1], GW), os_[2].reshape(dils[2], S // dils[2], GW),
            ls_[0], ls_[1].reshape(dils[1], S // dils[1], STAT_W), ls_[2].reshape(dils[2], S // dils[2], STAT_W),
            pb[0], pb[1], pf[0], pf[1], expand]
    return pl.pallas_call(
        body, grid=(S // PSTEP,),
        in_specs=[nat(GW), res(dils[1], GW), res(dils[2], GW), nat(STAT_W), res(dils[1], STAT_W), res(dils[2], STAT_W)]
                 + [cst(a) for a in args[6:]],
        out_specs=[nat(GW), nat(STAT_W)],
        out_shape=[jax.ShapeDtypeStruct((S, GW), BF), jax.ShapeDtypeStruct((S, STAT_W), F32)],
        compiler_params=_cp(), name="attn_merge",
    )(*args)


CT = 256
CBUF = HALO + CT + 8
RG = 4


def _ln_hat(u1):
    mu = jnp.mean(u1, axis=-1, keepdims=True)
    xc = u1 - mu
    rstd = lax.rsqrt(jnp.mean(xc * xc, axis=-1, keepdims=True) + LN_EPS)
    return xc * rstd, rstd


def _glu_window(hu_ref, hg_ref, huh_ref, hgh_ref, bglu_ref, buf_ref, i):
    bu = bglu_ref[:, 0:D]
    bg = bglu_ref[:, D:2 * D]
    uh = (huh_ref[...].astype(F32) + bu) * _sig(hgh_ref[...].astype(F32) + bg)
    buf_ref[0:HALO, :] = jnp.where(i > 0, uh, 0.0)
    a = hu_ref[...].astype(F32) + bu
    s = _sig(hg_ref[...].astype(F32) + bg)
    buf_ref[HALO:HALO + CT, :] = a * s
    buf_ref[HALO + CT:CBUF, :] = jnp.zeros((8, D), F32)
    return a, s


def _shift_copies(buf_ref, sh_ref):
    for r in range(8):
        sh_ref[r] = buf_ref[r:r + HALO + CT, :]


def _tap_rows(wb_ref, w_ref):
    for j in range(CONV_W):
        wb_ref[j * 8:(j + 1) * 8, :] = jnp.broadcast_to(w_ref[j:j + 1, :], (8, D))


def _conv_taps(sh_ref, wb_ref, out_ref, init, offset):
    for rg in range(CT // (8 * RG)):
        accs = [init] * RG
        for j in range(CONV_W):
            off = offset(j)
            wj = wb_ref[j * 8:(j + 1) * 8, :]
            for q in range(RG):
                row = 8 * (rg * RG + q + off // 8)
                accs[q] = accs[q] + wj * sh_ref[off % 8, row:row + 8, :]
        for q in range(RG):
            out_ref[(rg * RG + q) * 8:(rg * RG + q + 1) * 8, :] = accs[q]


def _conv_specs(S):
    cur = lambda cb: pl.BlockSpec((CT, D), lambda i: (i, cb))
    halo = lambda cb: pl.BlockSpec((HALO, D), lambda i: (jnp.maximum(i * (CT // HALO) - 1, 0), cb))
    full = lambda shp: pl.BlockSpec(shp, lambda i: (0, 0))
    return cur, halo, full


def _conv_fwd(z, b_glu, w_dw, b_dw, g_ln, b_ln):
    S = z.shape[0]
    cur, halo, full = _conv_specs(S)

    def body(hu, hg, huh, hgh, bglu, w, bdw, gln, bln, u1_ref, u3_ref, buf, sh, wb):
        i = pl.program_id(0)

        @pl.when(i == 0)
        def _():
            _tap_rows(wb, w)

        _glu_window(hu, hg, huh, hgh, bglu, buf, i)
        _shift_copies(buf, sh)
        _conv_taps(sh, wb, u1_ref, jnp.broadcast_to(bdw[...], (8, D)), lambda j: 2 + j)
        xh, _ = _ln_hat(u1_ref[...])
        u2 = xh * gln[...] + bln[...]
        u3_ref[...] = (u2 * _sig(u2)).astype(BF)

    return pl.pallas_call(
        body, grid=(S // CT,),
        in_specs=[cur(0), cur(1), halo(0), halo(1), full((1, 2 * D)), full((HALO, D)), full((1, D)), full((1, D)), full((1, D))],
        out_specs=[pl.BlockSpec((CT, D), lambda i: (i, 0))] * 2,
        out_shape=[jax.ShapeDtypeStruct((S, D), F32), jax.ShapeDtypeStruct((S, D), BF)],
        scratch_shapes=[pltpu.VMEM((CBUF, D), F32), pltpu.VMEM((8, HALO + CT, D), F32), pltpu.VMEM((CONV_W * 8, D), F32)],
        compiler_params=_cp(("arbitrary",)), name="conv_fwd",
    )(z, z, z, z, b_glu, w_dw, b_dw, g_ln, b_ln)


def _conv_bwd(du1, z, b_glu, w_dw):
    S = z.shape[0]
    n = S // CT
    cur, halo, full = _conv_specs(S)

    def body(du, dun, hu, hg, huh, hgh, bglu, w, dz_ref, dw_ref, dbg_ref, bufu, bufd, shu, shd, wb, du0_ref, dwacc):
        i = pl.program_id(0)

        @pl.when(i == 0)
        def _():
            _tap_rows(wb, w)
            dwacc[...] = jnp.zeros(dwacc.shape, F32)
            dbg_ref[...] = jnp.zeros(dbg_ref.shape, F32)

        a, s = _glu_window(hu, hg, huh, hgh, bglu, bufu, i)
        bufd[0:CT, :] = du[...]
        bufd[CT:CT + HALO, :] = jnp.where(i < n - 1, dun[...], 0.0)
        bufd[CT + HALO:CBUF, :] = jnp.zeros((8, D), F32)
        _shift_copies(bufu, shu)
        _shift_copies(bufd, shd)
        _conv_taps(shd, wb, du0_ref, jnp.zeros((8, D), F32), lambda j: 30 - j)
        for rg in range(CT // (8 * RG)):
            dch = [bufd[(rg * RG + q) * 8:(rg * RG + q + 1) * 8, :] for q in range(RG)]
            for j in range(CONV_W):
                off = 2 + j
                acc = dwacc[j * 8:(j + 1) * 8, :]
                for q in range(RG):
                    row = 8 * (rg * RG + q + off // 8)
                    acc = acc + dch[q] * shu[off % 8, row:row + 8, :]
                dwacc[j * 8:(j + 1) * 8, :] = acc
        du0 = du0_ref[...]
        dhu = du0 * s
        dhg = du0 * a * s * (1.0 - s)
        dz_ref[:, 0:D] = dhu.astype(BF)
        dz_ref[:, D:2 * D] = dhg.astype(BF)
        dbg_ref[:, 0:D] += _psum8(dhu)
        dbg_ref[:, D:2 * D] += _psum8(dhg)

        @pl.when(i == n - 1)
        def _():
            dbg_ref[0:1, :] = jnp.sum(dbg_ref[...], axis=0, keepdims=True)
            for j in range(CONV_W):
                dw_ref[j:j + 1, :] = jnp.sum(dwacc[j * 8:(j + 1) * 8, :], axis=0, keepdims=True)
            dw_ref[CONV_W:HALO, :] = jnp.zeros((HALO - CONV_W, D), F32)

    nxt = pl.BlockSpec((HALO, D), lambda i: (jnp.minimum((i + 1) * (CT // HALO), S // HALO - 1), 0))
    return pl.pallas_call(
        body, grid=(n,),
        in_specs=[pl.BlockSpec((CT, D), lambda i: (i, 0)), nxt, cur(0), cur(1), halo(0), halo(1), full((1, 2 * D)), full((HALO, D))],
        out_specs=[pl.BlockSpec((CT, 2 * D), lambda i: (i, 0)), full((HALO, D)), full((8, 2 * D))],
        out_shape=[jax.ShapeDtypeStruct((S, 2 * D), BF), jax.ShapeDtypeStruct((HALO, D), F32), jax.ShapeDtypeStruct((8, 2 * D), F32)],
        scratch_shapes=[pltpu.VMEM((CBUF, D), F32), pltpu.VMEM((CBUF, D), F32), pltpu.VMEM((8, HALO + CT, D), F32),
                        pltpu.VMEM((8, HALO + CT, D), F32), pltpu.VMEM((CONV_W * 8, D), F32), pltpu.VMEM((CT, D), F32),
                        pltpu.VMEM((CONV_W * 8, D), F32)],
        compiler_params=_cp(("arbitrary",)), name="conv_bwd",
    )(du1, du1, z, z, z, z, b_glu, w_dw)


MESH = pl.DeviceIdType.MESH


def _all_gather(name, shards):
    n = len(shards)

    def body(*refs):
        ins, outs = refs[:n], refs[n:2 * n]
        send_sems, recv_sems, local_sems = refs[2 * n:]
        x, y, c = lax.axis_index("x"), lax.axis_index("y"), lax.axis_index("c")
        me, sibling = (x, y, c), (x, y, 1 - c)
        chips = [(1 - x, y), (x, 1 - y), (1 - x, 1 - y)]

        def slot(a, px, py, pc):
            return outs[a].at[4 * px + 2 * py + pc]

        def copy(a, k, block, to, src=None):
            return pltpu.make_async_remote_copy(
                src_ref=slot(a, *block) if src is None else src, dst_ref=slot(a, *block),
                send_sem=send_sems.at[a, k], recv_sem=recv_sems.at[a, k], device_id=to, device_id_type=MESH)

        mine = [pltpu.make_async_copy(ins[a], slot(a, *me), local_sems.at[a]) for a in range(n)]
        for cp in mine:
            cp.start()
        first = []
        for a in range(n):
            first.append(copy(a, 0, me, sibling, src=ins[a]))
            first += [copy(a, 1 + j, me, (*chip, c), src=ins[a]) for j, chip in enumerate(chips)]
        for cp in first:
            cp.start()
        passed = []
        for j, chip in enumerate(chips):
            for a in range(n):
                copy(a, 1 + j, (*chip, c), me).wait_recv()
                fwd = copy(a, 4 + j, (*chip, c), sibling)
                fwd.start()
                passed.append(fwd)
        for a in range(n):
            copy(a, 0, sibling, me).wait_recv()
        for j, chip in enumerate(chips):
            for a in range(n):
                copy(a, 4 + j, (*chip, 1 - c), me).wait_recv()
        for cp in first + passed:
            cp.wait_send()
        for cp in mine:
            cp.wait()

    anyspec = pl.BlockSpec(memory_space=pl.ANY)
    return pl.pallas_call(
        body, in_specs=[anyspec] * n, out_specs=[anyspec] * n,
        out_shape=[jax.ShapeDtypeStruct((NDEV,) + s.shape, s.dtype) for s in shards],
        scratch_shapes=[pltpu.SemaphoreType.DMA((n, 7)), pltpu.SemaphoreType.DMA((n, 7)), pltpu.SemaphoreType.DMA((n,))],
        name=name,
    )(*shards)


HBM_SPEC = pl.BlockSpec(memory_space=pltpu.HBM)
SEM_SPEC = pl.BlockSpec(memory_space=pltpu.SEMAPHORE)
DATAFLOW = pltpu.SideEffectType.DATAFLOW_SIDE_EFFECTING


def _peers():
    x, y, c = lax.axis_index("x"), lax.axis_index("y"), lax.axis_index("c")
    out = []
    for k in range(1, NDEV):
        px = 1 - x if k & 4 else x
        py = 1 - y if k & 2 else y
        pc = 1 - c if k & 1 else c
        out.append(((px, py, pc), 4 * px + 2 * py + pc))
    return 4 * x + 2 * y + c, out


def _exchange_copies(srcs, lands, send_sems, recv_sems, gather):
    my, peers = _peers()
    pairs = []
    for k, (dev, pid) in enumerate(peers):
        for a in range(len(srcs)):
            src = srcs[a] if gather else srcs[a].at[pid]
            sems = dict(send_sem=send_sems[a * (NDEV - 1) + k], recv_sem=recv_sems[a * (NDEV - 1) + k], device_id=dev,
                        device_id_type=MESH)
            pairs.append((pltpu.make_async_remote_copy(src_ref=src, dst_ref=lands[a].at[my], **sems),
                          pltpu.make_async_remote_copy(src_ref=src, dst_ref=lands[a].at[pid], **sems)))
    return pairs


def _exchange_start(name, srcs, gather):
    n = len(srcs)
    ns = n * (NDEV - 1)
    shapes = [(s.shape if gather else s.shape[1:]) for s in srcs]
    lands = [lax.empty((NDEV,) + shp, s.dtype) for shp, s in zip(shapes, srcs)]

    def body(*refs):
        src_refs, land_refs = refs[:n], refs[n:2 * n]
        send_sems, recv_sems = refs[2 * n:2 * n + ns], refs[2 * n + ns:2 * n + 2 * ns]
        token = refs[-1]
        for mine, _ in _exchange_copies(src_refs, land_refs, send_sems, recv_sems, gather):
            mine.start()
        token[...] = jnp.zeros(token.shape, token.dtype)

    hbm = lambda a: pltpu.HBM(a.shape, a.dtype)
    res = pl.pallas_call(
        body, name=name,
        out_shape=(*([pltpu.SemaphoreType.DMA(())] * (2 * ns)), *[hbm(s) for s in srcs], *[hbm(l) for l in lands],
                   jax.ShapeDtypeStruct((8, 128), F32)),
        in_specs=[HBM_SPEC] * (2 * n),
        out_specs=(*([SEM_SPEC] * (2 * ns)), *([HBM_SPEC] * (2 * n)), pl.BlockSpec(memory_space=pltpu.VMEM)),
        input_output_aliases={i: 2 * ns + i for i in range(2 * n)},
        compiler_params=pltpu.CompilerParams(has_side_effects=DATAFLOW),
    )(*[pltpu.with_memory_space_constraint(s, pltpu.HBM) for s in srcs],
      *[pltpu.with_memory_space_constraint(l, pltpu.HBM) for l in lands])
    return list(res[:ns]), list(res[ns:2 * ns]), list(res[2 * ns:2 * ns + n]), list(res[2 * ns + n:2 * ns + 2 * n]), res[-1]


def _exchange_wait(name, handle, after, gather):
    send_sems, recv_sems, srcs, lands, _ = handle
    n = len(srcs)
    ns = n * (NDEV - 1)

    def body(*refs):
        src_refs, land_refs = refs[:n], refs[n:2 * n]
        s_sems, r_sems = refs[2 * n:2 * n + ns], refs[2 * n + ns:2 * n + 2 * ns]
        for mine, theirs in _exchange_copies(src_refs, land_refs, s_sems, r_sems, gather):
            mine.wait_send()
            theirs.wait_recv()

    hbm = lambda a: pltpu.HBM(a.shape, a.dtype)
    res = pl.pallas_call(
        body, name=name,
        out_shape=(*[hbm(s) for s in srcs], *[hbm(l) for l in lands]),
        in_specs=[HBM_SPEC] * (2 * n) + [SEM_SPEC] * (2 * ns) + [pl.BlockSpec(memory_space=pl.ANY)],
        out_specs=tuple([HBM_SPEC] * (2 * n)),
        input_output_aliases={i: i for i in range(2 * n)},
        compiler_params=pltpu.CompilerParams(has_side_effects=DATAFLOW),
    )(*srcs, *lands, *send_sems, *recv_sems, after)
    return list(res[n:])


def _set_own_slot(land, own):
    my = 4 * lax.axis_index("x") + 2 * lax.axis_index("y") + lax.axis_index("c")
    return lax.dynamic_update_slice(land, own[None], (my, 0, 0))


def _own_block(blocks):
    my = 4 * lax.axis_index("x") + 2 * lax.axis_index("y") + lax.axis_index("c")
    return lax.dynamic_index_in_dim(blocks, my, axis=0, keepdims=False)


_C1 = 1.0 - ADAM_B1 ** ADAM_STEP
_C2 = 1.0 - ADAM_B2 ** ADAM_STEP


def _adamw(name, w, m, v, recv, tr):
    R, C = w.shape

    def body(w_ref, m_ref, v_ref, r_ref, g_ref, d_ref, nm_ref, nv_ref):
        g = r_ref[0].astype(F32)
        for s in range(1, NDEV):
            g = g + r_ref[s].astype(F32)
        wv = w_ref[...]
        nm = ADAM_B1 * m_ref[...] + (1.0 - ADAM_B1) * g
        nv = ADAM_B2 * v_ref[...] + (1.0 - ADAM_B2) * (g * g)
        m_hat = nm / _C1
        v_hat = nv / _C2
        g_ref[...] = g
        d_ref[...] = -ADAM_LR * (m_hat / (jnp.sqrt(v_hat) + ADAM_EPS) + ADAM_WD * wv)
        nm_ref[...] = nm
        nv_ref[...] = nv

    blk = pl.BlockSpec((tr, C), lambda i: (i, 0))
    return pl.pallas_call(
        body, grid=(R // tr,), in_specs=[blk, blk, blk, pl.BlockSpec((NDEV, tr, C), lambda i: (0, i, 0))],
        out_specs=[blk] * 4, out_shape=[jax.ShapeDtypeStruct((R, C), F32)] * 4,
        compiler_params=_cp(), name=name,
    )(w, m, v, recv)


def _row(v):
    return v.reshape(1, -1)


def _local_step(xs, tgt, Wp, rest_fn, early_fn, late_fn, rel_bias_table, g_pre_mix, b_glu, b_dw, g_conv_ln,
                b_conv_ln, b_conv_out, g_post_mix, g_pre_ffn, g_post_ffn):
    S = xs.shape[0]
    g1, g2, g3, g4 = _row(g_pre_mix), _row(g_post_mix), _row(g_pre_ffn), _row(g_post_ffn)
    bglu, bdw, gln, bln, bco = _row(b_glu), _row(b_dw), _row(g_conv_ln), _row(b_conv_ln), _row(b_conv_out)
    full = (D, F32, D, 0, False)
    fullb = (D, BF, D, 0, False)

    def epi_rms(accs, r, c, o, p):
        v = r[0][...]
        o[0][...] = (v * _rms_r(v) * c[0][...]).astype(BF)

    (h1,) = _fused_mm("rms_in", S, 512, 1, [], [], [], [], [(xs, D, 0, False)], [g1], [fullb], [], epi_rms)

    def epi_cast(accs, r, c, o, p):
        o[0][...] = accs[0].astype(BF)

    ZT = IN_W // 4
    (z,) = _fused_mm("in_proj", S, 1024, 4, [(h1, D, 0, 1)], [(Wp, False, D, ZT, 0, 1, 0, 0, True)], [(0, 0, 0, 0, 1)],
                     [(1024, ZT)], [], [], [(IN_W, BF, ZT, 0, True)], [], epi_cast, n_outer=True)

    idx = jnp.asarray(_band_index())
    tab_t = rel_bias_table.T.reshape(3, HEADS, REL_BUCKETS)
    bias_all = _bias_build(tab_t, idx)
    bias_kq = [bias_all[g].reshape(HEADS, 2 * QBLK, QBLK) for g in range(3)]
    bias_kq2 = [bias_all[3 + g].reshape(HEADS, QBLK, 2 * QBLK) for g in range(3)]
    dils = [d for _, d in GROUPS]
    qkv = [(z, _kvq_blocks(0))] + [(_to_residue(f"qkv_to_residue_g{g}", z, _kvq_blocks(g), dils[g]), (0, 1, 2)) for g in (1, 2)]
    os_, ls_ = [], []
    for g in range(3):
        o_g, l_g = _attn_fwd(f"attn_fwd_g{g}", qkv[g][0], bias_kq[g], qkv[g][1], dils[g])
        os_.append(o_g)
        ls_.append(l_g)
    o_att, lse = _attn_merge(os_, ls_, S)

    Wfi, Wfo, Wco, Wmo, Wao, wdw = rest_fn(lse)
    u1, u3 = _conv_fwd(z, bglu, wdw, bdw, gln, bln)

    def epi_mix(accs, r, c, o, p):
        ya = accs[0]
        yc = accs[1] + c[0][...]
        mg = _sig(r[0][...].astype(F32)) * ya + _sig(r[1][...].astype(F32)) * yc
        mgb = mg.astype(BF)
        m2 = jnp.dot(mgb, c[1][...], preferred_element_type=F32)
        x1 = r[2][...] + m2 * _rms_r(m2) * c[2][...]
        o[0][...] = ya.astype(BF)
        o[1][...] = yc.astype(BF)
        o[2][...] = mgb
        o[3][...] = m2.astype(BF)
        o[4][...] = x1
        o[5][...] = (x1 * _rms_r(x1) * c[3][...]).astype(BF)

    y_attn, y_conv, merged, m2, x1, h2 = _fused_mm(
        "mix_fwd", S, 512, 1, [(o_att, GW, 0, 1), (u3, D, 0, 1)],
        [(Wao, False, GW, D, 0, 1, 0, 0, False), (Wco, False, D, D, 0, 1, 0, 0, False)], [(0, 0, 0, 0, 1), (1, 1, 1, 0, 1)],
        [(512, D), (512, D)], [(z, D, 2, False), (z, D, 3, False), (xs, D, 0, False)], [bco, Wmo, g2, g3],
        [fullb, fullb, fullb, fullb, full, fullb], [], epi_mix)

    HN = FFN // 2

    def epi_ffn_in(accs, r, c, o, p):
        gt, up = accs
        o[0][...] = gt.astype(BF)
        o[1][...] = up.astype(BF)
        o[2][...] = (gt * _sig(gt) * up).astype(BF)

    gate, up, act = _fused_mm(
        "ffn_in", S, 512, 2, [(h2, D, 0, 1)],
        [(Wfi, False, D, HN, 0, 1, 0, 0, True), (Wfi, False, D, HN, 0, 1, 0, 2, True)], [(0, 0, 0, 0, 1), (0, 1, 1, 0, 1)],
        [(512, HN), (512, HN)], [], [], [(FFN, BF, HN, 0, True)] * 3, [], epi_ffn_in, n_outer=True)

    def epi_loss(accs, r, c, o, p):
        f2 = accs[0]
        g = c[0][...]
        rr = _rms_r(f2)
        err = r[0][...] + f2 * rr * g - r[1][...]
        dy = err * (1.0 / D)
        df2, dgr = _rms_bwd(f2, rr, g, dy)
        o[0][...] = dy
        o[1][...] = df2.astype(BF)
        p[0][...] += _psum8(err * err)
        p[1][...] += _psum8(dgr)

    dy, df2, loss_p, dg4 = _fused_mm(
        "ffn_out_loss", S, 512, 1, [(act, FFN, 0, 1)], [(Wfo, False, FFN, D, 0, 1, 0, 0, False)], [(0, 0, 0, 0, 1)],
        [(512, D)], [(x1, D, 0, False), (tgt, D, 0, False)], [g4], [full, fullb], [(8, D), (8, D)], epi_loss)

    def epi_swiglu(accs, r, c, o, p):
        da = accs[0]
        gt = r[0][...].astype(F32)
        sg = _sig(gt)
        o[0][...] = (da * r[1][...].astype(F32) * sg * (1.0 + gt * (1.0 - sg))).astype(BF)
        o[1][...] = (da * gt * sg).astype(BF)

    dgate, dup = _fused_mm(
        "ffn_out_bwd", S, 512, 2, [(df2, D, 0, 1)], [(Wfo, True, D, HN, 0, 1, 0, 0, True)], [(0, 0, 0, 0, 1)],
        [(512, HN)], [(gate, HN, 0, True), (up, HN, 0, True)], [], [(FFN, BF, HN, 0, True)] * 2, [], epi_swiglu, n_outer=True)
    dWfo = _mm_tn("dw_ffn_out", act, df2, HN, D, WG_TK)

    def epi_dh2(accs, r, c, o, p):
        dh2 = accs[0]
        x1v = r[1][...]
        r3 = _rms_r(x1v)
        d1, dg3r = _rms_bwd(x1v, r3, c[0][...], dh2)
        dx1 = r[0][...] + d1
        m2v = r[2][...].astype(F32)
        r2 = _rms_r(m2v)
        dm2, dg2r = _rms_bwd(m2v, r2, c[1][...], dx1)
        o[0][...] = dx1
        o[1][...] = dm2.astype(BF)
        p[0][...] += _psum8(dg3r)
        p[1][...] += _psum8(dg2r)

    dx1, dm2, dg3, dg2 = _resident_mm(
        "ffn_in_bwd", S, 512, [dgate, dup], Wfi, [(dy, D), (x1, D), (m2, D)], [g3, g2], [(D, F32), (D, BF)], [(8, D), (8, D)], epi_dh2)
    dWfi = jnp.concatenate([_mm_tn("dw_ffn_gate", h2, dgate, D, HN, WG_TK), _mm_tn("dw_ffn_up", h2, dup, D, HN, WG_TK)], axis=1)

    def epi_dmix(accs, r, c, o, p):
        dm = accs[0]
        sa = _sig(r[0][...].astype(F32))
        sc = _sig(r[1][...].astype(F32))
        o[0][...] = (dm * sa).astype(BF)
        o[1][...] = (dm * sc).astype(BF)
        o[2][:, 0:D] = (dm * r[2][...].astype(F32) * sa * (1.0 - sa)).astype(BF)
        o[2][:, D:2 * D] = (dm * r[3][...].astype(F32) * sc * (1.0 - sc)).astype(BF)

    dy_attn, dy_conv, dz_gate = _fused_mm(
        "mix_bwd", S, 512, 1, [(dm2, D, 0, 1)], [(Wmo, True, D, D, 0, 1, 0, 0, False)], [(0, 0, 0, 0, 1)], [(512, D)],
        [(z, D, 2, False), (z, D, 3, False), (y_attn, D, 0, False), (y_conv, D, 0, False)], [],
        [fullb, fullb, (2 * D, BF, 2 * D, 0, False)], [], epi_dmix)
    dWmo = _mm_tn("dw_mix_out", merged, dm2, D, D, WG_TK)

    def epi_dconv(accs, r, c, o, p):
        du3 = accs[0]
        xh, rstd = _ln_hat(r[0][...])
        gl = c[0][...]
        u2 = xh * gl + c[1][...]
        sg = _sig(u2)
        du2 = du3 * sg * (1.0 + u2 * (1.0 - sg))
        dxh = du2 * gl
        du1 = rstd * (dxh - jnp.mean(dxh, axis=-1, keepdims=True) - xh * jnp.mean(dxh * xh, axis=-1, keepdims=True))
        o[0][...] = du1
        p[0][...] += _psum8(du2 * xh)
        p[1][...] += _psum8(du2)
        p[2][...] += _psum8(du1)
        p[3][...] += _psum8(r[1][...].astype(F32))

    du1, dgln, dbln, dbdw, dbco = _fused_mm(
        "conv_out_bwd", S, 512, 1, [(dy_conv, D, 0, 1)], [(Wco, True, D, D, 0, 1, 0, 0, False)], [(0, 0, 0, 0, 1)], [(512, D)],
        [(u1, D, 0, False), (dy_conv, D, 0, False)], [gln, bln], [full], [(8, D)] * 4, epi_dconv)
    dWco = _mm_tn("dw_conv_out", u3, dy_conv, D, D, WG_TK)
    dz_glu, dwdw, dbglu = _conv_bwd(du1, z, bglu, wdw)

    head_sum = np.zeros((GW, STAT_W), np.float32)
    for h in range(HEADS):
        head_sum[h * HEAD_DIM:(h + 1) * HEAD_DIM, HEADS + h] = 1.0
    head_sum = jnp.asarray(head_sum)

    def epi_do(accs, r, c, o, p):
        do = accs[0]
        o[0][...] = do.astype(BF)
        delta = jnp.dot(do * r[0][...].astype(F32), c[0][...], preferred_element_type=F32, precision=lax.Precision.HIGHEST)
        lane = lax.broadcasted_iota(jnp.int32, delta.shape, 1)
        o[1][...] = jnp.where(lane < HEADS, r[1][...], delta)

    do, stats = _fused_mm(
        "attn_out_bwd", S, 1024, 1, [(dy_attn, D, 0, 1)], [(Wao, True, D, GW, 0, 1, 0, 0, False)], [(0, 0, 0, 0, 1)], [(1024, GW)],
        [(o_att, GW, 0, False), (lse, STAT_W, 0, False)], [head_sum], [(GW, BF, GW, 0, False), (STAT_W, F32, STAT_W, 0, False)], [], epi_do)
    dWao = _mm_tn("dw_attn_out", o_att, dy_attn, GW, D, WG_TK)

    tie = early_fn(dict(w_ffn_in=dWfi, w_ffn_out=dWfo, w_conv_out=dWco, w_mix_out=dWmo, w_attn_out=dWao, w_dw=dwdw))
    stats = stats + tie
    dos = [do] + [_to_residue(f"do_to_residue_g{g}", do, (0,), dils[g]) for g in (1, 2)]
    sts = [stats] + [_to_residue_stats(f"stats_to_residue_g{g}", stats, dils[g]) for g in (1, 2)]
    dqkv, dbs = [], []
    for g in range(3):
        arr, cb = qkv[g]
        dg, db = _attn_bwd(f"attn_bwd_g{g}", arr, bias_kq2[g], dos[g], sts[g], cb, dils[g])
        dqkv.append(dg if g == 0 else _from_residue(f"dqkv_from_residue_g{g}", dg, dils[g]))
        dbs.append(db.reshape(HEADS, _NB))
    dtab = _bias_grad(jnp.stack(dbs), idx)[:, :, :REL_BUCKETS].reshape(3 * HEADS, REL_BUCKETS).T

    def epi_dx(accs, r, c, o, p):
        xv = r[1][...]
        d1, dg1r = _rms_bwd(xv, _rms_r(xv), c[0][...], accs[0])
        o[0][...] = r[0][...] + d1
        p[0][...] += _psum8(dg1r)

    dWg = [_mm_tn(f"dw_in_g{g}", h1, dqkv[g], D, ATTN_W, WG_TK) for g in range(3)]
    dW_in = jnp.concatenate(
        [t[:, 2 * GW:] for t in dWg] + [t[:, :GW] for t in dWg] + [t[:, GW:2 * GW] for t in dWg]
        + [_mm_tn("dw_in_glu", h1, dz_glu, D, D, WG_TK), _mm_tn("dw_in_gate", h1, dz_gate, D, D, WG_TK)], axis=1)
    g1_late = g1 + late_fn(dW_in)
    grad_x, dg1 = _resident_mm(
        "in_proj_bwd", S, 512, [dz_glu, dz_gate, dqkv[0], dqkv[1], dqkv[2]], Wp, [(dx1, D), (xs, D)], [g1_late], [(D, F32)], [(8, D)], epi_dx)

    small = dict(rel_bias_table=dtab, g_pre_mix=dg1[0], b_glu=dbglu[0], b_dw=dbdw[0], g_conv_ln=dgln[0], b_conv_ln=dbln[0],
                 b_conv_out=dbco[0], g_post_mix=dg2[0], g_pre_ffn=dg3[0], g_post_ffn=dg4[0])
    return loss_p[0], grad_x, small


SMALL = ['rel_bias_table', 'g_pre_mix', 'b_glu', 'b_dw', 'g_conv_ln', 'b_conv_ln', 'b_conv_out', 'g_post_mix', 'g_pre_ffn',
         'g_post_ffn']
BIG = ['w_in', 'w_ffn_in', 'w_ffn_out', 'w_conv_out', 'w_mix_out', 'w_attn_out', 'w_dw']
WEIGHTS = ['rel_bias_table', 'g_pre_mix', 'w_in', 'b_glu', 'w_dw', 'b_dw', 'g_conv_ln', 'b_conv_ln', 'w_conv_out', 'b_conv_out',
           'w_attn_out', 'w_mix_out', 'g_post_mix', 'g_pre_ffn', 'w_ffn_in', 'w_ffn_out', 'g_post_ffn']
SMALL_ROWS = 16


ROW_SMALL = ['g_pre_mix', 'b_glu', 'b_dw', 'g_conv_ln', 'b_conv_ln', 'b_conv_out', 'g_post_mix', 'g_pre_ffn', 'g_post_ffn']
LOSS_ROW = 10
TAB_LANES = 128


def _small_rows(small, loss_row):
    rows = [small[n].reshape(-1, D) for n in ROW_SMALL] + [loss_row.reshape(1, D)]
    n = sum(r.shape[0] for r in rows)
    return jnp.concatenate(rows + [jnp.zeros((SMALL_ROWS - n, D), F32)], axis=0)


def _adamw_small(recv_rows, recv_tab, ws, ms, vs):
    np_ = len(SMALL)

    def body(*refs):
        rr, rt = refs[0], refs[1]
        w_refs, m_refs, v_refs = refs[2:2 + np_], refs[2 + np_:2 + 2 * np_], refs[2 + 2 * np_:2 + 3 * np_]
        loss_ref = refs[2 + 3 * np_]
        outs = refs[3 + 3 * np_:]
        rows = rr[0]
        tab = rt[0]
        for s_ in range(1, NDEV):
            rows = rows + rr[s_]
            tab = tab + rt[s_]
        loss_ref[...] = jnp.sum(rows[LOSS_ROW:LOSS_ROW + 1, :], axis=1, keepdims=True) * (0.5 / D)
        row = 0
        for p, n in enumerate(SMALL):
            if n == 'rel_bias_table':
                g = tab[:, 0:3 * HEADS]
            else:
                k = w_refs[p].shape[1] // D
                g = rows[row:row + 1, :] if k == 1 else jnp.concatenate([rows[row + t:row + t + 1, :] for t in range(k)], axis=1)
                row += k
            nm = ADAM_B1 * m_refs[p][...] + (1.0 - ADAM_B1) * g
            nv = ADAM_B2 * v_refs[p][...] + (1.0 - ADAM_B2) * (g * g)
            outs[4 * p][...] = g
            outs[4 * p + 1][...] = -ADAM_LR * ((nm / _C1) / (jnp.sqrt(nv / _C2) + ADAM_EPS) + ADAM_WD * w_refs[p][...])
            outs[4 * p + 2][...] = nm
            outs[4 * p + 3][...] = nv

    out_shape = [jax.ShapeDtypeStruct((1, 1), F32)]
    for a_ in ws:
        out_shape += [jax.ShapeDtypeStruct(a_.shape, F32)] * 4
    res = pl.pallas_call(body, out_shape=out_shape, compiler_params=_cp(), name="adamw_small")(recv_rows, recv_tab, *ws, *ms, *vs)
    return res[0], [tuple(res[1 + 4 * p:5 + 4 * p]) for p in range(np_)]


def _cols_to_blocks(a):
    R = a.shape[0]
    return a.reshape(R, NDEV, a.shape[1] // NDEV).transpose(1, 0, 2)


def _blocks_to_cols(a):
    return a.transpose(1, 0, 2).reshape(a.shape[1], NDEV * a.shape[2])


def kernel(x, rel_bias_table, g_pre_mix, w_in, b_glu, w_dw, b_dw, g_conv_ln, b_conv_ln, w_conv_out, b_conv_out, w_attn_out, w_mix_out, g_post_mix, g_pre_ffn, w_ffn_in, w_ffn_out, g_post_ffn, loss_target, m_rel_bias_table, m_g_pre_mix, m_w_in, m_b_glu, m_w_dw, m_b_dw, m_g_conv_ln, m_b_conv_ln, m_w_conv_out, m_b_conv_out, m_w_attn_out, m_w_mix_out, m_g_post_mix, m_g_pre_ffn, m_w_ffn_in, m_w_ffn_out, m_g_post_ffn, v_rel_bias_table, v_g_pre_mix, v_w_in, v_b_glu, v_w_dw, v_b_dw, v_g_conv_ln, v_b_conv_ln, v_w_conv_out, v_b_conv_out, v_w_attn_out, v_w_mix_out, v_g_post_mix, v_g_pre_ffn, v_w_ffn_in, v_w_ffn_out, v_g_post_ffn):
    w = dict(rel_bias_table=rel_bias_table, g_pre_mix=g_pre_mix, w_in=w_in, b_glu=b_glu, w_dw=w_dw, b_dw=b_dw, g_conv_ln=g_conv_ln, b_conv_ln=b_conv_ln, w_conv_out=w_conv_out, b_conv_out=b_conv_out, w_attn_out=w_attn_out, w_mix_out=w_mix_out, g_post_mix=g_post_mix, g_pre_ffn=g_pre_ffn, w_ffn_in=w_ffn_in, w_ffn_out=w_ffn_out, g_post_ffn=g_post_ffn)
    m = dict(rel_bias_table=m_rel_bias_table, g_pre_mix=m_g_pre_mix, w_in=m_w_in, b_glu=m_b_glu, w_dw=m_w_dw, b_dw=m_b_dw, g_conv_ln=m_g_conv_ln, b_conv_ln=m_b_conv_ln, w_conv_out=m_w_conv_out, b_conv_out=m_b_conv_out, w_attn_out=m_w_attn_out, w_mix_out=m_w_mix_out, g_post_mix=m_g_post_mix, g_pre_ffn=m_g_pre_ffn, w_ffn_in=m_w_ffn_in, w_ffn_out=m_w_ffn_out, g_post_ffn=m_g_post_ffn)
    v = dict(rel_bias_table=v_rel_bias_table, g_pre_mix=v_g_pre_mix, w_in=v_w_in, b_glu=v_b_glu, w_dw=v_w_dw, b_dw=v_b_dw, g_conv_ln=v_g_conv_ln, b_conv_ln=v_b_conv_ln, w_conv_out=v_w_conv_out, b_conv_out=v_b_conv_out, w_attn_out=v_w_attn_out, w_mix_out=v_w_mix_out, g_post_mix=v_g_post_mix, g_pre_ffn=v_g_pre_ffn, w_ffn_in=v_w_ffn_in, w_ffn_out=v_w_ffn_out, g_post_ffn=v_g_post_ffn)

    def shard2d(d, n):
        a = d[n][0]
        return jnp.pad(a, ((0, HALO - CONV_W), (0, 0))) if n == 'w_dw' else a

    own = {n: shard2d(w, n).astype(F32 if n == 'w_dw' else BF) for n in BIG}
    packed = ['w_ffn_out', 'w_conv_out', 'w_mix_out', 'w_attn_out']
    alone = ['w_ffn_in', 'w_dw']
    shapes = [own[n].shape for n in packed]

    def pack(arrs, lead):
        return jnp.concatenate([a.reshape(lead + (-1, D)) for a in arrs], axis=len(lead))

    def unpack(p):
        out, pos = {}, 0
        for n, shp in zip(packed, shapes):
            rows = shp[0] * shp[1] // D
            out[n] = p[:, pos:pos + rows].reshape((NDEV,) + shp)
            pos += rows
        return out

    (g_in,) = _all_gather("gather_w_in", [own['w_in']])
    rest_own = [pack([own[n] for n in packed], ())] + [own[n] for n in alone]
    g_in, rest_own = lax.optimization_barrier((g_in, rest_own))
    gather_rest = _exchange_start("gather_rest_start", rest_own, True)
    W_in = _blocks_to_cols(g_in)
    kvq = [W_in[:, t * ATTN_W + g * GW:t * ATTN_W + (g + 1) * GW] for g in range(3) for t in (1, 2, 0)]
    Wp = jnp.concatenate([W_in[:, 3 * ATTN_W:]] + kvq, axis=1)

    def rest_fn(after):
        lands = _exchange_wait("gather_rest_wait", gather_rest, after, True)
        gw = unpack(_set_own_slot(lands[0], rest_own[0]))
        for n, l, o in zip(alone, lands[1:], rest_own[1:]):
            gw[n] = _set_own_slot(l, o)
        return (_blocks_to_cols(gw['w_ffn_in']), gw['w_ffn_out'].reshape(FFN, D), gw['w_conv_out'].reshape(D, D),
                gw['w_mix_out'].reshape(D, D), _blocks_to_cols(gw['w_attn_out']), _blocks_to_cols(gw['w_dw']))

    def to_blocks(n, g):
        if n in ('w_in', 'w_ffn_in', 'w_attn_out', 'w_dw'):
            return _cols_to_blocks(g)
        return g.reshape(NDEV, g.shape[0] // NDEV, g.shape[1])

    started = {}

    def early_fn(grads):
        blocks = [pack([to_blocks(n, grads[n]) for n in packed], (NDEV,))] + [to_blocks(n, grads[n]) for n in alone]
        started['blocks'] = blocks
        started['handle'] = _exchange_start("scatter_early_start", blocks, False)
        return started['handle'][4][0:1, 0:1]

    def late_fn(dW_in):
        started['in_blocks'] = [to_blocks('w_in', dW_in)]
        started['in_handle'] = _exchange_start("scatter_w_in_start", started['in_blocks'], False)
        return started['in_handle'][4][0:1, 0:1]

    g1_tied = g_pre_mix[0] + gather_rest[4][0, 0:1]
    loss_row, grad_x, small = _local_step(
        x[0], loss_target[0], Wp, rest_fn, early_fn, late_fn, rel_bias_table, g1_tied, b_glu[0], b_dw[0], g_conv_ln[0],
        b_conv_ln[0], b_conv_out[0], g_post_mix[0], g_pre_ffn[0], g_post_ffn[0])

    lands = _exchange_wait("scatter_early_wait", started['handle'], grad_x, False)
    lands = [_set_own_slot(l, _own_block(b)) for l, b in zip(lands, started['blocks'])]
    recv = unpack(lands[0])
    recv.update(zip(alone, lands[1:]))
    (land_in,) = _exchange_wait("scatter_w_in_wait", started['in_handle'], grad_x, False)
    recv['w_in'] = _set_own_slot(land_in, _own_block(started['in_blocks'][0]))
    tiles = dict(w_in=128, w_ffn_in=256, w_ffn_out=176, w_conv_out=128, w_mix_out=128, w_attn_out=512, w_dw=HALO)
    res = {}
    for n in BIG:
        g_, d_, nm_, nv_ = _adamw("adamw_" + n, shard2d(w, n), shard2d(m, n), shard2d(v, n), recv[n], tiles[n])
        if n == 'w_dw':
            g_, d_, nm_, nv_ = (t[:CONV_W] for t in (g_, d_, nm_, nv_))
        res[n] = tuple(t[None] for t in (g_, d_, nm_, nv_))

    tab = jnp.pad(small['rel_bias_table'], ((0, 0), (0, TAB_LANES - 3 * HEADS)))
    srows, stab = _all_gather("gather_small_grads", [_small_rows(small, loss_row), tab])
    loss11, small_res = _adamw_small(srows, stab, [w[n] for n in SMALL], [m[n] for n in SMALL], [v[n] for n in SMALL])
    loss = loss11.reshape(())
    for n, r in zip(SMALL, small_res):
        res[n] = r
    return (loss, grad_x[None], *[res[n][0] for n in WEIGHTS], *[res[n][1] for n in WEIGHTS],
            *[res[n][2] for n in WEIGHTS], *[res[n][3] for n in WEIGHTS])
```

```python
import functools
import math

import numpy as np
import jax
import jax.numpy as jnp
from jax import lax
from jax.experimental import pallas as pl
from jax.experimental.pallas import tpu as pltpu

F32 = jnp.float32
BF = jnp.bfloat16

D = 1024
HEAD_DIM = 64
HEADS = 8
GROUPS = ((128, 1), (512, 4), (2048, 16))
QBLK = 128
GW = HEADS * HEAD_DIM
ATTN_W = 3 * GW
REL_BUCKETS = 32
REL_MAX_DISTANCE = 2048
CONV_W = 31
HALO = 32
FFN = 2816
IN_W = 3 * ATTN_W + 2 * D + 2 * D
RMS_EPS = 1e-6
LN_EPS = 1e-5
NEG_INF = -1e30
SCALE = HEAD_DIM ** -0.5
NDEV = 8

ADAM_LR = 0.001
ADAM_B1 = 0.9
ADAM_B2 = 0.999
ADAM_EPS = 1e-08
ADAM_WD = 0.01
ADAM_STEP = 10

Z_G0 = 4096 // GW


def _kvq_blocks(g):
    return (Z_G0 + 3 * g, Z_G0 + 3 * g + 1, Z_G0 + 3 * g + 2)


WG_TK = 2048
VMEM_LIMIT = 52 * 1024 * 1024


def _cp(sem=None):
    if sem is None:
        return pltpu.CompilerParams(vmem_limit_bytes=VMEM_LIMIT)
    return pltpu.CompilerParams(vmem_limit_bytes=VMEM_LIMIT, dimension_semantics=sem)


def _sig(v):
    return jax.nn.sigmoid(v)


def _psum8(v):
    return v.reshape(v.shape[0] // 8, 8, v.shape[1]).sum(axis=0)


def _rms_r(v):
    return lax.rsqrt(jnp.mean(v * v, axis=-1, keepdims=True) + RMS_EPS)


def _rms_bwd(v, r, g, dy):
    gy = dy * g
    dv = r * gy - v * (r * r * r) * jnp.mean(v * gy, axis=-1, keepdims=True)
    return dv, dy * v * r


def _clip_k(k, k0, nk):
    return jnp.clip(k - k0, 0, nk - 1)


def _fused_mm(name, M, tm, grid_n, a_ops, b_ops, terms, acc_shapes, rows, consts, outs, parts, epilogue, n_outer=False):
    gm = M // tm
    nk_total = max([t[3] + t[4] for t in terms], default=1)
    n_a, n_b, n_r, n_c, n_o, n_p = len(a_ops), len(b_ops), len(rows), len(consts), len(outs), len(parts)
    n_acc = len(acc_shapes)
    use_scratch = nk_total > 1
    if parts:
        assert grid_n == 1

    def jj(j, follow):
        return j if follow else 0

    in_specs, args = [], []
    for (arr, tk, k0, nk) in a_ops:
        in_specs.append(pl.BlockSpec((tm, tk), functools.partial(lambda i, j, k, k0, nk: (i, _clip_k(k, k0, nk)), k0=k0, nk=nk)))
        args.append(arr)
    for (arr, nt, tk, tn, k0, nk, koff, joff, fj) in b_ops:
        if nt:
            in_specs.append(pl.BlockSpec((tn, tk), functools.partial(
                lambda i, j, k, k0, nk, koff, joff, fj: (joff + jj(j, fj), _clip_k(k, k0, nk) + koff),
                k0=k0, nk=nk, koff=koff, joff=joff, fj=fj)))
        else:
            in_specs.append(pl.BlockSpec((tk, tn), functools.partial(
                lambda i, j, k, k0, nk, koff, joff, fj: (_clip_k(k, k0, nk) + koff, joff + jj(j, fj)),
                k0=k0, nk=nk, koff=koff, joff=joff, fj=fj)))
        args.append(arr)
    for (arr, w, off, fj) in rows:
        in_specs.append(pl.BlockSpec((tm, w), functools.partial(lambda i, j, k, off, fj: (i, off + jj(j, fj)), off=off, fj=fj)))
        args.append(arr)
    for arr in consts:
        in_specs.append(pl.BlockSpec(arr.shape, functools.partial(lambda i, j, k, nd: (0,) * nd, nd=arr.ndim)))
        args.append(arr)
    out_specs, out_shape = [], []
    for (ncols, dt, w, off, fj) in outs:
        out_specs.append(pl.BlockSpec((tm, w), functools.partial(lambda i, j, k, off, fj: (i, off + jj(j, fj)), off=off, fj=fj)))
        out_shape.append(jax.ShapeDtypeStruct((M, ncols), dt))
    for (r, c) in parts:
        out_specs.append(pl.BlockSpec((r, c), lambda i, j, k: (0, 0)))
        out_shape.append(jax.ShapeDtypeStruct((r, c), F32))
    scratch = [pltpu.VMEM(s, F32) for s in acc_shapes] if use_scratch else []

    def body(*refs):
        pos = 0
        a_refs = refs[pos:pos + n_a]; pos += n_a
        b_refs = refs[pos:pos + n_b]; pos += n_b
        r_refs = refs[pos:pos + n_r]; pos += n_r
        c_refs = refs[pos:pos + n_c]; pos += n_c
        o_refs = refs[pos:pos + n_o]; pos += n_o
        p_refs = refs[pos:pos + n_p]; pos += n_p
        acc_refs = refs[pos:pos + n_acc] if use_scratch else ()
        i = pl.program_id(0)
        k = pl.program_id(2)

        def dot_of(ai, bi):
            a = a_refs[ai][...].astype(BF)
            b = b_refs[bi][...].astype(BF)
            if b_ops[bi][1]:
                return lax.dot_general(a, b, (((1,), (1,)), ((), ())), preferred_element_type=F32)
            return jnp.dot(a, b, preferred_element_type=F32)

        if parts:
            @pl.when((i == 0) & (k == 0))
            def _():
                for p in p_refs:
                    p[...] = jnp.zeros(p.shape, F32)

        def finish(accs):
            epilogue(accs, r_refs, c_refs, o_refs, p_refs)
            if parts:
                @pl.when(i == gm - 1)
                def _():
                    for p in p_refs:
                        p[0:1, :] = jnp.sum(p[...], axis=0, keepdims=True)

        if not use_scratch:
            accs = [None] * n_acc
            for (ai, bi, ci, k0, nk) in terms:
                d = dot_of(ai, bi)
                accs[ci] = d if accs[ci] is None else accs[ci] + d
            finish(accs)
        else:
            @pl.when(k == 0)
            def _():
                for acc in acc_refs:
                    acc[...] = jnp.zeros(acc.shape, F32)

            for (ai, bi, ci, k0, nk) in terms:
                def do(ai=ai, bi=bi, ci=ci):
                    acc_refs[ci][...] += dot_of(ai, bi)
                if k0 == 0 and nk == nk_total:
                    do()
                else:
                    pl.when((k >= k0) & (k < k0 + nk))(do)

            @pl.when(k == nk_total - 1)
            def _():
                finish([acc[...] for acc in acc_refs])

    grid = (gm, grid_n, nk_total)
    if n_outer:
        assert not parts
        swap = lambda spec: pl.BlockSpec(spec.block_shape, functools.partial(lambda j, i, k, f: f(i, j, k), f=spec.index_map))
        in_specs, out_specs, grid = [swap(sp) for sp in in_specs], [swap(sp) for sp in out_specs], (grid_n, gm, nk_total)
    res = pl.pallas_call(
        body, grid=grid, in_specs=in_specs, out_specs=out_specs, out_shape=out_shape,
        scratch_shapes=scratch, compiler_params=_cp(("arbitrary", "arbitrary", "arbitrary")), name=name,
    )(*args)
    return res


def _mm_tn(name, a, b, tm, tn, tk):
    S, Ka = a.shape
    Nb = b.shape[1]
    nk = S // tk

    def body(a_ref, b_ref, o_ref, acc):
        k = pl.program_id(2)

        @pl.when(k == 0)
        def _():
            acc[...] = jnp.zeros(acc.shape, F32)

        acc[...] += lax.dot_general(a_ref[...], b_ref[...], (((0,), (0,)), ((), ())), preferred_element_type=F32)

        @pl.when(k == nk - 1)
        def _():
            o_ref[...] = acc[...].astype(o_ref.dtype)

    return pl.pallas_call(
        body, grid=(Ka // tm, Nb // tn, nk),
        in_specs=[pl.BlockSpec((tk, tm), lambda i, j, k: (k, i)), pl.BlockSpec((tk, tn), lambda i, j, k: (k, j))],
        out_specs=pl.BlockSpec((tm, tn), lambda i, j, k: (i, j)),
        out_shape=jax.ShapeDtypeStruct((Ka, Nb), BF),
        scratch_shapes=[pltpu.VMEM((tm, tn), F32)],
        compiler_params=_cp(("parallel", "parallel", "arbitrary")), name=name,
    )(a, b)


def _resident_mm(name, M, tm, a_segs, w, rows, consts, outs, parts, epilogue):
    gm = M // tm
    n_a, n_r, n_c, n_o, n_p = len(a_segs), len(rows), len(consts), len(outs), len(parts)
    widths = [a.shape[1] for a in a_segs]
    offs = [sum(widths[:t]) for t in range(n_a)]
    once = pl.Buffered(1)

    def body(*refs):
        pos = 0
        a_refs = refs[pos:pos + n_a]; pos += n_a
        w_ref = refs[pos]; pos += 1
        r_refs = refs[pos:pos + n_r]; pos += n_r
        c_refs = refs[pos:pos + n_c]; pos += n_c
        o_refs = refs[pos:pos + n_o]; pos += n_o
        p_refs = refs[pos:pos + n_p]
        i = pl.program_id(0)
        if parts:
            @pl.when(i == 0)
            def _():
                for p in p_refs:
                    p[...] = jnp.zeros(p.shape, F32)
        acc = None
        for t in range(n_a):
            d = lax.dot_general(a_refs[t][...], w_ref[:, offs[t]:offs[t] + widths[t]], (((1,), (1,)), ((), ())),
                                preferred_element_type=F32)
            acc = d if acc is None else acc + d
        epilogue([acc], r_refs, c_refs, o_refs, p_refs)
        if parts:
            @pl.when(i == gm - 1)
            def _():
                for p in p_refs:
                    p[0:1, :] = jnp.sum(p[...], axis=0, keepdims=True)

    in_specs = [pl.BlockSpec((tm, wd), lambda i: (i, 0)) for wd in widths]
    in_specs.append(pl.BlockSpec(w.shape, lambda i: (0, 0), pipeline_mode=once))
    in_specs += [pl.BlockSpec((tm, c), lambda i: (i, 0)) for _, c in rows]
    in_specs += [pl.BlockSpec(c.shape, lambda i: (0, 0), pipeline_mode=once) for c in consts]
    out_specs = [pl.BlockSpec((tm, nc), lambda i: (i, 0)) for nc, _ in outs] + [pl.BlockSpec(pc, lambda i: (0, 0)) for pc in parts]
    out_shape = [jax.ShapeDtypeStruct((M, nc), dt) for nc, dt in outs] + [jax.ShapeDtypeStruct(pc, F32) for pc in parts]
    return pl.pallas_call(
        body, grid=(gm,), in_specs=in_specs, out_specs=out_specs, out_shape=out_shape,
        compiler_params=_cp(("arbitrary",)), name=name,
    )(*a_segs, w, *[r for r, _ in rows], *consts)


def _rel_bucket_np(dist):
    max_exact = REL_BUCKETS // 2
    d = np.maximum(dist, 0)
    df = np.maximum(d, 1).astype(np.float32)
    large = max_exact + (np.log(df / np.float32(max_exact)) / np.float32(math.log(REL_MAX_DISTANCE / max_exact))
                         * np.float32(REL_BUCKETS - max_exact)).astype(np.int32)
    large = np.minimum(large, REL_BUCKETS - 1)
    return np.where(d < max_exact, d, large).astype(np.int32)


N_LAYOUTS = 2


def _band_index():
    idx = np.zeros((N_LAYOUTS * 3, 1, QBLK * 2 * QBLK), np.int32)
    for g, (window, dil) in enumerate(GROUPS):
        span = window // dil
        k = np.arange(2 * QBLK)[:, None]; q = np.arange(QBLK)[None, :]
        off = q - k + QBLK
        idx[g, 0] = np.where((off >= 0) & (off <= span), _rel_bucket_np(off * dil), -1).reshape(-1)
        k = np.arange(QBLK)[:, None]; q = np.arange(2 * QBLK)[None, :]
        off = q - k
        idx[3 + g, 0] = np.where((off >= 0) & (off <= span), _rel_bucket_np(off * dil), -1).reshape(-1)
    return idx


_NB = QBLK * 2 * QBLK
_BCH = 4096


def _bias_build(tab_t, idx):
    def body(t_ref, i_ref, o_ref):
        ix = i_ref[0]
        t = t_ref[0]
        acc = jnp.full((HEADS, _BCH), NEG_INF, F32)
        for b in range(REL_BUCKETS):
            acc = jnp.where(ix == b, t[:, b:b + 1], acc)
        o_ref[0] = acc

    return pl.pallas_call(
        body, grid=(N_LAYOUTS * 3, _NB // _BCH),
        in_specs=[pl.BlockSpec((1, HEADS, REL_BUCKETS), lambda l, n: (l % 3, 0, 0)),
                  pl.BlockSpec((1, 1, _BCH), lambda l, n: (l, 0, n))],
        out_specs=pl.BlockSpec((1, HEADS, _BCH), lambda l, n: (l, 0, n)),
        out_shape=jax.ShapeDtypeStruct((N_LAYOUTS * 3, HEADS, _NB), F32), compiler_params=_cp(), name="bias_build",
    )(tab_t, idx)


def _bias_grad(ds, idx):
    nch = _NB // _BCH

    def body(d_ref, i_ref, o_ref):
        n = pl.program_id(1)

        @pl.when(n == 0)
        def _():
            o_ref[...] = jnp.zeros(o_ref.shape, F32)

        ix = i_ref[0]
        d = d_ref[0]
        lane = lax.broadcasted_iota(jnp.int32, (HEADS, 128), 1)
        acc = jnp.zeros((HEADS, 128), F32)
        for b in range(REL_BUCKETS):
            s = jnp.sum(jnp.where(ix == b, d, 0.0), axis=1, keepdims=True)
            acc = acc + jnp.where(lane == b, s, 0.0)
        o_ref[0] += acc

    return pl.pallas_call(
        body, grid=(3, nch),
        in_specs=[pl.BlockSpec((1, HEADS, _BCH), lambda l, n: (l, 0, n)),
                  pl.BlockSpec((1, 1, _BCH), lambda l, n: (3 + l, 0, n))],
        out_specs=pl.BlockSpec((1, HEADS, 128), lambda l, n: (l, 0, 0)),
        out_shape=jax.ShapeDtypeStruct((3, HEADS, 128), F32), compiler_params=_cp(), name="bias_grad",
    )(ds, idx)


PT = 256
PSTEP = 1024
STAT_W = 128


def _perm_np(dil):
    p = np.zeros((PT, PT), np.float32)
    m = np.arange(PT // dil)
    for c in range(dil):
        p[c * (PT // dil) + m, m * dil + c] = 1.0
    return p


def _perm_const(dil, dtype, inverse):
    p = _perm_np(dil)
    return jnp.asarray(p.T if inverse else p, dtype)


def _apply_perm(p, x):
    if x.dtype == F32:
        return jnp.dot(p, x, preferred_element_type=F32, precision=lax.Precision.HIGHEST)
    return jnp.dot(p, x, preferred_element_type=F32)


def _to_residue(name, arr, col_blocks, dil):
    S = arr.shape[0]
    nc = len(col_blocks)
    p = _perm_const(dil, arr.dtype, False)
    sub = PT // dil

    def body(*refs):
        p_ref, ins, o_ref = refs[0], refs[1:1 + nc], refs[1 + nc]
        for u in range(PSTEP // PT):
            for t, r in enumerate(ins):
                y = _apply_perm(p_ref[...], r[u * PT:(u + 1) * PT, :]).astype(o_ref.dtype)
                o_ref[:, u * sub:(u + 1) * sub, t * GW:(t + 1) * GW] = y.reshape(dil, sub, GW)

    out = pl.pallas_call(
        body, grid=(S // PSTEP,),
        in_specs=[pl.BlockSpec((PT, PT), lambda i: (0, 0))]
                 + [pl.BlockSpec((PSTEP, GW), functools.partial(lambda i, cb: (i, cb), cb=cb)) for cb in col_blocks],
        out_specs=pl.BlockSpec((dil, PSTEP // dil, nc * GW), lambda i: (0, i, 0)),
        out_shape=jax.ShapeDtypeStruct((dil, S // dil, nc * GW), arr.dtype), compiler_params=_cp(), name=name,
    )(p, *([arr] * nc))
    return out.reshape(S, nc * GW)


def _to_residue_stats(name, arr, dil):
    S = arr.shape[0]
    p = _perm_const(dil, F32, False)
    sub = PT // dil

    def body(p_ref, x_ref, o_ref):
        for u in range(PSTEP // PT):
            y = _apply_perm(p_ref[...], x_ref[u * PT:(u + 1) * PT, :])
            o_ref[:, u * sub:(u + 1) * sub, :] = y.reshape(dil, sub, STAT_W)

    out = pl.pallas_call(
        body, grid=(S // PSTEP,),
        in_specs=[pl.BlockSpec((PT, PT), lambda i: (0, 0)), pl.BlockSpec((PSTEP, STAT_W), lambda i: (i, 0))],
        out_specs=pl.BlockSpec((dil, PSTEP // dil, STAT_W), lambda i: (0, i, 0)),
        out_shape=jax.ShapeDtypeStruct((dil, S // dil, STAT_W), F32), compiler_params=_cp(), name=name,
    )(p, arr)
    return out.reshape(S, STAT_W)


def _from_residue(name, arr, dil):
    S, W = arr.shape
    p = _perm_const(dil, arr.dtype, True)
    sub = PT // dil

    def body(p_ref, x_ref, o_ref):
        for u in range(PSTEP // PT):
            x = x_ref[:, u * sub:(u + 1) * sub, :].reshape(PT, W)
            o_ref[u * PT:(u + 1) * PT, :] = _apply_perm(p_ref[...], x).astype(o_ref.dtype)

    return pl.pallas_call(
        body, grid=(S // PSTEP,),
        in_specs=[pl.BlockSpec((PT, PT), lambda i: (0, 0)), pl.BlockSpec((dil, PSTEP // dil, W), lambda i: (0, i, 0))],
        out_specs=pl.BlockSpec((PSTEP, W), lambda i: (i, 0)),
        out_shape=jax.ShapeDtypeStruct((S, W), arr.dtype), compiler_params=_cp(), name=name,
    )(p, arr.reshape(dil, S // dil, W))


PAIR_W = 2 * HEAD_DIM
NT_DIMS = (((1,), (1,)), ((), ()))
TN_DIMS = (((0,), (0,)), ((), ()))


def _attn_dims(S, dil):
    L = S // dil
    TQ = min(512, L)
    return L, TQ, L // TQ, TQ // QBLK


def _attn_specs(S, dil):
    L, TQ, nq, nsub = _attn_dims(S, dil)
    nb = L // QBLK
    cur = lambda cb, w=GW: pl.BlockSpec((TQ, w), lambda c, i: (c * nq + i, cb))
    prev = lambda cb, w=GW: pl.BlockSpec((QBLK, w), lambda c, i: (c * nb + jnp.maximum(i * nsub - 1, 0), cb))
    nxt = lambda cb, w=GW: pl.BlockSpec((QBLK, w), lambda c, i: (c * nb + jnp.minimum((i + 1) * nsub, nb - 1), cb))
    band = lambda r, c_: pl.BlockSpec((HEADS, r, c_), lambda c, i: (0, 0, 0))
    return L, TQ, nq, nsub, cur, prev, nxt, band


def _fill(buf, first_ref, second_ref):
    n = first_ref.shape[0]
    buf[0:n, :] = first_ref[...]
    buf[n:n + second_ref.shape[0], :] = second_ref[...]


def _attn_fwd(name, arr, bias_kq, cb, dil):
    S = arr.shape[0]
    kcb, vcb, qcb = cb
    L, TQ, nq, nsub, cur, prev, nxt, band = _attn_specs(S, dil)

    def body(q_ref, kc_ref, kp_ref, vc_ref, vp_ref, b_ref, o_ref, l_ref, kbuf, vbuf):
        i = pl.program_id(1)
        _fill(kbuf, kp_ref, kc_ref)
        _fill(vbuf, vp_ref, vc_ref)
        row = lax.broadcasted_iota(jnp.int32, (2 * QBLK, QBLK), 0)
        first = (row >= QBLK) | (i > 0)
        low = lax.broadcasted_iota(jnp.int32, (QBLK, PAIR_W), 1) < HEAD_DIM
        zero = jnp.zeros((QBLK, PAIR_W), BF)
        for j in range(nsub):
            rs = slice(j * QBLK, (j + 1) * QBLK)
            ks = slice(j * QBLK, (j + 2) * QBLK)
            lrows = []
            for hp in range(HEADS // 2):
                ps = slice(hp * PAIR_W, (hp + 1) * PAIR_W)
                qp = q_ref[rs, ps]
                kp = kbuf[ks, ps]
                vp = vbuf[ks, ps]
                halves = []
                for t in range(2):
                    qm = jnp.where(low if t == 0 else ~low, qp, zero)
                    s = lax.dot_general(kp, qm, NT_DIMS, preferred_element_type=F32) * SCALE + b_ref[2 * hp + t]
                    if j == 0:
                        s = jnp.where(first, s, NEG_INF)
                    m = jnp.max(s, axis=0, keepdims=True)
                    p = jnp.exp(s - m)
                    den = jnp.sum(p, axis=0, keepdims=True)
                    o2 = lax.dot_general(vp, p.astype(BF), TN_DIMS, preferred_element_type=F32)
                    halves.append(o2[t * HEAD_DIM:(t + 1) * HEAD_DIM, :] / den)
                    lrows.append(m + jnp.log(den))
                o_ref[rs, ps] = jnp.concatenate(halves, axis=0).T.astype(BF)
            lt = jnp.concatenate(lrows + [jnp.zeros((STAT_W - HEADS, QBLK), F32)], axis=0)
            l_ref[rs, :] = lt.T

    return pl.pallas_call(
        body, grid=(dil, nq),
        in_specs=[cur(qcb), cur(kcb), prev(kcb), cur(vcb), prev(vcb), band(2 * QBLK, QBLK)],
        out_specs=[cur(0), cur(0, STAT_W)],
        out_shape=[jax.ShapeDtypeStruct((S, GW), BF), jax.ShapeDtypeStruct((S, STAT_W), F32)],
        scratch_shapes=[pltpu.VMEM((QBLK + TQ, GW), BF), pltpu.VMEM((QBLK + TQ, GW), BF)],
        compiler_params=_cp(), name=name,
    )(arr, arr, arr, arr, arr, bias_kq)


def _attn_bwd(name, arr, bias_kq2, do, stats, cb, dil):
    S = arr.shape[0]
    kcb, vcb, qcb = cb
    L, TQ, nq, nsub, cur, prev, nxt, band = _attn_specs(S, dil)

    def body(k_ref, v_ref, qc_ref, qn_ref, b_ref, doc_ref, don_ref, sc_ref, sn_ref, o_ref, db_ref, qbuf, dobuf, sbuf, carry):
        c = pl.program_id(0)
        i = pl.program_id(1)

        @pl.when((c == 0) & (i == 0))
        def _():
            db_ref[...] = jnp.zeros(db_ref.shape, F32)
            carry[...] = jnp.zeros(carry.shape, F32)

        _fill(qbuf, qc_ref, qn_ref)
        _fill(dobuf, doc_ref, don_ref)
        for j in range(nsub + 1):
            rs = slice(j * QBLK, (j + 1) * QBLK)
            sbuf[:, rs] = (sc_ref[rs, :] if j < nsub else sn_ref[...]).T
        col = lax.broadcasted_iota(jnp.int32, (QBLK, 2 * QBLK), 1)
        last = (col < QBLK) | (i < nq - 1)
        low = lax.broadcasted_iota(jnp.int32, (QBLK, PAIR_W), 1) < HEAD_DIM
        zero = jnp.zeros((QBLK, PAIR_W), BF)
        for hp in range(HEADS // 2):
            ps = slice(hp * PAIR_W, (hp + 1) * PAIR_W)
            dbs = [jnp.zeros((QBLK, 2 * QBLK), F32), jnp.zeros((QBLK, 2 * QBLK), F32)]
            tail = carry[:, ps]
            for j in range(nsub):
                rs = slice(j * QBLK, (j + 1) * QBLK)
                qs = slice(j * QBLK, (j + 2) * QBLK)
                qp = qbuf[qs, ps]
                dd = dobuf[qs, ps]
                kp = k_ref[rs, ps]
                vp = v_ref[rs, ps]
                kt = kp.T
                dk, dv, dqt = [], [], []
                for t in range(2):
                    h = 2 * hp + t
                    sel = low if t == 0 else ~low
                    s = lax.dot_general(jnp.where(sel, kp, zero), qp, NT_DIMS, preferred_element_type=F32) * SCALE + b_ref[h]
                    if j == nsub - 1:
                        s = jnp.where(last, s, NEG_INF)
                    p = jnp.exp(s - sbuf[h:h + 1, qs])
                    dp = lax.dot_general(jnp.where(sel, vp, zero), dd, NT_DIMS, preferred_element_type=F32)
                    ds = p * (dp - sbuf[HEADS + h:HEADS + h + 1, qs])
                    dbs[t] = dbs[t] + ds
                    dsb = ds.astype(BF)
                    dk.append(jnp.dot(dsb, qp, preferred_element_type=F32))
                    dv.append(jnp.dot(p.astype(BF), dd, preferred_element_type=F32))
                    dqt.append(jnp.dot(kt[t * HEAD_DIM:(t + 1) * HEAD_DIM, :], dsb, preferred_element_type=F32))
                o_ref[rs, ps] = (jnp.where(low, dk[0], dk[1]) * SCALE).astype(BF)
                o_ref[rs, GW + hp * PAIR_W:GW + (hp + 1) * PAIR_W] = jnp.where(low, dv[0], dv[1]).astype(BF)
                dq2 = jnp.concatenate(dqt, axis=0).T * SCALE
                o_ref[rs, 2 * GW + hp * PAIR_W:2 * GW + (hp + 1) * PAIR_W] = (dq2[0:QBLK] + tail).astype(BF)
                tail = dq2[QBLK:2 * QBLK]
            carry[:, ps] = tail
            db_ref[2 * hp] += dbs[0]
            db_ref[2 * hp + 1] += dbs[1]

    return pl.pallas_call(
        body, grid=(dil, nq),
        in_specs=[cur(kcb), cur(vcb), cur(qcb), nxt(qcb), band(QBLK, 2 * QBLK),
                  cur(0), nxt(0), cur(0, STAT_W), nxt(0, STAT_W)],
        out_specs=[cur(0, ATTN_W), band(QBLK, 2 * QBLK)],
        out_shape=[jax.ShapeDtypeStruct((S, ATTN_W), BF), jax.ShapeDtypeStruct((HEADS, QBLK, 2 * QBLK), F32)],
        scratch_shapes=[pltpu.VMEM((TQ + QBLK, GW), BF), pltpu.VMEM((TQ + QBLK, GW), BF), pltpu.VMEM((STAT_W, TQ + QBLK), F32),
                        pltpu.VMEM((QBLK, GW), F32)],
        compiler_params=_cp(("arbitrary", "arbitrary")), name=name,
    )(arr, arr, arr, arr, bias_kq2, do, do, stats, stats)


def _head_expand():
    e = np.zeros((STAT_W, GW), np.float32)
    for h in range(HEADS):
        e[h, h * HEAD_DIM:(h + 1) * HEAD_DIM] = 1.0
    return e


def _attn_merge(os_, ls_, S):
    dils = [d for _, d in GROUPS]
    pb = [_perm_const(d, BF, True) for d in dils[1:]]
    pf = [_perm_const(d, F32, True) for d in dils[1:]]
    expand = jnp.asarray(_head_expand(), BF)

    def body(o0, o1, o2, l0, l1, l2, pb1, pb2, pf1, pf2, e_ref, o_ref, l_ref):
        for u in range(PSTEP // PT):
            rs = slice(u * PT, (u + 1) * PT)
            res = lambda r, d: r[:, u * (PT // d):(u + 1) * (PT // d), :].reshape(PT, r.shape[2])
            ov = [o0[rs, :].astype(F32), _apply_perm(pb1[...], res(o1, dils[1])), _apply_perm(pb2[...], res(o2, dils[2]))]
            lv = [l0[rs, :], _apply_perm(pf1[...], res(l1, dils[1])), _apply_perm(pf2[...], res(l2, dils[2]))]
            m = jnp.maximum(jnp.maximum(lv[0], lv[1]), lv[2])
            ev = [jnp.exp(l - m) for l in lv]
            den = ev[0] + ev[1] + ev[2]
            acc = jnp.zeros((PT, GW), F32)
            for g in range(3):
                wide = jnp.dot((ev[g] / den).astype(BF), e_ref[...], preferred_element_type=F32)
                acc = acc + wide * ov[g]
            o_ref[rs, :] = acc.astype(BF)
            l_ref[rs, :] = m + jnp.log(den)

    nat = lambda w: pl.BlockSpec((PSTEP, w), lambda i: (i, 0))
    res = lambda d, w: pl.BlockSpec((d, PSTEP // d, w), lambda i: (0, i, 0))
    cst = lambda a: pl.BlockSpec(a.shape, lambda i: (0, 0))
    args = [os_[0], os_[1].reshape(dils[1], S // dils[1], GW), os_[2].reshape(dils[2], S // dils[2], GW),
            ls_[0], ls_[1].reshape(dils[1], S // dils[1], STAT_W), ls_[2].reshape(dils[2], S // dils[2], STAT_W),
            pb[0], pb[1], pf[0], pf[1], expand]
    return pl.pallas_call(
        body, grid=(S // PSTEP,),
        in_specs=[nat(GW), res(dils[1], GW), res(dils[2], GW), nat(STAT_W), res(dils[1], STAT_W), res(dils[2], STAT_W)]
                 + [cst(a) for a in args[6:]],
        out_specs=[nat(GW), nat(STAT_W)],
        out_shape=[jax.ShapeDtypeStruct((S, GW), BF), jax.ShapeDtypeStruct((S, STAT_W), F32)],
        compiler_params=_cp(), name="attn_merge",
    )(*args)


CT = 256
CBUF = HALO + CT + 8
RG = 4


def _ln_hat(u1):
    mu = jnp.mean(u1, axis=-1, keepdims=True)
    xc = u1 - mu
    rstd = lax.rsqrt(jnp.mean(xc * xc, axis=-1, keepdims=True) + LN_EPS)
    return xc * rstd, rstd


def _glu_window(hu_ref, hg_ref, huh_ref, hgh_ref, bglu_ref, buf_ref, i):
    bu = bglu_ref[:, 0:D]
    bg = bglu_ref[:, D:2 * D]
    uh = (huh_ref[...].astype(F32) + bu) * _sig(hgh_ref[...].astype(F32) + bg)
    buf_ref[0:HALO, :] = jnp.where(i > 0, uh, 0.0)
    a = hu_ref[...].astype(F32) + bu
    s = _sig(hg_ref[...].astype(F32) + bg)
    buf_ref[HALO:HALO + CT, :] = a * s
    buf_ref[HALO + CT:CBUF, :] = jnp.zeros((8, D), F32)
    return a, s


def _shift_copies(buf_ref, sh_ref):
    for r in range(8):
        sh_ref[r] = buf_ref[r:r + HALO + CT, :]


def _tap_rows(wb_ref, w_ref):
    for j in range(CONV_W):
        wb_ref[j * 8:(j + 1) * 8, :] = jnp.broadcast_to(w_ref[j:j + 1, :], (8, D))


def _conv_taps(sh_ref, wb_ref, out_ref, init, offset):
    for rg in range(CT // (8 * RG)):
        accs = [init] * RG
        for j in range(CONV_W):
            off = offset(j)
            wj = wb_ref[j * 8:(j + 1) * 8, :]
            for q in range(RG):
                row = 8 * (rg * RG + q + off // 8)
                accs[q] = accs[q] + wj * sh_ref[off % 8, row:row + 8, :]
        for q in range(RG):
            out_ref[(rg * RG + q) * 8:(rg * RG + q + 1) * 8, :] = accs[q]


def _conv_specs(S):
    cur = lambda cb: pl.BlockSpec((CT, D), lambda i: (i, cb))
    halo = lambda cb: pl.BlockSpec((HALO, D), lambda i: (jnp.maximum(i * (CT // HALO) - 1, 0), cb))
    full = lambda shp: pl.BlockSpec(shp, lambda i: (0, 0))
    return cur, halo, full


def _conv_fwd(z, b_glu, w_dw, b_dw, g_ln, b_ln):
    S = z.shape[0]
    cur, halo, full = _conv_specs(S)

    def body(hu, hg, huh, hgh, bglu, w, bdw, gln, bln, u1_ref, u3_ref, buf, sh, wb):
        i = pl.program_id(0)

        @pl.when(i == 0)
        def _():
            _tap_rows(wb, w)

        _glu_window(hu, hg, huh, hgh, bglu, buf, i)
        _shift_copies(buf, sh)
        _conv_taps(sh, wb, u1_ref, jnp.broadcast_to(bdw[...], (8, D)), lambda j: 2 + j)
        xh, _ = _ln_hat(u1_ref[...])
        u2 = xh * gln[...] + bln[...]
        u3_ref[...] = (u2 * _sig(u2)).astype(BF)

    return pl.pallas_call(
        body, grid=(S // CT,),
        in_specs=[cur(0), cur(1), halo(0), halo(1), full((1, 2 * D)), full((HALO, D)), full((1, D)), full((1, D)), full((1, D))],
        out_specs=[pl.BlockSpec((CT, D), lambda i: (i, 0))] * 2,
        out_shape=[jax.ShapeDtypeStruct((S, D), F32), jax.ShapeDtypeStruct((S, D), BF)],
        scratch_shapes=[pltpu.VMEM((CBUF, D), F32), pltpu.VMEM((8, HALO + CT, D), F32), pltpu.VMEM((CONV_W * 8, D), F32)],
        compiler_params=_cp(("arbitrary",)), name="conv_fwd",
    )(z, z, z, z, b_glu, w_dw, b_dw, g_ln, b_ln)


def _conv_bwd(du1, z, b_glu, w_dw):
    S = z.shape[0]
    n = S // CT
    cur, halo, full = _conv_specs(S)
    RG2 = 2

    def body(du, dun, hu, hg, bglu, w, dz_ref, dw_ref, dbg_ref, bufd, shd, wb, u0_ref, du0_ref, dwacc):
        i = pl.program_id(0)

        @pl.when(i == 0)
        def _():
            _tap_rows(wb, w)
            dwacc[...] = jnp.zeros(dwacc.shape, F32)
            dbg_ref[...] = jnp.zeros(dbg_ref.shape, F32)

        a = hu[...].astype(F32) + bglu[:, 0:D]
        s = _sig(hg[...].astype(F32) + bglu[:, D:2 * D])
        u0_ref[...] = a * s
        bufd[0:CT, :] = du[...]
        bufd[CT:CT + HALO, :] = jnp.where(i < n - 1, dun[...], 0.0)
        bufd[CT + HALO:CBUF, :] = jnp.zeros((8, D), F32)
        _shift_copies(bufd, shd)
        for rg in range(CT // (8 * RG2)):
            uch = [u0_ref[(rg * RG2 + q) * 8:(rg * RG2 + q + 1) * 8, :] for q in range(RG2)]
            accs = [jnp.zeros((8, D), F32)] * RG2
            for j in range(CONV_W):
                off = 30 - j
                wj = wb[j * 8:(j + 1) * 8, :]
                dwj = dwacc[j * 8:(j + 1) * 8, :]
                for q in range(RG2):
                    row = 8 * (rg * RG2 + q + off // 8)
                    x = shd[off % 8, row:row + 8, :]
                    accs[q] = accs[q] + wj * x
                    dwj = dwj + uch[q] * x
                dwacc[j * 8:(j + 1) * 8, :] = dwj
            for q in range(RG2):
                du0_ref[(rg * RG2 + q) * 8:(rg * RG2 + q + 1) * 8, :] = accs[q]
        du0 = du0_ref[...]
        dhu = du0 * s
        dhg = du0 * a * s * (1.0 - s)
        dz_ref[:, 0:D] = dhu.astype(BF)
        dz_ref[:, D:2 * D] = dhg.astype(BF)
        dbg_ref[:, 0:D] += _psum8(dhu)
        dbg_ref[:, D:2 * D] += _psum8(dhg)

        @pl.when(i == n - 1)
        def _():
            dbg_ref[0:1, :] = jnp.sum(dbg_ref[...], axis=0, keepdims=True)
            for j in range(CONV_W):
                dw_ref[j:j + 1, :] = jnp.sum(dwacc[j * 8:(j + 1) * 8, :], axis=0, keepdims=True)
            dw_ref[CONV_W:HALO, :] = jnp.zeros((HALO - CONV_W, D), F32)

    nxt = pl.BlockSpec((HALO, D), lambda i: (jnp.minimum((i + 1) * (CT // HALO), S // HALO - 1), 0))
    return pl.pallas_call(
        body, grid=(n,),
        in_specs=[pl.BlockSpec((CT, D), lambda i: (i, 0)), nxt, cur(0), cur(1), full((1, 2 * D)), full((HALO, D))],
        out_specs=[pl.BlockSpec((CT, 2 * D), lambda i: (i, 0)), full((HALO, D)), full((8, 2 * D))],
        out_shape=[jax.ShapeDtypeStruct((S, 2 * D), BF), jax.ShapeDtypeStruct((HALO, D), F32), jax.ShapeDtypeStruct((8, 2 * D), F32)],
        scratch_shapes=[pltpu.VMEM((CBUF, D), F32), pltpu.VMEM((8, HALO + CT, D), F32), pltpu.VMEM((CONV_W * 8, D), F32),
                        pltpu.VMEM((CT, D), F32), pltpu.VMEM((CT, D), F32), pltpu.VMEM((CONV_W * 8, D), F32)],
        compiler_params=_cp(("arbitrary",)), name="conv_bwd",
    )(du1, du1, z, z, b_glu, w_dw)


MESH = pl.DeviceIdType.MESH


def _all_gather(name, shards):
    n = len(shards)

    def body(*refs):
        ins, outs = refs[:n], refs[n:2 * n]
        send_sems, recv_sems, local_sems = refs[2 * n:]
        x, y, c = lax.axis_index("x"), lax.axis_index("y"), lax.axis_index("c")
        me, sibling = (x, y, c), (x, y, 1 - c)
        chips = [(1 - x, y), (x, 1 - y), (1 - x, 1 - y)]

        def slot(a, px, py, pc):
            return outs[a].at[4 * px + 2 * py + pc]

        def copy(a, k, block, to, src=None):
            return pltpu.make_async_remote_copy(
                src_ref=slot(a, *block) if src is None else src, dst_ref=slot(a, *block),
                send_sem=send_sems.at[a, k], recv_sem=recv_sems.at[a, k], device_id=to, device_id_type=MESH)

        mine = [pltpu.make_async_copy(ins[a], slot(a, *me), local_sems.at[a]) for a in range(n)]
        for cp in mine:
            cp.start()
        first = []
        for a in range(n):
            first.append(copy(a, 0, me, sibling, src=ins[a]))
            first += [copy(a, 1 + j, me, (*chip, c), src=ins[a]) for j, chip in enumerate(chips)]
        for cp in first:
            cp.start()
        passed = []
        for j, chip in enumerate(chips):
            for a in range(n):
                copy(a, 1 + j, (*chip, c), me).wait_recv()
                fwd = copy(a, 4 + j, (*chip, c), sibling)
                fwd.start()
                passed.append(fwd)
        for a in range(n):
            copy(a, 0, sibling, me).wait_recv()
        for j, chip in enumerate(chips):
            for a in range(n):
                copy(a, 4 + j, (*chip, 1 - c), me).wait_recv()
        for cp in first + passed:
            cp.wait_send()
        for cp in mine:
            cp.wait()

    anyspec = pl.BlockSpec(memory_space=pl.ANY)
    return pl.pallas_call(
        body, in_specs=[anyspec] * n, out_specs=[anyspec] * n,
        out_shape=[jax.ShapeDtypeStruct((NDEV,) + s.shape, s.dtype) for s in shards],
        scratch_shapes=[pltpu.SemaphoreType.DMA((n, 7)), pltpu.SemaphoreType.DMA((n, 7)), pltpu.SemaphoreType.DMA((n,))],
        name=name,
    )(*shards)


HBM_SPEC = pl.BlockSpec(memory_space=pltpu.HBM)
SEM_SPEC = pl.BlockSpec(memory_space=pltpu.SEMAPHORE)
DATAFLOW = pltpu.SideEffectType.DATAFLOW_SIDE_EFFECTING


def _peers():
    x, y, c = lax.axis_index("x"), lax.axis_index("y"), lax.axis_index("c")
    out = []
    for k in range(1, NDEV):
        px = 1 - x if k & 4 else x
        py = 1 - y if k & 2 else y
        pc = 1 - c if k & 1 else c
        out.append(((px, py, pc), 4 * px + 2 * py + pc))
    return 4 * x + 2 * y + c, out


def _exchange_copies(srcs, lands, send_sems, recv_sems, gather):
    my, peers = _peers()
    pairs = []
    for k, (dev, pid) in enumerate(peers):
        for a in range(len(srcs)):
            src = srcs[a] if gather else srcs[a].at[pid]
            sems = dict(send_sem=send_sems[a * (NDEV - 1) + k], recv_sem=recv_sems[a * (NDEV - 1) + k], device_id=dev,
                        device_id_type=MESH)
            pairs.append((pltpu.make_async_remote_copy(src_ref=src, dst_ref=lands[a].at[my], **sems),
                          pltpu.make_async_remote_copy(src_ref=src, dst_ref=lands[a].at[pid], **sems)))
    return pairs


def _exchange_start(name, srcs, gather):
    n = len(srcs)
    ns = n * (NDEV - 1)
    shapes = [(s.shape if gather else s.shape[1:]) for s in srcs]
    lands = [lax.empty((NDEV,) + shp, s.dtype) for shp, s in zip(shapes, srcs)]

    def body(*refs):
        src_refs, land_refs = refs[:n], refs[n:2 * n]
        send_sems, recv_sems = refs[2 * n:2 * n + ns], refs[2 * n + ns:2 * n + 2 * ns]
        token = refs[-1]
        for mine, _ in _exchange_copies(src_refs, land_refs, send_sems, recv_sems, gather):
            mine.start()
        token[...] = jnp.zeros(token.shape, token.dtype)

    hbm = lambda a: pltpu.HBM(a.shape, a.dtype)
    res = pl.pallas_call(
        body, name=name,
        out_shape=(*([pltpu.SemaphoreType.DMA(())] * (2 * ns)), *[hbm(s) for s in srcs], *[hbm(l) for l in lands],
                   jax.ShapeDtypeStruct((8, 128), F32)),
        in_specs=[HBM_SPEC] * (2 * n),
        out_specs=(*([SEM_SPEC] * (2 * ns)), *([HBM_SPEC] * (2 * n)), pl.BlockSpec(memory_space=pltpu.VMEM)),
        input_output_aliases={i: 2 * ns + i for i in range(2 * n)},
        compiler_params=pltpu.CompilerParams(has_side_effects=DATAFLOW),
    )(*[pltpu.with_memory_space_constraint(s, pltpu.HBM) for s in srcs],
      *[pltpu.with_memory_space_constraint(l, pltpu.HBM) for l in lands])
    return list(res[:ns]), list(res[ns:2 * ns]), list(res[2 * ns:2 * ns + n]), list(res[2 * ns + n:2 * ns + 2 * n]), res[-1]


def _exchange_wait(name, handle, after, gather):
    send_sems, recv_sems, srcs, lands, _ = handle
    n = len(srcs)
    ns = n * (NDEV - 1)

    def body(*refs):
        src_refs, land_refs = refs[:n], refs[n:2 * n]
        s_sems, r_sems = refs[2 * n:2 * n + ns], refs[2 * n + ns:2 * n + 2 * ns]
        for mine, theirs in _exchange_copies(src_refs, land_refs, s_sems, r_sems, gather):
            mine.wait_send()
            theirs.wait_recv()

    hbm = lambda a: pltpu.HBM(a.shape, a.dtype)
    res = pl.pallas_call(
        body, name=name,
        out_shape=(*[hbm(s) for s in srcs], *[hbm(l) for l in lands]),
        in_specs=[HBM_SPEC] * (2 * n) + [SEM_SPEC] * (2 * ns) + [pl.BlockSpec(memory_space=pl.ANY)],
        out_specs=tuple([HBM_SPEC] * (2 * n)),
        input_output_aliases={i: i for i in range(2 * n)},
        compiler_params=pltpu.CompilerParams(has_side_effects=DATAFLOW),
    )(*srcs, *lands, *send_sems, *recv_sems, after)
    return list(res[n:])


def _set_own_slot(land, own):
    my = 4 * lax.axis_index("x") + 2 * lax.axis_index("y") + lax.axis_index("c")
    return lax.dynamic_update_slice(land, own[None], (my, 0, 0))


def _own_block(blocks):
    my = 4 * lax.axis_index("x") + 2 * lax.axis_index("y") + lax.axis_index("c")
    return lax.dynamic_index_in_dim(blocks, my, axis=0, keepdims=False)


_C1 = 1.0 - ADAM_B1 ** ADAM_STEP
_C2 = 1.0 - ADAM_B2 ** ADAM_STEP


def _adamw(name, w, m, v, recv, tr):
    R, C = w.shape

    def body(w_ref, m_ref, v_ref, r_ref, g_ref, d_ref, nm_ref, nv_ref):
        g = r_ref[0].astype(F32)
        for s in range(1, NDEV):
            g = g + r_ref[s].astype(F32)
        wv = w_ref[...]
        nm = ADAM_B1 * m_ref[...] + (1.0 - ADAM_B1) * g
        nv = ADAM_B2 * v_ref[...] + (1.0 - ADAM_B2) * (g * g)
        m_hat = nm / _C1
        v_hat = nv / _C2
        g_ref[...] = g
        d_ref[...] = -ADAM_LR * (m_hat / (jnp.sqrt(v_hat) + ADAM_EPS) + ADAM_WD * wv)
        nm_ref[...] = nm
        nv_ref[...] = nv

    blk = pl.BlockSpec((tr, C), lambda i: (i, 0))
    return pl.pallas_call(
        body, grid=(R // tr,), in_specs=[blk, blk, blk, pl.BlockSpec((NDEV, tr, C), lambda i: (0, i, 0))],
        out_specs=[blk] * 4, out_shape=[jax.ShapeDtypeStruct((R, C), F32)] * 4,
        compiler_params=_cp(), name=name,
    )(w, m, v, recv)


def _row(v):
    return v.reshape(1, -1)


def _local_step(xs, tgt, Wp, rest_fn, early_fn, late_fn, rel_bias_table, g_pre_mix, b_glu, b_dw, g_conv_ln,
                b_conv_ln, b_conv_out, g_post_mix, g_pre_ffn, g_post_ffn):
    S = xs.shape[0]
    g1, g2, g3, g4 = _row(g_pre_mix), _row(g_post_mix), _row(g_pre_ffn), _row(g_post_ffn)
    bglu, bdw, gln, bln, bco = _row(b_glu), _row(b_dw), _row(g_conv_ln), _row(b_conv_ln), _row(b_conv_out)
    full = (D, F32, D, 0, False)
    fullb = (D, BF, D, 0, False)

    def epi_rms(accs, r, c, o, p):
        v = r[0][...]
        o[0][...] = (v * _rms_r(v) * c[0][...]).astype(BF)

    (h1,) = _fused_mm("rms_in", S, 512, 1, [], [], [], [], [(xs, D, 0, False)], [g1], [fullb], [], epi_rms)

    def epi_cast(accs, r, c, o, p):
        o[0][...] = accs[0].astype(BF)

    ZT = IN_W // 4
    (z,) = _fused_mm("in_proj", S, 1024, 4, [(h1, D, 0, 1)], [(Wp, False, D, ZT, 0, 1, 0, 0, True)], [(0, 0, 0, 0, 1)],
                     [(1024, ZT)], [], [], [(IN_W, BF, ZT, 0, True)], [], epi_cast, n_outer=True)

    idx = jnp.asarray(_band_index())
    tab_t = rel_bias_table.T.reshape(3, HEADS, REL_BUCKETS)
    bias_all = _bias_build(tab_t, idx)
    bias_kq = [bias_all[g].reshape(HEADS, 2 * QBLK, QBLK) for g in range(3)]
    bias_kq2 = [bias_all[3 + g].reshape(HEADS, QBLK, 2 * QBLK) for g in range(3)]
    dils = [d for _, d in GROUPS]
    qkv = [(z, _kvq_blocks(0))] + [(_to_residue(f"qkv_to_residue_g{g}", z, _kvq_blocks(g), dils[g]), (0, 1, 2)) for g in (1, 2)]
    os_, ls_ = [], []
    for g in range(3):
        o_g, l_g = _attn_fwd(f"attn_fwd_g{g}", qkv[g][0], bias_kq[g], qkv[g][1], dils[g])
        os_.append(o_g)
        ls_.append(l_g)
    o_att, lse = _attn_merge(os_, ls_, S)

    Wfi, Wfo, Wco, Wmo, Wao, wdw = rest_fn(lse)
    u1, u3 = _conv_fwd(z, bglu, wdw, bdw, gln, bln)

    def epi_mix(accs, r, c, o, p):
        ya = accs[0]
        yc = accs[1] + c[0][...]
        mg = _sig(r[0][...].astype(F32)) * ya + _sig(r[1][...].astype(F32)) * yc
        mgb = mg.astype(BF)
        m2 = jnp.dot(mgb, c[1][...], preferred_element_type=F32)
        x1 = r[2][...] + m2 * _rms_r(m2) * c[2][...]
        o[0][...] = ya.astype(BF)
        o[1][...] = yc.astype(BF)
        o[2][...] = mgb
        o[3][...] = m2.astype(BF)
        o[4][...] = x1
        o[5][...] = (x1 * _rms_r(x1) * c[3][...]).astype(BF)

    y_attn, y_conv, merged, m2, x1, h2 = _fused_mm(
        "mix_fwd", S, 512, 1, [(o_att, GW, 0, 1), (u3, D, 0, 1)],
        [(Wao, False, GW, D, 0, 1, 0, 0, False), (Wco, False, D, D, 0, 1, 0, 0, False)], [(0, 0, 0, 0, 1), (1, 1, 1, 0, 1)],
        [(512, D), (512, D)], [(z, D, 2, False), (z, D, 3, False), (xs, D, 0, False)], [bco, Wmo, g2, g3],
        [fullb, fullb, fullb, fullb, full, fullb], [], epi_mix)

    HN = FFN // 2

    def epi_ffn_in(accs, r, c, o, p):
        gt, up = accs
        o[0][...] = gt.astype(BF)
        o[1][...] = up.astype(BF)
        o[2][...] = (gt * _sig(gt) * up).astype(BF)

    gate, up, act = _fused_mm(
        "ffn_in", S, 512, 2, [(h2, D, 0, 1)],
        [(Wfi, False, D, HN, 0, 1, 0, 0, True), (Wfi, False, D, HN, 0, 1, 0, 2, True)], [(0, 0, 0, 0, 1), (0, 1, 1, 0, 1)],
        [(512, HN), (512, HN)], [], [], [(FFN, BF, HN, 0, True)] * 3, [], epi_ffn_in, n_outer=True)

    def epi_loss(accs, r, c, o, p):
        f2 = accs[0]
        g = c[0][...]
        rr = _rms_r(f2)
        err = r[0][...] + f2 * rr * g - r[1][...]
        dy = err * (1.0 / D)
        df2, dgr = _rms_bwd(f2, rr, g, dy)
        o[0][...] = dy
        o[1][...] = df2.astype(BF)
        p[0][...] += _psum8(err * err)
        p[1][...] += _psum8(dgr)

    dy, df2, loss_p, dg4 = _fused_mm(
        "ffn_out_loss", S, 512, 1, [(act, FFN, 0, 1)], [(Wfo, False, FFN, D, 0, 1, 0, 0, False)], [(0, 0, 0, 0, 1)],
        [(512, D)], [(x1, D, 0, False), (tgt, D, 0, False)], [g4], [full, fullb], [(8, D), (8, D)], epi_loss)

    def epi_swiglu(accs, r, c, o, p):
        da = accs[0]
        gt = r[0][...].astype(F32)
        sg = _sig(gt)
        o[0][...] = (da * r[1][...].astype(F32) * sg * (1.0 + gt * (1.0 - sg))).astype(BF)
        o[1][...] = (da * gt * sg).astype(BF)

    dgate, dup = _fused_mm(
        "ffn_out_bwd", S, 512, 2, [(df2, D, 0, 1)], [(Wfo, True, D, HN, 0, 1, 0, 0, True)], [(0, 0, 0, 0, 1)],
        [(512, HN)], [(gate, HN, 0, True), (up, HN, 0, True)], [], [(FFN, BF, HN, 0, True)] * 2, [], epi_swiglu, n_outer=True)
    dWfo = _mm_tn("dw_ffn_out", act, df2, HN, D, WG_TK)

    def epi_dh2(accs, r, c, o, p):
        dh2 = accs[0]
        x1v = r[1][...]
        r3 = _rms_r(x1v)
        d1, dg3r = _rms_bwd(x1v, r3, c[0][...], dh2)
        dx1 = r[0][...] + d1
        m2v = r[2][...].astype(F32)
        r2 = _rms_r(m2v)
        dm2, dg2r = _rms_bwd(m2v, r2, c[1][...], dx1)
        o[0][...] = dx1
        o[1][...] = dm2.astype(BF)
        p[0][...] += _psum8(dg3r)
        p[1][...] += _psum8(dg2r)

    dx1, dm2, dg3, dg2 = _resident_mm(
        "ffn_in_bwd", S, 512, [dgate, dup], Wfi, [(dy, D), (x1, D), (m2, D)], [g3, g2], [(D, F32), (D, BF)], [(8, D), (8, D)], epi_dh2)
    dWfi = jnp.concatenate([_mm_tn("dw_ffn_gate", h2, dgate, D, HN, WG_TK), _mm_tn("dw_ffn_up", h2, dup, D, HN, WG_TK)], axis=1)

    def epi_dmix(accs, r, c, o, p):
        dm = accs[0]
        sa = _sig(r[0][...].astype(F32))
        sc = _sig(r[1][...].astype(F32))
        o[0][...] = (dm * sa).astype(BF)
        o[1][...] = (dm * sc).astype(BF)
        o[2][:, 0:D] = (dm * r[2][...].astype(F32) * sa * (1.0 - sa)).astype(BF)
        o[2][:, D:2 * D] = (dm * r[3][...].astype(F32) * sc * (1.0 - sc)).astype(BF)

    dy_attn, dy_conv, dz_gate = _fused_mm(
        "mix_bwd", S, 512, 1, [(dm2, D, 0, 1)], [(Wmo, True, D, D, 0, 1, 0, 0, False)], [(0, 0, 0, 0, 1)], [(512, D)],
        [(z, D, 2, False), (z, D, 3, False), (y_attn, D, 0, False), (y_conv, D, 0, False)], [],
        [fullb, fullb, (2 * D, BF, 2 * D, 0, False)], [], epi_dmix)
    dWmo = _mm_tn("dw_mix_out", merged, dm2, D, D, WG_TK)

    def epi_dconv(accs, r, c, o, p):
        du3 = accs[0]
        xh, rstd = _ln_hat(r[0][...])
        gl = c[0][...]
        u2 = xh * gl + c[1][...]
        sg = _sig(u2)
        du2 = du3 * sg * (1.0 + u2 * (1.0 - sg))
        dxh = du2 * gl
        du1 = rstd * (dxh - jnp.mean(dxh, axis=-1, keepdims=True) - xh * jnp.mean(dxh * xh, axis=-1, keepdims=True))
        o[0][...] = du1
        p[0][...] += _psum8(du2 * xh)
        p[1][...] += _psum8(du2)
        p[2][...] += _psum8(du1)
        p[3][...] += _psum8(r[1][...].astype(F32))

    du1, dgln, dbln, dbdw, dbco = _fused_mm(
        "conv_out_bwd", S, 512, 1, [(dy_conv, D, 0, 1)], [(Wco, True, D, D, 0, 1, 0, 0, False)], [(0, 0, 0, 0, 1)], [(512, D)],
        [(u1, D, 0, False), (dy_conv, D, 0, False)], [gln, bln], [full], [(8, D)] * 4, epi_dconv)
    dWco = _mm_tn("dw_conv_out", u3, dy_conv, D, D, WG_TK)
    dz_glu, dwdw, dbglu = _conv_bwd(du1, z, bglu, wdw)

    head_sum = np.zeros((GW, STAT_W), np.float32)
    for h in range(HEADS):
        head_sum[h * HEAD_DIM:(h + 1) * HEAD_DIM, HEADS + h] = 1.0
    head_sum = jnp.asarray(head_sum)

    def epi_do(accs, r, c, o, p):
        do = accs[0]
        o[0][...] = do.astype(BF)
        delta = jnp.dot(do * r[0][...].astype(F32), c[0][...], preferred_element_type=F32, precision=lax.Precision.HIGHEST)
        lane = lax.broadcasted_iota(jnp.int32, delta.shape, 1)
        o[1][...] = jnp.where(lane < HEADS, r[1][...], delta)

    do, stats = _fused_mm(
        "attn_out_bwd", S, 1024, 1, [(dy_attn, D, 0, 1)], [(Wao, True, D, GW, 0, 1, 0, 0, False)], [(0, 0, 0, 0, 1)], [(1024, GW)],
        [(o_att, GW, 0, False), (lse, STAT_W, 0, False)], [head_sum], [(GW, BF, GW, 0, False), (STAT_W, F32, STAT_W, 0, False)], [], epi_do)
    dWao = _mm_tn("dw_attn_out", o_att, dy_attn, GW, D, WG_TK)

    tie = early_fn(dict(w_ffn_in=dWfi, w_ffn_out=dWfo, w_conv_out=dWco, w_mix_out=dWmo, w_attn_out=dWao, w_dw=dwdw))
    stats = stats + tie
    dos = [do] + [_to_residue(f"do_to_residue_g{g}", do, (0,), dils[g]) for g in (1, 2)]
    sts = [stats] + [_to_residue_stats(f"stats_to_residue_g{g}", stats, dils[g]) for g in (1, 2)]
    dqkv, dbs = [], []
    for g in range(3):
        arr, cb = qkv[g]
        dg, db = _attn_bwd(f"attn_bwd_g{g}", arr, bias_kq2[g], dos[g], sts[g], cb, dils[g])
        dqkv.append(dg if g == 0 else _from_residue(f"dqkv_from_residue_g{g}", dg, dils[g]))
        dbs.append(db.reshape(HEADS, _NB))
    dtab = _bias_grad(jnp.stack(dbs), idx)[:, :, :REL_BUCKETS].reshape(3 * HEADS, REL_BUCKETS).T

    def epi_dx(accs, r, c, o, p):
        xv = r[1][...]
        d1, dg1r = _rms_bwd(xv, _rms_r(xv), c[0][...], accs[0])
        o[0][...] = r[0][...] + d1
        p[0][...] += _psum8(dg1r)

    dWg = [_mm_tn(f"dw_in_g{g}", h1, dqkv[g], D, ATTN_W, WG_TK) for g in range(3)]
    dW_in = jnp.concatenate(
        [t[:, 2 * GW:] for t in dWg] + [t[:, :GW] for t in dWg] + [t[:, GW:2 * GW] for t in dWg]
        + [_mm_tn("dw_in_glu", h1, dz_glu, D, D, WG_TK), _mm_tn("dw_in_gate", h1, dz_gate, D, D, WG_TK)], axis=1)
    g1_late = g1 + late_fn(dW_in)
    grad_x, dg1 = _resident_mm(
        "in_proj_bwd", S, 512, [dz_glu, dz_gate, dqkv[0], dqkv[1], dqkv[2]], Wp, [(dx1, D), (xs, D)], [g1_late], [(D, F32)], [(8, D)], epi_dx)

    small = dict(rel_bias_table=dtab, g_pre_mix=dg1[0], b_glu=dbglu[0], b_dw=dbdw[0], g_conv_ln=dgln[0], b_conv_ln=dbln[0],
                 b_conv_out=dbco[0], g_post_mix=dg2[0], g_pre_ffn=dg3[0], g_post_ffn=dg4[0])
    return loss_p[0], grad_x, small


SMALL = ['rel_bias_table', 'g_pre_mix', 'b_glu', 'b_dw', 'g_conv_ln', 'b_conv_ln', 'b_conv_out', 'g_post_mix', 'g_pre_ffn',
         'g_post_ffn']
BIG = ['w_in', 'w_ffn_in', 'w_ffn_out', 'w_conv_out', 'w_mix_out', 'w_attn_out', 'w_dw']
WEIGHTS = ['rel_bias_table', 'g_pre_mix', 'w_in', 'b_glu', 'w_dw', 'b_dw', 'g_conv_ln', 'b_conv_ln', 'w_conv_out', 'b_conv_out',
           'w_attn_out', 'w_mix_out', 'g_post_mix', 'g_pre_ffn', 'w_ffn_in', 'w_ffn_out', 'g_post_ffn']
SMALL_ROWS = 16


ROW_SMALL = ['g_pre_mix', 'b_glu', 'b_dw', 'g_conv_ln', 'b_conv_ln', 'b_conv_out', 'g_post_mix', 'g_pre_ffn', 'g_post_ffn']
LOSS_ROW = 10
TAB_LANES = 128


def _small_rows(small, loss_row):
    rows = [small[n].reshape(-1, D) for n in ROW_SMALL] + [loss_row.reshape(1, D)]
    n = sum(r.shape[0] for r in rows)
    return jnp.concatenate(rows + [jnp.zeros((SMALL_ROWS - n, D), F32)], axis=0)


def _adamw_small(recv_rows, recv_tab, ws, ms, vs):
    np_ = len(SMALL)

    def body(*refs):
        rr, rt = refs[0], refs[1]
        w_refs, m_refs, v_refs = refs[2:2 + np_], refs[2 + np_:2 + 2 * np_], refs[2 + 2 * np_:2 + 3 * np_]
        loss_ref = refs[2 + 3 * np_]
        outs = refs[3 + 3 * np_:]
        rows = rr[0]
        tab = rt[0]
        for s_ in range(1, NDEV):
            rows = rows + rr[s_]
            tab = tab + rt[s_]
        loss_ref[...] = jnp.sum(rows[LOSS_ROW:LOSS_ROW + 1, :], axis=1, keepdims=True) * (0.5 / D)
        row = 0
        for p, n in enumerate(SMALL):
            if n == 'rel_bias_table':
                g = tab[:, 0:3 * HEADS]
            else:
                k = w_refs[p].shape[1] // D
                g = rows[row:row + 1, :] if k == 1 else jnp.concatenate([rows[row + t:row + t + 1, :] for t in range(k)], axis=1)
                row += k
            nm = ADAM_B1 * m_refs[p][...] + (1.0 - ADAM_B1) * g
            nv = ADAM_B2 * v_refs[p][...] + (1.0 - ADAM_B2) * (g * g)
            outs[4 * p][...] = g
            outs[4 * p + 1][...] = -ADAM_LR * ((nm / _C1) / (jnp.sqrt(nv / _C2) + ADAM_EPS) + ADAM_WD * w_refs[p][...])
            outs[4 * p + 2][...] = nm
            outs[4 * p + 3][...] = nv

    out_shape = [jax.ShapeDtypeStruct((1, 1), F32)]
    for a_ in ws:
        out_shape += [jax.ShapeDtypeStruct(a_.shape, F32)] * 4
    res = pl.pallas_call(body, out_shape=out_shape, compiler_params=_cp(), name="adamw_small")(recv_rows, recv_tab, *ws, *ms, *vs)
    return res[0], [tuple(res[1 + 4 * p:5 + 4 * p]) for p in range(np_)]


def _cols_to_blocks(a):
    R = a.shape[0]
    return a.reshape(R, NDEV, a.shape[1] // NDEV).transpose(1, 0, 2)


def _blocks_to_cols(a):
    return a.transpose(1, 0, 2).reshape(a.shape[1], NDEV * a.shape[2])


def kernel(x, rel_bias_table, g_pre_mix, w_in, b_glu, w_dw, b_dw, g_conv_ln, b_conv_ln, w_conv_out, b_conv_out, w_attn_out, w_mix_out, g_post_mix, g_pre_ffn, w_ffn_in, w_ffn_out, g_post_ffn, loss_target, m_rel_bias_table, m_g_pre_mix, m_w_in, m_b_glu, m_w_dw, m_b_dw, m_g_conv_ln, m_b_conv_ln, m_w_conv_out, m_b_conv_out, m_w_attn_out, m_w_mix_out, m_g_post_mix, m_g_pre_ffn, m_w_ffn_in, m_w_ffn_out, m_g_post_ffn, v_rel_bias_table, v_g_pre_mix, v_w_in, v_b_glu, v_w_dw, v_b_dw, v_g_conv_ln, v_b_conv_ln, v_w_conv_out, v_b_conv_out, v_w_attn_out, v_w_mix_out, v_g_post_mix, v_g_pre_ffn, v_w_ffn_in, v_w_ffn_out, v_g_post_ffn):
    w = dict(rel_bias_table=rel_bias_table, g_pre_mix=g_pre_mix, w_in=w_in, b_glu=b_glu, w_dw=w_dw, b_dw=b_dw, g_conv_ln=g_conv_ln, b_conv_ln=b_conv_ln, w_conv_out=w_conv_out, b_conv_out=b_conv_out, w_attn_out=w_attn_out, w_mix_out=w_mix_out, g_post_mix=g_post_mix, g_pre_ffn=g_pre_ffn, w_ffn_in=w_ffn_in, w_ffn_out=w_ffn_out, g_post_ffn=g_post_ffn)
    m = dict(rel_bias_table=m_rel_bias_table, g_pre_mix=m_g_pre_mix, w_in=m_w_in, b_glu=m_b_glu, w_dw=m_w_dw, b_dw=m_b_dw, g_conv_ln=m_g_conv_ln, b_conv_ln=m_b_conv_ln, w_conv_out=m_w_conv_out, b_conv_out=m_b_conv_out, w_attn_out=m_w_attn_out, w_mix_out=m_w_mix_out, g_post_mix=m_g_post_mix, g_pre_ffn=m_g_pre_ffn, w_ffn_in=m_w_ffn_in, w_ffn_out=m_w_ffn_out, g_post_ffn=m_g_post_ffn)
    v = dict(rel_bias_table=v_rel_bias_table, g_pre_mix=v_g_pre_mix, w_in=v_w_in, b_glu=v_b_glu, w_dw=v_w_dw, b_dw=v_b_dw, g_conv_ln=v_g_conv_ln, b_conv_ln=v_b_conv_ln, w_conv_out=v_w_conv_out, b_conv_out=v_b_conv_out, w_attn_out=v_w_attn_out, w_mix_out=v_w_mix_out, g_post_mix=v_g_post_mix, g_pre_ffn=v_g_pre_ffn, w_ffn_in=v_w_ffn_in, w_ffn_out=v_w_ffn_out, g_post_ffn=v_g_post_ffn)

    def shard2d(d, n):
        a = d[n][0]
        return jnp.pad(a, ((0, HALO - CONV_W), (0, 0))) if n == 'w_dw' else a

    own = {n: shard2d(w, n).astype(F32 if n == 'w_dw' else BF) for n in BIG}
    packed = ['w_ffn_out', 'w_conv_out', 'w_mix_out', 'w_attn_out']
    alone = ['w_ffn_in', 'w_dw']
    shapes = [own[n].shape for n in packed]

    def pack(arrs, lead):
        return jnp.concatenate([a.reshape(lead + (-1, D)) for a in arrs], axis=len(lead))

    def unpack(p):
        out, pos = {}, 0
        for n, shp in zip(packed, shapes):
            rows = shp[0] * shp[1] // D
            out[n] = p[:, pos:pos + rows].reshape((NDEV,) + shp)
            pos += rows
        return out

    (g_in,) = _all_gather("gather_w_in", [own['w_in']])
    rest_own = [pack([own[n] for n in packed], ())] + [own[n] for n in alone]
    g_in, rest_own = lax.optimization_barrier((g_in, rest_own))
    gather_rest = _exchange_start("gather_rest_start", rest_own, True)
    W_in = _blocks_to_cols(g_in)
    kvq = [W_in[:, t * ATTN_W + g * GW:t * ATTN_W + (g + 1) * GW] for g in range(3) for t in (1, 2, 0)]
    Wp = jnp.concatenate([W_in[:, 3 * ATTN_W:]] + kvq, axis=1)

    def rest_fn(after):
        lands = _exchange_wait("gather_rest_wait", gather_rest, after, True)
        gw = unpack(_set_own_slot(lands[0], rest_own[0]))
        for n, l, o in zip(alone, lands[1:], rest_own[1:]):
            gw[n] = _set_own_slot(l, o)
        return (_blocks_to_cols(gw['w_ffn_in']), gw['w_ffn_out'].reshape(FFN, D), gw['w_conv_out'].reshape(D, D),
                gw['w_mix_out'].reshape(D, D), _blocks_to_cols(gw['w_attn_out']), _blocks_to_cols(gw['w_dw']))

    def to_blocks(n, g):
        if n in ('w_in', 'w_ffn_in', 'w_attn_out', 'w_dw'):
            return _cols_to_blocks(g)
        return g.reshape(NDEV, g.shape[0] // NDEV, g.shape[1])

    started = {}

    def early_fn(grads):
        blocks = [pack([to_blocks(n, grads[n]) for n in packed], (NDEV,))] + [to_blocks(n, grads[n]) for n in alone]
        started['blocks'] = blocks
        started['handle'] = _exchange_start("scatter_early_start", blocks, False)
        return started['handle'][4][0:1, 0:1]

    def late_fn(dW_in):
        started['in_blocks'] = [to_blocks('w_in', dW_in)]
        started['in_handle'] = _exchange_start("scatter_w_in_start", started['in_blocks'], False)
        return started['in_handle'][4][0:1, 0:1]

    g1_tied = g_pre_mix[0] + gather_rest[4][0, 0:1]
    loss_row, grad_x, small = _local_step(
        x[0], loss_target[0], Wp, rest_fn, early_fn, late_fn, rel_bias_table, g1_tied, b_glu[0], b_dw[0], g_conv_ln[0],
        b_conv_ln[0], b_conv_out[0], g_post_mix[0], g_pre_ffn[0], g_post_ffn[0])

    lands = _exchange_wait("scatter_early_wait", started['handle'], grad_x, False)
    lands = [_set_own_slot(l, _own_block(b)) for l, b in zip(lands, started['blocks'])]
    recv = unpack(lands[0])
    recv.update(zip(alone, lands[1:]))
    (land_in,) = _exchange_wait("scatter_w_in_wait", started['in_handle'], grad_x, False)
    recv['w_in'] = _set_own_slot(land_in, _own_block(started['in_blocks'][0]))
    tiles = dict(w_in=128, w_ffn_in=256, w_ffn_out=176, w_conv_out=128, w_mix_out=128, w_attn_out=512, w_dw=HALO)
    res = {}
    for n in BIG:
        g_, d_, nm_, nv_ = _adamw("adamw_" + n, shard2d(w, n), shard2d(m, n), shard2d(v, n), recv[n], tiles[n])
        if n == 'w_dw':
            g_, d_, nm_, nv_ = (t[:CONV_W] for t in (g_, d_, nm_, nv_))
        res[n] = tuple(t[None] for t in (g_, d_, nm_, nv_))

    tab = jnp.pad(small['rel_bias_table'], ((0, 0), (0, TAB_LANES - 3 * HEADS)))
    srows, stab = _all_gather("gather_small_grads", [_small_rows(small, loss_row), tab])
    loss11, small_res = _adamw_small(srows, stab, [w[n] for n in SMALL], [m[n] for n in SMALL], [v[n] for n in SMALL])
    loss = loss11.reshape(())
    for n, r in zip(SMALL, small_res):
        res[n] = r
    return (loss, grad_x[None], *[res[n][0] for n in WEIGHTS], *[res[n][1] for n in WEIGHTS],
            *[res[n][2] for n in WEIGHTS], *[res[n][3] for n in WEIGHTS])
```

```python
import functools
import math

import numpy as np
import jax
import jax.numpy as jnp
from jax import lax
from jax.experimental import pallas as pl
from jax.experimental.pallas import tpu as pltpu

F32 = jnp.float32
BF = jnp.bfloat16

D = 1024
HEAD_DIM = 64
HEADS = 8
GROUPS = ((128, 1), (512, 4), (2048, 16))
QBLK = 128
GW = HEADS * HEAD_DIM
ATTN_W = 3 * GW
REL_BUCKETS = 32
REL_MAX_DISTANCE = 2048
CONV_W = 31
HALO = 32
FFN = 2816
IN_W = 3 * ATTN_W + 2 * D + 2 * D
RMS_EPS = 1e-6
LN_EPS = 1e-5
NEG_INF = -1e30
SCALE = HEAD_DIM ** -0.5
NDEV = 8

ADAM_LR = 0.001
ADAM_B1 = 0.9
ADAM_B2 = 0.999
ADAM_EPS = 1e-08
ADAM_WD = 0.01
ADAM_STEP = 10

Z_G0 = 4096 // GW


def _kvq_blocks(g):
    return (Z_G0 + 3 * g, Z_G0 + 3 * g + 1, Z_G0 + 3 * g + 2)


WG_TK = 2048
VMEM_LIMIT = 52 * 1024 * 1024


def _cp(sem=None):
    if sem is None:
        return pltpu.CompilerParams(vmem_limit_bytes=VMEM_LIMIT)
    return pltpu.CompilerParams(vmem_limit_bytes=VMEM_LIMIT, dimension_semantics=sem)


def _sig(v):
    return jax.nn.sigmoid(v)


def _psum8(v):
    return v.reshape(v.shape[0] // 8, 8, v.shape[1]).sum(axis=0)


def _rms_r(v):
    return lax.rsqrt(jnp.mean(v * v, axis=-1, keepdims=True) + RMS_EPS)


def _rms_bwd(v, r, g, dy):
    gy = dy * g
    dv = r * gy - v * (r * r * r) * jnp.mean(v * gy, axis=-1, keepdims=True)
    return dv, dy * v * r


def _clip_k(k, k0, nk):
    return jnp.clip(k - k0, 0, nk - 1)


def _fused_mm(name, M, tm, grid_n, a_ops, b_ops, terms, acc_shapes, rows, consts, outs, parts, epilogue, n_outer=False):
    gm = M // tm
    nk_total = max([t[3] + t[4] for t in terms], default=1)
    n_a, n_b, n_r, n_c, n_o, n_p = len(a_ops), len(b_ops), len(rows), len(consts), len(outs), len(parts)
    n_acc = len(acc_shapes)
    use_scratch = nk_total > 1
    if parts:
        assert grid_n == 1

    def jj(j, follow):
        return j if follow else 0

    in_specs, args = [], []
    for (arr, tk, k0, nk) in a_ops:
        in_specs.append(pl.BlockSpec((tm, tk), functools.partial(lambda i, j, k, k0, nk: (i, _clip_k(k, k0, nk)), k0=k0, nk=nk)))
        args.append(arr)
    for (arr, nt, tk, tn, k0, nk, koff, joff, fj) in b_ops:
        if nt:
            in_specs.append(pl.BlockSpec((tn, tk), functools.partial(
                lambda i, j, k, k0, nk, koff, joff, fj: (joff + jj(j, fj), _clip_k(k, k0, nk) + koff),
                k0=k0, nk=nk, koff=koff, joff=joff, fj=fj)))
        else:
            in_specs.append(pl.BlockSpec((tk, tn), functools.partial(
                lambda i, j, k, k0, nk, koff, joff, fj: (_clip_k(k, k0, nk) + koff, joff + jj(j, fj)),
                k0=k0, nk=nk, koff=koff, joff=joff, fj=fj)))
        args.append(arr)
    for (arr, w, off, fj) in rows:
        in_specs.append(pl.BlockSpec((tm, w), functools.partial(lambda i, j, k, off, fj: (i, off + jj(j, fj)), off=off, fj=fj)))
        args.append(arr)
    for arr in consts:
        in_specs.append(pl.BlockSpec(arr.shape, functools.partial(lambda i, j, k, nd: (0,) * nd, nd=arr.ndim)))
        args.append(arr)
    out_specs, out_shape = [], []
    for (ncols, dt, w, off, fj) in outs:
        out_specs.append(pl.BlockSpec((tm, w), functools.partial(lambda i, j, k, off, fj: (i, off + jj(j, fj)), off=off, fj=fj)))
        out_shape.append(jax.ShapeDtypeStruct((M, ncols), dt))
    for (r, c) in parts:
        out_specs.append(pl.BlockSpec((r, c), lambda i, j, k: (0, 0)))
        out_shape.append(jax.ShapeDtypeStruct((r, c), F32))
    scratch = [pltpu.VMEM(s, F32) for s in acc_shapes] if use_scratch else []

    def body(*refs):
        pos = 0
        a_refs = refs[pos:pos + n_a]; pos += n_a
        b_refs = refs[pos:pos + n_b]; pos += n_b
        r_refs = refs[pos:pos + n_r]; pos += n_r
        c_refs = refs[pos:pos + n_c]; pos += n_c
        o_refs = refs[pos:pos + n_o]; pos += n_o
        p_refs = refs[pos:pos + n_p]; pos += n_p
        acc_refs = refs[pos:pos + n_acc] if use_scratch else ()
        i = pl.program_id(0)
        k = pl.program_id(2)

        def dot_of(ai, bi):
            a = a_refs[ai][...].astype(BF)
            b = b_refs[bi][...].astype(BF)
            if b_ops[bi][1]:
                return lax.dot_general(a, b, (((1,), (1,)), ((), ())), preferred_element_type=F32)
            return jnp.dot(a, b, preferred_element_type=F32)

        if parts:
            @pl.when((i == 0) & (k == 0))
            def _():
                for p in p_refs:
                    p[...] = jnp.zeros(p.shape, F32)

        def finish(accs):
            epilogue(accs, r_refs, c_refs, o_refs, p_refs)
            if parts:
                @pl.when(i == gm - 1)
                def _():
                    for p in p_refs:
                        p[0:1, :] = jnp.sum(p[...], axis=0, keepdims=True)

        if not use_scratch:
            accs = [None] * n_acc
            for (ai, bi, ci, k0, nk) in terms:
                d = dot_of(ai, bi)
                accs[ci] = d if accs[ci] is None else accs[ci] + d
            finish(accs)
        else:
            @pl.when(k == 0)
            def _():
                for acc in acc_refs:
                    acc[...] = jnp.zeros(acc.shape, F32)

            for (ai, bi, ci, k0, nk) in terms:
                def do(ai=ai, bi=bi, ci=ci):
                    acc_refs[ci][...] += dot_of(ai, bi)
                if k0 == 0 and nk == nk_total:
                    do()
                else:
                    pl.when((k >= k0) & (k < k0 + nk))(do)

            @pl.when(k == nk_total - 1)
            def _():
                finish([acc[...] for acc in acc_refs])

    grid = (gm, grid_n, nk_total)
    if n_outer:
        assert not parts
        swap = lambda spec: pl.BlockSpec(spec.block_shape, functools.partial(lambda j, i, k, f: f(i, j, k), f=spec.index_map))
        in_specs, out_specs, grid = [swap(sp) for sp in in_specs], [swap(sp) for sp in out_specs], (grid_n, gm, nk_total)
    res = pl.pallas_call(
        body, grid=grid, in_specs=in_specs, out_specs=out_specs, out_shape=out_shape,
        scratch_shapes=scratch, compiler_params=_cp(("arbitrary", "arbitrary", "arbitrary")), name=name,
    )(*args)
    return res


def _mm_tn(name, a, b, tm, tn, tk):
    S, Ka = a.shape
    Nb = b.shape[1]
    nk = S // tk

    def body(a_ref, b_ref, o_ref, acc):
        k = pl.program_id(2)

        @pl.when(k == 0)
        def _():
            acc[...] = jnp.zeros(acc.shape, F32)

        acc[...] += lax.dot_general(a_ref[...], b_ref[...], (((0,), (0,)), ((), ())), preferred_element_type=F32)

        @pl.when(k == nk - 1)
        def _():
            o_ref[...] = acc[...].astype(o_ref.dtype)

    return pl.pallas_call(
        body, grid=(Ka // tm, Nb // tn, nk),
        in_specs=[pl.BlockSpec((tk, tm), lambda i, j, k: (k, i)), pl.BlockSpec((tk, tn), lambda i, j, k: (k, j))],
        out_specs=pl.BlockSpec((tm, tn), lambda i, j, k: (i, j)),
        out_shape=jax.ShapeDtypeStruct((Ka, Nb), BF),
        scratch_shapes=[pltpu.VMEM((tm, tn), F32)],
        compiler_params=_cp(("parallel", "parallel", "arbitrary")), name=name,
    )(a, b)


def _resident_mm(name, M, tm, a_segs, w, rows, consts, outs, parts, epilogue):
    gm = M // tm
    n_a, n_r, n_c, n_o, n_p = len(a_segs), len(rows), len(consts), len(outs), len(parts)
    widths = [a.shape[1] for a in a_segs]
    offs = [sum(widths[:t]) for t in range(n_a)]
    once = pl.Buffered(1)

    def body(*refs):
        pos = 0
        a_refs = refs[pos:pos + n_a]; pos += n_a
        w_ref = refs[pos]; pos += 1
        r_refs = refs[pos:pos + n_r]; pos += n_r
        c_refs = refs[pos:pos + n_c]; pos += n_c
        o_refs = refs[pos:pos + n_o]; pos += n_o
        p_refs = refs[pos:pos + n_p]
        i = pl.program_id(0)
        if parts:
            @pl.when(i == 0)
            def _():
                for p in p_refs:
                    p[...] = jnp.zeros(p.shape, F32)
        acc = None
        for t in range(n_a):
            d = lax.dot_general(a_refs[t][...], w_ref[:, offs[t]:offs[t] + widths[t]], (((1,), (1,)), ((), ())),
                                preferred_element_type=F32)
            acc = d if acc is None else acc + d
        epilogue([acc], r_refs, c_refs, o_refs, p_refs)
        if parts:
            @pl.when(i == gm - 1)
            def _():
                for p in p_refs:
                    p[0:1, :] = jnp.sum(p[...], axis=0, keepdims=True)

    in_specs = [pl.BlockSpec((tm, wd), lambda i: (i, 0)) for wd in widths]
    in_specs.append(pl.BlockSpec(w.shape, lambda i: (0, 0), pipeline_mode=once))
    in_specs += [pl.BlockSpec((tm, c), lambda i: (i, 0)) for _, c in rows]
    in_specs += [pl.BlockSpec(c.shape, lambda i: (0, 0), pipeline_mode=once) for c in consts]
    out_specs = [pl.BlockSpec((tm, nc), lambda i: (i, 0)) for nc, _ in outs] + [pl.BlockSpec(pc, lambda i: (0, 0)) for pc in parts]
    out_shape = [jax.ShapeDtypeStruct((M, nc), dt) for nc, dt in outs] + [jax.ShapeDtypeStruct(pc, F32) for pc in parts]
    return pl.pallas_call(
        body, grid=(gm,), in_specs=in_specs, out_specs=out_specs, out_shape=out_shape,
        compiler_params=_cp(("arbitrary",)), name=name,
    )(*a_segs, w, *[r for r, _ in rows], *consts)


def _rel_bucket_np(dist):
    max_exact = REL_BUCKETS // 2
    d = np.maximum(dist, 0)
    df = np.maximum(d, 1).astype(np.float32)
    large = max_exact + (np.log(df / np.float32(max_exact)) / np.float32(math.log(REL_MAX_DISTANCE / max_exact))
                         * np.float32(REL_BUCKETS - max_exact)).astype(np.int32)
    large = np.minimum(large, REL_BUCKETS - 1)
    return np.where(d < max_exact, d, large).astype(np.int32)


N_LAYOUTS = 2


def _band_index():
    idx = np.zeros((N_LAYOUTS * 3, 1, QBLK * 2 * QBLK), np.int32)
    for g, (window, dil) in enumerate(GROUPS):
        span = window // dil
        k = np.arange(2 * QBLK)[:, None]; q = np.arange(QBLK)[None, :]
        off = q - k + QBLK
        idx[g, 0] = np.where((off >= 0) & (off <= span), _rel_bucket_np(off * dil), -1).reshape(-1)
        k = np.arange(QBLK)[:, None]; q = np.arange(2 * QBLK)[None, :]
        off = q - k
        idx[3 + g, 0] = np.where((off >= 0) & (off <= span), _rel_bucket_np(off * dil), -1).reshape(-1)
    return idx


_NB = QBLK * 2 * QBLK
_BCH = 4096


def _bias_build(tab_t, idx):
    def body(t_ref, i_ref, o_ref):
        ix = i_ref[0]
        t = t_ref[0]
        acc = jnp.full((HEADS, _BCH), NEG_INF, F32)
        for b in range(REL_BUCKETS):
            acc = jnp.where(ix == b, t[:, b:b + 1], acc)
        o_ref[0] = acc

    return pl.pallas_call(
        body, grid=(N_LAYOUTS * 3, _NB // _BCH),
        in_specs=[pl.BlockSpec((1, HEADS, REL_BUCKETS), lambda l, n: (l % 3, 0, 0)),
                  pl.BlockSpec((1, 1, _BCH), lambda l, n: (l, 0, n))],
        out_specs=pl.BlockSpec((1, HEADS, _BCH), lambda l, n: (l, 0, n)),
        out_shape=jax.ShapeDtypeStruct((N_LAYOUTS * 3, HEADS, _NB), F32), compiler_params=_cp(), name="bias_build",
    )(tab_t, idx)


def _bias_grad(ds, idx):
    nch = _NB // _BCH

    def body(d_ref, i_ref, o_ref):
        n = pl.program_id(1)

        @pl.when(n == 0)
        def _():
            o_ref[...] = jnp.zeros(o_ref.shape, F32)

        ix = i_ref[0]
        d = d_ref[0]
        lane = lax.broadcasted_iota(jnp.int32, (HEADS, 128), 1)
        acc = jnp.zeros((HEADS, 128), F32)
        for b in range(REL_BUCKETS):
            s = jnp.sum(jnp.where(ix == b, d, 0.0), axis=1, keepdims=True)
            acc = acc + jnp.where(lane == b, s, 0.0)
        o_ref[0] += acc

    return pl.pallas_call(
        body, grid=(3, nch),
        in_specs=[pl.BlockSpec((1, HEADS, _BCH), lambda l, n: (l, 0, n)),
                  pl.BlockSpec((1, 1, _BCH), lambda l, n: (3 + l, 0, n))],
        out_specs=pl.BlockSpec((1, HEADS, 128), lambda l, n: (l, 0, 0)),
        out_shape=jax.ShapeDtypeStruct((3, HEADS, 128), F32), compiler_params=_cp(), name="bias_grad",
    )(ds, idx)


PT = 256
PSTEP = 1024
STAT_W = 128


def _perm_np(dil):
    p = np.zeros((PT, PT), np.float32)
    m = np.arange(PT // dil)
    for c in range(dil):
        p[c * (PT // dil) + m, m * dil + c] = 1.0
    return p


def _perm_const(dil, dtype, inverse):
    p = _perm_np(dil)
    return jnp.asarray(p.T if inverse else p, dtype)


def _apply_perm(p, x):
    if x.dtype == F32:
        return jnp.dot(p, x, preferred_element_type=F32, precision=lax.Precision.HIGHEST)
    return jnp.dot(p, x, preferred_element_type=F32)


def _to_residue(name, arr, col_blocks, dil):
    S = arr.shape[0]
    nc = len(col_blocks)
    p = _perm_const(dil, arr.dtype, False)
    sub = PT // dil

    def body(*refs):
        p_ref, ins, o_ref = refs[0], refs[1:1 + nc], refs[1 + nc]
        for u in range(PSTEP // PT):
            for t, r in enumerate(ins):
                y = _apply_perm(p_ref[...], r[u * PT:(u + 1) * PT, :]).astype(o_ref.dtype)
                o_ref[:, u * sub:(u + 1) * sub, t * GW:(t + 1) * GW] = y.reshape(dil, sub, GW)

    out = pl.pallas_call(
        body, grid=(S // PSTEP,),
        in_specs=[pl.BlockSpec((PT, PT), lambda i: (0, 0))]
                 + [pl.BlockSpec((PSTEP, GW), functools.partial(lambda i, cb: (i, cb), cb=cb)) for cb in col_blocks],
        out_specs=pl.BlockSpec((dil, PSTEP // dil, nc * GW), lambda i: (0, i, 0)),
        out_shape=jax.ShapeDtypeStruct((dil, S // dil, nc * GW), arr.dtype), compiler_params=_cp(), name=name,
    )(p, *([arr] * nc))
    return out.reshape(S, nc * GW)


def _to_residue_pair(name, do, stats, dil):
    S = do.shape[0]
    pb = _perm_const(dil, BF, False)
    pf = _perm_const(dil, F32, False)
    sub = PT // dil

    def body(pb_ref, pf_ref, d_ref, s_ref, od_ref, os_ref):
        for u in range(PSTEP // PT):
            rs = slice(u * PT, (u + 1) * PT)
            od_ref[:, u * sub:(u + 1) * sub, :] = _apply_perm(pb_ref[...], d_ref[rs, :]).astype(BF).reshape(dil, sub, GW)
            os_ref[:, u * sub:(u + 1) * sub, :] = _apply_perm(pf_ref[...], s_ref[rs, :]).reshape(dil, sub, STAT_W)

    cst = pl.BlockSpec((PT, PT), lambda i: (0, 0))
    od, os_ = pl.pallas_call(
        body, grid=(S // PSTEP,),
        in_specs=[cst, cst, pl.BlockSpec((PSTEP, GW), lambda i: (i, 0)), pl.BlockSpec((PSTEP, STAT_W), lambda i: (i, 0))],
        out_specs=[pl.BlockSpec((dil, PSTEP // dil, GW), lambda i: (0, i, 0)), pl.BlockSpec((dil, PSTEP // dil, STAT_W), lambda i: (0, i, 0))],
        out_shape=[jax.ShapeDtypeStruct((dil, S // dil, GW), BF), jax.ShapeDtypeStruct((dil, S // dil, STAT_W), F32)],
        compiler_params=_cp(), name=name,
    )(pb, pf, do, stats)
    return od.reshape(S, GW), os_.reshape(S, STAT_W)


def _from_residue(name, arr, dil):
    S, W = arr.shape
    p = _perm_const(dil, arr.dtype, True)
    sub = PT // dil

    def body(p_ref, x_ref, o_ref):
        for u in range(PSTEP // PT):
            x = x_ref[:, u * sub:(u + 1) * sub, :].reshape(PT, W)
            o_ref[u * PT:(u + 1) * PT, :] = _apply_perm(p_ref[...], x).astype(o_ref.dtype)

    return pl.pallas_call(
        body, grid=(S // PSTEP,),
        in_specs=[pl.BlockSpec((PT, PT), lambda i: (0, 0)), pl.BlockSpec((dil, PSTEP // dil, W), lambda i: (0, i, 0))],
        out_specs=pl.BlockSpec((PSTEP, W), lambda i: (i, 0)),
        out_shape=jax.ShapeDtypeStruct((S, W), arr.dtype), compiler_params=_cp(), name=name,
    )(p, arr.reshape(dil, S // dil, W))


PAIR_W = 2 * HEAD_DIM
NT_DIMS = (((1,), (1,)), ((), ()))
TN_DIMS = (((0,), (0,)), ((), ()))


def _attn_dims(S, dil):
    L = S // dil
    TQ = min(512, L)
    return L, TQ, L // TQ, TQ // QBLK


def _attn_specs(S, dil):
    L, TQ, nq, nsub = _attn_dims(S, dil)
    nb = L // QBLK
    cur = lambda cb, w=GW: pl.BlockSpec((TQ, w), lambda c, i: (c * nq + i, cb))
    prev = lambda cb, w=GW: pl.BlockSpec((QBLK, w), lambda c, i: (c * nb + jnp.maximum(i * nsub - 1, 0), cb))
    nxt = lambda cb, w=GW: pl.BlockSpec((QBLK, w), lambda c, i: (c * nb + jnp.minimum((i + 1) * nsub, nb - 1), cb))
    band = lambda r, c_: pl.BlockSpec((HEADS, r, c_), lambda c, i: (0, 0, 0))
    return L, TQ, nq, nsub, cur, prev, nxt, band


def _fill(buf, first_ref, second_ref):
    n = first_ref.shape[0]
    buf[0:n, :] = first_ref[...]
    buf[n:n + second_ref.shape[0], :] = second_ref[...]


def _attn_fwd(name, arr, bias_kq, cb, dil):
    S = arr.shape[0]
    kcb, vcb, qcb = cb
    L, TQ, nq, nsub, cur, prev, nxt, band = _attn_specs(S, dil)

    def body(q_ref, kc_ref, kp_ref, vc_ref, vp_ref, b_ref, o_ref, l_ref, kbuf, vbuf):
        i = pl.program_id(1)
        _fill(kbuf, kp_ref, kc_ref)
        _fill(vbuf, vp_ref, vc_ref)
        row = lax.broadcasted_iota(jnp.int32, (2 * QBLK, QBLK), 0)
        first = (row >= QBLK) | (i > 0)
        low = lax.broadcasted_iota(jnp.int32, (QBLK, PAIR_W), 1) < HEAD_DIM
        zero = jnp.zeros((QBLK, PAIR_W), BF)
        for j in range(nsub):
            rs = slice(j * QBLK, (j + 1) * QBLK)
            ks = slice(j * QBLK, (j + 2) * QBLK)
            lrows = []
            for hp in range(HEADS // 2):
                ps = slice(hp * PAIR_W, (hp + 1) * PAIR_W)
                qp = q_ref[rs, ps]
                kp = kbuf[ks, ps]
                vp = vbuf[ks, ps]
                halves = []
                for t in range(2):
                    qm = jnp.where(low if t == 0 else ~low, qp, zero)
                    s = lax.dot_general(kp, qm, NT_DIMS, preferred_element_type=F32) * SCALE + b_ref[2 * hp + t]
                    if j == 0:
                        s = jnp.where(first, s, NEG_INF)
                    m = jnp.max(s, axis=0, keepdims=True)
                    p = jnp.exp(s - m)
                    den = jnp.sum(p, axis=0, keepdims=True)
                    o2 = lax.dot_general(vp, p.astype(BF), TN_DIMS, preferred_element_type=F32)
                    halves.append(o2[t * HEAD_DIM:(t + 1) * HEAD_DIM, :] / den)
                    lrows.append(m + jnp.log(den))
                o_ref[rs, ps] = jnp.concatenate(halves, axis=0).T.astype(BF)
            lt = jnp.concatenate(lrows + [jnp.zeros((STAT_W - HEADS, QBLK), F32)], axis=0)
            l_ref[rs, :] = lt.T

    return pl.pallas_call(
        body, grid=(dil, nq),
        in_specs=[cur(qcb), cur(kcb), prev(kcb), cur(vcb), prev(vcb), band(2 * QBLK, QBLK)],
        out_specs=[cur(0), cur(0, STAT_W)],
        out_shape=[jax.ShapeDtypeStruct((S, GW), BF), jax.ShapeDtypeStruct((S, STAT_W), F32)],
        scratch_shapes=[pltpu.VMEM((QBLK + TQ, GW), BF), pltpu.VMEM((QBLK + TQ, GW), BF)],
        compiler_params=_cp(), name=name,
    )(arr, arr, arr, arr, arr, bias_kq)


def _attn_bwd(name, arr, bias_kq2, do, stats, cb, dil):
    S = arr.shape[0]
    kcb, vcb, qcb = cb
    L, TQ, nq, nsub, cur, prev, nxt, band = _attn_specs(S, dil)

    def body(k_ref, v_ref, qc_ref, qn_ref, b_ref, doc_ref, don_ref, sc_ref, sn_ref, o_ref, db_ref, qbuf, dobuf, sbuf, carry):
        c = pl.program_id(0)
        i = pl.program_id(1)

        @pl.when((c == 0) & (i == 0))
        def _():
            db_ref[...] = jnp.zeros(db_ref.shape, F32)
            carry[...] = jnp.zeros(carry.shape, F32)

        _fill(qbuf, qc_ref, qn_ref)
        _fill(dobuf, doc_ref, don_ref)
        for j in range(nsub + 1):
            rs = slice(j * QBLK, (j + 1) * QBLK)
            sbuf[:, rs] = (sc_ref[rs, :] if j < nsub else sn_ref[...]).T
        col = lax.broadcasted_iota(jnp.int32, (QBLK, 2 * QBLK), 1)
        last = (col < QBLK) | (i < nq - 1)
        low = lax.broadcasted_iota(jnp.int32, (QBLK, PAIR_W), 1) < HEAD_DIM
        zero = jnp.zeros((QBLK, PAIR_W), BF)
        for hp in range(HEADS // 2):
            ps = slice(hp * PAIR_W, (hp + 1) * PAIR_W)
            dbs = [jnp.zeros((QBLK, 2 * QBLK), F32), jnp.zeros((QBLK, 2 * QBLK), F32)]
            tail = carry[:, ps]
            for j in range(nsub):
                rs = slice(j * QBLK, (j + 1) * QBLK)
                qs = slice(j * QBLK, (j + 2) * QBLK)
                qp = qbuf[qs, ps]
                dd = dobuf[qs, ps]
                kp = k_ref[rs, ps]
                vp = v_ref[rs, ps]
                kt = kp.T
                dk, dv, dqt = [], [], []
                for t in range(2):
                    h = 2 * hp + t
                    sel = low if t == 0 else ~low
                    s = lax.dot_general(jnp.where(sel, kp, zero), qp, NT_DIMS, preferred_element_type=F32) * SCALE + b_ref[h]
                    if j == nsub - 1:
                        s = jnp.where(last, s, NEG_INF)
                    p = jnp.exp(s - sbuf[h:h + 1, qs])
                    dp = lax.dot_general(jnp.where(sel, vp, zero), dd, NT_DIMS, preferred_element_type=F32)
                    ds = p * (dp - sbuf[HEADS + h:HEADS + h + 1, qs])
                    dbs[t] = dbs[t] + ds
                    dsb = ds.astype(BF)
                    dk.append(jnp.dot(dsb, qp, preferred_element_type=F32))
                    dv.append(jnp.dot(p.astype(BF), dd, preferred_element_type=F32))
                    dqt.append(jnp.dot(kt[t * HEAD_DIM:(t + 1) * HEAD_DIM, :], dsb, preferred_element_type=F32))
                o_ref[rs, ps] = (jnp.where(low, dk[0], dk[1]) * SCALE).astype(BF)
                o_ref[rs, GW + hp * PAIR_W:GW + (hp + 1) * PAIR_W] = jnp.where(low, dv[0], dv[1]).astype(BF)
                dq2 = jnp.concatenate(dqt, axis=0).T * SCALE
                o_ref[rs, 2 * GW + hp * PAIR_W:2 * GW + (hp + 1) * PAIR_W] = (dq2[0:QBLK] + tail).astype(BF)
                tail = dq2[QBLK:2 * QBLK]
            carry[:, ps] = tail
            db_ref[2 * hp] += dbs[0]
            db_ref[2 * hp + 1] += dbs[1]

    return pl.pallas_call(
        body, grid=(dil, nq),
        in_specs=[cur(kcb), cur(vcb), cur(qcb), nxt(qcb), band(QBLK, 2 * QBLK),
                  cur(0), nxt(0), cur(0, STAT_W), nxt(0, STAT_W)],
        out_specs=[cur(0, ATTN_W), band(QBLK, 2 * QBLK)],
        out_shape=[jax.ShapeDtypeStruct((S, ATTN_W), BF), jax.ShapeDtypeStruct((HEADS, QBLK, 2 * QBLK), F32)],
        scratch_shapes=[pltpu.VMEM((TQ + QBLK, GW), BF), pltpu.VMEM((TQ + QBLK, GW), BF), pltpu.VMEM((STAT_W, TQ + QBLK), F32),
                        pltpu.VMEM((QBLK, GW), F32)],
        compiler_params=_cp(("arbitrary", "arbitrary")), name=name,
    )(arr, arr, arr, arr, bias_kq2, do, do, stats, stats)


def _head_expand():
    e = np.zeros((STAT_W, GW), np.float32)
    for h in range(HEADS):
        e[h, h * HEAD_DIM:(h + 1) * HEAD_DIM] = 1.0
    return e


def _attn_merge(os_, ls_, S):
    dils = [d for _, d in GROUPS]
    pb = [_perm_const(d, BF, True) for d in dils[1:]]
    pf = [_perm_const(d, F32, True) for d in dils[1:]]
    expand = jnp.asarray(_head_expand(), BF)

    def body(o0, o1, o2, l0, l1, l2, pb1, pb2, pf1, pf2, e_ref, o_ref, l_ref):
        for u in range(PSTEP // PT):
            rs = slice(u * PT, (u + 1) * PT)
            res = lambda r, d: r[:, u * (PT // d):(u + 1) * (PT // d), :].reshape(PT, r.shape[2])
            ov = [o0[rs, :].astype(F32), _apply_perm(pb1[...], res(o1, dils[1])), _apply_perm(pb2[...], res(o2, dils[2]))]
            lv = [l0[rs, :], _apply_perm(pf1[...], res(l1, dils[1])), _apply_perm(pf2[...], res(l2, dils[2]))]
            m = jnp.maximum(jnp.maximum(lv[0], lv[1]), lv[2])
            ev = [jnp.exp(l - m) for l in lv]
            den = ev[0] + ev[1] + ev[2]
            acc = jnp.zeros((PT, GW), F32)
            for g in range(3):
                wide = jnp.dot((ev[g] / den).astype(BF), e_ref[...], preferred_element_type=F32)
                acc = acc + wide * ov[g]
            o_ref[rs, :] = acc.astype(BF)
            l_ref[rs, :] = m + jnp.log(den)

    nat = lambda w: pl.BlockSpec((PSTEP, w), lambda i: (i, 0))
    res = lambda d, w: pl.BlockSpec((d, PSTEP // d, w), lambda i: (0, i, 0))
    cst = lambda a: pl.BlockSpec(a.shape, lambda i: (0, 0))
    args = [os_[0], os_[1].reshape(dils[1], S // dils[1], GW), os_[2].reshape(dils[2], S // dils[2], GW),
            ls_[0], ls_[1].reshape(dils[1], S // dils[1], STAT_W), ls_[2].reshape(dils[2], S // dils[2], STAT_W),
            pb[0], pb[1], pf[0], pf[1], expand]
    return pl.pallas_call(
        body, grid=(S // PSTEP,),
        in_specs=[nat(GW), res(dils[1], GW), res(dils[2], GW), nat(STAT_W), res(dils[1], STAT_W), res(dils[2], STAT_W)]
                 + [cst(a) for a in args[6:]],
        out_specs=[nat(GW), nat(STAT_W)],
        out_shape=[jax.ShapeDtypeStruct((S, GW), BF), jax.ShapeDtypeStruct((S, STAT_W), F32)],
        compiler_params=_cp(), name="attn_merge",
    )(*args)


CT = 512
CBUF = HALO + CT + 8
RG = 4


def _ln_hat(u1):
    mu = jnp.mean(u1, axis=-1, keepdims=True)
    xc = u1 - mu
    rstd = lax.rsqrt(jnp.mean(xc * xc, axis=-1, keepdims=True) + LN_EPS)
    return xc * rstd, rstd


def _glu_window(hu_ref, hg_ref, huh_ref, hgh_ref, bglu_ref, buf_ref, i):
    bu = bglu_ref[:, 0:D]
    bg = bglu_ref[:, D:2 * D]
    uh = (huh_ref[...].astype(F32) + bu) * _sig(hgh_ref[...].astype(F32) + bg)
    buf_ref[0:HALO, :] = jnp.where(i > 0, uh, 0.0)
    a = hu_ref[...].astype(F32) + bu
    s = _sig(hg_ref[...].astype(F32) + bg)
    buf_ref[HALO:HALO + CT, :] = a * s
    buf_ref[HALO + CT:CBUF, :] = jnp.zeros((8, D), F32)
    return a, s


def _shift_copies(buf_ref, sh_ref):
    for r in range(8):
        sh_ref[r] = buf_ref[r:r + HALO + CT, :]


def _tap_rows(wb_ref, w_ref):
    for j in range(CONV_W):
        wb_ref[j * 8:(j + 1) * 8, :] = jnp.broadcast_to(w_ref[j:j + 1, :], (8, D))


def _conv_taps(sh_ref, wb_ref, out_ref, init, offset):
    for rg in range(CT // (8 * RG)):
        accs = [init] * RG
        for j in range(CONV_W):
            off = offset(j)
            wj = wb_ref[j * 8:(j + 1) * 8, :]
            for q in range(RG):
                row = 8 * (rg * RG + q + off // 8)
                accs[q] = accs[q] + wj * sh_ref[off % 8, row:row + 8, :]
        for q in range(RG):
            out_ref[(rg * RG + q) * 8:(rg * RG + q + 1) * 8, :] = accs[q]


def _conv_specs(S):
    cur = lambda cb: pl.BlockSpec((CT, D), lambda i: (i, cb))
    halo = lambda cb: pl.BlockSpec((HALO, D), lambda i: (jnp.maximum(i * (CT // HALO) - 1, 0), cb))
    full = lambda shp: pl.BlockSpec(shp, lambda i: (0, 0))
    return cur, halo, full


def _conv_fwd(z, b_glu, w_dw, b_dw, g_ln, b_ln):
    S = z.shape[0]
    cur, halo, full = _conv_specs(S)

    def body(hu, hg, huh, hgh, bglu, w, bdw, gln, bln, u1_ref, u3_ref, buf, sh, wb):
        i = pl.program_id(0)

        @pl.when(i == 0)
        def _():
            _tap_rows(wb, w)

        _glu_window(hu, hg, huh, hgh, bglu, buf, i)
        _shift_copies(buf, sh)
        _conv_taps(sh, wb, u1_ref, jnp.broadcast_to(bdw[...], (8, D)), lambda j: 2 + j)
        xh, _ = _ln_hat(u1_ref[...])
        u2 = xh * gln[...] + bln[...]
        u3_ref[...] = (u2 * _sig(u2)).astype(BF)

    return pl.pallas_call(
        body, grid=(S // CT,),
        in_specs=[cur(0), cur(1), halo(0), halo(1), full((1, 2 * D)), full((HALO, D)), full((1, D)), full((1, D)), full((1, D))],
        out_specs=[pl.BlockSpec((CT, D), lambda i: (i, 0))] * 2,
        out_shape=[jax.ShapeDtypeStruct((S, D), F32), jax.ShapeDtypeStruct((S, D), BF)],
        scratch_shapes=[pltpu.VMEM((CBUF, D), F32), pltpu.VMEM((8, HALO + CT, D), F32), pltpu.VMEM((CONV_W * 8, D), F32)],
        compiler_params=_cp(("arbitrary",)), name="conv_fwd",
    )(z, z, z, z, b_glu, w_dw, b_dw, g_ln, b_ln)


def _conv_bwd(du1, z, b_glu, w_dw):
    S = z.shape[0]
    n = S // CT
    cur, halo, full = _conv_specs(S)
    RG2 = 2

    def body(du, dun, hu, hg, bglu, w, dz_ref, dw_ref, dbg_ref, bufd, shd, wb, u0_ref, du0_ref, dwacc):
        i = pl.program_id(0)

        @pl.when(i == 0)
        def _():
            _tap_rows(wb, w)
            dwacc[...] = jnp.zeros(dwacc.shape, F32)
            dbg_ref[...] = jnp.zeros(dbg_ref.shape, F32)

        a = hu[...].astype(F32) + bglu[:, 0:D]
        s = _sig(hg[...].astype(F32) + bglu[:, D:2 * D])
        u0_ref[...] = a * s
        bufd[0:CT, :] = du[...]
        bufd[CT:CT + HALO, :] = jnp.where(i < n - 1, dun[...], 0.0)
        bufd[CT + HALO:CBUF, :] = jnp.zeros((8, D), F32)
        _shift_copies(bufd, shd)
        for rg in range(CT // (8 * RG2)):
            uch = [u0_ref[(rg * RG2 + q) * 8:(rg * RG2 + q + 1) * 8, :] for q in range(RG2)]
            accs = [jnp.zeros((8, D), F32)] * RG2
            for j in range(CONV_W):
                off = 30 - j
                wj = wb[j * 8:(j + 1) * 8, :]
                dwj = dwacc[j * 8:(j + 1) * 8, :]
                for q in range(RG2):
                    row = 8 * (rg * RG2 + q + off // 8)
                    x = shd[off % 8, row:row + 8, :]
                    accs[q] = accs[q] + wj * x
                    dwj = dwj + uch[q] * x
                dwacc[j * 8:(j + 1) * 8, :] = dwj
            for q in range(RG2):
                du0_ref[(rg * RG2 + q) * 8:(rg * RG2 + q + 1) * 8, :] = accs[q]
        du0 = du0_ref[...]
        dhu = du0 * s
        dhg = du0 * a * s * (1.0 - s)
        dz_ref[:, 0:D] = dhu.astype(BF)
        dz_ref[:, D:2 * D] = dhg.astype(BF)
        dbg_ref[:, 0:D] += _psum8(dhu)
        dbg_ref[:, D:2 * D] += _psum8(dhg)

        @pl.when(i == n - 1)
        def _():
            dbg_ref[0:1, :] = jnp.sum(dbg_ref[...], axis=0, keepdims=True)
            for j in range(CONV_W):
                dw_ref[j:j + 1, :] = jnp.sum(dwacc[j * 8:(j + 1) * 8, :], axis=0, keepdims=True)
            dw_ref[CONV_W:HALO, :] = jnp.zeros((HALO - CONV_W, D), F32)

    nxt = pl.BlockSpec((HALO, D), lambda i: (jnp.minimum((i + 1) * (CT // HALO), S // HALO - 1), 0))
    return pl.pallas_call(
        body, grid=(n,),
        in_specs=[pl.BlockSpec((CT, D), lambda i: (i, 0)), nxt, cur(0), cur(1), full((1, 2 * D)), full((HALO, D))],
        out_specs=[pl.BlockSpec((CT, 2 * D), lambda i: (i, 0)), full((HALO, D)), full((8, 2 * D))],
        out_shape=[jax.ShapeDtypeStruct((S, 2 * D), BF), jax.ShapeDtypeStruct((HALO, D), F32), jax.ShapeDtypeStruct((8, 2 * D), F32)],
        scratch_shapes=[pltpu.VMEM((CBUF, D), F32), pltpu.VMEM((8, HALO + CT, D), F32), pltpu.VMEM((CONV_W * 8, D), F32),
                        pltpu.VMEM((CT, D), F32), pltpu.VMEM((CT, D), F32), pltpu.VMEM((CONV_W * 8, D), F32)],
        compiler_params=_cp(("arbitrary",)), name="conv_bwd",
    )(du1, du1, z, z, b_glu, w_dw)


MESH = pl.DeviceIdType.MESH


def _all_gather(name, shards):
    n = len(shards)

    def body(*refs):
        ins, outs = refs[:n], refs[n:2 * n]
        send_sems, recv_sems, local_sems = refs[2 * n:]
        x, y, c = lax.axis_index("x"), lax.axis_index("y"), lax.axis_index("c")
        me, sibling = (x, y, c), (x, y, 1 - c)
        chips = [(1 - x, y), (x, 1 - y), (1 - x, 1 - y)]

        def slot(a, px, py, pc):
            return outs[a].at[4 * px + 2 * py + pc]

        def copy(a, k, block, to, src=None):
            return pltpu.make_async_remote_copy(
                src_ref=slot(a, *block) if src is None else src, dst_ref=slot(a, *block),
                send_sem=send_sems.at[a, k], recv_sem=recv_sems.at[a, k], device_id=to, device_id_type=MESH)

        mine = [pltpu.make_async_copy(ins[a], slot(a, *me), local_sems.at[a]) for a in range(n)]
        for cp in mine:
            cp.start()
        first = []
        for a in range(n):
            first.append(copy(a, 0, me, sibling, src=ins[a]))
            first += [copy(a, 1 + j, me, (*chip, c), src=ins[a]) for j, chip in enumerate(chips)]
        for cp in first:
            cp.start()
        passed = []
        for j, chip in enumerate(chips):
            for a in range(n):
                copy(a, 1 + j, (*chip, c), me).wait_recv()
                fwd = copy(a, 4 + j, (*chip, c), sibling)
                fwd.start()
                passed.append(fwd)
        for a in range(n):
            copy(a, 0, sibling, me).wait_recv()
        for j, chip in enumerate(chips):
            for a in range(n):
                copy(a, 4 + j, (*chip, 1 - c), me).wait_recv()
        for cp in first + passed:
            cp.wait_send()
        for cp in mine:
            cp.wait()

    anyspec = pl.BlockSpec(memory_space=pl.ANY)
    return pl.pallas_call(
        body, in_specs=[anyspec] * n, out_specs=[anyspec] * n,
        out_shape=[jax.ShapeDtypeStruct((NDEV,) + s.shape, s.dtype) for s in shards],
        scratch_shapes=[pltpu.SemaphoreType.DMA((n, 7)), pltpu.SemaphoreType.DMA((n, 7)), pltpu.SemaphoreType.DMA((n,))],
        name=name,
    )(*shards)


HBM_SPEC = pl.BlockSpec(memory_space=pltpu.HBM)
SEM_SPEC = pl.BlockSpec(memory_space=pltpu.SEMAPHORE)
DATAFLOW = pltpu.SideEffectType.DATAFLOW_SIDE_EFFECTING


def _peers():
    x, y, c = lax.axis_index("x"), lax.axis_index("y"), lax.axis_index("c")
    out = []
    for k in range(1, NDEV):
        px = 1 - x if k & 4 else x
        py = 1 - y if k & 2 else y
        pc = 1 - c if k & 1 else c
        out.append(((px, py, pc), 4 * px + 2 * py + pc))
    return 4 * x + 2 * y + c, out


def _exchange_copies(srcs, lands, send_sems, recv_sems, gather):
    my, peers = _peers()
    pairs = []
    for k, (dev, pid) in enumerate(peers):
        for a in range(len(srcs)):
            src = srcs[a] if gather else srcs[a].at[pid]
            sems = dict(send_sem=send_sems[a * (NDEV - 1) + k], recv_sem=recv_sems[a * (NDEV - 1) + k], device_id=dev,
                        device_id_type=MESH)
            pairs.append((pltpu.make_async_remote_copy(src_ref=src, dst_ref=lands[a].at[my], **sems),
                          pltpu.make_async_remote_copy(src_ref=src, dst_ref=lands[a].at[pid], **sems)))
    return pairs


def _exchange_start(name, srcs, gather):
    n = len(srcs)
    ns = n * (NDEV - 1)
    shapes = [(s.shape if gather else s.shape[1:]) for s in srcs]
    lands = [lax.empty((NDEV,) + shp, s.dtype) for shp, s in zip(shapes, srcs)]

    def body(*refs):
        src_refs, land_refs = refs[:n], refs[n:2 * n]
        send_sems, recv_sems = refs[2 * n:2 * n + ns], refs[2 * n + ns:2 * n + 2 * ns]
        token = refs[-1]
        for mine, _ in _exchange_copies(src_refs, land_refs, send_sems, recv_sems, gather):
            mine.start()
        token[...] = jnp.zeros(token.shape, token.dtype)

    hbm = lambda a: pltpu.HBM(a.shape, a.dtype)
    res = pl.pallas_call(
        body, name=name,
        out_shape=(*([pltpu.SemaphoreType.DMA(())] * (2 * ns)), *[hbm(s) for s in srcs], *[hbm(l) for l in lands],
                   jax.ShapeDtypeStruct((8, 128), F32)),
        in_specs=[HBM_SPEC] * (2 * n),
        out_specs=(*([SEM_SPEC] * (2 * ns)), *([HBM_SPEC] * (2 * n)), pl.BlockSpec(memory_space=pltpu.VMEM)),
        input_output_aliases={i: 2 * ns + i for i in range(2 * n)},
        compiler_params=pltpu.CompilerParams(has_side_effects=DATAFLOW),
    )(*[pltpu.with_memory_space_constraint(s, pltpu.HBM) for s in srcs],
      *[pltpu.with_memory_space_constraint(l, pltpu.HBM) for l in lands])
    return list(res[:ns]), list(res[ns:2 * ns]), list(res[2 * ns:2 * ns + n]), list(res[2 * ns + n:2 * ns + 2 * n]), res[-1]


def _exchange_wait(name, handle, after, gather):
    send_sems, recv_sems, srcs, lands, _ = handle
    n = len(srcs)
    ns = n * (NDEV - 1)

    def body(*refs):
        src_refs, land_refs = refs[:n], refs[n:2 * n]
        s_sems, r_sems = refs[2 * n:2 * n + ns], refs[2 * n + ns:2 * n + 2 * ns]
        for mine, theirs in _exchange_copies(src_refs, land_refs, s_sems, r_sems, gather):
            mine.wait_send()
            theirs.wait_recv()

    hbm = lambda a: pltpu.HBM(a.shape, a.dtype)
    res = pl.pallas_call(
        body, name=name,
        out_shape=(*[hbm(s) for s in srcs], *[hbm(l) for l in lands]),
        in_specs=[HBM_SPEC] * (2 * n) + [SEM_SPEC] * (2 * ns) + [pl.BlockSpec(memory_space=pl.ANY)],
        out_specs=tuple([HBM_SPEC] * (2 * n)),
        input_output_aliases={i: i for i in range(2 * n)},
        compiler_params=pltpu.CompilerParams(has_side_effects=DATAFLOW),
    )(*srcs, *lands, *send_sems, *recv_sems, after)
    return list(res[n:])


def _set_own_slot(land, own):
    my = 4 * lax.axis_index("x") + 2 * lax.axis_index("y") + lax.axis_index("c")
    return lax.dynamic_update_slice(land, own[None], (my, 0, 0))


def _own_block(blocks):
    my = 4 * lax.axis_index("x") + 2 * lax.axis_index("y") + lax.axis_index("c")
    return lax.dynamic_index_in_dim(blocks, my, axis=0, keepdims=False)


_C1 = 1.0 - ADAM_B1 ** ADAM_STEP
_C2 = 1.0 - ADAM_B2 ** ADAM_STEP


def _adamw(name, w, m, v, recv, tr):
    R, C = w.shape

    def body(w_ref, m_ref, v_ref, r_ref, g_ref, d_ref, nm_ref, nv_ref):
        g = r_ref[0].astype(F32)
        for s in range(1, NDEV):
            g = g + r_ref[s].astype(F32)
        wv = w_ref[...]
        nm = ADAM_B1 * m_ref[...] + (1.0 - ADAM_B1) * g
        nv = ADAM_B2 * v_ref[...] + (1.0 - ADAM_B2) * (g * g)
        m_hat = nm / _C1
        v_hat = nv / _C2
        g_ref[...] = g
        d_ref[...] = -ADAM_LR * (m_hat / (jnp.sqrt(v_hat) + ADAM_EPS) + ADAM_WD * wv)
        nm_ref[...] = nm
        nv_ref[...] = nv

    blk = pl.BlockSpec((tr, C), lambda i: (i, 0))
    return pl.pallas_call(
        body, grid=(R // tr,), in_specs=[blk, blk, blk, pl.BlockSpec((NDEV, tr, C), lambda i: (0, i, 0))],
        out_specs=[blk] * 4, out_shape=[jax.ShapeDtypeStruct((R, C), F32)] * 4,
        compiler_params=_cp(), name=name,
    )(w, m, v, recv)


def _row(v):
    return v.reshape(1, -1)


def _local_step(xs, tgt, Wp, rest_fn, early_fn, late_fn, rel_bias_table, g_pre_mix, b_glu, b_dw, g_conv_ln,
                b_conv_ln, b_conv_out, g_post_mix, g_pre_ffn, g_post_ffn):
    S = xs.shape[0]
    g1, g2, g3, g4 = _row(g_pre_mix), _row(g_post_mix), _row(g_pre_ffn), _row(g_post_ffn)
    bglu, bdw, gln, bln, bco = _row(b_glu), _row(b_dw), _row(g_conv_ln), _row(b_conv_ln), _row(b_conv_out)
    full = (D, F32, D, 0, False)
    fullb = (D, BF, D, 0, False)

    def epi_rms(accs, r, c, o, p):
        v = r[0][...]
        o[0][...] = (v * _rms_r(v) * c[0][...]).astype(BF)

    (h1,) = _fused_mm("rms_in", S, 512, 1, [], [], [], [], [(xs, D, 0, False)], [g1], [fullb], [], epi_rms)

    def epi_cast(accs, r, c, o, p):
        o[0][...] = accs[0].astype(BF)

    ZT = IN_W // 4
    (z,) = _fused_mm("in_proj", S, 1024, 4, [(h1, D, 0, 1)], [(Wp, False, D, ZT, 0, 1, 0, 0, True)], [(0, 0, 0, 0, 1)],
                     [(1024, ZT)], [], [], [(IN_W, BF, ZT, 0, True)], [], epi_cast, n_outer=True)

    idx = jnp.asarray(_band_index())
    tab_t = rel_bias_table.T.reshape(3, HEADS, REL_BUCKETS)
    bias_all = _bias_build(tab_t, idx)
    bias_kq = [bias_all[g].reshape(HEADS, 2 * QBLK, QBLK) for g in range(3)]
    bias_kq2 = [bias_all[3 + g].reshape(HEADS, QBLK, 2 * QBLK) for g in range(3)]
    dils = [d for _, d in GROUPS]
    qkv = [(z, _kvq_blocks(0))] + [(_to_residue(f"qkv_to_residue_g{g}", z, _kvq_blocks(g), dils[g]), (0, 1, 2)) for g in (1, 2)]
    os_, ls_ = [], []
    for g in range(3):
        o_g, l_g = _attn_fwd(f"attn_fwd_g{g}", qkv[g][0], bias_kq[g], qkv[g][1], dils[g])
        os_.append(o_g)
        ls_.append(l_g)
    o_att, lse = _attn_merge(os_, ls_, S)

    Wfi, Wfo, Wco, Wmo, Wao, wdw = rest_fn(lse)
    u1, u3 = _conv_fwd(z, bglu, wdw, bdw, gln, bln)

    def epi_mix(accs, r, c, o, p):
        ya = accs[0]
        yc = accs[1] + c[0][...]
        mg = _sig(r[0][...].astype(F32)) * ya + _sig(r[1][...].astype(F32)) * yc
        mgb = mg.astype(BF)
        m2 = jnp.dot(mgb, c[1][...], preferred_element_type=F32)
        x1 = r[2][...] + m2 * _rms_r(m2) * c[2][...]
        o[0][...] = ya.astype(BF)
        o[1][...] = yc.astype(BF)
        o[2][...] = mgb
        o[3][...] = m2.astype(BF)
        o[4][...] = x1
        o[5][...] = (x1 * _rms_r(x1) * c[3][...]).astype(BF)

    y_attn, y_conv, merged, m2, x1, h2 = _fused_mm(
        "mix_fwd", S, 512, 1, [(o_att, GW, 0, 1), (u3, D, 0, 1)],
        [(Wao, False, GW, D, 0, 1, 0, 0, False), (Wco, False, D, D, 0, 1, 0, 0, False)], [(0, 0, 0, 0, 1), (1, 1, 1, 0, 1)],
        [(512, D), (512, D)], [(z, D, 2, False), (z, D, 3, False), (xs, D, 0, False)], [bco, Wmo, g2, g3],
        [fullb, fullb, fullb, fullb, full, fullb], [], epi_mix)

    HN = FFN // 2

    def epi_ffn_in(accs, r, c, o, p):
        gt, up = accs
        o[0][...] = gt.astype(BF)
        o[1][...] = up.astype(BF)
        o[2][...] = (gt * _sig(gt) * up).astype(BF)

    gate, up, act = _fused_mm(
        "ffn_in", S, 512, 2, [(h2, D, 0, 1)],
        [(Wfi, False, D, HN, 0, 1, 0, 0, True), (Wfi, False, D, HN, 0, 1, 0, 2, True)], [(0, 0, 0, 0, 1), (0, 1, 1, 0, 1)],
        [(512, HN), (512, HN)], [], [], [(FFN, BF, HN, 0, True)] * 3, [], epi_ffn_in, n_outer=True)

    def epi_loss(accs, r, c, o, p):
        f2 = accs[0]
        g = c[0][...]
        rr = _rms_r(f2)
        err = r[0][...] + f2 * rr * g - r[1][...]
        dy = err * (1.0 / D)
        df2, dgr = _rms_bwd(f2, rr, g, dy)
        o[0][...] = dy
        o[1][...] = df2.astype(BF)
        p[0][...] += _psum8(err * err)
        p[1][...] += _psum8(dgr)

    dy, df2, loss_p, dg4 = _fused_mm(
        "ffn_out_loss", S, 512, 1, [(act, FFN, 0, 1)], [(Wfo, False, FFN, D, 0, 1, 0, 0, False)], [(0, 0, 0, 0, 1)],
        [(512, D)], [(x1, D, 0, False), (tgt, D, 0, False)], [g4], [full, fullb], [(8, D), (8, D)], epi_loss)

    def epi_swiglu(accs, r, c, o, p):
        da = accs[0]
        gt = r[0][...].astype(F32)
        sg = _sig(gt)
        o[0][...] = (da * r[1][...].astype(F32) * sg * (1.0 + gt * (1.0 - sg))).astype(BF)
        o[1][...] = (da * gt * sg).astype(BF)

    dgate, dup = _fused_mm(
        "ffn_out_bwd", S, 512, 2, [(df2, D, 0, 1)], [(Wfo, True, D, HN, 0, 1, 0, 0, True)], [(0, 0, 0, 0, 1)],
        [(512, HN)], [(gate, HN, 0, True), (up, HN, 0, True)], [], [(FFN, BF, HN, 0, True)] * 2, [], epi_swiglu, n_outer=True)
    dWfo = _mm_tn("dw_ffn_out", act, df2, HN, D, WG_TK)

    def epi_dh2(accs, r, c, o, p):
        dh2 = accs[0]
        x1v = r[1][...]
        r3 = _rms_r(x1v)
        d1, dg3r = _rms_bwd(x1v, r3, c[0][...], dh2)
        dx1 = r[0][...] + d1
        m2v = r[2][...].astype(F32)
        r2 = _rms_r(m2v)
        dm2, dg2r = _rms_bwd(m2v, r2, c[1][...], dx1)
        o[0][...] = dx1
        o[1][...] = dm2.astype(BF)
        p[0][...] += _psum8(dg3r)
        p[1][...] += _psum8(dg2r)

    dx1, dm2, dg3, dg2 = _resident_mm(
        "ffn_in_bwd", S, 512, [dgate, dup], Wfi, [(dy, D), (x1, D), (m2, D)], [g3, g2], [(D, F32), (D, BF)], [(8, D), (8, D)], epi_dh2)
    dWfi = jnp.concatenate([_mm_tn("dw_ffn_gate", h2, dgate, D, HN, WG_TK), _mm_tn("dw_ffn_up", h2, dup, D, HN, WG_TK)], axis=1)

    def epi_dmix(accs, r, c, o, p):
        dm = accs[0]
        sa = _sig(r[0][...].astype(F32))
        sc = _sig(r[1][...].astype(F32))
        o[0][...] = (dm * sa).astype(BF)
        o[1][...] = (dm * sc).astype(BF)
        o[2][:, 0:D] = (dm * r[2][...].astype(F32) * sa * (1.0 - sa)).astype(BF)
        o[2][:, D:2 * D] = (dm * r[3][...].astype(F32) * sc * (1.0 - sc)).astype(BF)

    dy_attn, dy_conv, dz_gate = _fused_mm(
        "mix_bwd", S, 512, 1, [(dm2, D, 0, 1)], [(Wmo, True, D, D, 0, 1, 0, 0, False)], [(0, 0, 0, 0, 1)], [(512, D)],
        [(z, D, 2, False), (z, D, 3, False), (y_attn, D, 0, False), (y_conv, D, 0, False)], [],
        [fullb, fullb, (2 * D, BF, 2 * D, 0, False)], [], epi_dmix)
    dWmo = _mm_tn("dw_mix_out", merged, dm2, D, D, WG_TK)

    def epi_dconv(accs, r, c, o, p):
        du3 = accs[0]
        xh, rstd = _ln_hat(r[0][...])
        gl = c[0][...]
        u2 = xh * gl + c[1][...]
        sg = _sig(u2)
        du2 = du3 * sg * (1.0 + u2 * (1.0 - sg))
        dxh = du2 * gl
        du1 = rstd * (dxh - jnp.mean(dxh, axis=-1, keepdims=True) - xh * jnp.mean(dxh * xh, axis=-1, keepdims=True))
        o[0][...] = du1
        p[0][...] += _psum8(du2 * xh)
        p[1][...] += _psum8(du2)
        p[2][...] += _psum8(du1)
        p[3][...] += _psum8(r[1][...].astype(F32))

    du1, dgln, dbln, dbdw, dbco = _fused_mm(
        "conv_out_bwd", S, 512, 1, [(dy_conv, D, 0, 1)], [(Wco, True, D, D, 0, 1, 0, 0, False)], [(0, 0, 0, 0, 1)], [(512, D)],
        [(u1, D, 0, False), (dy_conv, D, 0, False)], [gln, bln], [full], [(8, D)] * 4, epi_dconv)
    dWco = _mm_tn("dw_conv_out", u3, dy_conv, D, D, WG_TK)
    dz_glu, dwdw, dbglu = _conv_bwd(du1, z, bglu, wdw)

    head_sum = np.zeros((GW, STAT_W), np.float32)
    for h in range(HEADS):
        head_sum[h * HEAD_DIM:(h + 1) * HEAD_DIM, HEADS + h] = 1.0
    head_sum = jnp.asarray(head_sum)

    def epi_do(accs, r, c, o, p):
        do = accs[0]
        o[0][...] = do.astype(BF)
        delta = jnp.dot(do * r[0][...].astype(F32), c[0][...], preferred_element_type=F32, precision=lax.Precision.HIGHEST)
        lane = lax.broadcasted_iota(jnp.int32, delta.shape, 1)
        o[1][...] = jnp.where(lane < HEADS, r[1][...], delta)

    do, stats = _fused_mm(
        "attn_out_bwd", S, 1024, 1, [(dy_attn, D, 0, 1)], [(Wao, True, D, GW, 0, 1, 0, 0, False)], [(0, 0, 0, 0, 1)], [(1024, GW)],
        [(o_att, GW, 0, False), (lse, STAT_W, 0, False)], [head_sum], [(GW, BF, GW, 0, False), (STAT_W, F32, STAT_W, 0, False)], [], epi_do)
    dWao = _mm_tn("dw_attn_out", o_att, dy_attn, GW, D, WG_TK)

    tie = early_fn(dict(w_ffn_in=dWfi, w_ffn_out=dWfo, w_conv_out=dWco, w_mix_out=dWmo, w_attn_out=dWao, w_dw=dwdw))
    stats = stats + tie
    moved = [_to_residue_pair(f"do_stats_to_residue_g{g}", do, stats, dils[g]) for g in (1, 2)]
    dos = [do] + [m_[0] for m_ in moved]
    sts = [stats] + [m_[1] for m_ in moved]
    dqkv, dbs = [], []
    for g in range(3):
        arr, cb = qkv[g]
        dg, db = _attn_bwd(f"attn_bwd_g{g}", arr, bias_kq2[g], dos[g], sts[g], cb, dils[g])
        dqkv.append(dg if g == 0 else _from_residue(f"dqkv_from_residue_g{g}", dg, dils[g]))
        dbs.append(db.reshape(HEADS, _NB))
    dtab = _bias_grad(jnp.stack(dbs), idx)[:, :, :REL_BUCKETS].reshape(3 * HEADS, REL_BUCKETS).T

    def epi_dx(accs, r, c, o, p):
        xv = r[1][...]
        d1, dg1r = _rms_bwd(xv, _rms_r(xv), c[0][...], accs[0])
        o[0][...] = r[0][...] + d1
        p[0][...] += _psum8(dg1r)

    dWg = [_mm_tn(f"dw_in_g{g}", h1, dqkv[g], D, ATTN_W, WG_TK) for g in range(3)]
    dW_in = jnp.concatenate(
        [t[:, 2 * GW:] for t in dWg] + [t[:, :GW] for t in dWg] + [t[:, GW:2 * GW] for t in dWg]
        + [_mm_tn("dw_in_glu", h1, dz_glu, D, D, WG_TK), _mm_tn("dw_in_gate", h1, dz_gate, D, D, WG_TK)], axis=1)
    g1_late = g1 + late_fn(dW_in)
    grad_x, dg1 = _resident_mm(
        "in_proj_bwd", S, 512, [dz_glu, dz_gate, dqkv[0], dqkv[1], dqkv[2]], Wp, [(dx1, D), (xs, D)], [g1_late], [(D, F32)], [(8, D)], epi_dx)

    small = dict(rel_bias_table=dtab, g_pre_mix=dg1[0], b_glu=dbglu[0], b_dw=dbdw[0], g_conv_ln=dgln[0], b_conv_ln=dbln[0],
                 b_conv_out=dbco[0], g_post_mix=dg2[0], g_pre_ffn=dg3[0], g_post_ffn=dg4[0])
    return loss_p[0], grad_x, small


SMALL = ['rel_bias_table', 'g_pre_mix', 'b_glu', 'b_dw', 'g_conv_ln', 'b_conv_ln', 'b_conv_out', 'g_post_mix', 'g_pre_ffn',
         'g_post_ffn']
BIG = ['w_in', 'w_ffn_in', 'w_ffn_out', 'w_conv_out', 'w_mix_out', 'w_attn_out', 'w_dw']
WEIGHTS = ['rel_bias_table', 'g_pre_mix', 'w_in', 'b_glu', 'w_dw', 'b_dw', 'g_conv_ln', 'b_conv_ln', 'w_conv_out', 'b_conv_out',
           'w_attn_out', 'w_mix_out', 'g_post_mix', 'g_pre_ffn', 'w_ffn_in', 'w_ffn_out', 'g_post_ffn']
SMALL_ROWS = 16


ROW_SMALL = ['g_pre_mix', 'b_glu', 'b_dw', 'g_conv_ln', 'b_conv_ln', 'b_conv_out', 'g_post_mix', 'g_pre_ffn', 'g_post_ffn']
LOSS_ROW = 10
TAB_LANES = 128


def _small_rows(small, loss_row):
    rows = [small[n].reshape(-1, D) for n in ROW_SMALL] + [loss_row.reshape(1, D)]
    n = sum(r.shape[0] for r in rows)
    return jnp.concatenate(rows + [jnp.zeros((SMALL_ROWS - n, D), F32)], axis=0)


def _adamw_small(recv_rows, recv_tab, ws, ms, vs):
    np_ = len(SMALL)

    def body(*refs):
        rr, rt = refs[0], refs[1]
        w_refs, m_refs, v_refs = refs[2:2 + np_], refs[2 + np_:2 + 2 * np_], refs[2 + 2 * np_:2 + 3 * np_]
        loss_ref = refs[2 + 3 * np_]
        outs = refs[3 + 3 * np_:]
        rows = rr[0]
        tab = rt[0]
        for s_ in range(1, NDEV):
            rows = rows + rr[s_]
            tab = tab + rt[s_]
        loss_ref[...] = jnp.sum(rows[LOSS_ROW:LOSS_ROW + 1, :], axis=1, keepdims=True) * (0.5 / D)
        row = 0
        for p, n in enumerate(SMALL):
            if n == 'rel_bias_table':
                g = tab[:, 0:3 * HEADS]
            else:
                k = w_refs[p].shape[1] // D
                g = rows[row:row + 1, :] if k == 1 else jnp.concatenate([rows[row + t:row + t + 1, :] for t in range(k)], axis=1)
                row += k
            nm = ADAM_B1 * m_refs[p][...] + (1.0 - ADAM_B1) * g
            nv = ADAM_B2 * v_refs[p][...] + (1.0 - ADAM_B2) * (g * g)
            outs[4 * p][...] = g
            outs[4 * p + 1][...] = -ADAM_LR * ((nm / _C1) / (jnp.sqrt(nv / _C2) + ADAM_EPS) + ADAM_WD * w_refs[p][...])
            outs[4 * p + 2][...] = nm
            outs[4 * p + 3][...] = nv

    out_shape = [jax.ShapeDtypeStruct((1, 1), F32)]
    for a_ in ws:
        out_shape += [jax.ShapeDtypeStruct(a_.shape, F32)] * 4
    res = pl.pallas_call(body, out_shape=out_shape, compiler_params=_cp(), name="adamw_small")(recv_rows, recv_tab, *ws, *ms, *vs)
    return res[0], [tuple(res[1 + 4 * p:5 + 4 * p]) for p in range(np_)]


def _cols_to_blocks(a):
    R = a.shape[0]
    return a.reshape(R, NDEV, a.shape[1] // NDEV).transpose(1, 0, 2)


def _blocks_to_cols(a):
    return a.transpose(1, 0, 2).reshape(a.shape[1], NDEV * a.shape[2])


def kernel(x, rel_bias_table, g_pre_mix, w_in, b_glu, w_dw, b_dw, g_conv_ln, b_conv_ln, w_conv_out, b_conv_out, w_attn_out, w_mix_out, g_post_mix, g_pre_ffn, w_ffn_in, w_ffn_out, g_post_ffn, loss_target, m_rel_bias_table, m_g_pre_mix, m_w_in, m_b_glu, m_w_dw, m_b_dw, m_g_conv_ln, m_b_conv_ln, m_w_conv_out, m_b_conv_out, m_w_attn_out, m_w_mix_out, m_g_post_mix, m_g_pre_ffn, m_w_ffn_in, m_w_ffn_out, m_g_post_ffn, v_rel_bias_table, v_g_pre_mix, v_w_in, v_b_glu, v_w_dw, v_b_dw, v_g_conv_ln, v_b_conv_ln, v_w_conv_out, v_b_conv_out, v_w_attn_out, v_w_mix_out, v_g_post_mix, v_g_pre_ffn, v_w_ffn_in, v_w_ffn_out, v_g_post_ffn):
    w = dict(rel_bias_table=rel_bias_table, g_pre_mix=g_pre_mix, w_in=w_in, b_glu=b_glu, w_dw=w_dw, b_dw=b_dw, g_conv_ln=g_conv_ln, b_conv_ln=b_conv_ln, w_conv_out=w_conv_out, b_conv_out=b_conv_out, w_attn_out=w_attn_out, w_mix_out=w_mix_out, g_post_mix=g_post_mix, g_pre_ffn=g_pre_ffn, w_ffn_in=w_ffn_in, w_ffn_out=w_ffn_out, g_post_ffn=g_post_ffn)
    m = dict(rel_bias_table=m_rel_bias_table, g_pre_mix=m_g_pre_mix, w_in=m_w_in, b_glu=m_b_glu, w_dw=m_w_dw, b_dw=m_b_dw, g_conv_ln=m_g_conv_ln, b_conv_ln=m_b_conv_ln, w_conv_out=m_w_conv_out, b_conv_out=m_b_conv_out, w_attn_out=m_w_attn_out, w_mix_out=m_w_mix_out, g_post_mix=m_g_post_mix, g_pre_ffn=m_g_pre_ffn, w_ffn_in=m_w_ffn_in, w_ffn_out=m_w_ffn_out, g_post_ffn=m_g_post_ffn)
    v = dict(rel_bias_table=v_rel_bias_table, g_pre_mix=v_g_pre_mix, w_in=v_w_in, b_glu=v_b_glu, w_dw=v_w_dw, b_dw=v_b_dw, g_conv_ln=v_g_conv_ln, b_conv_ln=v_b_conv_ln, w_conv_out=v_w_conv_out, b_conv_out=v_b_conv_out, w_attn_out=v_w_attn_out, w_mix_out=v_w_mix_out, g_post_mix=v_g_post_mix, g_pre_ffn=v_g_pre_ffn, w_ffn_in=v_w_ffn_in, w_ffn_out=v_w_ffn_out, g_post_ffn=v_g_post_ffn)

    def shard2d(d, n):
        a = d[n][0]
        return jnp.pad(a, ((0, HALO - CONV_W), (0, 0))) if n == 'w_dw' else a

    own = {n: shard2d(w, n).astype(F32 if n == 'w_dw' else BF) for n in BIG}
    packed = ['w_ffn_out', 'w_conv_out', 'w_mix_out', 'w_attn_out']
    alone = ['w_ffn_in', 'w_dw']
    shapes = [own[n].shape for n in packed]

    def pack(arrs, lead):
        return jnp.concatenate([a.reshape(lead + (-1, D)) for a in arrs], axis=len(lead))

    def unpack(p):
        out, pos = {}, 0
        for n, shp in zip(packed, shapes):
            rows = shp[0] * shp[1] // D
            out[n] = p[:, pos:pos + rows].reshape((NDEV,) + shp)
            pos += rows
        return out

    (g_in,) = _all_gather("gather_w_in", [own['w_in']])
    rest_own = [pack([own[n] for n in packed], ())] + [own[n] for n in alone]
    g_in, rest_own = lax.optimization_barrier((g_in, rest_own))
    gather_rest = _exchange_start("gather_rest_start", rest_own, True)
    W_in = _blocks_to_cols(g_in)
    kvq = [W_in[:, t * ATTN_W + g * GW:t * ATTN_W + (g + 1) * GW] for g in range(3) for t in (1, 2, 0)]
    Wp = jnp.concatenate([W_in[:, 3 * ATTN_W:]] + kvq, axis=1)

    def rest_fn(after):
        lands = _exchange_wait("gather_rest_wait", gather_rest, after, True)
        gw = unpack(_set_own_slot(lands[0], rest_own[0]))
        for n, l, o in zip(alone, lands[1:], rest_own[1:]):
            gw[n] = _set_own_slot(l, o)
        return (_blocks_to_cols(gw['w_ffn_in']), gw['w_ffn_out'].reshape(FFN, D), gw['w_conv_out'].reshape(D, D),
                gw['w_mix_out'].reshape(D, D), _blocks_to_cols(gw['w_attn_out']), _blocks_to_cols(gw['w_dw']))

    def to_blocks(n, g):
        if n in ('w_in', 'w_ffn_in', 'w_attn_out', 'w_dw'):
            return _cols_to_blocks(g)
        return g.reshape(NDEV, g.shape[0] // NDEV, g.shape[1])

    started = {}

    def early_fn(grads):
        blocks = [pack([to_blocks(n, grads[n]) for n in packed], (NDEV,))] + [to_blocks(n, grads[n]) for n in alone]
        started['blocks'] = blocks
        started['handle'] = _exchange_start("scatter_early_start", blocks, False)
        return started['handle'][4][0:1, 0:1]

    def late_fn(dW_in):
        started['in_blocks'] = [to_blocks('w_in', dW_in)]
        started['in_handle'] = _exchange_start("scatter_w_in_start", started['in_blocks'], False)
        return started['in_handle'][4][0:1, 0:1]

    g1_tied = g_pre_mix[0] + gather_rest[4][0, 0:1]
    loss_row, grad_x, small = _local_step(
        x[0], loss_target[0], Wp, rest_fn, early_fn, late_fn, rel_bias_table, g1_tied, b_glu[0], b_dw[0], g_conv_ln[0],
        b_conv_ln[0], b_conv_out[0], g_post_mix[0], g_pre_ffn[0], g_post_ffn[0])

    lands = _exchange_wait("scatter_early_wait", started['handle'], grad_x, False)
    lands = [_set_own_slot(l, _own_block(b)) for l, b in zip(lands, started['blocks'])]
    recv = unpack(lands[0])
    recv.update(zip(alone, lands[1:]))
    (land_in,) = _exchange_wait("scatter_w_in_wait", started['in_handle'], grad_x, False)
    recv['w_in'] = _set_own_slot(land_in, _own_block(started['in_blocks'][0]))
    tiles = dict(w_in=128, w_ffn_in=256, w_ffn_out=176, w_conv_out=128, w_mix_out=128, w_attn_out=512, w_dw=HALO)
    res = {}
    for n in BIG:
        g_, d_, nm_, nv_ = _adamw("adamw_" + n, shard2d(w, n), shard2d(m, n), shard2d(v, n), recv[n], tiles[n])
        if n == 'w_dw':
            g_, d_, nm_, nv_ = (t[:CONV_W] for t in (g_, d_, nm_, nv_))
        res[n] = tuple(t[None] for t in (g_, d_, nm_, nv_))

    tab = jnp.pad(small['rel_bias_table'], ((0, 0), (0, TAB_LANES - 3 * HEADS)))
    srows, stab = _all_gather("gather_small_grads", [_small_rows(small, loss_row), tab])
    loss11, small_res = _adamw_small(srows, stab, [w[n] for n in SMALL], [m[n] for n in SMALL], [v[n] for n in SMALL])
    loss = loss11.reshape(())
    for n, r in zip(SMALL, small_res):
        res[n] = r
    return (loss, grad_x[None], *[res[n][0] for n in WEIGHTS], *[res[n][1] for n in WEIGHTS],
            *[res[n][2] for n in WEIGHTS], *[res[n][3] for n in WEIGHTS])
```

```python
import functools
import math

import numpy as np
import jax
import jax.numpy as jnp
from jax import lax
from jax.experimental import pallas as pl
from jax.experimental.pallas import tpu as pltpu

F32 = jnp.float32
BF = jnp.bfloat16

D = 1024
HEAD_DIM = 64
HEADS = 8
GROUPS = ((128, 1), (512, 4), (2048, 16))
QBLK = 128
GW = HEADS * HEAD_DIM
ATTN_W = 3 * GW
REL_BUCKETS = 32
REL_MAX_DISTANCE = 2048
CONV_W = 31
HALO = 32
FFN = 2816
IN_W = 3 * ATTN_W + 2 * D + 2 * D
RMS_EPS = 1e-6
LN_EPS = 1e-5
NEG_INF = -1e30
SCALE = HEAD_DIM ** -0.5
NDEV = 8

ADAM_LR = 0.001
ADAM_B1 = 0.9
ADAM_B2 = 0.999
ADAM_EPS = 1e-08
ADAM_WD = 0.01
ADAM_STEP = 10

Z_G0 = 4096 // GW


def _kvq_blocks(g):
    return (Z_G0 + 3 * g, Z_G0 + 3 * g + 1, Z_G0 + 3 * g + 2)


WG_TK = 2048
VMEM_LIMIT = 52 * 1024 * 1024


def _cp(sem=None):
    if sem is None:
        return pltpu.CompilerParams(vmem_limit_bytes=VMEM_LIMIT)
    return pltpu.CompilerParams(vmem_limit_bytes=VMEM_LIMIT, dimension_semantics=sem)


def _sig(v):
    return jax.nn.sigmoid(v)


def _psum8(v):
    return v.reshape(v.shape[0] // 8, 8, v.shape[1]).sum(axis=0)


def _rms_r(v):
    return lax.rsqrt(jnp.mean(v * v, axis=-1, keepdims=True) + RMS_EPS)


def _rms_bwd(v, r, g, dy):
    gy = dy * g
    dv = r * gy - v * (r * r * r) * jnp.mean(v * gy, axis=-1, keepdims=True)
    return dv, dy * v * r


def _clip_k(k, k0, nk):
    return jnp.clip(k - k0, 0, nk - 1)


def _fused_mm(name, M, tm, grid_n, a_ops, b_ops, terms, acc_shapes, rows, consts, outs, parts, epilogue, n_outer=False):
    gm = M // tm
    nk_total = max([t[3] + t[4] for t in terms], default=1)
    n_a, n_b, n_r, n_c, n_o, n_p = len(a_ops), len(b_ops), len(rows), len(consts), len(outs), len(parts)
    n_acc = len(acc_shapes)
    use_scratch = nk_total > 1
    if parts:
        assert grid_n == 1

    def jj(j, follow):
        return j if follow else 0

    in_specs, args = [], []
    for (arr, tk, k0, nk) in a_ops:
        in_specs.append(pl.BlockSpec((tm, tk), functools.partial(lambda i, j, k, k0, nk: (i, _clip_k(k, k0, nk)), k0=k0, nk=nk)))
        args.append(arr)
    for (arr, nt, tk, tn, k0, nk, koff, joff, fj) in b_ops:
        if nt:
            in_specs.append(pl.BlockSpec((tn, tk), functools.partial(
                lambda i, j, k, k0, nk, koff, joff, fj: (joff + jj(j, fj), _clip_k(k, k0, nk) + koff),
                k0=k0, nk=nk, koff=koff, joff=joff, fj=fj)))
        else:
            in_specs.append(pl.BlockSpec((tk, tn), functools.partial(
                lambda i, j, k, k0, nk, koff, joff, fj: (_clip_k(k, k0, nk) + koff, joff + jj(j, fj)),
                k0=k0, nk=nk, koff=koff, joff=joff, fj=fj)))
        args.append(arr)
    for (arr, w, off, fj) in rows:
        in_specs.append(pl.BlockSpec((tm, w), functools.partial(lambda i, j, k, off, fj: (i, off + jj(j, fj)), off=off, fj=fj)))
        args.append(arr)
    for arr in consts:
        in_specs.append(pl.BlockSpec(arr.shape, functools.partial(lambda i, j, k, nd: (0,) * nd, nd=arr.ndim)))
        args.append(arr)
    out_specs, out_shape = [], []
    for (ncols, dt, w, off, fj) in outs:
        out_specs.append(pl.BlockSpec((tm, w), functools.partial(lambda i, j, k, off, fj: (i, off + jj(j, fj)), off=off, fj=fj)))
        out_shape.append(jax.ShapeDtypeStruct((M, ncols), dt))
    for (r, c) in parts:
        out_specs.append(pl.BlockSpec((r, c), lambda i, j, k: (0, 0)))
        out_shape.append(jax.ShapeDtypeStruct((r, c), F32))
    scratch = [pltpu.VMEM(s, F32) for s in acc_shapes] if use_scratch else []

    def body(*refs):
        pos = 0
        a_refs = refs[pos:pos + n_a]; pos += n_a
        b_refs = refs[pos:pos + n_b]; pos += n_b
        r_refs = refs[pos:pos + n_r]; pos += n_r
        c_refs = refs[pos:pos + n_c]; pos += n_c
        o_refs = refs[pos:pos + n_o]; pos += n_o
        p_refs = refs[pos:pos + n_p]; pos += n_p
        acc_refs = refs[pos:pos + n_acc] if use_scratch else ()
        i = pl.program_id(0)
        k = pl.program_id(2)

        def dot_of(ai, bi):
            a = a_refs[ai][...].astype(BF)
            b = b_refs[bi][...].astype(BF)
            if b_ops[bi][1]:
                return lax.dot_general(a, b, (((1,), (1,)), ((), ())), preferred_element_type=F32)
            return jnp.dot(a, b, preferred_element_type=F32)

        if parts:
            @pl.when((i == 0) & (k == 0))
            def _():
                for p in p_refs:
                    p[...] = jnp.zeros(p.shape, F32)

        def finish(accs):
            epilogue(accs, r_refs, c_refs, o_refs, p_refs)
            if parts:
                @pl.when(i == gm - 1)
                def _():
                    for p in p_refs:
                        p[0:1, :] = jnp.sum(p[...], axis=0, keepdims=True)

        if not use_scratch:
            accs = [None] * n_acc
            for (ai, bi, ci, k0, nk) in terms:
                d = dot_of(ai, bi)
                accs[ci] = d if accs[ci] is None else accs[ci] + d
            finish(accs)
        else:
            @pl.when(k == 0)
            def _():
                for acc in acc_refs:
                    acc[...] = jnp.zeros(acc.shape, F32)

            for (ai, bi, ci, k0, nk) in terms:
                def do(ai=ai, bi=bi, ci=ci):
                    acc_refs[ci][...] += dot_of(ai, bi)
                if k0 == 0 and nk == nk_total:
                    do()
                else:
                    pl.when((k >= k0) & (k < k0 + nk))(do)

            @pl.when(k == nk_total - 1)
            def _():
                finish([acc[...] for acc in acc_refs])

    grid = (gm, grid_n, nk_total)
    if n_outer:
        assert not parts
        swap = lambda spec: pl.BlockSpec(spec.block_shape, functools.partial(lambda j, i, k, f: f(i, j, k), f=spec.index_map))
        in_specs, out_specs, grid = [swap(sp) for sp in in_specs], [swap(sp) for sp in out_specs], (grid_n, gm, nk_total)
    res = pl.pallas_call(
        body, grid=grid, in_specs=in_specs, out_specs=out_specs, out_shape=out_shape,
        scratch_shapes=scratch, compiler_params=_cp(("arbitrary", "arbitrary", "arbitrary")), name=name,
    )(*args)
    return res


def _mm_tn(name, a, b, tm, tn, tk):
    S, Ka = a.shape
    Nb = b.shape[1]
    nk = S // tk

    def body(a_ref, b_ref, o_ref, acc):
        k = pl.program_id(2)

        @pl.when(k == 0)
        def _():
            acc[...] = jnp.zeros(acc.shape, F32)

        acc[...] += lax.dot_general(a_ref[...], b_ref[...], (((0,), (0,)), ((), ())), preferred_element_type=F32)

        @pl.when(k == nk - 1)
        def _():
            o_ref[...] = acc[...].astype(o_ref.dtype)

    return pl.pallas_call(
        body, grid=(Ka // tm, Nb // tn, nk),
        in_specs=[pl.BlockSpec((tk, tm), lambda i, j, k: (k, i)), pl.BlockSpec((tk, tn), lambda i, j, k: (k, j))],
        out_specs=pl.BlockSpec((tm, tn), lambda i, j, k: (i, j)),
        out_shape=jax.ShapeDtypeStruct((Ka, Nb), BF),
        scratch_shapes=[pltpu.VMEM((tm, tn), F32)],
        compiler_params=_cp(("parallel", "parallel", "arbitrary")), name=name,
    )(a, b)


def _resident_mm(name, M, tm, a_segs, w, rows, consts, outs, parts, epilogue):
    gm = M // tm
    n_a, n_r, n_c, n_o, n_p = len(a_segs), len(rows), len(consts), len(outs), len(parts)
    widths = [a.shape[1] for a in a_segs]
    offs = [sum(widths[:t]) for t in range(n_a)]
    once = pl.Buffered(1)

    def body(*refs):
        pos = 0
        a_refs = refs[pos:pos + n_a]; pos += n_a
        w_ref = refs[pos]; pos += 1
        r_refs = refs[pos:pos + n_r]; pos += n_r
        c_refs = refs[pos:pos + n_c]; pos += n_c
        o_refs = refs[pos:pos + n_o]; pos += n_o
        p_refs = refs[pos:pos + n_p]
        i = pl.program_id(0)
        if parts:
            @pl.when(i == 0)
            def _():
                for p in p_refs:
                    p[...] = jnp.zeros(p.shape, F32)
        acc = None
        for t in range(n_a):
            d = lax.dot_general(a_refs[t][...], w_ref[:, offs[t]:offs[t] + widths[t]], (((1,), (1,)), ((), ())),
                                preferred_element_type=F32)
            acc = d if acc is None else acc + d
        epilogue([acc], r_refs, c_refs, o_refs, p_refs)
        if parts:
            @pl.when(i == gm - 1)
            def _():
                for p in p_refs:
                    p[0:1, :] = jnp.sum(p[...], axis=0, keepdims=True)

    in_specs = [pl.BlockSpec((tm, wd), lambda i: (i, 0)) for wd in widths]
    in_specs.append(pl.BlockSpec(w.shape, lambda i: (0, 0), pipeline_mode=once))
    in_specs += [pl.BlockSpec((tm, c), lambda i: (i, 0)) for _, c in rows]
    in_specs += [pl.BlockSpec(c.shape, lambda i: (0, 0), pipeline_mode=once) for c in consts]
    out_specs = [pl.BlockSpec((tm, nc), lambda i: (i, 0)) for nc, _ in outs] + [pl.BlockSpec(pc, lambda i: (0, 0)) for pc in parts]
    out_shape = [jax.ShapeDtypeStruct((M, nc), dt) for nc, dt in outs] + [jax.ShapeDtypeStruct(pc, F32) for pc in parts]
    return pl.pallas_call(
        body, grid=(gm,), in_specs=in_specs, out_specs=out_specs, out_shape=out_shape,
        compiler_params=_cp(("arbitrary",)), name=name,
    )(*a_segs, w, *[r for r, _ in rows], *consts)


def _in_proj(xs, g1, Wp):
    S = xs.shape[0]
    tm, tn = 1024, IN_W // 4

    def body(x_ref, g_ref, w_ref, h_ref, z_ref, hs):
        @pl.when(pl.program_id(1) == 0)
        def _():
            v = x_ref[...]
            h = (v * _rms_r(v) * g_ref[...]).astype(BF)
            hs[...] = h
            h_ref[...] = h

        z_ref[...] = jnp.dot(hs[...], w_ref[...], preferred_element_type=F32).astype(BF)

    return pl.pallas_call(
        body, grid=(S // tm, IN_W // tn),
        in_specs=[pl.BlockSpec((tm, D), lambda i, j: (i, 0)), pl.BlockSpec((1, D), lambda i, j: (0, 0)),
                  pl.BlockSpec((D, tn), lambda i, j: (0, j))],
        out_specs=[pl.BlockSpec((tm, D), lambda i, j: (i, 0)), pl.BlockSpec((tm, tn), lambda i, j: (i, j))],
        out_shape=[jax.ShapeDtypeStruct((S, D), BF), jax.ShapeDtypeStruct((S, IN_W), BF)],
        scratch_shapes=[pltpu.VMEM((tm, D), BF)],
        compiler_params=_cp(("arbitrary", "arbitrary")), name="in_proj",
    )(xs, g1, Wp)


def _rel_bucket_np(dist):
    max_exact = REL_BUCKETS // 2
    d = np.maximum(dist, 0)
    df = np.maximum(d, 1).astype(np.float32)
    large = max_exact + (np.log(df / np.float32(max_exact)) / np.float32(math.log(REL_MAX_DISTANCE / max_exact))
                         * np.float32(REL_BUCKETS - max_exact)).astype(np.int32)
    large = np.minimum(large, REL_BUCKETS - 1)
    return np.where(d < max_exact, d, large).astype(np.int32)


N_LAYOUTS = 2


def _band_index():
    idx = np.zeros((N_LAYOUTS * 3, 1, QBLK * 2 * QBLK), np.int32)
    for g, (window, dil) in enumerate(GROUPS):
        span = window // dil
        k = np.arange(2 * QBLK)[:, None]; q = np.arange(QBLK)[None, :]
        off = q - k + QBLK
        idx[g, 0] = np.where((off >= 0) & (off <= span), _rel_bucket_np(off * dil), -1).reshape(-1)
        k = np.arange(QBLK)[:, None]; q = np.arange(2 * QBLK)[None, :]
        off = q - k
        idx[3 + g, 0] = np.where((off >= 0) & (off <= span), _rel_bucket_np(off * dil), -1).reshape(-1)
    return idx


_NB = QBLK * 2 * QBLK
_BCH = 4096


def _bias_build(tab_t, idx):
    def body(t_ref, i_ref, o_ref):
        ix = i_ref[0]
        t = t_ref[0]
        acc = jnp.full((HEADS, _BCH), NEG_INF, F32)
        for b in range(REL_BUCKETS):
            acc = jnp.where(ix == b, t[:, b:b + 1], acc)
        o_ref[0] = acc

    return pl.pallas_call(
        body, grid=(N_LAYOUTS * 3, _NB // _BCH),
        in_specs=[pl.BlockSpec((1, HEADS, REL_BUCKETS), lambda l, n: (l % 3, 0, 0)),
                  pl.BlockSpec((1, 1, _BCH), lambda l, n: (l, 0, n))],
        out_specs=pl.BlockSpec((1, HEADS, _BCH), lambda l, n: (l, 0, n)),
        out_shape=jax.ShapeDtypeStruct((N_LAYOUTS * 3, HEADS, _NB), F32), compiler_params=_cp(), name="bias_build",
    )(tab_t, idx)


def _bias_grad(ds, idx):
    nch = _NB // _BCH

    def body(d_ref, i_ref, o_ref):
        n = pl.program_id(1)

        @pl.when(n == 0)
        def _():
            o_ref[...] = jnp.zeros(o_ref.shape, F32)

        ix = i_ref[0]
        d = d_ref[0]
        lane = lax.broadcasted_iota(jnp.int32, (HEADS, 128), 1)
        acc = jnp.zeros((HEADS, 128), F32)
        for b in range(REL_BUCKETS):
            s = jnp.sum(jnp.where(ix == b, d, 0.0), axis=1, keepdims=True)
            acc = acc + jnp.where(lane == b, s, 0.0)
        o_ref[0] += acc

    return pl.pallas_call(
        body, grid=(3, nch),
        in_specs=[pl.BlockSpec((1, HEADS, _BCH), lambda l, n: (l, 0, n)),
                  pl.BlockSpec((1, 1, _BCH), lambda l, n: (3 + l, 0, n))],
        out_specs=pl.BlockSpec((1, HEADS, 128), lambda l, n: (l, 0, 0)),
        out_shape=jax.ShapeDtypeStruct((3, HEADS, 128), F32), compiler_params=_cp(), name="bias_grad",
    )(ds, idx)


PT = 256
PSTEP = 1024
STAT_W = 128


def _perm_np(dil):
    p = np.zeros((PT, PT), np.float32)
    m = np.arange(PT // dil)
    for c in range(dil):
        p[c * (PT // dil) + m, m * dil + c] = 1.0
    return p


def _perm_const(dil, dtype, inverse):
    p = _perm_np(dil)
    return jnp.asarray(p.T if inverse else p, dtype)


def _apply_perm(p, x):
    if x.dtype == F32:
        return jnp.dot(p, x, preferred_element_type=F32, precision=lax.Precision.HIGHEST)
    return jnp.dot(p, x, preferred_element_type=F32)


def _to_residue(name, arr, col_blocks, dil):
    S = arr.shape[0]
    nc = len(col_blocks)
    p = _perm_const(dil, arr.dtype, False)
    sub = PT // dil

    def body(*refs):
        p_ref, ins, o_ref = refs[0], refs[1:1 + nc], refs[1 + nc]
        for u in range(PSTEP // PT):
            for t, r in enumerate(ins):
                y = _apply_perm(p_ref[...], r[u * PT:(u + 1) * PT, :]).astype(o_ref.dtype)
                o_ref[:, u * sub:(u + 1) * sub, t * GW:(t + 1) * GW] = y.reshape(dil, sub, GW)

    out = pl.pallas_call(
        body, grid=(S // PSTEP,),
        in_specs=[pl.BlockSpec((PT, PT), lambda i: (0, 0))]
                 + [pl.BlockSpec((PSTEP, GW), functools.partial(lambda i, cb: (i, cb), cb=cb)) for cb in col_blocks],
        out_specs=pl.BlockSpec((dil, PSTEP // dil, nc * GW), lambda i: (0, i, 0)),
        out_shape=jax.ShapeDtypeStruct((dil, S // dil, nc * GW), arr.dtype), compiler_params=_cp(), name=name,
    )(p, *([arr] * nc))
    return out.reshape(S, nc * GW)


def _to_residue_pair(name, do, stats, dil):
    S = do.shape[0]
    pb = _perm_const(dil, BF, False)
    pf = _perm_const(dil, F32, False)
    sub = PT // dil

    def body(pb_ref, pf_ref, d_ref, s_ref, od_ref, os_ref):
        for u in range(PSTEP // PT):
            rs = slice(u * PT, (u + 1) * PT)
            od_ref[:, u * sub:(u + 1) * sub, :] = _apply_perm(pb_ref[...], d_ref[rs, :]).astype(BF).reshape(dil, sub, GW)
            os_ref[:, u * sub:(u + 1) * sub, :] = _apply_perm(pf_ref[...], s_ref[rs, :]).reshape(dil, sub, STAT_W)

    cst = pl.BlockSpec((PT, PT), lambda i: (0, 0))
    od, os_ = pl.pallas_call(
        body, grid=(S // PSTEP,),
        in_specs=[cst, cst, pl.BlockSpec((PSTEP, GW), lambda i: (i, 0)), pl.BlockSpec((PSTEP, STAT_W), lambda i: (i, 0))],
        out_specs=[pl.BlockSpec((dil, PSTEP // dil, GW), lambda i: (0, i, 0)), pl.BlockSpec((dil, PSTEP // dil, STAT_W), lambda i: (0, i, 0))],
        out_shape=[jax.ShapeDtypeStruct((dil, S // dil, GW), BF), jax.ShapeDtypeStruct((dil, S // dil, STAT_W), F32)],
        compiler_params=_cp(), name=name,
    )(pb, pf, do, stats)
    return od.reshape(S, GW), os_.reshape(S, STAT_W)


def _from_residue(name, arr, dil):
    S, W = arr.shape
    p = _perm_const(dil, arr.dtype, True)
    sub = PT // dil

    def body(p_ref, x_ref, o_ref):
        for u in range(PSTEP // PT):
            x = x_ref[:, u * sub:(u + 1) * sub, :].reshape(PT, W)
            o_ref[u * PT:(u + 1) * PT, :] = _apply_perm(p_ref[...], x).astype(o_ref.dtype)

    return pl.pallas_call(
        body, grid=(S // PSTEP,),
        in_specs=[pl.BlockSpec((PT, PT), lambda i: (0, 0)), pl.BlockSpec((dil, PSTEP // dil, W), lambda i: (0, i, 0))],
        out_specs=pl.BlockSpec((PSTEP, W), lambda i: (i, 0)),
        out_shape=jax.ShapeDtypeStruct((S, W), arr.dtype), compiler_params=_cp(), name=name,
    )(p, arr.reshape(dil, S // dil, W))


PAIR_W = 2 * HEAD_DIM
NT_DIMS = (((1,), (1,)), ((), ()))
TN_DIMS = (((0,), (0,)), ((), ()))


def _attn_dims(S, dil):
    L = S // dil
    TQ = min(512, L)
    return L, TQ, L // TQ, TQ // QBLK


def _attn_specs(S, dil):
    L, TQ, nq, nsub = _attn_dims(S, dil)
    nb = L // QBLK
    cur = lambda cb, w=GW: pl.BlockSpec((TQ, w), lambda c, i: (c * nq + i, cb))
    prev = lambda cb, w=GW: pl.BlockSpec((QBLK, w), lambda c, i: (c * nb + jnp.maximum(i * nsub - 1, 0), cb))
    nxt = lambda cb, w=GW: pl.BlockSpec((QBLK, w), lambda c, i: (c * nb + jnp.minimum((i + 1) * nsub, nb - 1), cb))
    band = lambda r, c_: pl.BlockSpec((HEADS, r, c_), lambda c, i: (0, 0, 0))
    return L, TQ, nq, nsub, cur, prev, nxt, band


def _fill(buf, first_ref, second_ref):
    n = first_ref.shape[0]
    buf[0:n, :] = first_ref[...]
    buf[n:n + second_ref.shape[0], :] = second_ref[...]


def _attn_fwd(name, arr, bias_kq, cb, dil):
    S = arr.shape[0]
    kcb, vcb, qcb = cb
    L, TQ, nq, nsub, cur, prev, nxt, band = _attn_specs(S, dil)

    def body(q_ref, kc_ref, kp_ref, vc_ref, vp_ref, b_ref, o_ref, l_ref, kbuf, vbuf):
        i = pl.program_id(1)
        _fill(kbuf, kp_ref, kc_ref)
        _fill(vbuf, vp_ref, vc_ref)
        row = lax.broadcasted_iota(jnp.int32, (2 * QBLK, QBLK), 0)
        first = (row >= QBLK) | (i > 0)
        low = lax.broadcasted_iota(jnp.int32, (QBLK, PAIR_W), 1) < HEAD_DIM
        zero = jnp.zeros((QBLK, PAIR_W), BF)
        for j in range(nsub):
            rs = slice(j * QBLK, (j + 1) * QBLK)
            ks = slice(j * QBLK, (j + 2) * QBLK)
            lrows = []
            for hp in range(HEADS // 2):
                ps = slice(hp * PAIR_W, (hp + 1) * PAIR_W)
                qp = q_ref[rs, ps]
                kp = kbuf[ks, ps]
                vp = vbuf[ks, ps]
                halves = []
                for t in range(2):
                    qm = jnp.where(low if t == 0 else ~low, qp, zero)
                    s = lax.dot_general(kp, qm, NT_DIMS, preferred_element_type=F32) * SCALE + b_ref[2 * hp + t]
                    if j == 0:
                        s = jnp.where(first, s, NEG_INF)
                    m = jnp.max(s, axis=0, keepdims=True)
                    p = jnp.exp(s - m)
                    den = jnp.sum(p, axis=0, keepdims=True)
                    o2 = lax.dot_general(vp, p.astype(BF), TN_DIMS, preferred_element_type=F32)
                    halves.append(o2[t * HEAD_DIM:(t + 1) * HEAD_DIM, :] / den)
                    lrows.append(m + jnp.log(den))
                o_ref[rs, ps] = jnp.concatenate(halves, axis=0).T.astype(BF)
            lt = jnp.concatenate(lrows + [jnp.zeros((STAT_W - HEADS, QBLK), F32)], axis=0)
            l_ref[rs, :] = lt.T

    return pl.pallas_call(
        body, grid=(dil, nq),
        in_specs=[cur(qcb), cur(kcb), prev(kcb), cur(vcb), prev(vcb), band(2 * QBLK, QBLK)],
        out_specs=[cur(0), cur(0, STAT_W)],
        out_shape=[jax.ShapeDtypeStruct((S, GW), BF), jax.ShapeDtypeStruct((S, STAT_W), F32)],
        scratch_shapes=[pltpu.VMEM((QBLK + TQ, GW), BF), pltpu.VMEM((QBLK + TQ, GW), BF)],
        compiler_params=_cp(), name=name,
    )(arr, arr, arr, arr, arr, bias_kq)


def _attn_bwd(name, arr, bias_kq2, do, stats, cb, dil):
    S = arr.shape[0]
    kcb, vcb, qcb = cb
    L, TQ, nq, nsub, cur, prev, nxt, band = _attn_specs(S, dil)

    def body(k_ref, v_ref, qc_ref, qn_ref, b_ref, doc_ref, don_ref, sc_ref, sn_ref, o_ref, db_ref, qbuf, dobuf, sbuf, carry):
        c = pl.program_id(0)
        i = pl.program_id(1)

        @pl.when((c == 0) & (i == 0))
        def _():
            db_ref[...] = jnp.zeros(db_ref.shape, F32)
            carry[...] = jnp.zeros(carry.shape, F32)

        _fill(qbuf, qc_ref, qn_ref)
        _fill(dobuf, doc_ref, don_ref)
        for j in range(nsub + 1):
            rs = slice(j * QBLK, (j + 1) * QBLK)
            sbuf[:, rs] = (sc_ref[rs, :] if j < nsub else sn_ref[...]).T
        col = lax.broadcasted_iota(jnp.int32, (QBLK, 2 * QBLK), 1)
        last = (col < QBLK) | (i < nq - 1)
        low = lax.broadcasted_iota(jnp.int32, (QBLK, PAIR_W), 1) < HEAD_DIM
        zero = jnp.zeros((QBLK, PAIR_W), BF)
        for hp in range(HEADS // 2):
            ps = slice(hp * PAIR_W, (hp + 1) * PAIR_W)
            dbs = [jnp.zeros((QBLK, 2 * QBLK), F32), jnp.zeros((QBLK, 2 * QBLK), F32)]
            tail = carry[:, ps]
            for j in range(nsub):
                rs = slice(j * QBLK, (j + 1) * QBLK)
                qs = slice(j * QBLK, (j + 2) * QBLK)
                qp = qbuf[qs, ps]
                dd = dobuf[qs, ps]
                kp = k_ref[rs, ps]
                vp = v_ref[rs, ps]
                kt = kp.T
                dk, dv, dqt = [], [], []
                for t in range(2):
                    h = 2 * hp + t
                    sel = low if t == 0 else ~low
                    s = lax.dot_general(jnp.where(sel, kp, zero), qp, NT_DIMS, preferred_element_type=F32) * SCALE + b_ref[h]
                    if j == nsub - 1:
                        s = jnp.where(last, s, NEG_INF)
                    p = jnp.exp(s - sbuf[h:h + 1, qs])
                    dp = lax.dot_general(jnp.where(sel, vp, zero), dd, NT_DIMS, preferred_element_type=F32)
                    ds = p * (dp - sbuf[HEADS + h:HEADS + h + 1, qs])
                    dbs[t] = dbs[t] + ds
                    dsb = ds.astype(BF)
                    dk.append(jnp.dot(dsb, qp, preferred_element_type=F32))
                    dv.append(jnp.dot(p.astype(BF), dd, preferred_element_type=F32))
                    dqt.append(jnp.dot(kt[t * HEAD_DIM:(t + 1) * HEAD_DIM, :], dsb, preferred_element_type=F32))
                o_ref[rs, ps] = (jnp.where(low, dk[0], dk[1]) * SCALE).astype(BF)
                o_ref[rs, GW + hp * PAIR_W:GW + (hp + 1) * PAIR_W] = jnp.where(low, dv[0], dv[1]).astype(BF)
                dq2 = jnp.concatenate(dqt, axis=0).T * SCALE
                o_ref[rs, 2 * GW + hp * PAIR_W:2 * GW + (hp + 1) * PAIR_W] = (dq2[0:QBLK] + tail).astype(BF)
                tail = dq2[QBLK:2 * QBLK]
            carry[:, ps] = tail
            db_ref[2 * hp] += dbs[0]
            db_ref[2 * hp + 1] += dbs[1]

    return pl.pallas_call(
        body, grid=(dil, nq),
        in_specs=[cur(kcb), cur(vcb), cur(qcb), nxt(qcb), band(QBLK, 2 * QBLK),
                  cur(0), nxt(0), cur(0, STAT_W), nxt(0, STAT_W)],
        out_specs=[cur(0, ATTN_W), band(QBLK, 2 * QBLK)],
        out_shape=[jax.ShapeDtypeStruct((S, ATTN_W), BF), jax.ShapeDtypeStruct((HEADS, QBLK, 2 * QBLK), F32)],
        scratch_shapes=[pltpu.VMEM((TQ + QBLK, GW), BF), pltpu.VMEM((TQ + QBLK, GW), BF), pltpu.VMEM((STAT_W, TQ + QBLK), F32),
                        pltpu.VMEM((QBLK, GW), F32)],
        compiler_params=_cp(("arbitrary", "arbitrary")), name=name,
    )(arr, arr, arr, arr, bias_kq2, do, do, stats, stats)


def _head_expand():
    e = np.zeros((STAT_W, GW), np.float32)
    for h in range(HEADS):
        e[h, h * HEAD_DIM:(h + 1) * HEAD_DIM] = 1.0
    return e


def _attn_merge(os_, ls_, S):
    dils = [d for _, d in GROUPS]
    pb = [_perm_const(d, BF, True) for d in dils[1:]]
    pf = [_perm_const(d, F32, True) for d in dils[1:]]
    expand = jnp.asarray(_head_expand(), BF)

    def body(o0, o1, o2, l0, l1, l2, pb1, pb2, pf1, pf2, e_ref, o_ref, l_ref):
        for u in range(PSTEP // PT):
            rs = slice(u * PT, (u + 1) * PT)
            res = lambda r, d: r[:, u * (PT // d):(u + 1) * (PT // d), :].reshape(PT, r.shape[2])
            ov = [o0[rs, :].astype(F32), _apply_perm(pb1[...], res(o1, dils[1])), _apply_perm(pb2[...], res(o2, dils[2]))]
            lv = [l0[rs, :], _apply_perm(pf1[...], res(l1, dils[1])), _apply_perm(pf2[...], res(l2, dils[2]))]
            m = jnp.maximum(jnp.maximum(lv[0], lv[1]), lv[2])
            ev = [jnp.exp(l - m) for l in lv]
            den = ev[0] + ev[1] + ev[2]
            acc = jnp.zeros((PT, GW), F32)
            for g in range(3):
                wide = jnp.dot((ev[g] / den).astype(BF), e_ref[...], preferred_element_type=F32)
                acc = acc + wide * ov[g]
            o_ref[rs, :] = acc.astype(BF)
            l_ref[rs, :] = m + jnp.log(den)

    nat = lambda w: pl.BlockSpec((PSTEP, w), lambda i: (i, 0))
    res = lambda d, w: pl.BlockSpec((d, PSTEP // d, w), lambda i: (0, i, 0))
    cst = lambda a: pl.BlockSpec(a.shape, lambda i: (0, 0))
    args = [os_[0], os_[1].reshape(dils[1], S // dils[1], GW), os_[2].reshape(dils[2], S // dils[2], GW),
            ls_[0], ls_[1].reshape(dils[1], S // dils[1], STAT_W), ls_[2].reshape(dils[2], S // dils[2], STAT_W),
            pb[0], pb[1], pf[0], pf[1], expand]
    return pl.pallas_call(
        body, grid=(S // PSTEP,),
        in_specs=[nat(GW), res(dils[1], GW), res(dils[2], GW), nat(STAT_W), res(dils[1], STAT_W), res(dils[2], STAT_W)]
                 + [cst(a) for a in args[6:]],
        out_specs=[nat(GW), nat(STAT_W)],
        out_shape=[jax.ShapeDtypeStruct((S, GW), BF), jax.ShapeDtypeStruct((S, STAT_W), F32)],
        compiler_params=_cp(), name="attn_merge",
    )(*args)


CT = 512
CBUF = HALO + CT + 8
RG = 4


def _ln_hat(u1):
    mu = jnp.mean(u1, axis=-1, keepdims=True)
    xc = u1 - mu
    rstd = lax.rsqrt(jnp.mean(xc * xc, axis=-1, keepdims=True) + LN_EPS)
    return xc * rstd, rstd


def _glu_window(hu_ref, hg_ref, huh_ref, hgh_ref, bglu_ref, buf_ref, i):
    bu = bglu_ref[:, 0:D]
    bg = bglu_ref[:, D:2 * D]
    uh = (huh_ref[...].astype(F32) + bu) * _sig(hgh_ref[...].astype(F32) + bg)
    buf_ref[0:HALO, :] = jnp.where(i > 0, uh, 0.0)
    a = hu_ref[...].astype(F32) + bu
    s = _sig(hg_ref[...].astype(F32) + bg)
    buf_ref[HALO:HALO + CT, :] = a * s
    buf_ref[HALO + CT:CBUF, :] = jnp.zeros((8, D), F32)
    return a, s


def _shift_copies(buf_ref, sh_ref):
    for r in range(8):
        sh_ref[r] = buf_ref[r:r + HALO + CT, :]


def _tap_rows(wb_ref, w_ref):
    for j in range(CONV_W):
        wb_ref[j * 8:(j + 1) * 8, :] = jnp.broadcast_to(w_ref[j:j + 1, :], (8, D))


def _conv_taps(sh_ref, wb_ref, out_ref, init, offset):
    for rg in range(CT // (8 * RG)):
        accs = [init] * RG
        for j in range(CONV_W):
            off = offset(j)
            wj = wb_ref[j * 8:(j + 1) * 8, :]
            for q in range(RG):
                row = 8 * (rg * RG + q + off // 8)
                accs[q] = accs[q] + wj * sh_ref[off % 8, row:row + 8, :]
        for q in range(RG):
            out_ref[(rg * RG + q) * 8:(rg * RG + q + 1) * 8, :] = accs[q]


def _conv_specs(S):
    cur = lambda cb: pl.BlockSpec((CT, D), lambda i: (i, cb))
    halo = lambda cb: pl.BlockSpec((HALO, D), lambda i: (jnp.maximum(i * (CT // HALO) - 1, 0), cb))
    full = lambda shp: pl.BlockSpec(shp, lambda i: (0, 0))
    return cur, halo, full


def _conv_fwd(z, b_glu, w_dw, b_dw, g_ln, b_ln):
    S = z.shape[0]
    cur, halo, full = _conv_specs(S)

    def body(hu, hg, huh, hgh, bglu, w, bdw, gln, bln, u1_ref, u3_ref, buf, sh, wb):
        i = pl.program_id(0)

        @pl.when(i == 0)
        def _():
            _tap_rows(wb, w)

        _glu_window(hu, hg, huh, hgh, bglu, buf, i)
        _shift_copies(buf, sh)
        _conv_taps(sh, wb, u1_ref, jnp.broadcast_to(bdw[...], (8, D)), lambda j: 2 + j)
        xh, _ = _ln_hat(u1_ref[...])
        u2 = xh * gln[...] + bln[...]
        u3_ref[...] = (u2 * _sig(u2)).astype(BF)

    return pl.pallas_call(
        body, grid=(S // CT,),
        in_specs=[cur(0), cur(1), halo(0), halo(1), full((1, 2 * D)), full((HALO, D)), full((1, D)), full((1, D)), full((1, D))],
        out_specs=[pl.BlockSpec((CT, D), lambda i: (i, 0))] * 2,
        out_shape=[jax.ShapeDtypeStruct((S, D), F32), jax.ShapeDtypeStruct((S, D), BF)],
        scratch_shapes=[pltpu.VMEM((CBUF, D), F32), pltpu.VMEM((8, HALO + CT, D), F32), pltpu.VMEM((CONV_W * 8, D), F32)],
        compiler_params=_cp(("arbitrary",)), name="conv_fwd",
    )(z, z, z, z, b_glu, w_dw, b_dw, g_ln, b_ln)


def _conv_bwd(du1, z, b_glu, w_dw):
    S = z.shape[0]
    n = S // CT
    cur, halo, full = _conv_specs(S)
    RG2 = 2

    def body(du, dun, hu, hg, bglu, w, dz_ref, dw_ref, dbg_ref, bufd, shd, wb, u0_ref, du0_ref, dwacc):
        i = pl.program_id(0)

        @pl.when(i == 0)
        def _():
            _tap_rows(wb, w)
            dwacc[...] = jnp.zeros(dwacc.shape, F32)
            dbg_ref[...] = jnp.zeros(dbg_ref.shape, F32)

        a = hu[...].astype(F32) + bglu[:, 0:D]
        s = _sig(hg[...].astype(F32) + bglu[:, D:2 * D])
        u0_ref[...] = a * s
        bufd[0:CT, :] = du[...]
        bufd[CT:CT + HALO, :] = jnp.where(i < n - 1, dun[...], 0.0)
        bufd[CT + HALO:CBUF, :] = jnp.zeros((8, D), F32)
        _shift_copies(bufd, shd)
        for rg in range(CT // (8 * RG2)):
            uch = [u0_ref[(rg * RG2 + q) * 8:(rg * RG2 + q + 1) * 8, :] for q in range(RG2)]
            accs = [jnp.zeros((8, D), F32)] * RG2
            for j in range(CONV_W):
                off = 30 - j
                wj = wb[j * 8:(j + 1) * 8, :]
                dwj = dwacc[j * 8:(j + 1) * 8, :]
                for q in range(RG2):
                    row = 8 * (rg * RG2 + q + off // 8)
                    x = shd[off % 8, row:row + 8, :]
                    accs[q] = accs[q] + wj * x
                    dwj = dwj + uch[q] * x
                dwacc[j * 8:(j + 1) * 8, :] = dwj
            for q in range(RG2):
                du0_ref[(rg * RG2 + q) * 8:(rg * RG2 + q + 1) * 8, :] = accs[q]
        du0 = du0_ref[...]
        dhu = du0 * s
        dhg = du0 * a * s * (1.0 - s)
        dz_ref[:, 0:D] = dhu.astype(BF)
        dz_ref[:, D:2 * D] = dhg.astype(BF)
        dbg_ref[:, 0:D] += _psum8(dhu)
        dbg_ref[:, D:2 * D] += _psum8(dhg)

        @pl.when(i == n - 1)
        def _():
            dbg_ref[0:1, :] = jnp.sum(dbg_ref[...], axis=0, keepdims=True)
            for j in range(CONV_W):
                dw_ref[j:j + 1, :] = jnp.sum(dwacc[j * 8:(j + 1) * 8, :], axis=0, keepdims=True)
            dw_ref[CONV_W:HALO, :] = jnp.zeros((HALO - CONV_W, D), F32)

    nxt = pl.BlockSpec((HALO, D), lambda i: (jnp.minimum((i + 1) * (CT // HALO), S // HALO - 1), 0))
    return pl.pallas_call(
        body, grid=(n,),
        in_specs=[pl.BlockSpec((CT, D), lambda i: (i, 0)), nxt, cur(0), cur(1), full((1, 2 * D)), full((HALO, D))],
        out_specs=[pl.BlockSpec((CT, 2 * D), lambda i: (i, 0)), full((HALO, D)), full((8, 2 * D))],
        out_shape=[jax.ShapeDtypeStruct((S, 2 * D), BF), jax.ShapeDtypeStruct((HALO, D), F32), jax.ShapeDtypeStruct((8, 2 * D), F32)],
        scratch_shapes=[pltpu.VMEM((CBUF, D), F32), pltpu.VMEM((8, HALO + CT, D), F32), pltpu.VMEM((CONV_W * 8, D), F32),
                        pltpu.VMEM((CT, D), F32), pltpu.VMEM((CT, D), F32), pltpu.VMEM((CONV_W * 8, D), F32)],
        compiler_params=_cp(("arbitrary",)), name="conv_bwd",
    )(du1, du1, z, z, b_glu, w_dw)


MESH = pl.DeviceIdType.MESH


def _all_gather(name, shards):
    n = len(shards)

    def body(*refs):
        ins, outs = refs[:n], refs[n:2 * n]
        send_sems, recv_sems, local_sems = refs[2 * n:]
        x, y, c = lax.axis_index("x"), lax.axis_index("y"), lax.axis_index("c")
        me, sibling = (x, y, c), (x, y, 1 - c)
        chips = [(1 - x, y), (x, 1 - y), (1 - x, 1 - y)]

        def slot(a, px, py, pc):
            return outs[a].at[4 * px + 2 * py + pc]

        def copy(a, k, block, to, src=None):
            return pltpu.make_async_remote_copy(
                src_ref=slot(a, *block) if src is None else src, dst_ref=slot(a, *block),
                send_sem=send_sems.at[a, k], recv_sem=recv_sems.at[a, k], device_id=to, device_id_type=MESH)

        mine = [pltpu.make_async_copy(ins[a], slot(a, *me), local_sems.at[a]) for a in range(n)]
        for cp in mine:
            cp.start()
        first = []
        for a in range(n):
            first.append(copy(a, 0, me, sibling, src=ins[a]))
            first += [copy(a, 1 + j, me, (*chip, c), src=ins[a]) for j, chip in enumerate(chips)]
        for cp in first:
            cp.start()
        passed = []
        for j, chip in enumerate(chips):
            for a in range(n):
                copy(a, 1 + j, (*chip, c), me).wait_recv()
                fwd = copy(a, 4 + j, (*chip, c), sibling)
                fwd.start()
                passed.append(fwd)
        for a in range(n):
            copy(a, 0, sibling, me).wait_recv()
        for j, chip in enumerate(chips):
            for a in range(n):
                copy(a, 4 + j, (*chip, 1 - c), me).wait_recv()
        for cp in first + passed:
            cp.wait_send()
        for cp in mine:
            cp.wait()

    anyspec = pl.BlockSpec(memory_space=pl.ANY)
    return pl.pallas_call(
        body, in_specs=[anyspec] * n, out_specs=[anyspec] * n,
        out_shape=[jax.ShapeDtypeStruct((NDEV,) + s.shape, s.dtype) for s in shards],
        scratch_shapes=[pltpu.SemaphoreType.DMA((n, 7)), pltpu.SemaphoreType.DMA((n, 7)), pltpu.SemaphoreType.DMA((n,))],
        name=name,
    )(*shards)


HBM_SPEC = pl.BlockSpec(memory_space=pltpu.HBM)
SEM_SPEC = pl.BlockSpec(memory_space=pltpu.SEMAPHORE)
DATAFLOW = pltpu.SideEffectType.DATAFLOW_SIDE_EFFECTING


def _peers():
    x, y, c = lax.axis_index("x"), lax.axis_index("y"), lax.axis_index("c")
    out = []
    for k in range(1, NDEV):
        px = 1 - x if k & 4 else x
        py = 1 - y if k & 2 else y
        pc = 1 - c if k & 1 else c
        out.append(((px, py, pc), 4 * px + 2 * py + pc))
    return 4 * x + 2 * y + c, out


def _exchange_copies(srcs, lands, send_sems, recv_sems, gather):
    my, peers = _peers()
    pairs = []
    for k, (dev, pid) in enumerate(peers):
        for a in range(len(srcs)):
            src = srcs[a] if gather else srcs[a].at[pid]
            sems = dict(send_sem=send_sems[a * (NDEV - 1) + k], recv_sem=recv_sems[a * (NDEV - 1) + k], device_id=dev,
                        device_id_type=MESH)
            pairs.append((pltpu.make_async_remote_copy(src_ref=src, dst_ref=lands[a].at[my], **sems),
                          pltpu.make_async_remote_copy(src_ref=src, dst_ref=lands[a].at[pid], **sems)))
    return pairs


def _exchange_start(name, srcs, gather):
    n = len(srcs)
    ns = n * (NDEV - 1)
    shapes = [(s.shape if gather else s.shape[1:]) for s in srcs]
    lands = [lax.empty((NDEV,) + shp, s.dtype) for shp, s in zip(shapes, srcs)]

    def body(*refs):
        src_refs, land_refs = refs[:n], refs[n:2 * n]
        send_sems, recv_sems = refs[2 * n:2 * n + ns], refs[2 * n + ns:2 * n + 2 * ns]
        token = refs[-1]
        for mine, _ in _exchange_copies(src_refs, land_refs, send_sems, recv_sems, gather):
            mine.start()
        token[...] = jnp.zeros(token.shape, token.dtype)

    hbm = lambda a: pltpu.HBM(a.shape, a.dtype)
    res = pl.pallas_call(
        body, name=name,
        out_shape=(*([pltpu.SemaphoreType.DMA(())] * (2 * ns)), *[hbm(s) for s in srcs], *[hbm(l) for l in lands],
                   jax.ShapeDtypeStruct((8, 128), F32)),
        in_specs=[HBM_SPEC] * (2 * n),
        out_specs=(*([SEM_SPEC] * (2 * ns)), *([HBM_SPEC] * (2 * n)), pl.BlockSpec(memory_space=pltpu.VMEM)),
        input_output_aliases={i: 2 * ns + i for i in range(2 * n)},
        compiler_params=pltpu.CompilerParams(has_side_effects=DATAFLOW),
    )(*[pltpu.with_memory_space_constraint(s, pltpu.HBM) for s in srcs],
      *[pltpu.with_memory_space_constraint(l, pltpu.HBM) for l in lands])
    return list(res[:ns]), list(res[ns:2 * ns]), list(res[2 * ns:2 * ns + n]), list(res[2 * ns + n:2 * ns + 2 * n]), res[-1]


def _exchange_wait(name, handle, after, gather):
    send_sems, recv_sems, srcs, lands, _ = handle
    n = len(srcs)
    ns = n * (NDEV - 1)

    def body(*refs):
        src_refs, land_refs = refs[:n], refs[n:2 * n]
        s_sems, r_sems = refs[2 * n:2 * n + ns], refs[2 * n + ns:2 * n + 2 * ns]
        for mine, theirs in _exchange_copies(src_refs, land_refs, s_sems, r_sems, gather):
            mine.wait_send()
            theirs.wait_recv()

    hbm = lambda a: pltpu.HBM(a.shape, a.dtype)
    res = pl.pallas_call(
        body, name=name,
        out_shape=(*[hbm(s) for s in srcs], *[hbm(l) for l in lands]),
        in_specs=[HBM_SPEC] * (2 * n) + [SEM_SPEC] * (2 * ns) + [pl.BlockSpec(memory_space=pl.ANY)],
        out_specs=tuple([HBM_SPEC] * (2 * n)),
        input_output_aliases={i: i for i in range(2 * n)},
        compiler_params=pltpu.CompilerParams(has_side_effects=DATAFLOW),
    )(*srcs, *lands, *send_sems, *recv_sems, after)
    return list(res[n:])


def _set_own_slot(land, own):
    my = 4 * lax.axis_index("x") + 2 * lax.axis_index("y") + lax.axis_index("c")
    return lax.dynamic_update_slice(land, own[None], (my, 0, 0))


def _own_block(blocks):
    my = 4 * lax.axis_index("x") + 2 * lax.axis_index("y") + lax.axis_index("c")
    return lax.dynamic_index_in_dim(blocks, my, axis=0, keepdims=False)


_C1 = 1.0 - ADAM_B1 ** ADAM_STEP
_C2 = 1.0 - ADAM_B2 ** ADAM_STEP


def _adamw(name, w, m, v, recv, tr):
    R, C = w.shape

    def body(w_ref, m_ref, v_ref, r_ref, g_ref, d_ref, nm_ref, nv_ref):
        g = r_ref[0].astype(F32)
        for s in range(1, NDEV):
            g = g + r_ref[s].astype(F32)
        wv = w_ref[...]
        nm = ADAM_B1 * m_ref[...] + (1.0 - ADAM_B1) * g
        nv = ADAM_B2 * v_ref[...] + (1.0 - ADAM_B2) * (g * g)
        m_hat = nm / _C1
        v_hat = nv / _C2
        g_ref[...] = g
        d_ref[...] = -ADAM_LR * (m_hat / (jnp.sqrt(v_hat) + ADAM_EPS) + ADAM_WD * wv)
        nm_ref[...] = nm
        nv_ref[...] = nv

    blk = pl.BlockSpec((tr, C), lambda i: (i, 0))
    return pl.pallas_call(
        body, grid=(R // tr,), in_specs=[blk, blk, blk, pl.BlockSpec((NDEV, tr, C), lambda i: (0, i, 0))],
        out_specs=[blk] * 4, out_shape=[jax.ShapeDtypeStruct((R, C), F32)] * 4,
        compiler_params=_cp(), name=name,
    )(w, m, v, recv)


def _row(v):
    return v.reshape(1, -1)


def _local_step(xs, tgt, Wp, rest_fn, early_fn, late_fn, rel_bias_table, g_pre_mix, b_glu, b_dw, g_conv_ln,
                b_conv_ln, b_conv_out, g_post_mix, g_pre_ffn, g_post_ffn):
    S = xs.shape[0]
    g1, g2, g3, g4 = _row(g_pre_mix), _row(g_post_mix), _row(g_pre_ffn), _row(g_post_ffn)
    bglu, bdw, gln, bln, bco = _row(b_glu), _row(b_dw), _row(g_conv_ln), _row(b_conv_ln), _row(b_conv_out)
    full = (D, F32, D, 0, False)
    fullb = (D, BF, D, 0, False)

    h1, z = _in_proj(xs, g1, Wp)

    idx = jnp.asarray(_band_index())
    tab_t = rel_bias_table.T.reshape(3, HEADS, REL_BUCKETS)
    bias_all = _bias_build(tab_t, idx)
    bias_kq = [bias_all[g].reshape(HEADS, 2 * QBLK, QBLK) for g in range(3)]
    bias_kq2 = [bias_all[3 + g].reshape(HEADS, QBLK, 2 * QBLK) for g in range(3)]
    dils = [d for _, d in GROUPS]
    qkv = [(z, _kvq_blocks(0))] + [(_to_residue(f"qkv_to_residue_g{g}", z, _kvq_blocks(g), dils[g]), (0, 1, 2)) for g in (1, 2)]
    os_, ls_ = [], []
    for g in range(3):
        o_g, l_g = _attn_fwd(f"attn_fwd_g{g}", qkv[g][0], bias_kq[g], qkv[g][1], dils[g])
        os_.append(o_g)
        ls_.append(l_g)
    o_att, lse = _attn_merge(os_, ls_, S)

    Wfi, Wfo, Wco, Wmo, Wao, wdw = rest_fn(lse)
    u1, u3 = _conv_fwd(z, bglu, wdw, bdw, gln, bln)

    def epi_mix(accs, r, c, o, p):
        ya = accs[0]
        yc = accs[1] + c[0][...]
        mg = _sig(r[0][...].astype(F32)) * ya + _sig(r[1][...].astype(F32)) * yc
        mgb = mg.astype(BF)
        m2 = jnp.dot(mgb, c[1][...], preferred_element_type=F32)
        x1 = r[2][...] + m2 * _rms_r(m2) * c[2][...]
        o[0][...] = ya.astype(BF)
        o[1][...] = yc.astype(BF)
        o[2][...] = mgb
        o[3][...] = m2.astype(BF)
        o[4][...] = x1
        o[5][...] = (x1 * _rms_r(x1) * c[3][...]).astype(BF)

    y_attn, y_conv, merged, m2, x1, h2 = _fused_mm(
        "mix_fwd", S, 512, 1, [(o_att, GW, 0, 1), (u3, D, 0, 1)],
        [(Wao, False, GW, D, 0, 1, 0, 0, False), (Wco, False, D, D, 0, 1, 0, 0, False)], [(0, 0, 0, 0, 1), (1, 1, 1, 0, 1)],
        [(512, D), (512, D)], [(z, D, 2, False), (z, D, 3, False), (xs, D, 0, False)], [bco, Wmo, g2, g3],
        [fullb, fullb, fullb, fullb, full, fullb], [], epi_mix)

    HN = FFN // 2

    def epi_ffn_in(accs, r, c, o, p):
        gt, up = accs
        o[0][...] = gt.astype(BF)
        o[1][...] = up.astype(BF)
        o[2][...] = (gt * _sig(gt) * up).astype(BF)

    gate, up, act = _fused_mm(
        "ffn_in", S, 512, 2, [(h2, D, 0, 1)],
        [(Wfi, False, D, HN, 0, 1, 0, 0, True), (Wfi, False, D, HN, 0, 1, 0, 2, True)], [(0, 0, 0, 0, 1), (0, 1, 1, 0, 1)],
        [(512, HN), (512, HN)], [], [], [(FFN, BF, HN, 0, True)] * 3, [], epi_ffn_in, n_outer=True)

    def epi_loss(accs, r, c, o, p):
        f2 = accs[0]
        g = c[0][...]
        rr = _rms_r(f2)
        err = r[0][...] + f2 * rr * g - r[1][...]
        dy = err * (1.0 / D)
        df2, dgr = _rms_bwd(f2, rr, g, dy)
        o[0][...] = dy
        o[1][...] = df2.astype(BF)
        p[0][...] += _psum8(err * err)
        p[1][...] += _psum8(dgr)

    dy, df2, loss_p, dg4 = _fused_mm(
        "ffn_out_loss", S, 512, 1, [(act, FFN, 0, 1)], [(Wfo, False, FFN, D, 0, 1, 0, 0, False)], [(0, 0, 0, 0, 1)],
        [(512, D)], [(x1, D, 0, False), (tgt, D, 0, False)], [g4], [full, fullb], [(8, D), (8, D)], epi_loss)

    def epi_swiglu(accs, r, c, o, p):
        da = accs[0]
        gt = r[0][...].astype(F32)
        sg = _sig(gt)
        o[0][...] = (da * r[1][...].astype(F32) * sg * (1.0 + gt * (1.0 - sg))).astype(BF)
        o[1][...] = (da * gt * sg).astype(BF)

    dgate, dup = _fused_mm(
        "ffn_out_bwd", S, 512, 2, [(df2, D, 0, 1)], [(Wfo, True, D, HN, 0, 1, 0, 0, True)], [(0, 0, 0, 0, 1)],
        [(512, HN)], [(gate, HN, 0, True), (up, HN, 0, True)], [], [(FFN, BF, HN, 0, True)] * 2, [], epi_swiglu, n_outer=True)
    dWfo = _mm_tn("dw_ffn_out", act, df2, HN, D, WG_TK)

    def epi_dh2(accs, r, c, o, p):
        dh2 = accs[0]
        x1v = r[1][...]
        r3 = _rms_r(x1v)
        d1, dg3r = _rms_bwd(x1v, r3, c[0][...], dh2)
        dx1 = r[0][...] + d1
        m2v = r[2][...].astype(F32)
        r2 = _rms_r(m2v)
        dm2, dg2r = _rms_bwd(m2v, r2, c[1][...], dx1)
        o[0][...] = dx1
        o[1][...] = dm2.astype(BF)
        p[0][...] += _psum8(dg3r)
        p[1][...] += _psum8(dg2r)

    dx1, dm2, dg3, dg2 = _resident_mm(
        "ffn_in_bwd", S, 512, [dgate, dup], Wfi, [(dy, D), (x1, D), (m2, D)], [g3, g2], [(D, F32), (D, BF)], [(8, D), (8, D)], epi_dh2)
    dWfi = jnp.concatenate([_mm_tn("dw_ffn_gate", h2, dgate, D, HN, WG_TK), _mm_tn("dw_ffn_up", h2, dup, D, HN, WG_TK)], axis=1)

    def epi_dmix(accs, r, c, o, p):
        dm = accs[0]
        sa = _sig(r[0][...].astype(F32))
        sc = _sig(r[1][...].astype(F32))
        o[0][...] = (dm * sa).astype(BF)
        o[1][...] = (dm * sc).astype(BF)
        o[2][:, 0:D] = (dm * r[2][...].astype(F32) * sa * (1.0 - sa)).astype(BF)
        o[2][:, D:2 * D] = (dm * r[3][...].astype(F32) * sc * (1.0 - sc)).astype(BF)

    dy_attn, dy_conv, dz_gate = _fused_mm(
        "mix_bwd", S, 512, 1, [(dm2, D, 0, 1)], [(Wmo, True, D, D, 0, 1, 0, 0, False)], [(0, 0, 0, 0, 1)], [(512, D)],
        [(z, D, 2, False), (z, D, 3, False), (y_attn, D, 0, False), (y_conv, D, 0, False)], [],
        [fullb, fullb, (2 * D, BF, 2 * D, 0, False)], [], epi_dmix)
    dWmo = _mm_tn("dw_mix_out", merged, dm2, D, D, WG_TK)

    def epi_dconv(accs, r, c, o, p):
        du3 = accs[0]
        xh, rstd = _ln_hat(r[0][...])
        gl = c[0][...]
        u2 = xh * gl + c[1][...]
        sg = _sig(u2)
        du2 = du3 * sg * (1.0 + u2 * (1.0 - sg))
        dxh = du2 * gl
        du1 = rstd * (dxh - jnp.mean(dxh, axis=-1, keepdims=True) - xh * jnp.mean(dxh * xh, axis=-1, keepdims=True))
        o[0][...] = du1
        p[0][...] += _psum8(du2 * xh)
        p[1][...] += _psum8(du2)
        p[2][...] += _psum8(du1)
        p[3][...] += _psum8(r[1][...].astype(F32))

    du1, dgln, dbln, dbdw, dbco = _fused_mm(
        "conv_out_bwd", S, 512, 1, [(dy_conv, D, 0, 1)], [(Wco, True, D, D, 0, 1, 0, 0, False)], [(0, 0, 0, 0, 1)], [(512, D)],
        [(u1, D, 0, False), (dy_conv, D, 0, False)], [gln, bln], [full], [(8, D)] * 4, epi_dconv)
    dWco = _mm_tn("dw_conv_out", u3, dy_conv, D, D, WG_TK)
    dz_glu, dwdw, dbglu = _conv_bwd(du1, z, bglu, wdw)

    head_sum = np.zeros((GW, STAT_W), np.float32)
    for h in range(HEADS):
        head_sum[h * HEAD_DIM:(h + 1) * HEAD_DIM, HEADS + h] = 1.0
    head_sum = jnp.asarray(head_sum)

    def epi_do(accs, r, c, o, p):
        do = accs[0]
        o[0][...] = do.astype(BF)
        delta = jnp.dot(do * r[0][...].astype(F32), c[0][...], preferred_element_type=F32, precision=lax.Precision.HIGHEST)
        lane = lax.broadcasted_iota(jnp.int32, delta.shape, 1)
        o[1][...] = jnp.where(lane < HEADS, r[1][...], delta)

    do, stats = _fused_mm(
        "attn_out_bwd", S, 1024, 1, [(dy_attn, D, 0, 1)], [(Wao, True, D, GW, 0, 1, 0, 0, False)], [(0, 0, 0, 0, 1)], [(1024, GW)],
        [(o_att, GW, 0, False), (lse, STAT_W, 0, False)], [head_sum], [(GW, BF, GW, 0, False), (STAT_W, F32, STAT_W, 0, False)], [], epi_do)
    dWao = _mm_tn("dw_attn_out", o_att, dy_attn, GW, D, WG_TK)

    tie = early_fn(dict(w_ffn_in=dWfi, w_ffn_out=dWfo, w_conv_out=dWco, w_mix_out=dWmo, w_attn_out=dWao, w_dw=dwdw))
    stats = stats + tie
    moved = [_to_residue_pair(f"do_stats_to_residue_g{g}", do, stats, dils[g]) for g in (1, 2)]
    dos = [do] + [m_[0] for m_ in moved]
    sts = [stats] + [m_[1] for m_ in moved]
    dqkv, dbs = [], []
    for g in range(3):
        arr, cb = qkv[g]
        dg, db = _attn_bwd(f"attn_bwd_g{g}", arr, bias_kq2[g], dos[g], sts[g], cb, dils[g])
        dqkv.append(dg if g == 0 else _from_residue(f"dqkv_from_residue_g{g}", dg, dils[g]))
        dbs.append(db.reshape(HEADS, _NB))
    dtab = _bias_grad(jnp.stack(dbs), idx)[:, :, :REL_BUCKETS].reshape(3 * HEADS, REL_BUCKETS).T

    def epi_dx(accs, r, c, o, p):
        xv = r[1][...]
        d1, dg1r = _rms_bwd(xv, _rms_r(xv), c[0][...], accs[0])
        o[0][...] = r[0][...] + d1
        p[0][...] += _psum8(dg1r)

    dWg = [_mm_tn(f"dw_in_g{g}", h1, dqkv[g], D, ATTN_W, WG_TK) for g in range(3)]
    dW_in = jnp.concatenate(
        [t[:, 2 * GW:] for t in dWg] + [t[:, :GW] for t in dWg] + [t[:, GW:2 * GW] for t in dWg]
        + [_mm_tn("dw_in_glu", h1, dz_glu, D, D, WG_TK), _mm_tn("dw_in_gate", h1, dz_gate, D, D, WG_TK)], axis=1)
    g1_late = g1 + late_fn(dW_in)
    grad_x, dg1 = _resident_mm(
        "in_proj_bwd", S, 512, [dz_glu, dz_gate, dqkv[0], dqkv[1], dqkv[2]], Wp, [(dx1, D), (xs, D)], [g1_late], [(D, F32)], [(8, D)], epi_dx)

    small = dict(rel_bias_table=dtab, g_pre_mix=dg1[0], b_glu=dbglu[0], b_dw=dbdw[0], g_conv_ln=dgln[0], b_conv_ln=dbln[0],
                 b_conv_out=dbco[0], g_post_mix=dg2[0], g_pre_ffn=dg3[0], g_post_ffn=dg4[0])
    return loss_p[0], grad_x, small


SMALL = ['rel_bias_table', 'g_pre_mix', 'b_glu', 'b_dw', 'g_conv_ln', 'b_conv_ln', 'b_conv_out', 'g_post_mix', 'g_pre_ffn',
         'g_post_ffn']
BIG = ['w_in', 'w_ffn_in', 'w_ffn_out', 'w_conv_out', 'w_mix_out', 'w_attn_out', 'w_dw']
WEIGHTS = ['rel_bias_table', 'g_pre_mix', 'w_in', 'b_glu', 'w_dw', 'b_dw', 'g_conv_ln', 'b_conv_ln', 'w_conv_out', 'b_conv_out',
           'w_attn_out', 'w_mix_out', 'g_post_mix', 'g_pre_ffn', 'w_ffn_in', 'w_ffn_out', 'g_post_ffn']
SMALL_ROWS = 16


ROW_SMALL = ['g_pre_mix', 'b_glu', 'b_dw', 'g_conv_ln', 'b_conv_ln', 'b_conv_out', 'g_post_mix', 'g_pre_ffn', 'g_post_ffn']
LOSS_ROW = 10
TAB_LANES = 128


def _small_rows(small, loss_row):
    rows = [small[n].reshape(-1, D) for n in ROW_SMALL] + [loss_row.reshape(1, D)]
    n = sum(r.shape[0] for r in rows)
    return jnp.concatenate(rows + [jnp.zeros((SMALL_ROWS - n, D), F32)], axis=0)


def _adamw_small(recv_rows, recv_tab, ws, ms, vs):
    np_ = len(SMALL)

    def body(*refs):
        rr, rt = refs[0], refs[1]
        w_refs, m_refs, v_refs = refs[2:2 + np_], refs[2 + np_:2 + 2 * np_], refs[2 + 2 * np_:2 + 3 * np_]
        loss_ref = refs[2 + 3 * np_]
        outs = refs[3 + 3 * np_:]
        rows = rr[0]
        tab = rt[0]
        for s_ in range(1, NDEV):
            rows = rows + rr[s_]
            tab = tab + rt[s_]
        loss_ref[...] = jnp.sum(rows[LOSS_ROW:LOSS_ROW + 1, :], axis=1, keepdims=True) * (0.5 / D)
        row = 0
        for p, n in enumerate(SMALL):
            if n == 'rel_bias_table':
                g = tab[:, 0:3 * HEADS]
            else:
                k = w_refs[p].shape[1] // D
                g = rows[row:row + 1, :] if k == 1 else jnp.concatenate([rows[row + t:row + t + 1, :] for t in range(k)], axis=1)
                row += k
            nm = ADAM_B1 * m_refs[p][...] + (1.0 - ADAM_B1) * g
            nv = ADAM_B2 * v_refs[p][...] + (1.0 - ADAM_B2) * (g * g)
            outs[4 * p][...] = g
            outs[4 * p + 1][...] = -ADAM_LR * ((nm / _C1) / (jnp.sqrt(nv / _C2) + ADAM_EPS) + ADAM_WD * w_refs[p][...])
            outs[4 * p + 2][...] = nm
            outs[4 * p + 3][...] = nv

    out_shape = [jax.ShapeDtypeStruct((1, 1), F32)]
    for a_ in ws:
        out_shape += [jax.ShapeDtypeStruct(a_.shape, F32)] * 4
    res = pl.pallas_call(body, out_shape=out_shape, compiler_params=_cp(), name="adamw_small")(recv_rows, recv_tab, *ws, *ms, *vs)
    return res[0], [tuple(res[1 + 4 * p:5 + 4 * p]) for p in range(np_)]


def _cols_to_blocks(a):
    R = a.shape[0]
    return a.reshape(R, NDEV, a.shape[1] // NDEV).transpose(1, 0, 2)


def _blocks_to_cols(a):
    return a.transpose(1, 0, 2).reshape(a.shape[1], NDEV * a.shape[2])


def kernel(x, rel_bias_table, g_pre_mix, w_in, b_glu, w_dw, b_dw, g_conv_ln, b_conv_ln, w_conv_out, b_conv_out, w_attn_out, w_mix_out, g_post_mix, g_pre_ffn, w_ffn_in, w_ffn_out, g_post_ffn, loss_target, m_rel_bias_table, m_g_pre_mix, m_w_in, m_b_glu, m_w_dw, m_b_dw, m_g_conv_ln, m_b_conv_ln, m_w_conv_out, m_b_conv_out, m_w_attn_out, m_w_mix_out, m_g_post_mix, m_g_pre_ffn, m_w_ffn_in, m_w_ffn_out, m_g_post_ffn, v_rel_bias_table, v_g_pre_mix, v_w_in, v_b_glu, v_w_dw, v_b_dw, v_g_conv_ln, v_b_conv_ln, v_w_conv_out, v_b_conv_out, v_w_attn_out, v_w_mix_out, v_g_post_mix, v_g_pre_ffn, v_w_ffn_in, v_w_ffn_out, v_g_post_ffn):
    w = dict(rel_bias_table=rel_bias_table, g_pre_mix=g_pre_mix, w_in=w_in, b_glu=b_glu, w_dw=w_dw, b_dw=b_dw, g_conv_ln=g_conv_ln, b_conv_ln=b_conv_ln, w_conv_out=w_conv_out, b_conv_out=b_conv_out, w_attn_out=w_attn_out, w_mix_out=w_mix_out, g_post_mix=g_post_mix, g_pre_ffn=g_pre_ffn, w_ffn_in=w_ffn_in, w_ffn_out=w_ffn_out, g_post_ffn=g_post_ffn)
    m = dict(rel_bias_table=m_rel_bias_table, g_pre_mix=m_g_pre_mix, w_in=m_w_in, b_glu=m_b_glu, w_dw=m_w_dw, b_dw=m_b_dw, g_conv_ln=m_g_conv_ln, b_conv_ln=m_b_conv_ln, w_conv_out=m_w_conv_out, b_conv_out=m_b_conv_out, w_attn_out=m_w_attn_out, w_mix_out=m_w_mix_out, g_post_mix=m_g_post_mix, g_pre_ffn=m_g_pre_ffn, w_ffn_in=m_w_ffn_in, w_ffn_out=m_w_ffn_out, g_post_ffn=m_g_post_ffn)
    v = dict(rel_bias_table=v_rel_bias_table, g_pre_mix=v_g_pre_mix, w_in=v_w_in, b_glu=v_b_glu, w_dw=v_w_dw, b_dw=v_b_dw, g_conv_ln=v_g_conv_ln, b_conv_ln=v_b_conv_ln, w_conv_out=v_w_conv_out, b_conv_out=v_b_conv_out, w_attn_out=v_w_attn_out, w_mix_out=v_w_mix_out, g_post_mix=v_g_post_mix, g_pre_ffn=v_g_pre_ffn, w_ffn_in=v_w_ffn_in, w_ffn_out=v_w_ffn_out, g_post_ffn=v_g_post_ffn)

    def shard2d(d, n):
        a = d[n][0]
        return jnp.pad(a, ((0, HALO - CONV_W), (0, 0))) if n == 'w_dw' else a

    own = {n: shard2d(w, n).astype(F32 if n == 'w_dw' else BF) for n in BIG}
    packed = ['w_ffn_out', 'w_conv_out', 'w_mix_out', 'w_attn_out']
    alone = ['w_ffn_in', 'w_dw']
    shapes = [own[n].shape for n in packed]

    def pack(arrs, lead):
        return jnp.concatenate([a.reshape(lead + (-1, D)) for a in arrs], axis=len(lead))

    def unpack(p):
        out, pos = {}, 0
        for n, shp in zip(packed, shapes):
            rows = shp[0] * shp[1] // D
            out[n] = p[:, pos:pos + rows].reshape((NDEV,) + shp)
            pos += rows
        return out

    (g_in,) = _all_gather("gather_w_in", [own['w_in']])
    rest_own = [pack([own[n] for n in packed], ())] + [own[n] for n in alone]
    g_in, rest_own = lax.optimization_barrier((g_in, rest_own))
    gather_rest = _exchange_start("gather_rest_start", rest_own, True)
    W_in = _blocks_to_cols(g_in)
    kvq = [W_in[:, t * ATTN_W + g * GW:t * ATTN_W + (g + 1) * GW] for g in range(3) for t in (1, 2, 0)]
    Wp = jnp.concatenate([W_in[:, 3 * ATTN_W:]] + kvq, axis=1)

    def rest_fn(after):
        lands = _exchange_wait("gather_rest_wait", gather_rest, after, True)
        gw = unpack(_set_own_slot(lands[0], rest_own[0]))
        for n, l, o in zip(alone, lands[1:], rest_own[1:]):
            gw[n] = _set_own_slot(l, o)
        return (_blocks_to_cols(gw['w_ffn_in']), gw['w_ffn_out'].reshape(FFN, D), gw['w_conv_out'].reshape(D, D),
                gw['w_mix_out'].reshape(D, D), _blocks_to_cols(gw['w_attn_out']), _blocks_to_cols(gw['w_dw']))

    def to_blocks(n, g):
        if n in ('w_in', 'w_ffn_in', 'w_attn_out', 'w_dw'):
            return _cols_to_blocks(g)
        return g.reshape(NDEV, g.shape[0] // NDEV, g.shape[1])

    started = {}

    def early_fn(grads):
        blocks = [pack([to_blocks(n, grads[n]) for n in packed], (NDEV,))] + [to_blocks(n, grads[n]) for n in alone]
        started['blocks'] = blocks
        started['handle'] = _exchange_start("scatter_early_start", blocks, False)
        return started['handle'][4][0:1, 0:1]

    def late_fn(dW_in):
        started['in_blocks'] = [to_blocks('w_in', dW_in)]
        started['in_handle'] = _exchange_start("scatter_w_in_start", started['in_blocks'], False)
        return started['in_handle'][4][0:1, 0:1]

    g1_tied = g_pre_mix[0] + gather_rest[4][0, 0:1]
    loss_row, grad_x, small = _local_step(
        x[0], loss_target[0], Wp, rest_fn, early_fn, late_fn, rel_bias_table, g1_tied, b_glu[0], b_dw[0], g_conv_ln[0],
        b_conv_ln[0], b_conv_out[0], g_post_mix[0], g_pre_ffn[0], g_post_ffn[0])

    lands = _exchange_wait("scatter_early_wait", started['handle'], grad_x, False)
    lands = [_set_own_slot(l, _own_block(b)) for l, b in zip(lands, started['blocks'])]
    recv = unpack(lands[0])
    recv.update(zip(alone, lands[1:]))
    (land_in,) = _exchange_wait("scatter_w_in_wait", started['in_handle'], grad_x, False)
    recv['w_in'] = _set_own_slot(land_in, _own_block(started['in_blocks'][0]))
    tiles = dict(w_in=128, w_ffn_in=256, w_ffn_out=176, w_conv_out=128, w_mix_out=128, w_attn_out=512, w_dw=HALO)
    res = {}
    for n in BIG:
        g_, d_, nm_, nv_ = _adamw("adamw_" + n, shard2d(w, n), shard2d(m, n), shard2d(v, n), recv[n], tiles[n])
        if n == 'w_dw':
            g_, d_, nm_, nv_ = (t[:CONV_W] for t in (g_, d_, nm_, nv_))
        res[n] = tuple(t[None] for t in (g_, d_, nm_, nv_))

    tab = jnp.pad(small['rel_bias_table'], ((0, 0), (0, TAB_LANES - 3 * HEADS)))
    srows, stab = _all_gather("gather_small_grads", [_small_rows(small, loss_row), tab])
    loss11, small_res = _adamw_small(srows, stab, [w[n] for n in SMALL], [m[n] for n in SMALL], [v[n] for n in SMALL])
    loss = loss11.reshape(())
    for n, r in zip(SMALL, small_res):
        res[n] = r
    return (loss, grad_x[None], *[res[n][0] for n in WEIGHTS], *[res[n][1] for n in WEIGHTS],
            *[res[n][2] for n in WEIGHTS], *[res[n][3] for n in WEIGHTS])
```

```python
import functools
import math

import numpy as np
import jax
import jax.numpy as jnp
from jax import lax
from jax.experimental import pallas as pl
from jax.experimental.pallas import tpu as pltpu

F32 = jnp.float32
BF = jnp.bfloat16

D = 1024
HEAD_DIM = 64
HEADS = 8
GROUPS = ((128, 1), (512, 4), (2048, 16))
QBLK = 128
GW = HEADS * HEAD_DIM
ATTN_W = 3 * GW
REL_BUCKETS = 32
REL_MAX_DISTANCE = 2048
CONV_W = 31
HALO = 32
FFN = 2816
IN_W = 3 * ATTN_W + 2 * D + 2 * D
RMS_EPS = 1e-6
LN_EPS = 1e-5
NEG_INF = -1e30
SCALE = HEAD_DIM ** -0.5
NDEV = 8

ADAM_LR = 0.001
ADAM_B1 = 0.9
ADAM_B2 = 0.999
ADAM_EPS = 1e-08
ADAM_WD = 0.01
ADAM_STEP = 10

Z_G0 = 4096 // GW


def _kvq_blocks(g):
    return (Z_G0 + 3 * g, Z_G0 + 3 * g + 1, Z_G0 + 3 * g + 2)


WG_TK = 2048
VMEM_LIMIT = 52 * 1024 * 1024


def _cp(sem=None):
    if sem is None:
        return pltpu.CompilerParams(vmem_limit_bytes=VMEM_LIMIT)
    return pltpu.CompilerParams(vmem_limit_bytes=VMEM_LIMIT, dimension_semantics=sem)


def _sig(v):
    return jax.nn.sigmoid(v)


def _psum8(v):
    return v.reshape(v.shape[0] // 8, 8, v.shape[1]).sum(axis=0)


def _rms_r(v):
    return lax.rsqrt(jnp.mean(v * v, axis=-1, keepdims=True) + RMS_EPS)


def _rms_bwd(v, r, g, dy):
    gy = dy * g
    dv = r * gy - v * (r * r * r) * jnp.mean(v * gy, axis=-1, keepdims=True)
    return dv, dy * v * r


def _clip_k(k, k0, nk):
    return jnp.clip(k - k0, 0, nk - 1)


def _fused_mm(name, M, tm, grid_n, a_ops, b_ops, terms, acc_shapes, rows, consts, outs, parts, epilogue, n_outer=False):
    gm = M // tm
    nk_total = max([t[3] + t[4] for t in terms], default=1)
    n_a, n_b, n_r, n_c, n_o, n_p = len(a_ops), len(b_ops), len(rows), len(consts), len(outs), len(parts)
    n_acc = len(acc_shapes)
    use_scratch = nk_total > 1
    if parts:
        assert grid_n == 1

    def jj(j, follow):
        return j if follow else 0

    in_specs, args = [], []
    for (arr, tk, k0, nk) in a_ops:
        in_specs.append(pl.BlockSpec((tm, tk), functools.partial(lambda i, j, k, k0, nk: (i, _clip_k(k, k0, nk)), k0=k0, nk=nk)))
        args.append(arr)
    for (arr, nt, tk, tn, k0, nk, koff, joff, fj) in b_ops:
        if nt:
            in_specs.append(pl.BlockSpec((tn, tk), functools.partial(
                lambda i, j, k, k0, nk, koff, joff, fj: (joff + jj(j, fj), _clip_k(k, k0, nk) + koff),
                k0=k0, nk=nk, koff=koff, joff=joff, fj=fj)))
        else:
            in_specs.append(pl.BlockSpec((tk, tn), functools.partial(
                lambda i, j, k, k0, nk, koff, joff, fj: (_clip_k(k, k0, nk) + koff, joff + jj(j, fj)),
                k0=k0, nk=nk, koff=koff, joff=joff, fj=fj)))
        args.append(arr)
    for (arr, w, off, fj) in rows:
        in_specs.append(pl.BlockSpec((tm, w), functools.partial(lambda i, j, k, off, fj: (i, off + jj(j, fj)), off=off, fj=fj)))
        args.append(arr)
    for arr in consts:
        in_specs.append(pl.BlockSpec(arr.shape, functools.partial(lambda i, j, k, nd: (0,) * nd, nd=arr.ndim)))
        args.append(arr)
    out_specs, out_shape = [], []
    for (ncols, dt, w, off, fj) in outs:
        out_specs.append(pl.BlockSpec((tm, w), functools.partial(lambda i, j, k, off, fj: (i, off + jj(j, fj)), off=off, fj=fj)))
        out_shape.append(jax.ShapeDtypeStruct((M, ncols), dt))
    for (r, c) in parts:
        out_specs.append(pl.BlockSpec((r, c), lambda i, j, k: (0, 0)))
        out_shape.append(jax.ShapeDtypeStruct((r, c), F32))
    scratch = [pltpu.VMEM(s, F32) for s in acc_shapes] if use_scratch else []

    def body(*refs):
        pos = 0
        a_refs = refs[pos:pos + n_a]; pos += n_a
        b_refs = refs[pos:pos + n_b]; pos += n_b
        r_refs = refs[pos:pos + n_r]; pos += n_r
        c_refs = refs[pos:pos + n_c]; pos += n_c
        o_refs = refs[pos:pos + n_o]; pos += n_o
        p_refs = refs[pos:pos + n_p]; pos += n_p
        acc_refs = refs[pos:pos + n_acc] if use_scratch else ()
        i = pl.program_id(0)
        k = pl.program_id(2)

        def dot_of(ai, bi):
            a = a_refs[ai][...].astype(BF)
            b = b_refs[bi][...].astype(BF)
            if b_ops[bi][1]:
                return lax.dot_general(a, b, (((1,), (1,)), ((), ())), preferred_element_type=F32)
            return jnp.dot(a, b, preferred_element_type=F32)

        if parts:
            @pl.when((i == 0) & (k == 0))
            def _():
                for p in p_refs:
                    p[...] = jnp.zeros(p.shape, F32)

        def finish(accs):
            epilogue(accs, r_refs, c_refs, o_refs, p_refs)
            if parts:
                @pl.when(i == gm - 1)
                def _():
                    for p in p_refs:
                        p[0:1, :] = jnp.sum(p[...], axis=0, keepdims=True)

        if not use_scratch:
            accs = [None] * n_acc
            for (ai, bi, ci, k0, nk) in terms:
                d = dot_of(ai, bi)
                accs[ci] = d if accs[ci] is None else accs[ci] + d
            finish(accs)
        else:
            @pl.when(k == 0)
            def _():
                for acc in acc_refs:
                    acc[...] = jnp.zeros(acc.shape, F32)

            for (ai, bi, ci, k0, nk) in terms:
                def do(ai=ai, bi=bi, ci=ci):
                    acc_refs[ci][...] += dot_of(ai, bi)
                if k0 == 0 and nk == nk_total:
                    do()
                else:
                    pl.when((k >= k0) & (k < k0 + nk))(do)

            @pl.when(k == nk_total - 1)
            def _():
                finish([acc[...] for acc in acc_refs])

    grid = (gm, grid_n, nk_total)
    if n_outer:
        assert not parts
        swap = lambda spec: pl.BlockSpec(spec.block_shape, functools.partial(lambda j, i, k, f: f(i, j, k), f=spec.index_map))
        in_specs, out_specs, grid = [swap(sp) for sp in in_specs], [swap(sp) for sp in out_specs], (grid_n, gm, nk_total)
    res = pl.pallas_call(
        body, grid=grid, in_specs=in_specs, out_specs=out_specs, out_shape=out_shape,
        scratch_shapes=scratch, compiler_params=_cp(("arbitrary", "arbitrary", "arbitrary")), name=name,
    )(*args)
    return res


def _mm_tn(name, a, b, tm, tn, tk):
    S, Ka = a.shape
    Nb = b.shape[1]
    nk = S // tk

    def body(a_ref, b_ref, o_ref, acc):
        k = pl.program_id(2)

        @pl.when(k == 0)
        def _():
            acc[...] = jnp.zeros(acc.shape, F32)

        acc[...] += lax.dot_general(a_ref[...], b_ref[...], (((0,), (0,)), ((), ())), preferred_element_type=F32)

        @pl.when(k == nk - 1)
        def _():
            o_ref[...] = acc[...].astype(o_ref.dtype)

    return pl.pallas_call(
        body, grid=(Ka // tm, Nb // tn, nk),
        in_specs=[pl.BlockSpec((tk, tm), lambda i, j, k: (k, i)), pl.BlockSpec((tk, tn), lambda i, j, k: (k, j))],
        out_specs=pl.BlockSpec((tm, tn), lambda i, j, k: (i, j)),
        out_shape=jax.ShapeDtypeStruct((Ka, Nb), BF),
        scratch_shapes=[pltpu.VMEM((tm, tn), F32)],
        compiler_params=_cp(("parallel", "parallel", "arbitrary")), name=name,
    )(a, b)


def _resident_mm(name, M, tm, a_segs, w, rows, consts, outs, parts, epilogue):
    gm = M // tm
    n_a, n_r, n_c, n_o, n_p = len(a_segs), len(rows), len(consts), len(outs), len(parts)
    widths = [a.shape[1] for a in a_segs]
    offs = [sum(widths[:t]) for t in range(n_a)]
    once = pl.Buffered(1)

    def body(*refs):
        pos = 0
        a_refs = refs[pos:pos + n_a]; pos += n_a
        w_ref = refs[pos]; pos += 1
        r_refs = refs[pos:pos + n_r]; pos += n_r
        c_refs = refs[pos:pos + n_c]; pos += n_c
        o_refs = refs[pos:pos + n_o]; pos += n_o
        p_refs = refs[pos:pos + n_p]
        i = pl.program_id(0)
        if parts:
            @pl.when(i == 0)
            def _():
                for p in p_refs:
                    p[...] = jnp.zeros(p.shape, F32)
        acc = None
        for t in range(n_a):
            d = lax.dot_general(a_refs[t][...], w_ref[:, offs[t]:offs[t] + widths[t]], (((1,), (1,)), ((), ())),
                                preferred_element_type=F32)
            acc = d if acc is None else acc + d
        epilogue([acc], r_refs, c_refs, o_refs, p_refs)
        if parts:
            @pl.when(i == gm - 1)
            def _():
                for p in p_refs:
                    p[0:1, :] = jnp.sum(p[...], axis=0, keepdims=True)

    in_specs = [pl.BlockSpec((tm, wd), lambda i: (i, 0)) for wd in widths]
    in_specs.append(pl.BlockSpec(w.shape, lambda i: (0, 0), pipeline_mode=once))
    in_specs += [pl.BlockSpec((tm, c), lambda i: (i, 0)) for _, c in rows]
    in_specs += [pl.BlockSpec(c.shape, lambda i: (0, 0), pipeline_mode=once) for c in consts]
    out_specs = [pl.BlockSpec((tm, nc), lambda i: (i, 0)) for nc, _ in outs] + [pl.BlockSpec(pc, lambda i: (0, 0)) for pc in parts]
    out_shape = [jax.ShapeDtypeStruct((M, nc), dt) for nc, dt in outs] + [jax.ShapeDtypeStruct(pc, F32) for pc in parts]
    return pl.pallas_call(
        body, grid=(gm,), in_specs=in_specs, out_specs=out_specs, out_shape=out_shape,
        compiler_params=_cp(("arbitrary",)), name=name,
    )(*a_segs, w, *[r for r, _ in rows], *consts)


def _in_proj(xs, g1, Wp):
    S = xs.shape[0]
    tm, tn = 1024, IN_W // 4

    def body(x_ref, g_ref, w_ref, h_ref, z_ref, hs):
        @pl.when(pl.program_id(1) == 0)
        def _():
            v = x_ref[...]
            h = (v * _rms_r(v) * g_ref[...]).astype(BF)
            hs[...] = h
            h_ref[...] = h

        z_ref[...] = jnp.dot(hs[...], w_ref[...], preferred_element_type=F32).astype(BF)

    return pl.pallas_call(
        body, grid=(S // tm, IN_W // tn),
        in_specs=[pl.BlockSpec((tm, D), lambda i, j: (i, 0)), pl.BlockSpec((1, D), lambda i, j: (0, 0)),
                  pl.BlockSpec((D, tn), lambda i, j: (0, j))],
        out_specs=[pl.BlockSpec((tm, D), lambda i, j: (i, 0)), pl.BlockSpec((tm, tn), lambda i, j: (i, j))],
        out_shape=[jax.ShapeDtypeStruct((S, D), BF), jax.ShapeDtypeStruct((S, IN_W), BF)],
        scratch_shapes=[pltpu.VMEM((tm, D), BF)],
        compiler_params=_cp(("arbitrary", "arbitrary")), name="in_proj",
    )(xs, g1, Wp)


def _rel_bucket_np(dist):
    max_exact = REL_BUCKETS // 2
    d = np.maximum(dist, 0)
    df = np.maximum(d, 1).astype(np.float32)
    large = max_exact + (np.log(df / np.float32(max_exact)) / np.float32(math.log(REL_MAX_DISTANCE / max_exact))
                         * np.float32(REL_BUCKETS - max_exact)).astype(np.int32)
    large = np.minimum(large, REL_BUCKETS - 1)
    return np.where(d < max_exact, d, large).astype(np.int32)


N_LAYOUTS = 2


def _band_index():
    idx = np.zeros((N_LAYOUTS * 3, 1, QBLK * 2 * QBLK), np.int32)
    for g, (window, dil) in enumerate(GROUPS):
        span = window // dil
        k = np.arange(2 * QBLK)[:, None]; q = np.arange(QBLK)[None, :]
        off = q - k + QBLK
        idx[g, 0] = np.where((off >= 0) & (off <= span), _rel_bucket_np(off * dil), -1).reshape(-1)
        k = np.arange(QBLK)[:, None]; q = np.arange(2 * QBLK)[None, :]
        off = q - k
        idx[3 + g, 0] = np.where((off >= 0) & (off <= span), _rel_bucket_np(off * dil), -1).reshape(-1)
    return idx


_NB = QBLK * 2 * QBLK
_BCH = 4096


def _bias_build(tab_t, idx):
    def body(t_ref, i_ref, o_ref):
        ix = i_ref[0]
        t = t_ref[0]
        acc = jnp.full((HEADS, _BCH), NEG_INF, F32)
        for b in range(REL_BUCKETS):
            acc = jnp.where(ix == b, t[:, b:b + 1], acc)
        o_ref[0] = acc

    return pl.pallas_call(
        body, grid=(N_LAYOUTS * 3, _NB // _BCH),
        in_specs=[pl.BlockSpec((1, HEADS, REL_BUCKETS), lambda l, n: (l % 3, 0, 0)),
                  pl.BlockSpec((1, 1, _BCH), lambda l, n: (l, 0, n))],
        out_specs=pl.BlockSpec((1, HEADS, _BCH), lambda l, n: (l, 0, n)),
        out_shape=jax.ShapeDtypeStruct((N_LAYOUTS * 3, HEADS, _NB), F32), compiler_params=_cp(), name="bias_build",
    )(tab_t, idx)


def _bias_grad(ds, idx):
    nch = _NB // _BCH

    def body(d_ref, i_ref, o_ref):
        n = pl.program_id(1)

        @pl.when(n == 0)
        def _():
            o_ref[...] = jnp.zeros(o_ref.shape, F32)

        ix = i_ref[0]
        d = d_ref[0]
        lane = lax.broadcasted_iota(jnp.int32, (HEADS, 128), 1)
        acc = jnp.zeros((HEADS, 128), F32)
        for b in range(REL_BUCKETS):
            s = jnp.sum(jnp.where(ix == b, d, 0.0), axis=1, keepdims=True)
            acc = acc + jnp.where(lane == b, s, 0.0)
        o_ref[0] += acc

    return pl.pallas_call(
        body, grid=(3, nch),
        in_specs=[pl.BlockSpec((1, HEADS, _BCH), lambda l, n: (l, 0, n)),
                  pl.BlockSpec((1, 1, _BCH), lambda l, n: (3 + l, 0, n))],
        out_specs=pl.BlockSpec((1, HEADS, 128), lambda l, n: (l, 0, 0)),
        out_shape=jax.ShapeDtypeStruct((3, HEADS, 128), F32), compiler_params=_cp(), name="bias_grad",
    )(ds, idx)


PT = 256
PSTEP = 2048
STAT_W = 128


def _perm_np(dil):
    p = np.zeros((PT, PT), np.float32)
    m = np.arange(PT // dil)
    for c in range(dil):
        p[c * (PT // dil) + m, m * dil + c] = 1.0
    return p


def _perm_const(dil, dtype, inverse):
    p = _perm_np(dil)
    return jnp.asarray(p.T if inverse else p, dtype)


def _apply_perm(p, x):
    if x.dtype == F32:
        return jnp.dot(p, x, preferred_element_type=F32, precision=lax.Precision.HIGHEST)
    return jnp.dot(p, x, preferred_element_type=F32)


def _to_residue(name, arr, col_blocks, dil):
    S = arr.shape[0]
    nc = len(col_blocks)
    p = _perm_const(dil, arr.dtype, False)
    sub = PT // dil

    def body(*refs):
        p_ref, ins, o_ref = refs[0], refs[1:1 + nc], refs[1 + nc]
        for u in range(PSTEP // PT):
            for t, r in enumerate(ins):
                y = _apply_perm(p_ref[...], r[u * PT:(u + 1) * PT, :]).astype(o_ref.dtype)
                o_ref[:, u * sub:(u + 1) * sub, t * GW:(t + 1) * GW] = y.reshape(dil, sub, GW)

    out = pl.pallas_call(
        body, grid=(S // PSTEP,),
        in_specs=[pl.BlockSpec((PT, PT), lambda i: (0, 0))]
                 + [pl.BlockSpec((PSTEP, GW), functools.partial(lambda i, cb: (i, cb), cb=cb)) for cb in col_blocks],
        out_specs=pl.BlockSpec((dil, PSTEP // dil, nc * GW), lambda i: (0, i, 0)),
        out_shape=jax.ShapeDtypeStruct((dil, S // dil, nc * GW), arr.dtype), compiler_params=_cp(), name=name,
    )(p, *([arr] * nc))
    return out.reshape(S, nc * GW)


def _to_residue_pair(name, do, stats, dil):
    S = do.shape[0]
    pb = _perm_const(dil, BF, False)
    pf = _perm_const(dil, F32, False)
    sub = PT // dil

    def body(pb_ref, pf_ref, d_ref, s_ref, od_ref, os_ref):
        for u in range(PSTEP // PT):
            rs = slice(u * PT, (u + 1) * PT)
            od_ref[:, u * sub:(u + 1) * sub, :] = _apply_perm(pb_ref[...], d_ref[rs, :]).astype(BF).reshape(dil, sub, GW)
            os_ref[:, u * sub:(u + 1) * sub, :] = _apply_perm(pf_ref[...], s_ref[rs, :]).reshape(dil, sub, STAT_W)

    cst = pl.BlockSpec((PT, PT), lambda i: (0, 0))
    od, os_ = pl.pallas_call(
        body, grid=(S // PSTEP,),
        in_specs=[cst, cst, pl.BlockSpec((PSTEP, GW), lambda i: (i, 0)), pl.BlockSpec((PSTEP, STAT_W), lambda i: (i, 0))],
        out_specs=[pl.BlockSpec((dil, PSTEP // dil, GW), lambda i: (0, i, 0)), pl.BlockSpec((dil, PSTEP // dil, STAT_W), lambda i: (0, i, 0))],
        out_shape=[jax.ShapeDtypeStruct((dil, S // dil, GW), BF), jax.ShapeDtypeStruct((dil, S // dil, STAT_W), F32)],
        compiler_params=_cp(), name=name,
    )(pb, pf, do, stats)
    return od.reshape(S, GW), os_.reshape(S, STAT_W)


def _from_residue(name, arr, dil):
    S, W = arr.shape
    p = _perm_const(dil, arr.dtype, True)
    sub = PT // dil

    def body(p_ref, x_ref, o_ref):
        for u in range(PSTEP // PT):
            x = x_ref[:, u * sub:(u + 1) * sub, :].reshape(PT, W)
            o_ref[u * PT:(u + 1) * PT, :] = _apply_perm(p_ref[...], x).astype(o_ref.dtype)

    return pl.pallas_call(
        body, grid=(S // PSTEP,),
        in_specs=[pl.BlockSpec((PT, PT), lambda i: (0, 0)), pl.BlockSpec((dil, PSTEP // dil, W), lambda i: (0, i, 0))],
        out_specs=pl.BlockSpec((PSTEP, W), lambda i: (i, 0)),
        out_shape=jax.ShapeDtypeStruct((S, W), arr.dtype), compiler_params=_cp(), name=name,
    )(p, arr.reshape(dil, S // dil, W))


PAIR_W = 2 * HEAD_DIM
NT_DIMS = (((1,), (1,)), ((), ()))
TN_DIMS = (((0,), (0,)), ((), ()))


def _attn_dims(S, dil):
    L = S // dil
    TQ = min(512, L)
    return L, TQ, L // TQ, TQ // QBLK


def _attn_specs(S, dil):
    L, TQ, nq, nsub = _attn_dims(S, dil)
    nb = L // QBLK
    cur = lambda cb, w=GW: pl.BlockSpec((TQ, w), lambda c, i: (c * nq + i, cb))
    prev = lambda cb, w=GW: pl.BlockSpec((QBLK, w), lambda c, i: (c * nb + jnp.maximum(i * nsub - 1, 0), cb))
    nxt = lambda cb, w=GW: pl.BlockSpec((QBLK, w), lambda c, i: (c * nb + jnp.minimum((i + 1) * nsub, nb - 1), cb))
    band = lambda r, c_: pl.BlockSpec((HEADS, r, c_), lambda c, i: (0, 0, 0))
    return L, TQ, nq, nsub, cur, prev, nxt, band


def _fill(buf, first_ref, second_ref):
    n = first_ref.shape[0]
    buf[0:n, :] = first_ref[...]
    buf[n:n + second_ref.shape[0], :] = second_ref[...]


def _attn_fwd(name, arr, bias_kq, cb, dil):
    S = arr.shape[0]
    kcb, vcb, qcb = cb
    L, TQ, nq, nsub, cur, prev, nxt, band = _attn_specs(S, dil)

    def body(q_ref, kc_ref, kp_ref, vc_ref, vp_ref, b_ref, o_ref, l_ref, kbuf, vbuf):
        i = pl.program_id(1)
        _fill(kbuf, kp_ref, kc_ref)
        _fill(vbuf, vp_ref, vc_ref)
        row = lax.broadcasted_iota(jnp.int32, (2 * QBLK, QBLK), 0)
        first = (row >= QBLK) | (i > 0)
        low = lax.broadcasted_iota(jnp.int32, (QBLK, PAIR_W), 1) < HEAD_DIM
        zero = jnp.zeros((QBLK, PAIR_W), BF)
        for j in range(nsub):
            rs = slice(j * QBLK, (j + 1) * QBLK)
            ks = slice(j * QBLK, (j + 2) * QBLK)
            lrows = []
            for hp in range(HEADS // 2):
                ps = slice(hp * PAIR_W, (hp + 1) * PAIR_W)
                qp = q_ref[rs, ps]
                kp = kbuf[ks, ps]
                vp = vbuf[ks, ps]
                halves = []
                for t in range(2):
                    qm = jnp.where(low if t == 0 else ~low, qp, zero)
                    s = lax.dot_general(kp, qm, NT_DIMS, preferred_element_type=F32) * SCALE + b_ref[2 * hp + t]
                    if j == 0:
                        s = jnp.where(first, s, NEG_INF)
                    m = jnp.max(s, axis=0, keepdims=True)
                    p = jnp.exp(s - m)
                    den = jnp.sum(p, axis=0, keepdims=True)
                    o2 = lax.dot_general(vp, p.astype(BF), TN_DIMS, preferred_element_type=F32)
                    halves.append(o2[t * HEAD_DIM:(t + 1) * HEAD_DIM, :] / den)
                    lrows.append(m + jnp.log(den))
                o_ref[rs, ps] = jnp.concatenate(halves, axis=0).T.astype(BF)
            lt = jnp.concatenate(lrows + [jnp.zeros((STAT_W - HEADS, QBLK), F32)], axis=0)
            l_ref[rs, :] = lt.T

    return pl.pallas_call(
        body, grid=(dil, nq),
        in_specs=[cur(qcb), cur(kcb), prev(kcb), cur(vcb), prev(vcb), band(2 * QBLK, QBLK)],
        out_specs=[cur(0), cur(0, STAT_W)],
        out_shape=[jax.ShapeDtypeStruct((S, GW), BF), jax.ShapeDtypeStruct((S, STAT_W), F32)],
        scratch_shapes=[pltpu.VMEM((QBLK + TQ, GW), BF), pltpu.VMEM((QBLK + TQ, GW), BF)],
        compiler_params=_cp(), name=name,
    )(arr, arr, arr, arr, arr, bias_kq)


def _attn_bwd(name, arr, bias_kq2, do, stats, cb, dil):
    S = arr.shape[0]
    kcb, vcb, qcb = cb
    L, TQ, nq, nsub, cur, prev, nxt, band = _attn_specs(S, dil)

    def body(k_ref, v_ref, qc_ref, qn_ref, b_ref, doc_ref, don_ref, sc_ref, sn_ref, o_ref, db_ref, qbuf, dobuf, sbuf, carry):
        c = pl.program_id(0)
        i = pl.program_id(1)

        @pl.when((c == 0) & (i == 0))
        def _():
            db_ref[...] = jnp.zeros(db_ref.shape, F32)
            carry[...] = jnp.zeros(carry.shape, F32)

        _fill(qbuf, qc_ref, qn_ref)
        _fill(dobuf, doc_ref, don_ref)
        for j in range(nsub + 1):
            rs = slice(j * QBLK, (j + 1) * QBLK)
            sbuf[:, rs] = (sc_ref[rs, :] if j < nsub else sn_ref[...]).T
        col = lax.broadcasted_iota(jnp.int32, (QBLK, 2 * QBLK), 1)
        last = (col < QBLK) | (i < nq - 1)
        low = lax.broadcasted_iota(jnp.int32, (QBLK, PAIR_W), 1) < HEAD_DIM
        zero = jnp.zeros((QBLK, PAIR_W), BF)
        for hp in range(HEADS // 2):
            ps = slice(hp * PAIR_W, (hp + 1) * PAIR_W)
            dbs = [jnp.zeros((QBLK, 2 * QBLK), F32), jnp.zeros((QBLK, 2 * QBLK), F32)]
            tail = carry[:, ps]
            for j in range(nsub):
                rs = slice(j * QBLK, (j + 1) * QBLK)
                qs = slice(j * QBLK, (j + 2) * QBLK)
                qp = qbuf[qs, ps]
                dd = dobuf[qs, ps]
                kp = k_ref[rs, ps]
                vp = v_ref[rs, ps]
                kt = kp.T
                dk, dv, dqt = [], [], []
                for t in range(2):
                    h = 2 * hp + t
                    sel = low if t == 0 else ~low
                    s = lax.dot_general(jnp.where(sel, kp, zero), qp, NT_DIMS, preferred_element_type=F32) * SCALE + b_ref[h]
                    if j == nsub - 1:
                        s = jnp.where(last, s, NEG_INF)
                    p = jnp.exp(s - sbuf[h:h + 1, qs])
                    dp = lax.dot_general(jnp.where(sel, vp, zero), dd, NT_DIMS, preferred_element_type=F32)
                    ds = p * (dp - sbuf[HEADS + h:HEADS + h + 1, qs])
                    dbs[t] = dbs[t] + ds
                    dsb = ds.astype(BF)
                    dk.append(jnp.dot(dsb, qp, preferred_element_type=F32))
                    dv.append(jnp.dot(p.astype(BF), dd, preferred_element_type=F32))
                    dqt.append(jnp.dot(kt[t * HEAD_DIM:(t + 1) * HEAD_DIM, :], dsb, preferred_element_type=F32))
                o_ref[rs, ps] = (jnp.where(low, dk[0], dk[1]) * SCALE).astype(BF)
                o_ref[rs, GW + hp * PAIR_W:GW + (hp + 1) * PAIR_W] = jnp.where(low, dv[0], dv[1]).astype(BF)
                dq2 = jnp.concatenate(dqt, axis=0).T * SCALE
                o_ref[rs, 2 * GW + hp * PAIR_W:2 * GW + (hp + 1) * PAIR_W] = (dq2[0:QBLK] + tail).astype(BF)
                tail = dq2[QBLK:2 * QBLK]
            carry[:, ps] = tail
            db_ref[2 * hp] += dbs[0]
            db_ref[2 * hp + 1] += dbs[1]

    return pl.pallas_call(
        body, grid=(dil, nq),
        in_specs=[cur(kcb), cur(vcb), cur(qcb), nxt(qcb), band(QBLK, 2 * QBLK),
                  cur(0), nxt(0), cur(0, STAT_W), nxt(0, STAT_W)],
        out_specs=[cur(0, ATTN_W), band(QBLK, 2 * QBLK)],
        out_shape=[jax.ShapeDtypeStruct((S, ATTN_W), BF), jax.ShapeDtypeStruct((HEADS, QBLK, 2 * QBLK), F32)],
        scratch_shapes=[pltpu.VMEM((TQ + QBLK, GW), BF), pltpu.VMEM((TQ + QBLK, GW), BF), pltpu.VMEM((STAT_W, TQ + QBLK), F32),
                        pltpu.VMEM((QBLK, GW), F32)],
        compiler_params=_cp(("arbitrary", "arbitrary")), name=name,
    )(arr, arr, arr, arr, bias_kq2, do, do, stats, stats)


def _head_expand():
    e = np.zeros((STAT_W, GW), np.float32)
    for h in range(HEADS):
        e[h, h * HEAD_DIM:(h + 1) * HEAD_DIM] = 1.0
    return e


def _attn_merge(os_, ls_, S):
    dils = [d for _, d in GROUPS]
    pb = [_perm_const(d, BF, True) for d in dils[1:]]
    pf = [_perm_const(d, F32, True) for d in dils[1:]]
    expand = jnp.asarray(_head_expand(), BF)

    def body(o0, o1, o2, l0, l1, l2, pb1, pb2, pf1, pf2, e_ref, o_ref, l_ref):
        for u in range(PSTEP // PT):
            rs = slice(u * PT, (u + 1) * PT)
            res = lambda r, d: r[:, u * (PT // d):(u + 1) * (PT // d), :].reshape(PT, r.shape[2])
            ov = [o0[rs, :].astype(F32), _apply_perm(pb1[...], res(o1, dils[1])), _apply_perm(pb2[...], res(o2, dils[2]))]
            lv = [l0[rs, :], _apply_perm(pf1[...], res(l1, dils[1])), _apply_perm(pf2[...], res(l2, dils[2]))]
            m = jnp.maximum(jnp.maximum(lv[0], lv[1]), lv[2])
            ev = [jnp.exp(l - m) for l in lv]
            den = ev[0] + ev[1] + ev[2]
            acc = jnp.zeros((PT, GW), F32)
            for g in range(3):
                wide = jnp.dot((ev[g] / den).astype(BF), e_ref[...], preferred_element_type=F32)
                acc = acc + wide * ov[g]
            o_ref[rs, :] = acc.astype(BF)
            l_ref[rs, :] = m + jnp.log(den)

    nat = lambda w: pl.BlockSpec((PSTEP, w), lambda i: (i, 0))
    res = lambda d, w: pl.BlockSpec((d, PSTEP // d, w), lambda i: (0, i, 0))
    cst = lambda a: pl.BlockSpec(a.shape, lambda i: (0, 0))
    args = [os_[0], os_[1].reshape(dils[1], S // dils[1], GW), os_[2].reshape(dils[2], S // dils[2], GW),
            ls_[0], ls_[1].reshape(dils[1], S // dils[1], STAT_W), ls_[2].reshape(dils[2], S // dils[2], STAT_W),
            pb[0], pb[1], pf[0], pf[1], expand]
    return pl.pallas_call(
        body, grid=(S // PSTEP,),
        in_specs=[nat(GW), res(dils[1], GW), res(dils[2], GW), nat(STAT_W), res(dils[1], STAT_W), res(dils[2], STAT_W)]
                 + [cst(a) for a in args[6:]],
        out_specs=[nat(GW), nat(STAT_W)],
        out_shape=[jax.ShapeDtypeStruct((S, GW), BF), jax.ShapeDtypeStruct((S, STAT_W), F32)],
        compiler_params=_cp(), name="attn_merge",
    )(*args)


CT = 512
CBUF = HALO + CT + 8
RG = 4


def _ln_hat(u1):
    mu = jnp.mean(u1, axis=-1, keepdims=True)
    xc = u1 - mu
    rstd = lax.rsqrt(jnp.mean(xc * xc, axis=-1, keepdims=True) + LN_EPS)
    return xc * rstd, rstd


def _glu_window(hu_ref, hg_ref, huh_ref, hgh_ref, bglu_ref, buf_ref, i):
    bu = bglu_ref[:, 0:D]
    bg = bglu_ref[:, D:2 * D]
    uh = (huh_ref[...].astype(F32) + bu) * _sig(hgh_ref[...].astype(F32) + bg)
    buf_ref[0:HALO, :] = jnp.where(i > 0, uh, 0.0)
    a = hu_ref[...].astype(F32) + bu
    s = _sig(hg_ref[...].astype(F32) + bg)
    buf_ref[HALO:HALO + CT, :] = a * s
    buf_ref[HALO + CT:CBUF, :] = jnp.zeros((8, D), F32)
    return a, s


def _shift_copies(buf_ref, sh_ref):
    for r in range(8):
        sh_ref[r] = buf_ref[r:r + HALO + CT, :]


def _tap_rows(wb_ref, w_ref):
    for j in range(CONV_W):
        wb_ref[j * 8:(j + 1) * 8, :] = jnp.broadcast_to(w_ref[j:j + 1, :], (8, D))


def _conv_taps(sh_ref, wb_ref, out_ref, init, offset):
    for rg in range(CT // (8 * RG)):
        accs = [init] * RG
        for j in range(CONV_W):
            off = offset(j)
            wj = wb_ref[j * 8:(j + 1) * 8, :]
            for q in range(RG):
                row = 8 * (rg * RG + q + off // 8)
                accs[q] = accs[q] + wj * sh_ref[off % 8, row:row + 8, :]
        for q in range(RG):
            out_ref[(rg * RG + q) * 8:(rg * RG + q + 1) * 8, :] = accs[q]


def _conv_specs(S):
    cur = lambda cb: pl.BlockSpec((CT, D), lambda i: (i, cb))
    halo = lambda cb: pl.BlockSpec((HALO, D), lambda i: (jnp.maximum(i * (CT // HALO) - 1, 0), cb))
    full = lambda shp: pl.BlockSpec(shp, lambda i: (0, 0))
    return cur, halo, full


def _conv_fwd(z, b_glu, w_dw, b_dw, g_ln, b_ln):
    S = z.shape[0]
    cur, halo, full = _conv_specs(S)

    def body(hu, hg, huh, hgh, bglu, w, bdw, gln, bln, u1_ref, u3_ref, buf, sh, wb):
        i = pl.program_id(0)

        @pl.when(i == 0)
        def _():
            _tap_rows(wb, w)

        _glu_window(hu, hg, huh, hgh, bglu, buf, i)
        _shift_copies(buf, sh)
        _conv_taps(sh, wb, u1_ref, jnp.broadcast_to(bdw[...], (8, D)), lambda j: 2 + j)
        xh, _ = _ln_hat(u1_ref[...])
        u2 = xh * gln[...] + bln[...]
        u3_ref[...] = (u2 * _sig(u2)).astype(BF)

    return pl.pallas_call(
        body, grid=(S // CT,),
        in_specs=[cur(0), cur(1), halo(0), halo(1), full((1, 2 * D)), full((HALO, D)), full((1, D)), full((1, D)), full((1, D))],
        out_specs=[pl.BlockSpec((CT, D), lambda i: (i, 0))] * 2,
        out_shape=[jax.ShapeDtypeStruct((S, D), F32), jax.ShapeDtypeStruct((S, D), BF)],
        scratch_shapes=[pltpu.VMEM((CBUF, D), F32), pltpu.VMEM((8, HALO + CT, D), F32), pltpu.VMEM((CONV_W * 8, D), F32)],
        compiler_params=_cp(("arbitrary",)), name="conv_fwd",
    )(z, z, z, z, b_glu, w_dw, b_dw, g_ln, b_ln)


def _conv_bwd(du1, z, b_glu, w_dw):
    S = z.shape[0]
    n = S // CT
    cur, halo, full = _conv_specs(S)
    RG2 = 2

    def body(du, dun, hu, hg, bglu, w, dz_ref, dw_ref, dbg_ref, bufd, shd, wb, u0_ref, du0_ref, dwacc):
        i = pl.program_id(0)

        @pl.when(i == 0)
        def _():
            _tap_rows(wb, w)
            dwacc[...] = jnp.zeros(dwacc.shape, F32)
            dbg_ref[...] = jnp.zeros(dbg_ref.shape, F32)

        a = hu[...].astype(F32) + bglu[:, 0:D]
        s = _sig(hg[...].astype(F32) + bglu[:, D:2 * D])
        u0_ref[...] = a * s
        bufd[0:CT, :] = du[...]
        bufd[CT:CT + HALO, :] = jnp.where(i < n - 1, dun[...], 0.0)
        bufd[CT + HALO:CBUF, :] = jnp.zeros((8, D), F32)
        _shift_copies(bufd, shd)
        for rg in range(CT // (8 * RG2)):
            uch = [u0_ref[(rg * RG2 + q) * 8:(rg * RG2 + q + 1) * 8, :] for q in range(RG2)]
            accs = [jnp.zeros((8, D), F32)] * RG2
            for j in range(CONV_W):
                off = 30 - j
                wj = wb[j * 8:(j + 1) * 8, :]
                dwj = dwacc[j * 8:(j + 1) * 8, :]
                for q in range(RG2):
                    row = 8 * (rg * RG2 + q + off // 8)
                    x = shd[off % 8, row:row + 8, :]
                    accs[q] = accs[q] + wj * x
                    dwj = dwj + uch[q] * x
                dwacc[j * 8:(j + 1) * 8, :] = dwj
            for q in range(RG2):
                du0_ref[(rg * RG2 + q) * 8:(rg * RG2 + q + 1) * 8, :] = accs[q]
        du0 = du0_ref[...]
        dhu = du0 * s
        dhg = du0 * a * s * (1.0 - s)
        dz_ref[:, 0:D] = dhu.astype(BF)
        dz_ref[:, D:2 * D] = dhg.astype(BF)
        dbg_ref[:, 0:D] += _psum8(dhu)
        dbg_ref[:, D:2 * D] += _psum8(dhg)

        @pl.when(i == n - 1)
        def _():
            dbg_ref[0:1, :] = jnp.sum(dbg_ref[...], axis=0, keepdims=True)
            for j in range(CONV_W):
                dw_ref[j:j + 1, :] = jnp.sum(dwacc[j * 8:(j + 1) * 8, :], axis=0, keepdims=True)
            dw_ref[CONV_W:HALO, :] = jnp.zeros((HALO - CONV_W, D), F32)

    nxt = pl.BlockSpec((HALO, D), lambda i: (jnp.minimum((i + 1) * (CT // HALO), S // HALO - 1), 0))
    return pl.pallas_call(
        body, grid=(n,),
        in_specs=[pl.BlockSpec((CT, D), lambda i: (i, 0)), nxt, cur(0), cur(1), full((1, 2 * D)), full((HALO, D))],
        out_specs=[pl.BlockSpec((CT, 2 * D), lambda i: (i, 0)), full((HALO, D)), full((8, 2 * D))],
        out_shape=[jax.ShapeDtypeStruct((S, 2 * D), BF), jax.ShapeDtypeStruct((HALO, D), F32), jax.ShapeDtypeStruct((8, 2 * D), F32)],
        scratch_shapes=[pltpu.VMEM((CBUF, D), F32), pltpu.VMEM((8, HALO + CT, D), F32), pltpu.VMEM((CONV_W * 8, D), F32),
                        pltpu.VMEM((CT, D), F32), pltpu.VMEM((CT, D), F32), pltpu.VMEM((CONV_W * 8, D), F32)],
        compiler_params=_cp(("arbitrary",)), name="conv_bwd",
    )(du1, du1, z, z, b_glu, w_dw)


MESH = pl.DeviceIdType.MESH


def _all_gather(name, shards):
    n = len(shards)

    def body(*refs):
        ins, outs = refs[:n], refs[n:2 * n]
        send_sems, recv_sems, local_sems = refs[2 * n:]
        x, y, c = lax.axis_index("x"), lax.axis_index("y"), lax.axis_index("c")
        me, sibling = (x, y, c), (x, y, 1 - c)
        chips = [(1 - x, y), (x, 1 - y), (1 - x, 1 - y)]

        def slot(a, px, py, pc):
            return outs[a].at[4 * px + 2 * py + pc]

        def copy(a, k, block, to, src=None):
            return pltpu.make_async_remote_copy(
                src_ref=slot(a, *block) if src is None else src, dst_ref=slot(a, *block),
                send_sem=send_sems.at[a, k], recv_sem=recv_sems.at[a, k], device_id=to, device_id_type=MESH)

        mine = [pltpu.make_async_copy(ins[a], slot(a, *me), local_sems.at[a]) for a in range(n)]
        for cp in mine:
            cp.start()
        first = []
        for a in range(n):
            first.append(copy(a, 0, me, sibling, src=ins[a]))
            first += [copy(a, 1 + j, me, (*chip, c), src=ins[a]) for j, chip in enumerate(chips)]
        for cp in first:
            cp.start()
        passed = []
        for j, chip in enumerate(chips):
            for a in range(n):
                copy(a, 1 + j, (*chip, c), me).wait_recv()
                fwd = copy(a, 4 + j, (*chip, c), sibling)
                fwd.start()
                passed.append(fwd)
        for a in range(n):
            copy(a, 0, sibling, me).wait_recv()
        for j, chip in enumerate(chips):
            for a in range(n):
                copy(a, 4 + j, (*chip, 1 - c), me).wait_recv()
        for cp in first + passed:
            cp.wait_send()
        for cp in mine:
            cp.wait()

    anyspec = pl.BlockSpec(memory_space=pl.ANY)
    return pl.pallas_call(
        body, in_specs=[anyspec] * n, out_specs=[anyspec] * n,
        out_shape=[jax.ShapeDtypeStruct((NDEV,) + s.shape, s.dtype) for s in shards],
        scratch_shapes=[pltpu.SemaphoreType.DMA((n, 7)), pltpu.SemaphoreType.DMA((n, 7)), pltpu.SemaphoreType.DMA((n,))],
        name=name,
    )(*shards)


HBM_SPEC = pl.BlockSpec(memory_space=pltpu.HBM)
SEM_SPEC = pl.BlockSpec(memory_space=pltpu.SEMAPHORE)
DATAFLOW = pltpu.SideEffectType.DATAFLOW_SIDE_EFFECTING


def _peers():
    x, y, c = lax.axis_index("x"), lax.axis_index("y"), lax.axis_index("c")
    out = []
    for k in range(1, NDEV):
        px = 1 - x if k & 4 else x
        py = 1 - y if k & 2 else y
        pc = 1 - c if k & 1 else c
        out.append(((px, py, pc), 4 * px + 2 * py + pc))
    return 4 * x + 2 * y + c, out


def _exchange_copies(srcs, lands, send_sems, recv_sems, gather):
    my, peers = _peers()
    pairs = []
    for k, (dev, pid) in enumerate(peers):
        for a in range(len(srcs)):
            src = srcs[a] if gather else srcs[a].at[pid]
            sems = dict(send_sem=send_sems[a * (NDEV - 1) + k], recv_sem=recv_sems[a * (NDEV - 1) + k], device_id=dev,
                        device_id_type=MESH)
            pairs.append((pltpu.make_async_remote_copy(src_ref=src, dst_ref=lands[a].at[my], **sems),
                          pltpu.make_async_remote_copy(src_ref=src, dst_ref=lands[a].at[pid], **sems)))
    return pairs


def _exchange_start(name, srcs, gather):
    n = len(srcs)
    ns = n * (NDEV - 1)
    shapes = [(s.shape if gather else s.shape[1:]) for s in srcs]
    lands = [lax.empty((NDEV,) + shp, s.dtype) for shp, s in zip(shapes, srcs)]

    def body(*refs):
        src_refs, land_refs = refs[:n], refs[n:2 * n]
        send_sems, recv_sems = refs[2 * n:2 * n + ns], refs[2 * n + ns:2 * n + 2 * ns]
        token = refs[-1]
        for mine, _ in _exchange_copies(src_refs, land_refs, send_sems, recv_sems, gather):
            mine.start()
        token[...] = jnp.zeros(token.shape, token.dtype)

    hbm = lambda a: pltpu.HBM(a.shape, a.dtype)
    res = pl.pallas_call(
        body, name=name,
        out_shape=(*([pltpu.SemaphoreType.DMA(())] * (2 * ns)), *[hbm(s) for s in srcs], *[hbm(l) for l in lands],
                   jax.ShapeDtypeStruct((8, 128), F32)),
        in_specs=[HBM_SPEC] * (2 * n),
        out_specs=(*([SEM_SPEC] * (2 * ns)), *([HBM_SPEC] * (2 * n)), pl.BlockSpec(memory_space=pltpu.VMEM)),
        input_output_aliases={i: 2 * ns + i for i in range(2 * n)},
        compiler_params=pltpu.CompilerParams(has_side_effects=DATAFLOW),
    )(*[pltpu.with_memory_space_constraint(s, pltpu.HBM) for s in srcs],
      *[pltpu.with_memory_space_constraint(l, pltpu.HBM) for l in lands])
    return list(res[:ns]), list(res[ns:2 * ns]), list(res[2 * ns:2 * ns + n]), list(res[2 * ns + n:2 * ns + 2 * n]), res[-1]


def _exchange_wait(name, handle, after, gather):
    send_sems, recv_sems, srcs, lands, _ = handle
    n = len(srcs)
    ns = n * (NDEV - 1)

    def body(*refs):
        src_refs, land_refs = refs[:n], refs[n:2 * n]
        s_sems, r_sems = refs[2 * n:2 * n + ns], refs[2 * n + ns:2 * n + 2 * ns]
        for mine, theirs in _exchange_copies(src_refs, land_refs, s_sems, r_sems, gather):
            mine.wait_send()
            theirs.wait_recv()

    hbm = lambda a: pltpu.HBM(a.shape, a.dtype)
    res = pl.pallas_call(
        body, name=name,
        out_shape=(*[hbm(s) for s in srcs], *[hbm(l) for l in lands]),
        in_specs=[HBM_SPEC] * (2 * n) + [SEM_SPEC] * (2 * ns) + [pl.BlockSpec(memory_space=pl.ANY)],
        out_specs=tuple([HBM_SPEC] * (2 * n)),
        input_output_aliases={i: i for i in range(2 * n)},
        compiler_params=pltpu.CompilerParams(has_side_effects=DATAFLOW),
    )(*srcs, *lands, *send_sems, *recv_sems, after)
    return list(res[n:])


def _set_own_slot(land, own):
    my = 4 * lax.axis_index("x") + 2 * lax.axis_index("y") + lax.axis_index("c")
    return lax.dynamic_update_slice(land, own[None], (my, 0, 0))


def _own_block(blocks):
    my = 4 * lax.axis_index("x") + 2 * lax.axis_index("y") + lax.axis_index("c")
    return lax.dynamic_index_in_dim(blocks, my, axis=0, keepdims=False)


_C1 = 1.0 - ADAM_B1 ** ADAM_STEP
_C2 = 1.0 - ADAM_B2 ** ADAM_STEP


def _adamw(name, w, m, v, recv, tr):
    R, C = w.shape

    def body(w_ref, m_ref, v_ref, r_ref, g_ref, d_ref, nm_ref, nv_ref):
        g = r_ref[0].astype(F32)
        for s in range(1, NDEV):
            g = g + r_ref[s].astype(F32)
        wv = w_ref[...]
        nm = ADAM_B1 * m_ref[...] + (1.0 - ADAM_B1) * g
        nv = ADAM_B2 * v_ref[...] + (1.0 - ADAM_B2) * (g * g)
        m_hat = nm / _C1
        v_hat = nv / _C2
        g_ref[...] = g
        d_ref[...] = -ADAM_LR * (m_hat / (jnp.sqrt(v_hat) + ADAM_EPS) + ADAM_WD * wv)
        nm_ref[...] = nm
        nv_ref[...] = nv

    blk = pl.BlockSpec((tr, C), lambda i: (i, 0))
    return pl.pallas_call(
        body, grid=(R // tr,), in_specs=[blk, blk, blk, pl.BlockSpec((NDEV, tr, C), lambda i: (0, i, 0))],
        out_specs=[blk] * 4, out_shape=[jax.ShapeDtypeStruct((R, C), F32)] * 4,
        compiler_params=_cp(), name=name,
    )(w, m, v, recv)


def _row(v):
    return v.reshape(1, -1)


def _local_step(xs, tgt, Wp, rest_fn, early_fn, late_fn, rel_bias_table, g_pre_mix, b_glu, b_dw, g_conv_ln,
                b_conv_ln, b_conv_out, g_post_mix, g_pre_ffn, g_post_ffn):
    S = xs.shape[0]
    g1, g2, g3, g4 = _row(g_pre_mix), _row(g_post_mix), _row(g_pre_ffn), _row(g_post_ffn)
    bglu, bdw, gln, bln, bco = _row(b_glu), _row(b_dw), _row(g_conv_ln), _row(b_conv_ln), _row(b_conv_out)
    full = (D, F32, D, 0, False)
    fullb = (D, BF, D, 0, False)

    h1, z = _in_proj(xs, g1, Wp)

    idx = jnp.asarray(_band_index())
    tab_t = rel_bias_table.T.reshape(3, HEADS, REL_BUCKETS)
    bias_all = _bias_build(tab_t, idx)
    bias_kq = [bias_all[g].reshape(HEADS, 2 * QBLK, QBLK) for g in range(3)]
    bias_kq2 = [bias_all[3 + g].reshape(HEADS, QBLK, 2 * QBLK) for g in range(3)]
    dils = [d for _, d in GROUPS]
    qkv = [(z, _kvq_blocks(0))] + [(_to_residue(f"qkv_to_residue_g{g}", z, _kvq_blocks(g), dils[g]), (0, 1, 2)) for g in (1, 2)]
    os_, ls_ = [], []
    for g in range(3):
        o_g, l_g = _attn_fwd(f"attn_fwd_g{g}", qkv[g][0], bias_kq[g], qkv[g][1], dils[g])
        os_.append(o_g)
        ls_.append(l_g)
    o_att, lse = _attn_merge(os_, ls_, S)

    Wfi, Wfo, Wco, Wmo, Wao, wdw = rest_fn(lse)
    u1, u3 = _conv_fwd(z, bglu, wdw, bdw, gln, bln)

    def epi_mix(accs, r, c, o, p):
        ya = accs[0]
        yc = accs[1] + c[0][...]
        mg = _sig(r[0][...].astype(F32)) * ya + _sig(r[1][...].astype(F32)) * yc
        mgb = mg.astype(BF)
        m2 = jnp.dot(mgb, c[1][...], preferred_element_type=F32)
        x1 = r[2][...] + m2 * _rms_r(m2) * c[2][...]
        o[0][...] = ya.astype(BF)
        o[1][...] = yc.astype(BF)
        o[2][...] = mgb
        o[3][...] = m2.astype(BF)
        o[4][...] = x1
        o[5][...] = (x1 * _rms_r(x1) * c[3][...]).astype(BF)

    y_attn, y_conv, merged, m2, x1, h2 = _fused_mm(
        "mix_fwd", S, 512, 1, [(o_att, GW, 0, 1), (u3, D, 0, 1)],
        [(Wao, False, GW, D, 0, 1, 0, 0, False), (Wco, False, D, D, 0, 1, 0, 0, False)], [(0, 0, 0, 0, 1), (1, 1, 1, 0, 1)],
        [(512, D), (512, D)], [(z, D, 2, False), (z, D, 3, False), (xs, D, 0, False)], [bco, Wmo, g2, g3],
        [fullb, fullb, fullb, fullb, full, fullb], [], epi_mix)

    HN = FFN // 2

    def epi_ffn_in(accs, r, c, o, p):
        gt, up = accs
        o[0][...] = gt.astype(BF)
        o[1][...] = up.astype(BF)
        o[2][...] = (gt * _sig(gt) * up).astype(BF)

    gate, up, act = _fused_mm(
        "ffn_in", S, 512, 2, [(h2, D, 0, 1)],
        [(Wfi, False, D, HN, 0, 1, 0, 0, True), (Wfi, False, D, HN, 0, 1, 0, 2, True)], [(0, 0, 0, 0, 1), (0, 1, 1, 0, 1)],
        [(512, HN), (512, HN)], [], [], [(FFN, BF, HN, 0, True)] * 3, [], epi_ffn_in, n_outer=True)

    def epi_loss(accs, r, c, o, p):
        f2 = accs[0]
        g = c[0][...]
        rr = _rms_r(f2)
        err = r[0][...] + f2 * rr * g - r[1][...]
        dy = err * (1.0 / D)
        df2, dgr = _rms_bwd(f2, rr, g, dy)
        o[0][...] = dy
        o[1][...] = df2.astype(BF)
        p[0][...] += _psum8(err * err)
        p[1][...] += _psum8(dgr)

    dy, df2, loss_p, dg4 = _fused_mm(
        "ffn_out_loss", S, 512, 1, [(act, FFN, 0, 1)], [(Wfo, False, FFN, D, 0, 1, 0, 0, False)], [(0, 0, 0, 0, 1)],
        [(512, D)], [(x1, D, 0, False), (tgt, D, 0, False)], [g4], [full, fullb], [(8, D), (8, D)], epi_loss)

    def epi_swiglu(accs, r, c, o, p):
        da = accs[0]
        gt = r[0][...].astype(F32)
        sg = _sig(gt)
        o[0][...] = (da * r[1][...].astype(F32) * sg * (1.0 + gt * (1.0 - sg))).astype(BF)
        o[1][...] = (da * gt * sg).astype(BF)

    dgate, dup = _fused_mm(
        "ffn_out_bwd", S, 512, 2, [(df2, D, 0, 1)], [(Wfo, True, D, HN, 0, 1, 0, 0, True)], [(0, 0, 0, 0, 1)],
        [(512, HN)], [(gate, HN, 0, True), (up, HN, 0, True)], [], [(FFN, BF, HN, 0, True)] * 2, [], epi_swiglu, n_outer=True)
    dWfo = _mm_tn("dw_ffn_out", act, df2, HN, D, WG_TK)

    def epi_dh2(accs, r, c, o, p):
        dh2 = accs[0]
        x1v = r[1][...]
        r3 = _rms_r(x1v)
        d1, dg3r = _rms_bwd(x1v, r3, c[0][...], dh2)
        dx1 = r[0][...] + d1
        m2v = r[2][...].astype(F32)
        r2 = _rms_r(m2v)
        dm2, dg2r = _rms_bwd(m2v, r2, c[1][...], dx1)
        o[0][...] = dx1
        o[1][...] = dm2.astype(BF)
        p[0][...] += _psum8(dg3r)
        p[1][...] += _psum8(dg2r)

    dx1, dm2, dg3, dg2 = _resident_mm(
        "ffn_in_bwd", S, 512, [dgate, dup], Wfi, [(dy, D), (x1, D), (m2, D)], [g3, g2], [(D, F32), (D, BF)], [(8, D), (8, D)], epi_dh2)
    dWfi = jnp.concatenate([_mm_tn("dw_ffn_gate", h2, dgate, D, HN, WG_TK), _mm_tn("dw_ffn_up", h2, dup, D, HN, WG_TK)], axis=1)

    def epi_dmix(accs, r, c, o, p):
        dm = accs[0]
        sa = _sig(r[0][...].astype(F32))
        sc = _sig(r[1][...].astype(F32))
        o[0][...] = (dm * sa).astype(BF)
        o[1][...] = (dm * sc).astype(BF)
        o[2][:, 0:D] = (dm * r[2][...].astype(F32) * sa * (1.0 - sa)).astype(BF)
        o[2][:, D:2 * D] = (dm * r[3][...].astype(F32) * sc * (1.0 - sc)).astype(BF)

    dy_attn, dy_conv, dz_gate = _fused_mm(
        "mix_bwd", S, 512, 1, [(dm2, D, 0, 1)], [(Wmo, True, D, D, 0, 1, 0, 0, False)], [(0, 0, 0, 0, 1)], [(512, D)],
        [(z, D, 2, False), (z, D, 3, False), (y_attn, D, 0, False), (y_conv, D, 0, False)], [],
        [fullb, fullb, (2 * D, BF, 2 * D, 0, False)], [], epi_dmix)
    dWmo = _mm_tn("dw_mix_out", merged, dm2, D, D, WG_TK)

    def epi_dconv(accs, r, c, o, p):
        du3 = accs[0]
        xh, rstd = _ln_hat(r[0][...])
        gl = c[0][...]
        u2 = xh * gl + c[1][...]
        sg = _sig(u2)
        du2 = du3 * sg * (1.0 + u2 * (1.0 - sg))
        dxh = du2 * gl
        du1 = rstd * (dxh - jnp.mean(dxh, axis=-1, keepdims=True) - xh * jnp.mean(dxh * xh, axis=-1, keepdims=True))
        o[0][...] = du1
        p[0][...] += _psum8(du2 * xh)
        p[1][...] += _psum8(du2)
        p[2][...] += _psum8(du1)
        p[3][...] += _psum8(r[1][...].astype(F32))

    du1, dgln, dbln, dbdw, dbco = _fused_mm(
        "conv_out_bwd", S, 512, 1, [(dy_conv, D, 0, 1)], [(Wco, True, D, D, 0, 1, 0, 0, False)], [(0, 0, 0, 0, 1)], [(512, D)],
        [(u1, D, 0, False), (dy_conv, D, 0, False)], [gln, bln], [full], [(8, D)] * 4, epi_dconv)
    dWco = _mm_tn("dw_conv_out", u3, dy_conv, D, D, WG_TK)
    dz_glu, dwdw, dbglu = _conv_bwd(du1, z, bglu, wdw)

    head_sum = np.zeros((GW, STAT_W), np.float32)
    for h in range(HEADS):
        head_sum[h * HEAD_DIM:(h + 1) * HEAD_DIM, HEADS + h] = 1.0
    head_sum = jnp.asarray(head_sum)

    def epi_do(accs, r, c, o, p):
        do = accs[0]
        o[0][...] = do.astype(BF)
        delta = jnp.dot(do * r[0][...].astype(F32), c[0][...], preferred_element_type=F32, precision=lax.Precision.HIGHEST)
        lane = lax.broadcasted_iota(jnp.int32, delta.shape, 1)
        o[1][...] = jnp.where(lane < HEADS, r[1][...], delta)

    do, stats = _fused_mm(
        "attn_out_bwd", S, 1024, 1, [(dy_attn, D, 0, 1)], [(Wao, True, D, GW, 0, 1, 0, 0, False)], [(0, 0, 0, 0, 1)], [(1024, GW)],
        [(o_att, GW, 0, False), (lse, STAT_W, 0, False)], [head_sum], [(GW, BF, GW, 0, False), (STAT_W, F32, STAT_W, 0, False)], [], epi_do)
    dWao = _mm_tn("dw_attn_out", o_att, dy_attn, GW, D, WG_TK)

    tie = early_fn(dict(w_ffn_in=dWfi, w_ffn_out=dWfo, w_conv_out=dWco, w_mix_out=dWmo, w_attn_out=dWao, w_dw=dwdw))
    stats = stats + tie
    moved = [_to_residue_pair(f"do_stats_to_residue_g{g}", do, stats, dils[g]) for g in (1, 2)]
    dos = [do] + [m_[0] for m_ in moved]
    sts = [stats] + [m_[1] for m_ in moved]
    dqkv, dbs = [], []
    for g in range(3):
        arr, cb = qkv[g]
        dg, db = _attn_bwd(f"attn_bwd_g{g}", arr, bias_kq2[g], dos[g], sts[g], cb, dils[g])
        dqkv.append(dg if g == 0 else _from_residue(f"dqkv_from_residue_g{g}", dg, dils[g]))
        dbs.append(db.reshape(HEADS, _NB))
    dtab = _bias_grad(jnp.stack(dbs), idx)[:, :, :REL_BUCKETS].reshape(3 * HEADS, REL_BUCKETS).T

    def epi_dx(accs, r, c, o, p):
        xv = r[1][...]
        d1, dg1r = _rms_bwd(xv, _rms_r(xv), c[0][...], accs[0])
        o[0][...] = r[0][...] + d1
        p[0][...] += _psum8(dg1r)

    dWg = [_mm_tn(f"dw_in_g{g}", h1, dqkv[g], D, ATTN_W, WG_TK) for g in range(3)]
    dW_in = jnp.concatenate(
        [t[:, 2 * GW:] for t in dWg] + [t[:, :GW] for t in dWg] + [t[:, GW:2 * GW] for t in dWg]
        + [_mm_tn("dw_in_glu", h1, dz_glu, D, D, WG_TK), _mm_tn("dw_in_gate", h1, dz_gate, D, D, WG_TK)], axis=1)
    g1_late = g1 + late_fn(dW_in)
    grad_x, dg1 = _resident_mm(
        "in_proj_bwd", S, 512, [dz_glu, dz_gate, dqkv[0], dqkv[1], dqkv[2]], Wp, [(dx1, D), (xs, D)], [g1_late], [(D, F32)], [(8, D)], epi_dx)

    small = dict(rel_bias_table=dtab, g_pre_mix=dg1[0], b_glu=dbglu[0], b_dw=dbdw[0], g_conv_ln=dgln[0], b_conv_ln=dbln[0],
                 b_conv_out=dbco[0], g_post_mix=dg2[0], g_pre_ffn=dg3[0], g_post_ffn=dg4[0])
    return loss_p[0], grad_x, small


SMALL = ['rel_bias_table', 'g_pre_mix', 'b_glu', 'b_dw', 'g_conv_ln', 'b_conv_ln', 'b_conv_out', 'g_post_mix', 'g_pre_ffn',
         'g_post_ffn']
BIG = ['w_in', 'w_ffn_in', 'w_ffn_out', 'w_conv_out', 'w_mix_out', 'w_attn_out', 'w_dw']
WEIGHTS = ['rel_bias_table', 'g_pre_mix', 'w_in', 'b_glu', 'w_dw', 'b_dw', 'g_conv_ln', 'b_conv_ln', 'w_conv_out', 'b_conv_out',
           'w_attn_out', 'w_mix_out', 'g_post_mix', 'g_pre_ffn', 'w_ffn_in', 'w_ffn_out', 'g_post_ffn']
SMALL_ROWS = 16


ROW_SMALL = ['g_pre_mix', 'b_glu', 'b_dw', 'g_conv_ln', 'b_conv_ln', 'b_conv_out', 'g_post_mix', 'g_pre_ffn', 'g_post_ffn']
LOSS_ROW = 10
TAB_LANES = 128


def _small_rows(small, loss_row):
    rows = [small[n].reshape(-1, D) for n in ROW_SMALL] + [loss_row.reshape(1, D)]
    n = sum(r.shape[0] for r in rows)
    return jnp.concatenate(rows + [jnp.zeros((SMALL_ROWS - n, D), F32)], axis=0)


def _adamw_small(recv_rows, recv_tab, ws, ms, vs):
    np_ = len(SMALL)

    def body(*refs):
        rr, rt = refs[0], refs[1]
        w_refs, m_refs, v_refs = refs[2:2 + np_], refs[2 + np_:2 + 2 * np_], refs[2 + 2 * np_:2 + 3 * np_]
        loss_ref = refs[2 + 3 * np_]
        outs = refs[3 + 3 * np_:]
        rows = rr[0]
        tab = rt[0]
        for s_ in range(1, NDEV):
            rows = rows + rr[s_]
            tab = tab + rt[s_]
        loss_ref[...] = jnp.sum(rows[LOSS_ROW:LOSS_ROW + 1, :], axis=1, keepdims=True) * (0.5 / D)
        row = 0
        for p, n in enumerate(SMALL):
            if n == 'rel_bias_table':
                g = tab[:, 0:3 * HEADS]
            else:
                k = w_refs[p].shape[1] // D
                g = rows[row:row + 1, :] if k == 1 else jnp.concatenate([rows[row + t:row + t + 1, :] for t in range(k)], axis=1)
                row += k
            nm = ADAM_B1 * m_refs[p][...] + (1.0 - ADAM_B1) * g
            nv = ADAM_B2 * v_refs[p][...] + (1.0 - ADAM_B2) * (g * g)
            outs[4 * p][...] = g
            outs[4 * p + 1][...] = -ADAM_LR * ((nm / _C1) / (jnp.sqrt(nv / _C2) + ADAM_EPS) + ADAM_WD * w_refs[p][...])
            outs[4 * p + 2][...] = nm
            outs[4 * p + 3][...] = nv

    out_shape = [jax.ShapeDtypeStruct((1, 1), F32)]
    for a_ in ws:
        out_shape += [jax.ShapeDtypeStruct(a_.shape, F32)] * 4
    res = pl.pallas_call(body, out_shape=out_shape, compiler_params=_cp(), name="adamw_small")(recv_rows, recv_tab, *ws, *ms, *vs)
    return res[0], [tuple(res[1 + 4 * p:5 + 4 * p]) for p in range(np_)]


def _cols_to_blocks(a):
    R = a.shape[0]
    return a.reshape(R, NDEV, a.shape[1] // NDEV).transpose(1, 0, 2)


def _blocks_to_cols(a):
    return a.transpose(1, 0, 2).reshape(a.shape[1], NDEV * a.shape[2])


def kernel(x, rel_bias_table, g_pre_mix, w_in, b_glu, w_dw, b_dw, g_conv_ln, b_conv_ln, w_conv_out, b_conv_out, w_attn_out, w_mix_out, g_post_mix, g_pre_ffn, w_ffn_in, w_ffn_out, g_post_ffn, loss_target, m_rel_bias_table, m_g_pre_mix, m_w_in, m_b_glu, m_w_dw, m_b_dw, m_g_conv_ln, m_b_conv_ln, m_w_conv_out, m_b_conv_out, m_w_attn_out, m_w_mix_out, m_g_post_mix, m_g_pre_ffn, m_w_ffn_in, m_w_ffn_out, m_g_post_ffn, v_rel_bias_table, v_g_pre_mix, v_w_in, v_b_glu, v_w_dw, v_b_dw, v_g_conv_ln, v_b_conv_ln, v_w_conv_out, v_b_conv_out, v_w_attn_out, v_w_mix_out, v_g_post_mix, v_g_pre_ffn, v_w_ffn_in, v_w_ffn_out, v_g_post_ffn):
    w = dict(rel_bias_table=rel_bias_table, g_pre_mix=g_pre_mix, w_in=w_in, b_glu=b_glu, w_dw=w_dw, b_dw=b_dw, g_conv_ln=g_conv_ln, b_conv_ln=b_conv_ln, w_conv_out=w_conv_out, b_conv_out=b_conv_out, w_attn_out=w_attn_out, w_mix_out=w_mix_out, g_post_mix=g_post_mix, g_pre_ffn=g_pre_ffn, w_ffn_in=w_ffn_in, w_ffn_out=w_ffn_out, g_post_ffn=g_post_ffn)
    m = dict(rel_bias_table=m_rel_bias_table, g_pre_mix=m_g_pre_mix, w_in=m_w_in, b_glu=m_b_glu, w_dw=m_w_dw, b_dw=m_b_dw, g_conv_ln=m_g_conv_ln, b_conv_ln=m_b_conv_ln, w_conv_out=m_w_conv_out, b_conv_out=m_b_conv_out, w_attn_out=m_w_attn_out, w_mix_out=m_w_mix_out, g_post_mix=m_g_post_mix, g_pre_ffn=m_g_pre_ffn, w_ffn_in=m_w_ffn_in, w_ffn_out=m_w_ffn_out, g_post_ffn=m_g_post_ffn)
    v = dict(rel_bias_table=v_rel_bias_table, g_pre_mix=v_g_pre_mix, w_in=v_w_in, b_glu=v_b_glu, w_dw=v_w_dw, b_dw=v_b_dw, g_conv_ln=v_g_conv_ln, b_conv_ln=v_b_conv_ln, w_conv_out=v_w_conv_out, b_conv_out=v_b_conv_out, w_attn_out=v_w_attn_out, w_mix_out=v_w_mix_out, g_post_mix=v_g_post_mix, g_pre_ffn=v_g_pre_ffn, w_ffn_in=v_w_ffn_in, w_ffn_out=v_w_ffn_out, g_post_ffn=v_g_post_ffn)

    def shard2d(d, n):
        a = d[n][0]
        return jnp.pad(a, ((0, HALO - CONV_W), (0, 0))) if n == 'w_dw' else a

    own = {n: shard2d(w, n).astype(F32 if n == 'w_dw' else BF) for n in BIG}
    packed = ['w_ffn_out', 'w_conv_out', 'w_mix_out', 'w_attn_out']
    alone = ['w_ffn_in', 'w_dw']
    shapes = [own[n].shape for n in packed]

    def pack(arrs, lead):
        return jnp.concatenate([a.reshape(lead + (-1, D)) for a in arrs], axis=len(lead))

    def unpack(p):
        out, pos = {}, 0
        for n, shp in zip(packed, shapes):
            rows = shp[0] * shp[1] // D
            out[n] = p[:, pos:pos + rows].reshape((NDEV,) + shp)
            pos += rows
        return out

    (g_in,) = _all_gather("gather_w_in", [own['w_in']])
    rest_own = [pack([own[n] for n in packed], ())] + [own[n] for n in alone]
    g_in, rest_own = lax.optimization_barrier((g_in, rest_own))
    gather_rest = _exchange_start("gather_rest_start", rest_own, True)
    W_in = _blocks_to_cols(g_in)
    kvq = [W_in[:, t * ATTN_W + g * GW:t * ATTN_W + (g + 1) * GW] for g in range(3) for t in (1, 2, 0)]
    Wp = jnp.concatenate([W_in[:, 3 * ATTN_W:]] + kvq, axis=1)

    def rest_fn(after):
        lands = _exchange_wait("gather_rest_wait", gather_rest, after, True)
        gw = unpack(_set_own_slot(lands[0], rest_own[0]))
        for n, l, o in zip(alone, lands[1:], rest_own[1:]):
            gw[n] = _set_own_slot(l, o)
        return (_blocks_to_cols(gw['w_ffn_in']), gw['w_ffn_out'].reshape(FFN, D), gw['w_conv_out'].reshape(D, D),
                gw['w_mix_out'].reshape(D, D), _blocks_to_cols(gw['w_attn_out']), _blocks_to_cols(gw['w_dw']))

    def to_blocks(n, g):
        if n in ('w_in', 'w_ffn_in', 'w_attn_out', 'w_dw'):
            return _cols_to_blocks(g)
        return g.reshape(NDEV, g.shape[0] // NDEV, g.shape[1])

    started = {}

    def early_fn(grads):
        blocks = [pack([to_blocks(n, grads[n]) for n in packed], (NDEV,))] + [to_blocks(n, grads[n]) for n in alone]
        started['blocks'] = blocks
        started['handle'] = _exchange_start("scatter_early_start", blocks, False)
        return started['handle'][4][0:1, 0:1]

    def late_fn(dW_in):
        started['in_blocks'] = [to_blocks('w_in', dW_in)]
        started['in_handle'] = _exchange_start("scatter_w_in_start", started['in_blocks'], False)
        return started['in_handle'][4][0:1, 0:1]

    g1_tied = g_pre_mix[0] + gather_rest[4][0, 0:1]
    loss_row, grad_x, small = _local_step(
        x[0], loss_target[0], Wp, rest_fn, early_fn, late_fn, rel_bias_table, g1_tied, b_glu[0], b_dw[0], g_conv_ln[0],
        b_conv_ln[0], b_conv_out[0], g_post_mix[0], g_pre_ffn[0], g_post_ffn[0])

    lands = _exchange_wait("scatter_early_wait", started['handle'], grad_x, False)
    lands = [_set_own_slot(l, _own_block(b)) for l, b in zip(lands, started['blocks'])]
    recv = unpack(lands[0])
    recv.update(zip(alone, lands[1:]))
    (land_in,) = _exchange_wait("scatter_w_in_wait", started['in_handle'], grad_x, False)
    recv['w_in'] = _set_own_slot(land_in, _own_block(started['in_blocks'][0]))
    tiles = dict(w_in=128, w_ffn_in=256, w_ffn_out=176, w_conv_out=128, w_mix_out=128, w_attn_out=512, w_dw=HALO)
    res = {}
    for n in BIG:
        g_, d_, nm_, nv_ = _adamw("adamw_" + n, shard2d(w, n), shard2d(m, n), shard2d(v, n), recv[n], tiles[n])
        if n == 'w_dw':
            g_, d_, nm_, nv_ = (t[:CONV_W] for t in (g_, d_, nm_, nv_))
        res[n] = tuple(t[None] for t in (g_, d_, nm_, nv_))

    tab = jnp.pad(small['rel_bias_table'], ((0, 0), (0, TAB_LANES - 3 * HEADS)))
    srows, stab = _all_gather("gather_small_grads", [_small_rows(small, loss_row), tab])
    loss11, small_res = _adamw_small(srows, stab, [w[n] for n in SMALL], [m[n] for n in SMALL], [v[n] for n in SMALL])
    loss = loss11.reshape(())
    for n, r in zip(SMALL, small_res):
        res[n] = r
    return (loss, grad_x[None], *[res[n][0] for n in WEIGHTS], *[res[n][1] for n in WEIGHTS],
            *[res[n][2] for n in WEIGHTS], *[res[n][3] for n in WEIGHTS])
```

```python
import functools
import math

import numpy as np
import jax
import jax.numpy as jnp
from jax import lax
from jax.experimental import pallas as pl
from jax.experimental.pallas import tpu as pltpu

F32 = jnp.float32
BF = jnp.bfloat16

D = 1024
HEAD_DIM = 64
HEADS = 8
GROUPS = ((128, 1), (512, 4), (2048, 16))
QBLK = 128
GW = HEADS * HEAD_DIM
ATTN_W = 3 * GW
REL_BUCKETS = 32
REL_MAX_DISTANCE = 2048
CONV_W = 31
HALO = 32
FFN = 2816
IN_W = 3 * ATTN_W + 2 * D + 2 * D
RMS_EPS = 1e-6
LN_EPS = 1e-5
NEG_INF = -1e30
SCALE = HEAD_DIM ** -0.5
NDEV = 8

ADAM_LR = 0.001
ADAM_B1 = 0.9
ADAM_B2 = 0.999
ADAM_EPS = 1e-08
ADAM_WD = 0.01
ADAM_STEP = 10

Z_G0 = 4096 // GW


def _kvq_blocks(g):
    return (Z_G0 + 3 * g, Z_G0 + 3 * g + 1, Z_G0 + 3 * g + 2)


WG_TK = 2048
VMEM_LIMIT = 52 * 1024 * 1024


def _cp(sem=None):
    if sem is None:
        return pltpu.CompilerParams(vmem_limit_bytes=VMEM_LIMIT)
    return pltpu.CompilerParams(vmem_limit_bytes=VMEM_LIMIT, dimension_semantics=sem)


def _sig(v):
    return jax.nn.sigmoid(v)


def _psum8(v):
    return v.reshape(v.shape[0] // 8, 8, v.shape[1]).sum(axis=0)


def _rms_r(v):
    return lax.rsqrt(jnp.mean(v * v, axis=-1, keepdims=True) + RMS_EPS)


def _rms_bwd(v, r, g, dy):
    gy = dy * g
    dv = r * gy - v * (r * r * r) * jnp.mean(v * gy, axis=-1, keepdims=True)
    return dv, dy * v * r


def _clip_k(k, k0, nk):
    return jnp.clip(k - k0, 0, nk - 1)


def _fused_mm(name, M, tm, grid_n, a_ops, b_ops, terms, acc_shapes, rows, consts, outs, parts, epilogue, n_outer=False):
    gm = M // tm
    nk_total = max([t[3] + t[4] for t in terms], default=1)
    n_a, n_b, n_r, n_c, n_o, n_p = len(a_ops), len(b_ops), len(rows), len(consts), len(outs), len(parts)
    n_acc = len(acc_shapes)
    use_scratch = nk_total > 1
    if parts:
        assert grid_n == 1

    def jj(j, follow):
        return j if follow else 0

    in_specs, args = [], []
    for (arr, tk, k0, nk) in a_ops:
        in_specs.append(pl.BlockSpec((tm, tk), functools.partial(lambda i, j, k, k0, nk: (i, _clip_k(k, k0, nk)), k0=k0, nk=nk)))
        args.append(arr)
    for (arr, nt, tk, tn, k0, nk, koff, joff, fj) in b_ops:
        if nt:
            in_specs.append(pl.BlockSpec((tn, tk), functools.partial(
                lambda i, j, k, k0, nk, koff, joff, fj: (joff + jj(j, fj), _clip_k(k, k0, nk) + koff),
                k0=k0, nk=nk, koff=koff, joff=joff, fj=fj)))
        else:
            in_specs.append(pl.BlockSpec((tk, tn), functools.partial(
                lambda i, j, k, k0, nk, koff, joff, fj: (_clip_k(k, k0, nk) + koff, joff + jj(j, fj)),
                k0=k0, nk=nk, koff=koff, joff=joff, fj=fj)))
        args.append(arr)
    for (arr, w, off, fj) in rows:
        in_specs.append(pl.BlockSpec((tm, w), functools.partial(lambda i, j, k, off, fj: (i, off + jj(j, fj)), off=off, fj=fj)))
        args.append(arr)
    for arr in consts:
        in_specs.append(pl.BlockSpec(arr.shape, functools.partial(lambda i, j, k, nd: (0,) * nd, nd=arr.ndim)))
        args.append(arr)
    out_specs, out_shape = [], []
    for (ncols, dt, w, off, fj) in outs:
        out_specs.append(pl.BlockSpec((tm, w), functools.partial(lambda i, j, k, off, fj: (i, off + jj(j, fj)), off=off, fj=fj)))
        out_shape.append(jax.ShapeDtypeStruct((M, ncols), dt))
    for (r, c) in parts:
        out_specs.append(pl.BlockSpec((r, c), lambda i, j, k: (0, 0)))
        out_shape.append(jax.ShapeDtypeStruct((r, c), F32))
    scratch = [pltpu.VMEM(s, F32) for s in acc_shapes] if use_scratch else []

    def body(*refs):
        pos = 0
        a_refs = refs[pos:pos + n_a]; pos += n_a
        b_refs = refs[pos:pos + n_b]; pos += n_b
        r_refs = refs[pos:pos + n_r]; pos += n_r
        c_refs = refs[pos:pos + n_c]; pos += n_c
        o_refs = refs[pos:pos + n_o]; pos += n_o
        p_refs = refs[pos:pos + n_p]; pos += n_p
        acc_refs = refs[pos:pos + n_acc] if use_scratch else ()
        i = pl.program_id(0)
        k = pl.program_id(2)

        def dot_of(ai, bi):
            a = a_refs[ai][...].astype(BF)
            b = b_refs[bi][...].astype(BF)
            if b_ops[bi][1]:
                return lax.dot_general(a, b, (((1,), (1,)), ((), ())), preferred_element_type=F32)
            return jnp.dot(a, b, preferred_element_type=F32)

        if parts:
            @pl.when((i == 0) & (k == 0))
            def _():
                for p in p_refs:
                    p[...] = jnp.zeros(p.shape, F32)

        def finish(accs):
            epilogue(accs, r_refs, c_refs, o_refs, p_refs)
            if parts:
                @pl.when(i == gm - 1)
                def _():
                    for p in p_refs:
                        p[0:1, :] = jnp.sum(p[...], axis=0, keepdims=True)

        if not use_scratch:
            accs = [None] * n_acc
            for (ai, bi, ci, k0, nk) in terms:
                d = dot_of(ai, bi)
                accs[ci] = d if accs[ci] is None else accs[ci] + d
            finish(accs)
        else:
            @pl.when(k == 0)
            def _():
                for acc in acc_refs:
                    acc[...] = jnp.zeros(acc.shape, F32)

            for (ai, bi, ci, k0, nk) in terms:
                def do(ai=ai, bi=bi, ci=ci):
                    acc_refs[ci][...] += dot_of(ai, bi)
                if k0 == 0 and nk == nk_total:
                    do()
                else:
                    pl.when((k >= k0) & (k < k0 + nk))(do)

            @pl.when(k == nk_total - 1)
            def _():
                finish([acc[...] for acc in acc_refs])

    grid = (gm, grid_n, nk_total)
    if n_outer:
        assert not parts
        swap = lambda spec: pl.BlockSpec(spec.block_shape, functools.partial(lambda j, i, k, f: f(i, j, k), f=spec.index_map))
        in_specs, out_specs, grid = [swap(sp) for sp in in_specs], [swap(sp) for sp in out_specs], (grid_n, gm, nk_total)
    res = pl.pallas_call(
        body, grid=grid, in_specs=in_specs, out_specs=out_specs, out_shape=out_shape,
        scratch_shapes=scratch, compiler_params=_cp(("arbitrary", "arbitrary", "arbitrary")), name=name,
    )(*args)
    return res


def _mm_tn(name, a, b, tm, tn, tk):
    S, Ka = a.shape
    Nb = b.shape[1]
    nk = S // tk

    def body(a_ref, b_ref, o_ref, acc):
        k = pl.program_id(2)

        @pl.when(k == 0)
        def _():
            acc[...] = jnp.zeros(acc.shape, F32)

        acc[...] += lax.dot_general(a_ref[...], b_ref[...], (((0,), (0,)), ((), ())), preferred_element_type=F32)

        @pl.when(k == nk - 1)
        def _():
            o_ref[...] = acc[...].astype(o_ref.dtype)

    return pl.pallas_call(
        body, grid=(Ka // tm, Nb // tn, nk),
        in_specs=[pl.BlockSpec((tk, tm), lambda i, j, k: (k, i)), pl.BlockSpec((tk, tn), lambda i, j, k: (k, j))],
        out_specs=pl.BlockSpec((tm, tn), lambda i, j, k: (i, j)),
        out_shape=jax.ShapeDtypeStruct((Ka, Nb), BF),
        scratch_shapes=[pltpu.VMEM((tm, tn), F32)],
        compiler_params=_cp(("parallel", "parallel", "arbitrary")), name=name,
    )(a, b)


def _resident_mm(name, M, tm, a_segs, w, rows, consts, outs, parts, epilogue):
    gm = M // tm
    n_a, n_r, n_c, n_o, n_p = len(a_segs), len(rows), len(consts), len(outs), len(parts)
    widths = [a.shape[1] for a in a_segs]
    offs = [sum(widths[:t]) for t in range(n_a)]
    once = pl.Buffered(1)

    def body(*refs):
        pos = 0
        a_refs = refs[pos:pos + n_a]; pos += n_a
        w_ref = refs[pos]; pos += 1
        r_refs = refs[pos:pos + n_r]; pos += n_r
        c_refs = refs[pos:pos + n_c]; pos += n_c
        o_refs = refs[pos:pos + n_o]; pos += n_o
        p_refs = refs[pos:pos + n_p]
        i = pl.program_id(0)
        if parts:
            @pl.when(i == 0)
            def _():
                for p in p_refs:
                    p[...] = jnp.zeros(p.shape, F32)
        acc = None
        for t in range(n_a):
            d = lax.dot_general(a_refs[t][...], w_ref[:, offs[t]:offs[t] + widths[t]], (((1,), (1,)), ((), ())),
                                preferred_element_type=F32)
            acc = d if acc is None else acc + d
        epilogue([acc], r_refs, c_refs, o_refs, p_refs)
        if parts:
            @pl.when(i == gm - 1)
            def _():
                for p in p_refs:
                    p[0:1, :] = jnp.sum(p[...], axis=0, keepdims=True)

    in_specs = [pl.BlockSpec((tm, wd), lambda i: (i, 0)) for wd in widths]
    in_specs.append(pl.BlockSpec(w.shape, lambda i: (0, 0), pipeline_mode=once))
    in_specs += [pl.BlockSpec((tm, c), lambda i: (i, 0)) for _, c in rows]
    in_specs += [pl.BlockSpec(c.shape, lambda i: (0, 0), pipeline_mode=once) for c in consts]
    out_specs = [pl.BlockSpec((tm, nc), lambda i: (i, 0)) for nc, _ in outs] + [pl.BlockSpec(pc, lambda i: (0, 0)) for pc in parts]
    out_shape = [jax.ShapeDtypeStruct((M, nc), dt) for nc, dt in outs] + [jax.ShapeDtypeStruct(pc, F32) for pc in parts]
    return pl.pallas_call(
        body, grid=(gm,), in_specs=in_specs, out_specs=out_specs, out_shape=out_shape,
        compiler_params=_cp(("arbitrary",)), name=name,
    )(*a_segs, w, *[r for r, _ in rows], *consts)


def _in_proj(xs, g1, Wp):
    S = xs.shape[0]
    tm, tn = 1024, IN_W // 4

    def body(x_ref, g_ref, w_ref, h_ref, z_ref, hs):
        @pl.when(pl.program_id(1) == 0)
        def _():
            v = x_ref[...]
            h = (v * _rms_r(v) * g_ref[...]).astype(BF)
            hs[...] = h
            h_ref[...] = h

        z_ref[...] = jnp.dot(hs[...], w_ref[...], preferred_element_type=F32).astype(BF)

    return pl.pallas_call(
        body, grid=(S // tm, IN_W // tn),
        in_specs=[pl.BlockSpec((tm, D), lambda i, j: (i, 0)), pl.BlockSpec((1, D), lambda i, j: (0, 0)),
                  pl.BlockSpec((D, tn), lambda i, j: (0, j))],
        out_specs=[pl.BlockSpec((tm, D), lambda i, j: (i, 0)), pl.BlockSpec((tm, tn), lambda i, j: (i, j))],
        out_shape=[jax.ShapeDtypeStruct((S, D), BF), jax.ShapeDtypeStruct((S, IN_W), BF)],
        scratch_shapes=[pltpu.VMEM((tm, D), BF)],
        compiler_params=_cp(("arbitrary", "arbitrary")), name="in_proj",
    )(xs, g1, Wp)


def _rel_bucket_np(dist):
    max_exact = REL_BUCKETS // 2
    d = np.maximum(dist, 0)
    df = np.maximum(d, 1).astype(np.float32)
    large = max_exact + (np.log(df / np.float32(max_exact)) / np.float32(math.log(REL_MAX_DISTANCE / max_exact))
                         * np.float32(REL_BUCKETS - max_exact)).astype(np.int32)
    large = np.minimum(large, REL_BUCKETS - 1)
    return np.where(d < max_exact, d, large).astype(np.int32)


N_LAYOUTS = 2


def _band_index():
    idx = np.zeros((N_LAYOUTS * 3, 1, QBLK * 2 * QBLK), np.int32)
    for g, (window, dil) in enumerate(GROUPS):
        span = window // dil
        k = np.arange(2 * QBLK)[:, None]; q = np.arange(QBLK)[None, :]
        off = q - k + QBLK
        idx[g, 0] = np.where((off >= 0) & (off <= span), _rel_bucket_np(off * dil), -1).reshape(-1)
        k = np.arange(QBLK)[:, None]; q = np.arange(2 * QBLK)[None, :]
        off = q - k
        idx[3 + g, 0] = np.where((off >= 0) & (off <= span), _rel_bucket_np(off * dil), -1).reshape(-1)
    return idx


_NB = QBLK * 2 * QBLK
_BCH = 4096


def _bias_build(tab_t, idx):
    def body(t_ref, i_ref, o_ref):
        ix = i_ref[0]
        t = t_ref[0]
        acc = jnp.full((HEADS, _BCH), NEG_INF, F32)
        for b in range(REL_BUCKETS):
            acc = jnp.where(ix == b, t[:, b:b + 1], acc)
        o_ref[0] = acc

    return pl.pallas_call(
        body, grid=(N_LAYOUTS * 3, _NB // _BCH),
        in_specs=[pl.BlockSpec((1, HEADS, REL_BUCKETS), lambda l, n: (l % 3, 0, 0)),
                  pl.BlockSpec((1, 1, _BCH), lambda l, n: (l, 0, n))],
        out_specs=pl.BlockSpec((1, HEADS, _BCH), lambda l, n: (l, 0, n)),
        out_shape=jax.ShapeDtypeStruct((N_LAYOUTS * 3, HEADS, _NB), F32), compiler_params=_cp(), name="bias_build",
    )(tab_t, idx)


def _bias_grad(ds, idx):
    nch = _NB // _BCH

    def body(d_ref, i_ref, o_ref):
        n = pl.program_id(1)

        @pl.when(n == 0)
        def _():
            o_ref[...] = jnp.zeros(o_ref.shape, F32)

        ix = i_ref[0]
        d = d_ref[0]
        lane = lax.broadcasted_iota(jnp.int32, (HEADS, 128), 1)
        acc = jnp.zeros((HEADS, 128), F32)
        for b in range(REL_BUCKETS):
            s = jnp.sum(jnp.where(ix == b, d, 0.0), axis=1, keepdims=True)
            acc = acc + jnp.where(lane == b, s, 0.0)
        o_ref[0] += acc

    return pl.pallas_call(
        body, grid=(3, nch),
        in_specs=[pl.BlockSpec((1, HEADS, _BCH), lambda l, n: (l, 0, n)),
                  pl.BlockSpec((1, 1, _BCH), lambda l, n: (3 + l, 0, n))],
        out_specs=pl.BlockSpec((1, HEADS, 128), lambda l, n: (l, 0, 0)),
        out_shape=jax.ShapeDtypeStruct((3, HEADS, 128), F32), compiler_params=_cp(), name="bias_grad",
    )(ds, idx)


PT = 256
PSTEP = 2048
STAT_W = 128


def _perm_np(dil):
    p = np.zeros((PT, PT), np.float32)
    m = np.arange(PT // dil)
    for c in range(dil):
        p[c * (PT // dil) + m, m * dil + c] = 1.0
    return p


def _perm_const(dil, dtype, inverse):
    p = _perm_np(dil)
    return jnp.asarray(p.T if inverse else p, dtype)


def _apply_perm(p, x):
    if x.dtype == F32:
        return jnp.dot(p, x, preferred_element_type=F32, precision=lax.Precision.HIGHEST)
    return jnp.dot(p, x, preferred_element_type=F32)


def _to_residue(name, arr, col_blocks, dil):
    S = arr.shape[0]
    nc = len(col_blocks)
    p = _perm_const(dil, arr.dtype, False)
    sub = PT // dil

    def body(*refs):
        p_ref, ins, o_ref = refs[0], refs[1:1 + nc], refs[1 + nc]
        for u in range(PSTEP // PT):
            for t, r in enumerate(ins):
                y = _apply_perm(p_ref[...], r[u * PT:(u + 1) * PT, :]).astype(o_ref.dtype)
                o_ref[:, u * sub:(u + 1) * sub, t * GW:(t + 1) * GW] = y.reshape(dil, sub, GW)

    out = pl.pallas_call(
        body, grid=(S // PSTEP,),
        in_specs=[pl.BlockSpec((PT, PT), lambda i: (0, 0))]
                 + [pl.BlockSpec((PSTEP, GW), functools.partial(lambda i, cb: (i, cb), cb=cb)) for cb in col_blocks],
        out_specs=pl.BlockSpec((dil, PSTEP // dil, nc * GW), lambda i: (0, i, 0)),
        out_shape=jax.ShapeDtypeStruct((dil, S // dil, nc * GW), arr.dtype), compiler_params=_cp(), name=name,
    )(p, *([arr] * nc))
    return out.reshape(S, nc * GW)


def _to_residue_pair(name, do, stats, dil):
    S = do.shape[0]
    pb = _perm_const(dil, BF, False)
    pf = _perm_const(dil, F32, False)
    sub = PT // dil

    def body(pb_ref, pf_ref, d_ref, s_ref, od_ref, os_ref):
        for u in range(PSTEP // PT):
            rs = slice(u * PT, (u + 1) * PT)
            od_ref[:, u * sub:(u + 1) * sub, :] = _apply_perm(pb_ref[...], d_ref[rs, :]).astype(BF).reshape(dil, sub, GW)
            os_ref[:, u * sub:(u + 1) * sub, :] = _apply_perm(pf_ref[...], s_ref[rs, :]).reshape(dil, sub, STAT_W)

    cst = pl.BlockSpec((PT, PT), lambda i: (0, 0))
    od, os_ = pl.pallas_call(
        body, grid=(S // PSTEP,),
        in_specs=[cst, cst, pl.BlockSpec((PSTEP, GW), lambda i: (i, 0)), pl.BlockSpec((PSTEP, STAT_W), lambda i: (i, 0))],
        out_specs=[pl.BlockSpec((dil, PSTEP // dil, GW), lambda i: (0, i, 0)), pl.BlockSpec((dil, PSTEP // dil, STAT_W), lambda i: (0, i, 0))],
        out_shape=[jax.ShapeDtypeStruct((dil, S // dil, GW), BF), jax.ShapeDtypeStruct((dil, S // dil, STAT_W), F32)],
        compiler_params=_cp(), name=name,
    )(pb, pf, do, stats)
    return od.reshape(S, GW), os_.reshape(S, STAT_W)


def _from_residue(name, arr, dil):
    S, W = arr.shape
    p = _perm_const(dil, arr.dtype, True)
    sub = PT // dil

    def body(p_ref, x_ref, o_ref):
        for u in range(PSTEP // PT):
            x = x_ref[:, u * sub:(u + 1) * sub, :].reshape(PT, W)
            o_ref[u * PT:(u + 1) * PT, :] = _apply_perm(p_ref[...], x).astype(o_ref.dtype)

    return pl.pallas_call(
        body, grid=(S // PSTEP,),
        in_specs=[pl.BlockSpec((PT, PT), lambda i: (0, 0)), pl.BlockSpec((dil, PSTEP // dil, W), lambda i: (0, i, 0))],
        out_specs=pl.BlockSpec((PSTEP, W), lambda i: (i, 0)),
        out_shape=jax.ShapeDtypeStruct((S, W), arr.dtype), compiler_params=_cp(), name=name,
    )(p, arr.reshape(dil, S // dil, W))


PAIR_W = 2 * HEAD_DIM
NT_DIMS = (((1,), (1,)), ((), ()))
TN_DIMS = (((0,), (0,)), ((), ()))


def _attn_dims(S, dil):
    L = S // dil
    TQ = min(512, L)
    return L, TQ, L // TQ, TQ // QBLK


def _attn_specs(S, dil):
    L, TQ, nq, nsub = _attn_dims(S, dil)
    nb = L // QBLK
    cur = lambda cb, w=GW: pl.BlockSpec((TQ, w), lambda c, i: (c * nq + i, cb))
    prev = lambda cb, w=GW: pl.BlockSpec((QBLK, w), lambda c, i: (c * nb + jnp.maximum(i * nsub - 1, 0), cb))
    nxt = lambda cb, w=GW: pl.BlockSpec((QBLK, w), lambda c, i: (c * nb + jnp.minimum((i + 1) * nsub, nb - 1), cb))
    band = lambda r, c_: pl.BlockSpec((HEADS, r, c_), lambda c, i: (0, 0, 0))
    return L, TQ, nq, nsub, cur, prev, nxt, band


def _fill(buf, first_ref, second_ref):
    n = first_ref.shape[0]
    buf[0:n, :] = first_ref[...]
    buf[n:n + second_ref.shape[0], :] = second_ref[...]


def _attn_fwd(name, arr, bias_kq, cb, dil):
    S = arr.shape[0]
    kcb, vcb, qcb = cb
    L, TQ, nq, nsub, cur, prev, nxt, band = _attn_specs(S, dil)

    def body(q_ref, kc_ref, kp_ref, vc_ref, vp_ref, b_ref, o_ref, l_ref, kbuf, vbuf):
        i = pl.program_id(1)
        _fill(kbuf, kp_ref, kc_ref)
        _fill(vbuf, vp_ref, vc_ref)
        row = lax.broadcasted_iota(jnp.int32, (2 * QBLK, QBLK), 0)
        first = (row >= QBLK) | (i > 0)
        low = lax.broadcasted_iota(jnp.int32, (QBLK, PAIR_W), 1) < HEAD_DIM
        zero = jnp.zeros((QBLK, PAIR_W), BF)
        for j in range(nsub):
            rs = slice(j * QBLK, (j + 1) * QBLK)
            ks = slice(j * QBLK, (j + 2) * QBLK)
            lrows = []
            for hp in range(HEADS // 2):
                ps = slice(hp * PAIR_W, (hp + 1) * PAIR_W)
                qp = q_ref[rs, ps] * SCALE
                kp = kbuf[ks, ps]
                vp = vbuf[ks, ps]
                halves = []
                for t in range(2):
                    qm = jnp.where(low if t == 0 else ~low, qp, zero)
                    s = lax.dot_general(kp, qm, NT_DIMS, preferred_element_type=F32) + b_ref[2 * hp + t]
                    if j == 0:
                        s = jnp.where(first, s, NEG_INF)
                    m = jnp.max(s, axis=0, keepdims=True)
                    p = jnp.exp(s - m)
                    den = jnp.sum(p, axis=0, keepdims=True)
                    o2 = lax.dot_general(vp, p.astype(BF), TN_DIMS, preferred_element_type=F32)
                    halves.append(o2[t * HEAD_DIM:(t + 1) * HEAD_DIM, :] / den)
                    lrows.append(m + jnp.log(den))
                o_ref[rs, ps] = jnp.concatenate(halves, axis=0).T.astype(BF)
            lt = jnp.concatenate(lrows + [jnp.zeros((STAT_W - HEADS, QBLK), F32)], axis=0)
            l_ref[rs, :] = lt.T

    return pl.pallas_call(
        body, grid=(dil, nq),
        in_specs=[cur(qcb), cur(kcb), prev(kcb), cur(vcb), prev(vcb), band(2 * QBLK, QBLK)],
        out_specs=[cur(0), cur(0, STAT_W)],
        out_shape=[jax.ShapeDtypeStruct((S, GW), BF), jax.ShapeDtypeStruct((S, STAT_W), F32)],
        scratch_shapes=[pltpu.VMEM((QBLK + TQ, GW), BF), pltpu.VMEM((QBLK + TQ, GW), BF)],
        compiler_params=_cp(), name=name,
    )(arr, arr, arr, arr, arr, bias_kq)


def _attn_bwd(name, arr, bias_kq2, do, stats, cb, dil):
    S = arr.shape[0]
    kcb, vcb, qcb = cb
    L, TQ, nq, nsub, cur, prev, nxt, band = _attn_specs(S, dil)

    def body(k_ref, v_ref, qc_ref, qn_ref, b_ref, doc_ref, don_ref, sc_ref, sn_ref, o_ref, db_ref, qbuf, dobuf, sbuf, carry):
        c = pl.program_id(0)
        i = pl.program_id(1)

        @pl.when((c == 0) & (i == 0))
        def _():
            db_ref[...] = jnp.zeros(db_ref.shape, F32)
            carry[...] = jnp.zeros(carry.shape, F32)

        _fill(qbuf, qc_ref, qn_ref)
        _fill(dobuf, doc_ref, don_ref)
        for j in range(nsub + 1):
            rs = slice(j * QBLK, (j + 1) * QBLK)
            sbuf[:, rs] = (sc_ref[rs, :] if j < nsub else sn_ref[...]).T
        col = lax.broadcasted_iota(jnp.int32, (QBLK, 2 * QBLK), 1)
        last = (col < QBLK) | (i < nq - 1)
        low = lax.broadcasted_iota(jnp.int32, (QBLK, PAIR_W), 1) < HEAD_DIM
        zero = jnp.zeros((QBLK, PAIR_W), BF)
        for hp in range(HEADS // 2):
            ps = slice(hp * PAIR_W, (hp + 1) * PAIR_W)
            dbs = [jnp.zeros((QBLK, 2 * QBLK), F32), jnp.zeros((QBLK, 2 * QBLK), F32)]
            tail = carry[:, ps]
            for j in range(nsub):
                rs = slice(j * QBLK, (j + 1) * QBLK)
                qs = slice(j * QBLK, (j + 2) * QBLK)
                qp = qbuf[qs, ps]
                dd = dobuf[qs, ps]
                kp = k_ref[rs, ps] * SCALE
                vp = v_ref[rs, ps]
                kt = kp.T
                dk, dv, dqt = [], [], []
                for t in range(2):
                    h = 2 * hp + t
                    sel = low if t == 0 else ~low
                    s = lax.dot_general(jnp.where(sel, kp, zero), qp, NT_DIMS, preferred_element_type=F32) + b_ref[h]
                    if j == nsub - 1:
                        s = jnp.where(last, s, NEG_INF)
                    p = jnp.exp(s - sbuf[h:h + 1, qs])
                    dp = lax.dot_general(jnp.where(sel, vp, zero), dd, NT_DIMS, preferred_element_type=F32)
                    ds = p * (dp - sbuf[HEADS + h:HEADS + h + 1, qs])
                    dbs[t] = dbs[t] + ds
                    dsb = ds.astype(BF)
                    dk.append(jnp.dot(dsb, qp, preferred_element_type=F32))
                    dv.append(jnp.dot(p.astype(BF), dd, preferred_element_type=F32))
                    dqt.append(jnp.dot(kt[t * HEAD_DIM:(t + 1) * HEAD_DIM, :], dsb, preferred_element_type=F32))
                o_ref[rs, ps] = (jnp.where(low, dk[0], dk[1]) * SCALE).astype(BF)
                o_ref[rs, GW + hp * PAIR_W:GW + (hp + 1) * PAIR_W] = jnp.where(low, dv[0], dv[1]).astype(BF)
                dq2 = jnp.concatenate(dqt, axis=0).T
                o_ref[rs, 2 * GW + hp * PAIR_W:2 * GW + (hp + 1) * PAIR_W] = (dq2[0:QBLK] + tail).astype(BF)
                tail = dq2[QBLK:2 * QBLK]
            carry[:, ps] = tail
            db_ref[2 * hp] += dbs[0]
            db_ref[2 * hp + 1] += dbs[1]

    return pl.pallas_call(
        body, grid=(dil, nq),
        in_specs=[cur(kcb), cur(vcb), cur(qcb), nxt(qcb), band(QBLK, 2 * QBLK),
                  cur(0), nxt(0), cur(0, STAT_W), nxt(0, STAT_W)],
        out_specs=[cur(0, ATTN_W), band(QBLK, 2 * QBLK)],
        out_shape=[jax.ShapeDtypeStruct((S, ATTN_W), BF), jax.ShapeDtypeStruct((HEADS, QBLK, 2 * QBLK), F32)],
        scratch_shapes=[pltpu.VMEM((TQ + QBLK, GW), BF), pltpu.VMEM((TQ + QBLK, GW), BF), pltpu.VMEM((STAT_W, TQ + QBLK), F32),
                        pltpu.VMEM((QBLK, GW), F32)],
        compiler_params=_cp(("arbitrary", "arbitrary")), name=name,
    )(arr, arr, arr, arr, bias_kq2, do, do, stats, stats)


def _head_expand():
    e = np.zeros((STAT_W, GW), np.float32)
    for h in range(HEADS):
        e[h, h * HEAD_DIM:(h + 1) * HEAD_DIM] = 1.0
    return e


def _attn_merge(os_, ls_, S):
    dils = [d for _, d in GROUPS]
    pb = [_perm_const(d, BF, True) for d in dils[1:]]
    pf = [_perm_const(d, F32, True) for d in dils[1:]]
    expand = jnp.asarray(_head_expand(), BF)

    def body(o0, o1, o2, l0, l1, l2, pb1, pb2, pf1, pf2, e_ref, o_ref, l_ref):
        for u in range(PSTEP // PT):
            rs = slice(u * PT, (u + 1) * PT)
            res = lambda r, d: r[:, u * (PT // d):(u + 1) * (PT // d), :].reshape(PT, r.shape[2])
            ov = [o0[rs, :].astype(F32), _apply_perm(pb1[...], res(o1, dils[1])), _apply_perm(pb2[...], res(o2, dils[2]))]
            lv = [l0[rs, :], _apply_perm(pf1[...], res(l1, dils[1])), _apply_perm(pf2[...], res(l2, dils[2]))]
            m = jnp.maximum(jnp.maximum(lv[0], lv[1]), lv[2])
            ev = [jnp.exp(l - m) for l in lv]
            den = ev[0] + ev[1] + ev[2]
            acc = jnp.zeros((PT, GW), F32)
            for g in range(3):
                wide = jnp.dot((ev[g] / den).astype(BF), e_ref[...], preferred_element_type=F32)
                acc = acc + wide * ov[g]
            o_ref[rs, :] = acc.astype(BF)
            l_ref[rs, :] = m + jnp.log(den)

    nat = lambda w: pl.BlockSpec((PSTEP, w), lambda i: (i, 0))
    res = lambda d, w: pl.BlockSpec((d, PSTEP // d, w), lambda i: (0, i, 0))
    cst = lambda a: pl.BlockSpec(a.shape, lambda i: (0, 0))
    args = [os_[0], os_[1].reshape(dils[1], S // dils[1], GW), os_[2].reshape(dils[2], S // dils[2], GW),
            ls_[0], ls_[1].reshape(dils[1], S // dils[1], STAT_W), ls_[2].reshape(dils[2], S // dils[2], STAT_W),
            pb[0], pb[1], pf[0], pf[1], expand]
    return pl.pallas_call(
        body, grid=(S // PSTEP,),
        in_specs=[nat(GW), res(dils[1], GW), res(dils[2], GW), nat(STAT_W), res(dils[1], STAT_W), res(dils[2], STAT_W)]
                 + [cst(a) for a in args[6:]],
        out_specs=[nat(GW), nat(STAT_W)],
        out_shape=[jax.ShapeDtypeStruct((S, GW), BF), jax.ShapeDtypeStruct((S, STAT_W), F32)],
        compiler_params=_cp(), name="attn_merge",
    )(*args)


CT = 512
CBUF = HALO + CT + 8
RG = 4


def _ln_hat(u1):
    mu = jnp.mean(u1, axis=-1, keepdims=True)
    xc = u1 - mu
    rstd = lax.rsqrt(jnp.mean(xc * xc, axis=-1, keepdims=True) + LN_EPS)
    return xc * rstd, rstd


def _glu_window(hu_ref, hg_ref, huh_ref, hgh_ref, bglu_ref, buf_ref, i):
    bu = bglu_ref[:, 0:D]
    bg = bglu_ref[:, D:2 * D]
    uh = (huh_ref[...].astype(F32) + bu) * _sig(hgh_ref[...].astype(F32) + bg)
    buf_ref[0:HALO, :] = jnp.where(i > 0, uh, 0.0)
    a = hu_ref[...].astype(F32) + bu
    s = _sig(hg_ref[...].astype(F32) + bg)
    buf_ref[HALO:HALO + CT, :] = a * s
    buf_ref[HALO + CT:CBUF, :] = jnp.zeros((8, D), F32)
    return a, s


def _shift_copies(buf_ref, sh_ref):
    for r in range(8):
        sh_ref[r] = buf_ref[r:r + HALO + CT, :]


def _tap_rows(wb_ref, w_ref):
    for j in range(CONV_W):
        wb_ref[j * 8:(j + 1) * 8, :] = jnp.broadcast_to(w_ref[j:j + 1, :], (8, D))


def _conv_taps(sh_ref, wb_ref, out_ref, init, offset):
    for rg in range(CT // (8 * RG)):
        accs = [init] * RG
        for j in range(CONV_W):
            off = offset(j)
            wj = wb_ref[j * 8:(j + 1) * 8, :]
            for q in range(RG):
                row = 8 * (rg * RG + q + off // 8)
                accs[q] = accs[q] + wj * sh_ref[off % 8, row:row + 8, :]
        for q in range(RG):
            out_ref[(rg * RG + q) * 8:(rg * RG + q + 1) * 8, :] = accs[q]


def _conv_specs(S):
    cur = lambda cb: pl.BlockSpec((CT, D), lambda i: (i, cb))
    halo = lambda cb: pl.BlockSpec((HALO, D), lambda i: (jnp.maximum(i * (CT // HALO) - 1, 0), cb))
    full = lambda shp: pl.BlockSpec(shp, lambda i: (0, 0))
    return cur, halo, full


def _conv_fwd(z, b_glu, w_dw, b_dw, g_ln, b_ln):
    S = z.shape[0]
    cur, halo, full = _conv_specs(S)

    def body(hu, hg, huh, hgh, bglu, w, bdw, gln, bln, u1_ref, u3_ref, buf, sh, wb):
        i = pl.program_id(0)

        @pl.when(i == 0)
        def _():
            _tap_rows(wb, w)

        _glu_window(hu, hg, huh, hgh, bglu, buf, i)
        _shift_copies(buf, sh)
        _conv_taps(sh, wb, u1_ref, jnp.broadcast_to(bdw[...], (8, D)), lambda j: 2 + j)
        xh, _ = _ln_hat(u1_ref[...])
        u2 = xh * gln[...] + bln[...]
        u3_ref[...] = (u2 * _sig(u2)).astype(BF)

    return pl.pallas_call(
        body, grid=(S // CT,),
        in_specs=[cur(0), cur(1), halo(0), halo(1), full((1, 2 * D)), full((HALO, D)), full((1, D)), full((1, D)), full((1, D))],
        out_specs=[pl.BlockSpec((CT, D), lambda i: (i, 0))] * 2,
        out_shape=[jax.ShapeDtypeStruct((S, D), F32), jax.ShapeDtypeStruct((S, D), BF)],
        scratch_shapes=[pltpu.VMEM((CBUF, D), F32), pltpu.VMEM((8, HALO + CT, D), F32), pltpu.VMEM((CONV_W * 8, D), F32)],
        compiler_params=_cp(("arbitrary",)), name="conv_fwd",
    )(z, z, z, z, b_glu, w_dw, b_dw, g_ln, b_ln)


def _conv_bwd(du1, z, b_glu, w_dw):
    S = z.shape[0]
    n = S // CT
    cur, halo, full = _conv_specs(S)
    RG2 = 2

    def body(du, dun, hu, hg, bglu, w, dz_ref, dw_ref, dbg_ref, bufd, shd, wb, u0_ref, du0_ref, dwacc):
        i = pl.program_id(0)

        @pl.when(i == 0)
        def _():
            _tap_rows(wb, w)
            dwacc[...] = jnp.zeros(dwacc.shape, F32)
            dbg_ref[...] = jnp.zeros(dbg_ref.shape, F32)

        a = hu[...].astype(F32) + bglu[:, 0:D]
        s = _sig(hg[...].astype(F32) + bglu[:, D:2 * D])
        u0_ref[...] = a * s
        bufd[0:CT, :] = du[...]
        bufd[CT:CT + HALO, :] = jnp.where(i < n - 1, dun[...], 0.0)
        bufd[CT + HALO:CBUF, :] = jnp.zeros((8, D), F32)
        _shift_copies(bufd, shd)
        for rg in range(CT // (8 * RG2)):
            uch = [u0_ref[(rg * RG2 + q) * 8:(rg * RG2 + q + 1) * 8, :] for q in range(RG2)]
            accs = [jnp.zeros((8, D), F32)] * RG2
            for j in range(CONV_W):
                off = 30 - j
                wj = wb[j * 8:(j + 1) * 8, :]
                dwj = dwacc[j * 8:(j + 1) * 8, :]
                for q in range(RG2):
                    row = 8 * (rg * RG2 + q + off // 8)
                    x = shd[off % 8, row:row + 8, :]
                    accs[q] = accs[q] + wj * x
                    dwj = dwj + uch[q] * x
                dwacc[j * 8:(j + 1) * 8, :] = dwj
            for q in range(RG2):
                du0_ref[(rg * RG2 + q) * 8:(rg * RG2 + q + 1) * 8, :] = accs[q]
        du0 = du0_ref[...]
        dhu = du0 * s
        dhg = du0 * a * s * (1.0 - s)
        dz_ref[:, 0:D] = dhu.astype(BF)
        dz_ref[:, D:2 * D] = dhg.astype(BF)
        dbg_ref[:, 0:D] += _psum8(dhu)
        dbg_ref[:, D:2 * D] += _psum8(dhg)

        @pl.when(i == n - 1)
        def _():
            dbg_ref[0:1, :] = jnp.sum(dbg_ref[...], axis=0, keepdims=True)
            for j in range(CONV_W):
                dw_ref[j:j + 1, :] = jnp.sum(dwacc[j * 8:(j + 1) * 8, :], axis=0, keepdims=True)
            dw_ref[CONV_W:HALO, :] = jnp.zeros((HALO - CONV_W, D), F32)

    nxt = pl.BlockSpec((HALO, D), lambda i: (jnp.minimum((i + 1) * (CT // HALO), S // HALO - 1), 0))
    return pl.pallas_call(
        body, grid=(n,),
        in_specs=[pl.BlockSpec((CT, D), lambda i: (i, 0)), nxt, cur(0), cur(1), full((1, 2 * D)), full((HALO, D))],
        out_specs=[pl.BlockSpec((CT, 2 * D), lambda i: (i, 0)), full((HALO, D)), full((8, 2 * D))],
        out_shape=[jax.ShapeDtypeStruct((S, 2 * D), BF), jax.ShapeDtypeStruct((HALO, D), F32), jax.ShapeDtypeStruct((8, 2 * D), F32)],
        scratch_shapes=[pltpu.VMEM((CBUF, D), F32), pltpu.VMEM((8, HALO + CT, D), F32), pltpu.VMEM((CONV_W * 8, D), F32),
                        pltpu.VMEM((CT, D), F32), pltpu.VMEM((CT, D), F32), pltpu.VMEM((CONV_W * 8, D), F32)],
        compiler_params=_cp(("arbitrary",)), name="conv_bwd",
    )(du1, du1, z, z, b_glu, w_dw)


MESH = pl.DeviceIdType.MESH


def _all_gather(name, shards):
    n = len(shards)

    def body(*refs):
        ins, outs = refs[:n], refs[n:2 * n]
        send_sems, recv_sems, local_sems = refs[2 * n:]
        x, y, c = lax.axis_index("x"), lax.axis_index("y"), lax.axis_index("c")
        me, sibling = (x, y, c), (x, y, 1 - c)
        chips = [(1 - x, y), (x, 1 - y), (1 - x, 1 - y)]

        def slot(a, px, py, pc):
            return outs[a].at[4 * px + 2 * py + pc]

        def copy(a, k, block, to, src=None):
            return pltpu.make_async_remote_copy(
                src_ref=slot(a, *block) if src is None else src, dst_ref=slot(a, *block),
                send_sem=send_sems.at[a, k], recv_sem=recv_sems.at[a, k], device_id=to, device_id_type=MESH)

        mine = [pltpu.make_async_copy(ins[a], slot(a, *me), local_sems.at[a]) for a in range(n)]
        for cp in mine:
            cp.start()
        first = []
        for a in range(n):
            first.append(copy(a, 0, me, sibling, src=ins[a]))
            first += [copy(a, 1 + j, me, (*chip, c), src=ins[a]) for j, chip in enumerate(chips)]
        for cp in first:
            cp.start()
        passed = []
        for j, chip in enumerate(chips):
            for a in range(n):
                copy(a, 1 + j, (*chip, c), me).wait_recv()
                fwd = copy(a, 4 + j, (*chip, c), sibling)
                fwd.start()
                passed.append(fwd)
        for a in range(n):
            copy(a, 0, sibling, me).wait_recv()
        for j, chip in enumerate(chips):
            for a in range(n):
                copy(a, 4 + j, (*chip, 1 - c), me).wait_recv()
        for cp in first + passed:
            cp.wait_send()
        for cp in mine:
            cp.wait()

    anyspec = pl.BlockSpec(memory_space=pl.ANY)
    return pl.pallas_call(
        body, in_specs=[anyspec] * n, out_specs=[anyspec] * n,
        out_shape=[jax.ShapeDtypeStruct((NDEV,) + s.shape, s.dtype) for s in shards],
        scratch_shapes=[pltpu.SemaphoreType.DMA((n, 7)), pltpu.SemaphoreType.DMA((n, 7)), pltpu.SemaphoreType.DMA((n,))],
        name=name,
    )(*shards)


HBM_SPEC = pl.BlockSpec(memory_space=pltpu.HBM)
SEM_SPEC = pl.BlockSpec(memory_space=pltpu.SEMAPHORE)
DATAFLOW = pltpu.SideEffectType.DATAFLOW_SIDE_EFFECTING


def _peers():
    x, y, c = lax.axis_index("x"), lax.axis_index("y"), lax.axis_index("c")
    out = []
    for k in range(1, NDEV):
        px = 1 - x if k & 4 else x
        py = 1 - y if k & 2 else y
        pc = 1 - c if k & 1 else c
        out.append(((px, py, pc), 4 * px + 2 * py + pc))
    return 4 * x + 2 * y + c, out


def _exchange_copies(srcs, lands, send_sems, recv_sems, gather):
    my, peers = _peers()
    pairs = []
    for k, (dev, pid) in enumerate(peers):
        for a in range(len(srcs)):
            src = srcs[a] if gather else srcs[a].at[pid]
            sems = dict(send_sem=send_sems[a * (NDEV - 1) + k], recv_sem=recv_sems[a * (NDEV - 1) + k], device_id=dev,
                        device_id_type=MESH)
            pairs.append((pltpu.make_async_remote_copy(src_ref=src, dst_ref=lands[a].at[my], **sems),
                          pltpu.make_async_remote_copy(src_ref=src, dst_ref=lands[a].at[pid], **sems)))
    return pairs


def _exchange_start(name, srcs, gather):
    n = len(srcs)
    ns = n * (NDEV - 1)
    shapes = [(s.shape if gather else s.shape[1:]) for s in srcs]
    lands = [lax.empty((NDEV,) + shp, s.dtype) for shp, s in zip(shapes, srcs)]

    def body(*refs):
        src_refs, land_refs = refs[:n], refs[n:2 * n]
        send_sems, recv_sems = refs[2 * n:2 * n + ns], refs[2 * n + ns:2 * n + 2 * ns]
        token = refs[-1]
        for mine, _ in _exchange_copies(src_refs, land_refs, send_sems, recv_sems, gather):
            mine.start()
        token[...] = jnp.zeros(token.shape, token.dtype)

    hbm = lambda a: pltpu.HBM(a.shape, a.dtype)
    res = pl.pallas_call(
        body, name=name,
        out_shape=(*([pltpu.SemaphoreType.DMA(())] * (2 * ns)), *[hbm(s) for s in srcs], *[hbm(l) for l in lands],
                   jax.ShapeDtypeStruct((8, 128), F32)),
        in_specs=[HBM_SPEC] * (2 * n),
        out_specs=(*([SEM_SPEC] * (2 * ns)), *([HBM_SPEC] * (2 * n)), pl.BlockSpec(memory_space=pltpu.VMEM)),
        input_output_aliases={i: 2 * ns + i for i in range(2 * n)},
        compiler_params=pltpu.CompilerParams(has_side_effects=DATAFLOW),
    )(*[pltpu.with_memory_space_constraint(s, pltpu.HBM) for s in srcs],
      *[pltpu.with_memory_space_constraint(l, pltpu.HBM) for l in lands])
    return list(res[:ns]), list(res[ns:2 * ns]), list(res[2 * ns:2 * ns + n]), list(res[2 * ns + n:2 * ns + 2 * n]), res[-1]


def _exchange_wait(name, handle, after, gather):
    send_sems, recv_sems, srcs, lands, _ = handle
    n = len(srcs)
    ns = n * (NDEV - 1)

    def body(*refs):
        src_refs, land_refs = refs[:n], refs[n:2 * n]
        s_sems, r_sems = refs[2 * n:2 * n + ns], refs[2 * n + ns:2 * n + 2 * ns]
        for mine, theirs in _exchange_copies(src_refs, land_refs, s_sems, r_sems, gather):
            mine.wait_send()
            theirs.wait_recv()

    hbm = lambda a: pltpu.HBM(a.shape, a.dtype)
    res = pl.pallas_call(
        body, name=name,
        out_shape=(*[hbm(s) for s in srcs], *[hbm(l) for l in lands]),
        in_specs=[HBM_SPEC] * (2 * n) + [SEM_SPEC] * (2 * ns) + [pl.BlockSpec(memory_space=pl.ANY)],
        out_specs=tuple([HBM_SPEC] * (2 * n)),
        input_output_aliases={i: i for i in range(2 * n)},
        compiler_params=pltpu.CompilerParams(has_side_effects=DATAFLOW),
    )(*srcs, *lands, *send_sems, *recv_sems, after)
    return list(res[n:])


def _set_own_slot(land, own):
    my = 4 * lax.axis_index("x") + 2 * lax.axis_index("y") + lax.axis_index("c")
    return lax.dynamic_update_slice(land, own[None], (my, 0, 0))


def _own_block(blocks):
    my = 4 * lax.axis_index("x") + 2 * lax.axis_index("y") + lax.axis_index("c")
    return lax.dynamic_index_in_dim(blocks, my, axis=0, keepdims=False)


_C1 = 1.0 - ADAM_B1 ** ADAM_STEP
_C2 = 1.0 - ADAM_B2 ** ADAM_STEP


def _adamw(name, w, m, v, recv, tr):
    R, C = w.shape

    def body(w_ref, m_ref, v_ref, r_ref, g_ref, d_ref, nm_ref, nv_ref):
        g = r_ref[0].astype(F32)
        for s in range(1, NDEV):
            g = g + r_ref[s].astype(F32)
        wv = w_ref[...]
        nm = ADAM_B1 * m_ref[...] + (1.0 - ADAM_B1) * g
        nv = ADAM_B2 * v_ref[...] + (1.0 - ADAM_B2) * (g * g)
        m_hat = nm / _C1
        v_hat = nv / _C2
        g_ref[...] = g
        d_ref[...] = -ADAM_LR * (m_hat / (jnp.sqrt(v_hat) + ADAM_EPS) + ADAM_WD * wv)
        nm_ref[...] = nm
        nv_ref[...] = nv

    blk = pl.BlockSpec((tr, C), lambda i: (i, 0))
    return pl.pallas_call(
        body, grid=(R // tr,), in_specs=[blk, blk, blk, pl.BlockSpec((NDEV, tr, C), lambda i: (0, i, 0))],
        out_specs=[blk] * 4, out_shape=[jax.ShapeDtypeStruct((R, C), F32)] * 4,
        compiler_params=_cp(), name=name,
    )(w, m, v, recv)


def _row(v):
    return v.reshape(1, -1)


def _local_step(xs, tgt, Wp, rest_fn, early_fn, late_fn, rel_bias_table, g_pre_mix, b_glu, b_dw, g_conv_ln,
                b_conv_ln, b_conv_out, g_post_mix, g_pre_ffn, g_post_ffn):
    S = xs.shape[0]
    g1, g2, g3, g4 = _row(g_pre_mix), _row(g_post_mix), _row(g_pre_ffn), _row(g_post_ffn)
    bglu, bdw, gln, bln, bco = _row(b_glu), _row(b_dw), _row(g_conv_ln), _row(b_conv_ln), _row(b_conv_out)
    full = (D, F32, D, 0, False)
    fullb = (D, BF, D, 0, False)

    h1, z = _in_proj(xs, g1, Wp)

    idx = jnp.asarray(_band_index())
    tab_t = rel_bias_table.T.reshape(3, HEADS, REL_BUCKETS)
    bias_all = _bias_build(tab_t, idx)
    bias_kq = [bias_all[g].reshape(HEADS, 2 * QBLK, QBLK) for g in range(3)]
    bias_kq2 = [bias_all[3 + g].reshape(HEADS, QBLK, 2 * QBLK) for g in range(3)]
    dils = [d for _, d in GROUPS]
    qkv = [(z, _kvq_blocks(0))] + [(_to_residue(f"qkv_to_residue_g{g}", z, _kvq_blocks(g), dils[g]), (0, 1, 2)) for g in (1, 2)]
    os_, ls_ = [], []
    for g in range(3):
        o_g, l_g = _attn_fwd(f"attn_fwd_g{g}", qkv[g][0], bias_kq[g], qkv[g][1], dils[g])
        os_.append(o_g)
        ls_.append(l_g)
    o_att, lse = _attn_merge(os_, ls_, S)

    Wfi, Wfo, Wco, Wmo, Wao, wdw = rest_fn(lse)
    u1, u3 = _conv_fwd(z, bglu, wdw, bdw, gln, bln)

    def epi_mix(accs, r, c, o, p):
        ya = accs[0]
        yc = accs[1] + c[0][...]
        mg = _sig(r[0][...].astype(F32)) * ya + _sig(r[1][...].astype(F32)) * yc
        mgb = mg.astype(BF)
        m2 = jnp.dot(mgb, c[1][...], preferred_element_type=F32)
        x1 = r[2][...] + m2 * _rms_r(m2) * c[2][...]
        o[0][...] = ya.astype(BF)
        o[1][...] = yc.astype(BF)
        o[2][...] = mgb
        o[3][...] = m2.astype(BF)
        o[4][...] = x1
        o[5][...] = (x1 * _rms_r(x1) * c[3][...]).astype(BF)

    y_attn, y_conv, merged, m2, x1, h2 = _fused_mm(
        "mix_fwd", S, 512, 1, [(o_att, GW, 0, 1), (u3, D, 0, 1)],
        [(Wao, False, GW, D, 0, 1, 0, 0, False), (Wco, False, D, D, 0, 1, 0, 0, False)], [(0, 0, 0, 0, 1), (1, 1, 1, 0, 1)],
        [(512, D), (512, D)], [(z, D, 2, False), (z, D, 3, False), (xs, D, 0, False)], [bco, Wmo, g2, g3],
        [fullb, fullb, fullb, fullb, full, fullb], [], epi_mix)

    HN = FFN // 2

    def epi_ffn_in(accs, r, c, o, p):
        gt, up = accs
        o[0][...] = gt.astype(BF)
        o[1][...] = up.astype(BF)
        o[2][...] = (gt * _sig(gt) * up).astype(BF)

    gate, up, act = _fused_mm(
        "ffn_in", S, 512, 2, [(h2, D, 0, 1)],
        [(Wfi, False, D, HN, 0, 1, 0, 0, True), (Wfi, False, D, HN, 0, 1, 0, 2, True)], [(0, 0, 0, 0, 1), (0, 1, 1, 0, 1)],
        [(512, HN), (512, HN)], [], [], [(FFN, BF, HN, 0, True)] * 3, [], epi_ffn_in, n_outer=True)

    def epi_loss(accs, r, c, o, p):
        f2 = accs[0]
        g = c[0][...]
        rr = _rms_r(f2)
        err = r[0][...] + f2 * rr * g - r[1][...]
        dy = err * (1.0 / D)
        df2, dgr = _rms_bwd(f2, rr, g, dy)
        o[0][...] = dy
        o[1][...] = df2.astype(BF)
        p[0][...] += _psum8(err * err)
        p[1][...] += _psum8(dgr)

    dy, df2, loss_p, dg4 = _fused_mm(
        "ffn_out_loss", S, 512, 1, [(act, FFN, 0, 1)], [(Wfo, False, FFN, D, 0, 1, 0, 0, False)], [(0, 0, 0, 0, 1)],
        [(512, D)], [(x1, D, 0, False), (tgt, D, 0, False)], [g4], [full, fullb], [(8, D), (8, D)], epi_loss)

    def epi_swiglu(accs, r, c, o, p):
        da = accs[0]
        gt = r[0][...].astype(F32)
        sg = _sig(gt)
        o[0][...] = (da * r[1][...].astype(F32) * sg * (1.0 + gt * (1.0 - sg))).astype(BF)
        o[1][...] = (da * gt * sg).astype(BF)

    dgate, dup = _fused_mm(
        "ffn_out_bwd", S, 512, 2, [(df2, D, 0, 1)], [(Wfo, True, D, HN, 0, 1, 0, 0, True)], [(0, 0, 0, 0, 1)],
        [(512, HN)], [(gate, HN, 0, True), (up, HN, 0, True)], [], [(FFN, BF, HN, 0, True)] * 2, [], epi_swiglu, n_outer=True)
    dWfo = _mm_tn("dw_ffn_out", act, df2, HN, D, WG_TK)

    def epi_dh2(accs, r, c, o, p):
        dh2 = accs[0]
        x1v = r[1][...]
        r3 = _rms_r(x1v)
        d1, dg3r = _rms_bwd(x1v, r3, c[0][...], dh2)
        dx1 = r[0][...] + d1
        m2v = r[2][...].astype(F32)
        r2 = _rms_r(m2v)
        dm2, dg2r = _rms_bwd(m2v, r2, c[1][...], dx1)
        o[0][...] = dx1
        o[1][...] = dm2.astype(BF)
        p[0][...] += _psum8(dg3r)
        p[1][...] += _psum8(dg2r)

    dx1, dm2, dg3, dg2 = _resident_mm(
        "ffn_in_bwd", S, 512, [dgate, dup], Wfi, [(dy, D), (x1, D), (m2, D)], [g3, g2], [(D, F32), (D, BF)], [(8, D), (8, D)], epi_dh2)
    dWfi = jnp.concatenate([_mm_tn("dw_ffn_gate", h2, dgate, D, HN, WG_TK), _mm_tn("dw_ffn_up", h2, dup, D, HN, WG_TK)], axis=1)

    def epi_dmix(accs, r, c, o, p):
        dm = accs[0]
        sa = _sig(r[0][...].astype(F32))
        sc = _sig(r[1][...].astype(F32))
        o[0][...] = (dm * sa).astype(BF)
        o[1][...] = (dm * sc).astype(BF)
        o[2][:, 0:D] = (dm * r[2][...].astype(F32) * sa * (1.0 - sa)).astype(BF)
        o[2][:, D:2 * D] = (dm * r[3][...].astype(F32) * sc * (1.0 - sc)).astype(BF)

    dy_attn, dy_conv, dz_gate = _fused_mm(
        "mix_bwd", S, 512, 1, [(dm2, D, 0, 1)], [(Wmo, True, D, D, 0, 1, 0, 0, False)], [(0, 0, 0, 0, 1)], [(512, D)],
        [(z, D, 2, False), (z, D, 3, False), (y_attn, D, 0, False), (y_conv, D, 0, False)], [],
        [fullb, fullb, (2 * D, BF, 2 * D, 0, False)], [], epi_dmix)
    dWmo = _mm_tn("dw_mix_out", merged, dm2, D, D, WG_TK)

    def epi_dconv(accs, r, c, o, p):
        du3 = accs[0]
        xh, rstd = _ln_hat(r[0][...])
        gl = c[0][...]
        u2 = xh * gl + c[1][...]
        sg = _sig(u2)
        du2 = du3 * sg * (1.0 + u2 * (1.0 - sg))
        dxh = du2 * gl
        du1 = rstd * (dxh - jnp.mean(dxh, axis=-1, keepdims=True) - xh * jnp.mean(dxh * xh, axis=-1, keepdims=True))
        o[0][...] = du1
        p[0][...] += _psum8(du2 * xh)
        p[1][...] += _psum8(du2)
        p[2][...] += _psum8(du1)
        p[3][...] += _psum8(r[1][...].astype(F32))

    du1, dgln, dbln, dbdw, dbco = _fused_mm(
        "conv_out_bwd", S, 512, 1, [(dy_conv, D, 0, 1)], [(Wco, True, D, D, 0, 1, 0, 0, False)], [(0, 0, 0, 0, 1)], [(512, D)],
        [(u1, D, 0, False), (dy_conv, D, 0, False)], [gln, bln], [full], [(8, D)] * 4, epi_dconv)
    dWco = _mm_tn("dw_conv_out", u3, dy_conv, D, D, WG_TK)
    dz_glu, dwdw, dbglu = _conv_bwd(du1, z, bglu, wdw)

    head_sum = np.zeros((GW, STAT_W), np.float32)
    for h in range(HEADS):
        head_sum[h * HEAD_DIM:(h + 1) * HEAD_DIM, HEADS + h] = 1.0
    head_sum = jnp.asarray(head_sum)

    def epi_do(accs, r, c, o, p):
        do = accs[0]
        o[0][...] = do.astype(BF)
        delta = jnp.dot(do * r[0][...].astype(F32), c[0][...], preferred_element_type=F32, precision=lax.Precision.HIGHEST)
        lane = lax.broadcasted_iota(jnp.int32, delta.shape, 1)
        o[1][...] = jnp.where(lane < HEADS, r[1][...], delta)

    do, stats = _fused_mm(
        "attn_out_bwd", S, 1024, 1, [(dy_attn, D, 0, 1)], [(Wao, True, D, GW, 0, 1, 0, 0, False)], [(0, 0, 0, 0, 1)], [(1024, GW)],
        [(o_att, GW, 0, False), (lse, STAT_W, 0, False)], [head_sum], [(GW, BF, GW, 0, False), (STAT_W, F32, STAT_W, 0, False)], [], epi_do)
    dWao = _mm_tn("dw_attn_out", o_att, dy_attn, GW, D, WG_TK)

    tie = early_fn(dict(w_ffn_in=dWfi, w_ffn_out=dWfo, w_conv_out=dWco, w_mix_out=dWmo, w_attn_out=dWao, w_dw=dwdw))
    stats = stats + tie
    moved = [_to_residue_pair(f"do_stats_to_residue_g{g}", do, stats, dils[g]) for g in (1, 2)]
    dos = [do] + [m_[0] for m_ in moved]
    sts = [stats] + [m_[1] for m_ in moved]
    dqkv, dbs = [], []
    for g in range(3):
        arr, cb = qkv[g]
        dg, db = _attn_bwd(f"attn_bwd_g{g}", arr, bias_kq2[g], dos[g], sts[g], cb, dils[g])
        dqkv.append(dg if g == 0 else _from_residue(f"dqkv_from_residue_g{g}", dg, dils[g]))
        dbs.append(db.reshape(HEADS, _NB))
    dtab = _bias_grad(jnp.stack(dbs), idx)[:, :, :REL_BUCKETS].reshape(3 * HEADS, REL_BUCKETS).T

    def epi_dx(accs, r, c, o, p):
        xv = r[1][...]
        d1, dg1r = _rms_bwd(xv, _rms_r(xv), c[0][...], accs[0])
        o[0][...] = r[0][...] + d1
        p[0][...] += _psum8(dg1r)

    dWg = [_mm_tn(f"dw_in_g{g}", h1, dqkv[g], D, ATTN_W, WG_TK) for g in range(3)]
    dW_in = jnp.concatenate(
        [t[:, 2 * GW:] for t in dWg] + [t[:, :GW] for t in dWg] + [t[:, GW:2 * GW] for t in dWg]
        + [_mm_tn("dw_in_glu", h1, dz_glu, D, D, WG_TK), _mm_tn("dw_in_gate", h1, dz_gate, D, D, WG_TK)], axis=1)
    g1_late = g1 + late_fn(dW_in)
    grad_x, dg1 = _resident_mm(
        "in_proj_bwd", S, 512, [dz_glu, dz_gate, dqkv[0], dqkv[1], dqkv[2]], Wp, [(dx1, D), (xs, D)], [g1_late], [(D, F32)], [(8, D)], epi_dx)

    small = dict(rel_bias_table=dtab, g_pre_mix=dg1[0], b_glu=dbglu[0], b_dw=dbdw[0], g_conv_ln=dgln[0], b_conv_ln=dbln[0],
                 b_conv_out=dbco[0], g_post_mix=dg2[0], g_pre_ffn=dg3[0], g_post_ffn=dg4[0])
    return loss_p[0], grad_x, small


SMALL = ['rel_bias_table', 'g_pre_mix', 'b_glu', 'b_dw', 'g_conv_ln', 'b_conv_ln', 'b_conv_out', 'g_post_mix', 'g_pre_ffn',
         'g_post_ffn']
BIG = ['w_in', 'w_ffn_in', 'w_ffn_out', 'w_conv_out', 'w_mix_out', 'w_attn_out', 'w_dw']
WEIGHTS = ['rel_bias_table', 'g_pre_mix', 'w_in', 'b_glu', 'w_dw', 'b_dw', 'g_conv_ln', 'b_conv_ln', 'w_conv_out', 'b_conv_out',
           'w_attn_out', 'w_mix_out', 'g_post_mix', 'g_pre_ffn', 'w_ffn_in', 'w_ffn_out', 'g_post_ffn']
SMALL_ROWS = 16


ROW_SMALL = ['g_pre_mix', 'b_glu', 'b_dw', 'g_conv_ln', 'b_conv_ln', 'b_conv_out', 'g_post_mix', 'g_pre_ffn', 'g_post_ffn']
LOSS_ROW = 10
TAB_LANES = 128


def _small_rows(small, loss_row):
    rows = [small[n].reshape(-1, D) for n in ROW_SMALL] + [loss_row.reshape(1, D)]
    n = sum(r.shape[0] for r in rows)
    return jnp.concatenate(rows + [jnp.zeros((SMALL_ROWS - n, D), F32)], axis=0)


def _adamw_small(recv_rows, recv_tab, ws, ms, vs):
    np_ = len(SMALL)

    def body(*refs):
        rr, rt = refs[0], refs[1]
        w_refs, m_refs, v_refs = refs[2:2 + np_], refs[2 + np_:2 + 2 * np_], refs[2 + 2 * np_:2 + 3 * np_]
        loss_ref = refs[2 + 3 * np_]
        outs = refs[3 + 3 * np_:]
        rows = rr[0]
        tab = rt[0]
        for s_ in range(1, NDEV):
            rows = rows + rr[s_]
            tab = tab + rt[s_]
        loss_ref[...] = jnp.sum(rows[LOSS_ROW:LOSS_ROW + 1, :], axis=1, keepdims=True) * (0.5 / D)
        row = 0
        for p, n in enumerate(SMALL):
            if n == 'rel_bias_table':
                g = tab[:, 0:3 * HEADS]
            else:
                k = w_refs[p].shape[1] // D
                g = rows[row:row + 1, :] if k == 1 else jnp.concatenate([rows[row + t:row + t + 1, :] for t in range(k)], axis=1)
                row += k
            nm = ADAM_B1 * m_refs[p][...] + (1.0 - ADAM_B1) * g
            nv = ADAM_B2 * v_refs[p][...] + (1.0 - ADAM_B2) * (g * g)
            outs[4 * p][...] = g
            outs[4 * p + 1][...] = -ADAM_LR * ((nm / _C1) / (jnp.sqrt(nv / _C2) + ADAM_EPS) + ADAM_WD * w_refs[p][...])
            outs[4 * p + 2][...] = nm
            outs[4 * p + 3][...] = nv

    out_shape = [jax.ShapeDtypeStruct((1, 1), F32)]
    for a_ in ws:
        out_shape += [jax.ShapeDtypeStruct(a_.shape, F32)] * 4
    res = pl.pallas_call(body, out_shape=out_shape, compiler_params=_cp(), name="adamw_small")(recv_rows, recv_tab, *ws, *ms, *vs)
    return res[0], [tuple(res[1 + 4 * p:5 + 4 * p]) for p in range(np_)]


def _cols_to_blocks(a):
    R = a.shape[0]
    return a.reshape(R, NDEV, a.shape[1] // NDEV).transpose(1, 0, 2)


def _blocks_to_cols(a):
    return a.transpose(1, 0, 2).reshape(a.shape[1], NDEV * a.shape[2])


def kernel(x, rel_bias_table, g_pre_mix, w_in, b_glu, w_dw, b_dw, g_conv_ln, b_conv_ln, w_conv_out, b_conv_out, w_attn_out, w_mix_out, g_post_mix, g_pre_ffn, w_ffn_in, w_ffn_out, g_post_ffn, loss_target, m_rel_bias_table, m_g_pre_mix, m_w_in, m_b_glu, m_w_dw, m_b_dw, m_g_conv_ln, m_b_conv_ln, m_w_conv_out, m_b_conv_out, m_w_attn_out, m_w_mix_out, m_g_post_mix, m_g_pre_ffn, m_w_ffn_in, m_w_ffn_out, m_g_post_ffn, v_rel_bias_table, v_g_pre_mix, v_w_in, v_b_glu, v_w_dw, v_b_dw, v_g_conv_ln, v_b_conv_ln, v_w_conv_out, v_b_conv_out, v_w_attn_out, v_w_mix_out, v_g_post_mix, v_g_pre_ffn, v_w_ffn_in, v_w_ffn_out, v_g_post_ffn):
    w = dict(rel_bias_table=rel_bias_table, g_pre_mix=g_pre_mix, w_in=w_in, b_glu=b_glu, w_dw=w_dw, b_dw=b_dw, g_conv_ln=g_conv_ln, b_conv_ln=b_conv_ln, w_conv_out=w_conv_out, b_conv_out=b_conv_out, w_attn_out=w_attn_out, w_mix_out=w_mix_out, g_post_mix=g_post_mix, g_pre_ffn=g_pre_ffn, w_ffn_in=w_ffn_in, w_ffn_out=w_ffn_out, g_post_ffn=g_post_ffn)
    m = dict(rel_bias_table=m_rel_bias_table, g_pre_mix=m_g_pre_mix, w_in=m_w_in, b_glu=m_b_glu, w_dw=m_w_dw, b_dw=m_b_dw, g_conv_ln=m_g_conv_ln, b_conv_ln=m_b_conv_ln, w_conv_out=m_w_conv_out, b_conv_out=m_b_conv_out, w_attn_out=m_w_attn_out, w_mix_out=m_w_mix_out, g_post_mix=m_g_post_mix, g_pre_ffn=m_g_pre_ffn, w_ffn_in=m_w_ffn_in, w_ffn_out=m_w_ffn_out, g_post_ffn=m_g_post_ffn)
    v = dict(rel_bias_table=v_rel_bias_table, g_pre_mix=v_g_pre_mix, w_in=v_w_in, b_glu=v_b_glu, w_dw=v_w_dw, b_dw=v_b_dw, g_conv_ln=v_g_conv_ln, b_conv_ln=v_b_conv_ln, w_conv_out=v_w_conv_out, b_conv_out=v_b_conv_out, w_attn_out=v_w_attn_out, w_mix_out=v_w_mix_out, g_post_mix=v_g_post_mix, g_pre_ffn=v_g_pre_ffn, w_ffn_in=v_w_ffn_in, w_ffn_out=v_w_ffn_out, g_post_ffn=v_g_post_ffn)

    def shard2d(d, n):
        a = d[n][0]
        return jnp.pad(a, ((0, HALO - CONV_W), (0, 0))) if n == 'w_dw' else a

    own = {n: shard2d(w, n).astype(F32 if n == 'w_dw' else BF) for n in BIG}
    packed = ['w_ffn_out', 'w_conv_out', 'w_mix_out', 'w_attn_out']
    alone = ['w_ffn_in', 'w_dw']
    shapes = [own[n].shape for n in packed]

    def pack(arrs, lead):
        return jnp.concatenate([a.reshape(lead + (-1, D)) for a in arrs], axis=len(lead))

    def unpack(p):
        out, pos = {}, 0
        for n, shp in zip(packed, shapes):
            rows = shp[0] * shp[1] // D
            out[n] = p[:, pos:pos + rows].reshape((NDEV,) + shp)
            pos += rows
        return out

    (g_in,) = _all_gather("gather_w_in", [own['w_in']])
    rest_own = [pack([own[n] for n in packed], ())] + [own[n] for n in alone]
    g_in, rest_own = lax.optimization_barrier((g_in, rest_own))
    gather_rest = _exchange_start("gather_rest_start", rest_own, True)
    W_in = _blocks_to_cols(g_in)
    kvq = [W_in[:, t * ATTN_W + g * GW:t * ATTN_W + (g + 1) * GW] for g in range(3) for t in (1, 2, 0)]
    Wp = jnp.concatenate([W_in[:, 3 * ATTN_W:]] + kvq, axis=1)

    def rest_fn(after):
        lands = _exchange_wait("gather_rest_wait", gather_rest, after, True)
        gw = unpack(_set_own_slot(lands[0], rest_own[0]))
        for n, l, o in zip(alone, lands[1:], rest_own[1:]):
            gw[n] = _set_own_slot(l, o)
        return (_blocks_to_cols(gw['w_ffn_in']), gw['w_ffn_out'].reshape(FFN, D), gw['w_conv_out'].reshape(D, D),
                gw['w_mix_out'].reshape(D, D), _blocks_to_cols(gw['w_attn_out']), _blocks_to_cols(gw['w_dw']))

    def to_blocks(n, g):
        if n in ('w_in', 'w_ffn_in', 'w_attn_out', 'w_dw'):
            return _cols_to_blocks(g)
        return g.reshape(NDEV, g.shape[0] // NDEV, g.shape[1])

    started = {}

    def early_fn(grads):
        blocks = [pack([to_blocks(n, grads[n]) for n in packed], (NDEV,))] + [to_blocks(n, grads[n]) for n in alone]
        started['blocks'] = blocks
        started['handle'] = _exchange_start("scatter_early_start", blocks, False)
        return started['handle'][4][0:1, 0:1]

    def late_fn(dW_in):
        started['in_blocks'] = [to_blocks('w_in', dW_in)]
        started['in_handle'] = _exchange_start("scatter_w_in_start", started['in_blocks'], False)
        return started['in_handle'][4][0:1, 0:1]

    g1_tied = g_pre_mix[0] + gather_rest[4][0, 0:1]
    loss_row, grad_x, small = _local_step(
        x[0], loss_target[0], Wp, rest_fn, early_fn, late_fn, rel_bias_table, g1_tied, b_glu[0], b_dw[0], g_conv_ln[0],
        b_conv_ln[0], b_conv_out[0], g_post_mix[0], g_pre_ffn[0], g_post_ffn[0])

    lands = _exchange_wait("scatter_early_wait", started['handle'], grad_x, False)
    lands = [_set_own_slot(l, _own_block(b)) for l, b in zip(lands, started['blocks'])]
    recv = unpack(lands[0])
    recv.update(zip(alone, lands[1:]))
    (land_in,) = _exchange_wait("scatter_w_in_wait", started['in_handle'], grad_x, False)
    recv['w_in'] = _set_own_slot(land_in, _own_block(started['in_blocks'][0]))
    tiles = dict(w_in=128, w_ffn_in=256, w_ffn_out=176, w_conv_out=128, w_mix_out=128, w_attn_out=512, w_dw=HALO)
    res = {}
    for n in BIG:
        g_, d_, nm_, nv_ = _adamw("adamw_" + n, shard2d(w, n), shard2d(m, n), shard2d(v, n), recv[n], tiles[n])
        if n == 'w_dw':
            g_, d_, nm_, nv_ = (t[:CONV_W] for t in (g_, d_, nm_, nv_))
        res[n] = tuple(t[None] for t in (g_, d_, nm_, nv_))

    tab = jnp.pad(small['rel_bias_table'], ((0, 0), (0, TAB_LANES - 3 * HEADS)))
    srows, stab = _all_gather("gather_small_grads", [_small_rows(small, loss_row), tab])
    loss11, small_res = _adamw_small(srows, stab, [w[n] for n in SMALL], [m[n] for n in SMALL], [v[n] for n in SMALL])
    loss = loss11.reshape(())
    for n, r in zip(SMALL, small_res):
        res[n] = r
    return (loss, grad_x[None], *[res[n][0] for n in WEIGHTS], *[res[n][1] for n in WEIGHTS],
            *[res[n][2] for n in WEIGHTS], *[res[n][3] for n in WEIGHTS])
```
